```python
import jax, jax.numpy as jnp
from jax import lax
import numpy as np

D_MODEL = 1024
BATCH = 8
SEQ = 4096
DEPTH = 2

D_MIX = D_MODEL
LRU_WIDTH = D_MIX // 2
GMLP_WIDTH = D_MIX - LRU_WIDTH
LRU_HEADS = 8
LRU_HEAD_DIM = LRU_WIDTH // LRU_HEADS
GMLP_HEADS = 8
GMLP_HEAD_DIM = GMLP_WIDTH // GMLP_HEADS
CONV_WIDTH = 4
RG_LRU_C = 8.0
CHUNK = 128
D_FF = ((8 * D_MODEL // 3 + 127) // 128) * 128
N_MOD = 9
IN_COLS = 2 * LRU_WIDTH + 2 * GMLP_WIDTH
EPS = 1e-6

kernel_name = "hybrid_rglru_gmlp_macaron_adaln"


def rms_norm(x, g):
    x32 = x.astype(jnp.float32)
    y = x32 * lax.rsqrt(jnp.mean(x32 * x32, axis=-1, keepdims=True) + EPS)
    return (y * g.astype(jnp.float32)).astype(x.dtype)


def layer_norm(x, g):
    x32 = x.astype(jnp.float32)
    mu = jnp.mean(x32, axis=-1, keepdims=True)
    var = jnp.mean(jnp.square(x32 - mu), axis=-1, keepdims=True)
    return ((x32 - mu) * lax.rsqrt(var + EPS) * g.astype(jnp.float32)).astype(x.dtype)


def modulate(h, shift, scale):
    return h * (1.0 + scale) + shift


def swiglu(h, w_gu, w_down):
    g, u = jnp.split(h @ w_gu, 2, axis=-1)
    return (jax.nn.silu(g) * u) @ w_down


def causal_depthwise_conv(x, w, b):
    S = x.shape[1]
    xp = jnp.pad(x, ((0, 0), (CONV_WIDTH - 1, 0), (0, 0)))
    y = b
    for k in range(CONV_WIDTH):
        y = y + xp[:, k:k + S] * w[k]
    return y


def _lin_rec_combine(left, right):
    a1, b1 = left
    a2, b2 = right
    return a1 * a2, a2 * b1 + b2


def rg_lru(xb, wa, ba, wx, bx, lam):
    B, S, _ = xb.shape
    xh = xb.reshape(B, S, LRU_HEADS, LRU_HEAD_DIM)
    r = jax.nn.sigmoid(jnp.einsum('bshd,hde->bshe', xh, wa) + ba).reshape(B, S, LRU_WIDTH)
    i = jax.nn.sigmoid(jnp.einsum('bshd,hde->bshe', xh, wx) + bx).reshape(B, S, LRU_WIDTH)
    log_a = RG_LRU_C * r.astype(jnp.float32) * jax.nn.log_sigmoid(lam.astype(jnp.float32))
    a = jnp.exp(log_a)
    mult = jnp.sqrt(-jnp.expm1(2.0 * log_a))
    inp = mult * (i * xb).astype(jnp.float32)
    _, h = lax.associative_scan(_lin_rec_combine, (a, inp), axis=1)
    return h.astype(xb.dtype)


def chunked_gmlp(u, v, v_norm, spatial_w, spatial_b):
    B, S, _ = u.shape
    nc = S // CHUNK
    u = jax.nn.gelu(u)
    v = jax.nn.gelu(v)
    vh = v.reshape(B, nc, CHUNK, GMLP_HEADS, GMLP_HEAD_DIM)
    vh = layer_norm(vh, v_norm.reshape(GMLP_HEADS, GMLP_HEAD_DIM))
    mask = jnp.tril(jnp.ones((CHUNK, CHUNK), dtype=spatial_w.dtype))
    ws = spatial_w * mask
    z = jnp.einsum('hts,bnshd->bnthd', ws, vh) + spatial_b.T[:, :, None]
    return u * z.reshape(B, S, GMLP_WIDTH)


def _fwd_setup_inputs(seed: int = 0) -> dict:
    key = jax.random.key(seed)
    ks = jax.random.split(key, 32)
    f32 = jnp.float32
    L, D = DEPTH, D_MODEL

    def nrm(k, shape, scale):
        return jax.random.normal(k, shape, f32) * scale

    def gain(k, shape):
        return 1.0 + 0.05 * jax.random.normal(k, shape, f32)

    a0 = jax.random.uniform(ks[12], (L, LRU_WIDTH), f32, minval=0.9, maxval=0.999)
    lru_lambda = jnp.log(a0) - jnp.log1p(-a0)
    return {
        'x': jax.random.normal(ks[0], (BATCH, SEQ, D), f32),
        'c': jax.random.normal(ks[1], (BATCH, D), f32),
        'w_ada': nrm(ks[2], (L, D, N_MOD * D), 0.5 * D ** -0.5),
        'b_ada': nrm(ks[3], (L, N_MOD * D), 0.02),
        'ffn1_norm': gain(ks[4], (L, D)),
        'ffn1_w_gu': nrm(ks[5], (L, D, 2 * D_FF), D ** -0.5),
        'ffn1_w_down': nrm(ks[6], (L, D_FF, D), D_FF ** -0.5),
        'mix_norm': gain(ks[7], (L, D)),
        'w_in': nrm(ks[8], (L, D, IN_COLS), D ** -0.5),
        'conv_w': nrm(ks[9], (L, CONV_WIDTH, LRU_WIDTH), CONV_WIDTH ** -0.5),
        'conv_b': nrm(ks[10], (L, LRU_WIDTH), 0.02),
        'gate_a_w': nrm(ks[11], (L, LRU_HEADS, LRU_HEAD_DIM, LRU_HEAD_DIM), LRU_HEAD_DIM ** -0.5),
        'gate_a_b': nrm(ks[13], (L, LRU_HEADS, LRU_HEAD_DIM), 0.02),
        'gate_x_w': nrm(ks[14], (L, LRU_HEADS, LRU_HEAD_DIM, LRU_HEAD_DIM), LRU_HEAD_DIM ** -0.5),
        'gate_x_b': nrm(ks[15], (L, LRU_HEADS, LRU_HEAD_DIM), 0.02),
        'lru_lambda': lru_lambda,
        'v_norm': gain(ks[16], (L, GMLP_WIDTH)),
        'spatial_w': nrm(ks[17], (L, GMLP_HEADS, CHUNK, CHUNK), CHUNK ** -0.5),
        'spatial_b': nrm(ks[18], (L, GMLP_HEADS, CHUNK), 0.02),
        'lru_out_norm': gain(ks[19], (L, LRU_WIDTH)),
        'gmlp_out_norm': gain(ks[20], (L, GMLP_WIDTH)),
        'w_out': nrm(ks[21], (L, D_MIX, D), D_MIX ** -0.5),
        'ffn2_norm': gain(ks[22], (L, D)),
        'ffn2_w_gu': nrm(ks[23], (L, D, 2 * D_FF), D ** -0.5),
        'ffn2_w_down': nrm(ks[24], (L, D_FF, D), D_FF ** -0.5),
        'final_norm': gain(ks[25], (D,)),
    }


def _fwd_reference(x, c, w_ada, b_ada, ffn1_norm, ffn1_w_gu, ffn1_w_down, mix_norm, w_in,
              conv_w, conv_b, gate_a_w, gate_a_b, gate_x_w, gate_x_b, lru_lambda,
              v_norm, spatial_w, spatial_b, lru_out_norm, gmlp_out_norm, w_out,
              ffn2_norm, ffn2_w_gu, ffn2_w_down, final_norm):
    B = x.shape[0]
    sc = jax.nn.silu(c)
    for l in range(DEPTH):
        mod = (sc @ w_ada[l] + b_ada[l]).reshape(B, N_MOD, 1, D_MODEL)

        h = modulate(rms_norm(x, ffn1_norm[l]), mod[:, 0], mod[:, 1])
        x = x + 0.5 * mod[:, 2] * swiglu(h, ffn1_w_gu[l], ffn1_w_down[l])

        h = modulate(rms_norm(x, mix_norm[l]), mod[:, 3], mod[:, 4])
        proj = h @ w_in[l]
        x_lru, g_lru, u, v = jnp.split(
            proj, [LRU_WIDTH, 2 * LRU_WIDTH, 2 * LRU_WIDTH + GMLP_WIDTH], axis=-1)
        x_lru = causal_depthwise_conv(x_lru, conv_w[l], conv_b[l])
        y_lru = rg_lru(x_lru, gate_a_w[l], gate_a_b[l], gate_x_w[l], gate_x_b[l],
                       lru_lambda[l]) * jax.nn.gelu(g_lru)
        y_gmlp = chunked_gmlp(u, v, v_norm[l], spatial_w[l], spatial_b[l])
        y = jnp.concatenate([rms_norm(y_lru, lru_out_norm[l]),
                             rms_norm(y_gmlp, gmlp_out_norm[l])], axis=-1)
        x = x + mod[:, 5] * (y @ w_out[l])

        h = modulate(rms_norm(x, ffn2_norm[l]), mod[:, 6], mod[:, 7])
        x = x + 0.5 * mod[:, 8] * swiglu(h, ffn2_w_gu[l], ffn2_w_down[l])
    return rms_norm(x, final_norm)


import jax as _jax
import jax.numpy as _jnp

TWIN_FORMAT = 'train_step'
FWD_PARAMS = ['x', 'c', 'w_ada', 'b_ada', 'ffn1_norm', 'ffn1_w_gu', 'ffn1_w_down', 'mix_norm', 'w_in', 'conv_w', 'conv_b', 'gate_a_w', 'gate_a_b', 'gate_x_w', 'gate_x_b', 'lru_lambda', 'v_norm', 'spatial_w', 'spatial_b', 'lru_out_norm', 'gmlp_out_norm', 'w_out', 'ffn2_norm', 'ffn2_w_gu', 'ffn2_w_down', 'final_norm']
TWIN_WEIGHTS = ['w_ada', 'b_ada', 'ffn1_norm', 'ffn1_w_gu', 'ffn1_w_down', 'mix_norm', 'w_in', 'conv_w', 'conv_b', 'gate_a_w', 'gate_a_b', 'gate_x_w', 'gate_x_b', 'lru_lambda', 'v_norm', 'spatial_w', 'spatial_b', 'lru_out_norm', 'gmlp_out_norm', 'w_out', 'ffn2_norm', 'ffn2_w_gu', 'ffn2_w_down', 'final_norm']
TWIN_DIFF_INPUT = 'x'
TWIN_INPUTS = ['x', 'c', 'w_ada', 'b_ada', 'ffn1_norm', 'ffn1_w_gu', 'ffn1_w_down', 'mix_norm', 'w_in', 'conv_w', 'conv_b', 'gate_a_w', 'gate_a_b', 'gate_x_w', 'gate_x_b', 'lru_lambda', 'v_norm', 'spatial_w', 'spatial_b', 'lru_out_norm', 'gmlp_out_norm', 'w_out', 'ffn2_norm', 'ffn2_w_gu', 'ffn2_w_down', 'final_norm', 'loss_target', 'm_w_ada', 'm_b_ada', 'm_ffn1_norm', 'm_ffn1_w_gu', 'm_ffn1_w_down', 'm_mix_norm', 'm_w_in', 'm_conv_w', 'm_conv_b', 'm_gate_a_w', 'm_gate_a_b', 'm_gate_x_w', 'm_gate_x_b', 'm_lru_lambda', 'm_v_norm', 'm_spatial_w', 'm_spatial_b', 'm_lru_out_norm', 'm_gmlp_out_norm', 'm_w_out', 'm_ffn2_norm', 'm_ffn2_w_gu', 'm_ffn2_w_down', 'm_final_norm', 'v_w_ada', 'v_b_ada', 'v_ffn1_norm', 'v_ffn1_w_gu', 'v_ffn1_w_down', 'v_mix_norm', 'v_w_in', 'v_conv_w', 'v_conv_b', 'v_gate_a_w', 'v_gate_a_b', 'v_gate_x_w', 'v_gate_x_b', 'v_lru_lambda', 'v_v_norm', 'v_spatial_w', 'v_spatial_b', 'v_lru_out_norm', 'v_gmlp_out_norm', 'v_w_out', 'v_ffn2_norm', 'v_ffn2_w_gu', 'v_ffn2_w_down', 'v_final_norm']
TWIN_OUTPUTS = ['loss', 'grad_x', 'grad_w_ada', 'grad_b_ada', 'grad_ffn1_norm', 'grad_ffn1_w_gu', 'grad_ffn1_w_down', 'grad_mix_norm', 'grad_w_in', 'grad_conv_w', 'grad_conv_b', 'grad_gate_a_w', 'grad_gate_a_b', 'grad_gate_x_w', 'grad_gate_x_b', 'grad_lru_lambda', 'grad_v_norm', 'grad_spatial_w', 'grad_spatial_b', 'grad_lru_out_norm', 'grad_gmlp_out_norm', 'grad_w_out', 'grad_ffn2_norm', 'grad_ffn2_w_gu', 'grad_ffn2_w_down', 'grad_final_norm', 'delta_w_ada', 'delta_b_ada', 'delta_ffn1_norm', 'delta_ffn1_w_gu', 'delta_ffn1_w_down', 'delta_mix_norm', 'delta_w_in', 'delta_conv_w', 'delta_conv_b', 'delta_gate_a_w', 'delta_gate_a_b', 'delta_gate_x_w', 'delta_gate_x_b', 'delta_lru_lambda', 'delta_v_norm', 'delta_spatial_w', 'delta_spatial_b', 'delta_lru_out_norm', 'delta_gmlp_out_norm', 'delta_w_out', 'delta_ffn2_norm', 'delta_ffn2_w_gu', 'delta_ffn2_w_down', 'delta_final_norm', 'new_m_w_ada', 'new_m_b_ada', 'new_m_ffn1_norm', 'new_m_ffn1_w_gu', 'new_m_ffn1_w_down', 'new_m_mix_norm', 'new_m_w_in', 'new_m_conv_w', 'new_m_conv_b', 'new_m_gate_a_w', 'new_m_gate_a_b', 'new_m_gate_x_w', 'new_m_gate_x_b', 'new_m_lru_lambda', 'new_m_v_norm', 'new_m_spatial_w', 'new_m_spatial_b', 'new_m_lru_out_norm', 'new_m_gmlp_out_norm', 'new_m_w_out', 'new_m_ffn2_norm', 'new_m_ffn2_w_gu', 'new_m_ffn2_w_down', 'new_m_final_norm', 'new_v_w_ada', 'new_v_b_ada', 'new_v_ffn1_norm', 'new_v_ffn1_w_gu', 'new_v_ffn1_w_down', 'new_v_mix_norm', 'new_v_w_in', 'new_v_conv_w', 'new_v_conv_b', 'new_v_gate_a_w', 'new_v_gate_a_b', 'new_v_gate_x_w', 'new_v_gate_x_b', 'new_v_lru_lambda', 'new_v_v_norm', 'new_v_spatial_w', 'new_v_spatial_b', 'new_v_lru_out_norm', 'new_v_gmlp_out_norm', 'new_v_w_out', 'new_v_ffn2_norm', 'new_v_ffn2_w_gu', 'new_v_ffn2_w_down', 'new_v_final_norm']
TWIN_LEAF_KINDS = {'loss': 'loss', 'grad_x': 'grad_x', 'grad_w_ada': 'grad_w', 'grad_b_ada': 'grad_w', 'grad_ffn1_norm': 'grad_w', 'grad_ffn1_w_gu': 'grad_w', 'grad_ffn1_w_down': 'grad_w', 'grad_mix_norm': 'grad_w', 'grad_w_in': 'grad_w', 'grad_conv_w': 'grad_w', 'grad_conv_b': 'grad_w', 'grad_gate_a_w': 'grad_w', 'grad_gate_a_b': 'grad_w', 'grad_gate_x_w': 'grad_w', 'grad_gate_x_b': 'grad_w', 'grad_lru_lambda': 'grad_w', 'grad_v_norm': 'grad_w', 'grad_spatial_w': 'grad_w', 'grad_spatial_b': 'grad_w', 'grad_lru_out_norm': 'grad_w', 'grad_gmlp_out_norm': 'grad_w', 'grad_w_out': 'grad_w', 'grad_ffn2_norm': 'grad_w', 'grad_ffn2_w_gu': 'grad_w', 'grad_ffn2_w_down': 'grad_w', 'grad_final_norm': 'grad_w', 'delta_w_ada': 'delta_w', 'delta_b_ada': 'delta_w', 'delta_ffn1_norm': 'delta_w', 'delta_ffn1_w_gu': 'delta_w', 'delta_ffn1_w_down': 'delta_w', 'delta_mix_norm': 'delta_w', 'delta_w_in': 'delta_w', 'delta_conv_w': 'delta_w', 'delta_conv_b': 'delta_w', 'delta_gate_a_w': 'delta_w', 'delta_gate_a_b': 'delta_w', 'delta_gate_x_w': 'delta_w', 'delta_gate_x_b': 'delta_w', 'delta_lru_lambda': 'delta_w', 'delta_v_norm': 'delta_w', 'delta_spatial_w': 'delta_w', 'delta_spatial_b': 'delta_w', 'delta_lru_out_norm': 'delta_w', 'delta_gmlp_out_norm': 'delta_w', 'delta_w_out': 'delta_w', 'delta_ffn2_norm': 'delta_w', 'delta_ffn2_w_gu': 'delta_w', 'delta_ffn2_w_down': 'delta_w', 'delta_final_norm': 'delta_w', 'new_m_w_ada': 'new_m', 'new_m_b_ada': 'new_m', 'new_m_ffn1_norm': 'new_m', 'new_m_ffn1_w_gu': 'new_m', 'new_m_ffn1_w_down': 'new_m', 'new_m_mix_norm': 'new_m', 'new_m_w_in': 'new_m', 'new_m_conv_w': 'new_m', 'new_m_conv_b': 'new_m', 'new_m_gate_a_w': 'new_m', 'new_m_gate_a_b': 'new_m', 'new_m_gate_x_w': 'new_m', 'new_m_gate_x_b': 'new_m', 'new_m_lru_lambda': 'new_m', 'new_m_v_norm': 'new_m', 'new_m_spatial_w': 'new_m', 'new_m_spatial_b': 'new_m', 'new_m_lru_out_norm': 'new_m', 'new_m_gmlp_out_norm': 'new_m', 'new_m_w_out': 'new_m', 'new_m_ffn2_norm': 'new_m', 'new_m_ffn2_w_gu': 'new_m', 'new_m_ffn2_w_down': 'new_m', 'new_m_final_norm': 'new_m', 'new_v_w_ada': 'new_v', 'new_v_b_ada': 'new_v', 'new_v_ffn1_norm': 'new_v', 'new_v_ffn1_w_gu': 'new_v', 'new_v_ffn1_w_down': 'new_v', 'new_v_mix_norm': 'new_v', 'new_v_w_in': 'new_v', 'new_v_conv_w': 'new_v', 'new_v_conv_b': 'new_v', 'new_v_gate_a_w': 'new_v', 'new_v_gate_a_b': 'new_v', 'new_v_gate_x_w': 'new_v', 'new_v_gate_x_b': 'new_v', 'new_v_lru_lambda': 'new_v', 'new_v_v_norm': 'new_v', 'new_v_spatial_w': 'new_v', 'new_v_spatial_b': 'new_v', 'new_v_lru_out_norm': 'new_v', 'new_v_gmlp_out_norm': 'new_v', 'new_v_w_out': 'new_v', 'new_v_ffn2_norm': 'new_v', 'new_v_ffn2_w_gu': 'new_v', 'new_v_ffn2_w_down': 'new_v', 'new_v_final_norm': 'new_v'}


def _forward(args):
    return _fwd_reference(*[args[k] for k in FWD_PARAMS])


def _output_shape():
    out = _jax.eval_shape(lambda: _forward(_fwd_setup_inputs(0)))
    return out.shape, out.dtype

N_MICROBATCH = 1
ADAM_LR = 0.001
ADAM_B1 = 0.9
ADAM_B2 = 0.999
ADAM_EPS = 1e-08
ADAM_WD = 0.01
ADAM_STEP = 10
PER_EXAMPLE_BATCH_AXIS = {'x': 0, 'c': 0, 'loss_target': 0}
SHARED_INPUTS = []
_WEIGHT_DTYPES = {'w_ada': _jnp.float32, 'b_ada': _jnp.float32, 'ffn1_norm': _jnp.float32, 'ffn1_w_gu': _jnp.float32, 'ffn1_w_down': _jnp.float32, 'mix_norm': _jnp.float32, 'w_in': _jnp.float32, 'conv_w': _jnp.float32, 'conv_b': _jnp.float32, 'gate_a_w': _jnp.float32, 'gate_a_b': _jnp.float32, 'gate_x_w': _jnp.float32, 'gate_x_b': _jnp.float32, 'lru_lambda': _jnp.float32, 'v_norm': _jnp.float32, 'spatial_w': _jnp.float32, 'spatial_b': _jnp.float32, 'lru_out_norm': _jnp.float32, 'gmlp_out_norm': _jnp.float32, 'w_out': _jnp.float32, 'ffn2_norm': _jnp.float32, 'ffn2_w_gu': _jnp.float32, 'ffn2_w_down': _jnp.float32, 'final_norm': _jnp.float32}
MOMENT_SCALE = {'w_ada': 7.237176e-02, 'b_ada': 1.260740e-01, 'ffn1_norm': 2.940314e-02, 'ffn1_w_gu': 1.258630e-02, 'ffn1_w_down': 2.052429e-02, 'mix_norm': 7.444366e-02, 'w_in': 5.955195e-02, 'conv_w': 5.320920e-02, 'conv_b': 1.382578e-01, 'gate_a_w': 1.034238e-02, 'gate_a_b': 1.209593e-02, 'gate_x_w': 1.820188e-02, 'gate_x_b': 2.004225e-02, 'lru_lambda': 2.880003e-02, 'v_norm': 5.537371e-02, 'spatial_w': 3.772018e-02, 'spatial_b': 1.024767e-01, 'lru_out_norm': 5.423249e-02, 'gmlp_out_norm': 5.485144e-02, 'w_out': 5.432328e-02, 'ffn2_norm': 2.764900e-02, 'ffn2_w_gu': 1.161859e-02, 'ffn2_w_down': 1.907892e-02, 'final_norm': 3.210780e+01}


def _to_microbatches(a, axis):
    t = _jnp.moveaxis(a, axis, 0)
    t = t.reshape((N_MICROBATCH, t.shape[0] // N_MICROBATCH) + t.shape[1:])
    return _jnp.moveaxis(t, 1, axis + 1)


def setup_inputs(seed: int = 0) -> dict:
    inp = _fwd_setup_inputs(seed)
    key = _jax.random.fold_in(_jax.random.key(seed), 7919)
    shape, _ = _output_shape()
    out = dict(inp)
    out["loss_target"] = _jax.random.normal(_jax.random.fold_in(key, 0), shape, _jnp.float32)
    for i, name in enumerate(TWIN_WEIGHTS):
        w = inp[name].astype(_jnp.float32)
        if MOMENT_SCALE is None:
            s = _jnp.sqrt(_jnp.mean(_jnp.square(w)) + 1e-30)
        else:
            s = MOMENT_SCALE[name]
        km, kv = _jax.random.split(_jax.random.fold_in(key, i + 1))
        out[name] = w
        out["m_" + name] = s * _jax.random.normal(km, w.shape, _jnp.float32)
        out["v_" + name] = (s * s) * _jax.random.uniform(kv, w.shape, _jnp.float32, 0.5, 1.5)
    if N_MICROBATCH > 1:
        for name, axis in PER_EXAMPLE_BATCH_AXIS.items():
            out[name] = _to_microbatches(out[name], axis)
    return {'x': out['x'], 'c': out['c'], 'w_ada': out['w_ada'], 'b_ada': out['b_ada'], 'ffn1_norm': out['ffn1_norm'], 'ffn1_w_gu': out['ffn1_w_gu'], 'ffn1_w_down': out['ffn1_w_down'], 'mix_norm': out['mix_norm'], 'w_in': out['w_in'], 'conv_w': out['conv_w'], 'conv_b': out['conv_b'], 'gate_a_w': out['gate_a_w'], 'gate_a_b': out['gate_a_b'], 'gate_x_w': out['gate_x_w'], 'gate_x_b': out['gate_x_b'], 'lru_lambda': out['lru_lambda'], 'v_norm': out['v_norm'], 'spatial_w': out['spatial_w'], 'spatial_b': out['spatial_b'], 'lru_out_norm': out['lru_out_norm'], 'gmlp_out_norm': out['gmlp_out_norm'], 'w_out': out['w_out'], 'ffn2_norm': out['ffn2_norm'], 'ffn2_w_gu': out['ffn2_w_gu'], 'ffn2_w_down': out['ffn2_w_down'], 'final_norm': out['final_norm'], 'loss_target': out['loss_target'], 'm_w_ada': out['m_w_ada'], 'm_b_ada': out['m_b_ada'], 'm_ffn1_norm': out['m_ffn1_norm'], 'm_ffn1_w_gu': out['m_ffn1_w_gu'], 'm_ffn1_w_down': out['m_ffn1_w_down'], 'm_mix_norm': out['m_mix_norm'], 'm_w_in': out['m_w_in'], 'm_conv_w': out['m_conv_w'], 'm_conv_b': out['m_conv_b'], 'm_gate_a_w': out['m_gate_a_w'], 'm_gate_a_b': out['m_gate_a_b'], 'm_gate_x_w': out['m_gate_x_w'], 'm_gate_x_b': out['m_gate_x_b'], 'm_lru_lambda': out['m_lru_lambda'], 'm_v_norm': out['m_v_norm'], 'm_spatial_w': out['m_spatial_w'], 'm_spatial_b': out['m_spatial_b'], 'm_lru_out_norm': out['m_lru_out_norm'], 'm_gmlp_out_norm': out['m_gmlp_out_norm'], 'm_w_out': out['m_w_out'], 'm_ffn2_norm': out['m_ffn2_norm'], 'm_ffn2_w_gu': out['m_ffn2_w_gu'], 'm_ffn2_w_down': out['m_ffn2_w_down'], 'm_final_norm': out['m_final_norm'], 'v_w_ada': out['v_w_ada'], 'v_b_ada': out['v_b_ada'], 'v_ffn1_norm': out['v_ffn1_norm'], 'v_ffn1_w_gu': out['v_ffn1_w_gu'], 'v_ffn1_w_down': out['v_ffn1_w_down'], 'v_mix_norm': out['v_mix_norm'], 'v_w_in': out['v_w_in'], 'v_conv_w': out['v_conv_w'], 'v_conv_b': out['v_conv_b'], 'v_gate_a_w': out['v_gate_a_w'], 'v_gate_a_b': out['v_gate_a_b'], 'v_gate_x_w': out['v_gate_x_w'], 'v_gate_x_b': out['v_gate_x_b'], 'v_lru_lambda': out['v_lru_lambda'], 'v_v_norm': out['v_v_norm'], 'v_spatial_w': out['v_spatial_w'], 'v_spatial_b': out['v_spatial_b'], 'v_lru_out_norm': out['v_lru_out_norm'], 'v_gmlp_out_norm': out['v_gmlp_out_norm'], 'v_w_out': out['v_w_out'], 'v_ffn2_norm': out['v_ffn2_norm'], 'v_ffn2_w_gu': out['v_ffn2_w_gu'], 'v_ffn2_w_down': out['v_ffn2_w_down'], 'v_final_norm': out['v_final_norm']}


def _loss(weights, diff, rest, loss_target):
    with _jax.named_scope("forward"):
        args = {**rest, TWIN_DIFF_INPUT: diff, **{k: w.astype(_WEIGHT_DTYPES[k]) for k, w in weights.items()}}
        y = _forward(args)
    with _jax.named_scope("loss_head"):
        err = _jnp.square(y.astype(_jnp.float32) - loss_target)
        return 0.5 * _jnp.sum(_jnp.mean(err, axis=-1)) if err.ndim else 0.5 * err


def _adamw(w, g, m, v):
    m = ADAM_B1 * m + (1.0 - ADAM_B1) * g
    v = ADAM_B2 * v + (1.0 - ADAM_B2) * _jnp.square(g)
    m_hat = m / (1.0 - ADAM_B1 ** ADAM_STEP)
    v_hat = v / (1.0 - ADAM_B2 ** ADAM_STEP)
    delta = -ADAM_LR * (m_hat / (_jnp.sqrt(v_hat) + ADAM_EPS) + ADAM_WD * w)
    return delta, m, v


def reference(x, c, w_ada, b_ada, ffn1_norm, ffn1_w_gu, ffn1_w_down, mix_norm, w_in, conv_w, conv_b, gate_a_w, gate_a_b, gate_x_w, gate_x_b, lru_lambda, v_norm, spatial_w, spatial_b, lru_out_norm, gmlp_out_norm, w_out, ffn2_norm, ffn2_w_gu, ffn2_w_down, final_norm, loss_target, m_w_ada, m_b_ada, m_ffn1_norm, m_ffn1_w_gu, m_ffn1_w_down, m_mix_norm, m_w_in, m_conv_w, m_conv_b, m_gate_a_w, m_gate_a_b, m_gate_x_w, m_gate_x_b, m_lru_lambda, m_v_norm, m_spatial_w, m_spatial_b, m_lru_out_norm, m_gmlp_out_norm, m_w_out, m_ffn2_norm, m_ffn2_w_gu, m_ffn2_w_down, m_final_norm, v_w_ada, v_b_ada, v_ffn1_norm, v_ffn1_w_gu, v_ffn1_w_down, v_mix_norm, v_w_in, v_conv_w, v_conv_b, v_gate_a_w, v_gate_a_b, v_gate_x_w, v_gate_x_b, v_lru_lambda, v_v_norm, v_spatial_w, v_spatial_b, v_lru_out_norm, v_gmlp_out_norm, v_w_out, v_ffn2_norm, v_ffn2_w_gu, v_ffn2_w_down, v_final_norm):
    given = dict(x=x, c=c, w_ada=w_ada, b_ada=b_ada, ffn1_norm=ffn1_norm, ffn1_w_gu=ffn1_w_gu, ffn1_w_down=ffn1_w_down, mix_norm=mix_norm, w_in=w_in, conv_w=conv_w, conv_b=conv_b, gate_a_w=gate_a_w, gate_a_b=gate_a_b, gate_x_w=gate_x_w, gate_x_b=gate_x_b, lru_lambda=lru_lambda, v_norm=v_norm, spatial_w=spatial_w, spatial_b=spatial_b, lru_out_norm=lru_out_norm, gmlp_out_norm=gmlp_out_norm, w_out=w_out, ffn2_norm=ffn2_norm, ffn2_w_gu=ffn2_w_gu, ffn2_w_down=ffn2_w_down, final_norm=final_norm, loss_target=loss_target, m_w_ada=m_w_ada, m_b_ada=m_b_ada, m_ffn1_norm=m_ffn1_norm, m_ffn1_w_gu=m_ffn1_w_gu, m_ffn1_w_down=m_ffn1_w_down, m_mix_norm=m_mix_norm, m_w_in=m_w_in, m_conv_w=m_conv_w, m_conv_b=m_conv_b, m_gate_a_w=m_gate_a_w, m_gate_a_b=m_gate_a_b, m_gate_x_w=m_gate_x_w, m_gate_x_b=m_gate_x_b, m_lru_lambda=m_lru_lambda, m_v_norm=m_v_norm, m_spatial_w=m_spatial_w, m_spatial_b=m_spatial_b, m_lru_out_norm=m_lru_out_norm, m_gmlp_out_norm=m_gmlp_out_norm, m_w_out=m_w_out, m_ffn2_norm=m_ffn2_norm, m_ffn2_w_gu=m_ffn2_w_gu, m_ffn2_w_down=m_ffn2_w_down, m_final_norm=m_final_norm, v_w_ada=v_w_ada, v_b_ada=v_b_ada, v_ffn1_norm=v_ffn1_norm, v_ffn1_w_gu=v_ffn1_w_gu, v_ffn1_w_down=v_ffn1_w_down, v_mix_norm=v_mix_norm, v_w_in=v_w_in, v_conv_w=v_conv_w, v_conv_b=v_conv_b, v_gate_a_w=v_gate_a_w, v_gate_a_b=v_gate_a_b, v_gate_x_w=v_gate_x_w, v_gate_x_b=v_gate_x_b, v_lru_lambda=v_lru_lambda, v_v_norm=v_v_norm, v_spatial_w=v_spatial_w, v_spatial_b=v_spatial_b, v_lru_out_norm=v_lru_out_norm, v_gmlp_out_norm=v_gmlp_out_norm, v_w_out=v_w_out, v_ffn2_norm=v_ffn2_norm, v_ffn2_w_gu=v_ffn2_w_gu, v_ffn2_w_down=v_ffn2_w_down, v_final_norm=v_final_norm)
    weights = {n: given[n] for n in TWIN_WEIGHTS}
    shared = {n: given[n] for n in SHARED_INPUTS}
    per_example = {n: given[n] for n in ['x', 'c']}
    grad_fn = _jax.value_and_grad(_loss, argnums=(0, 1))

    def one_microbatch(ex, loss_target):
        ex = dict(ex)
        diff = ex.pop(TWIN_DIFF_INPUT)
        return grad_fn(weights, diff, {**shared, **ex}, loss_target)

    if N_MICROBATCH == 1:
        loss, (grad_w, grad_x) = one_microbatch(per_example, given["loss_target"])
    else:
        def body(carry, xs):
            loss_sum, grad_sum = carry
            l_k, (gw_k, gx_k) = one_microbatch(xs[0], xs[1])
            with _jax.named_scope("update"):
                return (loss_sum + l_k, _jax.tree.map(_jnp.add, grad_sum, gw_k)), gx_k

        init = (_jnp.zeros((), _jnp.float32), _jax.tree.map(_jnp.zeros_like, weights))
        (loss, grad_w), grad_x = _jax.lax.scan(body, init, (per_example, given["loss_target"]))
    with _jax.named_scope("update"):
        delta_w, new_m, new_v = {}, {}, {}
        for n in TWIN_WEIGHTS:
            delta_w[n], new_m[n], new_v[n] = _adamw(weights[n], grad_w[n], given["m_" + n], given["v_" + n])
    return (loss, grad_x, *[grad_w[n] for n in TWIN_WEIGHTS], *[delta_w[n] for n in TWIN_WEIGHTS],
            *[new_m[n] for n in TWIN_WEIGHTS], *[new_v[n] for n in TWIN_WEIGHTS])
```

```python
import jax
import jax.numpy as jnp
from jax import lax
from jax.experimental import pallas as pl
from jax.experimental.pallas import tpu as pltpu

F32 = jnp.float32
BF16 = jnp.bfloat16

NDEV = 8
DEPTH = 2
D = 1024
DFF = 2816
FC = 2 * DFF // NDEV
NCHUNK = DFF // FC
DR = DFF // NDEV
LW = 512
GW = 512
HD = 64
HEADS = 8
CHUNK = 128
PC = 2 * (LW + GW) // NDEV
OR = D // NDEV
NMOD = 9
AC = NMOD * D // NDEV
LC = 128
EPS = 1e-6
RG_LRU_C = 8.0
CONV_WIDTH = 4

ADAM_LR = 0.001
ADAM_B1 = 0.9
ADAM_B2 = 0.999
ADAM_EPS = 1e-08
ADAM_WD = 0.01
ADAM_STEP = 10

VMEM_LIMIT_BYTES = 60 * 1024 * 1024
MESH = pl.DeviceIdType.MESH
ANY = pl.BlockSpec(memory_space=pl.ANY)


def _cparams(*sem):
    return pltpu.CompilerParams(dimension_semantics=tuple(sem) if sem else None,
                                vmem_limit_bytes=VMEM_LIMIT_BYTES)


def _dot(a, b):
    return jnp.dot(a, b, preferred_element_type=F32)


def _dot_nt(a, b):
    return lax.dot_general(a, b, (((1,), (1,)), ((), ())), preferred_element_type=F32)


def _dot_tn(a, b):
    return lax.dot_general(a, b, (((0,), (0,)), ((), ())), preferred_element_type=F32)


def _split(a):
    hi = a.astype(BF16)
    lo = (a - hi.astype(F32)).astype(BF16)
    return hi, lo


def _dot3(a, b):
    ah, al = _split(a)
    bh, bl = _split(b)
    return _dot(ah, bh) + (_dot(ah, bl) + _dot(al, bh))


def _csum(a):
    return jnp.sum(a, axis=0, keepdims=True)


def _rmean(a):
    return jnp.mean(a, axis=-1, keepdims=True)


def _sigmoid(a):
    return 1.0 / (1.0 + jnp.exp(-a))


_GELU_K = 0.7978845608028654
_GELU_C = 0.044715


def _gelu(a):
    return 0.5 * a * (1.0 + jnp.tanh(_GELU_K * (a + _GELU_C * a * a * a)))


def _gelu_grad(a):
    t = jnp.tanh(_GELU_K * (a + _GELU_C * a * a * a))
    return 0.5 * (1.0 + t) + 0.5 * a * (1.0 - t * t) * (_GELU_K * (1.0 + 3.0 * _GELU_C * a * a))


def _norm_mod(x, gain, scale, shift):
    rstd = lax.rsqrt(_rmean(x * x) + EPS)
    return (x * rstd * gain) * (1.0 + scale) + shift


def _norm_mod_bwd(dh, x, gain, scale):
    rstd = lax.rsqrt(_rmean(x * x) + EPS)
    xhat = x * rstd
    dshift = _csum(dh)
    dscale = _csum(dh * (xhat * gain))
    dhn = dh * (1.0 + scale)
    dgain = _csum(dhn * xhat)
    dxhat = dhn * gain
    dx = rstd * (dxhat - xhat * _rmean(dxhat * xhat))
    return dx, dshift, dscale, dgain


def _rms(x, gain):
    rstd = lax.rsqrt(_rmean(x * x) + EPS)
    return x * rstd * gain


def _rms_bwd(dy, x, gain):
    rstd = lax.rsqrt(_rmean(x * x) + EPS)
    xhat = x * rstd
    dgain = _csum(dy * xhat)
    dxhat = dy * gain
    return rstd * (dxhat - xhat * _rmean(dxhat * xhat)), dgain


def _seg_mean(a, pavg):
    hi, lo = _split(a)
    return _dot(hi, pavg) + _dot(lo, pavg)


def _fetch_blocks(src_hbm, slot, dst_vmem, sems, rows):
    copies = []
    for k in range(NDEV):
        dst = dst_vmem.at[k] if rows is None else dst_vmem.at[pl.ds(k * rows, rows)]
        copies.append(pltpu.make_async_copy(src_hbm.at[k, slot], dst, sems.at[k]))
    for cp in copies:
        cp.start()
    for cp in copies:
        cp.wait()


FFN_TS = 256


def ffn_fwd(x, vec, wgu_g, wdown_g, slot, tag):
    S = x.shape[0]
    ts = min(FFN_TS, S)

    def body(x_ref, vec_ref, wgu_hbm, wd_hbm, xo_ref, h_ref, gu_ref, f_ref, wgu_v, wd_v, sems):
        @pl.when(pl.program_id(0) == 0)
        def _():
            _fetch_blocks(wgu_hbm, slot, wgu_v, sems.at[0], None)
            _fetch_blocks(wd_hbm, slot, wd_v, sems.at[1], DR)

        xv = x_ref[...]
        h = _norm_mod(xv, vec_ref[3:4, :], vec_ref[1:2, :], vec_ref[0:1, :]).astype(BF16)
        h_ref[...] = h
        acc = jnp.zeros((ts, D), F32)
        for j in range(NCHUNK):
            g = _dot(h, wgu_v[j])
            u = _dot(h, wgu_v[NCHUNK + j])
            gu_ref[j] = g.astype(BF16)
            gu_ref[NCHUNK + j] = u.astype(BF16)
            a = (g * _sigmoid(g) * u).astype(BF16)
            acc = acc + _dot(a, wd_v[pl.ds(j * FC, FC), :])
        f_ref[...] = acc.astype(BF16)
        xo_ref[...] = xv + (0.5 * vec_ref[2:3, :]) * acc

    return pl.pallas_call(
        body, name=f"ffn_fwd_{tag}",
        grid=(S // ts,),
        in_specs=[pl.BlockSpec((ts, D), lambda i: (i, 0)),
                  pl.BlockSpec((8, D), lambda i: (0, 0)), ANY, ANY],
        out_specs=[pl.BlockSpec((ts, D), lambda i: (i, 0)),
                   pl.BlockSpec((ts, D), lambda i: (i, 0)),
                   pl.BlockSpec((NDEV, ts, FC), lambda i: (0, i, 0)),
                   pl.BlockSpec((ts, D), lambda i: (i, 0))],
        out_shape=[jax.ShapeDtypeStruct((S, D), F32), jax.ShapeDtypeStruct((S, D), BF16),
                   jax.ShapeDtypeStruct((NDEV, S, FC), BF16), jax.ShapeDtypeStruct((S, D), BF16)],
        scratch_shapes=[pltpu.VMEM((NDEV, D, FC), BF16), pltpu.VMEM((DFF, D), BF16),
                        pltpu.SemaphoreType.DMA((2, NDEV))],
        compiler_params=_cparams("arbitrary"),
    )(x, vec, wgu_g, wdown_g)


def ffn_bwd(dxo, x, gu, f, vec, wgu_g, wdown_g, slot, tag):
    S = x.shape[0]
    ts = min(FFN_TS, S)

    def body(dxo_ref, x_ref, gu_ref, f_ref, vec_ref, wgu_hbm, wd_hbm,
             dx_ref, dgu_ref, a_ref, df_ref, acc_ref, wgu_v, wd_v, sems):
        @pl.when(pl.program_id(0) == 0)
        def _():
            _fetch_blocks(wgu_hbm, slot, wgu_v, sems.at[0], None)
            _fetch_blocks(wd_hbm, slot, wd_v, sems.at[1], DR)
            acc_ref[...] = jnp.zeros_like(acc_ref)

        dxo_v = dxo_ref[...]
        dgate = 0.5 * _csum(dxo_v * f_ref[...].astype(F32))
        df = ((0.5 * vec_ref[2:3, :]) * dxo_v).astype(BF16)
        df_ref[...] = df
        dh = jnp.zeros((ts, D), F32)
        for j in range(NCHUNK):
            da = _dot_nt(df, wd_v[pl.ds(j * FC, FC), :])
            g = gu_ref[j].astype(F32)
            u = gu_ref[NCHUNK + j].astype(F32)
            sg = _sigmoid(g)
            si = g * sg
            a_ref[j] = (si * u).astype(BF16)
            dg = (da * u * (sg * (1.0 + g * (1.0 - sg)))).astype(BF16)
            du = (da * si).astype(BF16)
            dgu_ref[j] = dg
            dgu_ref[NCHUNK + j] = du
            dh = dh + _dot_nt(dg, wgu_v[j]) + _dot_nt(du, wgu_v[NCHUNK + j])
        dx, dshift, dscale, dgain = _norm_mod_bwd(dh, x_ref[...], vec_ref[3:4, :], vec_ref[1:2, :])
        dx_ref[...] = dx + dxo_v
        acc_ref[0:1, :] += dshift
        acc_ref[1:2, :] += dscale
        acc_ref[2:3, :] += dgate
        acc_ref[3:4, :] += dgain

    row = pl.BlockSpec((ts, D), lambda i: (i, 0))
    return pl.pallas_call(
        body, name=f"ffn_bwd_{tag}",
        grid=(S // ts,),
        in_specs=[row, row, pl.BlockSpec((NDEV, ts, FC), lambda i: (0, i, 0)), row,
                  pl.BlockSpec((8, D), lambda i: (0, 0)), ANY, ANY],
        out_specs=[row, pl.BlockSpec((NDEV, ts, FC), lambda i: (0, i, 0)),
                   pl.BlockSpec((NCHUNK, ts, FC), lambda i: (0, i, 0)), row,
                   pl.BlockSpec((8, D), lambda i: (0, 0))],
        out_shape=[jax.ShapeDtypeStruct((S, D), F32), jax.ShapeDtypeStruct((NDEV, S, FC), BF16),
                   jax.ShapeDtypeStruct((NCHUNK, S, FC), BF16), jax.ShapeDtypeStruct((S, D), BF16),
                   jax.ShapeDtypeStruct((8, D), F32)],
        scratch_shapes=[pltpu.VMEM((NDEV, D, FC), BF16), pltpu.VMEM((DFF, D), BF16),
                        pltpu.SemaphoreType.DMA((2, NDEV))],
        compiler_params=_cparams("arbitrary"),
    )(dxo, x, gu, f, vec, wgu_g, wdown_g)


def tn_matmul(a, b, slot, nslots, prev, name, ntotal=None, koff=0, split=1):
    na, S, M = a.shape
    nb, _, N = b.shape
    ncall = max(na, nb)
    ntotal = ncall if ntotal is None else ntotal
    ts = min(512, S)
    nsteps = S // ts
    mp = M // split

    def body(*refs):
        a_ref, b_ref = refs[0], refs[1]
        o_ref, acc = refs[-2], refs[-1]
        s = pl.program_id(1)

        @pl.when(s == 0)
        def _():
            acc[...] = jnp.zeros_like(acc)

        acc[...] += _dot_tn(a_ref[...], b_ref[...])

        @pl.when(s == nsteps - 1)
        def _():
            for p in range(split):
                o_ref[p] = acc[p * mp:(p + 1) * mp, :].astype(BF16)

    in_specs = [pl.BlockSpec((None, ts, M), (lambda k, s: (k, s, 0)) if na > 1 else (lambda k, s: (0, s, 0))),
                pl.BlockSpec((None, ts, N), (lambda k, s: (k, s, 0)) if nb > 1 else (lambda k, s: (0, s, 0)))]
    args = [a, b]
    aliases = {}
    if prev is not None:
        in_specs.append(ANY)
        args.append(prev)
        aliases = {2: 0}
    return pl.pallas_call(
        body, name=name,
        grid=(ncall, nsteps),
        in_specs=in_specs,
        out_specs=pl.BlockSpec((split, None, mp, N), lambda k, s: (koff + k, slot, 0, 0)),
        out_shape=jax.ShapeDtypeStruct((ntotal * split, nslots, mp, N), BF16),
        scratch_shapes=[pltpu.VMEM((M, N), F32)],
        input_output_aliases=aliases,
        compiler_params=_cparams("arbitrary", "arbitrary"),
    )(*args)


MIX_TS = 256


def mix_in_fwd(x, vec, win_g, layer, tag):
    S = x.shape[0]
    ts = min(MIX_TS, S)

    def body(x_ref, vec_ref, win_ref, hm_ref, proj_ref):
        h = _norm_mod(x_ref[...], vec_ref[3:4, :], vec_ref[1:2, :], vec_ref[0:1, :]).astype(BF16)
        hm_ref[...] = h
        for k in range(NDEV):
            proj_ref[k] = _dot(h, win_ref[k])

    return pl.pallas_call(
        body, name=f"mix_in_fwd_{tag}",
        grid=(S // ts,),
        in_specs=[pl.BlockSpec((ts, D), lambda i: (i, 0)), pl.BlockSpec((8, D), lambda i: (0, 0)),
                  pl.BlockSpec((NDEV, None, D, PC), lambda i: (0, layer, 0, 0))],
        out_specs=[pl.BlockSpec((ts, D), lambda i: (i, 0)),
                   pl.BlockSpec((NDEV, ts, PC), lambda i: (0, i, 0))],
        out_shape=[jax.ShapeDtypeStruct((S, D), BF16), jax.ShapeDtypeStruct((NDEV, S, PC), F32)],
        compiler_params=_cparams("arbitrary"),
    )(x, vec, win_g)


def mix_in_bwd(dxl, dgl, duv, x, dxo, vec, win_g, layer, tag):
    S = x.shape[0]
    ts = min(MIX_TS, S)

    def body(dxl_ref, dgl_ref, duv_ref, x_ref, dxo_ref, vec_ref, win_ref, dx_ref, acc_ref):
        @pl.when(pl.program_id(0) == 0)
        def _():
            acc_ref[...] = jnp.zeros_like(acc_ref)

        parts = [dxl_ref[0], dxl_ref[1], dgl_ref[0], dgl_ref[1],
                 duv_ref[0], duv_ref[1], duv_ref[2], duv_ref[3]]
        dh = jnp.zeros((ts, D), F32)
        for k in range(NDEV):
            dh = dh + _dot_nt(parts[k], win_ref[k])
        dx, dshift, dscale, dgain = _norm_mod_bwd(dh, x_ref[...], vec_ref[3:4, :], vec_ref[1:2, :])
        dx_ref[...] = dx + dxo_ref[...]
        acc_ref[0:1, :] += dshift
        acc_ref[1:2, :] += dscale
        acc_ref[3:4, :] += dgain

    row = pl.BlockSpec((ts, D), lambda i: (i, 0))
    return pl.pallas_call(
        body, name=f"mix_in_bwd_{tag}",
        grid=(S // ts,),
        in_specs=[pl.BlockSpec((2, ts, PC), lambda i: (0, i, 0)), pl.BlockSpec((2, ts, PC), lambda i: (0, i, 0)),
                  pl.BlockSpec((4, ts, PC), lambda i: (0, i, 0)), row, row,
                  pl.BlockSpec((8, D), lambda i: (0, 0)),
                  pl.BlockSpec((NDEV, None, D, PC), lambda i: (0, layer, 0, 0))],
        out_specs=[row, pl.BlockSpec((8, D), lambda i: (0, 0))],
        out_shape=[jax.ShapeDtypeStruct((S, D), F32), jax.ShapeDtypeStruct((8, D), F32)],
        compiler_params=_cparams("arbitrary"),
    )(dxl, dgl, duv, x, dxo, vec, win_g)


def _shift_down(z, k, row):
    return jnp.where(row >= k, pltpu.roll(z, k, 0), 0.0)


def _shift_up(z, k, row, n):
    return jnp.where(row < n - k, pltpu.roll(z, n - k, 0), 0.0)


def _lru_gates(xc, lp_ref, wa_ref, wx_ref):
    xcb = xc.astype(BF16)
    ra = _sigmoid(_dot(xcb, wa_ref[...]) + lp_ref[5:6, :])
    ix = _sigmoid(_dot(xcb, wx_ref[...]) + lp_ref[6:7, :])
    lam = lp_ref[7:8, :]
    ls = jnp.minimum(lam, 0.0) - jnp.log(1.0 + jnp.exp(-jnp.abs(lam)))
    log_a = (RG_LRU_C * ls) * ra
    a = jnp.exp(log_a)
    mult = jnp.sqrt(-jnp.tanh(log_a) * (a * a + 1.0))
    return ra, ix, ls, a, mult


def _conv(x, lp_ref, row):
    return (lp_ref[4:5, :] + lp_ref[3:4, :] * x + lp_ref[2:3, :] * _shift_down(x, 1, row)
            + lp_ref[1:2, :] * _shift_down(x, 2, row) + lp_ref[0:1, :] * _shift_down(x, 3, row))


def lru_fwd(proj, lp, wa_t, wx_t, tag):
    S = proj.shape[1]
    nblk = S // 8

    def body(x_ref, g_ref, lp_ref, wa_ref, wx_ref, y_ref, xc_ref, h_ref, a_s, b_s):
        x = x_ref[...]
        row = lax.broadcasted_iota(jnp.int32, x.shape, 0)
        xc = _conv(x, lp_ref, row)
        xc_ref[...] = xc
        ra, ix, ls, a, mult = _lru_gates(xc, lp_ref, wa_ref, wx_ref)
        a_s[...] = a
        b_s[...] = mult * (ix * xc)
        rowb = lax.broadcasted_iota(jnp.int32, (8, LC), 0)

        def step(i, carry):
            r0 = pl.multiple_of(i * 8, 8)
            A = a_s[pl.ds(r0, 8), :]
            B = b_s[pl.ds(r0, 8), :]
            for d in (1, 2, 4):
                m = rowb >= d
                As = jnp.where(m, pltpu.roll(A, d, 0), 1.0)
                Bs = jnp.where(m, pltpu.roll(B, d, 0), 0.0)
                B = A * Bs + B
                A = A * As
            H = B + A * carry
            h_ref[pl.ds(r0, 8), :] = H
            return H[7:8, :]

        lax.fori_loop(0, nblk, step, jnp.zeros((1, LC), F32))
        y_ref[...] = h_ref[...] * _gelu(g_ref[...])

    col = pl.BlockSpec((S, LC), lambda c: (0, c))
    return pl.pallas_call(
        body, name=f"lru_fwd_{tag}",
        grid=(LW // LC,),
        in_specs=[pl.BlockSpec((None, S, LC), lambda c: (c // 2, 0, c % 2)),
                  pl.BlockSpec((None, S, LC), lambda c: (2 + c // 2, 0, c % 2)),
                  pl.BlockSpec((8, LC), lambda c: (0, c)),
                  pl.BlockSpec((None, LC, LC), lambda c: (c, 0, 0)),
                  pl.BlockSpec((None, LC, LC), lambda c: (c, 0, 0))],
        out_specs=[col, col, col],
        out_shape=[jax.ShapeDtypeStruct((S, LW), F32)] * 3,
        scratch_shapes=[pltpu.VMEM((S, LC), F32), pltpu.VMEM((S, LC), F32)],
        compiler_params=_cparams("arbitrary"),
    )(proj, proj, lp, wa_t, wx_t)


def lru_bwd(dy, proj, xc_all, hst, lp, wa_t, wx_t, tag):
    S = proj.shape[1]
    nblk = S // 8

    def body(dy_ref, x_ref, g_ref, xc_ref, h_ref, lp_ref, wa_ref, wx_ref,
             dx_ref, dg_ref, dlp_ref, dwa_ref, dwx_ref, c_s, l_s):
        xc = xc_ref[...]
        row = lax.broadcasted_iota(jnp.int32, xc.shape, 0)
        ra, ix, ls, a, mult = _lru_gates(xc, lp_ref, wa_ref, wx_ref)
        g = g_ref[...]
        dyv = dy_ref[...]
        h = h_ref[...]
        dg_ref[...] = (dyv * h * _gelu_grad(g)).astype(BF16)
        c_s[...] = _shift_up(a, 1, row, S)
        l_s[...] = dyv * _gelu(g)
        rowb = lax.broadcasted_iota(jnp.int32, (8, LC), 0)

        def step(i, carry):
            r0 = pl.multiple_of((nblk - 1 - i) * 8, 8)
            C = c_s[pl.ds(r0, 8), :]
            L = l_s[pl.ds(r0, 8), :]
            for d in (1, 2, 4):
                m = rowb < 8 - d
                Cs = jnp.where(m, pltpu.roll(C, 8 - d, 0), 1.0)
                Ls = jnp.where(m, pltpu.roll(L, 8 - d, 0), 0.0)
                L = C * Ls + L
                C = C * Cs
            L = L + C * carry
            l_s[pl.ds(r0, 8), :] = L
            return L[0:1, :]

        lax.fori_loop(0, nblk, step, jnp.zeros((1, LC), F32))
        db = l_s[...]
        da = db * _shift_down(h, 1, row)
        ixc = ix * xc
        dmult = db * ixc
        dix = db * (mult * xc)
        dxc = db * (mult * ix)
        dlog_a = da * a - dmult * (a * a) / mult
        dra = dlog_a * (RG_LRU_C * ls)
        dls = _csum(dlog_a * ra) * RG_LRU_C
        lam = lp_ref[7:8, :]
        dlam = dls * _sigmoid(-lam)
        dpa = dra * ra * (1.0 - ra)
        dpx = dix * ix * (1.0 - ix)
        dpab = dpa.astype(BF16)
        dpxb = dpx.astype(BF16)
        xcb = xc.astype(BF16)
        dwa_ref[...] = _dot_tn(xcb, dpab)
        dwx_ref[...] = _dot_tn(xcb, dpxb)
        dxc = dxc + _dot_nt(dpab, wa_ref[...]) + _dot_nt(dpxb, wx_ref[...])
        x = x_ref[...]
        dlp_ref[0:1, :] = _csum(dxc * _shift_down(x, 3, row))
        dlp_ref[1:2, :] = _csum(dxc * _shift_down(x, 2, row))
        dlp_ref[2:3, :] = _csum(dxc * _shift_down(x, 1, row))
        dlp_ref[3:4, :] = _csum(dxc * x)
        dlp_ref[4:5, :] = _csum(dxc)
        dlp_ref[5:6, :] = _csum(dpa)
        dlp_ref[6:7, :] = _csum(dpx)
        dlp_ref[7:8, :] = dlam
        dx = (lp_ref[3:4, :] * dxc + lp_ref[2:3, :] * _shift_up(dxc, 1, row, S)
              + lp_ref[1:2, :] * _shift_up(dxc, 2, row, S) + lp_ref[0:1, :] * _shift_up(dxc, 3, row, S))
        dx_ref[...] = dx.astype(BF16)

    col = pl.BlockSpec((S, LC), lambda c: (0, c))
    pcol = pl.BlockSpec((None, S, LC), lambda c: (c // 2, 0, c % 2))
    return pl.pallas_call(
        body, name=f"lru_bwd_{tag}",
        grid=(LW // LC,),
        in_specs=[col, pcol, pl.BlockSpec((None, S, LC), lambda c: (2 + c // 2, 0, c % 2)), col, col,
                  pl.BlockSpec((8, LC), lambda c: (0, c)),
                  pl.BlockSpec((None, LC, LC), lambda c: (c, 0, 0)),
                  pl.BlockSpec((None, LC, LC), lambda c: (c, 0, 0))],
        out_specs=[pcol, pcol, pl.BlockSpec((8, LC), lambda c: (0, c)),
                   pl.BlockSpec((None, LC, LC), lambda c: (c, 0, 0)),
                   pl.BlockSpec((None, LC, LC), lambda c: (c, 0, 0))],
        out_shape=[jax.ShapeDtypeStruct((2, S, PC), BF16), jax.ShapeDtypeStruct((2, S, PC), BF16),
                   jax.ShapeDtypeStruct((8, LW), F32),
                   jax.ShapeDtypeStruct((LW // LC, LC, LC), F32), jax.ShapeDtypeStruct((LW // LC, LC, LC), F32)],
        scratch_shapes=[pltpu.VMEM((S, LC), F32), pltpu.VMEM((S, LC), F32)],
        compiler_params=_cparams("arbitrary"),
    )(dy, proj, proj, xc_all, hst, lp, wa_t, wx_t)


def _head_stack(zc, lane_head):
    return jnp.concatenate([jnp.where(lane_head == hh, zc, 0.0) for hh in range(HEADS)], axis=0).astype(BF16)


def _gmlp_fwd_parts(u, v, gp_ref, wcat_ref, bz_ref, pavg_ref, ts):
    ug = _gelu(u)
    vg = _gelu(v)
    pavg = pavg_ref[...]
    vc = vg - _seg_mean(vg, pavg)
    rs = lax.rsqrt(_seg_mean(vc * vc, pavg) + EPS)
    vhat = vc * rs
    vh = vhat * gp_ref[0:1, :]
    lane_head = lax.broadcasted_iota(jnp.int32, (CHUNK, GW), 1) // HD
    zs = []
    for n in range(ts // CHUNK):
        stack = _head_stack(vh[n * CHUNK:(n + 1) * CHUNK, :], lane_head)
        zs.append(_dot(wcat_ref[...], stack) + bz_ref[...])
    z = jnp.concatenate(zs, axis=0) if len(zs) > 1 else zs[0]
    return ug, rs, vhat, vh, z


def mix_out_fwd(proj, ylru, x, vec, gp, wcat, bz, pavg, wout_g, layer, tag):
    S = x.shape[0]
    ts = min(MIX_TS, S)

    def body(u_ref, v_ref, yl_ref, x_ref, vec_ref, gp_ref, wcat_ref, bz_ref, pavg_ref, wout_ref,
             xo_ref, y_ref, fo_ref):
        u = jnp.concatenate([u_ref[0], u_ref[1]], axis=1)
        v = jnp.concatenate([v_ref[0], v_ref[1]], axis=1)
        ug, _, _, _, z = _gmlp_fwd_parts(u, v, gp_ref, wcat_ref, bz_ref, pavg_ref, ts)
        n1 = _rms(yl_ref[...], gp_ref[1:2, :])
        n2 = _rms(ug * z, gp_ref[2:3, :])
        y = jnp.concatenate([n1, n2], axis=1).astype(BF16)
        y_ref[...] = y
        fo = jnp.zeros((ts, D), F32)
        for k in range(NDEV):
            fo = fo + _dot(y[:, k * OR:(k + 1) * OR], wout_ref[k])
        fo_ref[...] = fo.astype(BF16)
        xo_ref[...] = x_ref[...] + vec_ref[2:3, :] * fo

    row = pl.BlockSpec((ts, D), lambda i: (i, 0))
    full = lambda shp: pl.BlockSpec(shp, lambda i: tuple(0 for _ in shp))
    return pl.pallas_call(
        body, name=f"mix_out_fwd_{tag}",
        grid=(S // ts,),
        in_specs=[pl.BlockSpec((2, ts, PC), lambda i: (2, i, 0)), pl.BlockSpec((2, ts, PC), lambda i: (3, i, 0)),
                  pl.BlockSpec((ts, LW), lambda i: (i, 0)), row, full((8, D)), full((8, GW)),
                  full((CHUNK, HEADS * CHUNK)), full((CHUNK, GW)), full((GW, GW)),
                  pl.BlockSpec((NDEV, None, OR, D), lambda i: (0, layer, 0, 0))],
        out_specs=[row, row, row],
        out_shape=[jax.ShapeDtypeStruct((S, D), F32), jax.ShapeDtypeStruct((S, D), BF16),
                   jax.ShapeDtypeStruct((S, D), BF16)],
        compiler_params=_cparams("arbitrary"),
    )(proj, proj, ylru, x, vec, gp, wcat, bz, pavg, wout_g)


def mix_out_bwd(dxo, proj, ylru, fo, vec, gp, wcat, wcat_t, bz, pavg, wout_g, layer, tag):
    S = dxo.shape[0]
    ts = min(MIX_TS, S)

    def body(dxo_ref, u_ref, v_ref, yl_ref, fo_ref, vec_ref, gp_ref, wcat_ref, wcatt_ref, bz_ref, pavg_ref,
             wout_ref, dyo_ref, dyl_ref, duv_ref, acc_ref, dgp_ref, dwm_ref, dbz_ref):
        @pl.when(pl.program_id(0) == 0)
        def _():
            acc_ref[...] = jnp.zeros_like(acc_ref)
            dgp_ref[...] = jnp.zeros_like(dgp_ref)
            dwm_ref[...] = jnp.zeros_like(dwm_ref)
            dbz_ref[...] = jnp.zeros_like(dbz_ref)

        dxo_v = dxo_ref[...]
        acc_ref[2:3, :] += _csum(dxo_v * fo_ref[...].astype(F32))
        dyo = (vec_ref[2:3, :] * dxo_v).astype(BF16)
        dyo_ref[...] = dyo
        dn = [_dot_nt(dyo, wout_ref[k]) for k in range(NDEV)]
        dn1 = jnp.concatenate(dn[:NDEV // 2], axis=1)
        dn2 = jnp.concatenate(dn[NDEV // 2:], axis=1)
        dyl, dg1 = _rms_bwd(dn1, yl_ref[...], gp_ref[1:2, :])
        dyl_ref[...] = dyl
        u = jnp.concatenate([u_ref[0], u_ref[1]], axis=1)
        v = jnp.concatenate([v_ref[0], v_ref[1]], axis=1)
        ug, rs, vhat, vh, z = _gmlp_fwd_parts(u, v, gp_ref, wcat_ref, bz_ref, pavg_ref, ts)
        dyg, dg2 = _rms_bwd(dn2, ug * z, gp_ref[2:3, :])
        du = (dyg * z) * _gelu_grad(u)
        dz = dyg * ug
        lane_head = lax.broadcasted_iota(jnp.int32, (CHUNK, GW), 1) // HD
        vhb = vh.astype(BF16)
        dvhs = []
        dbz = jnp.zeros((CHUNK, GW), F32)
        dwm = jnp.zeros((HEADS * CHUNK, CHUNK), F32)
        for n in range(ts // CHUNK):
            dzc = dz[n * CHUNK:(n + 1) * CHUNK, :]
            dbz = dbz + dzc
            stack = _head_stack(dzc, lane_head)
            dwm = dwm + _dot_nt(stack, vhb[n * CHUNK:(n + 1) * CHUNK, :])
            dvhs.append(_dot(wcatt_ref[...], stack))
        dbz_ref[...] += dbz
        dwm_ref[...] += dwm
        dvh = jnp.concatenate(dvhs, axis=0) if len(dvhs) > 1 else dvhs[0]
        pavg = pavg_ref[...]
        dvn = _csum(dvh * vhat)
        dvhat = dvh * gp_ref[0:1, :]
        dvg = rs * (dvhat - _seg_mean(dvhat, pavg) - vhat * _seg_mean(dvhat * vhat, pavg))
        dv = dvg * _gelu_grad(v)
        duv_ref[0] = du[:, :PC].astype(BF16)
        duv_ref[1] = du[:, PC:].astype(BF16)
        duv_ref[2] = dv[:, :PC].astype(BF16)
        duv_ref[3] = dv[:, PC:].astype(BF16)
        dgp_ref[0:1, :] += dvn
        dgp_ref[1:2, :] += dg1
        dgp_ref[2:3, :] += dg2

    row = pl.BlockSpec((ts, D), lambda i: (i, 0))
    full = lambda shp: pl.BlockSpec(shp, lambda i: tuple(0 for _ in shp))
    return pl.pallas_call(
        body, name=f"mix_out_bwd_{tag}",
        grid=(S // ts,),
        in_specs=[row, pl.BlockSpec((2, ts, PC), lambda i: (2, i, 0)), pl.BlockSpec((2, ts, PC), lambda i: (3, i, 0)),
                  pl.BlockSpec((ts, LW), lambda i: (i, 0)), row, full((8, D)), full((8, GW)),
                  full((CHUNK, HEADS * CHUNK)), full((CHUNK, HEADS * CHUNK)), full((CHUNK, GW)), full((GW, GW)),
                  pl.BlockSpec((NDEV, None, OR, D), lambda i: (0, layer, 0, 0))],
        out_specs=[row, pl.BlockSpec((ts, LW), lambda i: (i, 0)), pl.BlockSpec((4, ts, PC), lambda i: (0, i, 0)),
                   full((8, D)), full((8, GW)), full((HEADS * CHUNK, CHUNK)), full((CHUNK, GW))],
        out_shape=[jax.ShapeDtypeStruct((S, D), BF16), jax.ShapeDtypeStruct((S, LW), F32),
                   jax.ShapeDtypeStruct((4, S, PC), BF16), jax.ShapeDtypeStruct((8, D), F32),
                   jax.ShapeDtypeStruct((8, GW), F32), jax.ShapeDtypeStruct((HEADS * CHUNK, CHUNK), F32),
                   jax.ShapeDtypeStruct((CHUNK, GW), F32)],
        compiler_params=_cparams("arbitrary"),
    )(dxo, proj, proj, ylru, fo, vec, gp, wcat, wcat_t, bz, pavg, wout_g)


def final_loss(x, target, gain):
    S = x.shape[0]
    ts = min(512, S)

    def body(x_ref, t_ref, g_ref, loss_ref, dx_ref, dg_ref):
        @pl.when(pl.program_id(0) == 0)
        def _():
            loss_ref[...] = jnp.zeros_like(loss_ref)
            dg_ref[...] = jnp.zeros_like(dg_ref)

        xv = x_ref[...]
        gain_v = g_ref[0:1, :]
        rstd = lax.rsqrt(_rmean(xv * xv) + EPS)
        xhat = xv * rstd
        err = xhat * gain_v - t_ref[...]
        loss_ref[...] += 0.5 * _csum(_rmean(err * err))
        dy = err * (1.0 / D)
        dg_ref[0:1, :] += _csum(dy * xhat)
        dxhat = dy * gain_v
        dx_ref[...] = rstd * (dxhat - xhat * _rmean(dxhat * xhat))

    row = pl.BlockSpec((ts, D), lambda i: (i, 0))
    return pl.pallas_call(
        body, name="final_loss",
        grid=(S // ts,),
        in_specs=[row, row, pl.BlockSpec((8, D), lambda i: (0, 0))],
        out_specs=[pl.BlockSpec((8, 128), lambda i: (0, 0)), row, pl.BlockSpec((8, D), lambda i: (0, 0))],
        out_shape=[jax.ShapeDtypeStruct((8, 128), F32), jax.ShapeDtypeStruct((S, D), F32),
                   jax.ShapeDtypeStruct((8, D), F32)],
        compiler_params=_cparams("arbitrary"),
    )(x, target, gain)


def _vec(mod_l, j, gain):
    return jnp.concatenate([mod_l[3 * j:3 * j + 3], gain[None, :], jnp.zeros((4, D), F32)], axis=0)


def _block_diag_tiles(w):
    w4 = w.reshape(LW // LC, 2, HD, HD)
    eye2 = jnp.eye(2, dtype=w.dtype)
    return (w4[:, :, :, None, :] * eye2[None, :, None, :, None]).reshape(LW // LC, LC, LC).astype(BF16)


def _block_diag_extract(dw):
    d5 = dw.reshape(LW // LC, 2, HD, 2, HD)
    return jnp.einsum('cihkj,ik->cihj', d5, jnp.eye(2, dtype=dw.dtype)).reshape(HEADS, HD, HD)


def _layer_params(l, p, conv_w_full):
    lp = jnp.concatenate([conv_w_full[l], p['conv_b'][l][None], p['gate_a_b'][l].reshape(1, LW),
                          p['gate_x_b'][l].reshape(1, LW), p['lru_lambda'][l][None]], axis=0)
    gp = jnp.concatenate([p['v_norm'][l][None], p['lru_out_norm'][l][None], p['gmlp_out_norm'][l][None],
                          jnp.zeros((5, GW), F32)], axis=0)
    ws = p['spatial_w'][l] * jnp.tril(jnp.ones((CHUNK, CHUNK), F32))
    wcat = ws.transpose(1, 0, 2).reshape(CHUNK, HEADS * CHUNK).astype(BF16)
    wcat_t = ws.transpose(2, 0, 1).reshape(CHUNK, HEADS * CHUNK).astype(BF16)
    bz = jnp.repeat(p['spatial_b'][l].T, HD, axis=1)
    return dict(lp=lp, gp=gp, wcat=wcat, wcat_t=wcat_t, bz=bz,
                wa_t=_block_diag_tiles(p['gate_a_w'][l]), wx_t=_block_diag_tiles(p['gate_x_w'][l]))


def _pavg():
    return jnp.kron(jnp.eye(HEADS, dtype=F32), jnp.full((HD, HD), 1.0 / HD, F32)).astype(BF16)


def local_fwd_bwd(x, target, mod, p, g, conv_w_full):
    pavg = _pavg()
    saved = []
    h = x
    for l in range(DEPTH):
        q = _layer_params(l, p, conv_w_full)
        v1 = _vec(mod[l], 0, p['ffn1_norm'][l])
        vm = _vec(mod[l], 1, p['mix_norm'][l])
        v2 = _vec(mod[l], 2, p['ffn2_norm'][l])
        x0 = h
        x1, h1, gu1, f1 = ffn_fwd(x0, v1, g['gu'], g['down'], l, f"a{l}")
        hm, proj = mix_in_fwd(x1, vm, g['w_in'], l, f"{l}")
        ylru, xc, hst = lru_fwd(proj, q['lp'], q['wa_t'], q['wx_t'], f"{l}")
        x2, y, fo = mix_out_fwd(proj, ylru, x1, vm, q['gp'], q['wcat'], q['bz'], pavg, g['w_out'], l, f"{l}")
        x3, h2, gu2, f2 = ffn_fwd(x2, v2, g['gu'], g['down'], DEPTH + l, f"b{l}")
        saved.append(dict(q=q, v1=v1, vm=vm, v2=v2, x0=x0, x1=x1, x2=x2, h1=h1, gu1=gu1, f1=f1, hm=hm, proj=proj,
                          ylru=ylru, xc=xc, hst=hst, y=y, fo=fo, h2=h2, gu2=gu2, f2=f2))
        h = x3
    fin = jnp.concatenate([p['final_norm'][None], jnp.zeros((7, D), F32)], axis=0)
    loss8, dx, dfin = final_loss(h, target, fin)
    loss = loss8[0, 0]

    big = dict(gu=None, down=None, w_in=None, w_out=None)
    small = {k: [None] * DEPTH for k in ('ffn1_norm', 'mix_norm', 'ffn2_norm', 'conv_w', 'conv_b', 'gate_a_w',
                                         'gate_a_b', 'gate_x_w', 'gate_x_b', 'lru_lambda', 'v_norm', 'spatial_w',
                                         'spatial_b', 'lru_out_norm', 'gmlp_out_norm')}
    dmod = [None] * DEPTH
    tril = jnp.tril(jnp.ones((CHUNK, CHUNK), F32))
    for l in reversed(range(DEPTH)):
        sv = saved[l]
        q = sv['q']
        dx2, dgu, a, df, acc2 = ffn_bwd(dx, sv['x2'], sv['gu2'], sv['f2'], sv['v2'],
                                        g['gu'], g['down'], DEPTH + l, f"b{l}")
        big['gu'] = tn_matmul(sv['h2'][None], dgu, DEPTH + l, 2 * DEPTH, big['gu'], f"dw_gu_b{l}")
        big['down'] = tn_matmul(a, df[None], DEPTH + l, 2 * DEPTH, big['down'], f"dw_down_b{l}", split=2)
        dyo, dylru, duv, accmo, dgp, dwm, dbz = mix_out_bwd(dx2, sv['proj'], sv['ylru'], sv['fo'], sv['vm'], q['gp'],
                                                             q['wcat'], q['wcat_t'], q['bz'], pavg, g['w_out'], l, f"{l}")
        big['w_out'] = tn_matmul(sv['y'][None], dyo[None], l, DEPTH, big['w_out'], f"dw_out_{l}", split=NDEV)
        dxl, dgl, dlp, dwa, dwx = lru_bwd(dylru, sv['proj'], sv['xc'], sv['hst'], q['lp'], q['wa_t'], q['wx_t'], f"{l}")
        dx1, accmi = mix_in_bwd(dxl, dgl, duv, sv['x1'], dx2, sv['vm'], g['w_in'], l, f"{l}")
        hm3 = sv['hm'][None]
        win = tn_matmul(hm3, dxl, l, DEPTH, big['w_in'], f"dw_in_x{l}", ntotal=NDEV, koff=0)
        win = tn_matmul(hm3, dgl, l, DEPTH, win, f"dw_in_g{l}", ntotal=NDEV, koff=2)
        big['w_in'] = tn_matmul(hm3, duv, l, DEPTH, win, f"dw_in_uv{l}", ntotal=NDEV, koff=4)
        dx0, dgu, a, df, acc1 = ffn_bwd(dx1, sv['x0'], sv['gu1'], sv['f1'], sv['v1'],
                                        g['gu'], g['down'], l, f"a{l}")
        big['gu'] = tn_matmul(sv['h1'][None], dgu, l, 2 * DEPTH, big['gu'], f"dw_gu_a{l}")
        big['down'] = tn_matmul(a, df[None], l, 2 * DEPTH, big['down'], f"dw_down_a{l}", split=2)
        dx = dx0
        dmod[l] = jnp.concatenate([acc1[0:3], accmi[0:2], accmo[2:3], acc2[0:3]], axis=0)
        small['ffn1_norm'][l] = acc1[3]
        small['mix_norm'][l] = accmi[3]
        small['ffn2_norm'][l] = acc2[3]
        small['conv_w'][l] = dlp[0:4]
        small['conv_b'][l] = dlp[4]
        small['gate_a_b'][l] = dlp[5].reshape(HEADS, HD)
        small['gate_x_b'][l] = dlp[6].reshape(HEADS, HD)
        small['lru_lambda'][l] = dlp[7]
        small['gate_a_w'][l] = _block_diag_extract(dwa)
        small['gate_x_w'][l] = _block_diag_extract(dwx)
        small['v_norm'][l] = dgp[0]
        small['lru_out_norm'][l] = dgp[1]
        small['gmlp_out_norm'][l] = dgp[2]
        small['spatial_w'][l] = dwm.reshape(HEADS, CHUNK, CHUNK) * tril
        small['spatial_b'][l] = dbz.reshape(CHUNK, HEADS, HD).sum(-1).T
    small = {k: jnp.stack(v) for k, v in small.items()}
    small['final_norm'] = dfin[0]
    return loss, dx, big, small, jnp.stack(dmod)


def ada_fwd(c_all, w_ada, b_loc):
    def body(c_ref, w_ref, b_ref, mod_ref, sc_ref):
        cv = c_ref[...]
        sc = cv * _sigmoid(cv)
        sc_ref[...] = sc
        mod_ref[...] = _dot3(sc, w_ref[...]) + b_ref[...]

    return pl.pallas_call(
        body, name="ada_fwd",
        grid=(DEPTH,),
        in_specs=[pl.BlockSpec((NDEV, D), lambda l: (0, 0)), pl.BlockSpec((None, D, AC), lambda l: (l, 0, 0)),
                  pl.BlockSpec((None, 1, AC), lambda l: (l, 0, 0))],
        out_specs=[pl.BlockSpec((None, NDEV, AC), lambda l: (l, 0, 0)), pl.BlockSpec((NDEV, D), lambda l: (0, 0))],
        out_shape=[jax.ShapeDtypeStruct((DEPTH, NDEV, AC), F32), jax.ShapeDtypeStruct((NDEV, D), F32)],
        compiler_params=_cparams("arbitrary"),
    )(c_all, w_ada, b_loc)


def ada_bwd(sc_t, dmod_cols):
    def body(sc_ref, dm_ref, g_ref):
        sc = sc_ref[...]
        dm = dm_ref[...]
        acc = sc[:, 0:1] * dm[0:1, :]
        for b in range(1, NDEV):
            acc = acc + sc[:, b:b + 1] * dm[b:b + 1, :]
        g_ref[...] = acc

    return pl.pallas_call(
        body, name="ada_bwd",
        grid=(DEPTH,),
        in_specs=[pl.BlockSpec((D, NDEV), lambda l: (0, 0)), pl.BlockSpec((None, NDEV, AC), lambda l: (l, 0, 0))],
        out_specs=pl.BlockSpec((None, None, D, AC), lambda l: (0, l, 0, 0)),
        out_shape=jax.ShapeDtypeStruct((1, DEPTH, D, AC), F32),
        compiler_params=_cparams("arbitrary"),
    )(sc_t, dmod_cols)


def _row_tile(rows):
    for tr in (512, 384, 352, 256, 128, 64, 32, 16, 8):
        if rows % tr == 0:
            return tr
    return rows


def adamw(gparts, slot0, w, m, v, name):
    P, _, R, C = gparts.shape
    L = w.shape[0]
    tr = _row_tile(R)

    def body(g_ref, w_ref, m_ref, v_ref, go_ref, do_ref, mo_ref, vo_ref):
        g = g_ref[0].astype(F32)
        for p in range(1, P):
            g = g + g_ref[p].astype(F32)
        go_ref[...] = g
        mn = ADAM_B1 * m_ref[...] + (1.0 - ADAM_B1) * g
        vn = ADAM_B2 * v_ref[...] + (1.0 - ADAM_B2) * (g * g)
        mo_ref[...] = mn
        vo_ref[...] = vn
        m_hat = mn / (1.0 - ADAM_B1 ** ADAM_STEP)
        v_hat = vn / (1.0 - ADAM_B2 ** ADAM_STEP)
        do_ref[...] = -ADAM_LR * (m_hat / (jnp.sqrt(v_hat) + ADAM_EPS) + ADAM_WD * w_ref[...])

    blk = pl.BlockSpec((None, tr, C), lambda l, i: (l, i, 0))
    return pl.pallas_call(
        body, name=name,
        grid=(L, R // tr),
        in_specs=[pl.BlockSpec((P, None, tr, C), lambda l, i: (0, slot0 + l, i, 0)), blk, blk, blk],
        out_specs=[blk, blk, blk, blk],
        out_shape=[jax.ShapeDtypeStruct((L, R, C), F32)] * 4,
        compiler_params=_cparams("arbitrary", "arbitrary"),
    )(gparts, w, m, v)


def sum_parts(parts):
    P, R, C = parts.shape

    def body(p_ref, o_ref):
        acc = p_ref[0]
        for p in range(1, P):
            acc = acc + p_ref[p]
        o_ref[...] = acc

    return pl.pallas_call(
        body, name="sum_parts",
        in_specs=[pl.BlockSpec(memory_space=pltpu.VMEM)],
        out_specs=pl.BlockSpec(memory_space=pltpu.VMEM),
        out_shape=jax.ShapeDtypeStruct((R, C), F32),
    )(parts)


def _place():
    return lax.axis_index("x"), lax.axis_index("y"), lax.axis_index("c")


def _slot(p):
    return 4 * p[0] + 2 * p[1] + p[2]


def all_gather(arrs, name):
    n = len(arrs)

    def body(*refs):
        ins, outs = refs[:n], refs[n:2 * n]
        send_sems, recv_sems, local_sems = refs[2 * n:]
        x, y, c = _place()
        me, sibling = (x, y, c), (x, y, 1 - c)
        chips = [(1 - x, y), (x, 1 - y), (1 - x, 1 - y)]

        def copy(a, k, block, to, src=None):
            dst = outs[a].at[_slot(block)]
            return pltpu.make_async_remote_copy(
                src_ref=dst if src is None else src, dst_ref=dst,
                send_sem=send_sems.at[a, k], recv_sem=recv_sems.at[a, k],
                device_id=to, device_id_type=MESH)

        mine = [pltpu.make_async_copy(ins[a], outs[a].at[_slot(me)], local_sems.at[a]) for a in range(n)]
        for cp in mine:
            cp.start()
        first = []
        for a in range(n):
            first.append(copy(a, 0, me, sibling, src=ins[a]))
            first += [copy(a, 1 + j, me, (*chip, c), src=ins[a]) for j, chip in enumerate(chips)]
        for cp in first:
            cp.start()
        passed = []
        for j, chip in enumerate(chips):
            for a in range(n):
                copy(a, 1 + j, (*chip, c), me).wait_recv()
                cp = copy(a, 4 + j, (*chip, c), sibling)
                cp.start()
                passed.append(cp)
        for a in range(n):
            copy(a, 0, sibling, me).wait_recv()
            for j, chip in enumerate(chips):
                copy(a, 4 + j, (*chip, 1 - c), me).wait_recv()
        for cp in first + passed:
            cp.wait_send()
        for cp in mine:
            cp.wait()

    return pl.pallas_call(
        body, name=name,
        in_specs=[ANY] * n, out_specs=[ANY] * n,
        out_shape=[jax.ShapeDtypeStruct((NDEV,) + a.shape, a.dtype) for a in arrs],
        scratch_shapes=[pltpu.SemaphoreType.DMA((n, NDEV - 1)), pltpu.SemaphoreType.DMA((n, NDEV - 1)),
                        pltpu.SemaphoreType.DMA((n,))],
    )(*arrs)


def all_to_all(arrs, name):
    n = len(arrs)

    def body(*refs):
        ins, outs = refs[:n], refs[n:2 * n]
        send_sems, recv_sems, local_sems = refs[2 * n:]
        x, y, c = _place()
        me = (x, y, c)

        def peer(k):
            return (1 - x if k & 4 else x, 1 - y if k & 2 else y, 1 - c if k & 1 else c)

        def copy(a, k):
            return pltpu.make_async_remote_copy(
                src_ref=ins[a].at[_slot(peer(k))], dst_ref=outs[a].at[_slot(me)],
                send_sem=send_sems.at[a, k - 1], recv_sem=recv_sems.at[a, k - 1],
                device_id=peer(k), device_id_type=MESH)

        def landing(a, k):
            return pltpu.make_async_remote_copy(
                src_ref=outs[a].at[_slot(peer(k))], dst_ref=outs[a].at[_slot(peer(k))],
                send_sem=send_sems.at[a, k - 1], recv_sem=recv_sems.at[a, k - 1],
                device_id=me, device_id_type=MESH)

        mine = [pltpu.make_async_copy(ins[a].at[_slot(me)], outs[a].at[_slot(me)], local_sems.at[a]) for a in range(n)]
        for cp in mine:
            cp.start()
        sends = [copy(a, k) for a in range(n) for k in range(1, NDEV)]
        for cp in sends:
            cp.start()
        for a in range(n):
            for k in range(1, NDEV):
                landing(a, k).wait_recv()
        for cp in sends:
            cp.wait_send()
        for cp in mine:
            cp.wait()

    return pl.pallas_call(
        body, name=name,
        in_specs=[ANY] * n, out_specs=[ANY] * n,
        out_shape=[jax.ShapeDtypeStruct(a.shape, a.dtype) for a in arrs],
        scratch_shapes=[pltpu.SemaphoreType.DMA((n, NDEV - 1)), pltpu.SemaphoreType.DMA((n, NDEV - 1)),
                        pltpu.SemaphoreType.DMA((n,))],
    )(*arrs)


WEIGHTS = ['w_ada', 'b_ada', 'ffn1_norm', 'ffn1_w_gu', 'ffn1_w_down', 'mix_norm', 'w_in', 'conv_w', 'conv_b',
           'gate_a_w', 'gate_a_b', 'gate_x_w', 'gate_x_b', 'lru_lambda', 'v_norm', 'spatial_w', 'spatial_b',
           'lru_out_norm', 'gmlp_out_norm', 'w_out', 'ffn2_norm', 'ffn2_w_gu', 'ffn2_w_down', 'final_norm']
PACKED = ['b_ada', 'ffn1_norm', 'mix_norm', 'conv_b', 'gate_a_w', 'gate_a_b', 'gate_x_w', 'gate_x_b', 'lru_lambda',
          'v_norm', 'spatial_w', 'spatial_b', 'lru_out_norm', 'gmlp_out_norm', 'ffn2_norm', 'final_norm', 'conv_w']
PACK_LANES = 128
PACK_ROW_ALIGN = 8 * NDEV


def _pack(d):
    flat = jnp.concatenate([d[k].reshape(-1).astype(F32) for k in PACKED])
    rows = -(-flat.shape[0] // (PACK_LANES * PACK_ROW_ALIGN)) * PACK_ROW_ALIGN
    flat = jnp.concatenate([flat, jnp.zeros((rows * PACK_LANES - flat.shape[0],), F32)])
    return flat.reshape(rows, PACK_LANES)


def _unpack(buf, shapes):
    flat = buf.reshape(-1)
    out, off = {}, 0
    for k in PACKED:
        size = 1
        for s in shapes[k]:
            size *= s
        out[k] = flat[off:off + size].reshape(shapes[k])
        off += size
    return out


def kernel(x, c, w_ada, b_ada, ffn1_norm, ffn1_w_gu, ffn1_w_down, mix_norm, w_in, conv_w, conv_b, gate_a_w, gate_a_b, gate_x_w, gate_x_b, lru_lambda, v_norm, spatial_w, spatial_b, lru_out_norm, gmlp_out_norm, w_out, ffn2_norm, ffn2_w_gu, ffn2_w_down, final_norm, loss_target, m_w_ada, m_b_ada, m_ffn1_norm, m_ffn1_w_gu, m_ffn1_w_down, m_mix_norm, m_w_in, m_conv_w, m_conv_b, m_gate_a_w, m_gate_a_b, m_gate_x_w, m_gate_x_b, m_lru_lambda, m_v_norm, m_spatial_w, m_spatial_b, m_lru_out_norm, m_gmlp_out_norm, m_w_out, m_ffn2_norm, m_ffn2_w_gu, m_ffn2_w_down, m_final_norm, v_w_ada, v_b_ada, v_ffn1_norm, v_ffn1_w_gu, v_ffn1_w_down, v_mix_norm, v_w_in, v_conv_w, v_conv_b, v_gate_a_w, v_gate_a_b, v_gate_x_w, v_gate_x_b, v_lru_lambda, v_v_norm, v_spatial_w, v_spatial_b, v_lru_out_norm, v_gmlp_out_norm, v_w_out, v_ffn2_norm, v_ffn2_w_gu, v_ffn2_w_down, v_final_norm):
    w = dict(w_ada=w_ada, b_ada=b_ada, ffn1_norm=ffn1_norm, ffn1_w_gu=ffn1_w_gu, ffn1_w_down=ffn1_w_down, mix_norm=mix_norm, w_in=w_in, conv_w=conv_w, conv_b=conv_b, gate_a_w=gate_a_w, gate_a_b=gate_a_b, gate_x_w=gate_x_w, gate_x_b=gate_x_b, lru_lambda=lru_lambda, v_norm=v_norm, spatial_w=spatial_w, spatial_b=spatial_b, lru_out_norm=lru_out_norm, gmlp_out_norm=gmlp_out_norm, w_out=w_out, ffn2_norm=ffn2_norm, ffn2_w_gu=ffn2_w_gu, ffn2_w_down=ffn2_w_down, final_norm=final_norm)
    m = dict(w_ada=m_w_ada, b_ada=m_b_ada, ffn1_norm=m_ffn1_norm, ffn1_w_gu=m_ffn1_w_gu, ffn1_w_down=m_ffn1_w_down, mix_norm=m_mix_norm, w_in=m_w_in, conv_w=m_conv_w, conv_b=m_conv_b, gate_a_w=m_gate_a_w, gate_a_b=m_gate_a_b, gate_x_w=m_gate_x_w, gate_x_b=m_gate_x_b, lru_lambda=m_lru_lambda, v_norm=m_v_norm, spatial_w=m_spatial_w, spatial_b=m_spatial_b, lru_out_norm=m_lru_out_norm, gmlp_out_norm=m_gmlp_out_norm, w_out=m_w_out, ffn2_norm=m_ffn2_norm, ffn2_w_gu=m_ffn2_w_gu, ffn2_w_down=m_ffn2_w_down, final_norm=m_final_norm)
    v = dict(w_ada=v_w_ada, b_ada=v_b_ada, ffn1_norm=v_ffn1_norm, ffn1_w_gu=v_ffn1_w_gu, ffn1_w_down=v_ffn1_w_down, mix_norm=v_mix_norm, w_in=v_w_in, conv_w=v_conv_w, conv_b=v_conv_b, gate_a_w=v_gate_a_w, gate_a_b=v_gate_a_b, gate_x_w=v_gate_x_w, gate_x_b=v_gate_x_b, lru_lambda=v_lru_lambda, v_norm=v_v_norm, spatial_w=v_spatial_w, spatial_b=v_spatial_b, lru_out_norm=v_lru_out_norm, gmlp_out_norm=v_gmlp_out_norm, w_out=v_w_out, ffn2_norm=v_ffn2_norm, ffn2_w_gu=v_ffn2_w_gu, ffn2_w_down=v_ffn2_w_down, final_norm=v_final_norm)
    me = 4 * lax.axis_index("x") + 2 * lax.axis_index("y") + lax.axis_index("c")

    gu_loc = jnp.concatenate([ffn1_w_gu, ffn2_w_gu], axis=0).astype(BF16)
    down_loc = jnp.concatenate([ffn1_w_down, ffn2_w_down], axis=0).astype(BF16)
    gu_g, down_g, win_g, wout_g, conv_g, c_g = all_gather(
        [gu_loc, down_loc, w_in.astype(BF16), w_out.astype(BF16), conv_w, c], "gather_weights")
    gathered = dict(gu=gu_g, down=down_g, w_in=win_g, w_out=wout_g)
    conv_w_full = conv_g.transpose(1, 2, 0, 3).reshape(DEPTH, CONV_WIDTH, LW)

    b_loc = lax.dynamic_slice(b_ada, (0, me * AC), (DEPTH, AC)).reshape(DEPTH, 1, AC)
    mod_cols, sc_all = ada_fwd(c_g.reshape(NDEV, D), w_ada, b_loc)
    (mod_rows,) = all_to_all([mod_cols.transpose(1, 0, 2)], "scatter_mod")
    mod = mod_rows.transpose(1, 0, 2).reshape(DEPTH, NMOD, D)

    small_w = {k: w[k] for k in PACKED if k != 'conv_w'}
    loss_loc, dx, big, small_g, dmod = local_fwd_bwd(x[0], loss_target[0], mod, small_w, gathered, conv_w_full)
    loss = lax.psum(loss_loc, ("x", "y", "c"))

    small_g['b_ada'] = dmod.reshape(DEPTH, NMOD * D)
    gpack = _pack(small_g)
    rows = gpack.shape[0]
    dmod_out = dmod.reshape(DEPTH, NDEV, AC).transpose(1, 0, 2)
    gu_r, down_r, win_r, wout_r, dmod_r, pack_r = all_to_all(
        [big['gu'], big['down'], big['w_in'], big['w_out'], dmod_out, gpack.reshape(NDEV, rows // NDEV, PACK_LANES)],
        "scatter_grads")
    (gsum_g,) = all_gather([sum_parts(pack_r)], "gather_small_grads")
    gsum = gsum_g.reshape(1, 1, rows, PACK_LANES)

    res = {}
    res['ffn1_w_gu'] = adamw(gu_r, 0, w['ffn1_w_gu'], m['ffn1_w_gu'], v['ffn1_w_gu'], "adamw_gu_a")
    res['ffn2_w_gu'] = adamw(gu_r, DEPTH, w['ffn2_w_gu'], m['ffn2_w_gu'], v['ffn2_w_gu'], "adamw_gu_b")
    res['ffn1_w_down'] = adamw(down_r, 0, w['ffn1_w_down'], m['ffn1_w_down'], v['ffn1_w_down'], "adamw_down_a")
    res['ffn2_w_down'] = adamw(down_r, DEPTH, w['ffn2_w_down'], m['ffn2_w_down'], v['ffn2_w_down'], "adamw_down_b")
    res['w_in'] = adamw(win_r, 0, w['w_in'], m['w_in'], v['w_in'], "adamw_w_in")
    res['w_out'] = adamw(wout_r, 0, w['w_out'], m['w_out'], v['w_out'], "adamw_w_out")
    g_ada = ada_bwd(sc_all.T, dmod_r.transpose(1, 0, 2))
    res['w_ada'] = adamw(g_ada, 0, w['w_ada'], m['w_ada'], v['w_ada'], "adamw_w_ada")
    shapes = {k: w[k].shape for k in PACKED}
    shapes['conv_w'] = (DEPTH, CONV_WIDTH, LW)
    dummy = jnp.zeros(shapes['conv_w'], F32)
    packs = adamw(gsum, 0, _pack({**small_w, 'conv_w': dummy})[None], _pack({**{k: m[k] for k in small_w}, 'conv_w': dummy})[None],
                  _pack({**{k: v[k] for k in small_w}, 'conv_w': dummy})[None], "adamw_small")
    unpacked = [_unpack(b[0], shapes) for b in packs]
    for k in small_w:
        res[k] = tuple(u[k] for u in unpacked)
    gconv = lax.dynamic_slice(unpacked[0]['conv_w'], (0, 0, me * (LW // NDEV)), (DEPTH, CONV_WIDTH, LW // NDEV))
    cshape = (1, DEPTH * CONV_WIDTH, LW // NDEV)
    rc = adamw(gconv.reshape((1,) + cshape), 0, conv_w.reshape(cshape), m['conv_w'].reshape(cshape),
               v['conv_w'].reshape(cshape), "adamw_conv_w")
    res['conv_w'] = tuple(r.reshape(conv_w.shape) for r in rc)

    return (loss, dx[None], *[res[k][0] for k in WEIGHTS], *[res[k][1] for k in WEIGHTS],
            *[res[k][2] for k in WEIGHTS], *[res[k][3] for k in WEIGHTS])
```

```python
import jax
import jax.numpy as jnp
from jax import lax
from jax.experimental import pallas as pl
from jax.experimental.pallas import tpu as pltpu

F32 = jnp.float32
BF16 = jnp.bfloat16

NDEV = 8
DEPTH = 2
D = 1024
DFF = 2816
FC = 2 * DFF // NDEV
NCHUNK = DFF // FC
DR = DFF // NDEV
LW = 512
GW = 512
HD = 64
HEADS = 8
CHUNK = 128
PC = 2 * (LW + GW) // NDEV
OR = D // NDEV
NMOD = 9
AC = NMOD * D // NDEV
LC = 128
EPS = 1e-6
RG_LRU_C = 8.0
CONV_WIDTH = 4

ADAM_LR = 0.001
ADAM_B1 = 0.9
ADAM_B2 = 0.999
ADAM_EPS = 1e-08
ADAM_WD = 0.01
ADAM_STEP = 10

VMEM_LIMIT_BYTES = 60 * 1024 * 1024
MESH = pl.DeviceIdType.MESH
ANY = pl.BlockSpec(memory_space=pl.ANY)


def _cparams(*sem):
    return pltpu.CompilerParams(dimension_semantics=tuple(sem) if sem else None,
                                vmem_limit_bytes=VMEM_LIMIT_BYTES)


def _dot(a, b):
    return jnp.dot(a, b, preferred_element_type=F32)


def _dot_nt(a, b):
    return lax.dot_general(a, b, (((1,), (1,)), ((), ())), preferred_element_type=F32)


def _dot_tn(a, b):
    return lax.dot_general(a, b, (((0,), (0,)), ((), ())), preferred_element_type=F32)


def _split(a):
    hi = a.astype(BF16)
    lo = (a - hi.astype(F32)).astype(BF16)
    return hi, lo


def _dot3(a, b):
    ah, al = _split(a)
    bh, bl = _split(b)
    return _dot(ah, bh) + (_dot(ah, bl) + _dot(al, bh))


def _csum(a):
    return jnp.sum(a, axis=0, keepdims=True)


def _rmean(a):
    return jnp.mean(a, axis=-1, keepdims=True)


def _sigmoid(a):
    return 1.0 / (1.0 + jnp.exp(-a))


_GELU_K = 0.7978845608028654
_GELU_C = 0.044715


def _gelu(a):
    return 0.5 * a * (1.0 + jnp.tanh(_GELU_K * (a + _GELU_C * a * a * a)))


def _gelu_grad(a):
    t = jnp.tanh(_GELU_K * (a + _GELU_C * a * a * a))
    return 0.5 * (1.0 + t) + 0.5 * a * (1.0 - t * t) * (_GELU_K * (1.0 + 3.0 * _GELU_C * a * a))


def _norm_mod(x, gain, scale, shift):
    rstd = lax.rsqrt(_rmean(x * x) + EPS)
    return (x * rstd * gain) * (1.0 + scale) + shift


def _norm_mod_bwd(dh, x, gain, scale):
    rstd = lax.rsqrt(_rmean(x * x) + EPS)
    xhat = x * rstd
    dshift = _csum(dh)
    dscale = _csum(dh * (xhat * gain))
    dhn = dh * (1.0 + scale)
    dgain = _csum(dhn * xhat)
    dxhat = dhn * gain
    dx = rstd * (dxhat - xhat * _rmean(dxhat * xhat))
    return dx, dshift, dscale, dgain


def _rms(x, gain):
    rstd = lax.rsqrt(_rmean(x * x) + EPS)
    return x * rstd * gain


def _rms_bwd(dy, x, gain):
    rstd = lax.rsqrt(_rmean(x * x) + EPS)
    xhat = x * rstd
    dgain = _csum(dy * xhat)
    dxhat = dy * gain
    return rstd * (dxhat - xhat * _rmean(dxhat * xhat)), dgain


def _seg_mean(a, pavg):
    hi, lo = _split(a)
    return _dot(hi, pavg) + _dot(lo, pavg)


def _fetch_blocks(src_hbm, dst_vmem, sems, rows):
    copies = []
    for k in range(NDEV):
        dst = dst_vmem.at[k] if rows is None else dst_vmem.at[pl.ds(k * rows, rows)]
        copies.append(pltpu.make_async_copy(src_hbm.at[k], dst, sems.at[k]))
    for cp in copies:
        cp.start()
    for cp in copies:
        cp.wait()


def _place():
    return lax.axis_index("x"), lax.axis_index("y"), lax.axis_index("c")


def _slot(p):
    return 4 * p[0] + 2 * p[1] + p[2]


class GatherRide:
    def __init__(self, srcs):
        self.n = len(srcs)
        self.index = [i for _, i in srcs]
        self.args = [a for a, _ in srcs]
        self.out_shape = [jax.ShapeDtypeStruct((NDEV,) + (a.shape if i is None else a.shape[1:]), a.dtype)
                          for a, i in srcs]
        self.scratch = [pltpu.SemaphoreType.DMA((self.n, NDEV - 1)), pltpu.SemaphoreType.DMA((self.n, NDEV - 1)),
                        pltpu.SemaphoreType.DMA((self.n,))]

    def hooks(self, ins, outs, sems):
        send_sems, recv_sems, local_sems = sems
        n = self.n
        x, y, c = _place()
        me, sibling = (x, y, c), (x, y, 1 - c)
        chips = [(1 - x, y), (x, 1 - y), (1 - x, 1 - y)]

        def local(a):
            return ins[a] if self.index[a] is None else ins[a].at[self.index[a]]

        def copy(a, k, block, to, src=None):
            dst = outs[a].at[_slot(block)]
            return pltpu.make_async_remote_copy(
                src_ref=dst if src is None else src, dst_ref=dst,
                send_sem=send_sems.at[a, k], recv_sem=recv_sems.at[a, k],
                device_id=to, device_id_type=MESH)

        def mine():
            return [pltpu.make_async_copy(local(a), outs[a].at[_slot(me)], local_sems.at[a]) for a in range(n)]

        def first():
            cps = []
            for a in range(n):
                cps.append(copy(a, 0, me, sibling, src=local(a)))
                cps += [copy(a, 1 + j, me, (*chip, c), src=local(a)) for j, chip in enumerate(chips)]
            return cps

        def passed():
            return [copy(a, 4 + j, (*chip, c), sibling) for j, chip in enumerate(chips) for a in range(n)]

        def start():
            for cp in mine() + first():
                cp.start()

        def mid():
            for j, chip in enumerate(chips):
                for a in range(n):
                    copy(a, 1 + j, (*chip, c), me).wait_recv()
                    copy(a, 4 + j, (*chip, c), sibling).start()

        def finish():
            for a in range(n):
                copy(a, 0, sibling, me).wait_recv()
                for j, chip in enumerate(chips):
                    copy(a, 4 + j, (*chip, 1 - c), me).wait_recv()
            for cp in first() + passed():
                cp.wait_send()
            for cp in mine():
                cp.wait()

        return start, mid, finish


def all_gather(srcs, name):
    ride = GatherRide(srcs)
    n = ride.n

    def body(*refs):
        start, mid, finish = ride.hooks(refs[:n], refs[n:2 * n], refs[2 * n:])
        start()
        mid()
        finish()

    return pl.pallas_call(
        body, name=name,
        in_specs=[ANY] * n, out_specs=[ANY] * n, out_shape=ride.out_shape, scratch_shapes=ride.scratch,
    )(*ride.args)


def _call(core, ride, *, name, grid, in_specs, out_specs, out_shape, scratch_shapes, args):
    if ride is None:
        outs = pl.pallas_call(core, name=name, grid=grid, in_specs=in_specs, out_specs=out_specs,
                              out_shape=out_shape, scratch_shapes=scratch_shapes,
                              compiler_params=_cparams("arbitrary"))(*args)
        return outs, []
    n_in, n_out, n_sc, n = len(in_specs), len(out_shape), len(scratch_shapes), ride.n
    nsteps = grid[0]
    mid_step = max(nsteps - 2, 0)

    def body(*refs):
        cuts = [n_in, n_in + n, n_in + n + n_out, n_in + 2 * n + n_out, n_in + 2 * n + n_out + n_sc]
        ci, ri, co, ro, cs, rs = (refs[a:b] for a, b in zip([0] + cuts, cuts + [len(refs)]))
        start, mid, finish = ride.hooks(ri, ro, rs)
        i = pl.program_id(0)
        pl.when(i == 0)(start)
        core(*ci, *co, *cs)
        pl.when(i == mid_step)(mid)
        pl.when(i == nsteps - 1)(finish)

    outs = pl.pallas_call(
        body, name=name, grid=grid,
        in_specs=list(in_specs) + [ANY] * n, out_specs=list(out_specs) + [ANY] * n,
        out_shape=list(out_shape) + ride.out_shape, scratch_shapes=list(scratch_shapes) + ride.scratch,
        compiler_params=_cparams("arbitrary"))(*args, *ride.args)
    return outs[:n_out], outs[n_out:]


def all_to_all(arrs, name):
    n = len(arrs)

    def body(*refs):
        ins, outs = refs[:n], refs[n:2 * n]
        send_sems, recv_sems, local_sems = refs[2 * n:]
        x, y, c = _place()
        me = (x, y, c)

        def peer(k):
            return (1 - x if k & 4 else x, 1 - y if k & 2 else y, 1 - c if k & 1 else c)

        def copy(a, k):
            return pltpu.make_async_remote_copy(
                src_ref=ins[a].at[_slot(peer(k))], dst_ref=outs[a].at[_slot(me)],
                send_sem=send_sems.at[a, k - 1], recv_sem=recv_sems.at[a, k - 1],
                device_id=peer(k), device_id_type=MESH)

        def landing(a, k):
            return pltpu.make_async_remote_copy(
                src_ref=outs[a].at[_slot(peer(k))], dst_ref=outs[a].at[_slot(peer(k))],
                send_sem=send_sems.at[a, k - 1], recv_sem=recv_sems.at[a, k - 1],
                device_id=me, device_id_type=MESH)

        mine = [pltpu.make_async_copy(ins[a].at[_slot(me)], outs[a].at[_slot(me)], local_sems.at[a]) for a in range(n)]
        for cp in mine:
            cp.start()
        sends = [copy(a, k) for a in range(n) for k in range(1, NDEV)]
        for cp in sends:
            cp.start()
        for a in range(n):
            for k in range(1, NDEV):
                landing(a, k).wait_recv()
        for cp in sends:
            cp.wait_send()
        for cp in mine:
            cp.wait()

    return pl.pallas_call(
        body, name=name,
        in_specs=[ANY] * n, out_specs=[ANY] * n,
        out_shape=[jax.ShapeDtypeStruct(a.shape, a.dtype) for a in arrs],
        scratch_shapes=[pltpu.SemaphoreType.DMA((n, NDEV - 1)), pltpu.SemaphoreType.DMA((n, NDEV - 1)),
                        pltpu.SemaphoreType.DMA((n,))],
    )(*arrs)


FFN_TS = 256


def ffn_fwd(x, vec, wgu_g, wdown_g, tag, ride=None):
    S = x.shape[0]
    ts = min(FFN_TS, S)

    def body(x_ref, vec_ref, wgu_hbm, wd_hbm, xo_ref, h_ref, gu_ref, f_ref, wgu_v, wd_v, sems):
        @pl.when(pl.program_id(0) == 0)
        def _():
            _fetch_blocks(wgu_hbm, wgu_v, sems.at[0], None)
            _fetch_blocks(wd_hbm, wd_v, sems.at[1], DR)

        xv = x_ref[...]
        h = _norm_mod(xv, vec_ref[3:4, :], vec_ref[1:2, :], vec_ref[0:1, :]).astype(BF16)
        h_ref[...] = h
        acc = jnp.zeros((ts, D), F32)
        for j in range(NCHUNK):
            g = _dot(h, wgu_v[j])
            u = _dot(h, wgu_v[NCHUNK + j])
            gu_ref[j] = g.astype(BF16)
            gu_ref[NCHUNK + j] = u.astype(BF16)
            a = (g * _sigmoid(g) * u).astype(BF16)
            acc = acc + _dot(a, wd_v[pl.ds(j * FC, FC), :])
        f_ref[...] = acc.astype(BF16)
        xo_ref[...] = xv + (0.5 * vec_ref[2:3, :]) * acc

    return _call(
        body, ride, name=f"ffn_fwd_{tag}",
        grid=(S // ts,),
        in_specs=[pl.BlockSpec((ts, D), lambda i: (i, 0)),
                  pl.BlockSpec((8, D), lambda i: (0, 0)), ANY, ANY],
        out_specs=[pl.BlockSpec((ts, D), lambda i: (i, 0)),
                   pl.BlockSpec((ts, D), lambda i: (i, 0)),
                   pl.BlockSpec((NDEV, ts, FC), lambda i: (0, i, 0)),
                   pl.BlockSpec((ts, D), lambda i: (i, 0))],
        out_shape=[jax.ShapeDtypeStruct((S, D), F32), jax.ShapeDtypeStruct((S, D), BF16),
                   jax.ShapeDtypeStruct((NDEV, S, FC), BF16), jax.ShapeDtypeStruct((S, D), BF16)],
        scratch_shapes=[pltpu.VMEM((NDEV, D, FC), BF16), pltpu.VMEM((DFF, D), BF16),
                        pltpu.SemaphoreType.DMA((2, NDEV))],
        args=(x, vec, wgu_g, wdown_g))


def ffn_bwd(dxo, x, gu, f, vec, wgu_g, wdown_g, tag):
    S = x.shape[0]
    ts = min(FFN_TS, S)

    def body(dxo_ref, x_ref, gu_ref, f_ref, vec_ref, wgu_hbm, wd_hbm,
             dx_ref, dgu_ref, a_ref, df_ref, acc_ref, wgu_v, wd_v, sems):
        @pl.when(pl.program_id(0) == 0)
        def _():
            _fetch_blocks(wgu_hbm, wgu_v, sems.at[0], None)
            _fetch_blocks(wd_hbm, wd_v, sems.at[1], DR)
            acc_ref[...] = jnp.zeros_like(acc_ref)

        dxo_v = dxo_ref[...]
        dgate = 0.5 * _csum(dxo_v * f_ref[...].astype(F32))
        df = ((0.5 * vec_ref[2:3, :]) * dxo_v).astype(BF16)
        df_ref[...] = df
        dh = jnp.zeros((ts, D), F32)
        for j in range(NCHUNK):
            da = _dot_nt(df, wd_v[pl.ds(j * FC, FC), :])
            g = gu_ref[j].astype(F32)
            u = gu_ref[NCHUNK + j].astype(F32)
            sg = _sigmoid(g)
            si = g * sg
            a_ref[j] = (si * u).astype(BF16)
            dg = (da * u * (sg * (1.0 + g * (1.0 - sg)))).astype(BF16)
            du = (da * si).astype(BF16)
            dgu_ref[j] = dg
            dgu_ref[NCHUNK + j] = du
            dh = dh + _dot_nt(dg, wgu_v[j]) + _dot_nt(du, wgu_v[NCHUNK + j])
        dx, dshift, dscale, dgain = _norm_mod_bwd(dh, x_ref[...], vec_ref[3:4, :], vec_ref[1:2, :])
        dx_ref[...] = dx + dxo_v
        acc_ref[0:1, :] += dshift
        acc_ref[1:2, :] += dscale
        acc_ref[2:3, :] += dgate
        acc_ref[3:4, :] += dgain

    row = pl.BlockSpec((ts, D), lambda i: (i, 0))
    return pl.pallas_call(
        body, name=f"ffn_bwd_{tag}",
        grid=(S // ts,),
        in_specs=[row, row, pl.BlockSpec((NDEV, ts, FC), lambda i: (0, i, 0)), row,
                  pl.BlockSpec((8, D), lambda i: (0, 0)), ANY, ANY],
        out_specs=[row, pl.BlockSpec((NDEV, ts, FC), lambda i: (0, i, 0)),
                   pl.BlockSpec((NCHUNK, ts, FC), lambda i: (0, i, 0)), row,
                   pl.BlockSpec((8, D), lambda i: (0, 0))],
        out_shape=[jax.ShapeDtypeStruct((S, D), F32), jax.ShapeDtypeStruct((NDEV, S, FC), BF16),
                   jax.ShapeDtypeStruct((NCHUNK, S, FC), BF16), jax.ShapeDtypeStruct((S, D), BF16),
                   jax.ShapeDtypeStruct((8, D), F32)],
        scratch_shapes=[pltpu.VMEM((NDEV, D, FC), BF16), pltpu.VMEM((DFF, D), BF16),
                        pltpu.SemaphoreType.DMA((2, NDEV))],
        compiler_params=_cparams("arbitrary"),
    )(dxo, x, gu, f, vec, wgu_g, wdown_g)


def tn_matmul_scatter(a, b, slot, nslots, prev, name, koff=0, split=1):
    na, S, M = a.shape
    nb, _, N = b.shape
    ncall = max(na, nb)
    ts = min(512, S)
    nsteps = S // ts
    mp = M // split
    dests = [(koff + k) * split + p for k in range(ncall) for p in range(split)]

    def body(*refs):
        a_ref, b_ref = refs[0], refs[1]
        recv_ref, acc, sbuf, send_sems, recv_sems = refs[-5:]
        k = pl.program_id(0)
        s = pl.program_id(1)
        me = _slot(_place())

        def copies(kk, p):
            d = (koff + kk) * split + p
            src = sbuf.at[kk, pl.ds(p * mp, mp)]
            dst = recv_ref.at[me, slot]
            return d, pltpu.make_async_copy(src, dst, send_sems.at[d]), pltpu.make_async_remote_copy(
                src_ref=src, dst_ref=dst, send_sem=send_sems.at[d], recv_sem=recv_sems.at[me],
                device_id=(d // 4, (d // 2) % 2, d % 2), device_id_type=MESH)

        @pl.when(s == 0)
        def _():
            acc[...] = jnp.zeros_like(acc)

        acc[...] += _dot_tn(a_ref[...], b_ref[...])

        for kk in range(ncall):
            @pl.when((s == nsteps - 1) & (k == kk))
            def _():
                sbuf[kk] = acc[...].astype(BF16)
                for p in range(split):
                    d, loc, rem = copies(kk, p)
                    pl.when(me == d)(loc.start)
                    pl.when(me != d)(rem.start)

        @pl.when((s == nsteps - 1) & (k == ncall - 1))
        def _():
            for kk in range(ncall):
                for p in range(split):
                    d, loc, rem = copies(kk, p)
                    pl.when(me == d)(loc.wait)
                    pl.when(me != d)(rem.wait_send)
            owner = (me >= dests[0]) & (me <= dests[-1])
            for src in range(NDEV):
                @pl.when(owner & (me != src))
                def _():
                    pltpu.make_async_remote_copy(
                        src_ref=recv_ref.at[src, slot], dst_ref=recv_ref.at[src, slot],
                        send_sem=send_sems.at[src], recv_sem=recv_sems.at[src],
                        device_id=(src // 4, (src // 2) % 2, src % 2), device_id_type=MESH).wait_recv()

    in_specs = [pl.BlockSpec((None, ts, M), (lambda k, s: (k, s, 0)) if na > 1 else (lambda k, s: (0, s, 0))),
                pl.BlockSpec((None, ts, N), (lambda k, s: (k, s, 0)) if nb > 1 else (lambda k, s: (0, s, 0)))]
    args = [a, b]
    aliases = {}
    if prev is not None:
        in_specs.append(ANY)
        args.append(prev)
        aliases = {2: 0}
    return pl.pallas_call(
        body, name=name,
        grid=(ncall, nsteps),
        in_specs=in_specs,
        out_specs=ANY,
        out_shape=jax.ShapeDtypeStruct((NDEV, nslots, mp, N), BF16),
        scratch_shapes=[pltpu.VMEM((M, N), F32), pltpu.VMEM((ncall, M, N), BF16),
                        pltpu.SemaphoreType.DMA((NDEV,)), pltpu.SemaphoreType.DMA((NDEV,))],
        input_output_aliases=aliases,
        compiler_params=_cparams("arbitrary", "arbitrary"),
    )(*args)


MIX_TS = 256


def mix_in_fwd(x, vec, win_g, tag, ride=None):
    S = x.shape[0]
    ts = min(MIX_TS, S)

    def body(x_ref, vec_ref, win_ref, hm_ref, proj_ref):
        h = _norm_mod(x_ref[...], vec_ref[3:4, :], vec_ref[1:2, :], vec_ref[0:1, :]).astype(BF16)
        hm_ref[...] = h
        for k in range(NDEV):
            proj_ref[k] = _dot(h, win_ref[k])

    return _call(
        body, ride, name=f"mix_in_fwd_{tag}",
        grid=(S // ts,),
        in_specs=[pl.BlockSpec((ts, D), lambda i: (i, 0)), pl.BlockSpec((8, D), lambda i: (0, 0)),
                  pl.BlockSpec((NDEV, D, PC), lambda i: (0, 0, 0))],
        out_specs=[pl.BlockSpec((ts, D), lambda i: (i, 0)),
                   pl.BlockSpec((NDEV, ts, PC), lambda i: (0, i, 0))],
        out_shape=[jax.ShapeDtypeStruct((S, D), BF16), jax.ShapeDtypeStruct((NDEV, S, PC), F32)],
        scratch_shapes=[], args=(x, vec, win_g))


def mix_in_bwd(dxl, dgl, duv, x, dxo, vec, win_g, tag):
    S = x.shape[0]
    ts = min(MIX_TS, S)

    def body(dxl_ref, dgl_ref, duv_ref, x_ref, dxo_ref, vec_ref, win_ref, dx_ref, acc_ref):
        @pl.when(pl.program_id(0) == 0)
        def _():
            acc_ref[...] = jnp.zeros_like(acc_ref)

        parts = [dxl_ref[0], dxl_ref[1], dgl_ref[0], dgl_ref[1],
                 duv_ref[0], duv_ref[1], duv_ref[2], duv_ref[3]]
        dh = jnp.zeros((ts, D), F32)
        for k in range(NDEV):
            dh = dh + _dot_nt(parts[k], win_ref[k])
        dx, dshift, dscale, dgain = _norm_mod_bwd(dh, x_ref[...], vec_ref[3:4, :], vec_ref[1:2, :])
        dx_ref[...] = dx + dxo_ref[...]
        acc_ref[0:1, :] += dshift
        acc_ref[1:2, :] += dscale
        acc_ref[3:4, :] += dgain

    row = pl.BlockSpec((ts, D), lambda i: (i, 0))
    return pl.pallas_call(
        body, name=f"mix_in_bwd_{tag}",
        grid=(S // ts,),
        in_specs=[pl.BlockSpec((2, ts, PC), lambda i: (0, i, 0)), pl.BlockSpec((2, ts, PC), lambda i: (0, i, 0)),
                  pl.BlockSpec((4, ts, PC), lambda i: (0, i, 0)), row, row,
                  pl.BlockSpec((8, D), lambda i: (0, 0)),
                  pl.BlockSpec((NDEV, D, PC), lambda i: (0, 0, 0))],
        out_specs=[row, pl.BlockSpec((8, D), lambda i: (0, 0))],
        out_shape=[jax.ShapeDtypeStruct((S, D), F32), jax.ShapeDtypeStruct((8, D), F32)],
        compiler_params=_cparams("arbitrary"),
    )(dxl, dgl, duv, x, dxo, vec, win_g)


def _shift_down(z, k, row):
    return jnp.where(row >= k, pltpu.roll(z, k, 0), 0.0)


def _shift_up(z, k, row, n):
    return jnp.where(row < n - k, pltpu.roll(z, n - k, 0), 0.0)


def _lru_gates(xc, lp_ref, wa_ref, wx_ref):
    xcb = xc.astype(BF16)
    ra = _sigmoid(_dot(xcb, wa_ref[...]) + lp_ref[5:6, :])
    ix = _sigmoid(_dot(xcb, wx_ref[...]) + lp_ref[6:7, :])
    lam = lp_ref[7:8, :]
    ls = jnp.minimum(lam, 0.0) - jnp.log(1.0 + jnp.exp(-jnp.abs(lam)))
    log_a = (RG_LRU_C * ls) * ra
    a = jnp.exp(log_a)
    mult = jnp.sqrt(-jnp.tanh(log_a) * (a * a + 1.0))
    return ra, ix, ls, a, mult


def _conv(x, lp_ref, row):
    return (lp_ref[4:5, :] + lp_ref[3:4, :] * x + lp_ref[2:3, :] * _shift_down(x, 1, row)
            + lp_ref[1:2, :] * _shift_down(x, 2, row) + lp_ref[0:1, :] * _shift_down(x, 3, row))


def lru_fwd(proj, lp, wa_t, wx_t, tag, ride=None):
    S = proj.shape[1]
    nblk = S // 8

    def body(x_ref, g_ref, lp_ref, wa_ref, wx_ref, y_ref, xc_ref, h_ref, a_s, b_s):
        x = x_ref[...]
        row = lax.broadcasted_iota(jnp.int32, x.shape, 0)
        xc = _conv(x, lp_ref, row)
        xc_ref[...] = xc
        ra, ix, ls, a, mult = _lru_gates(xc, lp_ref, wa_ref, wx_ref)
        a_s[...] = a
        b_s[...] = mult * (ix * xc)
        rowb = lax.broadcasted_iota(jnp.int32, (8, LC), 0)

        def step(i, carry):
            r0 = pl.multiple_of(i * 8, 8)
            A = a_s[pl.ds(r0, 8), :]
            B = b_s[pl.ds(r0, 8), :]
            for d in (1, 2, 4):
                m = rowb >= d
                As = jnp.where(m, pltpu.roll(A, d, 0), 1.0)
                Bs = jnp.where(m, pltpu.roll(B, d, 0), 0.0)
                B = A * Bs + B
                A = A * As
            H = B + A * carry
            h_ref[pl.ds(r0, 8), :] = H
            return H[7:8, :]

        lax.fori_loop(0, nblk, step, jnp.zeros((1, LC), F32))
        y_ref[...] = h_ref[...] * _gelu(g_ref[...])

    col = pl.BlockSpec((S, LC), lambda c: (0, c))
    return _call(
        body, ride, name=f"lru_fwd_{tag}",
        grid=(LW // LC,),
        in_specs=[pl.BlockSpec((None, S, LC), lambda c: (c // 2, 0, c % 2)),
                  pl.BlockSpec((None, S, LC), lambda c: (2 + c // 2, 0, c % 2)),
                  pl.BlockSpec((8, LC), lambda c: (0, c)),
                  pl.BlockSpec((None, LC, LC), lambda c: (c, 0, 0)),
                  pl.BlockSpec((None, LC, LC), lambda c: (c, 0, 0))],
        out_specs=[col, col, col],
        out_shape=[jax.ShapeDtypeStruct((S, LW), F32)] * 3,
        scratch_shapes=[pltpu.VMEM((S, LC), F32), pltpu.VMEM((S, LC), F32)],
        args=(proj, proj, lp, wa_t, wx_t))


def lru_bwd(dy, proj, xc_all, hst, lp, wa_t, wx_t, tag):
    S = proj.shape[1]
    nblk = S // 8

    def body(dy_ref, x_ref, g_ref, xc_ref, h_ref, lp_ref, wa_ref, wx_ref,
             dx_ref, dg_ref, dlp_ref, dwa_ref, dwx_ref, c_s, l_s):
        xc = xc_ref[...]
        row = lax.broadcasted_iota(jnp.int32, xc.shape, 0)
        ra, ix, ls, a, mult = _lru_gates(xc, lp_ref, wa_ref, wx_ref)
        g = g_ref[...]
        dyv = dy_ref[...]
        h = h_ref[...]
        dg_ref[...] = (dyv * h * _gelu_grad(g)).astype(BF16)
        c_s[...] = _shift_up(a, 1, row, S)
        l_s[...] = dyv * _gelu(g)
        rowb = lax.broadcasted_iota(jnp.int32, (8, LC), 0)

        def step(i, carry):
            r0 = pl.multiple_of((nblk - 1 - i) * 8, 8)
            C = c_s[pl.ds(r0, 8), :]
            L = l_s[pl.ds(r0, 8), :]
            for d in (1, 2, 4):
                m = rowb < 8 - d
                Cs = jnp.where(m, pltpu.roll(C, 8 - d, 0), 1.0)
                Ls = jnp.where(m, pltpu.roll(L, 8 - d, 0), 0.0)
                L = C * Ls + L
                C = C * Cs
            L = L + C * carry
            l_s[pl.ds(r0, 8), :] = L
            return L[0:1, :]

        lax.fori_loop(0, nblk, step, jnp.zeros((1, LC), F32))
        db = l_s[...]
        da = db * _shift_down(h, 1, row)
        ixc = ix * xc
        dmult = db * ixc
        dix = db * (mult * xc)
        dxc = db * (mult * ix)
        dlog_a = da * a - dmult * (a * a) / mult
        dra = dlog_a * (RG_LRU_C * ls)
        dls = _csum(dlog_a * ra) * RG_LRU_C
        lam = lp_ref[7:8, :]
        dlam = dls * _sigmoid(-lam)
        dpa = dra * ra * (1.0 - ra)
        dpx = dix * ix * (1.0 - ix)
        dpab = dpa.astype(BF16)
        dpxb = dpx.astype(BF16)
        xcb = xc.astype(BF16)
        dwa_ref[...] = _dot_tn(xcb, dpab)
        dwx_ref[...] = _dot_tn(xcb, dpxb)
        dxc = dxc + _dot_nt(dpab, wa_ref[...]) + _dot_nt(dpxb, wx_ref[...])
        x = x_ref[...]
        dlp_ref[0:1, :] = _csum(dxc * _shift_down(x, 3, row))
        dlp_ref[1:2, :] = _csum(dxc * _shift_down(x, 2, row))
        dlp_ref[2:3, :] = _csum(dxc * _shift_down(x, 1, row))
        dlp_ref[3:4, :] = _csum(dxc * x)
        dlp_ref[4:5, :] = _csum(dxc)
        dlp_ref[5:6, :] = _csum(dpa)
        dlp_ref[6:7, :] = _csum(dpx)
        dlp_ref[7:8, :] = dlam
        dx = (lp_ref[3:4, :] * dxc + lp_ref[2:3, :] * _shift_up(dxc, 1, row, S)
              + lp_ref[1:2, :] * _shift_up(dxc, 2, row, S) + lp_ref[0:1, :] * _shift_up(dxc, 3, row, S))
        dx_ref[...] = dx.astype(BF16)

    col = pl.BlockSpec((S, LC), lambda c: (0, c))
    pcol = pl.BlockSpec((None, S, LC), lambda c: (c // 2, 0, c % 2))
    return pl.pallas_call(
        body, name=f"lru_bwd_{tag}",
        grid=(LW // LC,),
        in_specs=[col, pcol, pl.BlockSpec((None, S, LC), lambda c: (2 + c // 2, 0, c % 2)), col, col,
                  pl.BlockSpec((8, LC), lambda c: (0, c)),
                  pl.BlockSpec((None, LC, LC), lambda c: (c, 0, 0)),
                  pl.BlockSpec((None, LC, LC), lambda c: (c, 0, 0))],
        out_specs=[pcol, pcol, pl.BlockSpec((8, LC), lambda c: (0, c)),
                   pl.BlockSpec((None, LC, LC), lambda c: (c, 0, 0)),
                   pl.BlockSpec((None, LC, LC), lambda c: (c, 0, 0))],
        out_shape=[jax.ShapeDtypeStruct((2, S, PC), BF16), jax.ShapeDtypeStruct((2, S, PC), BF16),
                   jax.ShapeDtypeStruct((8, LW), F32),
                   jax.ShapeDtypeStruct((LW // LC, LC, LC), F32), jax.ShapeDtypeStruct((LW // LC, LC, LC), F32)],
        scratch_shapes=[pltpu.VMEM((S, LC), F32), pltpu.VMEM((S, LC), F32)],
        compiler_params=_cparams("arbitrary"),
    )(dy, proj, proj, xc_all, hst, lp, wa_t, wx_t)


def _head_stack(zc, lane_head):
    return jnp.concatenate([jnp.where(lane_head == hh, zc, 0.0) for hh in range(HEADS)], axis=0).astype(BF16)


def _gmlp_fwd_parts(u, v, gp_ref, wcat_ref, bz_ref, pavg_ref, ts):
    ug = _gelu(u)
    vg = _gelu(v)
    pavg = pavg_ref[...]
    vc = vg - _seg_mean(vg, pavg)
    rs = lax.rsqrt(_seg_mean(vc * vc, pavg) + EPS)
    vhat = vc * rs
    vh = vhat * gp_ref[0:1, :]
    lane_head = lax.broadcasted_iota(jnp.int32, (CHUNK, GW), 1) // HD
    zs = []
    for n in range(ts // CHUNK):
        stack = _head_stack(vh[n * CHUNK:(n + 1) * CHUNK, :], lane_head)
        zs.append(_dot(wcat_ref[...], stack) + bz_ref[...])
    z = jnp.concatenate(zs, axis=0) if len(zs) > 1 else zs[0]
    return ug, rs, vhat, vh, z


def mix_out_fwd(proj, ylru, x, vec, gp, wcat, bz, pavg, wout_g, tag, ride=None):
    S = x.shape[0]
    ts = min(MIX_TS, S)

    def body(u_ref, v_ref, yl_ref, x_ref, vec_ref, gp_ref, wcat_ref, bz_ref, pavg_ref, wout_ref,
             xo_ref, y_ref, fo_ref):
        u = jnp.concatenate([u_ref[0], u_ref[1]], axis=1)
        v = jnp.concatenate([v_ref[0], v_ref[1]], axis=1)
        ug, _, _, _, z = _gmlp_fwd_parts(u, v, gp_ref, wcat_ref, bz_ref, pavg_ref, ts)
        n1 = _rms(yl_ref[...], gp_ref[1:2, :])
        n2 = _rms(ug * z, gp_ref[2:3, :])
        y = jnp.concatenate([n1, n2], axis=1).astype(BF16)
        y_ref[...] = y
        fo = jnp.zeros((ts, D), F32)
        for k in range(NDEV):
            fo = fo + _dot(y[:, k * OR:(k + 1) * OR], wout_ref[k])
        fo_ref[...] = fo.astype(BF16)
        xo_ref[...] = x_ref[...] + vec_ref[2:3, :] * fo

    row = pl.BlockSpec((ts, D), lambda i: (i, 0))
    full = lambda shp: pl.BlockSpec(shp, lambda i: tuple(0 for _ in shp))
    return _call(
        body, ride, name=f"mix_out_fwd_{tag}",
        grid=(S // ts,),
        in_specs=[pl.BlockSpec((2, ts, PC), lambda i: (2, i, 0)), pl.BlockSpec((2, ts, PC), lambda i: (3, i, 0)),
                  pl.BlockSpec((ts, LW), lambda i: (i, 0)), row, full((8, D)), full((8, GW)),
                  full((CHUNK, HEADS * CHUNK)), full((CHUNK, GW)), full((GW, GW)),
                  pl.BlockSpec((NDEV, OR, D), lambda i: (0, 0, 0))],
        out_specs=[row, row, row],
        out_shape=[jax.ShapeDtypeStruct((S, D), F32), jax.ShapeDtypeStruct((S, D), BF16),
                   jax.ShapeDtypeStruct((S, D), BF16)],
        scratch_shapes=[], args=(proj, proj, ylru, x, vec, gp, wcat, bz, pavg, wout_g))


def mix_out_bwd(dxo, proj, ylru, fo, vec, gp, wcat, wcat_t, bz, pavg, wout_g, tag):
    S = dxo.shape[0]
    ts = min(MIX_TS, S)

    def body(dxo_ref, u_ref, v_ref, yl_ref, fo_ref, vec_ref, gp_ref, wcat_ref, wcatt_ref, bz_ref, pavg_ref,
             wout_ref, dyo_ref, dyl_ref, duv_ref, acc_ref, dgp_ref, dwm_ref, dbz_ref):
        @pl.when(pl.program_id(0) == 0)
        def _():
            acc_ref[...] = jnp.zeros_like(acc_ref)
            dgp_ref[...] = jnp.zeros_like(dgp_ref)
            dwm_ref[...] = jnp.zeros_like(dwm_ref)
            dbz_ref[...] = jnp.zeros_like(dbz_ref)

        dxo_v = dxo_ref[...]
        acc_ref[2:3, :] += _csum(dxo_v * fo_ref[...].astype(F32))
        dyo = (vec_ref[2:3, :] * dxo_v).astype(BF16)
        dyo_ref[...] = dyo
        dn = [_dot_nt(dyo, wout_ref[k]) for k in range(NDEV)]
        dn1 = jnp.concatenate(dn[:NDEV // 2], axis=1)
        dn2 = jnp.concatenate(dn[NDEV // 2:], axis=1)
        dyl, dg1 = _rms_bwd(dn1, yl_ref[...], gp_ref[1:2, :])
        dyl_ref[...] = dyl
        u = jnp.concatenate([u_ref[0], u_ref[1]], axis=1)
        v = jnp.concatenate([v_ref[0], v_ref[1]], axis=1)
        ug, rs, vhat, vh, z = _gmlp_fwd_parts(u, v, gp_ref, wcat_ref, bz_ref, pavg_ref, ts)
        dyg, dg2 = _rms_bwd(dn2, ug * z, gp_ref[2:3, :])
        du = (dyg * z) * _gelu_grad(u)
        dz = dyg * ug
        lane_head = lax.broadcasted_iota(jnp.int32, (CHUNK, GW), 1) // HD
        vhb = vh.astype(BF16)
        dvhs = []
        dbz = jnp.zeros((CHUNK, GW), F32)
        dwm = jnp.zeros((HEADS * CHUNK, CHUNK), F32)
        for n in range(ts // CHUNK):
            dzc = dz[n * CHUNK:(n + 1) * CHUNK, :]
            dbz = dbz + dzc
            stack = _head_stack(dzc, lane_head)
            dwm = dwm + _dot_nt(stack, vhb[n * CHUNK:(n + 1) * CHUNK, :])
            dvhs.append(_dot(wcatt_ref[...], stack))
        dbz_ref[...] += dbz
        dwm_ref[...] += dwm
        dvh = jnp.concatenate(dvhs, axis=0) if len(dvhs) > 1 else dvhs[0]
        pavg = pavg_ref[...]
        dvn = _csum(dvh * vhat)
        dvhat = dvh * gp_ref[0:1, :]
        dvg = rs * (dvhat - _seg_mean(dvhat, pavg) - vhat * _seg_mean(dvhat * vhat, pavg))
        dv = dvg * _gelu_grad(v)
        duv_ref[0] = du[:, :PC].astype(BF16)
        duv_ref[1] = du[:, PC:].astype(BF16)
        duv_ref[2] = dv[:, :PC].astype(BF16)
        duv_ref[3] = dv[:, PC:].astype(BF16)
        dgp_ref[0:1, :] += dvn
        dgp_ref[1:2, :] += dg1
        dgp_ref[2:3, :] += dg2

    row = pl.BlockSpec((ts, D), lambda i: (i, 0))
    full = lambda shp: pl.BlockSpec(shp, lambda i: tuple(0 for _ in shp))
    return pl.pallas_call(
        body, name=f"mix_out_bwd_{tag}",
        grid=(S // ts,),
        in_specs=[row, pl.BlockSpec((2, ts, PC), lambda i: (2, i, 0)), pl.BlockSpec((2, ts, PC), lambda i: (3, i, 0)),
                  pl.BlockSpec((ts, LW), lambda i: (i, 0)), row, full((8, D)), full((8, GW)),
                  full((CHUNK, HEADS * CHUNK)), full((CHUNK, HEADS * CHUNK)), full((CHUNK, GW)), full((GW, GW)),
                  pl.BlockSpec((NDEV, OR, D), lambda i: (0, 0, 0))],
        out_specs=[row, pl.BlockSpec((ts, LW), lambda i: (i, 0)), pl.BlockSpec((4, ts, PC), lambda i: (0, i, 0)),
                   full((8, D)), full((8, GW)), full((HEADS * CHUNK, CHUNK)), full((CHUNK, GW))],
        out_shape=[jax.ShapeDtypeStruct((S, D), BF16), jax.ShapeDtypeStruct((S, LW), F32),
                   jax.ShapeDtypeStruct((4, S, PC), BF16), jax.ShapeDtypeStruct((8, D), F32),
                   jax.ShapeDtypeStruct((8, GW), F32), jax.ShapeDtypeStruct((HEADS * CHUNK, CHUNK), F32),
                   jax.ShapeDtypeStruct((CHUNK, GW), F32)],
        compiler_params=_cparams("arbitrary"),
    )(dxo, proj, proj, ylru, fo, vec, gp, wcat, wcat_t, bz, pavg, wout_g)


def final_loss(x, target, gain):
    S = x.shape[0]
    ts = min(512, S)

    def body(x_ref, t_ref, g_ref, loss_ref, dx_ref, dg_ref):
        @pl.when(pl.program_id(0) == 0)
        def _():
            loss_ref[...] = jnp.zeros_like(loss_ref)
            dg_ref[...] = jnp.zeros_like(dg_ref)

        xv = x_ref[...]
        gain_v = g_ref[0:1, :]
        rstd = lax.rsqrt(_rmean(xv * xv) + EPS)
        xhat = xv * rstd
        err = xhat * gain_v - t_ref[...]
        loss_ref[...] += 0.5 * _csum(_rmean(err * err))
        dy = err * (1.0 / D)
        dg_ref[0:1, :] += _csum(dy * xhat)
        dxhat = dy * gain_v
        dx_ref[...] = rstd * (dxhat - xhat * _rmean(dxhat * xhat))

    row = pl.BlockSpec((ts, D), lambda i: (i, 0))
    return pl.pallas_call(
        body, name="final_loss",
        grid=(S // ts,),
        in_specs=[row, row, pl.BlockSpec((8, D), lambda i: (0, 0))],
        out_specs=[pl.BlockSpec((8, 128), lambda i: (0, 0)), row, pl.BlockSpec((8, D), lambda i: (0, 0))],
        out_shape=[jax.ShapeDtypeStruct((8, 128), F32), jax.ShapeDtypeStruct((S, D), F32),
                   jax.ShapeDtypeStruct((8, D), F32)],
        compiler_params=_cparams("arbitrary"),
    )(x, target, gain)


def _vec(mod_l, j, gain):
    return jnp.concatenate([mod_l[3 * j:3 * j + 3], gain[None, :], jnp.zeros((4, D), F32)], axis=0)


def _block_diag_tiles(w):
    w4 = w.reshape(LW // LC, 2, HD, HD)
    eye2 = jnp.eye(2, dtype=w.dtype)
    return (w4[:, :, :, None, :] * eye2[None, :, None, :, None]).reshape(LW // LC, LC, LC).astype(BF16)


def _block_diag_extract(dw):
    d5 = dw.reshape(LW // LC, 2, HD, 2, HD)
    return jnp.einsum('cihkj,ik->cihj', d5, jnp.eye(2, dtype=dw.dtype)).reshape(HEADS, HD, HD)


def _layer_params(l, p, conv_w_full):
    lp = jnp.concatenate([conv_w_full[l], p['conv_b'][l][None], p['gate_a_b'][l].reshape(1, LW),
                          p['gate_x_b'][l].reshape(1, LW), p['lru_lambda'][l][None]], axis=0)
    gp = jnp.concatenate([p['v_norm'][l][None], p['lru_out_norm'][l][None], p['gmlp_out_norm'][l][None],
                          jnp.zeros((5, GW), F32)], axis=0)
    ws = p['spatial_w'][l] * jnp.tril(jnp.ones((CHUNK, CHUNK), F32))
    wcat = ws.transpose(1, 0, 2).reshape(CHUNK, HEADS * CHUNK).astype(BF16)
    wcat_t = ws.transpose(2, 0, 1).reshape(CHUNK, HEADS * CHUNK).astype(BF16)
    bz = jnp.repeat(p['spatial_b'][l].T, HD, axis=1)
    return dict(lp=lp, gp=gp, wcat=wcat, wcat_t=wcat_t, bz=bz,
                wa_t=_block_diag_tiles(p['gate_a_w'][l]), wx_t=_block_diag_tiles(p['gate_x_w'][l]))


def _pavg():
    return jnp.kron(jnp.eye(HEADS, dtype=F32), jnp.full((HD, HD), 1.0 / HD, F32)).astype(BF16)


GATHER_RIDES = {
    ('ffn_a', 0): [('w_in', 0), ('w_out', 0), ('gu', DEPTH)],
    ('mix_in', 0): [('down', DEPTH)],
    ('lru', 0): [('down', 1)],
    ('mix_out', 0): [('w_in', 1), ('w_out', 1)],
    ('ffn_b', 0): [('gu', 1)],
    ('ffn_a', 1): [('gu', DEPTH + 1), ('down', DEPTH + 1)],
}


def local_fwd_bwd(x, target, mod, p, loc, gathered, conv_w_full):
    pavg = _pavg()
    g = dict(gathered)

    def ride(call, l):
        todo = GATHER_RIDES.get((call, l))
        return None if todo is None else (todo, GatherRide([(loc[kind], slot) for kind, slot in todo]))

    def run(fn, call, l, *args):
        r = ride(call, l)
        outs, got = fn(*args, ride=None if r is None else r[1])
        if r is not None:
            g.update(dict(zip(r[0], got)))
        return outs

    saved = []
    h = x
    for l in range(DEPTH):
        q = _layer_params(l, p, conv_w_full)
        v1 = _vec(mod[l], 0, p['ffn1_norm'][l])
        vm = _vec(mod[l], 1, p['mix_norm'][l])
        v2 = _vec(mod[l], 2, p['ffn2_norm'][l])
        x0 = h
        x1, h1, gu1, f1 = run(ffn_fwd, 'ffn_a', l, x0, v1, g['gu', l], g['down', l], f"a{l}")
        hm, proj = run(mix_in_fwd, 'mix_in', l, x1, vm, g['w_in', l], f"{l}")
        ylru, xc, hst = run(lru_fwd, 'lru', l, proj, q['lp'], q['wa_t'], q['wx_t'], f"{l}")
        x2, y, fo = run(mix_out_fwd, 'mix_out', l, proj, ylru, x1, vm, q['gp'], q['wcat'], q['bz'], pavg,
                        g['w_out', l], f"{l}")
        x3, h2, gu2, f2 = run(ffn_fwd, 'ffn_b', l, x2, v2, g['gu', DEPTH + l], g['down', DEPTH + l], f"b{l}")
        saved.append(dict(q=q, v1=v1, vm=vm, v2=v2, x0=x0, x1=x1, x2=x2, h1=h1, gu1=gu1, f1=f1, hm=hm, proj=proj,
                          ylru=ylru, xc=xc, hst=hst, y=y, fo=fo, h2=h2, gu2=gu2, f2=f2))
        h = x3
    fin = jnp.concatenate([p['final_norm'][None], jnp.zeros((7, D), F32)], axis=0)
    loss8, dx, dfin = final_loss(h, target, fin)
    loss = loss8[0, 0]

    big = dict(gu=None, down=None, w_in=None, w_out=None)
    small = {k: [None] * DEPTH for k in ('ffn1_norm', 'mix_norm', 'ffn2_norm', 'conv_w', 'conv_b', 'gate_a_w',
                                         'gate_a_b', 'gate_x_w', 'gate_x_b', 'lru_lambda', 'v_norm', 'spatial_w',
                                         'spatial_b', 'lru_out_norm', 'gmlp_out_norm')}
    dmod = [None] * DEPTH
    tril = jnp.tril(jnp.ones((CHUNK, CHUNK), F32))
    for l in reversed(range(DEPTH)):
        sv = saved[l]
        q = sv['q']
        dx2, dgu, a, df, acc2 = ffn_bwd(dx, sv['x2'], sv['gu2'], sv['f2'], sv['v2'],
                                        g['gu', DEPTH + l], g['down', DEPTH + l], f"b{l}")
        big['gu'] = tn_matmul_scatter(sv['h2'][None], dgu, DEPTH + l, 2 * DEPTH, big['gu'], f"dw_gu_b{l}")
        big['down'] = tn_matmul_scatter(a, df[None], DEPTH + l, 2 * DEPTH, big['down'], f"dw_down_b{l}", split=2)
        dyo, dylru, duv, accmo, dgp, dwm, dbz = mix_out_bwd(dx2, sv['proj'], sv['ylru'], sv['fo'], sv['vm'], q['gp'],
                                                             q['wcat'], q['wcat_t'], q['bz'], pavg, g['w_out', l], f"{l}")
        big['w_out'] = tn_matmul_scatter(sv['y'][None], dyo[None], l, DEPTH, big['w_out'], f"dw_out_{l}", split=NDEV)
        dxl, dgl, dlp, dwa, dwx = lru_bwd(dylru, sv['proj'], sv['xc'], sv['hst'], q['lp'], q['wa_t'], q['wx_t'], f"{l}")
        dx1, accmi = mix_in_bwd(dxl, dgl, duv, sv['x1'], dx2, sv['vm'], g['w_in', l], f"{l}")
        hm3 = sv['hm'][None]
        win = tn_matmul_scatter(hm3, dxl, l, DEPTH, big['w_in'], f"dw_in_x{l}", koff=0)
        win = tn_matmul_scatter(hm3, dgl, l, DEPTH, win, f"dw_in_g{l}", koff=2)
        big['w_in'] = tn_matmul_scatter(hm3, duv, l, DEPTH, win, f"dw_in_uv{l}", koff=4)
        dx0, dgu, a, df, acc1 = ffn_bwd(dx1, sv['x0'], sv['gu1'], sv['f1'], sv['v1'],
                                        g['gu', l], g['down', l], f"a{l}")
        big['gu'] = tn_matmul_scatter(sv['h1'][None], dgu, l, 2 * DEPTH, big['gu'], f"dw_gu_a{l}")
        big['down'] = tn_matmul_scatter(a, df[None], l, 2 * DEPTH, big['down'], f"dw_down_a{l}", split=2)
        dx = dx0
        dmod[l] = jnp.concatenate([acc1[0:3], accmi[0:2], accmo[2:3], acc2[0:3]], axis=0)
        small['ffn1_norm'][l] = acc1[3]
        small['mix_norm'][l] = accmi[3]
        small['ffn2_norm'][l] = acc2[3]
        small['conv_w'][l] = dlp[0:4]
        small['conv_b'][l] = dlp[4]
        small['gate_a_b'][l] = dlp[5].reshape(HEADS, HD)
        small['gate_x_b'][l] = dlp[6].reshape(HEADS, HD)
        small['lru_lambda'][l] = dlp[7]
        small['gate_a_w'][l] = _block_diag_extract(dwa)
        small['gate_x_w'][l] = _block_diag_extract(dwx)
        small['v_norm'][l] = dgp[0]
        small['lru_out_norm'][l] = dgp[1]
        small['gmlp_out_norm'][l] = dgp[2]
        small['spatial_w'][l] = dwm.reshape(HEADS, CHUNK, CHUNK) * tril
        small['spatial_b'][l] = dbz.reshape(CHUNK, HEADS, HD).sum(-1).T
    small = {k: jnp.stack(v) for k, v in small.items()}
    small['final_norm'] = dfin[0]
    return loss, dx, big, small, jnp.stack(dmod)


def ada_fwd(c_all, w_ada, b_loc):
    def body(c_ref, w_ref, b_ref, mod_ref, sc_ref):
        cv = c_ref[...]
        sc = cv * _sigmoid(cv)
        sc_ref[...] = sc
        mod_ref[...] = _dot3(sc, w_ref[...]) + b_ref[...]

    return pl.pallas_call(
        body, name="ada_fwd",
        grid=(DEPTH,),
        in_specs=[pl.BlockSpec((NDEV, D), lambda l: (0, 0)), pl.BlockSpec((None, D, AC), lambda l: (l, 0, 0)),
                  pl.BlockSpec((None, 1, AC), lambda l: (l, 0, 0))],
        out_specs=[pl.BlockSpec((None, NDEV, AC), lambda l: (l, 0, 0)), pl.BlockSpec((NDEV, D), lambda l: (0, 0))],
        out_shape=[jax.ShapeDtypeStruct((DEPTH, NDEV, AC), F32), jax.ShapeDtypeStruct((NDEV, D), F32)],
        compiler_params=_cparams("arbitrary"),
    )(c_all, w_ada, b_loc)


def ada_bwd(sc_t, dmod_cols):
    def body(sc_ref, dm_ref, g_ref):
        sc = sc_ref[...]
        dm = dm_ref[...]
        acc = sc[:, 0:1] * dm[0:1, :]
        for b in range(1, NDEV):
            acc = acc + sc[:, b:b + 1] * dm[b:b + 1, :]
        g_ref[...] = acc

    return pl.pallas_call(
        body, name="ada_bwd",
        grid=(DEPTH,),
        in_specs=[pl.BlockSpec((D, NDEV), lambda l: (0, 0)), pl.BlockSpec((None, NDEV, AC), lambda l: (l, 0, 0))],
        out_specs=pl.BlockSpec((None, None, D, AC), lambda l: (0, l, 0, 0)),
        out_shape=jax.ShapeDtypeStruct((1, DEPTH, D, AC), F32),
        compiler_params=_cparams("arbitrary"),
    )(sc_t, dmod_cols)


def _row_tile(rows):
    for tr in (512, 384, 352, 256, 128, 64, 32, 16, 8):
        if rows % tr == 0:
            return tr
    return rows


def adamw(gparts, slot0, w, m, v, name):
    P, _, R, C = gparts.shape
    L = w.shape[0]
    tr = _row_tile(R)

    def body(g_ref, w_ref, m_ref, v_ref, go_ref, do_ref, mo_ref, vo_ref):
        g = g_ref[0].astype(F32)
        for p in range(1, P):
            g = g + g_ref[p].astype(F32)
        go_ref[...] = g
        mn = ADAM_B1 * m_ref[...] + (1.0 - ADAM_B1) * g
        vn = ADAM_B2 * v_ref[...] + (1.0 - ADAM_B2) * (g * g)
        mo_ref[...] = mn
        vo_ref[...] = vn
        m_hat = mn / (1.0 - ADAM_B1 ** ADAM_STEP)
        v_hat = vn / (1.0 - ADAM_B2 ** ADAM_STEP)
        do_ref[...] = -ADAM_LR * (m_hat / (jnp.sqrt(v_hat) + ADAM_EPS) + ADAM_WD * w_ref[...])

    blk = pl.BlockSpec((None, tr, C), lambda l, i: (l, i, 0))
    return pl.pallas_call(
        body, name=name,
        grid=(L, R // tr),
        in_specs=[pl.BlockSpec((P, None, tr, C), lambda l, i: (0, slot0 + l, i, 0)), blk, blk, blk],
        out_specs=[blk, blk, blk, blk],
        out_shape=[jax.ShapeDtypeStruct((L, R, C), F32)] * 4,
        compiler_params=_cparams("arbitrary", "arbitrary"),
    )(gparts, w, m, v)


def sum_parts(parts):
    P, R, C = parts.shape

    def body(p_ref, o_ref):
        acc = p_ref[0]
        for p in range(1, P):
            acc = acc + p_ref[p]
        o_ref[...] = acc

    return pl.pallas_call(
        body, name="sum_parts",
        in_specs=[pl.BlockSpec(memory_space=pltpu.VMEM)],
        out_specs=pl.BlockSpec(memory_space=pltpu.VMEM),
        out_shape=jax.ShapeDtypeStruct((R, C), F32),
    )(parts)


WEIGHTS = ['w_ada', 'b_ada', 'ffn1_norm', 'ffn1_w_gu', 'ffn1_w_down', 'mix_norm', 'w_in', 'conv_w', 'conv_b',
           'gate_a_w', 'gate_a_b', 'gate_x_w', 'gate_x_b', 'lru_lambda', 'v_norm', 'spatial_w', 'spatial_b',
           'lru_out_norm', 'gmlp_out_norm', 'w_out', 'ffn2_norm', 'ffn2_w_gu', 'ffn2_w_down', 'final_norm']
PACKED = ['b_ada', 'ffn1_norm', 'mix_norm', 'conv_b', 'gate_a_w', 'gate_a_b', 'gate_x_w', 'gate_x_b', 'lru_lambda',
          'v_norm', 'spatial_w', 'spatial_b', 'lru_out_norm', 'gmlp_out_norm', 'ffn2_norm', 'final_norm', 'conv_w']
PACK_LANES = 128
PACK_ROW_ALIGN = 8 * NDEV


def _pack(d):
    flat = jnp.concatenate([d[k].reshape(-1).astype(F32) for k in PACKED])
    rows = -(-flat.shape[0] // (PACK_LANES * PACK_ROW_ALIGN)) * PACK_ROW_ALIGN
    flat = jnp.concatenate([flat, jnp.zeros((rows * PACK_LANES - flat.shape[0],), F32)])
    return flat.reshape(rows, PACK_LANES)


def _unpack(buf, shapes):
    flat = buf.reshape(-1)
    out, off = {}, 0
    for k in PACKED:
        size = 1
        for s in shapes[k]:
            size *= s
        out[k] = flat[off:off + size].reshape(shapes[k])
        off += size
    return out


def kernel(x, c, w_ada, b_ada, ffn1_norm, ffn1_w_gu, ffn1_w_down, mix_norm, w_in, conv_w, conv_b, gate_a_w, gate_a_b, gate_x_w, gate_x_b, lru_lambda, v_norm, spatial_w, spatial_b, lru_out_norm, gmlp_out_norm, w_out, ffn2_norm, ffn2_w_gu, ffn2_w_down, final_norm, loss_target, m_w_ada, m_b_ada, m_ffn1_norm, m_ffn1_w_gu, m_ffn1_w_down, m_mix_norm, m_w_in, m_conv_w, m_conv_b, m_gate_a_w, m_gate_a_b, m_gate_x_w, m_gate_x_b, m_lru_lambda, m_v_norm, m_spatial_w, m_spatial_b, m_lru_out_norm, m_gmlp_out_norm, m_w_out, m_ffn2_norm, m_ffn2_w_gu, m_ffn2_w_down, m_final_norm, v_w_ada, v_b_ada, v_ffn1_norm, v_ffn1_w_gu, v_ffn1_w_down, v_mix_norm, v_w_in, v_conv_w, v_conv_b, v_gate_a_w, v_gate_a_b, v_gate_x_w, v_gate_x_b, v_lru_lambda, v_v_norm, v_spatial_w, v_spatial_b, v_lru_out_norm, v_gmlp_out_norm, v_w_out, v_ffn2_norm, v_ffn2_w_gu, v_ffn2_w_down, v_final_norm):
    w = dict(w_ada=w_ada, b_ada=b_ada, ffn1_norm=ffn1_norm, ffn1_w_gu=ffn1_w_gu, ffn1_w_down=ffn1_w_down, mix_norm=mix_norm, w_in=w_in, conv_w=conv_w, conv_b=conv_b, gate_a_w=gate_a_w, gate_a_b=gate_a_b, gate_x_w=gate_x_w, gate_x_b=gate_x_b, lru_lambda=lru_lambda, v_norm=v_norm, spatial_w=spatial_w, spatial_b=spatial_b, lru_out_norm=lru_out_norm, gmlp_out_norm=gmlp_out_norm, w_out=w_out, ffn2_norm=ffn2_norm, ffn2_w_gu=ffn2_w_gu, ffn2_w_down=ffn2_w_down, final_norm=final_norm)
    m = dict(w_ada=m_w_ada, b_ada=m_b_ada, ffn1_norm=m_ffn1_norm, ffn1_w_gu=m_ffn1_w_gu, ffn1_w_down=m_ffn1_w_down, mix_norm=m_mix_norm, w_in=m_w_in, conv_w=m_conv_w, conv_b=m_conv_b, gate_a_w=m_gate_a_w, gate_a_b=m_gate_a_b, gate_x_w=m_gate_x_w, gate_x_b=m_gate_x_b, lru_lambda=m_lru_lambda, v_norm=m_v_norm, spatial_w=m_spatial_w, spatial_b=m_spatial_b, lru_out_norm=m_lru_out_norm, gmlp_out_norm=m_gmlp_out_norm, w_out=m_w_out, ffn2_norm=m_ffn2_norm, ffn2_w_gu=m_ffn2_w_gu, ffn2_w_down=m_ffn2_w_down, final_norm=m_final_norm)
    v = dict(w_ada=v_w_ada, b_ada=v_b_ada, ffn1_norm=v_ffn1_norm, ffn1_w_gu=v_ffn1_w_gu, ffn1_w_down=v_ffn1_w_down, mix_norm=v_mix_norm, w_in=v_w_in, conv_w=v_conv_w, conv_b=v_conv_b, gate_a_w=v_gate_a_w, gate_a_b=v_gate_a_b, gate_x_w=v_gate_x_w, gate_x_b=v_gate_x_b, lru_lambda=v_lru_lambda, v_norm=v_v_norm, spatial_w=v_spatial_w, spatial_b=v_spatial_b, lru_out_norm=v_lru_out_norm, gmlp_out_norm=v_gmlp_out_norm, w_out=v_w_out, ffn2_norm=v_ffn2_norm, ffn2_w_gu=v_ffn2_w_gu, ffn2_w_down=v_ffn2_w_down, final_norm=v_final_norm)
    me = 4 * lax.axis_index("x") + 2 * lax.axis_index("y") + lax.axis_index("c")

    loc = dict(gu=jnp.concatenate([ffn1_w_gu, ffn2_w_gu], axis=0).astype(BF16),
               down=jnp.concatenate([ffn1_w_down, ffn2_w_down], axis=0).astype(BF16),
               w_in=w_in.astype(BF16), w_out=w_out.astype(BF16))
    c_g, conv_g, gu0, down0 = all_gather([(c, None), (conv_w, None), (loc['gu'], 0), (loc['down'], 0)], "gather_first")
    conv_w_full = conv_g.transpose(1, 2, 0, 3).reshape(DEPTH, CONV_WIDTH, LW)

    b_loc = lax.dynamic_slice(b_ada, (0, me * AC), (DEPTH, AC)).reshape(DEPTH, 1, AC)
    mod_cols, sc_all = ada_fwd(c_g.reshape(NDEV, D), w_ada, b_loc)
    (mod_rows,) = all_to_all([mod_cols.transpose(1, 0, 2)], "scatter_mod")
    mod = mod_rows.transpose(1, 0, 2).reshape(DEPTH, NMOD, D)

    small_w = {k: w[k] for k in PACKED if k != 'conv_w'}
    loss_loc, dx, big, small_g, dmod = local_fwd_bwd(x[0], loss_target[0], mod, small_w, loc,
                                                     {('gu', 0): gu0, ('down', 0): down0}, conv_w_full)
    loss = lax.psum(loss_loc, ("x", "y", "c"))

    small_g['b_ada'] = dmod.reshape(DEPTH, NMOD * D)
    gpack = _pack(small_g)
    rows = gpack.shape[0]
    dmod_out = dmod.reshape(DEPTH, NDEV, AC).transpose(1, 0, 2)
    dmod_r, pack_r = all_to_all([dmod_out, gpack.reshape(NDEV, rows // NDEV, PACK_LANES)], "scatter_grads")
    (gsum_g,) = all_gather([(sum_parts(pack_r), None)], "gather_small_grads")
    gsum = gsum_g.reshape(1, 1, rows, PACK_LANES)

    res = {}
    res['ffn1_w_gu'] = adamw(big['gu'], 0, w['ffn1_w_gu'], m['ffn1_w_gu'], v['ffn1_w_gu'], "adamw_gu_a")
    res['ffn2_w_gu'] = adamw(big['gu'], DEPTH, w['ffn2_w_gu'], m['ffn2_w_gu'], v['ffn2_w_gu'], "adamw_gu_b")
    res['ffn1_w_down'] = adamw(big['down'], 0, w['ffn1_w_down'], m['ffn1_w_down'], v['ffn1_w_down'], "adamw_down_a")
    res['ffn2_w_down'] = adamw(big['down'], DEPTH, w['ffn2_w_down'], m['ffn2_w_down'], v['ffn2_w_down'], "adamw_down_b")
    res['w_in'] = adamw(big['w_in'], 0, w['w_in'], m['w_in'], v['w_in'], "adamw_w_in")
    res['w_out'] = adamw(big['w_out'], 0, w['w_out'], m['w_out'], v['w_out'], "adamw_w_out")
    g_ada = ada_bwd(sc_all.T, dmod_r.transpose(1, 0, 2))
    res['w_ada'] = adamw(g_ada, 0, w['w_ada'], m['w_ada'], v['w_ada'], "adamw_w_ada")
    shapes = {k: w[k].shape for k in PACKED}
    shapes['conv_w'] = (DEPTH, CONV_WIDTH, LW)
    dummy = jnp.zeros(shapes['conv_w'], F32)
    packs = adamw(gsum, 0, _pack({**small_w, 'conv_w': dummy})[None], _pack({**{k: m[k] for k in small_w}, 'conv_w': dummy})[None],
                  _pack({**{k: v[k] for k in small_w}, 'conv_w': dummy})[None], "adamw_small")
    unpacked = [_unpack(b[0], shapes) for b in packs]
    for k in small_w:
        res[k] = tuple(u[k] for u in unpacked)
    gconv = lax.dynamic_slice(unpacked[0]['conv_w'], (0, 0, me * (LW // NDEV)), (DEPTH, CONV_WIDTH, LW // NDEV))
    cshape = (1, DEPTH * CONV_WIDTH, LW // NDEV)
    rc = adamw(gconv.reshape((1,) + cshape), 0, conv_w.reshape(cshape), m['conv_w'].reshape(cshape),
               v['conv_w'].reshape(cshape), "adamw_conv_w")
    res['conv_w'] = tuple(r.reshape(conv_w.shape) for r in rc)

    return (loss, dx[None], *[res[k][0] for k in WEIGHTS], *[res[k][1] for k in WEIGHTS],
            *[res[k][2] for k in WEIGHTS], *[res[k][3] for k in WEIGHTS])
```

```python
import jax
import jax.numpy as jnp
from jax import lax
from jax.experimental import pallas as pl
from jax.experimental.pallas import tpu as pltpu

F32 = jnp.float32
BF16 = jnp.bfloat16

NDEV = 8
DEPTH = 2
D = 1024
DFF = 2816
FC = 2 * DFF // NDEV
NCHUNK = DFF // FC
DR = DFF // NDEV
LW = 512
GW = 512
HD = 64
HEADS = 8
CHUNK = 128
PC = 2 * (LW + GW) // NDEV
OR = D // NDEV
NMOD = 9
AC = NMOD * D // NDEV
LC = 128
EPS = 1e-6
RG_LRU_C = 8.0
CONV_WIDTH = 4

ADAM_LR = 0.001
ADAM_B1 = 0.9
ADAM_B2 = 0.999
ADAM_EPS = 1e-08
ADAM_WD = 0.01
ADAM_STEP = 10

VMEM_LIMIT_BYTES = 60 * 1024 * 1024
MESH = pl.DeviceIdType.MESH
ANY = pl.BlockSpec(memory_space=pl.ANY)


def _cparams(*sem):
    return pltpu.CompilerParams(dimension_semantics=tuple(sem) if sem else None,
                                vmem_limit_bytes=VMEM_LIMIT_BYTES)


def _dot(a, b):
    return jnp.dot(a, b, preferred_element_type=F32)


def _dot_nt(a, b):
    return lax.dot_general(a, b, (((1,), (1,)), ((), ())), preferred_element_type=F32)


def _dot_tn(a, b):
    return lax.dot_general(a, b, (((0,), (0,)), ((), ())), preferred_element_type=F32)


def _split(a):
    hi = a.astype(BF16)
    lo = (a - hi.astype(F32)).astype(BF16)
    return hi, lo


def _dot3(a, b):
    ah, al = _split(a)
    bh, bl = _split(b)
    return _dot(ah, bh) + (_dot(ah, bl) + _dot(al, bh))


def _csum(a):
    return jnp.sum(a, axis=0, keepdims=True)


def _rmean(a):
    return jnp.mean(a, axis=-1, keepdims=True)


def _sigmoid(a):
    return 1.0 / (1.0 + jnp.exp(-a))


_GELU_K = 0.7978845608028654
_GELU_C = 0.044715


def _gelu(a):
    return 0.5 * a * (1.0 + jnp.tanh(_GELU_K * (a + _GELU_C * a * a * a)))


def _gelu_grad(a):
    t = jnp.tanh(_GELU_K * (a + _GELU_C * a * a * a))
    return 0.5 * (1.0 + t) + 0.5 * a * (1.0 - t * t) * (_GELU_K * (1.0 + 3.0 * _GELU_C * a * a))


def _norm_mod(x, gain, scale, shift):
    rstd = lax.rsqrt(_rmean(x * x) + EPS)
    return (x * rstd * gain) * (1.0 + scale) + shift


def _norm_mod_bwd(dh, x, gain, scale):
    rstd = lax.rsqrt(_rmean(x * x) + EPS)
    xhat = x * rstd
    dshift = _csum(dh)
    dscale = _csum(dh * (xhat * gain))
    dhn = dh * (1.0 + scale)
    dgain = _csum(dhn * xhat)
    dxhat = dhn * gain
    dx = rstd * (dxhat - xhat * _rmean(dxhat * xhat))
    return dx, dshift, dscale, dgain


def _rms(x, gain):
    rstd = lax.rsqrt(_rmean(x * x) + EPS)
    return x * rstd * gain


def _rms_bwd(dy, x, gain):
    rstd = lax.rsqrt(_rmean(x * x) + EPS)
    xhat = x * rstd
    dgain = _csum(dy * xhat)
    dxhat = dy * gain
    return rstd * (dxhat - xhat * _rmean(dxhat * xhat)), dgain


def _seg_mean(a, pavg):
    hi, lo = _split(a)
    return _dot(hi, pavg) + _dot(lo, pavg)


def _fetch_blocks(src_hbm, dst_vmem, sems, rows):
    copies = []
    for k in range(NDEV):
        dst = dst_vmem.at[k] if rows is None else dst_vmem.at[pl.ds(k * rows, rows)]
        copies.append(pltpu.make_async_copy(src_hbm.at[k], dst, sems.at[k]))
    for cp in copies:
        cp.start()
    for cp in copies:
        cp.wait()


def _place():
    return lax.axis_index("x"), lax.axis_index("y"), lax.axis_index("c")


def _slot(p):
    return 4 * p[0] + 2 * p[1] + p[2]


class GatherRide:
    def __init__(self, srcs):
        self.n = len(srcs)
        self.index = [i for _, i in srcs]
        self.args = [a for a, _ in srcs]
        self.out_shape = [jax.ShapeDtypeStruct((NDEV,) + (a.shape if i is None else a.shape[1:]), a.dtype)
                          for a, i in srcs]
        self.scratch = [pltpu.SemaphoreType.DMA((self.n, NDEV - 1)), pltpu.SemaphoreType.DMA((self.n, NDEV - 1)),
                        pltpu.SemaphoreType.DMA((self.n,))]

    def hooks(self, ins, outs, sems):
        send_sems, recv_sems, local_sems = sems
        n = self.n
        x, y, c = _place()
        me, sibling = (x, y, c), (x, y, 1 - c)
        chips = [(1 - x, y), (x, 1 - y), (1 - x, 1 - y)]

        def local(a):
            return ins[a] if self.index[a] is None else ins[a].at[self.index[a]]

        def copy(a, k, block, to, src=None):
            dst = outs[a].at[_slot(block)]
            return pltpu.make_async_remote_copy(
                src_ref=dst if src is None else src, dst_ref=dst,
                send_sem=send_sems.at[a, k], recv_sem=recv_sems.at[a, k],
                device_id=to, device_id_type=MESH)

        def mine():
            return [pltpu.make_async_copy(local(a), outs[a].at[_slot(me)], local_sems.at[a]) for a in range(n)]

        def first():
            cps = []
            for a in range(n):
                cps.append(copy(a, 0, me, sibling, src=local(a)))
                cps += [copy(a, 1 + j, me, (*chip, c), src=local(a)) for j, chip in enumerate(chips)]
            return cps

        def passed():
            return [copy(a, 4 + j, (*chip, c), sibling) for j, chip in enumerate(chips) for a in range(n)]

        def start():
            for cp in mine() + first():
                cp.start()

        def mid():
            for j, chip in enumerate(chips):
                for a in range(n):
                    copy(a, 1 + j, (*chip, c), me).wait_recv()
                    copy(a, 4 + j, (*chip, c), sibling).start()

        def finish():
            for a in range(n):
                copy(a, 0, sibling, me).wait_recv()
                for j, chip in enumerate(chips):
                    copy(a, 4 + j, (*chip, 1 - c), me).wait_recv()
            for cp in first() + passed():
                cp.wait_send()
            for cp in mine():
                cp.wait()

        return start, mid, finish


def all_gather(srcs, name):
    ride = GatherRide(srcs)
    n = ride.n

    def body(*refs):
        start, mid, finish = ride.hooks(refs[:n], refs[n:2 * n], refs[2 * n:])
        start()
        mid()
        finish()

    return pl.pallas_call(
        body, name=name,
        in_specs=[ANY] * n, out_specs=[ANY] * n, out_shape=ride.out_shape, scratch_shapes=ride.scratch,
    )(*ride.args)


def _call(core, ride, *, name, grid, in_specs, out_specs, out_shape, scratch_shapes, args):
    if ride is None:
        outs = pl.pallas_call(core, name=name, grid=grid, in_specs=in_specs, out_specs=out_specs,
                              out_shape=out_shape, scratch_shapes=scratch_shapes,
                              compiler_params=_cparams("arbitrary"))(*args)
        return outs, []
    n_in, n_out, n_sc, n = len(in_specs), len(out_shape), len(scratch_shapes), ride.n
    nsteps = grid[0]
    mid_step = max(nsteps - 2, 0)

    def body(*refs):
        cuts = [n_in, n_in + n, n_in + n + n_out, n_in + 2 * n + n_out, n_in + 2 * n + n_out + n_sc]
        ci, ri, co, ro, cs, rs = (refs[a:b] for a, b in zip([0] + cuts, cuts + [len(refs)]))
        start, mid, finish = ride.hooks(ri, ro, rs)
        i = pl.program_id(0)
        pl.when(i == 0)(start)
        core(*ci, *co, *cs)
        pl.when(i == mid_step)(mid)
        pl.when(i == nsteps - 1)(finish)

    outs = pl.pallas_call(
        body, name=name, grid=grid,
        in_specs=list(in_specs) + [ANY] * n, out_specs=list(out_specs) + [ANY] * n,
        out_shape=list(out_shape) + ride.out_shape, scratch_shapes=list(scratch_shapes) + ride.scratch,
        compiler_params=_cparams("arbitrary"))(*args, *ride.args)
    return outs[:n_out], outs[n_out:]


def all_to_all(arrs, name):
    n = len(arrs)

    def body(*refs):
        ins, outs = refs[:n], refs[n:2 * n]
        send_sems, recv_sems, local_sems = refs[2 * n:]
        x, y, c = _place()
        me = (x, y, c)

        def peer(k):
            return (1 - x if k & 4 else x, 1 - y if k & 2 else y, 1 - c if k & 1 else c)

        def copy(a, k):
            return pltpu.make_async_remote_copy(
                src_ref=ins[a].at[_slot(peer(k))], dst_ref=outs[a].at[_slot(me)],
                send_sem=send_sems.at[a, k - 1], recv_sem=recv_sems.at[a, k - 1],
                device_id=peer(k), device_id_type=MESH)

        def landing(a, k):
            return pltpu.make_async_remote_copy(
                src_ref=outs[a].at[_slot(peer(k))], dst_ref=outs[a].at[_slot(peer(k))],
                send_sem=send_sems.at[a, k - 1], recv_sem=recv_sems.at[a, k - 1],
                device_id=me, device_id_type=MESH)

        mine = [pltpu.make_async_copy(ins[a].at[_slot(me)], outs[a].at[_slot(me)], local_sems.at[a]) for a in range(n)]
        for cp in mine:
            cp.start()
        sends = [copy(a, k) for a in range(n) for k in range(1, NDEV)]
        for cp in sends:
            cp.start()
        for a in range(n):
            for k in range(1, NDEV):
                landing(a, k).wait_recv()
        for cp in sends:
            cp.wait_send()
        for cp in mine:
            cp.wait()

    return pl.pallas_call(
        body, name=name,
        in_specs=[ANY] * n, out_specs=[ANY] * n,
        out_shape=[jax.ShapeDtypeStruct(a.shape, a.dtype) for a in arrs],
        scratch_shapes=[pltpu.SemaphoreType.DMA((n, NDEV - 1)), pltpu.SemaphoreType.DMA((n, NDEV - 1)),
                        pltpu.SemaphoreType.DMA((n,))],
    )(*arrs)


FFN_TS = 256


def ffn_fwd(x, vec, wgu_g, wdown_g, tag, ride=None):
    S = x.shape[0]
    ts = min(FFN_TS, S)

    def body(x_ref, vec_ref, wgu_hbm, wd_hbm, xo_ref, h_ref, gu_ref, f_ref, wgu_v, wd_v, sems):
        @pl.when(pl.program_id(0) == 0)
        def _():
            _fetch_blocks(wgu_hbm, wgu_v, sems.at[0], None)
            _fetch_blocks(wd_hbm, wd_v, sems.at[1], DR)

        xv = x_ref[...]
        h = _norm_mod(xv, vec_ref[3:4, :], vec_ref[1:2, :], vec_ref[0:1, :]).astype(BF16)
        h_ref[...] = h
        acc = jnp.zeros((ts, D), F32)
        for j in range(NCHUNK):
            g = _dot(h, wgu_v[j])
            u = _dot(h, wgu_v[NCHUNK + j])
            gu_ref[j] = g.astype(BF16)
            gu_ref[NCHUNK + j] = u.astype(BF16)
            a = (g * _sigmoid(g) * u).astype(BF16)
            acc = acc + _dot(a, wd_v[pl.ds(j * FC, FC), :])
        f_ref[...] = acc.astype(BF16)
        xo_ref[...] = xv + (0.5 * vec_ref[2:3, :]) * acc

    return _call(
        body, ride, name=f"ffn_fwd_{tag}",
        grid=(S // ts,),
        in_specs=[pl.BlockSpec((ts, D), lambda i: (i, 0)),
                  pl.BlockSpec((8, D), lambda i: (0, 0)), ANY, ANY],
        out_specs=[pl.BlockSpec((ts, D), lambda i: (i, 0)),
                   pl.BlockSpec((ts, D), lambda i: (i, 0)),
                   pl.BlockSpec((NDEV, ts, FC), lambda i: (0, i, 0)),
                   pl.BlockSpec((ts, D), lambda i: (i, 0))],
        out_shape=[jax.ShapeDtypeStruct((S, D), F32), jax.ShapeDtypeStruct((S, D), BF16),
                   jax.ShapeDtypeStruct((NDEV, S, FC), BF16), jax.ShapeDtypeStruct((S, D), BF16)],
        scratch_shapes=[pltpu.VMEM((NDEV, D, FC), BF16), pltpu.VMEM((DFF, D), BF16),
                        pltpu.SemaphoreType.DMA((2, NDEV))],
        args=(x, vec, wgu_g, wdown_g))


def ffn_bwd(dxo, x, gu, f, vec, wgu_g, wdown_g, tag):
    S = x.shape[0]
    ts = min(FFN_TS, S)

    def body(dxo_ref, x_ref, gu_ref, f_ref, vec_ref, wgu_hbm, wd_hbm,
             dx_ref, dgu_ref, a_ref, df_ref, acc_ref, wgu_v, wd_v, sems):
        @pl.when(pl.program_id(0) == 0)
        def _():
            _fetch_blocks(wgu_hbm, wgu_v, sems.at[0], None)
            _fetch_blocks(wd_hbm, wd_v, sems.at[1], DR)
            acc_ref[...] = jnp.zeros_like(acc_ref)

        dxo_v = dxo_ref[...]
        dgate = 0.5 * _csum(dxo_v * f_ref[...].astype(F32))
        df = ((0.5 * vec_ref[2:3, :]) * dxo_v).astype(BF16)
        df_ref[...] = df
        dh = jnp.zeros((ts, D), F32)
        for j in range(NCHUNK):
            da = _dot_nt(df, wd_v[pl.ds(j * FC, FC), :])
            g = gu_ref[j].astype(F32)
            u = gu_ref[NCHUNK + j].astype(F32)
            sg = _sigmoid(g)
            si = g * sg
            a_ref[j] = (si * u).astype(BF16)
            dg = (da * u * (sg * (1.0 + g * (1.0 - sg)))).astype(BF16)
            du = (da * si).astype(BF16)
            dgu_ref[j] = dg
            dgu_ref[NCHUNK + j] = du
            dh = dh + _dot_nt(dg, wgu_v[j]) + _dot_nt(du, wgu_v[NCHUNK + j])
        dx, dshift, dscale, dgain = _norm_mod_bwd(dh, x_ref[...], vec_ref[3:4, :], vec_ref[1:2, :])
        dx_ref[...] = dx + dxo_v
        acc_ref[0:1, :] += dshift
        acc_ref[1:2, :] += dscale
        acc_ref[2:3, :] += dgate
        acc_ref[3:4, :] += dgain

    row = pl.BlockSpec((ts, D), lambda i: (i, 0))
    return pl.pallas_call(
        body, name=f"ffn_bwd_{tag}",
        grid=(S // ts,),
        in_specs=[row, row, pl.BlockSpec((NDEV, ts, FC), lambda i: (0, i, 0)), row,
                  pl.BlockSpec((8, D), lambda i: (0, 0)), ANY, ANY],
        out_specs=[row, pl.BlockSpec((NDEV, ts, FC), lambda i: (0, i, 0)),
                   pl.BlockSpec((NCHUNK, ts, FC), lambda i: (0, i, 0)), row,
                   pl.BlockSpec((8, D), lambda i: (0, 0))],
        out_shape=[jax.ShapeDtypeStruct((S, D), F32), jax.ShapeDtypeStruct((NDEV, S, FC), BF16),
                   jax.ShapeDtypeStruct((NCHUNK, S, FC), BF16), jax.ShapeDtypeStruct((S, D), BF16),
                   jax.ShapeDtypeStruct((8, D), F32)],
        scratch_shapes=[pltpu.VMEM((NDEV, D, FC), BF16), pltpu.VMEM((DFF, D), BF16),
                        pltpu.SemaphoreType.DMA((2, NDEV))],
        compiler_params=_cparams("arbitrary"),
    )(dxo, x, gu, f, vec, wgu_g, wdown_g)


NCHIP = NDEV // 2


def tn_matmul_scatter(me_arr, a, b, slot, nslots, prev, name, split=1):
    na, S, M = a.shape
    nb, _, N = b.shape
    ncall = NDEV // split
    ts = min(512, S)
    nsteps = S // ts
    mp = M // split
    other_step = {1: lambda j: 2 * j, 2: lambda j: j, 8: lambda j: 0}[split]
    mine_step = {1: lambda j: 2 * j + 1, 2: lambda j: j, 8: lambda j: 0}[split]

    def group(k, me_ref):
        if split == 1:
            return jnp.bitwise_xor(me_ref[0], jnp.bitwise_xor(k, 1))
        if split == 2:
            return jnp.bitwise_xor(me_ref[0] // 2, k)
        return 0

    def body(me_ref, *refs):
        a_ref, b_ref = refs[0], refs[1]
        recv_ref, acc, sb_other, sb_mine, land, d2d_send, d2d_recv, ici_send, ici_recv = refs[-9:]
        k = pl.program_id(0)
        s = pl.program_id(1)
        x, y, c = _place()
        my_chip = 2 * x + y

        def chip_of(j):
            if split == 8:
                cx, cy = j // 2, j % 2
            else:
                cx, cy = (1 - x if j & 2 else x), (1 - y if j & 1 else y)
            return cx, cy, 2 * cx + cy

        def piece(j, core):
            if split == 1:
                return acc[...]
            start = core * mp if split == 2 else (2 * j + core) * mp
            return acc[pl.ds(pl.multiple_of(start, 8), mp), :]

        def to_sibling(j):
            return pltpu.make_async_remote_copy(
                src_ref=sb_other.at[j], dst_ref=land.at[j], send_sem=d2d_send.at[j], recv_sem=d2d_recv.at[j],
                device_id=(x, y, 1 - c), device_id_type=MESH)

        def to_owner(j):
            cx, cy, ci = chip_of(j)
            dst = recv_ref.at[my_chip, slot]
            return ci, pltpu.make_async_copy(sb_mine.at[j], dst, ici_send.at[j]), pltpu.make_async_remote_copy(
                src_ref=sb_mine.at[j], dst_ref=dst, send_sem=ici_send.at[j], recv_sem=ici_recv.at[my_chip],
                device_id=(cx, cy, c), device_id_type=MESH)

        @pl.when(s == 0)
        def _():
            acc[...] = jnp.zeros_like(acc)

        acc[...] += _dot_tn(a_ref[...], b_ref[...])

        for kk in range(ncall):
            @pl.when((s == nsteps - 1) & (k == kk))
            def _():
                for j in range(NCHIP):
                    if other_step(j) == kk:
                        sb_other[j] = piece(j, 1 - c).astype(BF16)
                        to_sibling(j).start()
                for j in range(NCHIP):
                    if mine_step(j) == kk:
                        to_sibling(j).wait_recv()
                        sb_mine[j] = (piece(j, c) + land[j].astype(F32)).astype(BF16)
                        ci, loc, rem = to_owner(j)
                        pl.when(ci == my_chip)(loc.start)
                        pl.when(ci != my_chip)(rem.start)

        @pl.when((s == nsteps - 1) & (k == ncall - 1))
        def _():
            for j in range(NCHIP):
                to_sibling(j).wait_send()
                ci, loc, rem = to_owner(j)
                pl.when(ci == my_chip)(loc.wait)
                pl.when(ci != my_chip)(rem.wait_send)
            for src in range(NCHIP):
                @pl.when(my_chip != src)
                def _():
                    pltpu.make_async_remote_copy(
                        src_ref=recv_ref.at[src, slot], dst_ref=recv_ref.at[src, slot],
                        send_sem=ici_send.at[src], recv_sem=ici_recv.at[src],
                        device_id=(src // 2, src % 2, c), device_id_type=MESH).wait_recv()

    in_specs = [pl.BlockSpec((None, ts, M), (lambda k, s, me: (group(k, me), s, 0)) if na > 1 else (lambda k, s, me: (0, s, 0))),
                pl.BlockSpec((None, ts, N), (lambda k, s, me: (group(k, me), s, 0)) if nb > 1 else (lambda k, s, me: (0, s, 0)))]
    args = [me_arr, a, b]
    aliases = {}
    if prev is not None:
        in_specs.append(ANY)
        args.append(prev)
        aliases = {3: 0}
    return pl.pallas_call(
        body, name=name,
        grid_spec=pltpu.PrefetchScalarGridSpec(
            num_scalar_prefetch=1, grid=(ncall, nsteps), in_specs=in_specs, out_specs=ANY,
            scratch_shapes=[pltpu.VMEM((M, N), F32), pltpu.VMEM((NCHIP, mp, N), BF16), pltpu.VMEM((NCHIP, mp, N), BF16),
                            pltpu.VMEM((NCHIP, mp, N), BF16), pltpu.SemaphoreType.DMA((NCHIP,)),
                            pltpu.SemaphoreType.DMA((NCHIP,)), pltpu.SemaphoreType.DMA((NCHIP,)),
                            pltpu.SemaphoreType.DMA((NCHIP,))]),
        out_shape=jax.ShapeDtypeStruct((NCHIP, nslots, mp, N), BF16),
        input_output_aliases=aliases,
        compiler_params=_cparams("arbitrary", "arbitrary"),
    )(*args)


MIX_TS = 256


def mix_in_fwd(x, vec, win_g, tag, ride=None):
    S = x.shape[0]
    ts = min(MIX_TS, S)

    def body(x_ref, vec_ref, win_ref, hm_ref, proj_ref):
        h = _norm_mod(x_ref[...], vec_ref[3:4, :], vec_ref[1:2, :], vec_ref[0:1, :]).astype(BF16)
        hm_ref[...] = h
        for k in range(NDEV):
            proj_ref[k] = _dot(h, win_ref[k])

    return _call(
        body, ride, name=f"mix_in_fwd_{tag}",
        grid=(S // ts,),
        in_specs=[pl.BlockSpec((ts, D), lambda i: (i, 0)), pl.BlockSpec((8, D), lambda i: (0, 0)),
                  pl.BlockSpec((NDEV, D, PC), lambda i: (0, 0, 0))],
        out_specs=[pl.BlockSpec((ts, D), lambda i: (i, 0)),
                   pl.BlockSpec((NDEV, ts, PC), lambda i: (0, i, 0))],
        out_shape=[jax.ShapeDtypeStruct((S, D), BF16), jax.ShapeDtypeStruct((NDEV, S, PC), F32)],
        scratch_shapes=[], args=(x, vec, win_g))


def mix_in_bwd(dproj, x, dxo, vec, win_g, tag):
    S = x.shape[0]
    ts = min(MIX_TS, S)

    def body(dp_ref, x_ref, dxo_ref, vec_ref, win_ref, dx_ref, acc_ref):
        @pl.when(pl.program_id(0) == 0)
        def _():
            acc_ref[...] = jnp.zeros_like(acc_ref)

        dh = jnp.zeros((ts, D), F32)
        for k in range(NDEV):
            dh = dh + _dot_nt(dp_ref[k], win_ref[k])
        dx, dshift, dscale, dgain = _norm_mod_bwd(dh, x_ref[...], vec_ref[3:4, :], vec_ref[1:2, :])
        dx_ref[...] = dx + dxo_ref[...]
        acc_ref[0:1, :] += dshift
        acc_ref[1:2, :] += dscale
        acc_ref[3:4, :] += dgain

    row = pl.BlockSpec((ts, D), lambda i: (i, 0))
    return pl.pallas_call(
        body, name=f"mix_in_bwd_{tag}",
        grid=(S // ts,),
        in_specs=[pl.BlockSpec((NDEV, ts, PC), lambda i: (0, i, 0)), row, row,
                  pl.BlockSpec((8, D), lambda i: (0, 0)),
                  pl.BlockSpec((NDEV, D, PC), lambda i: (0, 0, 0))],
        out_specs=[row, pl.BlockSpec((8, D), lambda i: (0, 0))],
        out_shape=[jax.ShapeDtypeStruct((S, D), F32), jax.ShapeDtypeStruct((8, D), F32)],
        compiler_params=_cparams("arbitrary"),
    )(dproj, x, dxo, vec, win_g)


def _shift_down(z, k, row):
    return jnp.where(row >= k, pltpu.roll(z, k, 0), 0.0)


def _shift_up(z, k, row, n):
    return jnp.where(row < n - k, pltpu.roll(z, n - k, 0), 0.0)


def _lru_gates(xc, lp_ref, wa_ref, wx_ref):
    xcb = xc.astype(BF16)
    ra = _sigmoid(_dot(xcb, wa_ref[...]) + lp_ref[5:6, :])
    ix = _sigmoid(_dot(xcb, wx_ref[...]) + lp_ref[6:7, :])
    lam = lp_ref[7:8, :]
    ls = jnp.minimum(lam, 0.0) - jnp.log(1.0 + jnp.exp(-jnp.abs(lam)))
    log_a = (RG_LRU_C * ls) * ra
    a = jnp.exp(log_a)
    mult = jnp.sqrt(-jnp.tanh(log_a) * (a * a + 1.0))
    return ra, ix, ls, a, mult


def _conv(x, lp_ref, row):
    return (lp_ref[4:5, :] + lp_ref[3:4, :] * x + lp_ref[2:3, :] * _shift_down(x, 1, row)
            + lp_ref[1:2, :] * _shift_down(x, 2, row) + lp_ref[0:1, :] * _shift_down(x, 3, row))


def lru_fwd(proj, lp, wa_t, wx_t, tag, ride=None):
    S = proj.shape[1]
    nblk = S // 8

    def body(x_ref, g_ref, lp_ref, wa_ref, wx_ref, y_ref, xc_ref, h_ref, a_s, b_s):
        x = x_ref[...]
        row = lax.broadcasted_iota(jnp.int32, x.shape, 0)
        xc = _conv(x, lp_ref, row)
        xc_ref[...] = xc
        ra, ix, ls, a, mult = _lru_gates(xc, lp_ref, wa_ref, wx_ref)
        a_s[...] = a
        b_s[...] = mult * (ix * xc)
        rowb = lax.broadcasted_iota(jnp.int32, (8, LC), 0)

        def step(i, carry):
            r0 = pl.multiple_of(i * 8, 8)
            A = a_s[pl.ds(r0, 8), :]
            B = b_s[pl.ds(r0, 8), :]
            for d in (1, 2, 4):
                m = rowb >= d
                As = jnp.where(m, pltpu.roll(A, d, 0), 1.0)
                Bs = jnp.where(m, pltpu.roll(B, d, 0), 0.0)
                B = A * Bs + B
                A = A * As
            H = B + A * carry
            h_ref[pl.ds(r0, 8), :] = H
            return H[7:8, :]

        lax.fori_loop(0, nblk, step, jnp.zeros((1, LC), F32))
        y_ref[...] = h_ref[...] * _gelu(g_ref[...])

    col = pl.BlockSpec((S, LC), lambda c: (0, c))
    return _call(
        body, ride, name=f"lru_fwd_{tag}",
        grid=(LW // LC,),
        in_specs=[pl.BlockSpec((None, S, LC), lambda c: (c // 2, 0, c % 2)),
                  pl.BlockSpec((None, S, LC), lambda c: (2 + c // 2, 0, c % 2)),
                  pl.BlockSpec((8, LC), lambda c: (0, c)),
                  pl.BlockSpec((None, LC, LC), lambda c: (c, 0, 0)),
                  pl.BlockSpec((None, LC, LC), lambda c: (c, 0, 0))],
        out_specs=[col, col, col],
        out_shape=[jax.ShapeDtypeStruct((S, LW), F32)] * 3,
        scratch_shapes=[pltpu.VMEM((S, LC), F32), pltpu.VMEM((S, LC), F32)],
        args=(proj, proj, lp, wa_t, wx_t))


def lru_bwd(dy, proj, xc_all, hst, lp, wa_t, wx_t, tag):
    S = proj.shape[1]
    nblk = S // 8

    def body(dy_ref, x_ref, g_ref, xc_ref, h_ref, lp_ref, wa_ref, wx_ref,
             dx_ref, dg_ref, dlp_ref, dwa_ref, dwx_ref, c_s, l_s):
        xc = xc_ref[...]
        row = lax.broadcasted_iota(jnp.int32, xc.shape, 0)
        ra, ix, ls, a, mult = _lru_gates(xc, lp_ref, wa_ref, wx_ref)
        g = g_ref[...]
        dyv = dy_ref[...]
        h = h_ref[...]
        dg_ref[...] = (dyv * h * _gelu_grad(g)).astype(BF16)
        c_s[...] = _shift_up(a, 1, row, S)
        l_s[...] = dyv * _gelu(g)
        rowb = lax.broadcasted_iota(jnp.int32, (8, LC), 0)

        def step(i, carry):
            r0 = pl.multiple_of((nblk - 1 - i) * 8, 8)
            C = c_s[pl.ds(r0, 8), :]
            L = l_s[pl.ds(r0, 8), :]
            for d in (1, 2, 4):
                m = rowb < 8 - d
                Cs = jnp.where(m, pltpu.roll(C, 8 - d, 0), 1.0)
                Ls = jnp.where(m, pltpu.roll(L, 8 - d, 0), 0.0)
                L = C * Ls + L
                C = C * Cs
            L = L + C * carry
            l_s[pl.ds(r0, 8), :] = L
            return L[0:1, :]

        lax.fori_loop(0, nblk, step, jnp.zeros((1, LC), F32))
        db = l_s[...]
        da = db * _shift_down(h, 1, row)
        ixc = ix * xc
        dmult = db * ixc
        dix = db * (mult * xc)
        dxc = db * (mult * ix)
        dlog_a = da * a - dmult * (a * a) / mult
        dra = dlog_a * (RG_LRU_C * ls)
        dls = _csum(dlog_a * ra) * RG_LRU_C
        lam = lp_ref[7:8, :]
        dlam = dls * _sigmoid(-lam)
        dpa = dra * ra * (1.0 - ra)
        dpx = dix * ix * (1.0 - ix)
        dpab = dpa.astype(BF16)
        dpxb = dpx.astype(BF16)
        xcb = xc.astype(BF16)
        dwa_ref[...] = _dot_tn(xcb, dpab)
        dwx_ref[...] = _dot_tn(xcb, dpxb)
        dxc = dxc + _dot_nt(dpab, wa_ref[...]) + _dot_nt(dpxb, wx_ref[...])
        x = x_ref[...]
        dlp_ref[0:1, :] = _csum(dxc * _shift_down(x, 3, row))
        dlp_ref[1:2, :] = _csum(dxc * _shift_down(x, 2, row))
        dlp_ref[2:3, :] = _csum(dxc * _shift_down(x, 1, row))
        dlp_ref[3:4, :] = _csum(dxc * x)
        dlp_ref[4:5, :] = _csum(dxc)
        dlp_ref[5:6, :] = _csum(dpa)
        dlp_ref[6:7, :] = _csum(dpx)
        dlp_ref[7:8, :] = dlam
        dx = (lp_ref[3:4, :] * dxc + lp_ref[2:3, :] * _shift_up(dxc, 1, row, S)
              + lp_ref[1:2, :] * _shift_up(dxc, 2, row, S) + lp_ref[0:1, :] * _shift_up(dxc, 3, row, S))
        dx_ref[...] = dx.astype(BF16)

    col = pl.BlockSpec((S, LC), lambda c: (0, c))
    pcol = pl.BlockSpec((None, S, LC), lambda c: (c // 2, 0, c % 2))
    return pl.pallas_call(
        body, name=f"lru_bwd_{tag}",
        grid=(LW // LC,),
        in_specs=[col, pcol, pl.BlockSpec((None, S, LC), lambda c: (2 + c // 2, 0, c % 2)), col, col,
                  pl.BlockSpec((8, LC), lambda c: (0, c)),
                  pl.BlockSpec((None, LC, LC), lambda c: (c, 0, 0)),
                  pl.BlockSpec((None, LC, LC), lambda c: (c, 0, 0))],
        out_specs=[pcol, pcol, pl.BlockSpec((8, LC), lambda c: (0, c)),
                   pl.BlockSpec((None, LC, LC), lambda c: (c, 0, 0)),
                   pl.BlockSpec((None, LC, LC), lambda c: (c, 0, 0))],
        out_shape=[jax.ShapeDtypeStruct((2, S, PC), BF16), jax.ShapeDtypeStruct((2, S, PC), BF16),
                   jax.ShapeDtypeStruct((8, LW), F32),
                   jax.ShapeDtypeStruct((LW // LC, LC, LC), F32), jax.ShapeDtypeStruct((LW // LC, LC, LC), F32)],
        scratch_shapes=[pltpu.VMEM((S, LC), F32), pltpu.VMEM((S, LC), F32)],
        compiler_params=_cparams("arbitrary"),
    )(dy, proj, proj, xc_all, hst, lp, wa_t, wx_t)


def _head_stack(zc, lane_head):
    return jnp.concatenate([jnp.where(lane_head == hh, zc, 0.0) for hh in range(HEADS)], axis=0).astype(BF16)


def _gmlp_fwd_parts(u, v, gp_ref, wcat_ref, bz_ref, pavg_ref, ts):
    ug = _gelu(u)
    vg = _gelu(v)
    pavg = pavg_ref[...]
    vc = vg - _seg_mean(vg, pavg)
    rs = lax.rsqrt(_seg_mean(vc * vc, pavg) + EPS)
    vhat = vc * rs
    vh = vhat * gp_ref[0:1, :]
    lane_head = lax.broadcasted_iota(jnp.int32, (CHUNK, GW), 1) // HD
    zs = []
    for n in range(ts // CHUNK):
        stack = _head_stack(vh[n * CHUNK:(n + 1) * CHUNK, :], lane_head)
        zs.append(_dot(wcat_ref[...], stack) + bz_ref[...])
    z = jnp.concatenate(zs, axis=0) if len(zs) > 1 else zs[0]
    return ug, rs, vhat, vh, z


def mix_out_fwd(proj, ylru, x, vec, gp, wcat, bz, pavg, wout_g, tag, ride=None):
    S = x.shape[0]
    ts = min(MIX_TS, S)

    def body(u_ref, v_ref, yl_ref, x_ref, vec_ref, gp_ref, wcat_ref, bz_ref, pavg_ref, wout_ref,
             xo_ref, y_ref, fo_ref):
        u = jnp.concatenate([u_ref[0], u_ref[1]], axis=1)
        v = jnp.concatenate([v_ref[0], v_ref[1]], axis=1)
        ug, _, _, _, z = _gmlp_fwd_parts(u, v, gp_ref, wcat_ref, bz_ref, pavg_ref, ts)
        n1 = _rms(yl_ref[...], gp_ref[1:2, :])
        n2 = _rms(ug * z, gp_ref[2:3, :])
        y = jnp.concatenate([n1, n2], axis=1).astype(BF16)
        y_ref[...] = y
        fo = jnp.zeros((ts, D), F32)
        for k in range(NDEV):
            fo = fo + _dot(y[:, k * OR:(k + 1) * OR], wout_ref[k])
        fo_ref[...] = fo.astype(BF16)
        xo_ref[...] = x_ref[...] + vec_ref[2:3, :] * fo

    row = pl.BlockSpec((ts, D), lambda i: (i, 0))
    full = lambda shp: pl.BlockSpec(shp, lambda i: tuple(0 for _ in shp))
    return _call(
        body, ride, name=f"mix_out_fwd_{tag}",
        grid=(S // ts,),
        in_specs=[pl.BlockSpec((2, ts, PC), lambda i: (2, i, 0)), pl.BlockSpec((2, ts, PC), lambda i: (3, i, 0)),
                  pl.BlockSpec((ts, LW), lambda i: (i, 0)), row, full((8, D)), full((8, GW)),
                  full((CHUNK, HEADS * CHUNK)), full((CHUNK, GW)), full((GW, GW)),
                  pl.BlockSpec((NDEV, OR, D), lambda i: (0, 0, 0))],
        out_specs=[row, row, row],
        out_shape=[jax.ShapeDtypeStruct((S, D), F32), jax.ShapeDtypeStruct((S, D), BF16),
                   jax.ShapeDtypeStruct((S, D), BF16)],
        scratch_shapes=[], args=(proj, proj, ylru, x, vec, gp, wcat, bz, pavg, wout_g))


def mix_out_bwd(dxo, proj, ylru, fo, vec, gp, wcat, wcat_t, bz, pavg, wout_g, tag):
    S = dxo.shape[0]
    ts = min(MIX_TS, S)

    def body(dxo_ref, u_ref, v_ref, yl_ref, fo_ref, vec_ref, gp_ref, wcat_ref, wcatt_ref, bz_ref, pavg_ref,
             wout_ref, dyo_ref, dyl_ref, duv_ref, acc_ref, dgp_ref, dwm_ref, dbz_ref):
        @pl.when(pl.program_id(0) == 0)
        def _():
            acc_ref[...] = jnp.zeros_like(acc_ref)
            dgp_ref[...] = jnp.zeros_like(dgp_ref)
            dwm_ref[...] = jnp.zeros_like(dwm_ref)
            dbz_ref[...] = jnp.zeros_like(dbz_ref)

        dxo_v = dxo_ref[...]
        acc_ref[2:3, :] += _csum(dxo_v * fo_ref[...].astype(F32))
        dyo = (vec_ref[2:3, :] * dxo_v).astype(BF16)
        dyo_ref[...] = dyo
        dn = [_dot_nt(dyo, wout_ref[k]) for k in range(NDEV)]
        dn1 = jnp.concatenate(dn[:NDEV // 2], axis=1)
        dn2 = jnp.concatenate(dn[NDEV // 2:], axis=1)
        dyl, dg1 = _rms_bwd(dn1, yl_ref[...], gp_ref[1:2, :])
        dyl_ref[...] = dyl
        u = jnp.concatenate([u_ref[0], u_ref[1]], axis=1)
        v = jnp.concatenate([v_ref[0], v_ref[1]], axis=1)
        ug, rs, vhat, vh, z = _gmlp_fwd_parts(u, v, gp_ref, wcat_ref, bz_ref, pavg_ref, ts)
        dyg, dg2 = _rms_bwd(dn2, ug * z, gp_ref[2:3, :])
        du = (dyg * z) * _gelu_grad(u)
        dz = dyg * ug
        lane_head = lax.broadcasted_iota(jnp.int32, (CHUNK, GW), 1) // HD
        vhb = vh.astype(BF16)
        dvhs = []
        dbz = jnp.zeros((CHUNK, GW), F32)
        dwm = jnp.zeros((HEADS * CHUNK, CHUNK), F32)
        for n in range(ts // CHUNK):
            dzc = dz[n * CHUNK:(n + 1) * CHUNK, :]
            dbz = dbz + dzc
            stack = _head_stack(dzc, lane_head)
            dwm = dwm + _dot_nt(stack, vhb[n * CHUNK:(n + 1) * CHUNK, :])
            dvhs.append(_dot(wcatt_ref[...], stack))
        dbz_ref[...] += dbz
        dwm_ref[...] += dwm
        dvh = jnp.concatenate(dvhs, axis=0) if len(dvhs) > 1 else dvhs[0]
        pavg = pavg_ref[...]
        dvn = _csum(dvh * vhat)
        dvhat = dvh * gp_ref[0:1, :]
        dvg = rs * (dvhat - _seg_mean(dvhat, pavg) - vhat * _seg_mean(dvhat * vhat, pavg))
        dv = dvg * _gelu_grad(v)
        duv_ref[0] = du[:, :PC].astype(BF16)
        duv_ref[1] = du[:, PC:].astype(BF16)
        duv_ref[2] = dv[:, :PC].astype(BF16)
        duv_ref[3] = dv[:, PC:].astype(BF16)
        dgp_ref[0:1, :] += dvn
        dgp_ref[1:2, :] += dg1
        dgp_ref[2:3, :] += dg2

    row = pl.BlockSpec((ts, D), lambda i: (i, 0))
    full = lambda shp: pl.BlockSpec(shp, lambda i: tuple(0 for _ in shp))
    return pl.pallas_call(
        body, name=f"mix_out_bwd_{tag}",
        grid=(S // ts,),
        in_specs=[row, pl.BlockSpec((2, ts, PC), lambda i: (2, i, 0)), pl.BlockSpec((2, ts, PC), lambda i: (3, i, 0)),
                  pl.BlockSpec((ts, LW), lambda i: (i, 0)), row, full((8, D)), full((8, GW)),
                  full((CHUNK, HEADS * CHUNK)), full((CHUNK, HEADS * CHUNK)), full((CHUNK, GW)), full((GW, GW)),
                  pl.BlockSpec((NDEV, OR, D), lambda i: (0, 0, 0))],
        out_specs=[row, pl.BlockSpec((ts, LW), lambda i: (i, 0)), pl.BlockSpec((4, ts, PC), lambda i: (0, i, 0)),
                   full((8, D)), full((8, GW)), full((HEADS * CHUNK, CHUNK)), full((CHUNK, GW))],
        out_shape=[jax.ShapeDtypeStruct((S, D), BF16), jax.ShapeDtypeStruct((S, LW), F32),
                   jax.ShapeDtypeStruct((4, S, PC), BF16), jax.ShapeDtypeStruct((8, D), F32),
                   jax.ShapeDtypeStruct((8, GW), F32), jax.ShapeDtypeStruct((HEADS * CHUNK, CHUNK), F32),
                   jax.ShapeDtypeStruct((CHUNK, GW), F32)],
        compiler_params=_cparams("arbitrary"),
    )(dxo, proj, proj, ylru, fo, vec, gp, wcat, wcat_t, bz, pavg, wout_g)


def final_loss(x, target, gain):
    S = x.shape[0]
    ts = min(512, S)

    def body(x_ref, t_ref, g_ref, loss_ref, dx_ref, dg_ref):
        @pl.when(pl.program_id(0) == 0)
        def _():
            loss_ref[...] = jnp.zeros_like(loss_ref)
            dg_ref[...] = jnp.zeros_like(dg_ref)

        xv = x_ref[...]
        gain_v = g_ref[0:1, :]
        rstd = lax.rsqrt(_rmean(xv * xv) + EPS)
        xhat = xv * rstd
        err = xhat * gain_v - t_ref[...]
        loss_ref[...] += 0.5 * _csum(_rmean(err * err))
        dy = err * (1.0 / D)
        dg_ref[0:1, :] += _csum(dy * xhat)
        dxhat = dy * gain_v
        dx_ref[...] = rstd * (dxhat - xhat * _rmean(dxhat * xhat))

    row = pl.BlockSpec((ts, D), lambda i: (i, 0))
    return pl.pallas_call(
        body, name="final_loss",
        grid=(S // ts,),
        in_specs=[row, row, pl.BlockSpec((8, D), lambda i: (0, 0))],
        out_specs=[pl.BlockSpec((8, 128), lambda i: (0, 0)), row, pl.BlockSpec((8, D), lambda i: (0, 0))],
        out_shape=[jax.ShapeDtypeStruct((8, 128), F32), jax.ShapeDtypeStruct((S, D), F32),
                   jax.ShapeDtypeStruct((8, D), F32)],
        compiler_params=_cparams("arbitrary"),
    )(x, target, gain)


def _vec(mod_l, j, gain):
    return jnp.concatenate([mod_l[3 * j:3 * j + 3], gain[None, :], jnp.zeros((4, D), F32)], axis=0)


def _block_diag_tiles(w):
    w4 = w.reshape(LW // LC, 2, HD, HD)
    eye2 = jnp.eye(2, dtype=w.dtype)
    return (w4[:, :, :, None, :] * eye2[None, :, None, :, None]).reshape(LW // LC, LC, LC).astype(BF16)


def _block_diag_extract(dw):
    d5 = dw.reshape(LW // LC, 2, HD, 2, HD)
    return jnp.einsum('cihkj,ik->cihj', d5, jnp.eye(2, dtype=dw.dtype)).reshape(HEADS, HD, HD)


def _layer_params(l, p, conv_w_full):
    lp = jnp.concatenate([conv_w_full[l], p['conv_b'][l][None], p['gate_a_b'][l].reshape(1, LW),
                          p['gate_x_b'][l].reshape(1, LW), p['lru_lambda'][l][None]], axis=0)
    gp = jnp.concatenate([p['v_norm'][l][None], p['lru_out_norm'][l][None], p['gmlp_out_norm'][l][None],
                          jnp.zeros((5, GW), F32)], axis=0)
    ws = p['spatial_w'][l] * jnp.tril(jnp.ones((CHUNK, CHUNK), F32))
    wcat = ws.transpose(1, 0, 2).reshape(CHUNK, HEADS * CHUNK).astype(BF16)
    wcat_t = ws.transpose(2, 0, 1).reshape(CHUNK, HEADS * CHUNK).astype(BF16)
    bz = jnp.repeat(p['spatial_b'][l].T, HD, axis=1)
    return dict(lp=lp, gp=gp, wcat=wcat, wcat_t=wcat_t, bz=bz,
                wa_t=_block_diag_tiles(p['gate_a_w'][l]), wx_t=_block_diag_tiles(p['gate_x_w'][l]))


def _pavg():
    return jnp.kron(jnp.eye(HEADS, dtype=F32), jnp.full((HD, HD), 1.0 / HD, F32)).astype(BF16)


GATHER_RIDES = {
    ('ffn_a', 0): [('w_in', 0), ('w_out', 0), ('gu', DEPTH)],
    ('mix_in', 0): [('down', DEPTH)],
    ('lru', 0): [('down', 1)],
    ('mix_out', 0): [('w_in', 1), ('w_out', 1)],
    ('ffn_b', 0): [('gu', 1)],
    ('ffn_a', 1): [('gu', DEPTH + 1), ('down', DEPTH + 1)],
}


def local_fwd_bwd(me_arr, x, target, mod, p, loc, gathered, conv_w_full):
    pavg = _pavg()
    g = dict(gathered)

    def ride(call, l):
        todo = GATHER_RIDES.get((call, l))
        return None if todo is None else (todo, GatherRide([(loc[kind], slot) for kind, slot in todo]))

    def run(fn, call, l, *args):
        r = ride(call, l)
        outs, got = fn(*args, ride=None if r is None else r[1])
        if r is not None:
            g.update(dict(zip(r[0], got)))
        return outs

    saved = []
    h = x
    for l in range(DEPTH):
        q = _layer_params(l, p, conv_w_full)
        v1 = _vec(mod[l], 0, p['ffn1_norm'][l])
        vm = _vec(mod[l], 1, p['mix_norm'][l])
        v2 = _vec(mod[l], 2, p['ffn2_norm'][l])
        x0 = h
        x1, h1, gu1, f1 = run(ffn_fwd, 'ffn_a', l, x0, v1, g['gu', l], g['down', l], f"a{l}")
        hm, proj = run(mix_in_fwd, 'mix_in', l, x1, vm, g['w_in', l], f"{l}")
        ylru, xc, hst = run(lru_fwd, 'lru', l, proj, q['lp'], q['wa_t'], q['wx_t'], f"{l}")
        x2, y, fo = run(mix_out_fwd, 'mix_out', l, proj, ylru, x1, vm, q['gp'], q['wcat'], q['bz'], pavg,
                        g['w_out', l], f"{l}")
        x3, h2, gu2, f2 = run(ffn_fwd, 'ffn_b', l, x2, v2, g['gu', DEPTH + l], g['down', DEPTH + l], f"b{l}")
        saved.append(dict(q=q, v1=v1, vm=vm, v2=v2, x0=x0, x1=x1, x2=x2, h1=h1, gu1=gu1, f1=f1, hm=hm, proj=proj,
                          ylru=ylru, xc=xc, hst=hst, y=y, fo=fo, h2=h2, gu2=gu2, f2=f2))
        h = x3
    fin = jnp.concatenate([p['final_norm'][None], jnp.zeros((7, D), F32)], axis=0)
    loss8, dx, dfin = final_loss(h, target, fin)
    loss = loss8[0, 0]

    big = dict(gu=None, down=None, w_in=None, w_out=None)
    small = {k: [None] * DEPTH for k in ('ffn1_norm', 'mix_norm', 'ffn2_norm', 'conv_w', 'conv_b', 'gate_a_w',
                                         'gate_a_b', 'gate_x_w', 'gate_x_b', 'lru_lambda', 'v_norm', 'spatial_w',
                                         'spatial_b', 'lru_out_norm', 'gmlp_out_norm')}
    dmod = [None] * DEPTH
    tril = jnp.tril(jnp.ones((CHUNK, CHUNK), F32))
    for l in reversed(range(DEPTH)):
        sv = saved[l]
        q = sv['q']
        dx2, dgu, a, df, acc2 = ffn_bwd(dx, sv['x2'], sv['gu2'], sv['f2'], sv['v2'],
                                        g['gu', DEPTH + l], g['down', DEPTH + l], f"b{l}")
        big['gu'] = tn_matmul_scatter(me_arr, sv['h2'][None], dgu, DEPTH + l, 2 * DEPTH, big['gu'], f"dw_gu_b{l}")
        big['down'] = tn_matmul_scatter(me_arr, a, df[None], DEPTH + l, 2 * DEPTH, big['down'], f"dw_down_b{l}", split=2)
        dyo, dylru, duv, accmo, dgp, dwm, dbz = mix_out_bwd(dx2, sv['proj'], sv['ylru'], sv['fo'], sv['vm'], q['gp'],
                                                             q['wcat'], q['wcat_t'], q['bz'], pavg, g['w_out', l], f"{l}")
        big['w_out'] = tn_matmul_scatter(me_arr, sv['y'][None], dyo[None], l, DEPTH, big['w_out'], f"dw_out_{l}",
                                         split=NDEV)
        dxl, dgl, dlp, dwa, dwx = lru_bwd(dylru, sv['proj'], sv['xc'], sv['hst'], q['lp'], q['wa_t'], q['wx_t'], f"{l}")
        dproj = jnp.concatenate([dxl, dgl, duv], axis=0)
        dx1, accmi = mix_in_bwd(dproj, sv['x1'], dx2, sv['vm'], g['w_in', l], f"{l}")
        big['w_in'] = tn_matmul_scatter(me_arr, sv['hm'][None], dproj, l, DEPTH, big['w_in'], f"dw_in_{l}")
        dx0, dgu, a, df, acc1 = ffn_bwd(dx1, sv['x0'], sv['gu1'], sv['f1'], sv['v1'],
                                        g['gu', l], g['down', l], f"a{l}")
        big['gu'] = tn_matmul_scatter(me_arr, sv['h1'][None], dgu, l, 2 * DEPTH, big['gu'], f"dw_gu_a{l}")
        big['down'] = tn_matmul_scatter(me_arr, a, df[None], l, 2 * DEPTH, big['down'], f"dw_down_a{l}", split=2)
        dx = dx0
        dmod[l] = jnp.concatenate([acc1[0:3], accmi[0:2], accmo[2:3], acc2[0:3]], axis=0)
        small['ffn1_norm'][l] = acc1[3]
        small['mix_norm'][l] = accmi[3]
        small['ffn2_norm'][l] = acc2[3]
        small['conv_w'][l] = dlp[0:4]
        small['conv_b'][l] = dlp[4]
        small['gate_a_b'][l] = dlp[5].reshape(HEADS, HD)
        small['gate_x_b'][l] = dlp[6].reshape(HEADS, HD)
        small['lru_lambda'][l] = dlp[7]
        small['gate_a_w'][l] = _block_diag_extract(dwa)
        small['gate_x_w'][l] = _block_diag_extract(dwx)
        small['v_norm'][l] = dgp[0]
        small['lru_out_norm'][l] = dgp[1]
        small['gmlp_out_norm'][l] = dgp[2]
        small['spatial_w'][l] = dwm.reshape(HEADS, CHUNK, CHUNK) * tril
        small['spatial_b'][l] = dbz.reshape(CHUNK, HEADS, HD).sum(-1).T
    small = {k: jnp.stack(v) for k, v in small.items()}
    small['final_norm'] = dfin[0]
    return loss, dx, big, small, jnp.stack(dmod)


def ada_fwd(c_all, w_ada, b_loc):
    def body(c_ref, w_ref, b_ref, mod_ref, sc_ref):
        cv = c_ref[...]
        sc = cv * _sigmoid(cv)
        sc_ref[...] = sc
        mod_ref[...] = _dot3(sc, w_ref[...]) + b_ref[...]

    return pl.pallas_call(
        body, name="ada_fwd",
        grid=(DEPTH,),
        in_specs=[pl.BlockSpec((NDEV, D), lambda l: (0, 0)), pl.BlockSpec((None, D, AC), lambda l: (l, 0, 0)),
                  pl.BlockSpec((None, 1, AC), lambda l: (l, 0, 0))],
        out_specs=[pl.BlockSpec((None, NDEV, AC), lambda l: (l, 0, 0)), pl.BlockSpec((NDEV, D), lambda l: (0, 0))],
        out_shape=[jax.ShapeDtypeStruct((DEPTH, NDEV, AC), F32), jax.ShapeDtypeStruct((NDEV, D), F32)],
        compiler_params=_cparams("arbitrary"),
    )(c_all, w_ada, b_loc)


def ada_bwd(sc_t, dmod_cols):
    def body(sc_ref, dm_ref, g_ref):
        sc = sc_ref[...]
        dm = dm_ref[...]
        acc = sc[:, 0:1] * dm[0:1, :]
        for b in range(1, NDEV):
            acc = acc + sc[:, b:b + 1] * dm[b:b + 1, :]
        g_ref[...] = acc

    return pl.pallas_call(
        body, name="ada_bwd",
        grid=(DEPTH,),
        in_specs=[pl.BlockSpec((D, NDEV), lambda l: (0, 0)), pl.BlockSpec((None, NDEV, AC), lambda l: (l, 0, 0))],
        out_specs=pl.BlockSpec((None, None, D, AC), lambda l: (0, l, 0, 0)),
        out_shape=jax.ShapeDtypeStruct((1, DEPTH, D, AC), F32),
        compiler_params=_cparams("arbitrary"),
    )(sc_t, dmod_cols)


def _row_tile(rows, cols):
    if rows * cols <= 512 * 1024:
        return rows
    for tr in (512, 384, 352, 256, 128, 64, 32, 16, 8):
        if rows % tr == 0:
            return tr
    return rows


def adamw(gparts, slot0, w, m, v, name):
    P, _, R, C = gparts.shape
    L = w.shape[0]
    tr = _row_tile(R, C)

    def body(g_ref, w_ref, m_ref, v_ref, go_ref, do_ref, mo_ref, vo_ref):
        g = g_ref[0].astype(F32)
        for p in range(1, P):
            g = g + g_ref[p].astype(F32)
        go_ref[...] = g
        mn = ADAM_B1 * m_ref[...] + (1.0 - ADAM_B1) * g
        vn = ADAM_B2 * v_ref[...] + (1.0 - ADAM_B2) * (g * g)
        mo_ref[...] = mn
        vo_ref[...] = vn
        m_hat = mn / (1.0 - ADAM_B1 ** ADAM_STEP)
        v_hat = vn / (1.0 - ADAM_B2 ** ADAM_STEP)
        do_ref[...] = -ADAM_LR * (m_hat / (jnp.sqrt(v_hat) + ADAM_EPS) + ADAM_WD * w_ref[...])

    blk = pl.BlockSpec((None, tr, C), lambda l, i: (l, i, 0))
    return pl.pallas_call(
        body, name=name,
        grid=(L, R // tr),
        in_specs=[pl.BlockSpec((P, None, tr, C), lambda l, i: (0, slot0 + l, i, 0)), blk, blk, blk],
        out_specs=[blk, blk, blk, blk],
        out_shape=[jax.ShapeDtypeStruct((L, R, C), F32)] * 4,
        compiler_params=_cparams("arbitrary", "arbitrary"),
    )(gparts, w, m, v)


def sum_parts(parts):
    P, R, C = parts.shape

    def body(p_ref, o_ref):
        acc = p_ref[0]
        for p in range(1, P):
            acc = acc + p_ref[p]
        o_ref[...] = acc

    return pl.pallas_call(
        body, name="sum_parts",
        in_specs=[pl.BlockSpec(memory_space=pltpu.VMEM)],
        out_specs=pl.BlockSpec(memory_space=pltpu.VMEM),
        out_shape=jax.ShapeDtypeStruct((R, C), F32),
    )(parts)


WEIGHTS = ['w_ada', 'b_ada', 'ffn1_norm', 'ffn1_w_gu', 'ffn1_w_down', 'mix_norm', 'w_in', 'conv_w', 'conv_b',
           'gate_a_w', 'gate_a_b', 'gate_x_w', 'gate_x_b', 'lru_lambda', 'v_norm', 'spatial_w', 'spatial_b',
           'lru_out_norm', 'gmlp_out_norm', 'w_out', 'ffn2_norm', 'ffn2_w_gu', 'ffn2_w_down', 'final_norm']
PACKED = ['b_ada', 'ffn1_norm', 'mix_norm', 'conv_b', 'gate_a_w', 'gate_a_b', 'gate_x_w', 'gate_x_b', 'lru_lambda',
          'v_norm', 'spatial_w', 'spatial_b', 'lru_out_norm', 'gmlp_out_norm', 'ffn2_norm', 'final_norm', 'conv_w']
PACK_LANES = 128
PACK_ROW_ALIGN = 8 * NDEV


def _pack(d):
    flat = jnp.concatenate([d[k].reshape(-1).astype(F32) for k in PACKED])
    rows = -(-flat.shape[0] // (PACK_LANES * PACK_ROW_ALIGN)) * PACK_ROW_ALIGN
    flat = jnp.concatenate([flat, jnp.zeros((rows * PACK_LANES - flat.shape[0],), F32)])
    return flat.reshape(rows, PACK_LANES)


def _unpack(buf, shapes):
    flat = buf.reshape(-1)
    out, off = {}, 0
    for k in PACKED:
        size = 1
        for s in shapes[k]:
            size *= s
        out[k] = flat[off:off + size].reshape(shapes[k])
        off += size
    return out


def kernel(x, c, w_ada, b_ada, ffn1_norm, ffn1_w_gu, ffn1_w_down, mix_norm, w_in, conv_w, conv_b, gate_a_w, gate_a_b, gate_x_w, gate_x_b, lru_lambda, v_norm, spatial_w, spatial_b, lru_out_norm, gmlp_out_norm, w_out, ffn2_norm, ffn2_w_gu, ffn2_w_down, final_norm, loss_target, m_w_ada, m_b_ada, m_ffn1_norm, m_ffn1_w_gu, m_ffn1_w_down, m_mix_norm, m_w_in, m_conv_w, m_conv_b, m_gate_a_w, m_gate_a_b, m_gate_x_w, m_gate_x_b, m_lru_lambda, m_v_norm, m_spatial_w, m_spatial_b, m_lru_out_norm, m_gmlp_out_norm, m_w_out, m_ffn2_norm, m_ffn2_w_gu, m_ffn2_w_down, m_final_norm, v_w_ada, v_b_ada, v_ffn1_norm, v_ffn1_w_gu, v_ffn1_w_down, v_mix_norm, v_w_in, v_conv_w, v_conv_b, v_gate_a_w, v_gate_a_b, v_gate_x_w, v_gate_x_b, v_lru_lambda, v_v_norm, v_spatial_w, v_spatial_b, v_lru_out_norm, v_gmlp_out_norm, v_w_out, v_ffn2_norm, v_ffn2_w_gu, v_ffn2_w_down, v_final_norm):
    w = dict(w_ada=w_ada, b_ada=b_ada, ffn1_norm=ffn1_norm, ffn1_w_gu=ffn1_w_gu, ffn1_w_down=ffn1_w_down, mix_norm=mix_norm, w_in=w_in, conv_w=conv_w, conv_b=conv_b, gate_a_w=gate_a_w, gate_a_b=gate_a_b, gate_x_w=gate_x_w, gate_x_b=gate_x_b, lru_lambda=lru_lambda, v_norm=v_norm, spatial_w=spatial_w, spatial_b=spatial_b, lru_out_norm=lru_out_norm, gmlp_out_norm=gmlp_out_norm, w_out=w_out, ffn2_norm=ffn2_norm, ffn2_w_gu=ffn2_w_gu, ffn2_w_down=ffn2_w_down, final_norm=final_norm)
    m = dict(w_ada=m_w_ada, b_ada=m_b_ada, ffn1_norm=m_ffn1_norm, ffn1_w_gu=m_ffn1_w_gu, ffn1_w_down=m_ffn1_w_down, mix_norm=m_mix_norm, w_in=m_w_in, conv_w=m_conv_w, conv_b=m_conv_b, gate_a_w=m_gate_a_w, gate_a_b=m_gate_a_b, gate_x_w=m_gate_x_w, gate_x_b=m_gate_x_b, lru_lambda=m_lru_lambda, v_norm=m_v_norm, spatial_w=m_spatial_w, spatial_b=m_spatial_b, lru_out_norm=m_lru_out_norm, gmlp_out_norm=m_gmlp_out_norm, w_out=m_w_out, ffn2_norm=m_ffn2_norm, ffn2_w_gu=m_ffn2_w_gu, ffn2_w_down=m_ffn2_w_down, final_norm=m_final_norm)
    v = dict(w_ada=v_w_ada, b_ada=v_b_ada, ffn1_norm=v_ffn1_norm, ffn1_w_gu=v_ffn1_w_gu, ffn1_w_down=v_ffn1_w_down, mix_norm=v_mix_norm, w_in=v_w_in, conv_w=v_conv_w, conv_b=v_conv_b, gate_a_w=v_gate_a_w, gate_a_b=v_gate_a_b, gate_x_w=v_gate_x_w, gate_x_b=v_gate_x_b, lru_lambda=v_lru_lambda, v_norm=v_v_norm, spatial_w=v_spatial_w, spatial_b=v_spatial_b, lru_out_norm=v_lru_out_norm, gmlp_out_norm=v_gmlp_out_norm, w_out=v_w_out, ffn2_norm=v_ffn2_norm, ffn2_w_gu=v_ffn2_w_gu, ffn2_w_down=v_ffn2_w_down, final_norm=v_final_norm)
    me = 4 * lax.axis_index("x") + 2 * lax.axis_index("y") + lax.axis_index("c")

    loc = dict(gu=jnp.concatenate([ffn1_w_gu, ffn2_w_gu], axis=0).astype(BF16),
               down=jnp.concatenate([ffn1_w_down, ffn2_w_down], axis=0).astype(BF16),
               w_in=w_in.astype(BF16), w_out=w_out.astype(BF16))
    c_g, conv_g, gu0, down0 = all_gather([(c, None), (conv_w, None), (loc['gu'], 0), (loc['down'], 0)], "gather_first")
    conv_w_full = conv_g.transpose(1, 2, 0, 3).reshape(DEPTH, CONV_WIDTH, LW)

    b_loc = lax.dynamic_slice(b_ada, (0, me * AC), (DEPTH, AC)).reshape(DEPTH, 1, AC)
    mod_cols, sc_all = ada_fwd(c_g.reshape(NDEV, D), w_ada, b_loc)
    (mod_rows,) = all_to_all([mod_cols.transpose(1, 0, 2)], "scatter_mod")
    mod = mod_rows.transpose(1, 0, 2).reshape(DEPTH, NMOD, D)

    small_w = {k: w[k] for k in PACKED if k != 'conv_w'}
    me_arr = jnp.reshape(me, (1,)).astype(jnp.int32)
    loss_loc, dx, big, small_g, dmod = local_fwd_bwd(me_arr, x[0], loss_target[0], mod, small_w, loc,
                                                     {('gu', 0): gu0, ('down', 0): down0}, conv_w_full)
    loss = lax.psum(loss_loc, ("x", "y", "c"))

    small_g['b_ada'] = dmod.reshape(DEPTH, NMOD * D)
    gpack = _pack(small_g)
    rows = gpack.shape[0]
    dmod_out = dmod.reshape(DEPTH, NDEV, AC).transpose(1, 0, 2)
    dmod_r, pack_r = all_to_all([dmod_out, gpack.reshape(NDEV, rows // NDEV, PACK_LANES)], "scatter_grads")
    (gsum_g,) = all_gather([(sum_parts(pack_r), None)], "gather_small_grads")
    gsum = gsum_g.reshape(1, 1, rows, PACK_LANES)

    res = {}
    res['ffn1_w_gu'] = adamw(big['gu'], 0, w['ffn1_w_gu'], m['ffn1_w_gu'], v['ffn1_w_gu'], "adamw_gu_a")
    res['ffn2_w_gu'] = adamw(big['gu'], DEPTH, w['ffn2_w_gu'], m['ffn2_w_gu'], v['ffn2_w_gu'], "adamw_gu_b")
    res['ffn1_w_down'] = adamw(big['down'], 0, w['ffn1_w_down'], m['ffn1_w_down'], v['ffn1_w_down'], "adamw_down_a")
    res['ffn2_w_down'] = adamw(big['down'], DEPTH, w['ffn2_w_down'], m['ffn2_w_down'], v['ffn2_w_down'], "adamw_down_b")
    res['w_in'] = adamw(big['w_in'], 0, w['w_in'], m['w_in'], v['w_in'], "adamw_w_in")
    res['w_out'] = adamw(big['w_out'], 0, w['w_out'], m['w_out'], v['w_out'], "adamw_w_out")
    g_ada = ada_bwd(sc_all.T, dmod_r.transpose(1, 0, 2))
    res['w_ada'] = adamw(g_ada, 0, w['w_ada'], m['w_ada'], v['w_ada'], "adamw_w_ada")
    shapes = {k: w[k].shape for k in PACKED}
    shapes['conv_w'] = (DEPTH, CONV_WIDTH, LW)
    dummy = jnp.zeros(shapes['conv_w'], F32)
    packs = adamw(gsum, 0, _pack({**small_w, 'conv_w': dummy})[None], _pack({**{k: m[k] for k in small_w}, 'conv_w': dummy})[None],
                  _pack({**{k: v[k] for k in small_w}, 'conv_w': dummy})[None], "adamw_small")
    unpacked = [_unpack(b[0], shapes) for b in packs]
    for k in small_w:
        res[k] = tuple(u[k] for u in unpacked)
    gconv = lax.dynamic_slice(unpacked[0]['conv_w'], (0, 0, me * (LW // NDEV)), (DEPTH, CONV_WIDTH, LW // NDEV))
    cshape = (1, DEPTH * CONV_WIDTH, LW // NDEV)
    rc = adamw(gconv.reshape((1,) + cshape), 0, conv_w.reshape(cshape), m['conv_w'].reshape(cshape),
               v['conv_w'].reshape(cshape), "adamw_conv_w")
    res['conv_w'] = tuple(r.reshape(conv_w.shape) for r in rc)

    return (loss, dx[None], *[res[k][0] for k in WEIGHTS], *[res[k][1] for k in WEIGHTS],
            *[res[k][2] for k in WEIGHTS], *[res[k][3] for k in WEIGHTS])
```

```python
import jax
import jax.numpy as jnp
from jax import lax
from jax.experimental import pallas as pl
from jax.experimental.pallas import tpu as pltpu

F32 = jnp.float32
BF16 = jnp.bfloat16

NDEV = 8
DEPTH = 2
D = 1024
DFF = 2816
FC = 2 * DFF // NDEV
NCHUNK = DFF // FC
DR = DFF // NDEV
LW = 512
GW = 512
HD = 64
HEADS = 8
CHUNK = 128
PC = 2 * (LW + GW) // NDEV
OR = D // NDEV
NMOD = 9
AC = NMOD * D // NDEV
LC = 128
EPS = 1e-6
RG_LRU_C = 8.0
CONV_WIDTH = 4

ADAM_LR = 0.001
ADAM_B1 = 0.9
ADAM_B2 = 0.999
ADAM_EPS = 1e-08
ADAM_WD = 0.01
ADAM_STEP = 10

VMEM_LIMIT_BYTES = 60 * 1024 * 1024
MESH = pl.DeviceIdType.MESH
ANY = pl.BlockSpec(memory_space=pl.ANY)


def _cparams(*sem):
    return pltpu.CompilerParams(dimension_semantics=tuple(sem) if sem else None,
                                vmem_limit_bytes=VMEM_LIMIT_BYTES)


def _dot(a, b):
    return jnp.dot(a, b, preferred_element_type=F32)


def _dot_nt(a, b):
    return lax.dot_general(a, b, (((1,), (1,)), ((), ())), preferred_element_type=F32)


def _dot_tn(a, b):
    return lax.dot_general(a, b, (((0,), (0,)), ((), ())), preferred_element_type=F32)


def _split(a):
    hi = a.astype(BF16)
    lo = (a - hi.astype(F32)).astype(BF16)
    return hi, lo


def _dot3(a, b):
    ah, al = _split(a)
    bh, bl = _split(b)
    return _dot(ah, bh) + (_dot(ah, bl) + _dot(al, bh))


def _csum(a):
    return jnp.sum(a, axis=0, keepdims=True)


def _rmean(a):
    return jnp.mean(a, axis=-1, keepdims=True)


def _sigmoid(a):
    return 1.0 / (1.0 + jnp.exp(-a))


_GELU_K = 0.7978845608028654
_GELU_C = 0.044715


def _gelu(a):
    return 0.5 * a * (1.0 + jnp.tanh(_GELU_K * (a + _GELU_C * a * a * a)))


def _gelu_grad(a):
    t = jnp.tanh(_GELU_K * (a + _GELU_C * a * a * a))
    return 0.5 * (1.0 + t) + 0.5 * a * (1.0 - t * t) * (_GELU_K * (1.0 + 3.0 * _GELU_C * a * a))


def _norm_mod(x, gain, scale, shift):
    rstd = lax.rsqrt(_rmean(x * x) + EPS)
    return (x * rstd * gain) * (1.0 + scale) + shift


def _norm_mod_bwd(dh, x, gain, scale):
    rstd = lax.rsqrt(_rmean(x * x) + EPS)
    xhat = x * rstd
    dshift = _csum(dh)
    dscale = _csum(dh * (xhat * gain))
    dhn = dh * (1.0 + scale)
    dgain = _csum(dhn * xhat)
    dxhat = dhn * gain
    dx = rstd * (dxhat - xhat * _rmean(dxhat * xhat))
    return dx, dshift, dscale, dgain


def _rms(x, gain):
    rstd = lax.rsqrt(_rmean(x * x) + EPS)
    return x * rstd * gain


def _rms_bwd(dy, x, gain):
    rstd = lax.rsqrt(_rmean(x * x) + EPS)
    xhat = x * rstd
    dgain = _csum(dy * xhat)
    dxhat = dy * gain
    return rstd * (dxhat - xhat * _rmean(dxhat * xhat)), dgain


def _seg_mean(a, pavg):
    hi, lo = _split(a)
    return _dot(hi, pavg) + _dot(lo, pavg)


def _fetch_blocks(src_hbm, dst_vmem, sems, rows):
    copies = []
    for k in range(NDEV):
        dst = dst_vmem.at[k] if rows is None else dst_vmem.at[pl.ds(k * rows, rows)]
        copies.append(pltpu.make_async_copy(src_hbm.at[k], dst, sems.at[k]))
    for cp in copies:
        cp.start()
    for cp in copies:
        cp.wait()


def _place():
    return lax.axis_index("x"), lax.axis_index("y"), lax.axis_index("c")


def _slot(p):
    return 4 * p[0] + 2 * p[1] + p[2]


class GatherRide:
    def __init__(self, srcs):
        self.n = len(srcs)
        self.index = [i for _, i in srcs]
        self.args = [a for a, _ in srcs]
        self.out_shape = [jax.ShapeDtypeStruct((NDEV,) + (a.shape if i is None else a.shape[1:]), a.dtype)
                          for a, i in srcs]
        self.scratch = [pltpu.SemaphoreType.DMA((self.n, NDEV - 1)), pltpu.SemaphoreType.DMA((self.n, NDEV - 1)),
                        pltpu.SemaphoreType.DMA((self.n,))]

    def hooks(self, ins, outs, sems):
        send_sems, recv_sems, local_sems = sems
        n = self.n
        x, y, c = _place()
        me, sibling = (x, y, c), (x, y, 1 - c)
        chips = [(1 - x, y), (x, 1 - y), (1 - x, 1 - y)]

        def local(a):
            return ins[a] if self.index[a] is None else ins[a].at[self.index[a]]

        def copy(a, k, block, to, src=None):
            dst = outs[a].at[_slot(block)]
            return pltpu.make_async_remote_copy(
                src_ref=dst if src is None else src, dst_ref=dst,
                send_sem=send_sems.at[a, k], recv_sem=recv_sems.at[a, k],
                device_id=to, device_id_type=MESH)

        def mine():
            return [pltpu.make_async_copy(local(a), outs[a].at[_slot(me)], local_sems.at[a]) for a in range(n)]

        def first():
            cps = []
            for a in range(n):
                cps.append(copy(a, 0, me, sibling, src=local(a)))
                cps += [copy(a, 1 + j, me, (*chip, c), src=local(a)) for j, chip in enumerate(chips)]
            return cps

        def passed():
            return [copy(a, 4 + j, (*chip, c), sibling) for j, chip in enumerate(chips) for a in range(n)]

        def start():
            for cp in mine() + first():
                cp.start()

        def mid():
            for j, chip in enumerate(chips):
                for a in range(n):
                    copy(a, 1 + j, (*chip, c), me).wait_recv()
                    copy(a, 4 + j, (*chip, c), sibling).start()

        def finish():
            for a in range(n):
                copy(a, 0, sibling, me).wait_recv()
                for j, chip in enumerate(chips):
                    copy(a, 4 + j, (*chip, 1 - c), me).wait_recv()
            for cp in first() + passed():
                cp.wait_send()
            for cp in mine():
                cp.wait()

        return start, mid, finish


def all_gather(srcs, name):
    ride = GatherRide(srcs)
    n = ride.n

    def body(*refs):
        start, mid, finish = ride.hooks(refs[:n], refs[n:2 * n], refs[2 * n:])
        start()
        mid()
        finish()

    return pl.pallas_call(
        body, name=name,
        in_specs=[ANY] * n, out_specs=[ANY] * n, out_shape=ride.out_shape, scratch_shapes=ride.scratch,
    )(*ride.args)


def _call(core, ride, *, name, grid, in_specs, out_specs, out_shape, scratch_shapes, args):
    if ride is None:
        outs = pl.pallas_call(core, name=name, grid=grid, in_specs=in_specs, out_specs=out_specs,
                              out_shape=out_shape, scratch_shapes=scratch_shapes,
                              compiler_params=_cparams("arbitrary"))(*args)
        return outs, []
    n_in, n_out, n_sc, n = len(in_specs), len(out_shape), len(scratch_shapes), ride.n
    nsteps = grid[0]
    mid_step = max(nsteps - 2, 0)

    def body(*refs):
        cuts = [n_in, n_in + n, n_in + n + n_out, n_in + 2 * n + n_out, n_in + 2 * n + n_out + n_sc]
        ci, ri, co, ro, cs, rs = (refs[a:b] for a, b in zip([0] + cuts, cuts + [len(refs)]))
        start, mid, finish = ride.hooks(ri, ro, rs)
        i = pl.program_id(0)
        pl.when(i == 0)(start)
        core(*ci, *co, *cs)
        pl.when(i == mid_step)(mid)
        pl.when(i == nsteps - 1)(finish)

    outs = pl.pallas_call(
        body, name=name, grid=grid,
        in_specs=list(in_specs) + [ANY] * n, out_specs=list(out_specs) + [ANY] * n,
        out_shape=list(out_shape) + ride.out_shape, scratch_shapes=list(scratch_shapes) + ride.scratch,
        compiler_params=_cparams("arbitrary"))(*args, *ride.args)
    return outs[:n_out], outs[n_out:]


def all_to_all(arrs, name):
    n = len(arrs)

    def body(*refs):
        ins, outs = refs[:n], refs[n:2 * n]
        send_sems, recv_sems, local_sems = refs[2 * n:]
        x, y, c = _place()
        me = (x, y, c)

        def peer(k):
            return (1 - x if k & 4 else x, 1 - y if k & 2 else y, 1 - c if k & 1 else c)

        def copy(a, k):
            return pltpu.make_async_remote_copy(
                src_ref=ins[a].at[_slot(peer(k))], dst_ref=outs[a].at[_slot(me)],
                send_sem=send_sems.at[a, k - 1], recv_sem=recv_sems.at[a, k - 1],
                device_id=peer(k), device_id_type=MESH)

        def landing(a, k):
            return pltpu.make_async_remote_copy(
                src_ref=outs[a].at[_slot(peer(k))], dst_ref=outs[a].at[_slot(peer(k))],
                send_sem=send_sems.at[a, k - 1], recv_sem=recv_sems.at[a, k - 1],
                device_id=me, device_id_type=MESH)

        mine = [pltpu.make_async_copy(ins[a].at[_slot(me)], outs[a].at[_slot(me)], local_sems.at[a]) for a in range(n)]
        for cp in mine:
            cp.start()
        sends = [copy(a, k) for a in range(n) for k in range(1, NDEV)]
        for cp in sends:
            cp.start()
        for a in range(n):
            for k in range(1, NDEV):
                landing(a, k).wait_recv()
        for cp in sends:
            cp.wait_send()
        for cp in mine:
            cp.wait()

    return pl.pallas_call(
        body, name=name,
        in_specs=[ANY] * n, out_specs=[ANY] * n,
        out_shape=[jax.ShapeDtypeStruct(a.shape, a.dtype) for a in arrs],
        scratch_shapes=[pltpu.SemaphoreType.DMA((n, NDEV - 1)), pltpu.SemaphoreType.DMA((n, NDEV - 1)),
                        pltpu.SemaphoreType.DMA((n,))],
    )(*arrs)


FFN_TS = 256


def ffn_fwd(x, vec, wgu_g, wdown_g, tag, ride=None):
    S = x.shape[0]
    ts = min(FFN_TS, S)

    def body(x_ref, vec_ref, wgu_hbm, wd_hbm, xo_ref, h_ref, gu_ref, f_ref, wgu_v, wd_v, sems):
        @pl.when(pl.program_id(0) == 0)
        def _():
            _fetch_blocks(wgu_hbm, wgu_v, sems.at[0], None)
            _fetch_blocks(wd_hbm, wd_v, sems.at[1], DR)

        xv = x_ref[...]
        h = _norm_mod(xv, vec_ref[3:4, :], vec_ref[1:2, :], vec_ref[0:1, :]).astype(BF16)
        h_ref[...] = h
        acc = jnp.zeros((ts, D), F32)
        for j in range(NCHUNK):
            g = _dot(h, wgu_v[j])
            u = _dot(h, wgu_v[NCHUNK + j])
            gu_ref[j] = g.astype(BF16)
            gu_ref[NCHUNK + j] = u.astype(BF16)
            a = (g * _sigmoid(g) * u).astype(BF16)
            acc = acc + _dot(a, wd_v[pl.ds(j * FC, FC), :])
        f_ref[...] = acc.astype(BF16)
        xo_ref[...] = xv + (0.5 * vec_ref[2:3, :]) * acc

    return _call(
        body, ride, name=f"ffn_fwd_{tag}",
        grid=(S // ts,),
        in_specs=[pl.BlockSpec((ts, D), lambda i: (i, 0)),
                  pl.BlockSpec((8, D), lambda i: (0, 0)), ANY, ANY],
        out_specs=[pl.BlockSpec((ts, D), lambda i: (i, 0)),
                   pl.BlockSpec((ts, D), lambda i: (i, 0)),
                   pl.BlockSpec((NDEV, ts, FC), lambda i: (0, i, 0)),
                   pl.BlockSpec((ts, D), lambda i: (i, 0))],
        out_shape=[jax.ShapeDtypeStruct((S, D), F32), jax.ShapeDtypeStruct((S, D), BF16),
                   jax.ShapeDtypeStruct((NDEV, S, FC), BF16), jax.ShapeDtypeStruct((S, D), BF16)],
        scratch_shapes=[pltpu.VMEM((NDEV, D, FC), BF16), pltpu.VMEM((DFF, D), BF16),
                        pltpu.SemaphoreType.DMA((2, NDEV))],
        args=(x, vec, wgu_g, wdown_g))


def ffn_bwd(dxo, x, gu, f, vec, wgu_g, wdown_g, tag):
    S = x.shape[0]
    ts = min(FFN_TS, S)

    def body(dxo_ref, x_ref, gu_ref, f_ref, vec_ref, wgu_hbm, wd_hbm,
             dx_ref, dgu_ref, a_ref, df_ref, acc_ref, wgu_v, wd_v, sems):
        @pl.when(pl.program_id(0) == 0)
        def _():
            _fetch_blocks(wgu_hbm, wgu_v, sems.at[0], None)
            _fetch_blocks(wd_hbm, wd_v, sems.at[1], DR)
            acc_ref[...] = jnp.zeros_like(acc_ref)

        dxo_v = dxo_ref[...]
        dgate = 0.5 * _csum(dxo_v * f_ref[...].astype(F32))
        df = ((0.5 * vec_ref[2:3, :]) * dxo_v).astype(BF16)
        df_ref[...] = df
        dh = jnp.zeros((ts, D), F32)
        for j in range(NCHUNK):
            da = _dot_nt(df, wd_v[pl.ds(j * FC, FC), :])
            g = gu_ref[j].astype(F32)
            u = gu_ref[NCHUNK + j].astype(F32)
            sg = _sigmoid(g)
            si = g * sg
            a_ref[j] = (si * u).astype(BF16)
            dg = (da * u * (sg * (1.0 + g * (1.0 - sg)))).astype(BF16)
            du = (da * si).astype(BF16)
            dgu_ref[j] = dg
            dgu_ref[NCHUNK + j] = du
            dh = dh + _dot_nt(dg, wgu_v[j]) + _dot_nt(du, wgu_v[NCHUNK + j])
        dx, dshift, dscale, dgain = _norm_mod_bwd(dh, x_ref[...], vec_ref[3:4, :], vec_ref[1:2, :])
        dx_ref[...] = dx + dxo_v
        acc_ref[0:1, :] += dshift
        acc_ref[1:2, :] += dscale
        acc_ref[2:3, :] += dgate
        acc_ref[3:4, :] += dgain

    row = pl.BlockSpec((ts, D), lambda i: (i, 0))
    return pl.pallas_call(
        body, name=f"ffn_bwd_{tag}",
        grid=(S // ts,),
        in_specs=[row, row, pl.BlockSpec((NDEV, ts, FC), lambda i: (0, i, 0)), row,
                  pl.BlockSpec((8, D), lambda i: (0, 0)), ANY, ANY],
        out_specs=[row, pl.BlockSpec((NDEV, ts, FC), lambda i: (0, i, 0)),
                   pl.BlockSpec((NCHUNK, ts, FC), lambda i: (0, i, 0)), row,
                   pl.BlockSpec((8, D), lambda i: (0, 0))],
        out_shape=[jax.ShapeDtypeStruct((S, D), F32), jax.ShapeDtypeStruct((NDEV, S, FC), BF16),
                   jax.ShapeDtypeStruct((NCHUNK, S, FC), BF16), jax.ShapeDtypeStruct((S, D), BF16),
                   jax.ShapeDtypeStruct((8, D), F32)],
        scratch_shapes=[pltpu.VMEM((NDEV, D, FC), BF16), pltpu.VMEM((DFF, D), BF16),
                        pltpu.SemaphoreType.DMA((2, NDEV))],
        compiler_params=_cparams("arbitrary"),
    )(dxo, x, gu, f, vec, wgu_g, wdown_g)


NCHIP = NDEV // 2


def tn_matmul_scatter(me_arr, a, b, slot, nslots, prev, name, split=1):
    na, S, M = a.shape
    nb, _, N = b.shape
    ncall = NDEV // split
    ts = min(2048, S)
    nsteps = S // ts
    mp = M // split
    other_step = {1: lambda j: 2 * j, 2: lambda j: j, 8: lambda j: 0}[split]
    mine_step = {1: lambda j: 2 * j + 1, 2: lambda j: j, 8: lambda j: 0}[split]

    def group(k, me_ref):
        if split == 1:
            return jnp.bitwise_xor(me_ref[0], NDEV - 1 - k)
        if split == 2:
            return jnp.bitwise_xor(me_ref[0] // 2, NCHIP - 1 - k)
        return 0

    def body(me_ref, *refs):
        a_ref, b_ref = refs[0], refs[1]
        recv_ref, acc, sb_other, sb_mine, land, d2d_send, d2d_recv, ici_send, ici_recv = refs[-9:]
        k = pl.program_id(0)
        s = pl.program_id(1)
        x, y, c = _place()
        my_chip = 2 * x + y

        def chip_of(j):
            if split == 8:
                cx, cy = j // 2, j % 2
            else:
                flip = NCHIP - 1 - j
                cx, cy = (1 - x if flip & 2 else x), (1 - y if flip & 1 else y)
            return cx, cy, 2 * cx + cy

        def piece(j, core):
            if split == 1:
                return acc[...]
            start = core * mp if split == 2 else (2 * j + core) * mp
            return acc[pl.ds(pl.multiple_of(start, 8), mp), :]

        def to_sibling(j):
            return pltpu.make_async_remote_copy(
                src_ref=sb_other.at[j], dst_ref=land.at[j], send_sem=d2d_send.at[j], recv_sem=d2d_recv.at[j],
                device_id=(x, y, 1 - c), device_id_type=MESH)

        def to_owner(j):
            cx, cy, ci = chip_of(j)
            dst = recv_ref.at[my_chip, slot]
            return ci, pltpu.make_async_copy(sb_mine.at[j], dst, ici_send.at[j]), pltpu.make_async_remote_copy(
                src_ref=sb_mine.at[j], dst_ref=dst, send_sem=ici_send.at[j], recv_sem=ici_recv.at[my_chip],
                device_id=(cx, cy, c), device_id_type=MESH)

        @pl.when(s == 0)
        def _():
            acc[...] = jnp.zeros_like(acc)

        acc[...] += _dot_tn(a_ref[...], b_ref[...])

        for kk in range(ncall):
            @pl.when((s == nsteps - 1) & (k == kk))
            def _():
                for j in range(NCHIP):
                    if other_step(j) == kk:
                        sb_other[j] = piece(j, 1 - c).astype(BF16)
                        to_sibling(j).start()
                for j in range(NCHIP):
                    if mine_step(j) == kk:
                        to_sibling(j).wait_recv()
                        sb_mine[j] = (piece(j, c) + land[j].astype(F32)).astype(BF16)
                        ci, loc, rem = to_owner(j)
                        pl.when(ci == my_chip)(loc.start)
                        pl.when(ci != my_chip)(rem.start)

        @pl.when((s == nsteps - 1) & (k == ncall - 1))
        def _():
            for j in range(NCHIP):
                to_sibling(j).wait_send()
                ci, loc, rem = to_owner(j)
                pl.when(ci == my_chip)(loc.wait)
                pl.when(ci != my_chip)(rem.wait_send)
            for src in range(NCHIP):
                @pl.when(my_chip != src)
                def _():
                    pltpu.make_async_remote_copy(
                        src_ref=recv_ref.at[src, slot], dst_ref=recv_ref.at[src, slot],
                        send_sem=ici_send.at[src], recv_sem=ici_recv.at[src],
                        device_id=(src // 2, src % 2, c), device_id_type=MESH).wait_recv()

    in_specs = [pl.BlockSpec((None, ts, M), (lambda k, s, me: (group(k, me), s, 0)) if na > 1 else (lambda k, s, me: (0, s, 0))),
                pl.BlockSpec((None, ts, N), (lambda k, s, me: (group(k, me), s, 0)) if nb > 1 else (lambda k, s, me: (0, s, 0)))]
    args = [me_arr, a, b]
    aliases = {}
    if prev is not None:
        in_specs.append(ANY)
        args.append(prev)
        aliases = {3: 0}
    return pl.pallas_call(
        body, name=name,
        grid_spec=pltpu.PrefetchScalarGridSpec(
            num_scalar_prefetch=1, grid=(ncall, nsteps), in_specs=in_specs, out_specs=ANY,
            scratch_shapes=[pltpu.VMEM((M, N), F32), pltpu.VMEM((NCHIP, mp, N), BF16), pltpu.VMEM((NCHIP, mp, N), BF16),
                            pltpu.VMEM((NCHIP, mp, N), BF16), pltpu.SemaphoreType.DMA((NCHIP,)),
                            pltpu.SemaphoreType.DMA((NCHIP,)), pltpu.SemaphoreType.DMA((NCHIP,)),
                            pltpu.SemaphoreType.DMA((NCHIP,))]),
        out_shape=jax.ShapeDtypeStruct((NCHIP, nslots, mp, N), BF16),
        input_output_aliases=aliases,
        compiler_params=_cparams("arbitrary", "arbitrary"),
    )(*args)


MIX_TS = 256


def mix_in_fwd(x, vec, win_g, tag, ride=None):
    S = x.shape[0]
    ts = min(MIX_TS, S)

    def body(x_ref, vec_ref, win_ref, hm_ref, proj_ref):
        h = _norm_mod(x_ref[...], vec_ref[3:4, :], vec_ref[1:2, :], vec_ref[0:1, :]).astype(BF16)
        hm_ref[...] = h
        for k in range(NDEV):
            proj_ref[k] = _dot(h, win_ref[k])

    return _call(
        body, ride, name=f"mix_in_fwd_{tag}",
        grid=(S // ts,),
        in_specs=[pl.BlockSpec((ts, D), lambda i: (i, 0)), pl.BlockSpec((8, D), lambda i: (0, 0)),
                  pl.BlockSpec((NDEV, D, PC), lambda i: (0, 0, 0))],
        out_specs=[pl.BlockSpec((ts, D), lambda i: (i, 0)),
                   pl.BlockSpec((NDEV, ts, PC), lambda i: (0, i, 0))],
        out_shape=[jax.ShapeDtypeStruct((S, D), BF16), jax.ShapeDtypeStruct((NDEV, S, PC), F32)],
        scratch_shapes=[], args=(x, vec, win_g))


def mix_in_bwd(dproj, x, dxo, vec, win_g, tag):
    S = x.shape[0]
    ts = min(MIX_TS, S)

    def body(dp_ref, x_ref, dxo_ref, vec_ref, win_ref, dx_ref, acc_ref):
        @pl.when(pl.program_id(0) == 0)
        def _():
            acc_ref[...] = jnp.zeros_like(acc_ref)

        dh = jnp.zeros((ts, D), F32)
        for k in range(NDEV):
            dh = dh + _dot_nt(dp_ref[k], win_ref[k])
        dx, dshift, dscale, dgain = _norm_mod_bwd(dh, x_ref[...], vec_ref[3:4, :], vec_ref[1:2, :])
        dx_ref[...] = dx + dxo_ref[...]
        acc_ref[0:1, :] += dshift
        acc_ref[1:2, :] += dscale
        acc_ref[3:4, :] += dgain

    row = pl.BlockSpec((ts, D), lambda i: (i, 0))
    return pl.pallas_call(
        body, name=f"mix_in_bwd_{tag}",
        grid=(S // ts,),
        in_specs=[pl.BlockSpec((NDEV, ts, PC), lambda i: (0, i, 0)), row, row,
                  pl.BlockSpec((8, D), lambda i: (0, 0)),
                  pl.BlockSpec((NDEV, D, PC), lambda i: (0, 0, 0))],
        out_specs=[row, pl.BlockSpec((8, D), lambda i: (0, 0))],
        out_shape=[jax.ShapeDtypeStruct((S, D), F32), jax.ShapeDtypeStruct((8, D), F32)],
        compiler_params=_cparams("arbitrary"),
    )(dproj, x, dxo, vec, win_g)


SCAN_UNROLL = 4


def _shift_down(z, k, row):
    return jnp.where(row >= k, pltpu.roll(z, k, 0), 0.0)


def _shift_up(z, k, row, n):
    return jnp.where(row < n - k, pltpu.roll(z, n - k, 0), 0.0)


def _lru_gates(xc, lp_ref, wa_ref, wx_ref):
    xcb = xc.astype(BF16)
    ra = _sigmoid(_dot(xcb, wa_ref[...]) + lp_ref[5:6, :])
    ix = _sigmoid(_dot(xcb, wx_ref[...]) + lp_ref[6:7, :])
    lam = lp_ref[7:8, :]
    ls = jnp.minimum(lam, 0.0) - jnp.log(1.0 + jnp.exp(-jnp.abs(lam)))
    log_a = (RG_LRU_C * ls) * ra
    a = jnp.exp(log_a)
    mult = jnp.sqrt(-jnp.tanh(log_a) * (a * a + 1.0))
    return ra, ix, ls, a, mult


def _conv(x, lp_ref, row):
    return (lp_ref[4:5, :] + lp_ref[3:4, :] * x + lp_ref[2:3, :] * _shift_down(x, 1, row)
            + lp_ref[1:2, :] * _shift_down(x, 2, row) + lp_ref[0:1, :] * _shift_down(x, 3, row))


def lru_fwd(proj, lp, wa_t, wx_t, tag, ride=None):
    S = proj.shape[1]
    nblk = S // 8

    def body(x_ref, g_ref, lp_ref, wa_ref, wx_ref, y_ref, xc_ref, h_ref, a_s, b_s):
        x = x_ref[...]
        row = lax.broadcasted_iota(jnp.int32, x.shape, 0)
        xc = _conv(x, lp_ref, row)
        xc_ref[...] = xc
        ra, ix, ls, a, mult = _lru_gates(xc, lp_ref, wa_ref, wx_ref)
        a_s[...] = a
        b_s[...] = mult * (ix * xc)
        rowb = lax.broadcasted_iota(jnp.int32, (8, LC), 0)

        def step(i, carry):
            r0 = pl.multiple_of(i * 8, 8)
            A = a_s[pl.ds(r0, 8), :]
            B = b_s[pl.ds(r0, 8), :]
            for d in (1, 2, 4):
                m = rowb >= d
                As = jnp.where(m, pltpu.roll(A, d, 0), 1.0)
                Bs = jnp.where(m, pltpu.roll(B, d, 0), 0.0)
                B = A * Bs + B
                A = A * As
            H = B + A * carry
            h_ref[pl.ds(r0, 8), :] = H
            return H[7:8, :]

        lax.fori_loop(0, nblk, step, jnp.zeros((1, LC), F32), unroll=SCAN_UNROLL)
        y_ref[...] = h_ref[...] * _gelu(g_ref[...])

    col = pl.BlockSpec((S, LC), lambda c: (0, c))
    return _call(
        body, ride, name=f"lru_fwd_{tag}",
        grid=(LW // LC,),
        in_specs=[pl.BlockSpec((None, S, LC), lambda c: (c // 2, 0, c % 2)),
                  pl.BlockSpec((None, S, LC), lambda c: (2 + c // 2, 0, c % 2)),
                  pl.BlockSpec((8, LC), lambda c: (0, c)),
                  pl.BlockSpec((None, LC, LC), lambda c: (c, 0, 0)),
                  pl.BlockSpec((None, LC, LC), lambda c: (c, 0, 0))],
        out_specs=[col, col, col],
        out_shape=[jax.ShapeDtypeStruct((S, LW), F32)] * 3,
        scratch_shapes=[pltpu.VMEM((S, LC), F32), pltpu.VMEM((S, LC), F32)],
        args=(proj, proj, lp, wa_t, wx_t))


def lru_bwd(dy, proj, xc_all, hst, lp, wa_t, wx_t, tag):
    S = proj.shape[1]
    nblk = S // 8

    def body(dy_ref, x_ref, g_ref, xc_ref, h_ref, lp_ref, wa_ref, wx_ref,
             dx_ref, dg_ref, dlp_ref, dwa_ref, dwx_ref, c_s, l_s):
        xc = xc_ref[...]
        row = lax.broadcasted_iota(jnp.int32, xc.shape, 0)
        ra, ix, ls, a, mult = _lru_gates(xc, lp_ref, wa_ref, wx_ref)
        g = g_ref[...]
        dyv = dy_ref[...]
        h = h_ref[...]
        dg_ref[...] = (dyv * h * _gelu_grad(g)).astype(BF16)
        c_s[...] = _shift_up(a, 1, row, S)
        l_s[...] = dyv * _gelu(g)
        rowb = lax.broadcasted_iota(jnp.int32, (8, LC), 0)

        def step(i, carry):
            r0 = pl.multiple_of((nblk - 1 - i) * 8, 8)
            C = c_s[pl.ds(r0, 8), :]
            L = l_s[pl.ds(r0, 8), :]
            for d in (1, 2, 4):
                m = rowb < 8 - d
                Cs = jnp.where(m, pltpu.roll(C, 8 - d, 0), 1.0)
                Ls = jnp.where(m, pltpu.roll(L, 8 - d, 0), 0.0)
                L = C * Ls + L
                C = C * Cs
            L = L + C * carry
            l_s[pl.ds(r0, 8), :] = L
            return L[0:1, :]

        lax.fori_loop(0, nblk, step, jnp.zeros((1, LC), F32), unroll=SCAN_UNROLL)
        db = l_s[...]
        da = db * _shift_down(h, 1, row)
        ixc = ix * xc
        dmult = db * ixc
        dix = db * (mult * xc)
        dxc = db * (mult * ix)
        dlog_a = da * a - dmult * (a * a) / mult
        dra = dlog_a * (RG_LRU_C * ls)
        dls = _csum(dlog_a * ra) * RG_LRU_C
        lam = lp_ref[7:8, :]
        dlam = dls * _sigmoid(-lam)
        dpa = dra * ra * (1.0 - ra)
        dpx = dix * ix * (1.0 - ix)
        dpab = dpa.astype(BF16)
        dpxb = dpx.astype(BF16)
        xcb = xc.astype(BF16)
        dwa_ref[...] = _dot_tn(xcb, dpab)
        dwx_ref[...] = _dot_tn(xcb, dpxb)
        dxc = dxc + _dot_nt(dpab, wa_ref[...]) + _dot_nt(dpxb, wx_ref[...])
        x = x_ref[...]
        dlp_ref[0:1, :] = _csum(dxc * _shift_down(x, 3, row))
        dlp_ref[1:2, :] = _csum(dxc * _shift_down(x, 2, row))
        dlp_ref[2:3, :] = _csum(dxc * _shift_down(x, 1, row))
        dlp_ref[3:4, :] = _csum(dxc * x)
        dlp_ref[4:5, :] = _csum(dxc)
        dlp_ref[5:6, :] = _csum(dpa)
        dlp_ref[6:7, :] = _csum(dpx)
        dlp_ref[7:8, :] = dlam
        dx = (lp_ref[3:4, :] * dxc + lp_ref[2:3, :] * _shift_up(dxc, 1, row, S)
              + lp_ref[1:2, :] * _shift_up(dxc, 2, row, S) + lp_ref[0:1, :] * _shift_up(dxc, 3, row, S))
        dx_ref[...] = dx.astype(BF16)

    col = pl.BlockSpec((S, LC), lambda c: (0, c))
    pcol = pl.BlockSpec((None, S, LC), lambda c: (c // 2, 0, c % 2))
    return pl.pallas_call(
        body, name=f"lru_bwd_{tag}",
        grid=(LW // LC,),
        in_specs=[col, pcol, pl.BlockSpec((None, S, LC), lambda c: (2 + c // 2, 0, c % 2)), col, col,
                  pl.BlockSpec((8, LC), lambda c: (0, c)),
                  pl.BlockSpec((None, LC, LC), lambda c: (c, 0, 0)),
                  pl.BlockSpec((None, LC, LC), lambda c: (c, 0, 0))],
        out_specs=[pcol, pcol, pl.BlockSpec((8, LC), lambda c: (0, c)),
                   pl.BlockSpec((None, LC, LC), lambda c: (c, 0, 0)),
                   pl.BlockSpec((None, LC, LC), lambda c: (c, 0, 0))],
        out_shape=[jax.ShapeDtypeStruct((2, S, PC), BF16), jax.ShapeDtypeStruct((2, S, PC), BF16),
                   jax.ShapeDtypeStruct((8, LW), F32),
                   jax.ShapeDtypeStruct((LW // LC, LC, LC), F32), jax.ShapeDtypeStruct((LW // LC, LC, LC), F32)],
        scratch_shapes=[pltpu.VMEM((S, LC), F32), pltpu.VMEM((S, LC), F32)],
        compiler_params=_cparams("arbitrary"),
    )(dy, proj, proj, xc_all, hst, lp, wa_t, wx_t)


def _head_stack(zc, lane_head):
    return jnp.concatenate([jnp.where(lane_head == hh, zc, 0.0) for hh in range(HEADS)], axis=0).astype(BF16)


def _gmlp_fwd_parts(u, v, gp_ref, wcat_ref, bz_ref, pavg_ref, ts):
    ug = _gelu(u)
    vg = _gelu(v)
    pavg = pavg_ref[...]
    vc = vg - _seg_mean(vg, pavg)
    rs = lax.rsqrt(_seg_mean(vc * vc, pavg) + EPS)
    vhat = vc * rs
    vh = vhat * gp_ref[0:1, :]
    lane_head = lax.broadcasted_iota(jnp.int32, (CHUNK, GW), 1) // HD
    zs = []
    for n in range(ts // CHUNK):
        stack = _head_stack(vh[n * CHUNK:(n + 1) * CHUNK, :], lane_head)
        zs.append(_dot(wcat_ref[...], stack) + bz_ref[...])
    z = jnp.concatenate(zs, axis=0) if len(zs) > 1 else zs[0]
    return ug, rs, vhat, vh, z


def mix_out_fwd(proj, ylru, x, vec, gp, wcat, bz, pavg, wout_g, tag, ride=None):
    S = x.shape[0]
    ts = min(MIX_TS, S)

    def body(u_ref, v_ref, yl_ref, x_ref, vec_ref, gp_ref, wcat_ref, bz_ref, pavg_ref, wout_ref,
             xo_ref, y_ref, fo_ref):
        u = jnp.concatenate([u_ref[0], u_ref[1]], axis=1)
        v = jnp.concatenate([v_ref[0], v_ref[1]], axis=1)
        ug, _, _, _, z = _gmlp_fwd_parts(u, v, gp_ref, wcat_ref, bz_ref, pavg_ref, ts)
        n1 = _rms(yl_ref[...], gp_ref[1:2, :])
        n2 = _rms(ug * z, gp_ref[2:3, :])
        y = jnp.concatenate([n1, n2], axis=1).astype(BF16)
        y_ref[...] = y
        fo = jnp.zeros((ts, D), F32)
        for k in range(NDEV):
            fo = fo + _dot(y[:, k * OR:(k + 1) * OR], wout_ref[k])
        fo_ref[...] = fo.astype(BF16)
        xo_ref[...] = x_ref[...] + vec_ref[2:3, :] * fo

    row = pl.BlockSpec((ts, D), lambda i: (i, 0))
    full = lambda shp: pl.BlockSpec(shp, lambda i: tuple(0 for _ in shp))
    return _call(
        body, ride, name=f"mix_out_fwd_{tag}",
        grid=(S // ts,),
        in_specs=[pl.BlockSpec((2, ts, PC), lambda i: (2, i, 0)), pl.BlockSpec((2, ts, PC), lambda i: (3, i, 0)),
                  pl.BlockSpec((ts, LW), lambda i: (i, 0)), row, full((8, D)), full((8, GW)),
                  full((CHUNK, HEADS * CHUNK)), full((CHUNK, GW)), full((GW, GW)),
                  pl.BlockSpec((NDEV, OR, D), lambda i: (0, 0, 0))],
        out_specs=[row, row, row],
        out_shape=[jax.ShapeDtypeStruct((S, D), F32), jax.ShapeDtypeStruct((S, D), BF16),
                   jax.ShapeDtypeStruct((S, D), BF16)],
        scratch_shapes=[], args=(proj, proj, ylru, x, vec, gp, wcat, bz, pavg, wout_g))


def mix_out_bwd(dxo, proj, ylru, fo, vec, gp, wcat, wcat_t, bz, pavg, wout_g, tag):
    S = dxo.shape[0]
    ts = min(MIX_TS, S)

    def body(dxo_ref, u_ref, v_ref, yl_ref, fo_ref, vec_ref, gp_ref, wcat_ref, wcatt_ref, bz_ref, pavg_ref,
             wout_ref, dyo_ref, dyl_ref, duv_ref, acc_ref, dgp_ref, dwm_ref, dbz_ref):
        @pl.when(pl.program_id(0) == 0)
        def _():
            acc_ref[...] = jnp.zeros_like(acc_ref)
            dgp_ref[...] = jnp.zeros_like(dgp_ref)
            dwm_ref[...] = jnp.zeros_like(dwm_ref)
            dbz_ref[...] = jnp.zeros_like(dbz_ref)

        dxo_v = dxo_ref[...]
        acc_ref[2:3, :] += _csum(dxo_v * fo_ref[...].astype(F32))
        dyo = (vec_ref[2:3, :] * dxo_v).astype(BF16)
        dyo_ref[...] = dyo
        dn = [_dot_nt(dyo, wout_ref[k]) for k in range(NDEV)]
        dn1 = jnp.concatenate(dn[:NDEV // 2], axis=1)
        dn2 = jnp.concatenate(dn[NDEV // 2:], axis=1)
        dyl, dg1 = _rms_bwd(dn1, yl_ref[...], gp_ref[1:2, :])
        dyl_ref[...] = dyl
        u = jnp.concatenate([u_ref[0], u_ref[1]], axis=1)
        v = jnp.concatenate([v_ref[0], v_ref[1]], axis=1)
        ug, rs, vhat, vh, z = _gmlp_fwd_parts(u, v, gp_ref, wcat_ref, bz_ref, pavg_ref, ts)
        dyg, dg2 = _rms_bwd(dn2, ug * z, gp_ref[2:3, :])
        du = (dyg * z) * _gelu_grad(u)
        dz = dyg * ug
        lane_head = lax.broadcasted_iota(jnp.int32, (CHUNK, GW), 1) // HD
        vhb = vh.astype(BF16)
        dvhs = []
        dbz = jnp.zeros((CHUNK, GW), F32)
        dwm = jnp.zeros((HEADS * CHUNK, CHUNK), F32)
        for n in range(ts // CHUNK):
            dzc = dz[n * CHUNK:(n + 1) * CHUNK, :]
            dbz = dbz + dzc
            stack = _head_stack(dzc, lane_head)
            dwm = dwm + _dot_nt(stack, vhb[n * CHUNK:(n + 1) * CHUNK, :])
            dvhs.append(_dot(wcatt_ref[...], stack))
        dbz_ref[...] += dbz
        dwm_ref[...] += dwm
        dvh = jnp.concatenate(dvhs, axis=0) if len(dvhs) > 1 else dvhs[0]
        pavg = pavg_ref[...]
        dvn = _csum(dvh * vhat)
        dvhat = dvh * gp_ref[0:1, :]
        dvg = rs * (dvhat - _seg_mean(dvhat, pavg) - vhat * _seg_mean(dvhat * vhat, pavg))
        dv = dvg * _gelu_grad(v)
        duv_ref[0] = du[:, :PC].astype(BF16)
        duv_ref[1] = du[:, PC:].astype(BF16)
        duv_ref[2] = dv[:, :PC].astype(BF16)
        duv_ref[3] = dv[:, PC:].astype(BF16)
        dgp_ref[0:1, :] += dvn
        dgp_ref[1:2, :] += dg1
        dgp_ref[2:3, :] += dg2

    row = pl.BlockSpec((ts, D), lambda i: (i, 0))
    full = lambda shp: pl.BlockSpec(shp, lambda i: tuple(0 for _ in shp))
    return pl.pallas_call(
        body, name=f"mix_out_bwd_{tag}",
        grid=(S // ts,),
        in_specs=[row, pl.BlockSpec((2, ts, PC), lambda i: (2, i, 0)), pl.BlockSpec((2, ts, PC), lambda i: (3, i, 0)),
                  pl.BlockSpec((ts, LW), lambda i: (i, 0)), row, full((8, D)), full((8, GW)),
                  full((CHUNK, HEADS * CHUNK)), full((CHUNK, HEADS * CHUNK)), full((CHUNK, GW)), full((GW, GW)),
                  pl.BlockSpec((NDEV, OR, D), lambda i: (0, 0, 0))],
        out_specs=[row, pl.BlockSpec((ts, LW), lambda i: (i, 0)), pl.BlockSpec((4, ts, PC), lambda i: (0, i, 0)),
                   full((8, D)), full((8, GW)), full((HEADS * CHUNK, CHUNK)), full((CHUNK, GW))],
        out_shape=[jax.ShapeDtypeStruct((S, D), BF16), jax.ShapeDtypeStruct((S, LW), F32),
                   jax.ShapeDtypeStruct((4, S, PC), BF16), jax.ShapeDtypeStruct((8, D), F32),
                   jax.ShapeDtypeStruct((8, GW), F32), jax.ShapeDtypeStruct((HEADS * CHUNK, CHUNK), F32),
                   jax.ShapeDtypeStruct((CHUNK, GW), F32)],
        compiler_params=_cparams("arbitrary"),
    )(dxo, proj, proj, ylru, fo, vec, gp, wcat, wcat_t, bz, pavg, wout_g)


def final_loss(x, target, gain):
    S = x.shape[0]
    ts = min(512, S)

    def body(x_ref, t_ref, g_ref, loss_ref, dx_ref, dg_ref):
        @pl.when(pl.program_id(0) == 0)
        def _():
            loss_ref[...] = jnp.zeros_like(loss_ref)
            dg_ref[...] = jnp.zeros_like(dg_ref)

        xv = x_ref[...]
        gain_v = g_ref[0:1, :]
        rstd = lax.rsqrt(_rmean(xv * xv) + EPS)
        xhat = xv * rstd
        err = xhat * gain_v - t_ref[...]
        loss_ref[...] += 0.5 * _csum(_rmean(err * err))
        dy = err * (1.0 / D)
        dg_ref[0:1, :] += _csum(dy * xhat)
        dxhat = dy * gain_v
        dx_ref[...] = rstd * (dxhat - xhat * _rmean(dxhat * xhat))

    row = pl.BlockSpec((ts, D), lambda i: (i, 0))
    return pl.pallas_call(
        body, name="final_loss",
        grid=(S // ts,),
        in_specs=[row, row, pl.BlockSpec((8, D), lambda i: (0, 0))],
        out_specs=[pl.BlockSpec((8, 128), lambda i: (0, 0)), row, pl.BlockSpec((8, D), lambda i: (0, 0))],
        out_shape=[jax.ShapeDtypeStruct((8, 128), F32), jax.ShapeDtypeStruct((S, D), F32),
                   jax.ShapeDtypeStruct((8, D), F32)],
        compiler_params=_cparams("arbitrary"),
    )(x, target, gain)


def _vec(mod_l, j, gain):
    return jnp.concatenate([mod_l[3 * j:3 * j + 3], gain[None, :], jnp.zeros((4, D), F32)], axis=0)


def _block_diag_tiles(w):
    w4 = w.reshape(LW // LC, 2, HD, HD)
    eye2 = jnp.eye(2, dtype=w.dtype)
    return (w4[:, :, :, None, :] * eye2[None, :, None, :, None]).reshape(LW // LC, LC, LC).astype(BF16)


def _block_diag_extract(dw):
    d5 = dw.reshape(LW // LC, 2, HD, 2, HD)
    return jnp.einsum('cihkj,ik->cihj', d5, jnp.eye(2, dtype=dw.dtype)).reshape(HEADS, HD, HD)


def _layer_params(l, p, conv_w_full):
    lp = jnp.concatenate([conv_w_full[l], p['conv_b'][l][None], p['gate_a_b'][l].reshape(1, LW),
                          p['gate_x_b'][l].reshape(1, LW), p['lru_lambda'][l][None]], axis=0)
    gp = jnp.concatenate([p['v_norm'][l][None], p['lru_out_norm'][l][None], p['gmlp_out_norm'][l][None],
                          jnp.zeros((5, GW), F32)], axis=0)
    ws = p['spatial_w'][l] * jnp.tril(jnp.ones((CHUNK, CHUNK), F32))
    wcat = ws.transpose(1, 0, 2).reshape(CHUNK, HEADS * CHUNK).astype(BF16)
    wcat_t = ws.transpose(2, 0, 1).reshape(CHUNK, HEADS * CHUNK).astype(BF16)
    bz = jnp.repeat(p['spatial_b'][l].T, HD, axis=1)
    return dict(lp=lp, gp=gp, wcat=wcat, wcat_t=wcat_t, bz=bz,
                wa_t=_block_diag_tiles(p['gate_a_w'][l]), wx_t=_block_diag_tiles(p['gate_x_w'][l]))


def _pavg():
    return jnp.kron(jnp.eye(HEADS, dtype=F32), jnp.full((HD, HD), 1.0 / HD, F32)).astype(BF16)


GATHER_RIDES = {
    ('ffn_a', 0): [('w_in', 0), ('w_out', 0), ('gu', DEPTH)],
    ('mix_in', 0): [('down', DEPTH)],
    ('lru', 0): [('down', 1)],
    ('mix_out', 0): [('w_in', 1), ('w_out', 1)],
    ('ffn_b', 0): [('gu', 1)],
    ('ffn_a', 1): [('gu', DEPTH + 1), ('down', DEPTH + 1)],
}


def local_fwd_bwd(me_arr, x, target, mod, p, loc, gathered, conv_w_full):
    pavg = _pavg()
    g = dict(gathered)

    def ride(call, l):
        todo = GATHER_RIDES.get((call, l))
        return None if todo is None else (todo, GatherRide([(loc[kind], slot) for kind, slot in todo]))

    def run(fn, call, l, *args):
        r = ride(call, l)
        outs, got = fn(*args, ride=None if r is None else r[1])
        if r is not None:
            g.update(dict(zip(r[0], got)))
        return outs

    saved = []
    h = x
    for l in range(DEPTH):
        q = _layer_params(l, p, conv_w_full)
        v1 = _vec(mod[l], 0, p['ffn1_norm'][l])
        vm = _vec(mod[l], 1, p['mix_norm'][l])
        v2 = _vec(mod[l], 2, p['ffn2_norm'][l])
        x0 = h
        x1, h1, gu1, f1 = run(ffn_fwd, 'ffn_a', l, x0, v1, g['gu', l], g['down', l], f"a{l}")
        hm, proj = run(mix_in_fwd, 'mix_in', l, x1, vm, g['w_in', l], f"{l}")
        ylru, xc, hst = run(lru_fwd, 'lru', l, proj, q['lp'], q['wa_t'], q['wx_t'], f"{l}")
        x2, y, fo = run(mix_out_fwd, 'mix_out', l, proj, ylru, x1, vm, q['gp'], q['wcat'], q['bz'], pavg,
                        g['w_out', l], f"{l}")
        x3, h2, gu2, f2 = run(ffn_fwd, 'ffn_b', l, x2, v2, g['gu', DEPTH + l], g['down', DEPTH + l], f"b{l}")
        saved.append(dict(q=q, v1=v1, vm=vm, v2=v2, x0=x0, x1=x1, x2=x2, h1=h1, gu1=gu1, f1=f1, hm=hm, proj=proj,
                          ylru=ylru, xc=xc, hst=hst, y=y, fo=fo, h2=h2, gu2=gu2, f2=f2))
        h = x3
    fin = jnp.concatenate([p['final_norm'][None], jnp.zeros((7, D), F32)], axis=0)
    loss8, dx, dfin = final_loss(h, target, fin)
    loss = loss8[0, 0]

    big = dict(gu=None, down=None, w_in=None, w_out=None)
    small = {k: [None] * DEPTH for k in ('ffn1_norm', 'mix_norm', 'ffn2_norm', 'conv_w', 'conv_b', 'gate_a_w',
                                         'gate_a_b', 'gate_x_w', 'gate_x_b', 'lru_lambda', 'v_norm', 'spatial_w',
                                         'spatial_b', 'lru_out_norm', 'gmlp_out_norm')}
    dmod = [None] * DEPTH
    tril = jnp.tril(jnp.ones((CHUNK, CHUNK), F32))
    for l in reversed(range(DEPTH)):
        sv = saved[l]
        q = sv['q']
        dx2, dgu, a, df, acc2 = ffn_bwd(dx, sv['x2'], sv['gu2'], sv['f2'], sv['v2'],
                                        g['gu', DEPTH + l], g['down', DEPTH + l], f"b{l}")
        big['gu'] = tn_matmul_scatter(me_arr, dgu, sv['h2'][None], DEPTH + l, 2 * DEPTH, big['gu'], f"dw_gu_b{l}")
        big['down'] = tn_matmul_scatter(me_arr, a, df[None], DEPTH + l, 2 * DEPTH, big['down'], f"dw_down_b{l}", split=2)
        dyo, dylru, duv, accmo, dgp, dwm, dbz = mix_out_bwd(dx2, sv['proj'], sv['ylru'], sv['fo'], sv['vm'], q['gp'],
                                                             q['wcat'], q['wcat_t'], q['bz'], pavg, g['w_out', l], f"{l}")
        big['w_out'] = tn_matmul_scatter(me_arr, sv['y'][None], dyo[None], l, DEPTH, big['w_out'], f"dw_out_{l}",
                                         split=NDEV)
        dxl, dgl, dlp, dwa, dwx = lru_bwd(dylru, sv['proj'], sv['xc'], sv['hst'], q['lp'], q['wa_t'], q['wx_t'], f"{l}")
        dproj = jnp.concatenate([dxl, dgl, duv], axis=0)
        dx1, accmi = mix_in_bwd(dproj, sv['x1'], dx2, sv['vm'], g['w_in', l], f"{l}")
        big['w_in'] = tn_matmul_scatter(me_arr, sv['hm'][None], dproj, l, DEPTH, big['w_in'], f"dw_in_{l}")
        dx0, dgu, a, df, acc1 = ffn_bwd(dx1, sv['x0'], sv['gu1'], sv['f1'], sv['v1'],
                                        g['gu', l], g['down', l], f"a{l}")
        big['gu'] = tn_matmul_scatter(me_arr, dgu, sv['h1'][None], l, 2 * DEPTH, big['gu'], f"dw_gu_a{l}")
        big['down'] = tn_matmul_scatter(me_arr, a, df[None], l, 2 * DEPTH, big['down'], f"dw_down_a{l}", split=2)
        dx = dx0
        dmod[l] = jnp.concatenate([acc1[0:3], accmi[0:2], accmo[2:3], acc2[0:3]], axis=0)
        small['ffn1_norm'][l] = acc1[3]
        small['mix_norm'][l] = accmi[3]
        small['ffn2_norm'][l] = acc2[3]
        small['conv_w'][l] = dlp[0:4]
        small['conv_b'][l] = dlp[4]
        small['gate_a_b'][l] = dlp[5].reshape(HEADS, HD)
        small['gate_x_b'][l] = dlp[6].reshape(HEADS, HD)
        small['lru_lambda'][l] = dlp[7]
        small['gate_a_w'][l] = _block_diag_extract(dwa)
        small['gate_x_w'][l] = _block_diag_extract(dwx)
        small['v_norm'][l] = dgp[0]
        small['lru_out_norm'][l] = dgp[1]
        small['gmlp_out_norm'][l] = dgp[2]
        small['spatial_w'][l] = dwm.reshape(HEADS, CHUNK, CHUNK) * tril
        small['spatial_b'][l] = dbz.reshape(CHUNK, HEADS, HD).sum(-1).T
    small = {k: jnp.stack(v) for k, v in small.items()}
    small['final_norm'] = dfin[0]
    return loss, dx, big, small, jnp.stack(dmod)


def ada_fwd(c_all, w_ada, b_loc):
    def body(c_ref, w_ref, b_ref, mod_ref, sc_ref):
        cv = c_ref[...]
        sc = cv * _sigmoid(cv)
        sc_ref[...] = sc
        mod_ref[...] = _dot3(sc, w_ref[...]) + b_ref[...]

    return pl.pallas_call(
        body, name="ada_fwd",
        grid=(DEPTH,),
        in_specs=[pl.BlockSpec((NDEV, D), lambda l: (0, 0)), pl.BlockSpec((None, D, AC), lambda l: (l, 0, 0)),
                  pl.BlockSpec((None, 1, AC), lambda l: (l, 0, 0))],
        out_specs=[pl.BlockSpec((None, NDEV, AC), lambda l: (l, 0, 0)), pl.BlockSpec((NDEV, D), lambda l: (0, 0))],
        out_shape=[jax.ShapeDtypeStruct((DEPTH, NDEV, AC), F32), jax.ShapeDtypeStruct((NDEV, D), F32)],
        compiler_params=_cparams("arbitrary"),
    )(c_all, w_ada, b_loc)


def ada_bwd(sc_t, dmod_cols):
    def body(sc_ref, dm_ref, g_ref):
        sc = sc_ref[...]
        dm = dm_ref[...]
        acc = sc[:, 0:1] * dm[0:1, :]
        for b in range(1, NDEV):
            acc = acc + sc[:, b:b + 1] * dm[b:b + 1, :]
        g_ref[...] = acc

    return pl.pallas_call(
        body, name="ada_bwd",
        grid=(DEPTH,),
        in_specs=[pl.BlockSpec((D, NDEV), lambda l: (0, 0)), pl.BlockSpec((None, NDEV, AC), lambda l: (l, 0, 0))],
        out_specs=pl.BlockSpec((None, None, D, AC), lambda l: (0, l, 0, 0)),
        out_shape=jax.ShapeDtypeStruct((1, DEPTH, D, AC), F32),
        compiler_params=_cparams("arbitrary"),
    )(sc_t, dmod_cols)


def _row_tile(rows, cols):
    if rows * cols <= 512 * 1024:
        return rows
    for tr in (512, 384, 352, 256, 128, 64, 32, 16, 8):
        if rows % tr == 0:
            return tr
    return rows


def adamw(gparts, slot0, w, m, v, name):
    P, _, R, C = gparts.shape
    L = w.shape[0]
    tr = _row_tile(R, C)

    def body(g_ref, w_ref, m_ref, v_ref, go_ref, do_ref, mo_ref, vo_ref):
        g = g_ref[0].astype(F32)
        for p in range(1, P):
            g = g + g_ref[p].astype(F32)
        go_ref[...] = g
        mn = ADAM_B1 * m_ref[...] + (1.0 - ADAM_B1) * g
        vn = ADAM_B2 * v_ref[...] + (1.0 - ADAM_B2) * (g * g)
        mo_ref[...] = mn
        vo_ref[...] = vn
        m_hat = mn / (1.0 - ADAM_B1 ** ADAM_STEP)
        v_hat = vn / (1.0 - ADAM_B2 ** ADAM_STEP)
        do_ref[...] = -ADAM_LR * (m_hat / (jnp.sqrt(v_hat) + ADAM_EPS) + ADAM_WD * w_ref[...])

    blk = pl.BlockSpec((None, tr, C), lambda l, i: (l, i, 0))
    return pl.pallas_call(
        body, name=name,
        grid=(L, R // tr),
        in_specs=[pl.BlockSpec((P, None, tr, C), lambda l, i: (0, slot0 + l, i, 0)), blk, blk, blk],
        out_specs=[blk, blk, blk, blk],
        out_shape=[jax.ShapeDtypeStruct((L, R, C), F32)] * 4,
        compiler_params=_cparams("arbitrary", "arbitrary"),
    )(gparts, w, m, v)


def sum_parts(parts):
    P, R, C = parts.shape

    def body(p_ref, o_ref):
        acc = p_ref[0]
        for p in range(1, P):
            acc = acc + p_ref[p]
        o_ref[...] = acc

    return pl.pallas_call(
        body, name="sum_parts",
        in_specs=[pl.BlockSpec(memory_space=pltpu.VMEM)],
        out_specs=pl.BlockSpec(memory_space=pltpu.VMEM),
        out_shape=jax.ShapeDtypeStruct((R, C), F32),
    )(parts)


WEIGHTS = ['w_ada', 'b_ada', 'ffn1_norm', 'ffn1_w_gu', 'ffn1_w_down', 'mix_norm', 'w_in', 'conv_w', 'conv_b',
           'gate_a_w', 'gate_a_b', 'gate_x_w', 'gate_x_b', 'lru_lambda', 'v_norm', 'spatial_w', 'spatial_b',
           'lru_out_norm', 'gmlp_out_norm', 'w_out', 'ffn2_norm', 'ffn2_w_gu', 'ffn2_w_down', 'final_norm']
PACKED = ['b_ada', 'ffn1_norm', 'mix_norm', 'conv_b', 'gate_a_w', 'gate_a_b', 'gate_x_w', 'gate_x_b', 'lru_lambda',
          'v_norm', 'spatial_w', 'spatial_b', 'lru_out_norm', 'gmlp_out_norm', 'ffn2_norm', 'final_norm', 'conv_w']
PACK_LANES = 128
PACK_ROW_ALIGN = 8 * NDEV


def _pack(d):
    flat = jnp.concatenate([d[k].reshape(-1).astype(F32) for k in PACKED])
    rows = -(-flat.shape[0] // (PACK_LANES * PACK_ROW_ALIGN)) * PACK_ROW_ALIGN
    flat = jnp.concatenate([flat, jnp.zeros((rows * PACK_LANES - flat.shape[0],), F32)])
    return flat.reshape(rows, PACK_LANES)


def _unpack(buf, shapes):
    flat = buf.reshape(-1)
    out, off = {}, 0
    for k in PACKED:
        size = 1
        for s in shapes[k]:
            size *= s
        out[k] = flat[off:off + size].reshape(shapes[k])
        off += size
    return out


def kernel(x, c, w_ada, b_ada, ffn1_norm, ffn1_w_gu, ffn1_w_down, mix_norm, w_in, conv_w, conv_b, gate_a_w, gate_a_b, gate_x_w, gate_x_b, lru_lambda, v_norm, spatial_w, spatial_b, lru_out_norm, gmlp_out_norm, w_out, ffn2_norm, ffn2_w_gu, ffn2_w_down, final_norm, loss_target, m_w_ada, m_b_ada, m_ffn1_norm, m_ffn1_w_gu, m_ffn1_w_down, m_mix_norm, m_w_in, m_conv_w, m_conv_b, m_gate_a_w, m_gate_a_b, m_gate_x_w, m_gate_x_b, m_lru_lambda, m_v_norm, m_spatial_w, m_spatial_b, m_lru_out_norm, m_gmlp_out_norm, m_w_out, m_ffn2_norm, m_ffn2_w_gu, m_ffn2_w_down, m_final_norm, v_w_ada, v_b_ada, v_ffn1_norm, v_ffn1_w_gu, v_ffn1_w_down, v_mix_norm, v_w_in, v_conv_w, v_conv_b, v_gate_a_w, v_gate_a_b, v_gate_x_w, v_gate_x_b, v_lru_lambda, v_v_norm, v_spatial_w, v_spatial_b, v_lru_out_norm, v_gmlp_out_norm, v_w_out, v_ffn2_norm, v_ffn2_w_gu, v_ffn2_w_down, v_final_norm):
    w = dict(w_ada=w_ada, b_ada=b_ada, ffn1_norm=ffn1_norm, ffn1_w_gu=ffn1_w_gu, ffn1_w_down=ffn1_w_down, mix_norm=mix_norm, w_in=w_in, conv_w=conv_w, conv_b=conv_b, gate_a_w=gate_a_w, gate_a_b=gate_a_b, gate_x_w=gate_x_w, gate_x_b=gate_x_b, lru_lambda=lru_lambda, v_norm=v_norm, spatial_w=spatial_w, spatial_b=spatial_b, lru_out_norm=lru_out_norm, gmlp_out_norm=gmlp_out_norm, w_out=w_out, ffn2_norm=ffn2_norm, ffn2_w_gu=ffn2_w_gu, ffn2_w_down=ffn2_w_down, final_norm=final_norm)
    m = dict(w_ada=m_w_ada, b_ada=m_b_ada, ffn1_norm=m_ffn1_norm, ffn1_w_gu=m_ffn1_w_gu, ffn1_w_down=m_ffn1_w_down, mix_norm=m_mix_norm, w_in=m_w_in, conv_w=m_conv_w, conv_b=m_conv_b, gate_a_w=m_gate_a_w, gate_a_b=m_gate_a_b, gate_x_w=m_gate_x_w, gate_x_b=m_gate_x_b, lru_lambda=m_lru_lambda, v_norm=m_v_norm, spatial_w=m_spatial_w, spatial_b=m_spatial_b, lru_out_norm=m_lru_out_norm, gmlp_out_norm=m_gmlp_out_norm, w_out=m_w_out, ffn2_norm=m_ffn2_norm, ffn2_w_gu=m_ffn2_w_gu, ffn2_w_down=m_ffn2_w_down, final_norm=m_final_norm)
    v = dict(w_ada=v_w_ada, b_ada=v_b_ada, ffn1_norm=v_ffn1_norm, ffn1_w_gu=v_ffn1_w_gu, ffn1_w_down=v_ffn1_w_down, mix_norm=v_mix_norm, w_in=v_w_in, conv_w=v_conv_w, conv_b=v_conv_b, gate_a_w=v_gate_a_w, gate_a_b=v_gate_a_b, gate_x_w=v_gate_x_w, gate_x_b=v_gate_x_b, lru_lambda=v_lru_lambda, v_norm=v_v_norm, spatial_w=v_spatial_w, spatial_b=v_spatial_b, lru_out_norm=v_lru_out_norm, gmlp_out_norm=v_gmlp_out_norm, w_out=v_w_out, ffn2_norm=v_ffn2_norm, ffn2_w_gu=v_ffn2_w_gu, ffn2_w_down=v_ffn2_w_down, final_norm=v_final_norm)
    me = 4 * lax.axis_index("x") + 2 * lax.axis_index("y") + lax.axis_index("c")

    loc = dict(gu=jnp.concatenate([ffn1_w_gu, ffn2_w_gu], axis=0).astype(BF16),
               down=jnp.concatenate([ffn1_w_down, ffn2_w_down], axis=0).astype(BF16),
               w_in=w_in.astype(BF16), w_out=w_out.astype(BF16))
    c_g, conv_g, gu0, down0 = all_gather([(c, None), (conv_w, None), (loc['gu'], 0), (loc['down'], 0)], "gather_first")
    conv_w_full = conv_g.transpose(1, 2, 0, 3).reshape(DEPTH, CONV_WIDTH, LW)

    b_loc = lax.dynamic_slice(b_ada, (0, me * AC), (DEPTH, AC)).reshape(DEPTH, 1, AC)
    mod_cols, sc_all = ada_fwd(c_g.reshape(NDEV, D), w_ada, b_loc)
    (mod_rows,) = all_to_all([mod_cols.transpose(1, 0, 2)], "scatter_mod")
    mod = mod_rows.transpose(1, 0, 2).reshape(DEPTH, NMOD, D)

    small_w = {k: w[k] for k in PACKED if k != 'conv_w'}
    me_arr = jnp.reshape(me, (1,)).astype(jnp.int32)
    loss_loc, dx, big, small_g, dmod = local_fwd_bwd(me_arr, x[0], loss_target[0], mod, small_w, loc,
                                                     {('gu', 0): gu0, ('down', 0): down0}, conv_w_full)
    loss = lax.psum(loss_loc, ("x", "y", "c"))

    small_g['b_ada'] = dmod.reshape(DEPTH, NMOD * D)
    gpack = _pack(small_g)
    rows = gpack.shape[0]
    dmod_out = dmod.reshape(DEPTH, NDEV, AC).transpose(1, 0, 2)
    dmod_r, pack_r = all_to_all([dmod_out, gpack.reshape(NDEV, rows // NDEV, PACK_LANES)], "scatter_grads")
    (gsum_g,) = all_gather([(sum_parts(pack_r), None)], "gather_small_grads")
    gsum = gsum_g.reshape(1, 1, rows, PACK_LANES)

    res = {}
    t = lambda a: a.transpose(0, 2, 1)
    res['ffn1_w_gu'] = tuple(t(r) for r in adamw(big['gu'], 0, t(w['ffn1_w_gu']), t(m['ffn1_w_gu']), t(v['ffn1_w_gu']),
                                                 "adamw_gu_a"))
    res['ffn2_w_gu'] = tuple(t(r) for r in adamw(big['gu'], DEPTH, t(w['ffn2_w_gu']), t(m['ffn2_w_gu']),
                                                 t(v['ffn2_w_gu']), "adamw_gu_b"))
    res['ffn1_w_down'] = adamw(big['down'], 0, w['ffn1_w_down'], m['ffn1_w_down'], v['ffn1_w_down'], "adamw_down_a")
    res['ffn2_w_down'] = adamw(big['down'], DEPTH, w['ffn2_w_down'], m['ffn2_w_down'], v['ffn2_w_down'], "adamw_down_b")
    res['w_in'] = adamw(big['w_in'], 0, w['w_in'], m['w_in'], v['w_in'], "adamw_w_in")
    res['w_out'] = adamw(big['w_out'], 0, w['w_out'], m['w_out'], v['w_out'], "adamw_w_out")
    g_ada = ada_bwd(sc_all.T, dmod_r.transpose(1, 0, 2))
    res['w_ada'] = adamw(g_ada, 0, w['w_ada'], m['w_ada'], v['w_ada'], "adamw_w_ada")
    shapes = {k: w[k].shape for k in PACKED}
    shapes['conv_w'] = (DEPTH, CONV_WIDTH, LW)
    dummy = jnp.zeros(shapes['conv_w'], F32)
    packs = adamw(gsum, 0, _pack({**small_w, 'conv_w': dummy})[None], _pack({**{k: m[k] for k in small_w}, 'conv_w': dummy})[None],
                  _pack({**{k: v[k] for k in small_w}, 'conv_w': dummy})[None], "adamw_small")
    unpacked = [_unpack(b[0], shapes) for b in packs]
    for k in small_w:
        res[k] = tuple(u[k] for u in unpacked)
    gconv = lax.dynamic_slice(unpacked[0]['conv_w'], (0, 0, me * (LW // NDEV)), (DEPTH, CONV_WIDTH, LW // NDEV))
    cshape = (1, DEPTH * CONV_WIDTH, LW // NDEV)
    rc = adamw(gconv.reshape((1,) + cshape), 0, conv_w.reshape(cshape), m['conv_w'].reshape(cshape),
               v['conv_w'].reshape(cshape), "adamw_conv_w")
    res['conv_w'] = tuple(r.reshape(conv_w.shape) for r in rc)

    return (loss, dx[None], *[res[k][0] for k in WEIGHTS], *[res[k][1] for k in WEIGHTS],
            *[res[k][2] for k in WEIGHTS], *[res[k][3] for k in WEIGHTS])
```

```python
import jax
import jax.numpy as jnp
from jax import lax
from jax.experimental import pallas as pl
from jax.experimental.pallas import tpu as pltpu

F32 = jnp.float32
BF16 = jnp.bfloat16

NDEV = 8
DEPTH = 2
D = 1024
DFF = 2816
FC = 2 * DFF // NDEV
NCHUNK = DFF // FC
DR = DFF // NDEV
LW = 512
GW = 512
HD = 64
HEADS = 8
CHUNK = 128
PC = 2 * (LW + GW) // NDEV
OR = D // NDEV
NMOD = 9
AC = NMOD * D // NDEV
LC = 128
EPS = 1e-6
RG_LRU_C = 8.0
CONV_WIDTH = 4

ADAM_LR = 0.001
ADAM_B1 = 0.9
ADAM_B2 = 0.999
ADAM_EPS = 1e-08
ADAM_WD = 0.01
ADAM_STEP = 10

VMEM_LIMIT_BYTES = 60 * 1024 * 1024
MESH = pl.DeviceIdType.MESH
ANY = pl.BlockSpec(memory_space=pl.ANY)


def _cparams(*sem):
    return pltpu.CompilerParams(dimension_semantics=tuple(sem) if sem else None,
                                vmem_limit_bytes=VMEM_LIMIT_BYTES)


def _dot(a, b):
    return jnp.dot(a, b, preferred_element_type=F32)


def _dot_nt(a, b):
    return lax.dot_general(a, b, (((1,), (1,)), ((), ())), preferred_element_type=F32)


def _dot_tn(a, b):
    return lax.dot_general(a, b, (((0,), (0,)), ((), ())), preferred_element_type=F32)


def _split(a):
    hi = a.astype(BF16)
    lo = (a - hi.astype(F32)).astype(BF16)
    return hi, lo


def _dot3(a, b):
    ah, al = _split(a)
    bh, bl = _split(b)
    return _dot(ah, bh) + (_dot(ah, bl) + _dot(al, bh))


def _csum(a):
    return jnp.sum(a, axis=0, keepdims=True)


def _rmean(a):
    return jnp.mean(a, axis=-1, keepdims=True)


def _sigmoid(a):
    return 1.0 / (1.0 + jnp.exp(-a))


_GELU_K = 0.7978845608028654
_GELU_C = 0.044715


def _gelu(a):
    return 0.5 * a * (1.0 + jnp.tanh(_GELU_K * (a + _GELU_C * a * a * a)))


def _gelu_grad(a):
    t = jnp.tanh(_GELU_K * (a + _GELU_C * a * a * a))
    return 0.5 * (1.0 + t) + 0.5 * a * (1.0 - t * t) * (_GELU_K * (1.0 + 3.0 * _GELU_C * a * a))


def _norm_mod(x, gain, scale, shift):
    rstd = lax.rsqrt(_rmean(x * x) + EPS)
    return (x * rstd * gain) * (1.0 + scale) + shift


def _norm_mod_bwd(dh, x, gain, scale):
    rstd = lax.rsqrt(_rmean(x * x) + EPS)
    xhat = x * rstd
    dshift = _csum(dh)
    dscale = _csum(dh * (xhat * gain))
    dhn = dh * (1.0 + scale)
    dgain = _csum(dhn * xhat)
    dxhat = dhn * gain
    dx = rstd * (dxhat - xhat * _rmean(dxhat * xhat))
    return dx, dshift, dscale, dgain


def _rms(x, gain):
    rstd = lax.rsqrt(_rmean(x * x) + EPS)
    return x * rstd * gain


def _rms_bwd(dy, x, gain):
    rstd = lax.rsqrt(_rmean(x * x) + EPS)
    xhat = x * rstd
    dgain = _csum(dy * xhat)
    dxhat = dy * gain
    return rstd * (dxhat - xhat * _rmean(dxhat * xhat)), dgain


def _seg_mean(a, pavg):
    hi, lo = _split(a)
    return _dot(hi, pavg) + _dot(lo, pavg)


def _fetch_blocks(src_hbm, dst_vmem, sems, rows):
    copies = []
    for k in range(NDEV):
        dst = dst_vmem.at[k] if rows is None else dst_vmem.at[pl.ds(k * rows, rows)]
        copies.append(pltpu.make_async_copy(src_hbm.at[k], dst, sems.at[k]))
    for cp in copies:
        cp.start()
    for cp in copies:
        cp.wait()


def _place():
    return lax.axis_index("x"), lax.axis_index("y"), lax.axis_index("c")


def _slot(p):
    return 4 * p[0] + 2 * p[1] + p[2]


class GatherRide:
    def __init__(self, srcs):
        self.n = len(srcs)
        self.index = [i for _, i in srcs]
        self.args = [a for a, _ in srcs]
        self.out_shape = [jax.ShapeDtypeStruct((NDEV,) + (a.shape if i is None else a.shape[1:]), a.dtype)
                          for a, i in srcs]
        self.scratch = [pltpu.SemaphoreType.DMA((self.n, NDEV - 1)), pltpu.SemaphoreType.DMA((self.n, NDEV - 1)),
                        pltpu.SemaphoreType.DMA((self.n,))]

    def hooks(self, ins, outs, sems):
        send_sems, recv_sems, local_sems = sems
        n = self.n
        x, y, c = _place()
        me, sibling = (x, y, c), (x, y, 1 - c)
        chips = [(1 - x, y), (x, 1 - y), (1 - x, 1 - y)]

        def local(a):
            return ins[a] if self.index[a] is None else ins[a].at[self.index[a]]

        def copy(a, k, block, to, src=None):
            dst = outs[a].at[_slot(block)]
            return pltpu.make_async_remote_copy(
                src_ref=dst if src is None else src, dst_ref=dst,
                send_sem=send_sems.at[a, k], recv_sem=recv_sems.at[a, k],
                device_id=to, device_id_type=MESH)

        def mine():
            return [pltpu.make_async_copy(local(a), outs[a].at[_slot(me)], local_sems.at[a]) for a in range(n)]

        def first():
            cps = []
            for a in range(n):
                cps.append(copy(a, 0, me, sibling, src=local(a)))
                cps += [copy(a, 1 + j, me, (*chip, c), src=local(a)) for j, chip in enumerate(chips)]
            return cps

        def passed():
            return [copy(a, 4 + j, (*chip, c), sibling) for j, chip in enumerate(chips) for a in range(n)]

        def start():
            for cp in mine() + first():
                cp.start()

        def mid():
            for j, chip in enumerate(chips):
                for a in range(n):
                    copy(a, 1 + j, (*chip, c), me).wait_recv()
                    copy(a, 4 + j, (*chip, c), sibling).start()

        def finish():
            for a in range(n):
                copy(a, 0, sibling, me).wait_recv()
                for j, chip in enumerate(chips):
                    copy(a, 4 + j, (*chip, 1 - c), me).wait_recv()
            for cp in first() + passed():
                cp.wait_send()
            for cp in mine():
                cp.wait()

        return start, mid, finish


def all_gather(srcs, name):
    ride = GatherRide(srcs)
    n = ride.n

    def body(*refs):
        start, mid, finish = ride.hooks(refs[:n], refs[n:2 * n], refs[2 * n:])
        start()
        mid()
        finish()

    return pl.pallas_call(
        body, name=name,
        in_specs=[ANY] * n, out_specs=[ANY] * n, out_shape=ride.out_shape, scratch_shapes=ride.scratch,
    )(*ride.args)


def _call(core, ride, *, name, grid, in_specs, out_specs, out_shape, scratch_shapes, args):
    if ride is None:
        outs = pl.pallas_call(core, name=name, grid=grid, in_specs=in_specs, out_specs=out_specs,
                              out_shape=out_shape, scratch_shapes=scratch_shapes,
                              compiler_params=_cparams("arbitrary"))(*args)
        return outs, []
    n_in, n_out, n_sc, n = len(in_specs), len(out_shape), len(scratch_shapes), ride.n
    nsteps = grid[0]
    mid_step = max(nsteps - 2, 0)

    def body(*refs):
        cuts = [n_in, n_in + n, n_in + n + n_out, n_in + 2 * n + n_out, n_in + 2 * n + n_out + n_sc]
        ci, ri, co, ro, cs, rs = (refs[a:b] for a, b in zip([0] + cuts, cuts + [len(refs)]))
        start, mid, finish = ride.hooks(ri, ro, rs)
        i = pl.program_id(0)
        pl.when(i == 0)(start)
        core(*ci, *co, *cs)
        pl.when(i == mid_step)(mid)
        pl.when(i == nsteps - 1)(finish)

    outs = pl.pallas_call(
        body, name=name, grid=grid,
        in_specs=list(in_specs) + [ANY] * n, out_specs=list(out_specs) + [ANY] * n,
        out_shape=list(out_shape) + ride.out_shape, scratch_shapes=list(scratch_shapes) + ride.scratch,
        compiler_params=_cparams("arbitrary"))(*args, *ride.args)
    return outs[:n_out], outs[n_out:]


def all_to_all(arrs, name):
    n = len(arrs)

    def body(*refs):
        ins, outs = refs[:n], refs[n:2 * n]
        send_sems, recv_sems, local_sems = refs[2 * n:]
        x, y, c = _place()
        me = (x, y, c)

        def peer(k):
            return (1 - x if k & 4 else x, 1 - y if k & 2 else y, 1 - c if k & 1 else c)

        def copy(a, k):
            return pltpu.make_async_remote_copy(
                src_ref=ins[a].at[_slot(peer(k))], dst_ref=outs[a].at[_slot(me)],
                send_sem=send_sems.at[a, k - 1], recv_sem=recv_sems.at[a, k - 1],
                device_id=peer(k), device_id_type=MESH)

        def landing(a, k):
            return pltpu.make_async_remote_copy(
                src_ref=outs[a].at[_slot(peer(k))], dst_ref=outs[a].at[_slot(peer(k))],
                send_sem=send_sems.at[a, k - 1], recv_sem=recv_sems.at[a, k - 1],
                device_id=me, device_id_type=MESH)

        mine = [pltpu.make_async_copy(ins[a].at[_slot(me)], outs[a].at[_slot(me)], local_sems.at[a]) for a in range(n)]
        for cp in mine:
            cp.start()
        sends = [copy(a, k) for a in range(n) for k in range(1, NDEV)]
        for cp in sends:
            cp.start()
        for a in range(n):
            for k in range(1, NDEV):
                landing(a, k).wait_recv()
        for cp in sends:
            cp.wait_send()
        for cp in mine:
            cp.wait()

    return pl.pallas_call(
        body, name=name,
        in_specs=[ANY] * n, out_specs=[ANY] * n,
        out_shape=[jax.ShapeDtypeStruct(a.shape, a.dtype) for a in arrs],
        scratch_shapes=[pltpu.SemaphoreType.DMA((n, NDEV - 1)), pltpu.SemaphoreType.DMA((n, NDEV - 1)),
                        pltpu.SemaphoreType.DMA((n,))],
    )(*arrs)


FFN_TS = 256


def ffn_fwd(x, vec, wgu_g, wdown_g, tag, ride=None):
    S = x.shape[0]
    ts = min(FFN_TS, S)

    def body(x_ref, vec_ref, wgu_hbm, wd_hbm, xo_ref, h_ref, gu_ref, f_ref, wgu_v, wd_v, sems):
        @pl.when(pl.program_id(0) == 0)
        def _():
            _fetch_blocks(wgu_hbm, wgu_v, sems.at[0], None)
            _fetch_blocks(wd_hbm, wd_v, sems.at[1], DR)

        xv = x_ref[...]
        h = _norm_mod(xv, vec_ref[3:4, :], vec_ref[1:2, :], vec_ref[0:1, :]).astype(BF16)
        h_ref[...] = h
        acc = jnp.zeros((ts, D), F32)
        for j in range(NCHUNK):
            g = _dot(h, wgu_v[j])
            u = _dot(h, wgu_v[NCHUNK + j])
            gu_ref[j] = g.astype(BF16)
            gu_ref[NCHUNK + j] = u.astype(BF16)
            a = (g * _sigmoid(g) * u).astype(BF16)
            acc = acc + _dot(a, wd_v[pl.ds(j * FC, FC), :])
        f_ref[...] = acc.astype(BF16)
        xo_ref[...] = xv + (0.5 * vec_ref[2:3, :]) * acc

    return _call(
        body, ride, name=f"ffn_fwd_{tag}",
        grid=(S // ts,),
        in_specs=[pl.BlockSpec((ts, D), lambda i: (i, 0)),
                  pl.BlockSpec((8, D), lambda i: (0, 0)), ANY, ANY],
        out_specs=[pl.BlockSpec((ts, D), lambda i: (i, 0)),
                   pl.BlockSpec((ts, D), lambda i: (i, 0)),
                   pl.BlockSpec((NDEV, ts, FC), lambda i: (0, i, 0)),
                   pl.BlockSpec((ts, D), lambda i: (i, 0))],
        out_shape=[jax.ShapeDtypeStruct((S, D), F32), jax.ShapeDtypeStruct((S, D), BF16),
                   jax.ShapeDtypeStruct((NDEV, S, FC), BF16), jax.ShapeDtypeStruct((S, D), BF16)],
        scratch_shapes=[pltpu.VMEM((NDEV, D, FC), BF16), pltpu.VMEM((DFF, D), BF16),
                        pltpu.SemaphoreType.DMA((2, NDEV))],
        args=(x, vec, wgu_g, wdown_g))


def ffn_bwd(dxo, x, gu, f, vec, wgu_g, wdown_g, tag):
    S = x.shape[0]
    ts = min(FFN_TS, S)

    def body(dxo_ref, x_ref, gu_ref, f_ref, vec_ref, wgu_hbm, wd_hbm,
             dx_ref, dgu_ref, a_ref, df_ref, acc_ref, wgu_v, wd_v, sems):
        @pl.when(pl.program_id(0) == 0)
        def _():
            _fetch_blocks(wgu_hbm, wgu_v, sems.at[0], None)
            _fetch_blocks(wd_hbm, wd_v, sems.at[1], DR)
            acc_ref[...] = jnp.zeros_like(acc_ref)

        dxo_v = dxo_ref[...]
        dgate = 0.5 * _csum(dxo_v * f_ref[...].astype(F32))
        df = ((0.5 * vec_ref[2:3, :]) * dxo_v).astype(BF16)
        df_ref[...] = df
        dh = jnp.zeros((ts, D), F32)
        for j in range(NCHUNK):
            da = _dot_nt(df, wd_v[pl.ds(j * FC, FC), :])
            g = gu_ref[j].astype(F32)
            u = gu_ref[NCHUNK + j].astype(F32)
            sg = _sigmoid(g)
            si = g * sg
            a_ref[j] = (si * u).astype(BF16)
            dg = (da * u * (sg * (1.0 + g * (1.0 - sg)))).astype(BF16)
            du = (da * si).astype(BF16)
            dgu_ref[j] = dg
            dgu_ref[NCHUNK + j] = du
            dh = dh + _dot_nt(dg, wgu_v[j]) + _dot_nt(du, wgu_v[NCHUNK + j])
        dx, dshift, dscale, dgain = _norm_mod_bwd(dh, x_ref[...], vec_ref[3:4, :], vec_ref[1:2, :])
        dx_ref[...] = dx + dxo_v
        acc_ref[0:1, :] += dshift
        acc_ref[1:2, :] += dscale
        acc_ref[2:3, :] += dgate
        acc_ref[3:4, :] += dgain

    row = pl.BlockSpec((ts, D), lambda i: (i, 0))
    return pl.pallas_call(
        body, name=f"ffn_bwd_{tag}",
        grid=(S // ts,),
        in_specs=[row, row, pl.BlockSpec((NDEV, ts, FC), lambda i: (0, i, 0)), row,
                  pl.BlockSpec((8, D), lambda i: (0, 0)), ANY, ANY],
        out_specs=[row, pl.BlockSpec((NDEV, ts, FC), lambda i: (0, i, 0)),
                   pl.BlockSpec((NCHUNK, ts, FC), lambda i: (0, i, 0)), row,
                   pl.BlockSpec((8, D), lambda i: (0, 0))],
        out_shape=[jax.ShapeDtypeStruct((S, D), F32), jax.ShapeDtypeStruct((NDEV, S, FC), BF16),
                   jax.ShapeDtypeStruct((NCHUNK, S, FC), BF16), jax.ShapeDtypeStruct((S, D), BF16),
                   jax.ShapeDtypeStruct((8, D), F32)],
        scratch_shapes=[pltpu.VMEM((NDEV, D, FC), BF16), pltpu.VMEM((DFF, D), BF16),
                        pltpu.SemaphoreType.DMA((2, NDEV))],
        compiler_params=_cparams("arbitrary"),
    )(dxo, x, gu, f, vec, wgu_g, wdown_g)


NCHIP = NDEV // 2


def tn_matmul_scatter(me_arr, a, b, slot, nslots, prev, name, split=1):
    na, S, M = a.shape
    nb, _, N = b.shape
    ncall = NDEV // split
    ts = min(4096, S)
    nsteps = S // ts
    mp = M // split
    other_step = {1: lambda j: 2 * j, 2: lambda j: j, 8: lambda j: 0}[split]
    mine_step = {1: lambda j: 2 * j + 1, 2: lambda j: j, 8: lambda j: 0}[split]

    def group(k, me_ref):
        if split == 1:
            return jnp.bitwise_xor(me_ref[0], NDEV - 1 - k)
        if split == 2:
            return jnp.bitwise_xor(me_ref[0] // 2, NCHIP - 1 - k)
        return 0

    def body(me_ref, *refs):
        a_ref, b_ref = refs[0], refs[1]
        recv_ref, acc, sb_other, sb_mine, land, d2d_send, d2d_recv, ici_send, ici_recv = refs[-9:]
        k = pl.program_id(0)
        s = pl.program_id(1)
        x, y, c = _place()
        my_chip = 2 * x + y

        def chip_of(j):
            if split == 8:
                cx, cy = j // 2, j % 2
            else:
                flip = NCHIP - 1 - j
                cx, cy = (1 - x if flip & 2 else x), (1 - y if flip & 1 else y)
            return cx, cy, 2 * cx + cy

        def piece(j, core):
            if split == 1:
                return acc[...]
            start = core * mp if split == 2 else (2 * j + core) * mp
            return acc[pl.ds(pl.multiple_of(start, 8), mp), :]

        def to_sibling(j):
            return pltpu.make_async_remote_copy(
                src_ref=sb_other.at[j], dst_ref=land.at[j], send_sem=d2d_send.at[j], recv_sem=d2d_recv.at[j],
                device_id=(x, y, 1 - c), device_id_type=MESH)

        def to_owner(j):
            cx, cy, ci = chip_of(j)
            dst = recv_ref.at[my_chip, slot]
            return ci, pltpu.make_async_copy(sb_mine.at[j], dst, ici_send.at[j]), pltpu.make_async_remote_copy(
                src_ref=sb_mine.at[j], dst_ref=dst, send_sem=ici_send.at[j], recv_sem=ici_recv.at[my_chip],
                device_id=(cx, cy, c), device_id_type=MESH)

        if nsteps == 1:
            acc[...] = _dot_tn(a_ref[...], b_ref[...])
        else:
            @pl.when(s == 0)
            def _():
                acc[...] = jnp.zeros_like(acc)

            acc[...] += _dot_tn(a_ref[...], b_ref[...])

        for kk in range(ncall):
            @pl.when((s == nsteps - 1) & (k == kk))
            def _():
                for j in range(NCHIP):
                    if other_step(j) == kk:
                        sb_other[j] = piece(j, 1 - c).astype(BF16)
                        to_sibling(j).start()
                for j in range(NCHIP):
                    if mine_step(j) == kk:
                        to_sibling(j).wait_recv()
                        sb_mine[j] = (piece(j, c) + land[j].astype(F32)).astype(BF16)
                        ci, loc, rem = to_owner(j)
                        pl.when(ci == my_chip)(loc.start)
                        pl.when(ci != my_chip)(rem.start)

        @pl.when((s == nsteps - 1) & (k == ncall - 1))
        def _():
            for j in range(NCHIP):
                to_sibling(j).wait_send()
                ci, loc, rem = to_owner(j)
                pl.when(ci == my_chip)(loc.wait)
                pl.when(ci != my_chip)(rem.wait_send)
            for src in range(NCHIP):
                @pl.when(my_chip != src)
                def _():
                    pltpu.make_async_remote_copy(
                        src_ref=recv_ref.at[src, slot], dst_ref=recv_ref.at[src, slot],
                        send_sem=ici_send.at[src], recv_sem=ici_recv.at[src],
                        device_id=(src // 2, src % 2, c), device_id_type=MESH).wait_recv()

    in_specs = [pl.BlockSpec((None, ts, M), (lambda k, s, me: (group(k, me), s, 0)) if na > 1 else (lambda k, s, me: (0, s, 0))),
                pl.BlockSpec((None, ts, N), (lambda k, s, me: (group(k, me), s, 0)) if nb > 1 else (lambda k, s, me: (0, s, 0)))]
    args = [me_arr, a, b]
    aliases = {}
    if prev is not None:
        in_specs.append(ANY)
        args.append(prev)
        aliases = {3: 0}
    return pl.pallas_call(
        body, name=name,
        grid_spec=pltpu.PrefetchScalarGridSpec(
            num_scalar_prefetch=1, grid=(ncall, nsteps), in_specs=in_specs, out_specs=ANY,
            scratch_shapes=[pltpu.VMEM((M, N), F32), pltpu.VMEM((NCHIP, mp, N), BF16), pltpu.VMEM((NCHIP, mp, N), BF16),
                            pltpu.VMEM((NCHIP, mp, N), BF16), pltpu.SemaphoreType.DMA((NCHIP,)),
                            pltpu.SemaphoreType.DMA((NCHIP,)), pltpu.SemaphoreType.DMA((NCHIP,)),
                            pltpu.SemaphoreType.DMA((NCHIP,))]),
        out_shape=jax.ShapeDtypeStruct((NCHIP, nslots, mp, N), BF16),
        input_output_aliases=aliases,
        compiler_params=_cparams("arbitrary", "arbitrary"),
    )(*args)


MIX_TS = 256


def mix_in_fwd(x, vec, win_g, tag, ride=None):
    S = x.shape[0]
    ts = min(MIX_TS, S)

    def body(x_ref, vec_ref, win_ref, hm_ref, proj_ref):
        h = _norm_mod(x_ref[...], vec_ref[3:4, :], vec_ref[1:2, :], vec_ref[0:1, :]).astype(BF16)
        hm_ref[...] = h
        for k in range(NDEV):
            proj_ref[k] = _dot(h, win_ref[k])

    return _call(
        body, ride, name=f"mix_in_fwd_{tag}",
        grid=(S // ts,),
        in_specs=[pl.BlockSpec((ts, D), lambda i: (i, 0)), pl.BlockSpec((8, D), lambda i: (0, 0)),
                  pl.BlockSpec((NDEV, D, PC), lambda i: (0, 0, 0))],
        out_specs=[pl.BlockSpec((ts, D), lambda i: (i, 0)),
                   pl.BlockSpec((NDEV, ts, PC), lambda i: (0, i, 0))],
        out_shape=[jax.ShapeDtypeStruct((S, D), BF16), jax.ShapeDtypeStruct((NDEV, S, PC), F32)],
        scratch_shapes=[], args=(x, vec, win_g))


def mix_in_bwd(dproj, x, dxo, vec, win_g, tag):
    S = x.shape[0]
    ts = min(MIX_TS, S)

    def body(dp_ref, x_ref, dxo_ref, vec_ref, win_ref, dx_ref, acc_ref):
        @pl.when(pl.program_id(0) == 0)
        def _():
            acc_ref[...] = jnp.zeros_like(acc_ref)

        dh = jnp.zeros((ts, D), F32)
        for k in range(NDEV):
            dh = dh + _dot_nt(dp_ref[k], win_ref[k])
        dx, dshift, dscale, dgain = _norm_mod_bwd(dh, x_ref[...], vec_ref[3:4, :], vec_ref[1:2, :])
        dx_ref[...] = dx + dxo_ref[...]
        acc_ref[0:1, :] += dshift
        acc_ref[1:2, :] += dscale
        acc_ref[3:4, :] += dgain

    row = pl.BlockSpec((ts, D), lambda i: (i, 0))
    return pl.pallas_call(
        body, name=f"mix_in_bwd_{tag}",
        grid=(S // ts,),
        in_specs=[pl.BlockSpec((NDEV, ts, PC), lambda i: (0, i, 0)), row, row,
                  pl.BlockSpec((8, D), lambda i: (0, 0)),
                  pl.BlockSpec((NDEV, D, PC), lambda i: (0, 0, 0))],
        out_specs=[row, pl.BlockSpec((8, D), lambda i: (0, 0))],
        out_shape=[jax.ShapeDtypeStruct((S, D), F32), jax.ShapeDtypeStruct((8, D), F32)],
        compiler_params=_cparams("arbitrary"),
    )(dproj, x, dxo, vec, win_g)


SCAN_UNROLL = 4


def _shift_down(z, k, row):
    return jnp.where(row >= k, pltpu.roll(z, k, 0), 0.0)


def _shift_up(z, k, row, n):
    return jnp.where(row < n - k, pltpu.roll(z, n - k, 0), 0.0)


def _lru_gates(xc, lp_ref, wa_ref, wx_ref):
    xcb = xc.astype(BF16)
    ra = _sigmoid(_dot(xcb, wa_ref[...]) + lp_ref[5:6, :])
    ix = _sigmoid(_dot(xcb, wx_ref[...]) + lp_ref[6:7, :])
    lam = lp_ref[7:8, :]
    ls = jnp.minimum(lam, 0.0) - jnp.log(1.0 + jnp.exp(-jnp.abs(lam)))
    log_a = (RG_LRU_C * ls) * ra
    a = jnp.exp(log_a)
    mult = jnp.sqrt(-jnp.tanh(log_a) * (a * a + 1.0))
    return ra, ix, ls, a, mult


def _conv(x, lp_ref, row):
    return (lp_ref[4:5, :] + lp_ref[3:4, :] * x + lp_ref[2:3, :] * _shift_down(x, 1, row)
            + lp_ref[1:2, :] * _shift_down(x, 2, row) + lp_ref[0:1, :] * _shift_down(x, 3, row))


def lru_fwd(proj, lp, wa_t, wx_t, tag, ride=None):
    S = proj.shape[1]
    nblk = S // 8

    def body(x_ref, g_ref, lp_ref, wa_ref, wx_ref, y_ref, xc_ref, h_ref, a_s, b_s):
        x = x_ref[...]
        row = lax.broadcasted_iota(jnp.int32, x.shape, 0)
        xc = _conv(x, lp_ref, row)
        xc_ref[...] = xc
        ra, ix, ls, a, mult = _lru_gates(xc, lp_ref, wa_ref, wx_ref)
        a_s[...] = a
        b_s[...] = mult * (ix * xc)
        rowb = lax.broadcasted_iota(jnp.int32, (8, LC), 0)

        def step(i, carry):
            for q in range(SCAN_UNROLL):
                r0 = pl.multiple_of((i * SCAN_UNROLL + q) * 8, 8)
                A = a_s[pl.ds(r0, 8), :]
                B = b_s[pl.ds(r0, 8), :]
                for d in (1, 2, 4):
                    m = rowb >= d
                    As = jnp.where(m, pltpu.roll(A, d, 0), 1.0)
                    Bs = jnp.where(m, pltpu.roll(B, d, 0), 0.0)
                    B = A * Bs + B
                    A = A * As
                H = B + A * carry
                h_ref[pl.ds(r0, 8), :] = H
                carry = H[7:8, :]
            return carry

        lax.fori_loop(0, nblk // SCAN_UNROLL, step, jnp.zeros((1, LC), F32))
        y_ref[...] = h_ref[...] * _gelu(g_ref[...])

    col = pl.BlockSpec((S, LC), lambda c: (0, c))
    return _call(
        body, ride, name=f"lru_fwd_{tag}",
        grid=(LW // LC,),
        in_specs=[pl.BlockSpec((None, S, LC), lambda c: (c // 2, 0, c % 2)),
                  pl.BlockSpec((None, S, LC), lambda c: (2 + c // 2, 0, c % 2)),
                  pl.BlockSpec((8, LC), lambda c: (0, c)),
                  pl.BlockSpec((None, LC, LC), lambda c: (c, 0, 0)),
                  pl.BlockSpec((None, LC, LC), lambda c: (c, 0, 0))],
        out_specs=[col, col, col],
        out_shape=[jax.ShapeDtypeStruct((S, LW), F32)] * 3,
        scratch_shapes=[pltpu.VMEM((S, LC), F32), pltpu.VMEM((S, LC), F32)],
        args=(proj, proj, lp, wa_t, wx_t))


def lru_bwd(dy, proj, xc_all, hst, lp, wa_t, wx_t, tag):
    S = proj.shape[1]
    nblk = S // 8

    def body(dy_ref, x_ref, g_ref, xc_ref, h_ref, lp_ref, wa_ref, wx_ref,
             dx_ref, dg_ref, dlp_ref, dwa_ref, dwx_ref, c_s, l_s):
        xc = xc_ref[...]
        row = lax.broadcasted_iota(jnp.int32, xc.shape, 0)
        ra, ix, ls, a, mult = _lru_gates(xc, lp_ref, wa_ref, wx_ref)
        g = g_ref[...]
        dyv = dy_ref[...]
        h = h_ref[...]
        dg_ref[...] = (dyv * h * _gelu_grad(g)).astype(BF16)
        c_s[...] = _shift_up(a, 1, row, S)
        l_s[...] = dyv * _gelu(g)
        rowb = lax.broadcasted_iota(jnp.int32, (8, LC), 0)

        def step(i, carry):
            for q in range(SCAN_UNROLL):
                r0 = pl.multiple_of((nblk - 1 - (i * SCAN_UNROLL + q)) * 8, 8)
                C = c_s[pl.ds(r0, 8), :]
                L = l_s[pl.ds(r0, 8), :]
                for d in (1, 2, 4):
                    m = rowb < 8 - d
                    Cs = jnp.where(m, pltpu.roll(C, 8 - d, 0), 1.0)
                    Ls = jnp.where(m, pltpu.roll(L, 8 - d, 0), 0.0)
                    L = C * Ls + L
                    C = C * Cs
                L = L + C * carry
                l_s[pl.ds(r0, 8), :] = L
                carry = L[0:1, :]
            return carry

        lax.fori_loop(0, nblk // SCAN_UNROLL, step, jnp.zeros((1, LC), F32))
        db = l_s[...]
        da = db * _shift_down(h, 1, row)
        ixc = ix * xc
        dmult = db * ixc
        dix = db * (mult * xc)
        dxc = db * (mult * ix)
        dlog_a = da * a - dmult * (a * a) / mult
        dra = dlog_a * (RG_LRU_C * ls)
        dls = _csum(dlog_a * ra) * RG_LRU_C
        lam = lp_ref[7:8, :]
        dlam = dls * _sigmoid(-lam)
        dpa = dra * ra * (1.0 - ra)
        dpx = dix * ix * (1.0 - ix)
        dpab = dpa.astype(BF16)
        dpxb = dpx.astype(BF16)
        xcb = xc.astype(BF16)
        dwa_ref[...] = _dot_tn(xcb, dpab)
        dwx_ref[...] = _dot_tn(xcb, dpxb)
        dxc = dxc + _dot_nt(dpab, wa_ref[...]) + _dot_nt(dpxb, wx_ref[...])
        x = x_ref[...]
        dlp_ref[0:1, :] = _csum(dxc * _shift_down(x, 3, row))
        dlp_ref[1:2, :] = _csum(dxc * _shift_down(x, 2, row))
        dlp_ref[2:3, :] = _csum(dxc * _shift_down(x, 1, row))
        dlp_ref[3:4, :] = _csum(dxc * x)
        dlp_ref[4:5, :] = _csum(dxc)
        dlp_ref[5:6, :] = _csum(dpa)
        dlp_ref[6:7, :] = _csum(dpx)
        dlp_ref[7:8, :] = dlam
        dx = (lp_ref[3:4, :] * dxc + lp_ref[2:3, :] * _shift_up(dxc, 1, row, S)
              + lp_ref[1:2, :] * _shift_up(dxc, 2, row, S) + lp_ref[0:1, :] * _shift_up(dxc, 3, row, S))
        dx_ref[...] = dx.astype(BF16)

    col = pl.BlockSpec((S, LC), lambda c: (0, c))
    pcol = pl.BlockSpec((None, S, LC), lambda c: (c // 2, 0, c % 2))
    return pl.pallas_call(
        body, name=f"lru_bwd_{tag}",
        grid=(LW // LC,),
        in_specs=[col, pcol, pl.BlockSpec((None, S, LC), lambda c: (2 + c // 2, 0, c % 2)), col, col,
                  pl.BlockSpec((8, LC), lambda c: (0, c)),
                  pl.BlockSpec((None, LC, LC), lambda c: (c, 0, 0)),
                  pl.BlockSpec((None, LC, LC), lambda c: (c, 0, 0))],
        out_specs=[pcol, pcol, pl.BlockSpec((8, LC), lambda c: (0, c)),
                   pl.BlockSpec((None, LC, LC), lambda c: (c, 0, 0)),
                   pl.BlockSpec((None, LC, LC), lambda c: (c, 0, 0))],
        out_shape=[jax.ShapeDtypeStruct((2, S, PC), BF16), jax.ShapeDtypeStruct((2, S, PC), BF16),
                   jax.ShapeDtypeStruct((8, LW), F32),
                   jax.ShapeDtypeStruct((LW // LC, LC, LC), F32), jax.ShapeDtypeStruct((LW // LC, LC, LC), F32)],
        scratch_shapes=[pltpu.VMEM((S, LC), F32), pltpu.VMEM((S, LC), F32)],
        compiler_params=_cparams("arbitrary"),
    )(dy, proj, proj, xc_all, hst, lp, wa_t, wx_t)


def _head_stack(zc, lane_head):
    return jnp.concatenate([jnp.where(lane_head == hh, zc, 0.0) for hh in range(HEADS)], axis=0).astype(BF16)


def _gmlp_fwd_parts(u, v, gp_ref, wcat_ref, bz_ref, pavg_ref, ts):
    ug = _gelu(u)
    vg = _gelu(v)
    pavg = pavg_ref[...]
    vc = vg - _seg_mean(vg, pavg)
    rs = lax.rsqrt(_seg_mean(vc * vc, pavg) + EPS)
    vhat = vc * rs
    vh = vhat * gp_ref[0:1, :]
    lane_head = lax.broadcasted_iota(jnp.int32, (CHUNK, GW), 1) // HD
    zs = []
    for n in range(ts // CHUNK):
        stack = _head_stack(vh[n * CHUNK:(n + 1) * CHUNK, :], lane_head)
        zs.append(_dot(wcat_ref[...], stack) + bz_ref[...])
    z = jnp.concatenate(zs, axis=0) if len(zs) > 1 else zs[0]
    return ug, rs, vhat, vh, z


def mix_out_fwd(proj, ylru, x, vec, gp, wcat, bz, pavg, wout_g, tag, ride=None):
    S = x.shape[0]
    ts = min(MIX_TS, S)

    def body(u_ref, v_ref, yl_ref, x_ref, vec_ref, gp_ref, wcat_ref, bz_ref, pavg_ref, wout_ref,
             xo_ref, y_ref, fo_ref):
        u = jnp.concatenate([u_ref[0], u_ref[1]], axis=1)
        v = jnp.concatenate([v_ref[0], v_ref[1]], axis=1)
        ug, _, _, _, z = _gmlp_fwd_parts(u, v, gp_ref, wcat_ref, bz_ref, pavg_ref, ts)
        n1 = _rms(yl_ref[...], gp_ref[1:2, :])
        n2 = _rms(ug * z, gp_ref[2:3, :])
        y = jnp.concatenate([n1, n2], axis=1).astype(BF16)
        y_ref[...] = y
        fo = jnp.zeros((ts, D), F32)
        for k in range(NDEV):
            fo = fo + _dot(y[:, k * OR:(k + 1) * OR], wout_ref[k])
        fo_ref[...] = fo.astype(BF16)
        xo_ref[...] = x_ref[...] + vec_ref[2:3, :] * fo

    row = pl.BlockSpec((ts, D), lambda i: (i, 0))
    full = lambda shp: pl.BlockSpec(shp, lambda i: tuple(0 for _ in shp))
    return _call(
        body, ride, name=f"mix_out_fwd_{tag}",
        grid=(S // ts,),
        in_specs=[pl.BlockSpec((2, ts, PC), lambda i: (2, i, 0)), pl.BlockSpec((2, ts, PC), lambda i: (3, i, 0)),
                  pl.BlockSpec((ts, LW), lambda i: (i, 0)), row, full((8, D)), full((8, GW)),
                  full((CHUNK, HEADS * CHUNK)), full((CHUNK, GW)), full((GW, GW)),
                  pl.BlockSpec((NDEV, OR, D), lambda i: (0, 0, 0))],
        out_specs=[row, row, row],
        out_shape=[jax.ShapeDtypeStruct((S, D), F32), jax.ShapeDtypeStruct((S, D), BF16),
                   jax.ShapeDtypeStruct((S, D), BF16)],
        scratch_shapes=[], args=(proj, proj, ylru, x, vec, gp, wcat, bz, pavg, wout_g))


def mix_out_bwd(dxo, proj, ylru, fo, vec, gp, wcat, wcat_t, bz, pavg, wout_g, tag):
    S = dxo.shape[0]
    ts = min(MIX_TS, S)

    def body(dxo_ref, u_ref, v_ref, yl_ref, fo_ref, vec_ref, gp_ref, wcat_ref, wcatt_ref, bz_ref, pavg_ref,
             wout_ref, dyo_ref, dyl_ref, duv_ref, acc_ref, dgp_ref, dwm_ref, dbz_ref):
        @pl.when(pl.program_id(0) == 0)
        def _():
            acc_ref[...] = jnp.zeros_like(acc_ref)
            dgp_ref[...] = jnp.zeros_like(dgp_ref)
            dwm_ref[...] = jnp.zeros_like(dwm_ref)
            dbz_ref[...] = jnp.zeros_like(dbz_ref)

        dxo_v = dxo_ref[...]
        acc_ref[2:3, :] += _csum(dxo_v * fo_ref[...].astype(F32))
        dyo = (vec_ref[2:3, :] * dxo_v).astype(BF16)
        dyo_ref[...] = dyo
        dn = [_dot_nt(dyo, wout_ref[k]) for k in range(NDEV)]
        dn1 = jnp.concatenate(dn[:NDEV // 2], axis=1)
        dn2 = jnp.concatenate(dn[NDEV // 2:], axis=1)
        dyl, dg1 = _rms_bwd(dn1, yl_ref[...], gp_ref[1:2, :])
        dyl_ref[...] = dyl
        u = jnp.concatenate([u_ref[0], u_ref[1]], axis=1)
        v = jnp.concatenate([v_ref[0], v_ref[1]], axis=1)
        ug, rs, vhat, vh, z = _gmlp_fwd_parts(u, v, gp_ref, wcat_ref, bz_ref, pavg_ref, ts)
        dyg, dg2 = _rms_bwd(dn2, ug * z, gp_ref[2:3, :])
        du = (dyg * z) * _gelu_grad(u)
        dz = dyg * ug
        lane_head = lax.broadcasted_iota(jnp.int32, (CHUNK, GW), 1) // HD
        vhb = vh.astype(BF16)
        dvhs = []
        dbz = jnp.zeros((CHUNK, GW), F32)
        dwm = jnp.zeros((HEADS * CHUNK, CHUNK), F32)
        for n in range(ts // CHUNK):
            dzc = dz[n * CHUNK:(n + 1) * CHUNK, :]
            dbz = dbz + dzc
            stack = _head_stack(dzc, lane_head)
            dwm = dwm + _dot_nt(stack, vhb[n * CHUNK:(n + 1) * CHUNK, :])
            dvhs.append(_dot(wcatt_ref[...], stack))
        dbz_ref[...] += dbz
        dwm_ref[...] += dwm
        dvh = jnp.concatenate(dvhs, axis=0) if len(dvhs) > 1 else dvhs[0]
        pavg = pavg_ref[...]
        dvn = _csum(dvh * vhat)
        dvhat = dvh * gp_ref[0:1, :]
        dvg = rs * (dvhat - _seg_mean(dvhat, pavg) - vhat * _seg_mean(dvhat * vhat, pavg))
        dv = dvg * _gelu_grad(v)
        duv_ref[0] = du[:, :PC].astype(BF16)
        duv_ref[1] = du[:, PC:].astype(BF16)
        duv_ref[2] = dv[:, :PC].astype(BF16)
        duv_ref[3] = dv[:, PC:].astype(BF16)
        dgp_ref[0:1, :] += dvn
        dgp_ref[1:2, :] += dg1
        dgp_ref[2:3, :] += dg2

    row = pl.BlockSpec((ts, D), lambda i: (i, 0))
    full = lambda shp: pl.BlockSpec(shp, lambda i: tuple(0 for _ in shp))
    return pl.pallas_call(
        body, name=f"mix_out_bwd_{tag}",
        grid=(S // ts,),
        in_specs=[row, pl.BlockSpec((2, ts, PC), lambda i: (2, i, 0)), pl.BlockSpec((2, ts, PC), lambda i: (3, i, 0)),
                  pl.BlockSpec((ts, LW), lambda i: (i, 0)), row, full((8, D)), full((8, GW)),
                  full((CHUNK, HEADS * CHUNK)), full((CHUNK, HEADS * CHUNK)), full((CHUNK, GW)), full((GW, GW)),
                  pl.BlockSpec((NDEV, OR, D), lambda i: (0, 0, 0))],
        out_specs=[row, pl.BlockSpec((ts, LW), lambda i: (i, 0)), pl.BlockSpec((4, ts, PC), lambda i: (0, i, 0)),
                   full((8, D)), full((8, GW)), full((HEADS * CHUNK, CHUNK)), full((CHUNK, GW))],
        out_shape=[jax.ShapeDtypeStruct((S, D), BF16), jax.ShapeDtypeStruct((S, LW), F32),
                   jax.ShapeDtypeStruct((4, S, PC), BF16), jax.ShapeDtypeStruct((8, D), F32),
                   jax.ShapeDtypeStruct((8, GW), F32), jax.ShapeDtypeStruct((HEADS * CHUNK, CHUNK), F32),
                   jax.ShapeDtypeStruct((CHUNK, GW), F32)],
        compiler_params=_cparams("arbitrary"),
    )(dxo, proj, proj, ylru, fo, vec, gp, wcat, wcat_t, bz, pavg, wout_g)


def final_loss(x, target, gain):
    S = x.shape[0]
    ts = min(512, S)

    def body(x_ref, t_ref, g_ref, loss_ref, dx_ref, dg_ref):
        @pl.when(pl.program_id(0) == 0)
        def _():
            loss_ref[...] = jnp.zeros_like(loss_ref)
            dg_ref[...] = jnp.zeros_like(dg_ref)

        xv = x_ref[...]
        gain_v = g_ref[0:1, :]
        rstd = lax.rsqrt(_rmean(xv * xv) + EPS)
        xhat = xv * rstd
        err = xhat * gain_v - t_ref[...]
        loss_ref[...] += 0.5 * _csum(_rmean(err * err))
        dy = err * (1.0 / D)
        dg_ref[0:1, :] += _csum(dy * xhat)
        dxhat = dy * gain_v
        dx_ref[...] = rstd * (dxhat - xhat * _rmean(dxhat * xhat))

    row = pl.BlockSpec((ts, D), lambda i: (i, 0))
    return pl.pallas_call(
        body, name="final_loss",
        grid=(S // ts,),
        in_specs=[row, row, pl.BlockSpec((8, D), lambda i: (0, 0))],
        out_specs=[pl.BlockSpec((8, 128), lambda i: (0, 0)), row, pl.BlockSpec((8, D), lambda i: (0, 0))],
        out_shape=[jax.ShapeDtypeStruct((8, 128), F32), jax.ShapeDtypeStruct((S, D), F32),
                   jax.ShapeDtypeStruct((8, D), F32)],
        compiler_params=_cparams("arbitrary"),
    )(x, target, gain)


def _vec(mod_l, j, gain):
    return jnp.concatenate([mod_l[3 * j:3 * j + 3], gain[None, :], jnp.zeros((4, D), F32)], axis=0)


def _block_diag_tiles(w):
    w4 = w.reshape(LW // LC, 2, HD, HD)
    eye2 = jnp.eye(2, dtype=w.dtype)
    return (w4[:, :, :, None, :] * eye2[None, :, None, :, None]).reshape(LW // LC, LC, LC).astype(BF16)


def _block_diag_extract(dw):
    d5 = dw.reshape(LW // LC, 2, HD, 2, HD)
    return jnp.einsum('cihkj,ik->cihj', d5, jnp.eye(2, dtype=dw.dtype)).reshape(HEADS, HD, HD)


def _layer_params(l, p, conv_w_full):
    lp = jnp.concatenate([conv_w_full[l], p['conv_b'][l][None], p['gate_a_b'][l].reshape(1, LW),
                          p['gate_x_b'][l].reshape(1, LW), p['lru_lambda'][l][None]], axis=0)
    gp = jnp.concatenate([p['v_norm'][l][None], p['lru_out_norm'][l][None], p['gmlp_out_norm'][l][None],
                          jnp.zeros((5, GW), F32)], axis=0)
    ws = p['spatial_w'][l] * jnp.tril(jnp.ones((CHUNK, CHUNK), F32))
    wcat = ws.transpose(1, 0, 2).reshape(CHUNK, HEADS * CHUNK).astype(BF16)
    wcat_t = ws.transpose(2, 0, 1).reshape(CHUNK, HEADS * CHUNK).astype(BF16)
    bz = jnp.repeat(p['spatial_b'][l].T, HD, axis=1)
    return dict(lp=lp, gp=gp, wcat=wcat, wcat_t=wcat_t, bz=bz,
                wa_t=_block_diag_tiles(p['gate_a_w'][l]), wx_t=_block_diag_tiles(p['gate_x_w'][l]))


def _pavg():
    return jnp.kron(jnp.eye(HEADS, dtype=F32), jnp.full((HD, HD), 1.0 / HD, F32)).astype(BF16)


GATHER_RIDES = {
    ('ffn_a', 0): [('w_in', 0), ('w_out', 0), ('gu', DEPTH)],
    ('mix_in', 0): [('down', DEPTH)],
    ('lru', 0): [('down', 1)],
    ('mix_out', 0): [('w_in', 1), ('w_out', 1)],
    ('ffn_b', 0): [('gu', 1)],
    ('ffn_a', 1): [('gu', DEPTH + 1), ('down', DEPTH + 1)],
}


def local_fwd_bwd(me_arr, x, target, mod, p, loc, gathered, conv_w_full):
    pavg = _pavg()
    g = dict(gathered)

    def ride(call, l):
        todo = GATHER_RIDES.get((call, l))
        return None if todo is None else (todo, GatherRide([(loc[kind], slot) for kind, slot in todo]))

    def run(fn, call, l, *args):
        r = ride(call, l)
        outs, got = fn(*args, ride=None if r is None else r[1])
        if r is not None:
            g.update(dict(zip(r[0], got)))
        return outs

    saved = []
    h = x
    for l in range(DEPTH):
        q = _layer_params(l, p, conv_w_full)
        v1 = _vec(mod[l], 0, p['ffn1_norm'][l])
        vm = _vec(mod[l], 1, p['mix_norm'][l])
        v2 = _vec(mod[l], 2, p['ffn2_norm'][l])
        x0 = h
        x1, h1, gu1, f1 = run(ffn_fwd, 'ffn_a', l, x0, v1, g['gu', l], g['down', l], f"a{l}")
        hm, proj = run(mix_in_fwd, 'mix_in', l, x1, vm, g['w_in', l], f"{l}")
        ylru, xc, hst = run(lru_fwd, 'lru', l, proj, q['lp'], q['wa_t'], q['wx_t'], f"{l}")
        x2, y, fo = run(mix_out_fwd, 'mix_out', l, proj, ylru, x1, vm, q['gp'], q['wcat'], q['bz'], pavg,
                        g['w_out', l], f"{l}")
        x3, h2, gu2, f2 = run(ffn_fwd, 'ffn_b', l, x2, v2, g['gu', DEPTH + l], g['down', DEPTH + l], f"b{l}")
        saved.append(dict(q=q, v1=v1, vm=vm, v2=v2, x0=x0, x1=x1, x2=x2, h1=h1, gu1=gu1, f1=f1, hm=hm, proj=proj,
                          ylru=ylru, xc=xc, hst=hst, y=y, fo=fo, h2=h2, gu2=gu2, f2=f2))
        h = x3
    fin = jnp.concatenate([p['final_norm'][None], jnp.zeros((7, D), F32)], axis=0)
    loss8, dx, dfin = final_loss(h, target, fin)
    loss = loss8[0, 0]

    big = dict(gu=None, down=None, w_in=None, w_out=None)
    small = {k: [None] * DEPTH for k in ('ffn1_norm', 'mix_norm', 'ffn2_norm', 'conv_w', 'conv_b', 'gate_a_w',
                                         'gate_a_b', 'gate_x_w', 'gate_x_b', 'lru_lambda', 'v_norm', 'spatial_w',
                                         'spatial_b', 'lru_out_norm', 'gmlp_out_norm')}
    dmod = [None] * DEPTH
    tril = jnp.tril(jnp.ones((CHUNK, CHUNK), F32))
    for l in reversed(range(DEPTH)):
        sv = saved[l]
        q = sv['q']
        dx2, dgu, a, df, acc2 = ffn_bwd(dx, sv['x2'], sv['gu2'], sv['f2'], sv['v2'],
                                        g['gu', DEPTH + l], g['down', DEPTH + l], f"b{l}")
        big['gu'] = tn_matmul_scatter(me_arr, dgu, sv['h2'][None], DEPTH + l, 2 * DEPTH, big['gu'], f"dw_gu_b{l}")
        big['down'] = tn_matmul_scatter(me_arr, a, df[None], DEPTH + l, 2 * DEPTH, big['down'], f"dw_down_b{l}", split=2)
        dyo, dylru, duv, accmo, dgp, dwm, dbz = mix_out_bwd(dx2, sv['proj'], sv['ylru'], sv['fo'], sv['vm'], q['gp'],
                                                             q['wcat'], q['wcat_t'], q['bz'], pavg, g['w_out', l], f"{l}")
        big['w_out'] = tn_matmul_scatter(me_arr, sv['y'][None], dyo[None], l, DEPTH, big['w_out'], f"dw_out_{l}",
                                         split=NDEV)
        dxl, dgl, dlp, dwa, dwx = lru_bwd(dylru, sv['proj'], sv['xc'], sv['hst'], q['lp'], q['wa_t'], q['wx_t'], f"{l}")
        dproj = jnp.concatenate([dxl, dgl, duv], axis=0)
        dx1, accmi = mix_in_bwd(dproj, sv['x1'], dx2, sv['vm'], g['w_in', l], f"{l}")
        big['w_in'] = tn_matmul_scatter(me_arr, sv['hm'][None], dproj, l, DEPTH, big['w_in'], f"dw_in_{l}")
        dx0, dgu, a, df, acc1 = ffn_bwd(dx1, sv['x0'], sv['gu1'], sv['f1'], sv['v1'],
                                        g['gu', l], g['down', l], f"a{l}")
        big['gu'] = tn_matmul_scatter(me_arr, dgu, sv['h1'][None], l, 2 * DEPTH, big['gu'], f"dw_gu_a{l}")
        big['down'] = tn_matmul_scatter(me_arr, a, df[None], l, 2 * DEPTH, big['down'], f"dw_down_a{l}", split=2)
        dx = dx0
        dmod[l] = jnp.concatenate([acc1[0:3], accmi[0:2], accmo[2:3], acc2[0:3]], axis=0)
        small['ffn1_norm'][l] = acc1[3]
        small['mix_norm'][l] = accmi[3]
        small['ffn2_norm'][l] = acc2[3]
        small['conv_w'][l] = dlp[0:4]
        small['conv_b'][l] = dlp[4]
        small['gate_a_b'][l] = dlp[5].reshape(HEADS, HD)
        small['gate_x_b'][l] = dlp[6].reshape(HEADS, HD)
        small['lru_lambda'][l] = dlp[7]
        small['gate_a_w'][l] = _block_diag_extract(dwa)
        small['gate_x_w'][l] = _block_diag_extract(dwx)
        small['v_norm'][l] = dgp[0]
        small['lru_out_norm'][l] = dgp[1]
        small['gmlp_out_norm'][l] = dgp[2]
        small['spatial_w'][l] = dwm.reshape(HEADS, CHUNK, CHUNK) * tril
        small['spatial_b'][l] = dbz.reshape(CHUNK, HEADS, HD).sum(-1).T
    small = {k: jnp.stack(v) for k, v in small.items()}
    small['final_norm'] = dfin[0]
    return loss, dx, big, small, jnp.stack(dmod)


def ada_fwd(c_all, w_ada, b_loc):
    def body(c_ref, w_ref, b_ref, mod_ref, sc_ref):
        cv = c_ref[...]
        sc = cv * _sigmoid(cv)
        sc_ref[...] = sc
        mod_ref[...] = _dot3(sc, w_ref[...]) + b_ref[...]

    return pl.pallas_call(
        body, name="ada_fwd",
        grid=(DEPTH,),
        in_specs=[pl.BlockSpec((NDEV, D), lambda l: (0, 0)), pl.BlockSpec((None, D, AC), lambda l: (l, 0, 0)),
                  pl.BlockSpec((None, 1, AC), lambda l: (l, 0, 0))],
        out_specs=[pl.BlockSpec((None, NDEV, AC), lambda l: (l, 0, 0)), pl.BlockSpec((NDEV, D), lambda l: (0, 0))],
        out_shape=[jax.ShapeDtypeStruct((DEPTH, NDEV, AC), F32), jax.ShapeDtypeStruct((NDEV, D), F32)],
        compiler_params=_cparams("arbitrary"),
    )(c_all, w_ada, b_loc)


def ada_bwd(sc_t, dmod_cols):
    def body(sc_ref, dm_ref, g_ref):
        sc = sc_ref[...]
        dm = dm_ref[...]
        acc = sc[:, 0:1] * dm[0:1, :]
        for b in range(1, NDEV):
            acc = acc + sc[:, b:b + 1] * dm[b:b + 1, :]
        g_ref[...] = acc

    return pl.pallas_call(
        body, name="ada_bwd",
        grid=(DEPTH,),
        in_specs=[pl.BlockSpec((D, NDEV), lambda l: (0, 0)), pl.BlockSpec((None, NDEV, AC), lambda l: (l, 0, 0))],
        out_specs=pl.BlockSpec((None, None, D, AC), lambda l: (0, l, 0, 0)),
        out_shape=jax.ShapeDtypeStruct((1, DEPTH, D, AC), F32),
        compiler_params=_cparams("arbitrary"),
    )(sc_t, dmod_cols)


def _row_tile(rows, cols):
    if rows * cols <= 512 * 1024:
        return rows
    for tr in (512, 384, 352, 256, 128, 64, 32, 16, 8):
        if rows % tr == 0:
            return tr
    return rows


def adamw(gparts, slot0, w, m, v, name):
    P, _, R, C = gparts.shape
    L = w.shape[0]
    tr = _row_tile(R, C)

    def body(g_ref, w_ref, m_ref, v_ref, go_ref, do_ref, mo_ref, vo_ref):
        g = g_ref[0].astype(F32)
        for p in range(1, P):
            g = g + g_ref[p].astype(F32)
        go_ref[...] = g
        mn = ADAM_B1 * m_ref[...] + (1.0 - ADAM_B1) * g
        vn = ADAM_B2 * v_ref[...] + (1.0 - ADAM_B2) * (g * g)
        mo_ref[...] = mn
        vo_ref[...] = vn
        m_hat = mn / (1.0 - ADAM_B1 ** ADAM_STEP)
        v_hat = vn / (1.0 - ADAM_B2 ** ADAM_STEP)
        do_ref[...] = -ADAM_LR * (m_hat / (jnp.sqrt(v_hat) + ADAM_EPS) + ADAM_WD * w_ref[...])

    blk = pl.BlockSpec((None, tr, C), lambda l, i: (l, i, 0))
    return pl.pallas_call(
        body, name=name,
        grid=(L, R // tr),
        in_specs=[pl.BlockSpec((P, None, tr, C), lambda l, i: (0, slot0 + l, i, 0)), blk, blk, blk],
        out_specs=[blk, blk, blk, blk],
        out_shape=[jax.ShapeDtypeStruct((L, R, C), F32)] * 4,
        compiler_params=_cparams("arbitrary", "arbitrary"),
    )(gparts, w, m, v)


def sum_parts(parts):
    P, R, C = parts.shape

    def body(p_ref, o_ref):
        acc = p_ref[0]
        for p in range(1, P):
            acc = acc + p_ref[p]
        o_ref[...] = acc

    return pl.pallas_call(
        body, name="sum_parts",
        in_specs=[pl.BlockSpec(memory_space=pltpu.VMEM)],
        out_specs=pl.BlockSpec(memory_space=pltpu.VMEM),
        out_shape=jax.ShapeDtypeStruct((R, C), F32),
    )(parts)


WEIGHTS = ['w_ada', 'b_ada', 'ffn1_norm', 'ffn1_w_gu', 'ffn1_w_down', 'mix_norm', 'w_in', 'conv_w', 'conv_b',
           'gate_a_w', 'gate_a_b', 'gate_x_w', 'gate_x_b', 'lru_lambda', 'v_norm', 'spatial_w', 'spatial_b',
           'lru_out_norm', 'gmlp_out_norm', 'w_out', 'ffn2_norm', 'ffn2_w_gu', 'ffn2_w_down', 'final_norm']
PACKED = ['b_ada', 'ffn1_norm', 'mix_norm', 'conv_b', 'gate_a_w', 'gate_a_b', 'gate_x_w', 'gate_x_b', 'lru_lambda',
          'v_norm', 'spatial_w', 'spatial_b', 'lru_out_norm', 'gmlp_out_norm', 'ffn2_norm', 'final_norm', 'conv_w']
PACK_LANES = 128
PACK_ROW_ALIGN = 8 * NDEV


def _pack(d):
    flat = jnp.concatenate([d[k].reshape(-1).astype(F32) for k in PACKED])
    rows = -(-flat.shape[0] // (PACK_LANES * PACK_ROW_ALIGN)) * PACK_ROW_ALIGN
    flat = jnp.concatenate([flat, jnp.zeros((rows * PACK_LANES - flat.shape[0],), F32)])
    return flat.reshape(rows, PACK_LANES)


def _unpack(buf, shapes):
    flat = buf.reshape(-1)
    out, off = {}, 0
    for k in PACKED:
        size = 1
        for s in shapes[k]:
            size *= s
        out[k] = flat[off:off + size].reshape(shapes[k])
        off += size
    return out


def kernel(x, c, w_ada, b_ada, ffn1_norm, ffn1_w_gu, ffn1_w_down, mix_norm, w_in, conv_w, conv_b, gate_a_w, gate_a_b, gate_x_w, gate_x_b, lru_lambda, v_norm, spatial_w, spatial_b, lru_out_norm, gmlp_out_norm, w_out, ffn2_norm, ffn2_w_gu, ffn2_w_down, final_norm, loss_target, m_w_ada, m_b_ada, m_ffn1_norm, m_ffn1_w_gu, m_ffn1_w_down, m_mix_norm, m_w_in, m_conv_w, m_conv_b, m_gate_a_w, m_gate_a_b, m_gate_x_w, m_gate_x_b, m_lru_lambda, m_v_norm, m_spatial_w, m_spatial_b, m_lru_out_norm, m_gmlp_out_norm, m_w_out, m_ffn2_norm, m_ffn2_w_gu, m_ffn2_w_down, m_final_norm, v_w_ada, v_b_ada, v_ffn1_norm, v_ffn1_w_gu, v_ffn1_w_down, v_mix_norm, v_w_in, v_conv_w, v_conv_b, v_gate_a_w, v_gate_a_b, v_gate_x_w, v_gate_x_b, v_lru_lambda, v_v_norm, v_spatial_w, v_spatial_b, v_lru_out_norm, v_gmlp_out_norm, v_w_out, v_ffn2_norm, v_ffn2_w_gu, v_ffn2_w_down, v_final_norm):
    w = dict(w_ada=w_ada, b_ada=b_ada, ffn1_norm=ffn1_norm, ffn1_w_gu=ffn1_w_gu, ffn1_w_down=ffn1_w_down, mix_norm=mix_norm, w_in=w_in, conv_w=conv_w, conv_b=conv_b, gate_a_w=gate_a_w, gate_a_b=gate_a_b, gate_x_w=gate_x_w, gate_x_b=gate_x_b, lru_lambda=lru_lambda, v_norm=v_norm, spatial_w=spatial_w, spatial_b=spatial_b, lru_out_norm=lru_out_norm, gmlp_out_norm=gmlp_out_norm, w_out=w_out, ffn2_norm=ffn2_norm, ffn2_w_gu=ffn2_w_gu, ffn2_w_down=ffn2_w_down, final_norm=final_norm)
    m = dict(w_ada=m_w_ada, b_ada=m_b_ada, ffn1_norm=m_ffn1_norm, ffn1_w_gu=m_ffn1_w_gu, ffn1_w_down=m_ffn1_w_down, mix_norm=m_mix_norm, w_in=m_w_in, conv_w=m_conv_w, conv_b=m_conv_b, gate_a_w=m_gate_a_w, gate_a_b=m_gate_a_b, gate_x_w=m_gate_x_w, gate_x_b=m_gate_x_b, lru_lambda=m_lru_lambda, v_norm=m_v_norm, spatial_w=m_spatial_w, spatial_b=m_spatial_b, lru_out_norm=m_lru_out_norm, gmlp_out_norm=m_gmlp_out_norm, w_out=m_w_out, ffn2_norm=m_ffn2_norm, ffn2_w_gu=m_ffn2_w_gu, ffn2_w_down=m_ffn2_w_down, final_norm=m_final_norm)
    v = dict(w_ada=v_w_ada, b_ada=v_b_ada, ffn1_norm=v_ffn1_norm, ffn1_w_gu=v_ffn1_w_gu, ffn1_w_down=v_ffn1_w_down, mix_norm=v_mix_norm, w_in=v_w_in, conv_w=v_conv_w, conv_b=v_conv_b, gate_a_w=v_gate_a_w, gate_a_b=v_gate_a_b, gate_x_w=v_gate_x_w, gate_x_b=v_gate_x_b, lru_lambda=v_lru_lambda, v_norm=v_v_norm, spatial_w=v_spatial_w, spatial_b=v_spatial_b, lru_out_norm=v_lru_out_norm, gmlp_out_norm=v_gmlp_out_norm, w_out=v_w_out, ffn2_norm=v_ffn2_norm, ffn2_w_gu=v_ffn2_w_gu, ffn2_w_down=v_ffn2_w_down, final_norm=v_final_norm)
    me = 4 * lax.axis_index("x") + 2 * lax.axis_index("y") + lax.axis_index("c")

    loc = dict(gu=jnp.concatenate([ffn1_w_gu, ffn2_w_gu], axis=0).astype(BF16),
               down=jnp.concatenate([ffn1_w_down, ffn2_w_down], axis=0).astype(BF16),
               w_in=w_in.astype(BF16), w_out=w_out.astype(BF16))
    c_g, conv_g, gu0, down0 = all_gather([(c, None), (conv_w, None), (loc['gu'], 0), (loc['down'], 0)], "gather_first")
    conv_w_full = conv_g.transpose(1, 2, 0, 3).reshape(DEPTH, CONV_WIDTH, LW)

    b_loc = lax.dynamic_slice(b_ada, (0, me * AC), (DEPTH, AC)).reshape(DEPTH, 1, AC)
    mod_cols, sc_all = ada_fwd(c_g.reshape(NDEV, D), w_ada, b_loc)
    (mod_rows,) = all_to_all([mod_cols.transpose(1, 0, 2)], "scatter_mod")
    mod = mod_rows.transpose(1, 0, 2).reshape(DEPTH, NMOD, D)

    small_w = {k: w[k] for k in PACKED if k != 'conv_w'}
    me_arr = jnp.reshape(me, (1,)).astype(jnp.int32)
    loss_loc, dx, big, small_g, dmod = local_fwd_bwd(me_arr, x[0], loss_target[0], mod, small_w, loc,
                                                     {('gu', 0): gu0, ('down', 0): down0}, conv_w_full)
    loss = lax.psum(loss_loc, ("x", "y", "c"))

    small_g['b_ada'] = dmod.reshape(DEPTH, NMOD * D)
    gpack = _pack(small_g)
    rows = gpack.shape[0]
    dmod_out = dmod.reshape(DEPTH, NDEV, AC).transpose(1, 0, 2)
    dmod_r, pack_r = all_to_all([dmod_out, gpack.reshape(NDEV, rows // NDEV, PACK_LANES)], "scatter_grads")
    (gsum_g,) = all_gather([(sum_parts(pack_r), None)], "gather_small_grads")
    gsum = gsum_g.reshape(1, 1, rows, PACK_LANES)

    res = {}
    t = lambda a: a.transpose(0, 2, 1)
    res['ffn1_w_gu'] = tuple(t(r) for r in adamw(big['gu'], 0, t(w['ffn1_w_gu']), t(m['ffn1_w_gu']), t(v['ffn1_w_gu']),
                                                 "adamw_gu_a"))
    res['ffn2_w_gu'] = tuple(t(r) for r in adamw(big['gu'], DEPTH, t(w['ffn2_w_gu']), t(m['ffn2_w_gu']),
                                                 t(v['ffn2_w_gu']), "adamw_gu_b"))
    res['ffn1_w_down'] = adamw(big['down'], 0, w['ffn1_w_down'], m['ffn1_w_down'], v['ffn1_w_down'], "adamw_down_a")
    res['ffn2_w_down'] = adamw(big['down'], DEPTH, w['ffn2_w_down'], m['ffn2_w_down'], v['ffn2_w_down'], "adamw_down_b")
    res['w_in'] = adamw(big['w_in'], 0, w['w_in'], m['w_in'], v['w_in'], "adamw_w_in")
    res['w_out'] = adamw(big['w_out'], 0, w['w_out'], m['w_out'], v['w_out'], "adamw_w_out")
    g_ada = ada_bwd(sc_all.T, dmod_r.transpose(1, 0, 2))
    res['w_ada'] = adamw(g_ada, 0, w['w_ada'], m['w_ada'], v['w_ada'], "adamw_w_ada")
    shapes = {k: w[k].shape for k in PACKED}
    shapes['conv_w'] = (DEPTH, CONV_WIDTH, LW)
    dummy = jnp.zeros(shapes['conv_w'], F32)
    packs = adamw(gsum, 0, _pack({**small_w, 'conv_w': dummy})[None], _pack({**{k: m[k] for k in small_w}, 'conv_w': dummy})[None],
                  _pack({**{k: v[k] for k in small_w}, 'conv_w': dummy})[None], "adamw_small")
    unpacked = [_unpack(b[0], shapes) for b in packs]
    for k in small_w:
        res[k] = tuple(u[k] for u in unpacked)
    gconv = lax.dynamic_slice(unpacked[0]['conv_w'], (0, 0, me * (LW // NDEV)), (DEPTH, CONV_WIDTH, LW // NDEV))
    cshape = (1, DEPTH * CONV_WIDTH, LW // NDEV)
    rc = adamw(gconv.reshape((1,) + cshape), 0, conv_w.reshape(cshape), m['conv_w'].reshape(cshape),
               v['conv_w'].reshape(cshape), "adamw_conv_w")
    res['conv_w'] = tuple(r.reshape(conv_w.shape) for r in rc)

    return (loss, dx[None], *[res[k][0] for k in WEIGHTS], *[res[k][1] for k in WEIGHTS],
            *[res[k][2] for k in WEIGHTS], *[res[k][3] for k in WEIGHTS])
```

```python
import jax
import jax.numpy as jnp
from jax import lax
from jax.experimental import pallas as pl
from jax.experimental.pallas import tpu as pltpu

F32 = jnp.float32
BF16 = jnp.bfloat16

NDEV = 8
DEPTH = 2
D = 1024
DFF = 2816
FC = 2 * DFF // NDEV
NCHUNK = DFF // FC
DR = DFF // NDEV
LW = 512
GW = 512
HD = 64
HEADS = 8
CHUNK = 128
PC = 2 * (LW + GW) // NDEV
OR = D // NDEV
NMOD = 9
AC = NMOD * D // NDEV
LC = 128
EPS = 1e-6
RG_LRU_C = 8.0
CONV_WIDTH = 4

ADAM_LR = 0.001
ADAM_B1 = 0.9
ADAM_B2 = 0.999
ADAM_EPS = 1e-08
ADAM_WD = 0.01
ADAM_STEP = 10

VMEM_LIMIT_BYTES = 60 * 1024 * 1024
MESH = pl.DeviceIdType.MESH
ANY = pl.BlockSpec(memory_space=pl.ANY)


def _cparams(*sem):
    return pltpu.CompilerParams(dimension_semantics=tuple(sem) if sem else None,
                                vmem_limit_bytes=VMEM_LIMIT_BYTES)


def _dot(a, b):
    return jnp.dot(a, b, preferred_element_type=F32)


def _dot_nt(a, b):
    return lax.dot_general(a, b, (((1,), (1,)), ((), ())), preferred_element_type=F32)


def _dot_tn(a, b):
    return lax.dot_general(a, b, (((0,), (0,)), ((), ())), preferred_element_type=F32)


def _split(a):
    hi = a.astype(BF16)
    lo = (a - hi.astype(F32)).astype(BF16)
    return hi, lo


def _dot3(a, b):
    ah, al = _split(a)
    bh, bl = _split(b)
    return _dot(ah, bh) + (_dot(ah, bl) + _dot(al, bh))


def _csum(a):
    return jnp.sum(a, axis=0, keepdims=True)


def _rmean(a):
    return jnp.mean(a, axis=-1, keepdims=True)


def _sigmoid(a):
    return 1.0 / (1.0 + jnp.exp(-a))


_GELU_K = 0.7978845608028654
_GELU_C = 0.044715


def _gelu(a):
    return 0.5 * a * (1.0 + jnp.tanh(_GELU_K * (a + _GELU_C * a * a * a)))


def _gelu_grad(a):
    t = jnp.tanh(_GELU_K * (a + _GELU_C * a * a * a))
    return 0.5 * (1.0 + t) + 0.5 * a * (1.0 - t * t) * (_GELU_K * (1.0 + 3.0 * _GELU_C * a * a))


def _norm_mod(x, gain, scale, shift):
    rstd = lax.rsqrt(_rmean(x * x) + EPS)
    return (x * rstd * gain) * (1.0 + scale) + shift


def _norm_mod_bwd(dh, x, gain, scale):
    rstd = lax.rsqrt(_rmean(x * x) + EPS)
    xhat = x * rstd
    dshift = _csum(dh)
    dscale = _csum(dh * (xhat * gain))
    dhn = dh * (1.0 + scale)
    dgain = _csum(dhn * xhat)
    dxhat = dhn * gain
    dx = rstd * (dxhat - xhat * _rmean(dxhat * xhat))
    return dx, dshift, dscale, dgain


def _rms(x, gain):
    rstd = lax.rsqrt(_rmean(x * x) + EPS)
    return x * rstd * gain


def _rms_bwd(dy, x, gain):
    rstd = lax.rsqrt(_rmean(x * x) + EPS)
    xhat = x * rstd
    dgain = _csum(dy * xhat)
    dxhat = dy * gain
    return rstd * (dxhat - xhat * _rmean(dxhat * xhat)), dgain


def _seg_mean(a, pavg):
    hi, lo = _split(a)
    return _dot(hi, pavg) + _dot(lo, pavg)


def _block_copies(src_hbm, dst_vmem, sems, rows):
    copies = []
    for k in range(NDEV):
        dst = dst_vmem.at[k] if rows is None else dst_vmem.at[pl.ds(k * rows, rows)]
        copies.append(pltpu.make_async_copy(src_hbm.at[k], dst, sems.at[k]))
    return copies


def _ffn_weight_fetch(wgu_hbm, wd_hbm, wgu_v, wd_v, sems):
    first = pl.program_id(0) == 0
    gu = _block_copies(wgu_hbm, wgu_v, sems.at[0], None)
    down = _block_copies(wd_hbm, wd_v, sems.at[1], DR)

    @pl.when(first)
    def _():
        for cp in gu + down:
            cp.start()

    def wait(j, which):
        @pl.when(first)
        def _():
            for cp in ([gu[j], gu[NCHUNK + j]] if which == 0 else [down[2 * j], down[2 * j + 1]]):
                cp.wait()

    return wait


def _place():
    return lax.axis_index("x"), lax.axis_index("y"), lax.axis_index("c")


def _slot(p):
    return 4 * p[0] + 2 * p[1] + p[2]


class GatherRide:
    def __init__(self, srcs):
        self.n = len(srcs)
        self.index = [i for _, i in srcs]
        self.args = [a for a, _ in srcs]
        self.out_shape = [jax.ShapeDtypeStruct((NDEV,) + (a.shape if i is None else a.shape[1:]), a.dtype)
                          for a, i in srcs]
        self.scratch = [pltpu.SemaphoreType.DMA((self.n, NDEV - 1)), pltpu.SemaphoreType.DMA((self.n, NDEV - 1)),
                        pltpu.SemaphoreType.DMA((self.n,))]

    def hooks(self, ins, outs, sems):
        send_sems, recv_sems, local_sems = sems
        n = self.n
        x, y, c = _place()
        me, sibling = (x, y, c), (x, y, 1 - c)
        chips = [(1 - x, y), (x, 1 - y), (1 - x, 1 - y)]

        def local(a):
            return ins[a] if self.index[a] is None else ins[a].at[self.index[a]]

        def copy(a, k, block, to, src=None):
            dst = outs[a].at[_slot(block)]
            return pltpu.make_async_remote_copy(
                src_ref=dst if src is None else src, dst_ref=dst,
                send_sem=send_sems.at[a, k], recv_sem=recv_sems.at[a, k],
                device_id=to, device_id_type=MESH)

        def mine():
            return [pltpu.make_async_copy(local(a), outs[a].at[_slot(me)], local_sems.at[a]) for a in range(n)]

        def first():
            cps = []
            for a in range(n):
                cps.append(copy(a, 0, me, sibling, src=local(a)))
                cps += [copy(a, 1 + j, me, (*chip, c), src=local(a)) for j, chip in enumerate(chips)]
            return cps

        def passed():
            return [copy(a, 4 + j, (*chip, c), sibling) for j, chip in enumerate(chips) for a in range(n)]

        def start():
            for cp in mine() + first():
                cp.start()

        def mid():
            for j, chip in enumerate(chips):
                for a in range(n):
                    copy(a, 1 + j, (*chip, c), me).wait_recv()
                    copy(a, 4 + j, (*chip, c), sibling).start()

        def finish():
            for a in range(n):
                copy(a, 0, sibling, me).wait_recv()
                for j, chip in enumerate(chips):
                    copy(a, 4 + j, (*chip, 1 - c), me).wait_recv()
            for cp in first() + passed():
                cp.wait_send()
            for cp in mine():
                cp.wait()

        return start, mid, finish


def all_gather(srcs, name):
    ride = GatherRide(srcs)
    n = ride.n

    def body(*refs):
        start, mid, finish = ride.hooks(refs[:n], refs[n:2 * n], refs[2 * n:])
        start()
        mid()
        finish()

    return pl.pallas_call(
        body, name=name,
        in_specs=[ANY] * n, out_specs=[ANY] * n, out_shape=ride.out_shape, scratch_shapes=ride.scratch,
    )(*ride.args)


def _call(core, ride, *, name, grid, in_specs, out_specs, out_shape, scratch_shapes, args):
    if ride is None:
        outs = pl.pallas_call(core, name=name, grid=grid, in_specs=in_specs, out_specs=out_specs,
                              out_shape=out_shape, scratch_shapes=scratch_shapes,
                              compiler_params=_cparams("arbitrary"))(*args)
        return outs, []
    n_in, n_out, n_sc, n = len(in_specs), len(out_shape), len(scratch_shapes), ride.n
    nsteps = grid[0]
    mid_step = max(nsteps - 2, 0)

    def body(*refs):
        cuts = [n_in, n_in + n, n_in + n + n_out, n_in + 2 * n + n_out, n_in + 2 * n + n_out + n_sc]
        ci, ri, co, ro, cs, rs = (refs[a:b] for a, b in zip([0] + cuts, cuts + [len(refs)]))
        start, mid, finish = ride.hooks(ri, ro, rs)
        i = pl.program_id(0)
        pl.when(i == 0)(start)
        core(*ci, *co, *cs)
        pl.when(i == mid_step)(mid)
        pl.when(i == nsteps - 1)(finish)

    outs = pl.pallas_call(
        body, name=name, grid=grid,
        in_specs=list(in_specs) + [ANY] * n, out_specs=list(out_specs) + [ANY] * n,
        out_shape=list(out_shape) + ride.out_shape, scratch_shapes=list(scratch_shapes) + ride.scratch,
        compiler_params=_cparams("arbitrary"))(*args, *ride.args)
    return outs[:n_out], outs[n_out:]


def all_to_all(arrs, name):
    n = len(arrs)

    def body(*refs):
        ins, outs = refs[:n], refs[n:2 * n]
        send_sems, recv_sems, local_sems = refs[2 * n:]
        x, y, c = _place()
        me = (x, y, c)

        def peer(k):
            return (1 - x if k & 4 else x, 1 - y if k & 2 else y, 1 - c if k & 1 else c)

        def copy(a, k):
            return pltpu.make_async_remote_copy(
                src_ref=ins[a].at[_slot(peer(k))], dst_ref=outs[a].at[_slot(me)],
                send_sem=send_sems.at[a, k - 1], recv_sem=recv_sems.at[a, k - 1],
                device_id=peer(k), device_id_type=MESH)

        def landing(a, k):
            return pltpu.make_async_remote_copy(
                src_ref=outs[a].at[_slot(peer(k))], dst_ref=outs[a].at[_slot(peer(k))],
                send_sem=send_sems.at[a, k - 1], recv_sem=recv_sems.at[a, k - 1],
                device_id=me, device_id_type=MESH)

        mine = [pltpu.make_async_copy(ins[a].at[_slot(me)], outs[a].at[_slot(me)], local_sems.at[a]) for a in range(n)]
        for cp in mine:
            cp.start()
        sends = [copy(a, k) for a in range(n) for k in range(1, NDEV)]
        for cp in sends:
            cp.start()
        for a in range(n):
            for k in range(1, NDEV):
                landing(a, k).wait_recv()
        for cp in sends:
            cp.wait_send()
        for cp in mine:
            cp.wait()

    return pl.pallas_call(
        body, name=name,
        in_specs=[ANY] * n, out_specs=[ANY] * n,
        out_shape=[jax.ShapeDtypeStruct(a.shape, a.dtype) for a in arrs],
        scratch_shapes=[pltpu.SemaphoreType.DMA((n, NDEV - 1)), pltpu.SemaphoreType.DMA((n, NDEV - 1)),
                        pltpu.SemaphoreType.DMA((n,))],
    )(*arrs)


FFN_TS = 256


def ffn_fwd(x, vec, wgu_g, wdown_g, tag, ride=None):
    S = x.shape[0]
    ts = min(FFN_TS, S)

    def body(x_ref, vec_ref, wgu_hbm, wd_hbm, xo_ref, h_ref, gu_ref, f_ref, wgu_v, wd_v, sems):
        wait = _ffn_weight_fetch(wgu_hbm, wd_hbm, wgu_v, wd_v, sems)
        xv = x_ref[...]
        h = _norm_mod(xv, vec_ref[3:4, :], vec_ref[1:2, :], vec_ref[0:1, :]).astype(BF16)
        h_ref[...] = h
        acc = jnp.zeros((ts, D), F32)
        for j in range(NCHUNK):
            wait(j, 0)
            g = _dot(h, wgu_v[j])
            u = _dot(h, wgu_v[NCHUNK + j])
            gu_ref[j] = g.astype(BF16)
            gu_ref[NCHUNK + j] = u.astype(BF16)
            a = (g * _sigmoid(g) * u).astype(BF16)
            wait(j, 1)
            acc = acc + _dot(a, wd_v[pl.ds(j * FC, FC), :])
        f_ref[...] = acc.astype(BF16)
        xo_ref[...] = xv + (0.5 * vec_ref[2:3, :]) * acc

    return _call(
        body, ride, name=f"ffn_fwd_{tag}",
        grid=(S // ts,),
        in_specs=[pl.BlockSpec((ts, D), lambda i: (i, 0)),
                  pl.BlockSpec((8, D), lambda i: (0, 0)), ANY, ANY],
        out_specs=[pl.BlockSpec((ts, D), lambda i: (i, 0)),
                   pl.BlockSpec((ts, D), lambda i: (i, 0)),
                   pl.BlockSpec((NDEV, ts, FC), lambda i: (0, i, 0)),
                   pl.BlockSpec((ts, D), lambda i: (i, 0))],
        out_shape=[jax.ShapeDtypeStruct((S, D), F32), jax.ShapeDtypeStruct((S, D), BF16),
                   jax.ShapeDtypeStruct((NDEV, S, FC), BF16), jax.ShapeDtypeStruct((S, D), BF16)],
        scratch_shapes=[pltpu.VMEM((NDEV, D, FC), BF16), pltpu.VMEM((DFF, D), BF16),
                        pltpu.SemaphoreType.DMA((2, NDEV))],
        args=(x, vec, wgu_g, wdown_g))


def ffn_bwd(dxo, x, gu, f, vec, wgu_g, wdown_g, tag):
    S = x.shape[0]
    ts = min(FFN_TS, S)

    def body(dxo_ref, x_ref, gu_ref, f_ref, vec_ref, wgu_hbm, wd_hbm,
             dx_ref, dgu_ref, a_ref, df_ref, acc_ref, wgu_v, wd_v, sems):
        wait = _ffn_weight_fetch(wgu_hbm, wd_hbm, wgu_v, wd_v, sems)

        @pl.when(pl.program_id(0) == 0)
        def _():
            acc_ref[...] = jnp.zeros_like(acc_ref)

        dxo_v = dxo_ref[...]
        dgate = 0.5 * _csum(dxo_v * f_ref[...].astype(F32))
        df = ((0.5 * vec_ref[2:3, :]) * dxo_v).astype(BF16)
        df_ref[...] = df
        dh = jnp.zeros((ts, D), F32)
        for j in range(NCHUNK):
            wait(j, 1)
            da = _dot_nt(df, wd_v[pl.ds(j * FC, FC), :])
            g = gu_ref[j].astype(F32)
            u = gu_ref[NCHUNK + j].astype(F32)
            sg = _sigmoid(g)
            si = g * sg
            a_ref[j] = (si * u).astype(BF16)
            dg = (da * u * (sg * (1.0 + g * (1.0 - sg)))).astype(BF16)
            du = (da * si).astype(BF16)
            dgu_ref[j] = dg
            dgu_ref[NCHUNK + j] = du
            wait(j, 0)
            dh = dh + _dot_nt(dg, wgu_v[j]) + _dot_nt(du, wgu_v[NCHUNK + j])
        dx, dshift, dscale, dgain = _norm_mod_bwd(dh, x_ref[...], vec_ref[3:4, :], vec_ref[1:2, :])
        dx_ref[...] = dx + dxo_v
        acc_ref[0:1, :] += dshift
        acc_ref[1:2, :] += dscale
        acc_ref[2:3, :] += dgate
        acc_ref[3:4, :] += dgain

    row = pl.BlockSpec((ts, D), lambda i: (i, 0))
    return pl.pallas_call(
        body, name=f"ffn_bwd_{tag}",
        grid=(S // ts,),
        in_specs=[row, row, pl.BlockSpec((NDEV, ts, FC), lambda i: (0, i, 0)), row,
                  pl.BlockSpec((8, D), lambda i: (0, 0)), ANY, ANY],
        out_specs=[row, pl.BlockSpec((NDEV, ts, FC), lambda i: (0, i, 0)),
                   pl.BlockSpec((NCHUNK, ts, FC), lambda i: (0, i, 0)), row,
                   pl.BlockSpec((8, D), lambda i: (0, 0))],
        out_shape=[jax.ShapeDtypeStruct((S, D), F32), jax.ShapeDtypeStruct((NDEV, S, FC), BF16),
                   jax.ShapeDtypeStruct((NCHUNK, S, FC), BF16), jax.ShapeDtypeStruct((S, D), BF16),
                   jax.ShapeDtypeStruct((8, D), F32)],
        scratch_shapes=[pltpu.VMEM((NDEV, D, FC), BF16), pltpu.VMEM((DFF, D), BF16),
                        pltpu.SemaphoreType.DMA((2, NDEV))],
        compiler_params=_cparams("arbitrary"),
    )(dxo, x, gu, f, vec, wgu_g, wdown_g)


NCHIP = NDEV // 2


def tn_matmul_scatter(me_arr, a, b, slot, nslots, prev, name, split=1):
    na, S, M = a.shape
    nb, _, N = b.shape
    ncall = NDEV // split
    ts = min(4096, S)
    nsteps = S // ts
    mp = M // split
    other_step = {1: lambda j: 2 * j, 2: lambda j: j, 8: lambda j: 0}[split]
    mine_step = {1: lambda j: 2 * j + 1, 2: lambda j: j, 8: lambda j: 0}[split]

    def group(k, me_ref):
        if split == 1:
            return jnp.bitwise_xor(me_ref[0], NDEV - 1 - k)
        if split == 2:
            return jnp.bitwise_xor(me_ref[0] // 2, NCHIP - 1 - k)
        return 0

    def body(me_ref, *refs):
        a_ref, b_ref = refs[0], refs[1]
        recv_ref, acc, sb_other, sb_mine, land, d2d_send, d2d_recv, ici_send, ici_recv = refs[-9:]
        k = pl.program_id(0)
        s = pl.program_id(1)
        x, y, c = _place()
        my_chip = 2 * x + y

        def chip_of(j):
            if split == 8:
                cx, cy = j // 2, j % 2
            else:
                flip = NCHIP - 1 - j
                cx, cy = (1 - x if flip & 2 else x), (1 - y if flip & 1 else y)
            return cx, cy, 2 * cx + cy

        def piece(j, core):
            if split == 1:
                return acc[...]
            start = core * mp if split == 2 else (2 * j + core) * mp
            return acc[pl.ds(pl.multiple_of(start, 8), mp), :]

        def to_sibling(j):
            return pltpu.make_async_remote_copy(
                src_ref=sb_other.at[j], dst_ref=land.at[j], send_sem=d2d_send.at[j], recv_sem=d2d_recv.at[j],
                device_id=(x, y, 1 - c), device_id_type=MESH)

        def to_owner(j):
            cx, cy, ci = chip_of(j)
            dst = recv_ref.at[my_chip, slot]
            return ci, pltpu.make_async_copy(sb_mine.at[j], dst, ici_send.at[j]), pltpu.make_async_remote_copy(
                src_ref=sb_mine.at[j], dst_ref=dst, send_sem=ici_send.at[j], recv_sem=ici_recv.at[my_chip],
                device_id=(cx, cy, c), device_id_type=MESH)

        if nsteps == 1:
            acc[...] = _dot_tn(a_ref[...], b_ref[...])
        else:
            @pl.when(s == 0)
            def _():
                acc[...] = jnp.zeros_like(acc)

            acc[...] += _dot_tn(a_ref[...], b_ref[...])

        for kk in range(ncall):
            @pl.when((s == nsteps - 1) & (k == kk))
            def _():
                for j in range(NCHIP):
                    if other_step(j) == kk:
                        sb_other[j] = piece(j, 1 - c).astype(BF16)
                        to_sibling(j).start()
                for j in range(NCHIP):
                    if mine_step(j) == kk:
                        to_sibling(j).wait_recv()
                        sb_mine[j] = (piece(j, c) + land[j].astype(F32)).astype(BF16)
                        ci, loc, rem = to_owner(j)
                        pl.when(ci == my_chip)(loc.start)
                        pl.when(ci != my_chip)(rem.start)

        @pl.when((s == nsteps - 1) & (k == ncall - 1))
        def _():
            for j in range(NCHIP):
                to_sibling(j).wait_send()
                ci, loc, rem = to_owner(j)
                pl.when(ci == my_chip)(loc.wait)
                pl.when(ci != my_chip)(rem.wait_send)
            for src in range(NCHIP):
                @pl.when(my_chip != src)
                def _():
                    pltpu.make_async_remote_copy(
                        src_ref=recv_ref.at[src, slot], dst_ref=recv_ref.at[src, slot],
                        send_sem=ici_send.at[src], recv_sem=ici_recv.at[src],
                        device_id=(src // 2, src % 2, c), device_id_type=MESH).wait_recv()

    in_specs = [pl.BlockSpec((None, ts, M), (lambda k, s, me: (group(k, me), s, 0)) if na > 1 else (lambda k, s, me: (0, s, 0))),
                pl.BlockSpec((None, ts, N), (lambda k, s, me: (group(k, me), s, 0)) if nb > 1 else (lambda k, s, me: (0, s, 0)))]
    args = [me_arr, a, b]
    aliases = {}
    if prev is not None:
        in_specs.append(ANY)
        args.append(prev)
        aliases = {3: 0}
    return pl.pallas_call(
        body, name=name,
        grid_spec=pltpu.PrefetchScalarGridSpec(
            num_scalar_prefetch=1, grid=(ncall, nsteps), in_specs=in_specs, out_specs=ANY,
            scratch_shapes=[pltpu.VMEM((M, N), F32), pltpu.VMEM((NCHIP, mp, N), BF16), pltpu.VMEM((NCHIP, mp, N), BF16),
                            pltpu.VMEM((NCHIP, mp, N), BF16), pltpu.SemaphoreType.DMA((NCHIP,)),
                            pltpu.SemaphoreType.DMA((NCHIP,)), pltpu.SemaphoreType.DMA((NCHIP,)),
                            pltpu.SemaphoreType.DMA((NCHIP,))]),
        out_shape=jax.ShapeDtypeStruct((NCHIP, nslots, mp, N), BF16),
        input_output_aliases=aliases,
        compiler_params=_cparams("arbitrary", "arbitrary"),
    )(*args)


MIX_TS = 256


def mix_in_fwd(x, vec, win_g, tag, ride=None):
    S = x.shape[0]
    ts = min(MIX_TS, S)

    def body(x_ref, vec_ref, win_ref, hm_ref, proj_ref):
        h = _norm_mod(x_ref[...], vec_ref[3:4, :], vec_ref[1:2, :], vec_ref[0:1, :]).astype(BF16)
        hm_ref[...] = h
        for k in range(NDEV):
            proj_ref[k] = _dot(h, win_ref[k])

    return _call(
        body, ride, name=f"mix_in_fwd_{tag}",
        grid=(S // ts,),
        in_specs=[pl.BlockSpec((ts, D), lambda i: (i, 0)), pl.BlockSpec((8, D), lambda i: (0, 0)),
                  pl.BlockSpec((NDEV, D, PC), lambda i: (0, 0, 0))],
        out_specs=[pl.BlockSpec((ts, D), lambda i: (i, 0)),
                   pl.BlockSpec((NDEV, ts, PC), lambda i: (0, i, 0))],
        out_shape=[jax.ShapeDtypeStruct((S, D), BF16), jax.ShapeDtypeStruct((NDEV, S, PC), F32)],
        scratch_shapes=[], args=(x, vec, win_g))


def mix_in_bwd(dproj, x, dxo, vec, win_g, tag):
    S = x.shape[0]
    ts = min(MIX_TS, S)

    def body(dp_ref, x_ref, dxo_ref, vec_ref, win_ref, dx_ref, acc_ref):
        @pl.when(pl.program_id(0) == 0)
        def _():
            acc_ref[...] = jnp.zeros_like(acc_ref)

        dh = jnp.zeros((ts, D), F32)
        for k in range(NDEV):
            dh = dh + _dot_nt(dp_ref[k], win_ref[k])
        dx, dshift, dscale, dgain = _norm_mod_bwd(dh, x_ref[...], vec_ref[3:4, :], vec_ref[1:2, :])
        dx_ref[...] = dx + dxo_ref[...]
        acc_ref[0:1, :] += dshift
        acc_ref[1:2, :] += dscale
        acc_ref[3:4, :] += dgain

    row = pl.BlockSpec((ts, D), lambda i: (i, 0))
    return pl.pallas_call(
        body, name=f"mix_in_bwd_{tag}",
        grid=(S // ts,),
        in_specs=[pl.BlockSpec((NDEV, ts, PC), lambda i: (0, i, 0)), row, row,
                  pl.BlockSpec((8, D), lambda i: (0, 0)),
                  pl.BlockSpec((NDEV, D, PC), lambda i: (0, 0, 0))],
        out_specs=[row, pl.BlockSpec((8, D), lambda i: (0, 0))],
        out_shape=[jax.ShapeDtypeStruct((S, D), F32), jax.ShapeDtypeStruct((8, D), F32)],
        compiler_params=_cparams("arbitrary"),
    )(dproj, x, dxo, vec, win_g)


SCAN_UNROLL = 4


def _shift_down(z, k, row):
    return jnp.where(row >= k, pltpu.roll(z, k, 0), 0.0)


def _shift_up(z, k, row, n):
    return jnp.where(row < n - k, pltpu.roll(z, n - k, 0), 0.0)


def _lru_gates(xc, lp_ref, wa_ref, wx_ref):
    xcb = xc.astype(BF16)
    ra = _sigmoid(_dot(xcb, wa_ref[...]) + lp_ref[5:6, :])
    ix = _sigmoid(_dot(xcb, wx_ref[...]) + lp_ref[6:7, :])
    lam = lp_ref[7:8, :]
    ls = jnp.minimum(lam, 0.0) - jnp.log(1.0 + jnp.exp(-jnp.abs(lam)))
    log_a = (RG_LRU_C * ls) * ra
    a = jnp.exp(log_a)
    mult = jnp.sqrt(-jnp.tanh(log_a) * (a * a + 1.0))
    return ra, ix, ls, a, mult


def _conv(x, lp_ref, row):
    return (lp_ref[4:5, :] + lp_ref[3:4, :] * x + lp_ref[2:3, :] * _shift_down(x, 1, row)
            + lp_ref[1:2, :] * _shift_down(x, 2, row) + lp_ref[0:1, :] * _shift_down(x, 3, row))


def lru_fwd(proj, lp, wa_t, wx_t, tag, ride=None):
    S = proj.shape[1]
    nblk = S // 8

    def body(x_ref, g_ref, lp_ref, wa_ref, wx_ref, y_ref, xc_ref, h_ref, a_s, b_s):
        x = x_ref[...]
        row = lax.broadcasted_iota(jnp.int32, x.shape, 0)
        xc = _conv(x, lp_ref, row)
        xc_ref[...] = xc
        ra, ix, ls, a, mult = _lru_gates(xc, lp_ref, wa_ref, wx_ref)
        a_s[...] = a
        b_s[...] = mult * (ix * xc)
        rowb = lax.broadcasted_iota(jnp.int32, (8, LC), 0)

        def step(i, carry):
            for q in range(SCAN_UNROLL):
                r0 = pl.multiple_of((i * SCAN_UNROLL + q) * 8, 8)
                A = a_s[pl.ds(r0, 8), :]
                B = b_s[pl.ds(r0, 8), :]
                for d in (1, 2, 4):
                    m = rowb >= d
                    As = jnp.where(m, pltpu.roll(A, d, 0), 1.0)
                    Bs = jnp.where(m, pltpu.roll(B, d, 0), 0.0)
                    B = A * Bs + B
                    A = A * As
                H = B + A * carry
                h_ref[pl.ds(r0, 8), :] = H
                carry = H[7:8, :]
            return carry

        lax.fori_loop(0, nblk // SCAN_UNROLL, step, jnp.zeros((1, LC), F32))
        y_ref[...] = h_ref[...] * _gelu(g_ref[...])

    col = pl.BlockSpec((S, LC), lambda c: (0, c))
    return _call(
        body, ride, name=f"lru_fwd_{tag}",
        grid=(LW // LC,),
        in_specs=[pl.BlockSpec((None, S, LC), lambda c: (c // 2, 0, c % 2)),
                  pl.BlockSpec((None, S, LC), lambda c: (2 + c // 2, 0, c % 2)),
                  pl.BlockSpec((8, LC), lambda c: (0, c)),
                  pl.BlockSpec((None, LC, LC), lambda c: (c, 0, 0)),
                  pl.BlockSpec((None, LC, LC), lambda c: (c, 0, 0))],
        out_specs=[col, col, col],
        out_shape=[jax.ShapeDtypeStruct((S, LW), F32)] * 3,
        scratch_shapes=[pltpu.VMEM((S, LC), F32), pltpu.VMEM((S, LC), F32)],
        args=(proj, proj, lp, wa_t, wx_t))


def lru_bwd(dy, proj, xc_all, hst, lp, wa_t, wx_t, tag):
    S = proj.shape[1]
    nblk = S // 8

    def body(dy_ref, x_ref, g_ref, xc_ref, h_ref, lp_ref, wa_ref, wx_ref,
             dx_ref, dg_ref, dlp_ref, dwa_ref, dwx_ref, c_s, l_s):
        xc = xc_ref[...]
        row = lax.broadcasted_iota(jnp.int32, xc.shape, 0)
        ra, ix, ls, a, mult = _lru_gates(xc, lp_ref, wa_ref, wx_ref)
        g = g_ref[...]
        dyv = dy_ref[...]
        h = h_ref[...]
        dg_ref[...] = (dyv * h * _gelu_grad(g)).astype(BF16)
        c_s[...] = _shift_up(a, 1, row, S)
        l_s[...] = dyv * _gelu(g)
        rowb = lax.broadcasted_iota(jnp.int32, (8, LC), 0)

        def step(i, carry):
            for q in range(SCAN_UNROLL):
                r0 = pl.multiple_of((nblk - 1 - (i * SCAN_UNROLL + q)) * 8, 8)
                C = c_s[pl.ds(r0, 8), :]
                L = l_s[pl.ds(r0, 8), :]
                for d in (1, 2, 4):
                    m = rowb < 8 - d
                    Cs = jnp.where(m, pltpu.roll(C, 8 - d, 0), 1.0)
                    Ls = jnp.where(m, pltpu.roll(L, 8 - d, 0), 0.0)
                    L = C * Ls + L
                    C = C * Cs
                L = L + C * carry
                l_s[pl.ds(r0, 8), :] = L
                carry = L[0:1, :]
            return carry

        lax.fori_loop(0, nblk // SCAN_UNROLL, step, jnp.zeros((1, LC), F32))
        db = l_s[...]
        da = db * _shift_down(h, 1, row)
        ixc = ix * xc
        dmult = db * ixc
        dix = db * (mult * xc)
        dxc = db * (mult * ix)
        dlog_a = da * a - dmult * (a * a) / mult
        dra = dlog_a * (RG_LRU_C * ls)
        dls = _csum(dlog_a * ra) * RG_LRU_C
        lam = lp_ref[7:8, :]
        dlam = dls * _sigmoid(-lam)
        dpa = dra * ra * (1.0 - ra)
        dpx = dix * ix * (1.0 - ix)
        dpab = dpa.astype(BF16)
        dpxb = dpx.astype(BF16)
        xcb = xc.astype(BF16)
        dwa_ref[...] = _dot_tn(xcb, dpab)
        dwx_ref[...] = _dot_tn(xcb, dpxb)
        dxc = dxc + _dot_nt(dpab, wa_ref[...]) + _dot_nt(dpxb, wx_ref[...])
        x = x_ref[...]
        dlp_ref[0:1, :] = _csum(dxc * _shift_down(x, 3, row))
        dlp_ref[1:2, :] = _csum(dxc * _shift_down(x, 2, row))
        dlp_ref[2:3, :] = _csum(dxc * _shift_down(x, 1, row))
        dlp_ref[3:4, :] = _csum(dxc * x)
        dlp_ref[4:5, :] = _csum(dxc)
        dlp_ref[5:6, :] = _csum(dpa)
        dlp_ref[6:7, :] = _csum(dpx)
        dlp_ref[7:8, :] = dlam
        dx = (lp_ref[3:4, :] * dxc + lp_ref[2:3, :] * _shift_up(dxc, 1, row, S)
              + lp_ref[1:2, :] * _shift_up(dxc, 2, row, S) + lp_ref[0:1, :] * _shift_up(dxc, 3, row, S))
        dx_ref[...] = dx.astype(BF16)

    col = pl.BlockSpec((S, LC), lambda c: (0, c))
    pcol = pl.BlockSpec((None, S, LC), lambda c: (c // 2, 0, c % 2))
    return pl.pallas_call(
        body, name=f"lru_bwd_{tag}",
        grid=(LW // LC,),
        in_specs=[col, pcol, pl.BlockSpec((None, S, LC), lambda c: (2 + c // 2, 0, c % 2)), col, col,
                  pl.BlockSpec((8, LC), lambda c: (0, c)),
                  pl.BlockSpec((None, LC, LC), lambda c: (c, 0, 0)),
                  pl.BlockSpec((None, LC, LC), lambda c: (c, 0, 0))],
        out_specs=[pcol, pcol, pl.BlockSpec((8, LC), lambda c: (0, c)),
                   pl.BlockSpec((None, LC, LC), lambda c: (c, 0, 0)),
                   pl.BlockSpec((None, LC, LC), lambda c: (c, 0, 0))],
        out_shape=[jax.ShapeDtypeStruct((2, S, PC), BF16), jax.ShapeDtypeStruct((2, S, PC), BF16),
                   jax.ShapeDtypeStruct((8, LW), F32),
                   jax.ShapeDtypeStruct((LW // LC, LC, LC), F32), jax.ShapeDtypeStruct((LW // LC, LC, LC), F32)],
        scratch_shapes=[pltpu.VMEM((S, LC), F32), pltpu.VMEM((S, LC), F32)],
        compiler_params=_cparams("arbitrary"),
    )(dy, proj, proj, xc_all, hst, lp, wa_t, wx_t)


def _head_stack(zc, lane_head):
    return jnp.concatenate([jnp.where(lane_head == hh, zc, 0.0) for hh in range(HEADS)], axis=0).astype(BF16)


def _gmlp_fwd_parts(u, v, gp_ref, wcat_ref, bz_ref, pavg_ref, ts):
    ug = _gelu(u)
    vg = _gelu(v)
    pavg = pavg_ref[...]
    vc = vg - _seg_mean(vg, pavg)
    rs = lax.rsqrt(_seg_mean(vc * vc, pavg) + EPS)
    vhat = vc * rs
    vh = vhat * gp_ref[0:1, :]
    lane_head = lax.broadcasted_iota(jnp.int32, (CHUNK, GW), 1) // HD
    zs = []
    for n in range(ts // CHUNK):
        stack = _head_stack(vh[n * CHUNK:(n + 1) * CHUNK, :], lane_head)
        zs.append(_dot(wcat_ref[...], stack) + bz_ref[...])
    z = jnp.concatenate(zs, axis=0) if len(zs) > 1 else zs[0]
    return ug, rs, vhat, vh, z


def mix_out_fwd(proj, ylru, x, vec, gp, wcat, bz, pavg, wout_g, tag, ride=None):
    S = x.shape[0]
    ts = min(MIX_TS, S)

    def body(u_ref, v_ref, yl_ref, x_ref, vec_ref, gp_ref, wcat_ref, bz_ref, pavg_ref, wout_ref,
             xo_ref, y_ref, fo_ref):
        u = jnp.concatenate([u_ref[0], u_ref[1]], axis=1)
        v = jnp.concatenate([v_ref[0], v_ref[1]], axis=1)
        ug, _, _, _, z = _gmlp_fwd_parts(u, v, gp_ref, wcat_ref, bz_ref, pavg_ref, ts)
        n1 = _rms(yl_ref[...], gp_ref[1:2, :])
        n2 = _rms(ug * z, gp_ref[2:3, :])
        y = jnp.concatenate([n1, n2], axis=1).astype(BF16)
        y_ref[...] = y
        fo = jnp.zeros((ts, D), F32)
        for k in range(NDEV):
            fo = fo + _dot(y[:, k * OR:(k + 1) * OR], wout_ref[k])
        fo_ref[...] = fo.astype(BF16)
        xo_ref[...] = x_ref[...] + vec_ref[2:3, :] * fo

    row = pl.BlockSpec((ts, D), lambda i: (i, 0))
    full = lambda shp: pl.BlockSpec(shp, lambda i: tuple(0 for _ in shp))
    return _call(
        body, ride, name=f"mix_out_fwd_{tag}",
        grid=(S // ts,),
        in_specs=[pl.BlockSpec((2, ts, PC), lambda i: (2, i, 0)), pl.BlockSpec((2, ts, PC), lambda i: (3, i, 0)),
                  pl.BlockSpec((ts, LW), lambda i: (i, 0)), row, full((8, D)), full((8, GW)),
                  full((CHUNK, HEADS * CHUNK)), full((CHUNK, GW)), full((GW, GW)),
                  pl.BlockSpec((NDEV, OR, D), lambda i: (0, 0, 0))],
        out_specs=[row, row, row],
        out_shape=[jax.ShapeDtypeStruct((S, D), F32), jax.ShapeDtypeStruct((S, D), BF16),
                   jax.ShapeDtypeStruct((S, D), BF16)],
        scratch_shapes=[], args=(proj, proj, ylru, x, vec, gp, wcat, bz, pavg, wout_g))


def mix_out_bwd(dxo, proj, ylru, fo, vec, gp, wcat, wcat_t, bz, pavg, wout_g, tag):
    S = dxo.shape[0]
    ts = min(MIX_TS, S)

    def body(dxo_ref, u_ref, v_ref, yl_ref, fo_ref, vec_ref, gp_ref, wcat_ref, wcatt_ref, bz_ref, pavg_ref,
             wout_ref, dyo_ref, dyl_ref, duv_ref, acc_ref, dgp_ref, dwm_ref, dbz_ref):
        @pl.when(pl.program_id(0) == 0)
        def _():
            acc_ref[...] = jnp.zeros_like(acc_ref)
            dgp_ref[...] = jnp.zeros_like(dgp_ref)
            dwm_ref[...] = jnp.zeros_like(dwm_ref)
            dbz_ref[...] = jnp.zeros_like(dbz_ref)

        dxo_v = dxo_ref[...]
        acc_ref[2:3, :] += _csum(dxo_v * fo_ref[...].astype(F32))
        dyo = (vec_ref[2:3, :] * dxo_v).astype(BF16)
        dyo_ref[...] = dyo
        dn = [_dot_nt(dyo, wout_ref[k]) for k in range(NDEV)]
        dn1 = jnp.concatenate(dn[:NDEV // 2], axis=1)
        dn2 = jnp.concatenate(dn[NDEV // 2:], axis=1)
        dyl, dg1 = _rms_bwd(dn1, yl_ref[...], gp_ref[1:2, :])
        dyl_ref[...] = dyl
        u = jnp.concatenate([u_ref[0], u_ref[1]], axis=1)
        v = jnp.concatenate([v_ref[0], v_ref[1]], axis=1)
        ug, rs, vhat, vh, z = _gmlp_fwd_parts(u, v, gp_ref, wcat_ref, bz_ref, pavg_ref, ts)
        dyg, dg2 = _rms_bwd(dn2, ug * z, gp_ref[2:3, :])
        du = (dyg * z) * _gelu_grad(u)
        dz = dyg * ug
        lane_head = lax.broadcasted_iota(jnp.int32, (CHUNK, GW), 1) // HD
        vhb = vh.astype(BF16)
        dvhs = []
        dbz = jnp.zeros((CHUNK, GW), F32)
        dwm = jnp.zeros((HEADS * CHUNK, CHUNK), F32)
        for n in range(ts // CHUNK):
            dzc = dz[n * CHUNK:(n + 1) * CHUNK, :]
            dbz = dbz + dzc
            stack = _head_stack(dzc, lane_head)
            dwm = dwm + _dot_nt(stack, vhb[n * CHUNK:(n + 1) * CHUNK, :])
            dvhs.append(_dot(wcatt_ref[...], stack))
        dbz_ref[...] += dbz
        dwm_ref[...] += dwm
        dvh = jnp.concatenate(dvhs, axis=0) if len(dvhs) > 1 else dvhs[0]
        pavg = pavg_ref[...]
        dvn = _csum(dvh * vhat)
        dvhat = dvh * gp_ref[0:1, :]
        dvg = rs * (dvhat - _seg_mean(dvhat, pavg) - vhat * _seg_mean(dvhat * vhat, pavg))
        dv = dvg * _gelu_grad(v)
        duv_ref[0] = du[:, :PC].astype(BF16)
        duv_ref[1] = du[:, PC:].astype(BF16)
        duv_ref[2] = dv[:, :PC].astype(BF16)
        duv_ref[3] = dv[:, PC:].astype(BF16)
        dgp_ref[0:1, :] += dvn
        dgp_ref[1:2, :] += dg1
        dgp_ref[2:3, :] += dg2

    row = pl.BlockSpec((ts, D), lambda i: (i, 0))
    full = lambda shp: pl.BlockSpec(shp, lambda i: tuple(0 for _ in shp))
    return pl.pallas_call(
        body, name=f"mix_out_bwd_{tag}",
        grid=(S // ts,),
        in_specs=[row, pl.BlockSpec((2, ts, PC), lambda i: (2, i, 0)), pl.BlockSpec((2, ts, PC), lambda i: (3, i, 0)),
                  pl.BlockSpec((ts, LW), lambda i: (i, 0)), row, full((8, D)), full((8, GW)),
                  full((CHUNK, HEADS * CHUNK)), full((CHUNK, HEADS * CHUNK)), full((CHUNK, GW)), full((GW, GW)),
                  pl.BlockSpec((NDEV, OR, D), lambda i: (0, 0, 0))],
        out_specs=[row, pl.BlockSpec((ts, LW), lambda i: (i, 0)), pl.BlockSpec((4, ts, PC), lambda i: (0, i, 0)),
                   full((8, D)), full((8, GW)), full((HEADS * CHUNK, CHUNK)), full((CHUNK, GW))],
        out_shape=[jax.ShapeDtypeStruct((S, D), BF16), jax.ShapeDtypeStruct((S, LW), F32),
                   jax.ShapeDtypeStruct((4, S, PC), BF16), jax.ShapeDtypeStruct((8, D), F32),
                   jax.ShapeDtypeStruct((8, GW), F32), jax.ShapeDtypeStruct((HEADS * CHUNK, CHUNK), F32),
                   jax.ShapeDtypeStruct((CHUNK, GW), F32)],
        compiler_params=_cparams("arbitrary"),
    )(dxo, proj, proj, ylru, fo, vec, gp, wcat, wcat_t, bz, pavg, wout_g)


def final_loss(x, target, gain):
    S = x.shape[0]
    ts = min(512, S)

    def body(x_ref, t_ref, g_ref, loss_ref, dx_ref, dg_ref):
        @pl.when(pl.program_id(0) == 0)
        def _():
            loss_ref[...] = jnp.zeros_like(loss_ref)
            dg_ref[...] = jnp.zeros_like(dg_ref)

        xv = x_ref[...]
        gain_v = g_ref[0:1, :]
        rstd = lax.rsqrt(_rmean(xv * xv) + EPS)
        xhat = xv * rstd
        err = xhat * gain_v - t_ref[...]
        loss_ref[...] += 0.5 * _csum(_rmean(err * err))
        dy = err * (1.0 / D)
        dg_ref[0:1, :] += _csum(dy * xhat)
        dxhat = dy * gain_v
        dx_ref[...] = rstd * (dxhat - xhat * _rmean(dxhat * xhat))

    row = pl.BlockSpec((ts, D), lambda i: (i, 0))
    return pl.pallas_call(
        body, name="final_loss",
        grid=(S // ts,),
        in_specs=[row, row, pl.BlockSpec((8, D), lambda i: (0, 0))],
        out_specs=[pl.BlockSpec((8, 128), lambda i: (0, 0)), row, pl.BlockSpec((8, D), lambda i: (0, 0))],
        out_shape=[jax.ShapeDtypeStruct((8, 128), F32), jax.ShapeDtypeStruct((S, D), F32),
                   jax.ShapeDtypeStruct((8, D), F32)],
        compiler_params=_cparams("arbitrary"),
    )(x, target, gain)


def _vec(mod_l, j, gain):
    return jnp.concatenate([mod_l[3 * j:3 * j + 3], gain[None, :], jnp.zeros((4, D), F32)], axis=0)


def _block_diag_tiles(w):
    w4 = w.reshape(LW // LC, 2, HD, HD)
    eye2 = jnp.eye(2, dtype=w.dtype)
    return (w4[:, :, :, None, :] * eye2[None, :, None, :, None]).reshape(LW // LC, LC, LC).astype(BF16)


def _block_diag_extract(dw):
    d5 = dw.reshape(LW // LC, 2, HD, 2, HD)
    return jnp.einsum('cihkj,ik->cihj', d5, jnp.eye(2, dtype=dw.dtype)).reshape(HEADS, HD, HD)


def _layer_params(l, p, conv_w_full):
    lp = jnp.concatenate([conv_w_full[l], p['conv_b'][l][None], p['gate_a_b'][l].reshape(1, LW),
                          p['gate_x_b'][l].reshape(1, LW), p['lru_lambda'][l][None]], axis=0)
    gp = jnp.concatenate([p['v_norm'][l][None], p['lru_out_norm'][l][None], p['gmlp_out_norm'][l][None],
                          jnp.zeros((5, GW), F32)], axis=0)
    ws = p['spatial_w'][l] * jnp.tril(jnp.ones((CHUNK, CHUNK), F32))
    wcat = ws.transpose(1, 0, 2).reshape(CHUNK, HEADS * CHUNK).astype(BF16)
    wcat_t = ws.transpose(2, 0, 1).reshape(CHUNK, HEADS * CHUNK).astype(BF16)
    bz = jnp.repeat(p['spatial_b'][l].T, HD, axis=1)
    return dict(lp=lp, gp=gp, wcat=wcat, wcat_t=wcat_t, bz=bz,
                wa_t=_block_diag_tiles(p['gate_a_w'][l]), wx_t=_block_diag_tiles(p['gate_x_w'][l]))


def _pavg():
    return jnp.kron(jnp.eye(HEADS, dtype=F32), jnp.full((HD, HD), 1.0 / HD, F32)).astype(BF16)


GATHER_RIDES = {
    ('ffn_a', 0): [('w_in', 0), ('gu', DEPTH)],
    ('mix_in', 0): [('w_out', 0)],
    ('lru', 0): [('down', DEPTH)],
    ('mix_out', 0): [('down', 1)],
    ('ffn_b', 0): [('gu', 1), ('w_in', 1)],
    ('ffn_a', 1): [('gu', DEPTH + 1), ('w_out', 1)],
    ('mix_in', 1): [('down', DEPTH + 1)],
}


def local_fwd_bwd(me_arr, x, target, mod, p, loc, gathered, conv_w_full):
    pavg = _pavg()
    g = dict(gathered)

    def ride(call, l):
        todo = GATHER_RIDES.get((call, l))
        return None if todo is None else (todo, GatherRide([(loc[kind], slot) for kind, slot in todo]))

    def run(fn, call, l, *args):
        r = ride(call, l)
        outs, got = fn(*args, ride=None if r is None else r[1])
        if r is not None:
            g.update(dict(zip(r[0], got)))
        return outs

    saved = []
    h = x
    for l in range(DEPTH):
        q = _layer_params(l, p, conv_w_full)
        v1 = _vec(mod[l], 0, p['ffn1_norm'][l])
        vm = _vec(mod[l], 1, p['mix_norm'][l])
        v2 = _vec(mod[l], 2, p['ffn2_norm'][l])
        x0 = h
        x1, h1, gu1, f1 = run(ffn_fwd, 'ffn_a', l, x0, v1, g['gu', l], g['down', l], f"a{l}")
        hm, proj = run(mix_in_fwd, 'mix_in', l, x1, vm, g['w_in', l], f"{l}")
        ylru, xc, hst = run(lru_fwd, 'lru', l, proj, q['lp'], q['wa_t'], q['wx_t'], f"{l}")
        x2, y, fo = run(mix_out_fwd, 'mix_out', l, proj, ylru, x1, vm, q['gp'], q['wcat'], q['bz'], pavg,
                        g['w_out', l], f"{l}")
        x3, h2, gu2, f2 = run(ffn_fwd, 'ffn_b', l, x2, v2, g['gu', DEPTH + l], g['down', DEPTH + l], f"b{l}")
        saved.append(dict(q=q, v1=v1, vm=vm, v2=v2, x0=x0, x1=x1, x2=x2, h1=h1, gu1=gu1, f1=f1, hm=hm, proj=proj,
                          ylru=ylru, xc=xc, hst=hst, y=y, fo=fo, h2=h2, gu2=gu2, f2=f2))
        h = x3
    fin = jnp.concatenate([p['final_norm'][None], jnp.zeros((7, D), F32)], axis=0)
    loss8, dx, dfin = final_loss(h, target, fin)
    loss = loss8[0, 0]

    big = dict(gu=None, down=None, w_in=None, w_out=None)
    small = {k: [None] * DEPTH for k in ('ffn1_norm', 'mix_norm', 'ffn2_norm', 'conv_w', 'conv_b', 'gate_a_w',
                                         'gate_a_b', 'gate_x_w', 'gate_x_b', 'lru_lambda', 'v_norm', 'spatial_w',
                                         'spatial_b', 'lru_out_norm', 'gmlp_out_norm')}
    dmod = [None] * DEPTH
    tril = jnp.tril(jnp.ones((CHUNK, CHUNK), F32))
    for l in reversed(range(DEPTH)):
        sv = saved[l]
        q = sv['q']
        dx2, dgu, a, df, acc2 = ffn_bwd(dx, sv['x2'], sv['gu2'], sv['f2'], sv['v2'],
                                        g['gu', DEPTH + l], g['down', DEPTH + l], f"b{l}")
        big['gu'] = tn_matmul_scatter(me_arr, dgu, sv['h2'][None], DEPTH + l, 2 * DEPTH, big['gu'], f"dw_gu_b{l}")
        big['down'] = tn_matmul_scatter(me_arr, a, df[None], DEPTH + l, 2 * DEPTH, big['down'], f"dw_down_b{l}", split=2)
        dyo, dylru, duv, accmo, dgp, dwm, dbz = mix_out_bwd(dx2, sv['proj'], sv['ylru'], sv['fo'], sv['vm'], q['gp'],
                                                             q['wcat'], q['wcat_t'], q['bz'], pavg, g['w_out', l], f"{l}")
        big['w_out'] = tn_matmul_scatter(me_arr, sv['y'][None], dyo[None], l, DEPTH, big['w_out'], f"dw_out_{l}",
                                         split=NDEV)
        dxl, dgl, dlp, dwa, dwx = lru_bwd(dylru, sv['proj'], sv['xc'], sv['hst'], q['lp'], q['wa_t'], q['wx_t'], f"{l}")
        dproj = jnp.concatenate([dxl, dgl, duv], axis=0)
        dx1, accmi = mix_in_bwd(dproj, sv['x1'], dx2, sv['vm'], g['w_in', l], f"{l}")
        big['w_in'] = tn_matmul_scatter(me_arr, sv['hm'][None], dproj, l, DEPTH, big['w_in'], f"dw_in_{l}")
        dx0, dgu, a, df, acc1 = ffn_bwd(dx1, sv['x0'], sv['gu1'], sv['f1'], sv['v1'],
                                        g['gu', l], g['down', l], f"a{l}")
        big['gu'] = tn_matmul_scatter(me_arr, dgu, sv['h1'][None], l, 2 * DEPTH, big['gu'], f"dw_gu_a{l}")
        big['down'] = tn_matmul_scatter(me_arr, a, df[None], l, 2 * DEPTH, big['down'], f"dw_down_a{l}", split=2)
        dx = dx0
        dmod[l] = jnp.concatenate([acc1[0:3], accmi[0:2], accmo[2:3], acc2[0:3]], axis=0)
        small['ffn1_norm'][l] = acc1[3]
        small['mix_norm'][l] = accmi[3]
        small['ffn2_norm'][l] = acc2[3]
        small['conv_w'][l] = dlp[0:4]
        small['conv_b'][l] = dlp[4]
        small['gate_a_b'][l] = dlp[5].reshape(HEADS, HD)
        small['gate_x_b'][l] = dlp[6].reshape(HEADS, HD)
        small['lru_lambda'][l] = dlp[7]
        small['gate_a_w'][l] = _block_diag_extract(dwa)
        small['gate_x_w'][l] = _block_diag_extract(dwx)
        small['v_norm'][l] = dgp[0]
        small['lru_out_norm'][l] = dgp[1]
        small['gmlp_out_norm'][l] = dgp[2]
        small['spatial_w'][l] = dwm.reshape(HEADS, CHUNK, CHUNK) * tril
        small['spatial_b'][l] = dbz.reshape(CHUNK, HEADS, HD).sum(-1).T
    small = {k: jnp.stack(v) for k, v in small.items()}
    small['final_norm'] = dfin[0]
    return loss, dx, big, small, jnp.stack(dmod)


def ada_fwd(c_all, w_ada, b_loc):
    def body(c_ref, w_ref, b_ref, mod_ref, sc_ref):
        cv = c_ref[...]
        sc = cv * _sigmoid(cv)
        sc_ref[...] = sc
        mod_ref[...] = _dot3(sc, w_ref[...]) + b_ref[...]

    return pl.pallas_call(
        body, name="ada_fwd",
        grid=(DEPTH,),
        in_specs=[pl.BlockSpec((NDEV, D), lambda l: (0, 0)), pl.BlockSpec((None, D, AC), lambda l: (l, 0, 0)),
                  pl.BlockSpec((None, 1, AC), lambda l: (l, 0, 0))],
        out_specs=[pl.BlockSpec((None, NDEV, AC), lambda l: (l, 0, 0)), pl.BlockSpec((NDEV, D), lambda l: (0, 0))],
        out_shape=[jax.ShapeDtypeStruct((DEPTH, NDEV, AC), F32), jax.ShapeDtypeStruct((NDEV, D), F32)],
        compiler_params=_cparams("arbitrary"),
    )(c_all, w_ada, b_loc)


def ada_bwd(sc_t, dmod_cols):
    def body(sc_ref, dm_ref, g_ref):
        sc = sc_ref[...]
        dm = dm_ref[...]
        acc = sc[:, 0:1] * dm[0:1, :]
        for b in range(1, NDEV):
            acc = acc + sc[:, b:b + 1] * dm[b:b + 1, :]
        g_ref[...] = acc

    return pl.pallas_call(
        body, name="ada_bwd",
        grid=(DEPTH,),
        in_specs=[pl.BlockSpec((D, NDEV), lambda l: (0, 0)), pl.BlockSpec((None, NDEV, AC), lambda l: (l, 0, 0))],
        out_specs=pl.BlockSpec((None, None, D, AC), lambda l: (0, l, 0, 0)),
        out_shape=jax.ShapeDtypeStruct((1, DEPTH, D, AC), F32),
        compiler_params=_cparams("arbitrary"),
    )(sc_t, dmod_cols)


def _row_tile(rows, cols):
    if rows * cols <= 512 * 1024:
        return rows
    for tr in (512, 384, 352, 256, 128, 64, 32, 16, 8):
        if rows % tr == 0:
            return tr
    return rows


def adamw(gparts, slot0, w, m, v, name):
    P, _, R, C = gparts.shape
    L = w.shape[0]
    tr = _row_tile(R, C)

    def body(g_ref, w_ref, m_ref, v_ref, go_ref, do_ref, mo_ref, vo_ref):
        g = g_ref[0].astype(F32)
        for p in range(1, P):
            g = g + g_ref[p].astype(F32)
        go_ref[...] = g
        mn = ADAM_B1 * m_ref[...] + (1.0 - ADAM_B1) * g
        vn = ADAM_B2 * v_ref[...] + (1.0 - ADAM_B2) * (g * g)
        mo_ref[...] = mn
        vo_ref[...] = vn
        m_hat = mn / (1.0 - ADAM_B1 ** ADAM_STEP)
        v_hat = vn / (1.0 - ADAM_B2 ** ADAM_STEP)
        do_ref[...] = -ADAM_LR * (m_hat / (jnp.sqrt(v_hat) + ADAM_EPS) + ADAM_WD * w_ref[...])

    blk = pl.BlockSpec((None, tr, C), lambda l, i: (l, i, 0))
    return pl.pallas_call(
        body, name=name,
        grid=(L, R // tr),
        in_specs=[pl.BlockSpec((P, None, tr, C), lambda l, i: (0, slot0 + l, i, 0)), blk, blk, blk],
        out_specs=[blk, blk, blk, blk],
        out_shape=[jax.ShapeDtypeStruct((L, R, C), F32)] * 4,
        compiler_params=_cparams("arbitrary", "arbitrary"),
    )(gparts, w, m, v)


def sum_parts(parts):
    P, R, C = parts.shape

    def body(p_ref, o_ref):
        acc = p_ref[0]
        for p in range(1, P):
            acc = acc + p_ref[p]
        o_ref[...] = acc

    return pl.pallas_call(
        body, name="sum_parts",
        in_specs=[pl.BlockSpec(memory_space=pltpu.VMEM)],
        out_specs=pl.BlockSpec(memory_space=pltpu.VMEM),
        out_shape=jax.ShapeDtypeStruct((R, C), F32),
    )(parts)


WEIGHTS = ['w_ada', 'b_ada', 'ffn1_norm', 'ffn1_w_gu', 'ffn1_w_down', 'mix_norm', 'w_in', 'conv_w', 'conv_b',
           'gate_a_w', 'gate_a_b', 'gate_x_w', 'gate_x_b', 'lru_lambda', 'v_norm', 'spatial_w', 'spatial_b',
           'lru_out_norm', 'gmlp_out_norm', 'w_out', 'ffn2_norm', 'ffn2_w_gu', 'ffn2_w_down', 'final_norm']
PACKED = ['b_ada', 'ffn1_norm', 'mix_norm', 'conv_b', 'gate_a_w', 'gate_a_b', 'gate_x_w', 'gate_x_b', 'lru_lambda',
          'v_norm', 'spatial_w', 'spatial_b', 'lru_out_norm', 'gmlp_out_norm', 'ffn2_norm', 'final_norm', 'conv_w']
PACK_LANES = 128
PACK_ROW_ALIGN = 8 * NDEV


def _pack(d):
    flat = jnp.concatenate([d[k].reshape(-1).astype(F32) for k in PACKED])
    rows = -(-flat.shape[0] // (PACK_LANES * PACK_ROW_ALIGN)) * PACK_ROW_ALIGN
    flat = jnp.concatenate([flat, jnp.zeros((rows * PACK_LANES - flat.shape[0],), F32)])
    return flat.reshape(rows, PACK_LANES)


def _unpack(buf, shapes):
    flat = buf.reshape(-1)
    out, off = {}, 0
    for k in PACKED:
        size = 1
        for s in shapes[k]:
            size *= s
        out[k] = flat[off:off + size].reshape(shapes[k])
        off += size
    return out


def kernel(x, c, w_ada, b_ada, ffn1_norm, ffn1_w_gu, ffn1_w_down, mix_norm, w_in, conv_w, conv_b, gate_a_w, gate_a_b, gate_x_w, gate_x_b, lru_lambda, v_norm, spatial_w, spatial_b, lru_out_norm, gmlp_out_norm, w_out, ffn2_norm, ffn2_w_gu, ffn2_w_down, final_norm, loss_target, m_w_ada, m_b_ada, m_ffn1_norm, m_ffn1_w_gu, m_ffn1_w_down, m_mix_norm, m_w_in, m_conv_w, m_conv_b, m_gate_a_w, m_gate_a_b, m_gate_x_w, m_gate_x_b, m_lru_lambda, m_v_norm, m_spatial_w, m_spatial_b, m_lru_out_norm, m_gmlp_out_norm, m_w_out, m_ffn2_norm, m_ffn2_w_gu, m_ffn2_w_down, m_final_norm, v_w_ada, v_b_ada, v_ffn1_norm, v_ffn1_w_gu, v_ffn1_w_down, v_mix_norm, v_w_in, v_conv_w, v_conv_b, v_gate_a_w, v_gate_a_b, v_gate_x_w, v_gate_x_b, v_lru_lambda, v_v_norm, v_spatial_w, v_spatial_b, v_lru_out_norm, v_gmlp_out_norm, v_w_out, v_ffn2_norm, v_ffn2_w_gu, v_ffn2_w_down, v_final_norm):
    w = dict(w_ada=w_ada, b_ada=b_ada, ffn1_norm=ffn1_norm, ffn1_w_gu=ffn1_w_gu, ffn1_w_down=ffn1_w_down, mix_norm=mix_norm, w_in=w_in, conv_w=conv_w, conv_b=conv_b, gate_a_w=gate_a_w, gate_a_b=gate_a_b, gate_x_w=gate_x_w, gate_x_b=gate_x_b, lru_lambda=lru_lambda, v_norm=v_norm, spatial_w=spatial_w, spatial_b=spatial_b, lru_out_norm=lru_out_norm, gmlp_out_norm=gmlp_out_norm, w_out=w_out, ffn2_norm=ffn2_norm, ffn2_w_gu=ffn2_w_gu, ffn2_w_down=ffn2_w_down, final_norm=final_norm)
    m = dict(w_ada=m_w_ada, b_ada=m_b_ada, ffn1_norm=m_ffn1_norm, ffn1_w_gu=m_ffn1_w_gu, ffn1_w_down=m_ffn1_w_down, mix_norm=m_mix_norm, w_in=m_w_in, conv_w=m_conv_w, conv_b=m_conv_b, gate_a_w=m_gate_a_w, gate_a_b=m_gate_a_b, gate_x_w=m_gate_x_w, gate_x_b=m_gate_x_b, lru_lambda=m_lru_lambda, v_norm=m_v_norm, spatial_w=m_spatial_w, spatial_b=m_spatial_b, lru_out_norm=m_lru_out_norm, gmlp_out_norm=m_gmlp_out_norm, w_out=m_w_out, ffn2_norm=m_ffn2_norm, ffn2_w_gu=m_ffn2_w_gu, ffn2_w_down=m_ffn2_w_down, final_norm=m_final_norm)
    v = dict(w_ada=v_w_ada, b_ada=v_b_ada, ffn1_norm=v_ffn1_norm, ffn1_w_gu=v_ffn1_w_gu, ffn1_w_down=v_ffn1_w_down, mix_norm=v_mix_norm, w_in=v_w_in, conv_w=v_conv_w, conv_b=v_conv_b, gate_a_w=v_gate_a_w, gate_a_b=v_gate_a_b, gate_x_w=v_gate_x_w, gate_x_b=v_gate_x_b, lru_lambda=v_lru_lambda, v_norm=v_v_norm, spatial_w=v_spatial_w, spatial_b=v_spatial_b, lru_out_norm=v_lru_out_norm, gmlp_out_norm=v_gmlp_out_norm, w_out=v_w_out, ffn2_norm=v_ffn2_norm, ffn2_w_gu=v_ffn2_w_gu, ffn2_w_down=v_ffn2_w_down, final_norm=v_final_norm)
    me = 4 * lax.axis_index("x") + 2 * lax.axis_index("y") + lax.axis_index("c")

    loc = dict(gu=jnp.concatenate([ffn1_w_gu, ffn2_w_gu], axis=0).astype(BF16),
               down=jnp.concatenate([ffn1_w_down, ffn2_w_down], axis=0).astype(BF16),
               w_in=w_in.astype(BF16), w_out=w_out.astype(BF16))
    c_g, conv_g, gu0, down0 = all_gather([(c, None), (conv_w, None), (loc['gu'], 0), (loc['down'], 0)], "gather_first")
    conv_w_full = conv_g.transpose(1, 2, 0, 3).reshape(DEPTH, CONV_WIDTH, LW)

    b_loc = lax.dynamic_slice(b_ada, (0, me * AC), (DEPTH, AC)).reshape(DEPTH, 1, AC)
    mod_cols, sc_all = ada_fwd(c_g.reshape(NDEV, D), w_ada, b_loc)
    (mod_rows,) = all_to_all([mod_cols.transpose(1, 0, 2)], "scatter_mod")
    mod = mod_rows.transpose(1, 0, 2).reshape(DEPTH, NMOD, D)

    small_w = {k: w[k] for k in PACKED if k != 'conv_w'}
    me_arr = jnp.reshape(me, (1,)).astype(jnp.int32)
    loss_loc, dx, big, small_g, dmod = local_fwd_bwd(me_arr, x[0], loss_target[0], mod, small_w, loc,
                                                     {('gu', 0): gu0, ('down', 0): down0}, conv_w_full)
    loss = lax.psum(loss_loc, ("x", "y", "c"))

    small_g['b_ada'] = dmod.reshape(DEPTH, NMOD * D)
    gpack = _pack(small_g)
    rows = gpack.shape[0]
    dmod_out = dmod.reshape(DEPTH, NDEV, AC).transpose(1, 0, 2)
    dmod_r, pack_r = all_to_all([dmod_out, gpack.reshape(NDEV, rows // NDEV, PACK_LANES)], "scatter_grads")
    (gsum_g,) = all_gather([(sum_parts(pack_r), None)], "gather_small_grads")
    gsum = gsum_g.reshape(1, 1, rows, PACK_LANES)

    res = {}
    t = lambda a: a.transpose(0, 2, 1)
    res['ffn1_w_gu'] = tuple(t(r) for r in adamw(big['gu'], 0, t(w['ffn1_w_gu']), t(m['ffn1_w_gu']), t(v['ffn1_w_gu']),
                                                 "adamw_gu_a"))
    res['ffn2_w_gu'] = tuple(t(r) for r in adamw(big['gu'], DEPTH, t(w['ffn2_w_gu']), t(m['ffn2_w_gu']),
                                                 t(v['ffn2_w_gu']), "adamw_gu_b"))
    res['ffn1_w_down'] = adamw(big['down'], 0, w['ffn1_w_down'], m['ffn1_w_down'], v['ffn1_w_down'], "adamw_down_a")
    res['ffn2_w_down'] = adamw(big['down'], DEPTH, w['ffn2_w_down'], m['ffn2_w_down'], v['ffn2_w_down'], "adamw_down_b")
    res['w_in'] = adamw(big['w_in'], 0, w['w_in'], m['w_in'], v['w_in'], "adamw_w_in")
    res['w_out'] = adamw(big['w_out'], 0, w['w_out'], m['w_out'], v['w_out'], "adamw_w_out")
    g_ada = ada_bwd(sc_all.T, dmod_r.transpose(1, 0, 2))
    res['w_ada'] = adamw(g_ada, 0, w['w_ada'], m['w_ada'], v['w_ada'], "adamw_w_ada")
    shapes = {k: w[k].shape for k in PACKED}
    shapes['conv_w'] = (DEPTH, CONV_WIDTH, LW)
    dummy = jnp.zeros(shapes['conv_w'], F32)
    packs = adamw(gsum, 0, _pack({**small_w, 'conv_w': dummy})[None], _pack({**{k: m[k] for k in small_w}, 'conv_w': dummy})[None],
                  _pack({**{k: v[k] for k in small_w}, 'conv_w': dummy})[None], "adamw_small")
    unpacked = [_unpack(b[0], shapes) for b in packs]
    for k in small_w:
        res[k] = tuple(u[k] for u in unpacked)
    gconv = lax.dynamic_slice(unpacked[0]['conv_w'], (0, 0, me * (LW // NDEV)), (DEPTH, CONV_WIDTH, LW // NDEV))
    cshape = (1, DEPTH * CONV_WIDTH, LW // NDEV)
    rc = adamw(gconv.reshape((1,) + cshape), 0, conv_w.reshape(cshape), m['conv_w'].reshape(cshape),
               v['conv_w'].reshape(cshape), "adamw_conv_w")
    res['conv_w'] = tuple(r.reshape(conv_w.shape) for r in rc)

    return (loss, dx[None], *[res[k][0] for k in WEIGHTS], *[res[k][1] for k in WEIGHTS],
            *[res[k][2] for k in WEIGHTS], *[res[k][3] for k in WEIGHTS])
```

```python
import jax
import jax.numpy as jnp
from jax import lax
from jax.experimental import pallas as pl
from jax.experimental.pallas import tpu as pltpu

F32 = jnp.float32
BF16 = jnp.bfloat16

NDEV = 8
DEPTH = 2
D = 1024
DFF = 2816
FC = 2 * DFF // NDEV
NCHUNK = DFF // FC
DR = DFF // NDEV
LW = 512
GW = 512
HD = 64
HEADS = 8
CHUNK = 128
PC = 2 * (LW + GW) // NDEV
OR = D // NDEV
NMOD = 9
AC = NMOD * D // NDEV
LC = 128
EPS = 1e-6
RG_LRU_C = 8.0
CONV_WIDTH = 4

ADAM_LR = 0.001
ADAM_B1 = 0.9
ADAM_B2 = 0.999
ADAM_EPS = 1e-08
ADAM_WD = 0.01
ADAM_STEP = 10

VMEM_LIMIT_BYTES = 60 * 1024 * 1024
MESH = pl.DeviceIdType.MESH
ANY = pl.BlockSpec(memory_space=pl.ANY)


def _cparams(*sem):
    return pltpu.CompilerParams(dimension_semantics=tuple(sem) if sem else None,
                                vmem_limit_bytes=VMEM_LIMIT_BYTES)


def _dot(a, b):
    return jnp.dot(a, b, preferred_element_type=F32)


def _dot_nt(a, b):
    return lax.dot_general(a, b, (((1,), (1,)), ((), ())), preferred_element_type=F32)


def _dot_tn(a, b):
    return lax.dot_general(a, b, (((0,), (0,)), ((), ())), preferred_element_type=F32)


def _split(a):
    hi = a.astype(BF16)
    lo = (a - hi.astype(F32)).astype(BF16)
    return hi, lo


def _dot3(a, b):
    ah, al = _split(a)
    bh, bl = _split(b)
    return _dot(ah, bh) + (_dot(ah, bl) + _dot(al, bh))


def _csum(a):
    return jnp.sum(a, axis=0, keepdims=True)


def _rmean(a):
    return jnp.mean(a, axis=-1, keepdims=True)


def _sigmoid(a):
    return 1.0 / (1.0 + jnp.exp(-a))


_GELU_K = 0.7978845608028654
_GELU_C = 0.044715


def _gelu(a):
    return 0.5 * a * (1.0 + jnp.tanh(_GELU_K * (a + _GELU_C * a * a * a)))


def _gelu_grad(a):
    t = jnp.tanh(_GELU_K * (a + _GELU_C * a * a * a))
    return 0.5 * (1.0 + t) + 0.5 * a * (1.0 - t * t) * (_GELU_K * (1.0 + 3.0 * _GELU_C * a * a))


def _norm_mod(x, gain, scale, shift):
    rstd = lax.rsqrt(_rmean(x * x) + EPS)
    return (x * rstd * gain) * (1.0 + scale) + shift


def _norm_mod_bwd(dh, x, gain, scale):
    rstd = lax.rsqrt(_rmean(x * x) + EPS)
    xhat = x * rstd
    dshift = _csum(dh)
    dscale = _csum(dh * (xhat * gain))
    dhn = dh * (1.0 + scale)
    dgain = _csum(dhn * xhat)
    dxhat = dhn * gain
    dx = rstd * (dxhat - xhat * _rmean(dxhat * xhat))
    return dx, dshift, dscale, dgain


def _rms(x, gain):
    rstd = lax.rsqrt(_rmean(x * x) + EPS)
    return x * rstd * gain


def _rms_bwd(dy, x, gain):
    rstd = lax.rsqrt(_rmean(x * x) + EPS)
    xhat = x * rstd
    dgain = _csum(dy * xhat)
    dxhat = dy * gain
    return rstd * (dxhat - xhat * _rmean(dxhat * xhat)), dgain


def _seg_mean(a, pavg):
    hi, lo = _split(a)
    return _dot(hi, pavg) + _dot(lo, pavg)


def _block_copies(src_hbm, dst_vmem, sems, rows):
    copies = []
    for k in range(NDEV):
        dst = dst_vmem.at[k] if rows is None else dst_vmem.at[pl.ds(k * rows, rows)]
        copies.append(pltpu.make_async_copy(src_hbm.at[k], dst, sems.at[k]))
    return copies


def _ffn_weight_fetch(wgu_hbm, wd_hbm, wgu_v, wd_v, sems):
    @pl.when(pl.program_id(0) == 0)
    def _():
        copies = _block_copies(wgu_hbm, wgu_v, sems.at[0], None) + _block_copies(wd_hbm, wd_v, sems.at[1], DR)
        for cp in copies:
            cp.start()
        for cp in copies:
            cp.wait()


def _place():
    return lax.axis_index("x"), lax.axis_index("y"), lax.axis_index("c")


def _slot(p):
    return 4 * p[0] + 2 * p[1] + p[2]


class GatherRide:
    def __init__(self, srcs):
        self.n = len(srcs)
        self.index = [i for _, i in srcs]
        self.args = [a for a, _ in srcs]
        self.out_shape = [jax.ShapeDtypeStruct((NDEV,) + (a.shape if i is None else a.shape[1:]), a.dtype)
                          for a, i in srcs]
        self.scratch = [pltpu.SemaphoreType.DMA((self.n, NDEV - 1)), pltpu.SemaphoreType.DMA((self.n, NDEV - 1)),
                        pltpu.SemaphoreType.DMA((self.n,))]

    def hooks(self, ins, outs, sems):
        send_sems, recv_sems, local_sems = sems
        n = self.n
        x, y, c = _place()
        me, sibling = (x, y, c), (x, y, 1 - c)
        chips = [(1 - x, y), (x, 1 - y), (1 - x, 1 - y)]

        def local(a):
            return ins[a] if self.index[a] is None else ins[a].at[self.index[a]]

        def copy(a, k, block, to, src=None):
            dst = outs[a].at[_slot(block)]
            return pltpu.make_async_remote_copy(
                src_ref=dst if src is None else src, dst_ref=dst,
                send_sem=send_sems.at[a, k], recv_sem=recv_sems.at[a, k],
                device_id=to, device_id_type=MESH)

        def mine():
            return [pltpu.make_async_copy(local(a), outs[a].at[_slot(me)], local_sems.at[a]) for a in range(n)]

        def first():
            cps = []
            for a in range(n):
                cps.append(copy(a, 0, me, sibling, src=local(a)))
                cps += [copy(a, 1 + j, me, (*chip, c), src=local(a)) for j, chip in enumerate(chips)]
            return cps

        def passed():
            return [copy(a, 4 + j, (*chip, c), sibling) for j, chip in enumerate(chips) for a in range(n)]

        def start():
            for cp in mine() + first():
                cp.start()

        def mid():
            for j, chip in enumerate(chips):
                for a in range(n):
                    copy(a, 1 + j, (*chip, c), me).wait_recv()
                    copy(a, 4 + j, (*chip, c), sibling).start()

        def finish():
            for a in range(n):
                copy(a, 0, sibling, me).wait_recv()
                for j, chip in enumerate(chips):
                    copy(a, 4 + j, (*chip, 1 - c), me).wait_recv()
            for cp in first() + passed():
                cp.wait_send()
            for cp in mine():
                cp.wait()

        return start, mid, finish


def all_gather(srcs, name):
    ride = GatherRide(srcs)
    n = ride.n

    def body(*refs):
        start, mid, finish = ride.hooks(refs[:n], refs[n:2 * n], refs[2 * n:])
        start()
        mid()
        finish()

    return pl.pallas_call(
        body, name=name,
        in_specs=[ANY] * n, out_specs=[ANY] * n, out_shape=ride.out_shape, scratch_shapes=ride.scratch,
    )(*ride.args)


def _call(core, ride, *, name, grid, in_specs, out_specs, out_shape, scratch_shapes, args):
    if ride is None:
        outs = pl.pallas_call(core, name=name, grid=grid, in_specs=in_specs, out_specs=out_specs,
                              out_shape=out_shape, scratch_shapes=scratch_shapes,
                              compiler_params=_cparams("arbitrary"))(*args)
        return outs, []
    n_in, n_out, n_sc, n = len(in_specs), len(out_shape), len(scratch_shapes), ride.n
    nsteps = grid[0]
    mid_step = max(nsteps - 2, 0)

    def body(*refs):
        cuts = [n_in, n_in + n, n_in + n + n_out, n_in + 2 * n + n_out, n_in + 2 * n + n_out + n_sc]
        ci, ri, co, ro, cs, rs = (refs[a:b] for a, b in zip([0] + cuts, cuts + [len(refs)]))
        start, mid, finish = ride.hooks(ri, ro, rs)
        i = pl.program_id(0)
        pl.when(i == 0)(start)
        core(*ci, *co, *cs)
        pl.when(i == mid_step)(mid)
        pl.when(i == nsteps - 1)(finish)

    outs = pl.pallas_call(
        body, name=name, grid=grid,
        in_specs=list(in_specs) + [ANY] * n, out_specs=list(out_specs) + [ANY] * n,
        out_shape=list(out_shape) + ride.out_shape, scratch_shapes=list(scratch_shapes) + ride.scratch,
        compiler_params=_cparams("arbitrary"))(*args, *ride.args)
    return outs[:n_out], outs[n_out:]


def all_to_all(arrs, name):
    n = len(arrs)

    def body(*refs):
        ins, outs = refs[:n], refs[n:2 * n]
        send_sems, recv_sems, local_sems = refs[2 * n:]
        x, y, c = _place()
        me = (x, y, c)

        def peer(k):
            return (1 - x if k & 4 else x, 1 - y if k & 2 else y, 1 - c if k & 1 else c)

        def copy(a, k):
            return pltpu.make_async_remote_copy(
                src_ref=ins[a].at[_slot(peer(k))], dst_ref=outs[a].at[_slot(me)],
                send_sem=send_sems.at[a, k - 1], recv_sem=recv_sems.at[a, k - 1],
                device_id=peer(k), device_id_type=MESH)

        def landing(a, k):
            return pltpu.make_async_remote_copy(
                src_ref=outs[a].at[_slot(peer(k))], dst_ref=outs[a].at[_slot(peer(k))],
                send_sem=send_sems.at[a, k - 1], recv_sem=recv_sems.at[a, k - 1],
                device_id=me, device_id_type=MESH)

        mine = [pltpu.make_async_copy(ins[a].at[_slot(me)], outs[a].at[_slot(me)], local_sems.at[a]) for a in range(n)]
        for cp in mine:
            cp.start()
        sends = [copy(a, k) for a in range(n) for k in range(1, NDEV)]
        for cp in sends:
            cp.start()
        for a in range(n):
            for k in range(1, NDEV):
                landing(a, k).wait_recv()
        for cp in sends:
            cp.wait_send()
        for cp in mine:
            cp.wait()

    return pl.pallas_call(
        body, name=name,
        in_specs=[ANY] * n, out_specs=[ANY] * n,
        out_shape=[jax.ShapeDtypeStruct(a.shape, a.dtype) for a in arrs],
        scratch_shapes=[pltpu.SemaphoreType.DMA((n, NDEV - 1)), pltpu.SemaphoreType.DMA((n, NDEV - 1)),
                        pltpu.SemaphoreType.DMA((n,))],
    )(*arrs)


FFN_TS = 256


def ffn_fwd(x, vec, wgu_g, wdown_g, tag, ride=None):
    S = x.shape[0]
    ts = min(FFN_TS, S)

    def body(x_ref, vec_ref, wgu_hbm, wd_hbm, xo_ref, h_ref, gu_ref, f_ref, wgu_v, wd_v, sems):
        _ffn_weight_fetch(wgu_hbm, wd_hbm, wgu_v, wd_v, sems)
        xv = x_ref[...]
        h = _norm_mod(xv, vec_ref[3:4, :], vec_ref[1:2, :], vec_ref[0:1, :]).astype(BF16)
        h_ref[...] = h
        acc = jnp.zeros((ts, D), F32)
        for j in range(NCHUNK):
            g = _dot(h, wgu_v[j])
            u = _dot(h, wgu_v[NCHUNK + j])
            gu_ref[j] = g.astype(BF16)
            gu_ref[NCHUNK + j] = u.astype(BF16)
            a = (g * _sigmoid(g) * u).astype(BF16)
            acc = acc + _dot(a, wd_v[pl.ds(j * FC, FC), :])
        f_ref[...] = acc.astype(BF16)
        xo_ref[...] = xv + (0.5 * vec_ref[2:3, :]) * acc

    return _call(
        body, ride, name=f"ffn_fwd_{tag}",
        grid=(S // ts,),
        in_specs=[pl.BlockSpec((ts, D), lambda i: (i, 0)),
                  pl.BlockSpec((8, D), lambda i: (0, 0)), ANY, ANY],
        out_specs=[pl.BlockSpec((ts, D), lambda i: (i, 0)),
                   pl.BlockSpec((ts, D), lambda i: (i, 0)),
                   pl.BlockSpec((NDEV, ts, FC), lambda i: (0, i, 0)),
                   pl.BlockSpec((ts, D), lambda i: (i, 0))],
        out_shape=[jax.ShapeDtypeStruct((S, D), F32), jax.ShapeDtypeStruct((S, D), BF16),
                   jax.ShapeDtypeStruct((NDEV, S, FC), BF16), jax.ShapeDtypeStruct((S, D), BF16)],
        scratch_shapes=[pltpu.VMEM((NDEV, D, FC), BF16), pltpu.VMEM((DFF, D), BF16),
                        pltpu.SemaphoreType.DMA((2, NDEV))],
        args=(x, vec, wgu_g, wdown_g))


def ffn_bwd(dxo, x, gu, f, vec, wgu_g, wdown_g, tag):
    S = x.shape[0]
    ts = min(FFN_TS, S)

    def body(dxo_ref, x_ref, gu_ref, f_ref, vec_ref, wgu_hbm, wd_hbm,
             dx_ref, dgu_ref, a_ref, df_ref, acc_ref, wgu_v, wd_v, sems):
        _ffn_weight_fetch(wgu_hbm, wd_hbm, wgu_v, wd_v, sems)

        @pl.when(pl.program_id(0) == 0)
        def _():
            acc_ref[...] = jnp.zeros_like(acc_ref)

        dxo_v = dxo_ref[...]
        dgate = 0.5 * _csum(dxo_v * f_ref[...].astype(F32))
        df = ((0.5 * vec_ref[2:3, :]) * dxo_v).astype(BF16)
        df_ref[...] = df
        dh = jnp.zeros((ts, D), F32)
        for j in range(NCHUNK):
            da = _dot_nt(df, wd_v[pl.ds(j * FC, FC), :])
            g = gu_ref[j].astype(F32)
            u = gu_ref[NCHUNK + j].astype(F32)
            sg = _sigmoid(g)
            si = g * sg
            a_ref[j] = (si * u).astype(BF16)
            dg = (da * u * (sg * (1.0 + g * (1.0 - sg)))).astype(BF16)
            du = (da * si).astype(BF16)
            dgu_ref[j] = dg
            dgu_ref[NCHUNK + j] = du
            dh = dh + _dot_nt(dg, wgu_v[j]) + _dot_nt(du, wgu_v[NCHUNK + j])
        dx, dshift, dscale, dgain = _norm_mod_bwd(dh, x_ref[...], vec_ref[3:4, :], vec_ref[1:2, :])
        dx_ref[...] = dx + dxo_v
        acc_ref[0:1, :] += dshift
        acc_ref[1:2, :] += dscale
        acc_ref[2:3, :] += dgate
        acc_ref[3:4, :] += dgain

    row = pl.BlockSpec((ts, D), lambda i: (i, 0))
    return pl.pallas_call(
        body, name=f"ffn_bwd_{tag}",
        grid=(S // ts,),
        in_specs=[row, row, pl.BlockSpec((NDEV, ts, FC), lambda i: (0, i, 0)), row,
                  pl.BlockSpec((8, D), lambda i: (0, 0)), ANY, ANY],
        out_specs=[row, pl.BlockSpec((NDEV, ts, FC), lambda i: (0, i, 0)),
                   pl.BlockSpec((NCHUNK, ts, FC), lambda i: (0, i, 0)), row,
                   pl.BlockSpec((8, D), lambda i: (0, 0))],
        out_shape=[jax.ShapeDtypeStruct((S, D), F32), jax.ShapeDtypeStruct((NDEV, S, FC), BF16),
                   jax.ShapeDtypeStruct((NCHUNK, S, FC), BF16), jax.ShapeDtypeStruct((S, D), BF16),
                   jax.ShapeDtypeStruct((8, D), F32)],
        scratch_shapes=[pltpu.VMEM((NDEV, D, FC), BF16), pltpu.VMEM((DFF, D), BF16),
                        pltpu.SemaphoreType.DMA((2, NDEV))],
        compiler_params=_cparams("arbitrary"),
    )(dxo, x, gu, f, vec, wgu_g, wdown_g)


NCHIP = NDEV // 2


def tn_matmul_scatter(me_arr, a, b, slot, nslots, prev, name, split=1):
    na, S, M = a.shape
    nb, _, N = b.shape
    ncall = NDEV // split
    ts = min(4096, S)
    nsteps = S // ts
    mp = M // split
    other_step = {1: lambda j: 2 * j, 2: lambda j: j, 8: lambda j: 0}[split]
    mine_step = {1: lambda j: 2 * j + 1, 2: lambda j: j, 8: lambda j: 0}[split]

    def group(k, me_ref):
        if split == 1:
            return jnp.bitwise_xor(me_ref[0], NDEV - 1 - k)
        if split == 2:
            return jnp.bitwise_xor(me_ref[0] // 2, NCHIP - 1 - k)
        return 0

    def body(me_ref, *refs):
        a_ref, b_ref = refs[0], refs[1]
        recv_ref, acc, sb_other, sb_mine, land, d2d_send, d2d_recv, ici_send, ici_recv = refs[-9:]
        k = pl.program_id(0)
        s = pl.program_id(1)
        x, y, c = _place()
        my_chip = 2 * x + y

        def chip_of(j):
            if split == 8:
                cx, cy = j // 2, j % 2
            else:
                flip = NCHIP - 1 - j
                cx, cy = (1 - x if flip & 2 else x), (1 - y if flip & 1 else y)
            return cx, cy, 2 * cx + cy

        def piece(j, core):
            if split == 1:
                return acc[...]
            start = core * mp if split == 2 else (2 * j + core) * mp
            return acc[pl.ds(pl.multiple_of(start, 8), mp), :]

        def to_sibling(j):
            return pltpu.make_async_remote_copy(
                src_ref=sb_other.at[j], dst_ref=land.at[j], send_sem=d2d_send.at[j], recv_sem=d2d_recv.at[j],
                device_id=(x, y, 1 - c), device_id_type=MESH)

        def to_owner(j):
            cx, cy, ci = chip_of(j)
            dst = recv_ref.at[my_chip, slot]
            return ci, pltpu.make_async_copy(sb_mine.at[j], dst, ici_send.at[j]), pltpu.make_async_remote_copy(
                src_ref=sb_mine.at[j], dst_ref=dst, send_sem=ici_send.at[j], recv_sem=ici_recv.at[my_chip],
                device_id=(cx, cy, c), device_id_type=MESH)

        if nsteps == 1:
            acc[...] = _dot_tn(a_ref[...], b_ref[...])
        else:
            @pl.when(s == 0)
            def _():
                acc[...] = jnp.zeros_like(acc)

            acc[...] += _dot_tn(a_ref[...], b_ref[...])

        for kk in range(ncall):
            @pl.when((s == nsteps - 1) & (k == kk))
            def _():
                for j in range(NCHIP):
                    if other_step(j) == kk:
                        sb_other[j] = piece(j, 1 - c).astype(BF16)
                        to_sibling(j).start()
                for j in range(NCHIP):
                    if mine_step(j) == kk:
                        to_sibling(j).wait_recv()
                        sb_mine[j] = (piece(j, c) + land[j].astype(F32)).astype(BF16)
                        ci, loc, rem = to_owner(j)
                        pl.when(ci == my_chip)(loc.start)
                        pl.when(ci != my_chip)(rem.start)

        @pl.when((s == nsteps - 1) & (k == ncall - 1))
        def _():
            for j in range(NCHIP):
                to_sibling(j).wait_send()
                ci, loc, rem = to_owner(j)
                pl.when(ci == my_chip)(loc.wait)
                pl.when(ci != my_chip)(rem.wait_send)
            for src in range(NCHIP):
                @pl.when(my_chip != src)
                def _():
                    pltpu.make_async_remote_copy(
                        src_ref=recv_ref.at[src, slot], dst_ref=recv_ref.at[src, slot],
                        send_sem=ici_send.at[src], recv_sem=ici_recv.at[src],
                        device_id=(src // 2, src % 2, c), device_id_type=MESH).wait_recv()

    in_specs = [pl.BlockSpec((None, ts, M), (lambda k, s, me: (group(k, me), s, 0)) if na > 1 else (lambda k, s, me: (0, s, 0))),
                pl.BlockSpec((None, ts, N), (lambda k, s, me: (group(k, me), s, 0)) if nb > 1 else (lambda k, s, me: (0, s, 0)))]
    args = [me_arr, a, b]
    aliases = {}
    if prev is not None:
        in_specs.append(ANY)
        args.append(prev)
        aliases = {3: 0}
    return pl.pallas_call(
        body, name=name,
        grid_spec=pltpu.PrefetchScalarGridSpec(
            num_scalar_prefetch=1, grid=(ncall, nsteps), in_specs=in_specs, out_specs=ANY,
            scratch_shapes=[pltpu.VMEM((M, N), F32), pltpu.VMEM((NCHIP, mp, N), BF16), pltpu.VMEM((NCHIP, mp, N), BF16),
                            pltpu.VMEM((NCHIP, mp, N), BF16), pltpu.SemaphoreType.DMA((NCHIP,)),
                            pltpu.SemaphoreType.DMA((NCHIP,)), pltpu.SemaphoreType.DMA((NCHIP,)),
                            pltpu.SemaphoreType.DMA((NCHIP,))]),
        out_shape=jax.ShapeDtypeStruct((NCHIP, nslots, mp, N), BF16),
        input_output_aliases=aliases,
        compiler_params=_cparams("arbitrary", "arbitrary"),
    )(*args)


MIX_TS = 256


def mix_in_fwd(x, vec, win_g, tag, ride=None):
    S = x.shape[0]
    ts = min(MIX_TS, S)

    def body(x_ref, vec_ref, win_ref, hm_ref, proj_ref):
        h = _norm_mod(x_ref[...], vec_ref[3:4, :], vec_ref[1:2, :], vec_ref[0:1, :]).astype(BF16)
        hm_ref[...] = h
        for k in range(NDEV):
            proj_ref[k] = _dot(h, win_ref[k])

    return _call(
        body, ride, name=f"mix_in_fwd_{tag}",
        grid=(S // ts,),
        in_specs=[pl.BlockSpec((ts, D), lambda i: (i, 0)), pl.BlockSpec((8, D), lambda i: (0, 0)),
                  pl.BlockSpec((NDEV, D, PC), lambda i: (0, 0, 0))],
        out_specs=[pl.BlockSpec((ts, D), lambda i: (i, 0)),
                   pl.BlockSpec((NDEV, ts, PC), lambda i: (0, i, 0))],
        out_shape=[jax.ShapeDtypeStruct((S, D), BF16), jax.ShapeDtypeStruct((NDEV, S, PC), F32)],
        scratch_shapes=[], args=(x, vec, win_g))


def mix_in_bwd(dproj, x, dxo, vec, win_g, tag):
    S = x.shape[0]
    ts = min(MIX_TS, S)

    def body(dp_ref, x_ref, dxo_ref, vec_ref, win_ref, dx_ref, acc_ref):
        @pl.when(pl.program_id(0) == 0)
        def _():
            acc_ref[...] = jnp.zeros_like(acc_ref)

        dh = jnp.zeros((ts, D), F32)
        for k in range(NDEV):
            dh = dh + _dot_nt(dp_ref[k], win_ref[k])
        dx, dshift, dscale, dgain = _norm_mod_bwd(dh, x_ref[...], vec_ref[3:4, :], vec_ref[1:2, :])
        dx_ref[...] = dx + dxo_ref[...]
        acc_ref[0:1, :] += dshift
        acc_ref[1:2, :] += dscale
        acc_ref[3:4, :] += dgain

    row = pl.BlockSpec((ts, D), lambda i: (i, 0))
    return pl.pallas_call(
        body, name=f"mix_in_bwd_{tag}",
        grid=(S // ts,),
        in_specs=[pl.BlockSpec((NDEV, ts, PC), lambda i: (0, i, 0)), row, row,
                  pl.BlockSpec((8, D), lambda i: (0, 0)),
                  pl.BlockSpec((NDEV, D, PC), lambda i: (0, 0, 0))],
        out_specs=[row, pl.BlockSpec((8, D), lambda i: (0, 0))],
        out_shape=[jax.ShapeDtypeStruct((S, D), F32), jax.ShapeDtypeStruct((8, D), F32)],
        compiler_params=_cparams("arbitrary"),
    )(dproj, x, dxo, vec, win_g)


SCAN_UNROLL = 4


def _shift_down(z, k, row):
    return jnp.where(row >= k, pltpu.roll(z, k, 0), 0.0)


def _shift_up(z, k, row, n):
    return jnp.where(row < n - k, pltpu.roll(z, n - k, 0), 0.0)


def _lru_gates(xc, lp_ref, wa_ref, wx_ref):
    xcb = xc.astype(BF16)
    ra = _sigmoid(_dot(xcb, wa_ref[...]) + lp_ref[5:6, :])
    ix = _sigmoid(_dot(xcb, wx_ref[...]) + lp_ref[6:7, :])
    lam = lp_ref[7:8, :]
    ls = jnp.minimum(lam, 0.0) - jnp.log(1.0 + jnp.exp(-jnp.abs(lam)))
    log_a = (RG_LRU_C * ls) * ra
    a = jnp.exp(log_a)
    mult = jnp.sqrt(-jnp.tanh(log_a) * (a * a + 1.0))
    return ra, ix, ls, a, mult


def _conv(x, lp_ref, row):
    return (lp_ref[4:5, :] + lp_ref[3:4, :] * x + lp_ref[2:3, :] * _shift_down(x, 1, row)
            + lp_ref[1:2, :] * _shift_down(x, 2, row) + lp_ref[0:1, :] * _shift_down(x, 3, row))


def lru_fwd(proj, lp, wa_t, wx_t, tag, ride=None):
    S = proj.shape[1]
    nblk = S // 8

    def body(x_ref, g_ref, lp_ref, wa_ref, wx_ref, y_ref, xc_ref, h_ref, a_s, b_s):
        x = x_ref[...]
        row = lax.broadcasted_iota(jnp.int32, x.shape, 0)
        xc = _conv(x, lp_ref, row)
        xc_ref[...] = xc
        ra, ix, ls, a, mult = _lru_gates(xc, lp_ref, wa_ref, wx_ref)
        a_s[...] = a
        b_s[...] = mult * (ix * xc)
        rowb = lax.broadcasted_iota(jnp.int32, (8, LC), 0)

        def step(i, carry):
            for q in range(SCAN_UNROLL):
                r0 = pl.multiple_of((i * SCAN_UNROLL + q) * 8, 8)
                A = a_s[pl.ds(r0, 8), :]
                B = b_s[pl.ds(r0, 8), :]
                for d in (1, 2, 4):
                    m = rowb >= d
                    As = jnp.where(m, pltpu.roll(A, d, 0), 1.0)
                    Bs = jnp.where(m, pltpu.roll(B, d, 0), 0.0)
                    B = A * Bs + B
                    A = A * As
                H = B + A * carry
                h_ref[pl.ds(r0, 8), :] = H
                carry = H[7:8, :]
            return carry

        lax.fori_loop(0, nblk // SCAN_UNROLL, step, jnp.zeros((1, LC), F32))
        y_ref[...] = h_ref[...] * _gelu(g_ref[...])

    col = pl.BlockSpec((S, LC), lambda c: (0, c))
    return _call(
        body, ride, name=f"lru_fwd_{tag}",
        grid=(LW // LC,),
        in_specs=[pl.BlockSpec((None, S, LC), lambda c: (c // 2, 0, c % 2)),
                  pl.BlockSpec((None, S, LC), lambda c: (2 + c // 2, 0, c % 2)),
                  pl.BlockSpec((8, LC), lambda c: (0, c)),
                  pl.BlockSpec((None, LC, LC), lambda c: (c, 0, 0)),
                  pl.BlockSpec((None, LC, LC), lambda c: (c, 0, 0))],
        out_specs=[col, col, col],
        out_shape=[jax.ShapeDtypeStruct((S, LW), F32)] * 3,
        scratch_shapes=[pltpu.VMEM((S, LC), F32), pltpu.VMEM((S, LC), F32)],
        args=(proj, proj, lp, wa_t, wx_t))


def lru_bwd(dy, proj, xc_all, hst, lp, wa_t, wx_t, tag):
    S = proj.shape[1]
    nblk = S // 8

    def body(dy_ref, x_ref, g_ref, xc_ref, h_ref, lp_ref, wa_ref, wx_ref,
             dx_ref, dg_ref, dlp_ref, dwa_ref, dwx_ref, c_s, l_s):
        xc = xc_ref[...]
        row = lax.broadcasted_iota(jnp.int32, xc.shape, 0)
        ra, ix, ls, a, mult = _lru_gates(xc, lp_ref, wa_ref, wx_ref)
        g = g_ref[...]
        dyv = dy_ref[...]
        h = h_ref[...]
        dg_ref[...] = (dyv * h * _gelu_grad(g)).astype(BF16)
        c_s[...] = _shift_up(a, 1, row, S)
        l_s[...] = dyv * _gelu(g)
        rowb = lax.broadcasted_iota(jnp.int32, (8, LC), 0)

        def step(i, carry):
            for q in range(SCAN_UNROLL):
                r0 = pl.multiple_of((nblk - 1 - (i * SCAN_UNROLL + q)) * 8, 8)
                C = c_s[pl.ds(r0, 8), :]
                L = l_s[pl.ds(r0, 8), :]
                for d in (1, 2, 4):
                    m = rowb < 8 - d
                    Cs = jnp.where(m, pltpu.roll(C, 8 - d, 0), 1.0)
                    Ls = jnp.where(m, pltpu.roll(L, 8 - d, 0), 0.0)
                    L = C * Ls + L
                    C = C * Cs
                L = L + C * carry
                l_s[pl.ds(r0, 8), :] = L
                carry = L[0:1, :]
            return carry

        lax.fori_loop(0, nblk // SCAN_UNROLL, step, jnp.zeros((1, LC), F32))
        db = l_s[...]
        da = db * _shift_down(h, 1, row)
        ixc = ix * xc
        dmult = db * ixc
        dix = db * (mult * xc)
        dxc = db * (mult * ix)
        dlog_a = da * a - dmult * (a * a) / mult
        dra = dlog_a * (RG_LRU_C * ls)
        dls = _csum(dlog_a * ra) * RG_LRU_C
        lam = lp_ref[7:8, :]
        dlam = dls * _sigmoid(-lam)
        dpa = dra * ra * (1.0 - ra)
        dpx = dix * ix * (1.0 - ix)
        dpab = dpa.astype(BF16)
        dpxb = dpx.astype(BF16)
        xcb = xc.astype(BF16)
        dwa_ref[...] = _dot_tn(xcb, dpab)
        dwx_ref[...] = _dot_tn(xcb, dpxb)
        dxc = dxc + _dot_nt(dpab, wa_ref[...]) + _dot_nt(dpxb, wx_ref[...])
        x = x_ref[...]
        dlp_ref[0:1, :] = _csum(dxc * _shift_down(x, 3, row))
        dlp_ref[1:2, :] = _csum(dxc * _shift_down(x, 2, row))
        dlp_ref[2:3, :] = _csum(dxc * _shift_down(x, 1, row))
        dlp_ref[3:4, :] = _csum(dxc * x)
        dlp_ref[4:5, :] = _csum(dxc)
        dlp_ref[5:6, :] = _csum(dpa)
        dlp_ref[6:7, :] = _csum(dpx)
        dlp_ref[7:8, :] = dlam
        dx = (lp_ref[3:4, :] * dxc + lp_ref[2:3, :] * _shift_up(dxc, 1, row, S)
              + lp_ref[1:2, :] * _shift_up(dxc, 2, row, S) + lp_ref[0:1, :] * _shift_up(dxc, 3, row, S))
        dx_ref[...] = dx.astype(BF16)

    col = pl.BlockSpec((S, LC), lambda c: (0, c))
    pcol = pl.BlockSpec((None, S, LC), lambda c: (c // 2, 0, c % 2))
    return pl.pallas_call(
        body, name=f"lru_bwd_{tag}",
        grid=(LW // LC,),
        in_specs=[col, pcol, pl.BlockSpec((None, S, LC), lambda c: (2 + c // 2, 0, c % 2)), col, col,
                  pl.BlockSpec((8, LC), lambda c: (0, c)),
                  pl.BlockSpec((None, LC, LC), lambda c: (c, 0, 0)),
                  pl.BlockSpec((None, LC, LC), lambda c: (c, 0, 0))],
        out_specs=[pcol, pcol, pl.BlockSpec((8, LC), lambda c: (0, c)),
                   pl.BlockSpec((None, LC, LC), lambda c: (c, 0, 0)),
                   pl.BlockSpec((None, LC, LC), lambda c: (c, 0, 0))],
        out_shape=[jax.ShapeDtypeStruct((2, S, PC), BF16), jax.ShapeDtypeStruct((2, S, PC), BF16),
                   jax.ShapeDtypeStruct((8, LW), F32),
                   jax.ShapeDtypeStruct((LW // LC, LC, LC), F32), jax.ShapeDtypeStruct((LW // LC, LC, LC), F32)],
        scratch_shapes=[pltpu.VMEM((S, LC), F32), pltpu.VMEM((S, LC), F32)],
        compiler_params=_cparams("arbitrary"),
    )(dy, proj, proj, xc_all, hst, lp, wa_t, wx_t)


def _head_stack(zc, lane_head):
    return jnp.concatenate([jnp.where(lane_head == hh, zc, 0.0) for hh in range(HEADS)], axis=0).astype(BF16)


def _gmlp_fwd_parts(u, v, gp_ref, wcat_ref, bz_ref, pavg_ref, ts):
    ug = _gelu(u)
    vg = _gelu(v)
    pavg = pavg_ref[...]
    vc = vg - _seg_mean(vg, pavg)
    rs = lax.rsqrt(_seg_mean(vc * vc, pavg) + EPS)
    vhat = vc * rs
    vh = vhat * gp_ref[0:1, :]
    lane_head = lax.broadcasted_iota(jnp.int32, (CHUNK, GW), 1) // HD
    zs = []
    for n in range(ts // CHUNK):
        stack = _head_stack(vh[n * CHUNK:(n + 1) * CHUNK, :], lane_head)
        zs.append(_dot(wcat_ref[...], stack) + bz_ref[...])
    z = jnp.concatenate(zs, axis=0) if len(zs) > 1 else zs[0]
    return ug, rs, vhat, vh, z


def mix_out_fwd(proj, ylru, x, vec, gp, wcat, bz, pavg, wout_g, tag, ride=None):
    S = x.shape[0]
    ts = min(MIX_TS, S)

    def body(u_ref, v_ref, yl_ref, x_ref, vec_ref, gp_ref, wcat_ref, bz_ref, pavg_ref, wout_ref,
             xo_ref, y_ref, fo_ref):
        u = jnp.concatenate([u_ref[0], u_ref[1]], axis=1)
        v = jnp.concatenate([v_ref[0], v_ref[1]], axis=1)
        ug, _, _, _, z = _gmlp_fwd_parts(u, v, gp_ref, wcat_ref, bz_ref, pavg_ref, ts)
        n1 = _rms(yl_ref[...], gp_ref[1:2, :])
        n2 = _rms(ug * z, gp_ref[2:3, :])
        y = jnp.concatenate([n1, n2], axis=1).astype(BF16)
        y_ref[...] = y
        fo = jnp.zeros((ts, D), F32)
        for k in range(NDEV):
            fo = fo + _dot(y[:, k * OR:(k + 1) * OR], wout_ref[k])
        fo_ref[...] = fo.astype(BF16)
        xo_ref[...] = x_ref[...] + vec_ref[2:3, :] * fo

    row = pl.BlockSpec((ts, D), lambda i: (i, 0))
    full = lambda shp: pl.BlockSpec(shp, lambda i: tuple(0 for _ in shp))
    return _call(
        body, ride, name=f"mix_out_fwd_{tag}",
        grid=(S // ts,),
        in_specs=[pl.BlockSpec((2, ts, PC), lambda i: (2, i, 0)), pl.BlockSpec((2, ts, PC), lambda i: (3, i, 0)),
                  pl.BlockSpec((ts, LW), lambda i: (i, 0)), row, full((8, D)), full((8, GW)),
                  full((CHUNK, HEADS * CHUNK)), full((CHUNK, GW)), full((GW, GW)),
                  pl.BlockSpec((NDEV, OR, D), lambda i: (0, 0, 0))],
        out_specs=[row, row, row],
        out_shape=[jax.ShapeDtypeStruct((S, D), F32), jax.ShapeDtypeStruct((S, D), BF16),
                   jax.ShapeDtypeStruct((S, D), BF16)],
        scratch_shapes=[], args=(proj, proj, ylru, x, vec, gp, wcat, bz, pavg, wout_g))


def mix_out_bwd(dxo, proj, ylru, fo, vec, gp, wcat, wcat_t, bz, pavg, wout_g, tag):
    S = dxo.shape[0]
    ts = min(MIX_TS, S)

    def body(dxo_ref, u_ref, v_ref, yl_ref, fo_ref, vec_ref, gp_ref, wcat_ref, wcatt_ref, bz_ref, pavg_ref,
             wout_ref, dyo_ref, dyl_ref, duv_ref, acc_ref, dgp_ref, dwm_ref, dbz_ref):
        @pl.when(pl.program_id(0) == 0)
        def _():
            acc_ref[...] = jnp.zeros_like(acc_ref)
            dgp_ref[...] = jnp.zeros_like(dgp_ref)
            dwm_ref[...] = jnp.zeros_like(dwm_ref)
            dbz_ref[...] = jnp.zeros_like(dbz_ref)

        dxo_v = dxo_ref[...]
        acc_ref[2:3, :] += _csum(dxo_v * fo_ref[...].astype(F32))
        dyo = (vec_ref[2:3, :] * dxo_v).astype(BF16)
        dyo_ref[...] = dyo
        dn = [_dot_nt(dyo, wout_ref[k]) for k in range(NDEV)]
        dn1 = jnp.concatenate(dn[:NDEV // 2], axis=1)
        dn2 = jnp.concatenate(dn[NDEV // 2:], axis=1)
        dyl, dg1 = _rms_bwd(dn1, yl_ref[...], gp_ref[1:2, :])
        dyl_ref[...] = dyl
        u = jnp.concatenate([u_ref[0], u_ref[1]], axis=1)
        v = jnp.concatenate([v_ref[0], v_ref[1]], axis=1)
        ug, rs, vhat, vh, z = _gmlp_fwd_parts(u, v, gp_ref, wcat_ref, bz_ref, pavg_ref, ts)
        dyg, dg2 = _rms_bwd(dn2, ug * z, gp_ref[2:3, :])
        du = (dyg * z) * _gelu_grad(u)
        dz = dyg * ug
        lane_head = lax.broadcasted_iota(jnp.int32, (CHUNK, GW), 1) // HD
        vhb = vh.astype(BF16)
        dvhs = []
        dbz = jnp.zeros((CHUNK, GW), F32)
        dwm = jnp.zeros((HEADS * CHUNK, CHUNK), F32)
        for n in range(ts // CHUNK):
            dzc = dz[n * CHUNK:(n + 1) * CHUNK, :]
            dbz = dbz + dzc
            stack = _head_stack(dzc, lane_head)
            dwm = dwm + _dot_nt(stack, vhb[n * CHUNK:(n + 1) * CHUNK, :])
            dvhs.append(_dot(wcatt_ref[...], stack))
        dbz_ref[...] += dbz
        dwm_ref[...] += dwm
        dvh = jnp.concatenate(dvhs, axis=0) if len(dvhs) > 1 else dvhs[0]
        pavg = pavg_ref[...]
        dvn = _csum(dvh * vhat)
        dvhat = dvh * gp_ref[0:1, :]
        dvg = rs * (dvhat - _seg_mean(dvhat, pavg) - vhat * _seg_mean(dvhat * vhat, pavg))
        dv = dvg * _gelu_grad(v)
        duv_ref[0] = du[:, :PC].astype(BF16)
        duv_ref[1] = du[:, PC:].astype(BF16)
        duv_ref[2] = dv[:, :PC].astype(BF16)
        duv_ref[3] = dv[:, PC:].astype(BF16)
        dgp_ref[0:1, :] += dvn
        dgp_ref[1:2, :] += dg1
        dgp_ref[2:3, :] += dg2

    row = pl.BlockSpec((ts, D), lambda i: (i, 0))
    full = lambda shp: pl.BlockSpec(shp, lambda i: tuple(0 for _ in shp))
    return pl.pallas_call(
        body, name=f"mix_out_bwd_{tag}",
        grid=(S // ts,),
        in_specs=[row, pl.BlockSpec((2, ts, PC), lambda i: (2, i, 0)), pl.BlockSpec((2, ts, PC), lambda i: (3, i, 0)),
                  pl.BlockSpec((ts, LW), lambda i: (i, 0)), row, full((8, D)), full((8, GW)),
                  full((CHUNK, HEADS * CHUNK)), full((CHUNK, HEADS * CHUNK)), full((CHUNK, GW)), full((GW, GW)),
                  pl.BlockSpec((NDEV, OR, D), lambda i: (0, 0, 0))],
        out_specs=[row, pl.BlockSpec((ts, LW), lambda i: (i, 0)), pl.BlockSpec((4, ts, PC), lambda i: (0, i, 0)),
                   full((8, D)), full((8, GW)), full((HEADS * CHUNK, CHUNK)), full((CHUNK, GW))],
        out_shape=[jax.ShapeDtypeStruct((S, D), BF16), jax.ShapeDtypeStruct((S, LW), F32),
                   jax.ShapeDtypeStruct((4, S, PC), BF16), jax.ShapeDtypeStruct((8, D), F32),
                   jax.ShapeDtypeStruct((8, GW), F32), jax.ShapeDtypeStruct((HEADS * CHUNK, CHUNK), F32),
                   jax.ShapeDtypeStruct((CHUNK, GW), F32)],
        compiler_params=_cparams("arbitrary"),
    )(dxo, proj, proj, ylru, fo, vec, gp, wcat, wcat_t, bz, pavg, wout_g)


def final_loss(x, target, gain):
    S = x.shape[0]
    ts = min(512, S)

    def body(x_ref, t_ref, g_ref, loss_ref, dx_ref, dg_ref):
        @pl.when(pl.program_id(0) == 0)
        def _():
            loss_ref[...] = jnp.zeros_like(loss_ref)
            dg_ref[...] = jnp.zeros_like(dg_ref)

        xv = x_ref[...]
        gain_v = g_ref[0:1, :]
        rstd = lax.rsqrt(_rmean(xv * xv) + EPS)
        xhat = xv * rstd
        err = xhat * gain_v - t_ref[...]
        loss_ref[...] += 0.5 * _csum(_rmean(err * err))
        dy = err * (1.0 / D)
        dg_ref[0:1, :] += _csum(dy * xhat)
        dxhat = dy * gain_v
        dx_ref[...] = rstd * (dxhat - xhat * _rmean(dxhat * xhat))

    row = pl.BlockSpec((ts, D), lambda i: (i, 0))
    return pl.pallas_call(
        body, name="final_loss",
        grid=(S // ts,),
        in_specs=[row, row, pl.BlockSpec((8, D), lambda i: (0, 0))],
        out_specs=[pl.BlockSpec((8, 128), lambda i: (0, 0)), row, pl.BlockSpec((8, D), lambda i: (0, 0))],
        out_shape=[jax.ShapeDtypeStruct((8, 128), F32), jax.ShapeDtypeStruct((S, D), F32),
                   jax.ShapeDtypeStruct((8, D), F32)],
        compiler_params=_cparams("arbitrary"),
    )(x, target, gain)


def _vec(mod_l, j, gain):
    return jnp.concatenate([mod_l[3 * j:3 * j + 3], gain[None, :], jnp.zeros((4, D), F32)], axis=0)


def _block_diag_tiles(w):
    w4 = w.reshape(LW // LC, 2, HD, HD)
    eye2 = jnp.eye(2, dtype=w.dtype)
    return (w4[:, :, :, None, :] * eye2[None, :, None, :, None]).reshape(LW // LC, LC, LC).astype(BF16)


def _block_diag_extract(dw):
    d5 = dw.reshape(LW // LC, 2, HD, 2, HD)
    return jnp.einsum('cihkj,ik->cihj', d5, jnp.eye(2, dtype=dw.dtype)).reshape(HEADS, HD, HD)


def _layer_params(l, p, conv_w_full):
    lp = jnp.concatenate([conv_w_full[l], p['conv_b'][l][None], p['gate_a_b'][l].reshape(1, LW),
                          p['gate_x_b'][l].reshape(1, LW), p['lru_lambda'][l][None]], axis=0)
    gp = jnp.concatenate([p['v_norm'][l][None], p['lru_out_norm'][l][None], p['gmlp_out_norm'][l][None],
                          jnp.zeros((5, GW), F32)], axis=0)
    ws = p['spatial_w'][l] * jnp.tril(jnp.ones((CHUNK, CHUNK), F32))
    wcat = ws.transpose(1, 0, 2).reshape(CHUNK, HEADS * CHUNK).astype(BF16)
    wcat_t = ws.transpose(2, 0, 1).reshape(CHUNK, HEADS * CHUNK).astype(BF16)
    bz = jnp.repeat(p['spatial_b'][l].T, HD, axis=1)
    return dict(lp=lp, gp=gp, wcat=wcat, wcat_t=wcat_t, bz=bz,
                wa_t=_block_diag_tiles(p['gate_a_w'][l]), wx_t=_block_diag_tiles(p['gate_x_w'][l]))


def _pavg():
    return jnp.kron(jnp.eye(HEADS, dtype=F32), jnp.full((HD, HD), 1.0 / HD, F32)).astype(BF16)


GATHER_RIDES = {
    ('ffn_a', 0): [('w_in', 0), ('gu', DEPTH)],
    ('mix_in', 0): [('w_out', 0)],
    ('lru', 0): [('down', DEPTH)],
    ('mix_out', 0): [('down', 1)],
    ('ffn_b', 0): [('gu', 1), ('w_in', 1)],
    ('ffn_a', 1): [('gu', DEPTH + 1), ('w_out', 1)],
    ('mix_in', 1): [('down', DEPTH + 1)],
}


def local_fwd_bwd(me_arr, x, target, mod, p, loc, gathered, conv_w_full):
    pavg = _pavg()
    g = dict(gathered)

    def ride(call, l):
        todo = GATHER_RIDES.get((call, l))
        return None if todo is None else (todo, GatherRide([(loc[kind], slot) for kind, slot in todo]))

    def run(fn, call, l, *args):
        r = ride(call, l)
        outs, got = fn(*args, ride=None if r is None else r[1])
        if r is not None:
            g.update(dict(zip(r[0], got)))
        return outs

    saved = []
    h = x
    for l in range(DEPTH):
        q = _layer_params(l, p, conv_w_full)
        v1 = _vec(mod[l], 0, p['ffn1_norm'][l])
        vm = _vec(mod[l], 1, p['mix_norm'][l])
        v2 = _vec(mod[l], 2, p['ffn2_norm'][l])
        x0 = h
        x1, h1, gu1, f1 = run(ffn_fwd, 'ffn_a', l, x0, v1, g['gu', l], g['down', l], f"a{l}")
        hm, proj = run(mix_in_fwd, 'mix_in', l, x1, vm, g['w_in', l], f"{l}")
        ylru, xc, hst = run(lru_fwd, 'lru', l, proj, q['lp'], q['wa_t'], q['wx_t'], f"{l}")
        x2, y, fo = run(mix_out_fwd, 'mix_out', l, proj, ylru, x1, vm, q['gp'], q['wcat'], q['bz'], pavg,
                        g['w_out', l], f"{l}")
        x3, h2, gu2, f2 = run(ffn_fwd, 'ffn_b', l, x2, v2, g['gu', DEPTH + l], g['down', DEPTH + l], f"b{l}")
        saved.append(dict(q=q, v1=v1, vm=vm, v2=v2, x0=x0, x1=x1, x2=x2, h1=h1, gu1=gu1, f1=f1, hm=hm, proj=proj,
                          ylru=ylru, xc=xc, hst=hst, y=y, fo=fo, h2=h2, gu2=gu2, f2=f2))
        h = x3
    fin = jnp.concatenate([p['final_norm'][None], jnp.zeros((7, D), F32)], axis=0)
    loss8, dx, dfin = final_loss(h, target, fin)
    loss = loss8[0, 0]

    big = dict(gu=None, down=None, w_in=None, w_out=None)
    small = {k: [None] * DEPTH for k in ('ffn1_norm', 'mix_norm', 'ffn2_norm', 'conv_w', 'conv_b', 'gate_a_w',
                                         'gate_a_b', 'gate_x_w', 'gate_x_b', 'lru_lambda', 'v_norm', 'spatial_w',
                                         'spatial_b', 'lru_out_norm', 'gmlp_out_norm')}
    dmod = [None] * DEPTH
    tril = jnp.tril(jnp.ones((CHUNK, CHUNK), F32))
    for l in reversed(range(DEPTH)):
        sv = saved[l]
        q = sv['q']
        dx2, dgu, a, df, acc2 = ffn_bwd(dx, sv['x2'], sv['gu2'], sv['f2'], sv['v2'],
                                        g['gu', DEPTH + l], g['down', DEPTH + l], f"b{l}")
        big['gu'] = tn_matmul_scatter(me_arr, dgu, sv['h2'][None], DEPTH + l, 2 * DEPTH, big['gu'], f"dw_gu_b{l}")
        big['down'] = tn_matmul_scatter(me_arr, a, df[None], DEPTH + l, 2 * DEPTH, big['down'], f"dw_down_b{l}", split=2)
        dyo, dylru, duv, accmo, dgp, dwm, dbz = mix_out_bwd(dx2, sv['proj'], sv['ylru'], sv['fo'], sv['vm'], q['gp'],
                                                             q['wcat'], q['wcat_t'], q['bz'], pavg, g['w_out', l], f"{l}")
        big['w_out'] = tn_matmul_scatter(me_arr, sv['y'][None], dyo[None], l, DEPTH, big['w_out'], f"dw_out_{l}",
                                         split=NDEV)
        dxl, dgl, dlp, dwa, dwx = lru_bwd(dylru, sv['proj'], sv['xc'], sv['hst'], q['lp'], q['wa_t'], q['wx_t'], f"{l}")
        dproj = jnp.concatenate([dxl, dgl, duv], axis=0)
        dx1, accmi = mix_in_bwd(dproj, sv['x1'], dx2, sv['vm'], g['w_in', l], f"{l}")
        big['w_in'] = tn_matmul_scatter(me_arr, sv['hm'][None], dproj, l, DEPTH, big['w_in'], f"dw_in_{l}")
        dx0, dgu, a, df, acc1 = ffn_bwd(dx1, sv['x0'], sv['gu1'], sv['f1'], sv['v1'],
                                        g['gu', l], g['down', l], f"a{l}")
        big['gu'] = tn_matmul_scatter(me_arr, dgu, sv['h1'][None], l, 2 * DEPTH, big['gu'], f"dw_gu_a{l}")
        big['down'] = tn_matmul_scatter(me_arr, a, df[None], l, 2 * DEPTH, big['down'], f"dw_down_a{l}", split=2)
        dx = dx0
        dmod[l] = jnp.concatenate([acc1[0:3], accmi[0:2], accmo[2:3], acc2[0:3]], axis=0)
        small['ffn1_norm'][l] = acc1[3]
        small['mix_norm'][l] = accmi[3]
        small['ffn2_norm'][l] = acc2[3]
        small['conv_w'][l] = dlp[0:4]
        small['conv_b'][l] = dlp[4]
        small['gate_a_b'][l] = dlp[5].reshape(HEADS, HD)
        small['gate_x_b'][l] = dlp[6].reshape(HEADS, HD)
        small['lru_lambda'][l] = dlp[7]
        small['gate_a_w'][l] = _block_diag_extract(dwa)
        small['gate_x_w'][l] = _block_diag_extract(dwx)
        small['v_norm'][l] = dgp[0]
        small['lru_out_norm'][l] = dgp[1]
        small['gmlp_out_norm'][l] = dgp[2]
        small['spatial_w'][l] = dwm.reshape(HEADS, CHUNK, CHUNK) * tril
        small['spatial_b'][l] = dbz.reshape(CHUNK, HEADS, HD).sum(-1).T
    small = {k: jnp.stack(v) for k, v in small.items()}
    small['final_norm'] = dfin[0]
    return loss, dx, big, small, jnp.stack(dmod)


def ada_fwd(c_all, w_ada, b_loc):
    def body(c_ref, w_ref, b_ref, mod_ref, sc_ref):
        cv = c_ref[...]
        sc = cv * _sigmoid(cv)
        sc_ref[...] = sc
        mod_ref[...] = _dot3(sc, w_ref[...]) + b_ref[...]

    return pl.pallas_call(
        body, name="ada_fwd",
        grid=(DEPTH,),
        in_specs=[pl.BlockSpec((NDEV, D), lambda l: (0, 0)), pl.BlockSpec((None, D, AC), lambda l: (l, 0, 0)),
                  pl.BlockSpec((None, 1, AC), lambda l: (l, 0, 0))],
        out_specs=[pl.BlockSpec((None, NDEV, AC), lambda l: (l, 0, 0)), pl.BlockSpec((NDEV, D), lambda l: (0, 0))],
        out_shape=[jax.ShapeDtypeStruct((DEPTH, NDEV, AC), F32), jax.ShapeDtypeStruct((NDEV, D), F32)],
        compiler_params=_cparams("arbitrary"),
    )(c_all, w_ada, b_loc)


def ada_bwd(sc_t, dmod_cols):
    def body(sc_ref, dm_ref, g_ref):
        sc = sc_ref[...]
        dm = dm_ref[...]
        acc = sc[:, 0:1] * dm[0:1, :]
        for b in range(1, NDEV):
            acc = acc + sc[:, b:b + 1] * dm[b:b + 1, :]
        g_ref[...] = acc

    return pl.pallas_call(
        body, name="ada_bwd",
        grid=(DEPTH,),
        in_specs=[pl.BlockSpec((D, NDEV), lambda l: (0, 0)), pl.BlockSpec((None, NDEV, AC), lambda l: (l, 0, 0))],
        out_specs=pl.BlockSpec((None, None, D, AC), lambda l: (0, l, 0, 0)),
        out_shape=jax.ShapeDtypeStruct((1, DEPTH, D, AC), F32),
        compiler_params=_cparams("arbitrary"),
    )(sc_t, dmod_cols)


def _row_tile(rows, cols):
    if rows * cols <= 512 * 1024:
        return rows
    for tr in (512, 384, 352, 256, 128, 64, 32, 16, 8):
        if rows % tr == 0:
            return tr
    return rows


def adamw(gparts, slot0, w, m, v, name):
    P, _, R, C = gparts.shape
    L = w.shape[0]
    tr = _row_tile(R, C)

    def body(g_ref, w_ref, m_ref, v_ref, go_ref, do_ref, mo_ref, vo_ref):
        g = g_ref[0].astype(F32)
        for p in range(1, P):
            g = g + g_ref[p].astype(F32)
        go_ref[...] = g
        mn = ADAM_B1 * m_ref[...] + (1.0 - ADAM_B1) * g
        vn = ADAM_B2 * v_ref[...] + (1.0 - ADAM_B2) * (g * g)
        mo_ref[...] = mn
        vo_ref[...] = vn
        m_hat = mn / (1.0 - ADAM_B1 ** ADAM_STEP)
        v_hat = vn / (1.0 - ADAM_B2 ** ADAM_STEP)
        do_ref[...] = -ADAM_LR * (m_hat / (jnp.sqrt(v_hat) + ADAM_EPS) + ADAM_WD * w_ref[...])

    blk = pl.BlockSpec((None, tr, C), lambda l, i: (l, i, 0))
    return pl.pallas_call(
        body, name=name,
        grid=(L, R // tr),
        in_specs=[pl.BlockSpec((P, None, tr, C), lambda l, i: (0, slot0 + l, i, 0)), blk, blk, blk],
        out_specs=[blk, blk, blk, blk],
        out_shape=[jax.ShapeDtypeStruct((L, R, C), F32)] * 4,
        compiler_params=_cparams("arbitrary", "arbitrary"),
    )(gparts, w, m, v)


def sum_parts(parts):
    P, R, C = parts.shape

    def body(p_ref, o_ref):
        acc = p_ref[0]
        for p in range(1, P):
            acc = acc + p_ref[p]
        o_ref[...] = acc

    return pl.pallas_call(
        body, name="sum_parts",
        in_specs=[pl.BlockSpec(memory_space=pltpu.VMEM)],
        out_specs=pl.BlockSpec(memory_space=pltpu.VMEM),
        out_shape=jax.ShapeDtypeStruct((R, C), F32),
    )(parts)


WEIGHTS = ['w_ada', 'b_ada', 'ffn1_norm', 'ffn1_w_gu', 'ffn1_w_down', 'mix_norm', 'w_in', 'conv_w', 'conv_b',
           'gate_a_w', 'gate_a_b', 'gate_x_w', 'gate_x_b', 'lru_lambda', 'v_norm', 'spatial_w', 'spatial_b',
           'lru_out_norm', 'gmlp_out_norm', 'w_out', 'ffn2_norm', 'ffn2_w_gu', 'ffn2_w_down', 'final_norm']
PACKED = ['b_ada', 'ffn1_norm', 'mix_norm', 'conv_b', 'gate_a_w', 'gate_a_b', 'gate_x_w', 'gate_x_b', 'lru_lambda',
          'v_norm', 'spatial_w', 'spatial_b', 'lru_out_norm', 'gmlp_out_norm', 'ffn2_norm', 'final_norm', 'conv_w']
PACK_LANES = 128
PACK_ROW_ALIGN = 8 * NDEV


def _pack(d):
    flat = jnp.concatenate([d[k].reshape(-1).astype(F32) for k in PACKED])
    rows = -(-flat.shape[0] // (PACK_LANES * PACK_ROW_ALIGN)) * PACK_ROW_ALIGN
    flat = jnp.concatenate([flat, jnp.zeros((rows * PACK_LANES - flat.shape[0],), F32)])
    return flat.reshape(rows, PACK_LANES)


def _unpack(buf, shapes):
    flat = buf.reshape(-1)
    out, off = {}, 0
    for k in PACKED:
        size = 1
        for s in shapes[k]:
            size *= s
        out[k] = flat[off:off + size].reshape(shapes[k])
        off += size
    return out


def kernel(x, c, w_ada, b_ada, ffn1_norm, ffn1_w_gu, ffn1_w_down, mix_norm, w_in, conv_w, conv_b, gate_a_w, gate_a_b, gate_x_w, gate_x_b, lru_lambda, v_norm, spatial_w, spatial_b, lru_out_norm, gmlp_out_norm, w_out, ffn2_norm, ffn2_w_gu, ffn2_w_down, final_norm, loss_target, m_w_ada, m_b_ada, m_ffn1_norm, m_ffn1_w_gu, m_ffn1_w_down, m_mix_norm, m_w_in, m_conv_w, m_conv_b, m_gate_a_w, m_gate_a_b, m_gate_x_w, m_gate_x_b, m_lru_lambda, m_v_norm, m_spatial_w, m_spatial_b, m_lru_out_norm, m_gmlp_out_norm, m_w_out, m_ffn2_norm, m_ffn2_w_gu, m_ffn2_w_down, m_final_norm, v_w_ada, v_b_ada, v_ffn1_norm, v_ffn1_w_gu, v_ffn1_w_down, v_mix_norm, v_w_in, v_conv_w, v_conv_b, v_gate_a_w, v_gate_a_b, v_gate_x_w, v_gate_x_b, v_lru_lambda, v_v_norm, v_spatial_w, v_spatial_b, v_lru_out_norm, v_gmlp_out_norm, v_w_out, v_ffn2_norm, v_ffn2_w_gu, v_ffn2_w_down, v_final_norm):
    w = dict(w_ada=w_ada, b_ada=b_ada, ffn1_norm=ffn1_norm, ffn1_w_gu=ffn1_w_gu, ffn1_w_down=ffn1_w_down, mix_norm=mix_norm, w_in=w_in, conv_w=conv_w, conv_b=conv_b, gate_a_w=gate_a_w, gate_a_b=gate_a_b, gate_x_w=gate_x_w, gate_x_b=gate_x_b, lru_lambda=lru_lambda, v_norm=v_norm, spatial_w=spatial_w, spatial_b=spatial_b, lru_out_norm=lru_out_norm, gmlp_out_norm=gmlp_out_norm, w_out=w_out, ffn2_norm=ffn2_norm, ffn2_w_gu=ffn2_w_gu, ffn2_w_down=ffn2_w_down, final_norm=final_norm)
    m = dict(w_ada=m_w_ada, b_ada=m_b_ada, ffn1_norm=m_ffn1_norm, ffn1_w_gu=m_ffn1_w_gu, ffn1_w_down=m_ffn1_w_down, mix_norm=m_mix_norm, w_in=m_w_in, conv_w=m_conv_w, conv_b=m_conv_b, gate_a_w=m_gate_a_w, gate_a_b=m_gate_a_b, gate_x_w=m_gate_x_w, gate_x_b=m_gate_x_b, lru_lambda=m_lru_lambda, v_norm=m_v_norm, spatial_w=m_spatial_w, spatial_b=m_spatial_b, lru_out_norm=m_lru_out_norm, gmlp_out_norm=m_gmlp_out_norm, w_out=m_w_out, ffn2_norm=m_ffn2_norm, ffn2_w_gu=m_ffn2_w_gu, ffn2_w_down=m_ffn2_w_down, final_norm=m_final_norm)
    v = dict(w_ada=v_w_ada, b_ada=v_b_ada, ffn1_norm=v_ffn1_norm, ffn1_w_gu=v_ffn1_w_gu, ffn1_w_down=v_ffn1_w_down, mix_norm=v_mix_norm, w_in=v_w_in, conv_w=v_conv_w, conv_b=v_conv_b, gate_a_w=v_gate_a_w, gate_a_b=v_gate_a_b, gate_x_w=v_gate_x_w, gate_x_b=v_gate_x_b, lru_lambda=v_lru_lambda, v_norm=v_v_norm, spatial_w=v_spatial_w, spatial_b=v_spatial_b, lru_out_norm=v_lru_out_norm, gmlp_out_norm=v_gmlp_out_norm, w_out=v_w_out, ffn2_norm=v_ffn2_norm, ffn2_w_gu=v_ffn2_w_gu, ffn2_w_down=v_ffn2_w_down, final_norm=v_final_norm)
    me = 4 * lax.axis_index("x") + 2 * lax.axis_index("y") + lax.axis_index("c")

    loc = dict(gu=jnp.concatenate([ffn1_w_gu, ffn2_w_gu], axis=0).astype(BF16),
               down=jnp.concatenate([ffn1_w_down, ffn2_w_down], axis=0).astype(BF16),
               w_in=w_in.astype(BF16), w_out=w_out.astype(BF16))
    c_g, conv_g, gu0, down0 = all_gather([(c, None), (conv_w, None), (loc['gu'], 0), (loc['down'], 0)], "gather_first")
    conv_w_full = conv_g.transpose(1, 2, 0, 3).reshape(DEPTH, CONV_WIDTH, LW)

    b_loc = lax.dynamic_slice(b_ada, (0, me * AC), (DEPTH, AC)).reshape(DEPTH, 1, AC)
    mod_cols, sc_all = ada_fwd(c_g.reshape(NDEV, D), w_ada, b_loc)
    (mod_rows,) = all_to_all([mod_cols.transpose(1, 0, 2)], "scatter_mod")
    mod = mod_rows.transpose(1, 0, 2).reshape(DEPTH, NMOD, D)

    small_w = {k: w[k] for k in PACKED if k != 'conv_w'}
    me_arr = jnp.reshape(me, (1,)).astype(jnp.int32)
    loss_loc, dx, big, small_g, dmod = local_fwd_bwd(me_arr, x[0], loss_target[0], mod, small_w, loc,
                                                     {('gu', 0): gu0, ('down', 0): down0}, conv_w_full)
    loss = lax.psum(loss_loc, ("x", "y", "c"))

    small_g['b_ada'] = dmod.reshape(DEPTH, NMOD * D)
    gpack = _pack(small_g)
    rows = gpack.shape[0]
    dmod_out = dmod.reshape(DEPTH, NDEV, AC).transpose(1, 0, 2)
    dmod_r, pack_r = all_to_all([dmod_out, gpack.reshape(NDEV, rows // NDEV, PACK_LANES)], "scatter_grads")
    (gsum_g,) = all_gather([(sum_parts(pack_r), None)], "gather_small_grads")
    gsum = gsum_g.reshape(1, 1, rows, PACK_LANES)

    res = {}
    t = lambda a: a.transpose(0, 2, 1)
    res['ffn1_w_gu'] = tuple(t(r) for r in adamw(big['gu'], 0, t(w['ffn1_w_gu']), t(m['ffn1_w_gu']), t(v['ffn1_w_gu']),
                                                 "adamw_gu_a"))
    res['ffn2_w_gu'] = tuple(t(r) for r in adamw(big['gu'], DEPTH, t(w['ffn2_w_gu']), t(m['ffn2_w_gu']),
                                                 t(v['ffn2_w_gu']), "adamw_gu_b"))
    res['ffn1_w_down'] = adamw(big['down'], 0, w['ffn1_w_down'], m['ffn1_w_down'], v['ffn1_w_down'], "adamw_down_a")
    res['ffn2_w_down'] = adamw(big['down'], DEPTH, w['ffn2_w_down'], m['ffn2_w_down'], v['ffn2_w_down'], "adamw_down_b")
    res['w_in'] = adamw(big['w_in'], 0, w['w_in'], m['w_in'], v['w_in'], "adamw_w_in")
    res['w_out'] = adamw(big['w_out'], 0, w['w_out'], m['w_out'], v['w_out'], "adamw_w_out")
    g_ada = ada_bwd(sc_all.T, dmod_r.transpose(1, 0, 2))
    res['w_ada'] = adamw(g_ada, 0, w['w_ada'], m['w_ada'], v['w_ada'], "adamw_w_ada")
    shapes = {k: w[k].shape for k in PACKED}
    shapes['conv_w'] = (DEPTH, CONV_WIDTH, LW)
    dummy = jnp.zeros(shapes['conv_w'], F32)
    packs = adamw(gsum, 0, _pack({**small_w, 'conv_w': dummy})[None], _pack({**{k: m[k] for k in small_w}, 'conv_w': dummy})[None],
                  _pack({**{k: v[k] for k in small_w}, 'conv_w': dummy})[None], "adamw_small")
    unpacked = [_unpack(b[0], shapes) for b in packs]
    for k in small_w:
        res[k] = tuple(u[k] for u in unpacked)
    gconv = lax.dynamic_slice(unpacked[0]['conv_w'], (0, 0, me * (LW // NDEV)), (DEPTH, CONV_WIDTH, LW // NDEV))
    cshape = (1, DEPTH * CONV_WIDTH, LW // NDEV)
    rc = adamw(gconv.reshape((1,) + cshape), 0, conv_w.reshape(cshape), m['conv_w'].reshape(cshape),
               v['conv_w'].reshape(cshape), "adamw_conv_w")
    res['conv_w'] = tuple(r.reshape(conv_w.shape) for r in rc)

    return (loss, dx[None], *[res[k][0] for k in WEIGHTS], *[res[k][1] for k in WEIGHTS],
            *[res[k][2] for k in WEIGHTS], *[res[k][3] for k in WEIGHTS])
```

```python
import jax
import jax.numpy as jnp
from jax import lax
from jax.experimental import pallas as pl
from jax.experimental.pallas import tpu as pltpu

F32 = jnp.float32
BF16 = jnp.bfloat16

NDEV = 8
DEPTH = 2
D = 1024
DFF = 2816
FC = 2 * DFF // NDEV
NCHUNK = DFF // FC
DR = DFF // NDEV
LW = 512
GW = 512
HD = 64
HEADS = 8
CHUNK = 128
PC = 2 * (LW + GW) // NDEV
OR = D // NDEV
NMOD = 9
AC = NMOD * D // NDEV
LC = 128
EPS = 1e-6
RG_LRU_C = 8.0
CONV_WIDTH = 4

ADAM_LR = 0.001
ADAM_B1 = 0.9
ADAM_B2 = 0.999
ADAM_EPS = 1e-08
ADAM_WD = 0.01
ADAM_STEP = 10

VMEM_LIMIT_BYTES = 60 * 1024 * 1024
MESH = pl.DeviceIdType.MESH
ANY = pl.BlockSpec(memory_space=pl.ANY)


def _cparams(*sem):
    return pltpu.CompilerParams(dimension_semantics=tuple(sem) if sem else None,
                                vmem_limit_bytes=VMEM_LIMIT_BYTES)


def _dot(a, b):
    return jnp.dot(a, b, preferred_element_type=F32)


def _dot_nt(a, b):
    return lax.dot_general(a, b, (((1,), (1,)), ((), ())), preferred_element_type=F32)


def _dot_tn(a, b):
    return lax.dot_general(a, b, (((0,), (0,)), ((), ())), preferred_element_type=F32)


def _split(a):
    hi = a.astype(BF16)
    lo = (a - hi.astype(F32)).astype(BF16)
    return hi, lo


def _dot3(a, b):
    ah, al = _split(a)
    bh, bl = _split(b)
    return _dot(ah, bh) + (_dot(ah, bl) + _dot(al, bh))


def _csum(a):
    return jnp.sum(a, axis=0, keepdims=True)


def _rmean(a):
    return jnp.mean(a, axis=-1, keepdims=True)


def _sigmoid(a):
    return 1.0 / (1.0 + jnp.exp(-a))


_GELU_K = 0.7978845608028654
_GELU_C = 0.044715


def _gelu(a):
    return 0.5 * a * (1.0 + jnp.tanh(_GELU_K * (a + _GELU_C * a * a * a)))


def _gelu_grad(a):
    t = jnp.tanh(_GELU_K * (a + _GELU_C * a * a * a))
    return 0.5 * (1.0 + t) + 0.5 * a * (1.0 - t * t) * (_GELU_K * (1.0 + 3.0 * _GELU_C * a * a))


def _norm_mod(x, gain, scale, shift):
    rstd = lax.rsqrt(_rmean(x * x) + EPS)
    return (x * rstd * gain) * (1.0 + scale) + shift


def _norm_mod_bwd(dh, x, gain, scale):
    rstd = lax.rsqrt(_rmean(x * x) + EPS)
    xhat = x * rstd
    dshift = _csum(dh)
    dscale = _csum(dh * (xhat * gain))
    dhn = dh * (1.0 + scale)
    dgain = _csum(dhn * xhat)
    dxhat = dhn * gain
    dx = rstd * (dxhat - xhat * _rmean(dxhat * xhat))
    return dx, dshift, dscale, dgain


def _rms(x, gain):
    rstd = lax.rsqrt(_rmean(x * x) + EPS)
    return x * rstd * gain


def _rms_bwd(dy, x, gain):
    rstd = lax.rsqrt(_rmean(x * x) + EPS)
    xhat = x * rstd
    dgain = _csum(dy * xhat)
    dxhat = dy * gain
    return rstd * (dxhat - xhat * _rmean(dxhat * xhat)), dgain


PAIR = 2 * HD


def _seg_mean(a, pavg):
    hi, lo = _split(a)
    return jnp.concatenate([_dot(hi[:, p:p + PAIR], pavg) + _dot(lo[:, p:p + PAIR], pavg)
                            for p in range(0, a.shape[1], PAIR)], axis=1)


def _block_copies(src_hbm, dst_vmem, sems, rows):
    copies = []
    for k in range(NDEV):
        dst = dst_vmem.at[k] if rows is None else dst_vmem.at[pl.ds(k * rows, rows)]
        copies.append(pltpu.make_async_copy(src_hbm.at[k], dst, sems.at[k]))
    return copies


def _ffn_weight_fetch(wgu_hbm, wd_hbm, wgu_v, wd_v, sems):
    @pl.when(pl.program_id(0) == 0)
    def _():
        copies = _block_copies(wgu_hbm, wgu_v, sems.at[0], None) + _block_copies(wd_hbm, wd_v, sems.at[1], DR)
        for cp in copies:
            cp.start()
        for cp in copies:
            cp.wait()


def _place():
    return lax.axis_index("x"), lax.axis_index("y"), lax.axis_index("c")


def _slot(p):
    return 4 * p[0] + 2 * p[1] + p[2]


class GatherRide:
    def __init__(self, srcs):
        self.n = len(srcs)
        self.index = [i for _, i in srcs]
        self.args = [a for a, _ in srcs]
        self.out_shape = [jax.ShapeDtypeStruct((NDEV,) + (a.shape if i is None else a.shape[1:]), a.dtype)
                          for a, i in srcs]
        self.scratch = [pltpu.SemaphoreType.DMA((self.n, NDEV - 1)), pltpu.SemaphoreType.DMA((self.n, NDEV - 1)),
                        pltpu.SemaphoreType.DMA((self.n,))]

    def hooks(self, ins, outs, sems):
        send_sems, recv_sems, local_sems = sems
        n = self.n
        x, y, c = _place()
        me, sibling = (x, y, c), (x, y, 1 - c)
        chips = [(1 - x, y), (x, 1 - y), (1 - x, 1 - y)]

        def local(a):
            return ins[a] if self.index[a] is None else ins[a].at[self.index[a]]

        def copy(a, k, block, to, src=None):
            dst = outs[a].at[_slot(block)]
            return pltpu.make_async_remote_copy(
                src_ref=dst if src is None else src, dst_ref=dst,
                send_sem=send_sems.at[a, k], recv_sem=recv_sems.at[a, k],
                device_id=to, device_id_type=MESH)

        def mine():
            return [pltpu.make_async_copy(local(a), outs[a].at[_slot(me)], local_sems.at[a]) for a in range(n)]

        def first():
            cps = []
            for a in range(n):
                cps.append(copy(a, 0, me, sibling, src=local(a)))
                cps += [copy(a, 1 + j, me, (*chip, c), src=local(a)) for j, chip in enumerate(chips)]
            return cps

        def passed():
            return [copy(a, 4 + j, (*chip, c), sibling) for j, chip in enumerate(chips) for a in range(n)]

        def start():
            for cp in mine() + first():
                cp.start()

        def mid():
            for j, chip in enumerate(chips):
                for a in range(n):
                    copy(a, 1 + j, (*chip, c), me).wait_recv()
                    copy(a, 4 + j, (*chip, c), sibling).start()

        def finish():
            for a in range(n):
                copy(a, 0, sibling, me).wait_recv()
                for j, chip in enumerate(chips):
                    copy(a, 4 + j, (*chip, 1 - c), me).wait_recv()
            for cp in first() + passed():
                cp.wait_send()
            for cp in mine():
                cp.wait()

        return start, mid, finish


def all_gather(srcs, name):
    ride = GatherRide(srcs)
    n = ride.n

    def body(*refs):
        start, mid, finish = ride.hooks(refs[:n], refs[n:2 * n], refs[2 * n:])
        start()
        mid()
        finish()

    return pl.pallas_call(
        body, name=name,
        in_specs=[ANY] * n, out_specs=[ANY] * n, out_shape=ride.out_shape, scratch_shapes=ride.scratch,
    )(*ride.args)


def _call(core, ride, *, name, grid, in_specs, out_specs, out_shape, scratch_shapes, args):
    if ride is None:
        outs = pl.pallas_call(core, name=name, grid=grid, in_specs=in_specs, out_specs=out_specs,
                              out_shape=out_shape, scratch_shapes=scratch_shapes,
                              compiler_params=_cparams("arbitrary"))(*args)
        return outs, []
    n_in, n_out, n_sc, n = len(in_specs), len(out_shape), len(scratch_shapes), ride.n
    nsteps = grid[0]
    mid_step = max(nsteps - 2, 0)

    def body(*refs):
        cuts = [n_in, n_in + n, n_in + n + n_out, n_in + 2 * n + n_out, n_in + 2 * n + n_out + n_sc]
        ci, ri, co, ro, cs, rs = (refs[a:b] for a, b in zip([0] + cuts, cuts + [len(refs)]))
        start, mid, finish = ride.hooks(ri, ro, rs)
        i = pl.program_id(0)
        pl.when(i == 0)(start)
        core(*ci, *co, *cs)
        pl.when(i == mid_step)(mid)
        pl.when(i == nsteps - 1)(finish)

    outs = pl.pallas_call(
        body, name=name, grid=grid,
        in_specs=list(in_specs) + [ANY] * n, out_specs=list(out_specs) + [ANY] * n,
        out_shape=list(out_shape) + ride.out_shape, scratch_shapes=list(scratch_shapes) + ride.scratch,
        compiler_params=_cparams("arbitrary"))(*args, *ride.args)
    return outs[:n_out], outs[n_out:]


def all_to_all(arrs, name):
    n = len(arrs)

    def body(*refs):
        ins, outs = refs[:n], refs[n:2 * n]
        send_sems, recv_sems, local_sems = refs[2 * n:]
        x, y, c = _place()
        me = (x, y, c)

        def peer(k):
            return (1 - x if k & 4 else x, 1 - y if k & 2 else y, 1 - c if k & 1 else c)

        def copy(a, k):
            return pltpu.make_async_remote_copy(
                src_ref=ins[a].at[_slot(peer(k))], dst_ref=outs[a].at[_slot(me)],
                send_sem=send_sems.at[a, k - 1], recv_sem=recv_sems.at[a, k - 1],
                device_id=peer(k), device_id_type=MESH)

        def landing(a, k):
            return pltpu.make_async_remote_copy(
                src_ref=outs[a].at[_slot(peer(k))], dst_ref=outs[a].at[_slot(peer(k))],
                send_sem=send_sems.at[a, k - 1], recv_sem=recv_sems.at[a, k - 1],
                device_id=me, device_id_type=MESH)

        mine = [pltpu.make_async_copy(ins[a].at[_slot(me)], outs[a].at[_slot(me)], local_sems.at[a]) for a in range(n)]
        for cp in mine:
            cp.start()
        sends = [copy(a, k) for a in range(n) for k in range(1, NDEV)]
        for cp in sends:
            cp.start()
        for a in range(n):
            for k in range(1, NDEV):
                landing(a, k).wait_recv()
        for cp in sends:
            cp.wait_send()
        for cp in mine:
            cp.wait()

    return pl.pallas_call(
        body, name=name,
        in_specs=[ANY] * n, out_specs=[ANY] * n,
        out_shape=[jax.ShapeDtypeStruct(a.shape, a.dtype) for a in arrs],
        scratch_shapes=[pltpu.SemaphoreType.DMA((n, NDEV - 1)), pltpu.SemaphoreType.DMA((n, NDEV - 1)),
                        pltpu.SemaphoreType.DMA((n,))],
    )(*arrs)


FFN_TS = 256


def ffn_fwd(x, vec, wgu_g, wdown_g, tag, ride=None):
    S = x.shape[0]
    ts = min(FFN_TS, S)

    def body(x_ref, vec_ref, wgu_hbm, wd_hbm, xo_ref, h_ref, gu_ref, f_ref, wgu_v, wd_v, sems):
        _ffn_weight_fetch(wgu_hbm, wd_hbm, wgu_v, wd_v, sems)
        xv = x_ref[...]
        h = _norm_mod(xv, vec_ref[3:4, :], vec_ref[1:2, :], vec_ref[0:1, :]).astype(BF16)
        h_ref[...] = h
        acc = jnp.zeros((ts, D), F32)
        for j in range(NCHUNK):
            g = _dot(h, wgu_v[j])
            u = _dot(h, wgu_v[NCHUNK + j])
            gu_ref[j] = g.astype(BF16)
            gu_ref[NCHUNK + j] = u.astype(BF16)
            a = (g * _sigmoid(g) * u).astype(BF16)
            acc = acc + _dot(a, wd_v[pl.ds(j * FC, FC), :])
        f_ref[...] = acc.astype(BF16)
        xo_ref[...] = xv + (0.5 * vec_ref[2:3, :]) * acc

    return _call(
        body, ride, name=f"ffn_fwd_{tag}",
        grid=(S // ts,),
        in_specs=[pl.BlockSpec((ts, D), lambda i: (i, 0)),
                  pl.BlockSpec((8, D), lambda i: (0, 0)), ANY, ANY],
        out_specs=[pl.BlockSpec((ts, D), lambda i: (i, 0)),
                   pl.BlockSpec((ts, D), lambda i: (i, 0)),
                   pl.BlockSpec((NDEV, ts, FC), lambda i: (0, i, 0)),
                   pl.BlockSpec((ts, D), lambda i: (i, 0))],
        out_shape=[jax.ShapeDtypeStruct((S, D), F32), jax.ShapeDtypeStruct((S, D), BF16),
                   jax.ShapeDtypeStruct((NDEV, S, FC), BF16), jax.ShapeDtypeStruct((S, D), BF16)],
        scratch_shapes=[pltpu.VMEM((NDEV, D, FC), BF16), pltpu.VMEM((DFF, D), BF16),
                        pltpu.SemaphoreType.DMA((2, NDEV))],
        args=(x, vec, wgu_g, wdown_g))


def ffn_bwd(dxo, x, gu, f, vec, wgu_g, wdown_g, tag):
    S = x.shape[0]
    ts = min(FFN_TS, S)

    def body(dxo_ref, x_ref, gu_ref, f_ref, vec_ref, wgu_hbm, wd_hbm,
             dx_ref, dgu_ref, a_ref, df_ref, acc_ref, wgu_v, wd_v, sems):
        _ffn_weight_fetch(wgu_hbm, wd_hbm, wgu_v, wd_v, sems)

        @pl.when(pl.program_id(0) == 0)
        def _():
            acc_ref[...] = jnp.zeros_like(acc_ref)

        dxo_v = dxo_ref[...]
        dgate = 0.5 * _csum(dxo_v * f_ref[...].astype(F32))
        df = ((0.5 * vec_ref[2:3, :]) * dxo_v).astype(BF16)
        df_ref[...] = df
        dh = jnp.zeros((ts, D), F32)
        for j in range(NCHUNK):
            da = _dot_nt(df, wd_v[pl.ds(j * FC, FC), :])
            g = gu_ref[j].astype(F32)
            u = gu_ref[NCHUNK + j].astype(F32)
            sg = _sigmoid(g)
            si = g * sg
            a_ref[j] = (si * u).astype(BF16)
            dg = (da * u * (sg * (1.0 + g * (1.0 - sg)))).astype(BF16)
            du = (da * si).astype(BF16)
            dgu_ref[j] = dg
            dgu_ref[NCHUNK + j] = du
            dh = dh + _dot_nt(dg, wgu_v[j]) + _dot_nt(du, wgu_v[NCHUNK + j])
        dx, dshift, dscale, dgain = _norm_mod_bwd(dh, x_ref[...], vec_ref[3:4, :], vec_ref[1:2, :])
        dx_ref[...] = dx + dxo_v
        acc_ref[0:1, :] += dshift
        acc_ref[1:2, :] += dscale
        acc_ref[2:3, :] += dgate
        acc_ref[3:4, :] += dgain

    row = pl.BlockSpec((ts, D), lambda i: (i, 0))
    return pl.pallas_call(
        body, name=f"ffn_bwd_{tag}",
        grid=(S // ts,),
        in_specs=[row, row, pl.BlockSpec((NDEV, ts, FC), lambda i: (0, i, 0)), row,
                  pl.BlockSpec((8, D), lambda i: (0, 0)), ANY, ANY],
        out_specs=[row, pl.BlockSpec((NDEV, ts, FC), lambda i: (0, i, 0)),
                   pl.BlockSpec((NCHUNK, ts, FC), lambda i: (0, i, 0)), row,
                   pl.BlockSpec((8, D), lambda i: (0, 0))],
        out_shape=[jax.ShapeDtypeStruct((S, D), F32), jax.ShapeDtypeStruct((NDEV, S, FC), BF16),
                   jax.ShapeDtypeStruct((NCHUNK, S, FC), BF16), jax.ShapeDtypeStruct((S, D), BF16),
                   jax.ShapeDtypeStruct((8, D), F32)],
        scratch_shapes=[pltpu.VMEM((NDEV, D, FC), BF16), pltpu.VMEM((DFF, D), BF16),
                        pltpu.SemaphoreType.DMA((2, NDEV))],
        compiler_params=_cparams("arbitrary"),
    )(dxo, x, gu, f, vec, wgu_g, wdown_g)


NCHIP = NDEV // 2


def tn_matmul_scatter(me_arr, a, b, slot, nslots, prev, name, split=1):
    na, S, M = a.shape
    nb, _, N = b.shape
    ncall = NDEV // split
    ts = min(4096, S)
    nsteps = S // ts
    mp = M // split
    other_step = {1: lambda j: 2 * j, 2: lambda j: j, 8: lambda j: 0}[split]
    mine_step = {1: lambda j: 2 * j + 1, 2: lambda j: j, 8: lambda j: 0}[split]

    def group(k, me_ref):
        if split == 1:
            return jnp.bitwise_xor(me_ref[0], NDEV - 1 - k)
        if split == 2:
            return jnp.bitwise_xor(me_ref[0] // 2, NCHIP - 1 - k)
        return 0

    def body(me_ref, *refs):
        a_ref, b_ref = refs[0], refs[1]
        recv_ref, acc, sb_other, sb_mine, land, d2d_send, d2d_recv, ici_send, ici_recv = refs[-9:]
        k = pl.program_id(0)
        s = pl.program_id(1)
        x, y, c = _place()
        my_chip = 2 * x + y

        def chip_of(j):
            if split == 8:
                cx, cy = j // 2, j % 2
            else:
                flip = NCHIP - 1 - j
                cx, cy = (1 - x if flip & 2 else x), (1 - y if flip & 1 else y)
            return cx, cy, 2 * cx + cy

        def piece(j, core):
            if split == 1:
                return acc[...]
            start = core * mp if split == 2 else (2 * j + core) * mp
            return acc[pl.ds(pl.multiple_of(start, 8), mp), :]

        def to_sibling(j):
            return pltpu.make_async_remote_copy(
                src_ref=sb_other.at[j], dst_ref=land.at[j], send_sem=d2d_send.at[j], recv_sem=d2d_recv.at[j],
                device_id=(x, y, 1 - c), device_id_type=MESH)

        def to_owner(j):
            cx, cy, ci = chip_of(j)
            dst = recv_ref.at[my_chip, slot]
            return ci, pltpu.make_async_copy(sb_mine.at[j], dst, ici_send.at[j]), pltpu.make_async_remote_copy(
                src_ref=sb_mine.at[j], dst_ref=dst, send_sem=ici_send.at[j], recv_sem=ici_recv.at[my_chip],
                device_id=(cx, cy, c), device_id_type=MESH)

        if nsteps == 1:
            acc[...] = _dot_tn(a_ref[...], b_ref[...])
        else:
            @pl.when(s == 0)
            def _():
                acc[...] = jnp.zeros_like(acc)

            acc[...] += _dot_tn(a_ref[...], b_ref[...])

        for kk in range(ncall):
            @pl.when((s == nsteps - 1) & (k == kk))
            def _():
                for j in range(NCHIP):
                    if other_step(j) == kk:
                        sb_other[j] = piece(j, 1 - c).astype(BF16)
                        to_sibling(j).start()
                for j in range(NCHIP):
                    if mine_step(j) == kk:
                        to_sibling(j).wait_recv()
                        sb_mine[j] = (piece(j, c) + land[j].astype(F32)).astype(BF16)
                        ci, loc, rem = to_owner(j)
                        pl.when(ci == my_chip)(loc.start)
                        pl.when(ci != my_chip)(rem.start)

        @pl.when((s == nsteps - 1) & (k == ncall - 1))
        def _():
            for j in range(NCHIP):
                to_sibling(j).wait_send()
                ci, loc, rem = to_owner(j)
                pl.when(ci == my_chip)(loc.wait)
                pl.when(ci != my_chip)(rem.wait_send)
            for src in range(NCHIP):
                @pl.when(my_chip != src)
                def _():
                    pltpu.make_async_remote_copy(
                        src_ref=recv_ref.at[src, slot], dst_ref=recv_ref.at[src, slot],
                        send_sem=ici_send.at[src], recv_sem=ici_recv.at[src],
                        device_id=(src // 2, src % 2, c), device_id_type=MESH).wait_recv()

    in_specs = [pl.BlockSpec((None, ts, M), (lambda k, s, me: (group(k, me), s, 0)) if na > 1 else (lambda k, s, me: (0, s, 0))),
                pl.BlockSpec((None, ts, N), (lambda k, s, me: (group(k, me), s, 0)) if nb > 1 else (lambda k, s, me: (0, s, 0)))]
    args = [me_arr, a, b]
    aliases = {}
    if prev is not None:
        in_specs.append(ANY)
        args.append(prev)
        aliases = {3: 0}
    return pl.pallas_call(
        body, name=name,
        grid_spec=pltpu.PrefetchScalarGridSpec(
            num_scalar_prefetch=1, grid=(ncall, nsteps), in_specs=in_specs, out_specs=ANY,
            scratch_shapes=[pltpu.VMEM((M, N), F32), pltpu.VMEM((NCHIP, mp, N), BF16), pltpu.VMEM((NCHIP, mp, N), BF16),
                            pltpu.VMEM((NCHIP, mp, N), BF16), pltpu.SemaphoreType.DMA((NCHIP,)),
                            pltpu.SemaphoreType.DMA((NCHIP,)), pltpu.SemaphoreType.DMA((NCHIP,)),
                            pltpu.SemaphoreType.DMA((NCHIP,))]),
        out_shape=jax.ShapeDtypeStruct((NCHIP, nslots, mp, N), BF16),
        input_output_aliases=aliases,
        compiler_params=_cparams("arbitrary", "arbitrary"),
    )(*args)


MIX_TS = 256


def mix_in_fwd(x, vec, win_g, tag, ride=None):
    S = x.shape[0]
    ts = min(MIX_TS, S)

    def body(x_ref, vec_ref, win_ref, hm_ref, proj_ref):
        h = _norm_mod(x_ref[...], vec_ref[3:4, :], vec_ref[1:2, :], vec_ref[0:1, :]).astype(BF16)
        hm_ref[...] = h
        for k in range(NDEV):
            proj_ref[k] = _dot(h, win_ref[k])

    return _call(
        body, ride, name=f"mix_in_fwd_{tag}",
        grid=(S // ts,),
        in_specs=[pl.BlockSpec((ts, D), lambda i: (i, 0)), pl.BlockSpec((8, D), lambda i: (0, 0)),
                  pl.BlockSpec((NDEV, D, PC), lambda i: (0, 0, 0))],
        out_specs=[pl.BlockSpec((ts, D), lambda i: (i, 0)),
                   pl.BlockSpec((NDEV, ts, PC), lambda i: (0, i, 0))],
        out_shape=[jax.ShapeDtypeStruct((S, D), BF16), jax.ShapeDtypeStruct((NDEV, S, PC), F32)],
        scratch_shapes=[], args=(x, vec, win_g))


def mix_in_bwd(dproj, x, dxo, vec, win_g, tag):
    S = x.shape[0]
    ts = min(MIX_TS, S)

    def body(dp_ref, x_ref, dxo_ref, vec_ref, win_ref, dx_ref, acc_ref):
        @pl.when(pl.program_id(0) == 0)
        def _():
            acc_ref[...] = jnp.zeros_like(acc_ref)

        dh = jnp.zeros((ts, D), F32)
        for k in range(NDEV):
            dh = dh + _dot_nt(dp_ref[k], win_ref[k])
        dx, dshift, dscale, dgain = _norm_mod_bwd(dh, x_ref[...], vec_ref[3:4, :], vec_ref[1:2, :])
        dx_ref[...] = dx + dxo_ref[...]
        acc_ref[0:1, :] += dshift
        acc_ref[1:2, :] += dscale
        acc_ref[3:4, :] += dgain

    row = pl.BlockSpec((ts, D), lambda i: (i, 0))
    return pl.pallas_call(
        body, name=f"mix_in_bwd_{tag}",
        grid=(S // ts,),
        in_specs=[pl.BlockSpec((NDEV, ts, PC), lambda i: (0, i, 0)), row, row,
                  pl.BlockSpec((8, D), lambda i: (0, 0)),
                  pl.BlockSpec((NDEV, D, PC), lambda i: (0, 0, 0))],
        out_specs=[row, pl.BlockSpec((8, D), lambda i: (0, 0))],
        out_shape=[jax.ShapeDtypeStruct((S, D), F32), jax.ShapeDtypeStruct((8, D), F32)],
        compiler_params=_cparams("arbitrary"),
    )(dproj, x, dxo, vec, win_g)


SCAN_UNROLL = 4


def _shift_down(z, k, row):
    return jnp.where(row >= k, pltpu.roll(z, k, 0), 0.0)


def _shift_up(z, k, row, n):
    return jnp.where(row < n - k, pltpu.roll(z, n - k, 0), 0.0)


def _lru_gates(xc, lp_ref, wa_ref, wx_ref):
    xcb = xc.astype(BF16)
    ra = _sigmoid(_dot(xcb, wa_ref[...]) + lp_ref[5:6, :])
    ix = _sigmoid(_dot(xcb, wx_ref[...]) + lp_ref[6:7, :])
    lam = lp_ref[7:8, :]
    ls = jnp.minimum(lam, 0.0) - jnp.log(1.0 + jnp.exp(-jnp.abs(lam)))
    log_a = (RG_LRU_C * ls) * ra
    a = jnp.exp(log_a)
    mult = jnp.sqrt(-jnp.tanh(log_a) * (a * a + 1.0))
    return ra, ix, ls, a, mult


def _conv(x, lp_ref, row):
    return (lp_ref[4:5, :] + lp_ref[3:4, :] * x + lp_ref[2:3, :] * _shift_down(x, 1, row)
            + lp_ref[1:2, :] * _shift_down(x, 2, row) + lp_ref[0:1, :] * _shift_down(x, 3, row))


def lru_fwd(proj, lp, wa_t, wx_t, tag, ride=None):
    S = proj.shape[1]
    nblk = S // 8

    def body(x_ref, g_ref, lp_ref, wa_ref, wx_ref, y_ref, xc_ref, h_ref, a_s, b_s):
        x = x_ref[...]
        row = lax.broadcasted_iota(jnp.int32, x.shape, 0)
        xc = _conv(x, lp_ref, row)
        xc_ref[...] = xc
        ra, ix, ls, a, mult = _lru_gates(xc, lp_ref, wa_ref, wx_ref)
        a_s[...] = a
        b_s[...] = mult * (ix * xc)
        rowb = lax.broadcasted_iota(jnp.int32, (8, LC), 0)

        def step(i, carry):
            for q in range(SCAN_UNROLL):
                r0 = pl.multiple_of((i * SCAN_UNROLL + q) * 8, 8)
                A = a_s[pl.ds(r0, 8), :]
                B = b_s[pl.ds(r0, 8), :]
                for d in (1, 2, 4):
                    m = rowb >= d
                    As = jnp.where(m, pltpu.roll(A, d, 0), 1.0)
                    Bs = jnp.where(m, pltpu.roll(B, d, 0), 0.0)
                    B = A * Bs + B
                    A = A * As
                H = B + A * carry
                h_ref[pl.ds(r0, 8), :] = H
                carry = H[7:8, :]
            return carry

        lax.fori_loop(0, nblk // SCAN_UNROLL, step, jnp.zeros((1, LC), F32))
        y_ref[...] = h_ref[...] * _gelu(g_ref[...])

    col = pl.BlockSpec((S, LC), lambda c: (0, c))
    return _call(
        body, ride, name=f"lru_fwd_{tag}",
        grid=(LW // LC,),
        in_specs=[pl.BlockSpec((None, S, LC), lambda c: (c // 2, 0, c % 2)),
                  pl.BlockSpec((None, S, LC), lambda c: (2 + c // 2, 0, c % 2)),
                  pl.BlockSpec((8, LC), lambda c: (0, c)),
                  pl.BlockSpec((None, LC, LC), lambda c: (c, 0, 0)),
                  pl.BlockSpec((None, LC, LC), lambda c: (c, 0, 0))],
        out_specs=[col, col, col],
        out_shape=[jax.ShapeDtypeStruct((S, LW), F32)] * 3,
        scratch_shapes=[pltpu.VMEM((S, LC), F32), pltpu.VMEM((S, LC), F32)],
        args=(proj, proj, lp, wa_t, wx_t))


def lru_bwd(dy, proj, xc_all, hst, lp, wa_t, wx_t, tag):
    S = proj.shape[1]
    nblk = S // 8

    def body(dy_ref, x_ref, g_ref, xc_ref, h_ref, lp_ref, wa_ref, wx_ref,
             dx_ref, dg_ref, dlp_ref, dwa_ref, dwx_ref, c_s, l_s):
        xc = xc_ref[...]
        row = lax.broadcasted_iota(jnp.int32, xc.shape, 0)
        ra, ix, ls, a, mult = _lru_gates(xc, lp_ref, wa_ref, wx_ref)
        g = g_ref[...]
        dyv = dy_ref[...]
        h = h_ref[...]
        dg_ref[...] = (dyv * h * _gelu_grad(g)).astype(BF16)
        c_s[...] = _shift_up(a, 1, row, S)
        l_s[...] = dyv * _gelu(g)
        rowb = lax.broadcasted_iota(jnp.int32, (8, LC), 0)

        def step(i, carry):
            for q in range(SCAN_UNROLL):
                r0 = pl.multiple_of((nblk - 1 - (i * SCAN_UNROLL + q)) * 8, 8)
                C = c_s[pl.ds(r0, 8), :]
                L = l_s[pl.ds(r0, 8), :]
                for d in (1, 2, 4):
                    m = rowb < 8 - d
                    Cs = jnp.where(m, pltpu.roll(C, 8 - d, 0), 1.0)
                    Ls = jnp.where(m, pltpu.roll(L, 8 - d, 0), 0.0)
                    L = C * Ls + L
                    C = C * Cs
                L = L + C * carry
                l_s[pl.ds(r0, 8), :] = L
                carry = L[0:1, :]
            return carry

        lax.fori_loop(0, nblk // SCAN_UNROLL, step, jnp.zeros((1, LC), F32))
        db = l_s[...]
        da = db * _shift_down(h, 1, row)
        ixc = ix * xc
        dmult = db * ixc
        dix = db * (mult * xc)
        dxc = db * (mult * ix)
        dlog_a = da * a - dmult * (a * a) / mult
        dra = dlog_a * (RG_LRU_C * ls)
        dls = _csum(dlog_a * ra) * RG_LRU_C
        lam = lp_ref[7:8, :]
        dlam = dls * _sigmoid(-lam)
        dpa = dra * ra * (1.0 - ra)
        dpx = dix * ix * (1.0 - ix)
        dpab = dpa.astype(BF16)
        dpxb = dpx.astype(BF16)
        xcb = xc.astype(BF16)
        dwa_ref[...] = _dot_tn(xcb, dpab)
        dwx_ref[...] = _dot_tn(xcb, dpxb)
        dxc = dxc + _dot_nt(dpab, wa_ref[...]) + _dot_nt(dpxb, wx_ref[...])
        x = x_ref[...]
        dlp_ref[0:1, :] = _csum(dxc * _shift_down(x, 3, row))
        dlp_ref[1:2, :] = _csum(dxc * _shift_down(x, 2, row))
        dlp_ref[2:3, :] = _csum(dxc * _shift_down(x, 1, row))
        dlp_ref[3:4, :] = _csum(dxc * x)
        dlp_ref[4:5, :] = _csum(dxc)
        dlp_ref[5:6, :] = _csum(dpa)
        dlp_ref[6:7, :] = _csum(dpx)
        dlp_ref[7:8, :] = dlam
        dx = (lp_ref[3:4, :] * dxc + lp_ref[2:3, :] * _shift_up(dxc, 1, row, S)
              + lp_ref[1:2, :] * _shift_up(dxc, 2, row, S) + lp_ref[0:1, :] * _shift_up(dxc, 3, row, S))
        dx_ref[...] = dx.astype(BF16)

    col = pl.BlockSpec((S, LC), lambda c: (0, c))
    pcol = pl.BlockSpec((None, S, LC), lambda c: (c // 2, 0, c % 2))
    return pl.pallas_call(
        body, name=f"lru_bwd_{tag}",
        grid=(LW // LC,),
        in_specs=[col, pcol, pl.BlockSpec((None, S, LC), lambda c: (2 + c // 2, 0, c % 2)), col, col,
                  pl.BlockSpec((8, LC), lambda c: (0, c)),
                  pl.BlockSpec((None, LC, LC), lambda c: (c, 0, 0)),
                  pl.BlockSpec((None, LC, LC), lambda c: (c, 0, 0))],
        out_specs=[pcol, pcol, pl.BlockSpec((8, LC), lambda c: (0, c)),
                   pl.BlockSpec((None, LC, LC), lambda c: (c, 0, 0)),
                   pl.BlockSpec((None, LC, LC), lambda c: (c, 0, 0))],
        out_shape=[jax.ShapeDtypeStruct((2, S, PC), BF16), jax.ShapeDtypeStruct((2, S, PC), BF16),
                   jax.ShapeDtypeStruct((8, LW), F32),
                   jax.ShapeDtypeStruct((LW // LC, LC, LC), F32), jax.ShapeDtypeStruct((LW // LC, LC, LC), F32)],
        scratch_shapes=[pltpu.VMEM((S, LC), F32), pltpu.VMEM((S, LC), F32)],
        compiler_params=_cparams("arbitrary"),
    )(dy, proj, proj, xc_all, hst, lp, wa_t, wx_t)


def _pair_stack(zp, low):
    return jnp.concatenate([jnp.where(low, zp, 0.0), jnp.where(low, 0.0, zp)], axis=0).astype(BF16)


def _spatial(w_ref, zc, low):
    return jnp.concatenate(
        [_dot(w_ref[:, 2 * p * CHUNK:2 * (p + 1) * CHUNK], _pair_stack(zc[:, p * PAIR:(p + 1) * PAIR], low))
         for p in range(GW // PAIR)], axis=1)


def _gmlp_fwd_parts(u, v, gp_ref, wcat_ref, bz_ref, pavg_ref, ts):
    ug = _gelu(u)
    vg = _gelu(v)
    pavg = pavg_ref[...]
    vc = vg - _seg_mean(vg, pavg)
    rs = lax.rsqrt(_seg_mean(vc * vc, pavg) + EPS)
    vhat = vc * rs
    vh = vhat * gp_ref[0:1, :]
    low = lax.broadcasted_iota(jnp.int32, (CHUNK, PAIR), 1) < HD
    zs = [_spatial(wcat_ref, vh[n * CHUNK:(n + 1) * CHUNK, :], low) + bz_ref[...] for n in range(ts // CHUNK)]
    z = jnp.concatenate(zs, axis=0) if len(zs) > 1 else zs[0]
    return ug, rs, vhat, vh, z


def mix_out_fwd(proj, ylru, x, vec, gp, wcat, bz, pavg, wout_g, tag, ride=None):
    S = x.shape[0]
    ts = min(MIX_TS, S)

    def body(u_ref, v_ref, yl_ref, x_ref, vec_ref, gp_ref, wcat_ref, bz_ref, pavg_ref, wout_ref,
             xo_ref, y_ref, fo_ref):
        u = jnp.concatenate([u_ref[0], u_ref[1]], axis=1)
        v = jnp.concatenate([v_ref[0], v_ref[1]], axis=1)
        ug, _, _, _, z = _gmlp_fwd_parts(u, v, gp_ref, wcat_ref, bz_ref, pavg_ref, ts)
        n1 = _rms(yl_ref[...], gp_ref[1:2, :])
        n2 = _rms(ug * z, gp_ref[2:3, :])
        y = jnp.concatenate([n1, n2], axis=1).astype(BF16)
        y_ref[...] = y
        fo = jnp.zeros((ts, D), F32)
        for k in range(NDEV):
            fo = fo + _dot(y[:, k * OR:(k + 1) * OR], wout_ref[k])
        fo_ref[...] = fo.astype(BF16)
        xo_ref[...] = x_ref[...] + vec_ref[2:3, :] * fo

    row = pl.BlockSpec((ts, D), lambda i: (i, 0))
    full = lambda shp: pl.BlockSpec(shp, lambda i: tuple(0 for _ in shp))
    return _call(
        body, ride, name=f"mix_out_fwd_{tag}",
        grid=(S // ts,),
        in_specs=[pl.BlockSpec((2, ts, PC), lambda i: (2, i, 0)), pl.BlockSpec((2, ts, PC), lambda i: (3, i, 0)),
                  pl.BlockSpec((ts, LW), lambda i: (i, 0)), row, full((8, D)), full((8, GW)),
                  full((CHUNK, HEADS * CHUNK)), full((CHUNK, GW)), full((PAIR, PAIR)),
                  pl.BlockSpec((NDEV, OR, D), lambda i: (0, 0, 0))],
        out_specs=[row, row, row],
        out_shape=[jax.ShapeDtypeStruct((S, D), F32), jax.ShapeDtypeStruct((S, D), BF16),
                   jax.ShapeDtypeStruct((S, D), BF16)],
        scratch_shapes=[], args=(proj, proj, ylru, x, vec, gp, wcat, bz, pavg, wout_g))


def mix_out_bwd(dxo, proj, ylru, fo, vec, gp, wcat, wcat_t, bz, pavg, wout_g, tag):
    S = dxo.shape[0]
    ts = min(MIX_TS, S)

    def body(dxo_ref, u_ref, v_ref, yl_ref, fo_ref, vec_ref, gp_ref, wcat_ref, wcatt_ref, bz_ref, pavg_ref,
             wout_ref, dyo_ref, dyl_ref, duv_ref, acc_ref, dgp_ref, dwm_ref, dbz_ref):
        @pl.when(pl.program_id(0) == 0)
        def _():
            acc_ref[...] = jnp.zeros_like(acc_ref)
            dgp_ref[...] = jnp.zeros_like(dgp_ref)
            dwm_ref[...] = jnp.zeros_like(dwm_ref)
            dbz_ref[...] = jnp.zeros_like(dbz_ref)

        dxo_v = dxo_ref[...]
        acc_ref[2:3, :] += _csum(dxo_v * fo_ref[...].astype(F32))
        dyo = (vec_ref[2:3, :] * dxo_v).astype(BF16)
        dyo_ref[...] = dyo
        dn = [_dot_nt(dyo, wout_ref[k]) for k in range(NDEV)]
        dn1 = jnp.concatenate(dn[:NDEV // 2], axis=1)
        dn2 = jnp.concatenate(dn[NDEV // 2:], axis=1)
        dyl, dg1 = _rms_bwd(dn1, yl_ref[...], gp_ref[1:2, :])
        dyl_ref[...] = dyl
        u = jnp.concatenate([u_ref[0], u_ref[1]], axis=1)
        v = jnp.concatenate([v_ref[0], v_ref[1]], axis=1)
        ug, rs, vhat, vh, z = _gmlp_fwd_parts(u, v, gp_ref, wcat_ref, bz_ref, pavg_ref, ts)
        dyg, dg2 = _rms_bwd(dn2, ug * z, gp_ref[2:3, :])
        du = (dyg * z) * _gelu_grad(u)
        dz = dyg * ug
        low = lax.broadcasted_iota(jnp.int32, (CHUNK, PAIR), 1) < HD
        vhb = vh.astype(BF16)
        dvhs = []
        dbz = jnp.zeros((CHUNK, GW), F32)
        dwm = [jnp.zeros((2 * CHUNK, CHUNK), F32) for _ in range(GW // PAIR)]
        for n in range(ts // CHUNK):
            dzc = dz[n * CHUNK:(n + 1) * CHUNK, :]
            dbz = dbz + dzc
            for p in range(GW // PAIR):
                stack = _pair_stack(dzc[:, p * PAIR:(p + 1) * PAIR], low)
                dwm[p] = dwm[p] + _dot_nt(stack, vhb[n * CHUNK:(n + 1) * CHUNK, p * PAIR:(p + 1) * PAIR])
            dvhs.append(_spatial(wcatt_ref, dzc, low))
        dbz_ref[...] += dbz
        for p in range(GW // PAIR):
            dwm_ref[2 * p * CHUNK:2 * (p + 1) * CHUNK, :] += dwm[p]
        dvh = jnp.concatenate(dvhs, axis=0) if len(dvhs) > 1 else dvhs[0]
        pavg = pavg_ref[...]
        dvn = _csum(dvh * vhat)
        dvhat = dvh * gp_ref[0:1, :]
        dvg = rs * (dvhat - _seg_mean(dvhat, pavg) - vhat * _seg_mean(dvhat * vhat, pavg))
        dv = dvg * _gelu_grad(v)
        duv_ref[0] = du[:, :PC].astype(BF16)
        duv_ref[1] = du[:, PC:].astype(BF16)
        duv_ref[2] = dv[:, :PC].astype(BF16)
        duv_ref[3] = dv[:, PC:].astype(BF16)
        dgp_ref[0:1, :] += dvn
        dgp_ref[1:2, :] += dg1
        dgp_ref[2:3, :] += dg2

    row = pl.BlockSpec((ts, D), lambda i: (i, 0))
    full = lambda shp: pl.BlockSpec(shp, lambda i: tuple(0 for _ in shp))
    return pl.pallas_call(
        body, name=f"mix_out_bwd_{tag}",
        grid=(S // ts,),
        in_specs=[row, pl.BlockSpec((2, ts, PC), lambda i: (2, i, 0)), pl.BlockSpec((2, ts, PC), lambda i: (3, i, 0)),
                  pl.BlockSpec((ts, LW), lambda i: (i, 0)), row, full((8, D)), full((8, GW)),
                  full((CHUNK, HEADS * CHUNK)), full((CHUNK, HEADS * CHUNK)), full((CHUNK, GW)), full((PAIR, PAIR)),
                  pl.BlockSpec((NDEV, OR, D), lambda i: (0, 0, 0))],
        out_specs=[row, pl.BlockSpec((ts, LW), lambda i: (i, 0)), pl.BlockSpec((4, ts, PC), lambda i: (0, i, 0)),
                   full((8, D)), full((8, GW)), full((HEADS * CHUNK, CHUNK)), full((CHUNK, GW))],
        out_shape=[jax.ShapeDtypeStruct((S, D), BF16), jax.ShapeDtypeStruct((S, LW), F32),
                   jax.ShapeDtypeStruct((4, S, PC), BF16), jax.ShapeDtypeStruct((8, D), F32),
                   jax.ShapeDtypeStruct((8, GW), F32), jax.ShapeDtypeStruct((HEADS * CHUNK, CHUNK), F32),
                   jax.ShapeDtypeStruct((CHUNK, GW), F32)],
        compiler_params=_cparams("arbitrary"),
    )(dxo, proj, proj, ylru, fo, vec, gp, wcat, wcat_t, bz, pavg, wout_g)


def final_loss(x, target, gain):
    S = x.shape[0]
    ts = min(512, S)

    def body(x_ref, t_ref, g_ref, loss_ref, dx_ref, dg_ref):
        @pl.when(pl.program_id(0) == 0)
        def _():
            loss_ref[...] = jnp.zeros_like(loss_ref)
            dg_ref[...] = jnp.zeros_like(dg_ref)

        xv = x_ref[...]
        gain_v = g_ref[0:1, :]
        rstd = lax.rsqrt(_rmean(xv * xv) + EPS)
        xhat = xv * rstd
        err = xhat * gain_v - t_ref[...]
        loss_ref[...] += 0.5 * _csum(_rmean(err * err))
        dy = err * (1.0 / D)
        dg_ref[0:1, :] += _csum(dy * xhat)
        dxhat = dy * gain_v
        dx_ref[...] = rstd * (dxhat - xhat * _rmean(dxhat * xhat))

    row = pl.BlockSpec((ts, D), lambda i: (i, 0))
    return pl.pallas_call(
        body, name="final_loss",
        grid=(S // ts,),
        in_specs=[row, row, pl.BlockSpec((8, D), lambda i: (0, 0))],
        out_specs=[pl.BlockSpec((8, 128), lambda i: (0, 0)), row, pl.BlockSpec((8, D), lambda i: (0, 0))],
        out_shape=[jax.ShapeDtypeStruct((8, 128), F32), jax.ShapeDtypeStruct((S, D), F32),
                   jax.ShapeDtypeStruct((8, D), F32)],
        compiler_params=_cparams("arbitrary"),
    )(x, target, gain)


def _vec(mod_l, j, gain):
    return jnp.concatenate([mod_l[3 * j:3 * j + 3], gain[None, :], jnp.zeros((4, D), F32)], axis=0)


def _block_diag_tiles(w):
    w4 = w.reshape(LW // LC, 2, HD, HD)
    eye2 = jnp.eye(2, dtype=w.dtype)
    return (w4[:, :, :, None, :] * eye2[None, :, None, :, None]).reshape(LW // LC, LC, LC).astype(BF16)


def _block_diag_extract(dw):
    d5 = dw.reshape(LW // LC, 2, HD, 2, HD)
    return jnp.einsum('cihkj,ik->cihj', d5, jnp.eye(2, dtype=dw.dtype)).reshape(HEADS, HD, HD)


def _layer_params(l, p, conv_w_full):
    lp = jnp.concatenate([conv_w_full[l], p['conv_b'][l][None], p['gate_a_b'][l].reshape(1, LW),
                          p['gate_x_b'][l].reshape(1, LW), p['lru_lambda'][l][None]], axis=0)
    gp = jnp.concatenate([p['v_norm'][l][None], p['lru_out_norm'][l][None], p['gmlp_out_norm'][l][None],
                          jnp.zeros((5, GW), F32)], axis=0)
    ws = p['spatial_w'][l] * jnp.tril(jnp.ones((CHUNK, CHUNK), F32))
    wcat = ws.transpose(1, 0, 2).reshape(CHUNK, HEADS * CHUNK).astype(BF16)
    wcat_t = ws.transpose(2, 0, 1).reshape(CHUNK, HEADS * CHUNK).astype(BF16)
    bz = jnp.repeat(p['spatial_b'][l].T, HD, axis=1)
    return dict(lp=lp, gp=gp, wcat=wcat, wcat_t=wcat_t, bz=bz,
                wa_t=_block_diag_tiles(p['gate_a_w'][l]), wx_t=_block_diag_tiles(p['gate_x_w'][l]))


def _pavg():
    return jnp.kron(jnp.eye(2, dtype=F32), jnp.full((HD, HD), 1.0 / HD, F32)).astype(BF16)


GATHER_RIDES = {
    ('ffn_a', 0): [('w_in', 0), ('gu', DEPTH)],
    ('mix_in', 0): [('w_out', 0)],
    ('lru', 0): [('down', DEPTH)],
    ('mix_out', 0): [('down', 1)],
    ('ffn_b', 0): [('gu', 1), ('w_in', 1)],
    ('ffn_a', 1): [('gu', DEPTH + 1), ('w_out', 1)],
    ('mix_in', 1): [('down', DEPTH + 1)],
}


def local_fwd_bwd(me_arr, x, target, mod, p, loc, gathered, conv_w_full):
    pavg = _pavg()
    g = dict(gathered)

    def ride(call, l):
        todo = GATHER_RIDES.get((call, l))
        return None if todo is None else (todo, GatherRide([(loc[kind], slot) for kind, slot in todo]))

    def run(fn, call, l, *args):
        r = ride(call, l)
        outs, got = fn(*args, ride=None if r is None else r[1])
        if r is not None:
            g.update(dict(zip(r[0], got)))
        return outs

    saved = []
    h = x
    for l in range(DEPTH):
        q = _layer_params(l, p, conv_w_full)
        v1 = _vec(mod[l], 0, p['ffn1_norm'][l])
        vm = _vec(mod[l], 1, p['mix_norm'][l])
        v2 = _vec(mod[l], 2, p['ffn2_norm'][l])
        x0 = h
        x1, h1, gu1, f1 = run(ffn_fwd, 'ffn_a', l, x0, v1, g['gu', l], g['down', l], f"a{l}")
        hm, proj = run(mix_in_fwd, 'mix_in', l, x1, vm, g['w_in', l], f"{l}")
        ylru, xc, hst = run(lru_fwd, 'lru', l, proj, q['lp'], q['wa_t'], q['wx_t'], f"{l}")
        x2, y, fo = run(mix_out_fwd, 'mix_out', l, proj, ylru, x1, vm, q['gp'], q['wcat'], q['bz'], pavg,
                        g['w_out', l], f"{l}")
        x3, h2, gu2, f2 = run(ffn_fwd, 'ffn_b', l, x2, v2, g['gu', DEPTH + l], g['down', DEPTH + l], f"b{l}")
        saved.append(dict(q=q, v1=v1, vm=vm, v2=v2, x0=x0, x1=x1, x2=x2, h1=h1, gu1=gu1, f1=f1, hm=hm, proj=proj,
                          ylru=ylru, xc=xc, hst=hst, y=y, fo=fo, h2=h2, gu2=gu2, f2=f2))
        h = x3
    fin = jnp.concatenate([p['final_norm'][None], jnp.zeros((7, D), F32)], axis=0)
    loss8, dx, dfin = final_loss(h, target, fin)
    loss = loss8[0, 0]

    big = dict(gu=None, down=None, w_in=None, w_out=None)
    small = {k: [None] * DEPTH for k in ('ffn1_norm', 'mix_norm', 'ffn2_norm', 'conv_w', 'conv_b', 'gate_a_w',
                                         'gate_a_b', 'gate_x_w', 'gate_x_b', 'lru_lambda', 'v_norm', 'spatial_w',
                                         'spatial_b', 'lru_out_norm', 'gmlp_out_norm')}
    dmod = [None] * DEPTH
    tril = jnp.tril(jnp.ones((CHUNK, CHUNK), F32))
    for l in reversed(range(DEPTH)):
        sv = saved[l]
        q = sv['q']
        dx2, dgu, a, df, acc2 = ffn_bwd(dx, sv['x2'], sv['gu2'], sv['f2'], sv['v2'],
                                        g['gu', DEPTH + l], g['down', DEPTH + l], f"b{l}")
        big['gu'] = tn_matmul_scatter(me_arr, dgu, sv['h2'][None], DEPTH + l, 2 * DEPTH, big['gu'], f"dw_gu_b{l}")
        big['down'] = tn_matmul_scatter(me_arr, a, df[None], DEPTH + l, 2 * DEPTH, big['down'], f"dw_down_b{l}", split=2)
        dyo, dylru, duv, accmo, dgp, dwm, dbz = mix_out_bwd(dx2, sv['proj'], sv['ylru'], sv['fo'], sv['vm'], q['gp'],
                                                             q['wcat'], q['wcat_t'], q['bz'], pavg, g['w_out', l], f"{l}")
        big['w_out'] = tn_matmul_scatter(me_arr, sv['y'][None], dyo[None], l, DEPTH, big['w_out'], f"dw_out_{l}",
                                         split=NDEV)
        dxl, dgl, dlp, dwa, dwx = lru_bwd(dylru, sv['proj'], sv['xc'], sv['hst'], q['lp'], q['wa_t'], q['wx_t'], f"{l}")
        dproj = jnp.concatenate([dxl, dgl, duv], axis=0)
        dx1, accmi = mix_in_bwd(dproj, sv['x1'], dx2, sv['vm'], g['w_in', l], f"{l}")
        big['w_in'] = tn_matmul_scatter(me_arr, sv['hm'][None], dproj, l, DEPTH, big['w_in'], f"dw_in_{l}")
        dx0, dgu, a, df, acc1 = ffn_bwd(dx1, sv['x0'], sv['gu1'], sv['f1'], sv['v1'],
                                        g['gu', l], g['down', l], f"a{l}")
        big['gu'] = tn_matmul_scatter(me_arr, dgu, sv['h1'][None], l, 2 * DEPTH, big['gu'], f"dw_gu_a{l}")
        big['down'] = tn_matmul_scatter(me_arr, a, df[None], l, 2 * DEPTH, big['down'], f"dw_down_a{l}", split=2)
        dx = dx0
        dmod[l] = jnp.concatenate([acc1[0:3], accmi[0:2], accmo[2:3], acc2[0:3]], axis=0)
        small['ffn1_norm'][l] = acc1[3]
        small['mix_norm'][l] = accmi[3]
        small['ffn2_norm'][l] = acc2[3]
        small['conv_w'][l] = dlp[0:4]
        small['conv_b'][l] = dlp[4]
        small['gate_a_b'][l] = dlp[5].reshape(HEADS, HD)
        small['gate_x_b'][l] = dlp[6].reshape(HEADS, HD)
        small['lru_lambda'][l] = dlp[7]
        small['gate_a_w'][l] = _block_diag_extract(dwa)
        small['gate_x_w'][l] = _block_diag_extract(dwx)
        small['v_norm'][l] = dgp[0]
        small['lru_out_norm'][l] = dgp[1]
        small['gmlp_out_norm'][l] = dgp[2]
        small['spatial_w'][l] = dwm.reshape(HEADS, CHUNK, CHUNK) * tril
        small['spatial_b'][l] = dbz.reshape(CHUNK, HEADS, HD).sum(-1).T
    small = {k: jnp.stack(v) for k, v in small.items()}
    small['final_norm'] = dfin[0]
    return loss, dx, big, small, jnp.stack(dmod)


def ada_fwd(c_all, w_ada, b_loc):
    def body(c_ref, w_ref, b_ref, mod_ref, sc_ref):
        cv = c_ref[...]
        sc = cv * _sigmoid(cv)
        sc_ref[...] = sc
        mod_ref[...] = _dot3(sc, w_ref[...]) + b_ref[...]

    return pl.pallas_call(
        body, name="ada_fwd",
        grid=(DEPTH,),
        in_specs=[pl.BlockSpec((NDEV, D), lambda l: (0, 0)), pl.BlockSpec((None, D, AC), lambda l: (l, 0, 0)),
                  pl.BlockSpec((None, 1, AC), lambda l: (l, 0, 0))],
        out_specs=[pl.BlockSpec((None, NDEV, AC), lambda l: (l, 0, 0)), pl.BlockSpec((NDEV, D), lambda l: (0, 0))],
        out_shape=[jax.ShapeDtypeStruct((DEPTH, NDEV, AC), F32), jax.ShapeDtypeStruct((NDEV, D), F32)],
        compiler_params=_cparams("arbitrary"),
    )(c_all, w_ada, b_loc)


def ada_bwd(sc_t, dmod_cols):
    def body(sc_ref, dm_ref, g_ref):
        sc = sc_ref[...]
        dm = dm_ref[...]
        acc = sc[:, 0:1] * dm[0:1, :]
        for b in range(1, NDEV):
            acc = acc + sc[:, b:b + 1] * dm[b:b + 1, :]
        g_ref[...] = acc

    return pl.pallas_call(
        body, name="ada_bwd",
        grid=(DEPTH,),
        in_specs=[pl.BlockSpec((D, NDEV), lambda l: (0, 0)), pl.BlockSpec((None, NDEV, AC), lambda l: (l, 0, 0))],
        out_specs=pl.BlockSpec((None, None, D, AC), lambda l: (0, l, 0, 0)),
        out_shape=jax.ShapeDtypeStruct((1, DEPTH, D, AC), F32),
        compiler_params=_cparams("arbitrary"),
    )(sc_t, dmod_cols)


def _row_tile(rows, cols):
    if rows * cols <= 512 * 1024:
        return rows
    for tr in (512, 384, 352, 256, 128, 64, 32, 16, 8):
        if rows % tr == 0:
            return tr
    return rows


def adamw(gparts, slot0, w, m, v, name):
    P, _, R, C = gparts.shape
    L = w.shape[0]
    tr = _row_tile(R, C)

    def body(g_ref, w_ref, m_ref, v_ref, go_ref, do_ref, mo_ref, vo_ref):
        g = g_ref[0].astype(F32)
        for p in range(1, P):
            g = g + g_ref[p].astype(F32)
        go_ref[...] = g
        mn = ADAM_B1 * m_ref[...] + (1.0 - ADAM_B1) * g
        vn = ADAM_B2 * v_ref[...] + (1.0 - ADAM_B2) * (g * g)
        mo_ref[...] = mn
        vo_ref[...] = vn
        m_hat = mn / (1.0 - ADAM_B1 ** ADAM_STEP)
        v_hat = vn / (1.0 - ADAM_B2 ** ADAM_STEP)
        do_ref[...] = -ADAM_LR * (m_hat / (jnp.sqrt(v_hat) + ADAM_EPS) + ADAM_WD * w_ref[...])

    blk = pl.BlockSpec((None, tr, C), lambda l, i: (l, i, 0))
    return pl.pallas_call(
        body, name=name,
        grid=(L, R // tr),
        in_specs=[pl.BlockSpec((P, None, tr, C), lambda l, i: (0, slot0 + l, i, 0)), blk, blk, blk],
        out_specs=[blk, blk, blk, blk],
        out_shape=[jax.ShapeDtypeStruct((L, R, C), F32)] * 4,
        compiler_params=_cparams("arbitrary", "arbitrary"),
    )(gparts, w, m, v)


def sum_parts(parts):
    P, R, C = parts.shape

    def body(p_ref, o_ref):
        acc = p_ref[0]
        for p in range(1, P):
            acc = acc + p_ref[p]
        o_ref[...] = acc

    return pl.pallas_call(
        body, name="sum_parts",
        in_specs=[pl.BlockSpec(memory_space=pltpu.VMEM)],
        out_specs=pl.BlockSpec(memory_space=pltpu.VMEM),
        out_shape=jax.ShapeDtypeStruct((R, C), F32),
    )(parts)


WEIGHTS = ['w_ada', 'b_ada', 'ffn1_norm', 'ffn1_w_gu', 'ffn1_w_down', 'mix_norm', 'w_in', 'conv_w', 'conv_b',
           'gate_a_w', 'gate_a_b', 'gate_x_w', 'gate_x_b', 'lru_lambda', 'v_norm', 'spatial_w', 'spatial_b',
           'lru_out_norm', 'gmlp_out_norm', 'w_out', 'ffn2_norm', 'ffn2_w_gu', 'ffn2_w_down', 'final_norm']
PACKED = ['b_ada', 'ffn1_norm', 'mix_norm', 'conv_b', 'gate_a_w', 'gate_a_b', 'gate_x_w', 'gate_x_b', 'lru_lambda',
          'v_norm', 'spatial_w', 'spatial_b', 'lru_out_norm', 'gmlp_out_norm', 'ffn2_norm', 'final_norm', 'conv_w']
PACK_LANES = 128
PACK_ROW_ALIGN = 8 * NDEV


def _pack(d):
    flat = jnp.concatenate([d[k].reshape(-1).astype(F32) for k in PACKED])
    rows = -(-flat.shape[0] // (PACK_LANES * PACK_ROW_ALIGN)) * PACK_ROW_ALIGN
    flat = jnp.concatenate([flat, jnp.zeros((rows * PACK_LANES - flat.shape[0],), F32)])
    return flat.reshape(rows, PACK_LANES)


def _unpack(buf, shapes):
    flat = buf.reshape(-1)
    out, off = {}, 0
    for k in PACKED:
        size = 1
        for s in shapes[k]:
            size *= s
        out[k] = flat[off:off + size].reshape(shapes[k])
        off += size
    return out


def kernel(x, c, w_ada, b_ada, ffn1_norm, ffn1_w_gu, ffn1_w_down, mix_norm, w_in, conv_w, conv_b, gate_a_w, gate_a_b, gate_x_w, gate_x_b, lru_lambda, v_norm, spatial_w, spatial_b, lru_out_norm, gmlp_out_norm, w_out, ffn2_norm, ffn2_w_gu, ffn2_w_down, final_norm, loss_target, m_w_ada, m_b_ada, m_ffn1_norm, m_ffn1_w_gu, m_ffn1_w_down, m_mix_norm, m_w_in, m_conv_w, m_conv_b, m_gate_a_w, m_gate_a_b, m_gate_x_w, m_gate_x_b, m_lru_lambda, m_v_norm, m_spatial_w, m_spatial_b, m_lru_out_norm, m_gmlp_out_norm, m_w_out, m_ffn2_norm, m_ffn2_w_gu, m_ffn2_w_down, m_final_norm, v_w_ada, v_b_ada, v_ffn1_norm, v_ffn1_w_gu, v_ffn1_w_down, v_mix_norm, v_w_in, v_conv_w, v_conv_b, v_gate_a_w, v_gate_a_b, v_gate_x_w, v_gate_x_b, v_lru_lambda, v_v_norm, v_spatial_w, v_spatial_b, v_lru_out_norm, v_gmlp_out_norm, v_w_out, v_ffn2_norm, v_ffn2_w_gu, v_ffn2_w_down, v_final_norm):
    w = dict(w_ada=w_ada, b_ada=b_ada, ffn1_norm=ffn1_norm, ffn1_w_gu=ffn1_w_gu, ffn1_w_down=ffn1_w_down, mix_norm=mix_norm, w_in=w_in, conv_w=conv_w, conv_b=conv_b, gate_a_w=gate_a_w, gate_a_b=gate_a_b, gate_x_w=gate_x_w, gate_x_b=gate_x_b, lru_lambda=lru_lambda, v_norm=v_norm, spatial_w=spatial_w, spatial_b=spatial_b, lru_out_norm=lru_out_norm, gmlp_out_norm=gmlp_out_norm, w_out=w_out, ffn2_norm=ffn2_norm, ffn2_w_gu=ffn2_w_gu, ffn2_w_down=ffn2_w_down, final_norm=final_norm)
    m = dict(w_ada=m_w_ada, b_ada=m_b_ada, ffn1_norm=m_ffn1_norm, ffn1_w_gu=m_ffn1_w_gu, ffn1_w_down=m_ffn1_w_down, mix_norm=m_mix_norm, w_in=m_w_in, conv_w=m_conv_w, conv_b=m_conv_b, gate_a_w=m_gate_a_w, gate_a_b=m_gate_a_b, gate_x_w=m_gate_x_w, gate_x_b=m_gate_x_b, lru_lambda=m_lru_lambda, v_norm=m_v_norm, spatial_w=m_spatial_w, spatial_b=m_spatial_b, lru_out_norm=m_lru_out_norm, gmlp_out_norm=m_gmlp_out_norm, w_out=m_w_out, ffn2_norm=m_ffn2_norm, ffn2_w_gu=m_ffn2_w_gu, ffn2_w_down=m_ffn2_w_down, final_norm=m_final_norm)
    v = dict(w_ada=v_w_ada, b_ada=v_b_ada, ffn1_norm=v_ffn1_norm, ffn1_w_gu=v_ffn1_w_gu, ffn1_w_down=v_ffn1_w_down, mix_norm=v_mix_norm, w_in=v_w_in, conv_w=v_conv_w, conv_b=v_conv_b, gate_a_w=v_gate_a_w, gate_a_b=v_gate_a_b, gate_x_w=v_gate_x_w, gate_x_b=v_gate_x_b, lru_lambda=v_lru_lambda, v_norm=v_v_norm, spatial_w=v_spatial_w, spatial_b=v_spatial_b, lru_out_norm=v_lru_out_norm, gmlp_out_norm=v_gmlp_out_norm, w_out=v_w_out, ffn2_norm=v_ffn2_norm, ffn2_w_gu=v_ffn2_w_gu, ffn2_w_down=v_ffn2_w_down, final_norm=v_final_norm)
    me = 4 * lax.axis_index("x") + 2 * lax.axis_index("y") + lax.axis_index("c")

    loc = dict(gu=jnp.concatenate([ffn1_w_gu, ffn2_w_gu], axis=0).astype(BF16),
               down=jnp.concatenate([ffn1_w_down, ffn2_w_down], axis=0).astype(BF16),
               w_in=w_in.astype(BF16), w_out=w_out.astype(BF16))
    c_g, conv_g, gu0, down0 = all_gather([(c, None), (conv_w, None), (loc['gu'], 0), (loc['down'], 0)], "gather_first")
    conv_w_full = conv_g.transpose(1, 2, 0, 3).reshape(DEPTH, CONV_WIDTH, LW)

    b_loc = lax.dynamic_slice(b_ada, (0, me * AC), (DEPTH, AC)).reshape(DEPTH, 1, AC)
    mod_cols, sc_all = ada_fwd(c_g.reshape(NDEV, D), w_ada, b_loc)
    (mod_rows,) = all_to_all([mod_cols.transpose(1, 0, 2)], "scatter_mod")
    mod = mod_rows.transpose(1, 0, 2).reshape(DEPTH, NMOD, D)

    small_w = {k: w[k] for k in PACKED if k != 'conv_w'}
    me_arr = jnp.reshape(me, (1,)).astype(jnp.int32)
    loss_loc, dx, big, small_g, dmod = local_fwd_bwd(me_arr, x[0], loss_target[0], mod, small_w, loc,
                                                     {('gu', 0): gu0, ('down', 0): down0}, conv_w_full)
    loss = lax.psum(loss_loc, ("x", "y", "c"))

    small_g['b_ada'] = dmod.reshape(DEPTH, NMOD * D)
    gpack = _pack(small_g)
    rows = gpack.shape[0]
    dmod_out = dmod.reshape(DEPTH, NDEV, AC).transpose(1, 0, 2)
    dmod_r, pack_r = all_to_all([dmod_out, gpack.reshape(NDEV, rows // NDEV, PACK_LANES)], "scatter_grads")
    (gsum_g,) = all_gather([(sum_parts(pack_r), None)], "gather_small_grads")
    gsum = gsum_g.reshape(1, 1, rows, PACK_LANES)

    res = {}
    t = lambda a: a.transpose(0, 2, 1)
    res['ffn1_w_gu'] = tuple(t(r) for r in adamw(big['gu'], 0, t(w['ffn1_w_gu']), t(m['ffn1_w_gu']), t(v['ffn1_w_gu']),
                                                 "adamw_gu_a"))
    res['ffn2_w_gu'] = tuple(t(r) for r in adamw(big['gu'], DEPTH, t(w['ffn2_w_gu']), t(m['ffn2_w_gu']),
                                                 t(v['ffn2_w_gu']), "adamw_gu_b"))
    res['ffn1_w_down'] = adamw(big['down'], 0, w['ffn1_w_down'], m['ffn1_w_down'], v['ffn1_w_down'], "adamw_down_a")
    res['ffn2_w_down'] = adamw(big['down'], DEPTH, w['ffn2_w_down'], m['ffn2_w_down'], v['ffn2_w_down'], "adamw_down_b")
    res['w_in'] = adamw(big['w_in'], 0, w['w_in'], m['w_in'], v['w_in'], "adamw_w_in")
    res['w_out'] = adamw(big['w_out'], 0, w['w_out'], m['w_out'], v['w_out'], "adamw_w_out")
    g_ada = ada_bwd(sc_all.T, dmod_r.transpose(1, 0, 2))
    res['w_ada'] = adamw(g_ada, 0, w['w_ada'], m['w_ada'], v['w_ada'], "adamw_w_ada")
    shapes = {k: w[k].shape for k in PACKED}
    shapes['conv_w'] = (DEPTH, CONV_WIDTH, LW)
    dummy = jnp.zeros(shapes['conv_w'], F32)
    packs = adamw(gsum, 0, _pack({**small_w, 'conv_w': dummy})[None], _pack({**{k: m[k] for k in small_w}, 'conv_w': dummy})[None],
                  _pack({**{k: v[k] for k in small_w}, 'conv_w': dummy})[None], "adamw_small")
    unpacked = [_unpack(b[0], shapes) for b in packs]
    for k in small_w:
        res[k] = tuple(u[k] for u in unpacked)
    gconv = lax.dynamic_slice(unpacked[0]['conv_w'], (0, 0, me * (LW // NDEV)), (DEPTH, CONV_WIDTH, LW // NDEV))
    cshape = (1, DEPTH * CONV_WIDTH, LW // NDEV)
    rc = adamw(gconv.reshape((1,) + cshape), 0, conv_w.reshape(cshape), m['conv_w'].reshape(cshape),
               v['conv_w'].reshape(cshape), "adamw_conv_w")
    res['conv_w'] = tuple(r.reshape(conv_w.shape) for r in rc)

    return (loss, dx[None], *[res[k][0] for k in WEIGHTS], *[res[k][1] for k in WEIGHTS],
            *[res[k][2] for k in WEIGHTS], *[res[k][3] for k in WEIGHTS])
```

```python
import jax
import jax.numpy as jnp
from jax import lax
from jax.experimental import pallas as pl
from jax.experimental.pallas import tpu as pltpu

F32 = jnp.float32
BF16 = jnp.bfloat16

NDEV = 8
DEPTH = 2
D = 1024
DFF = 2816
FC = 2 * DFF // NDEV
NCHUNK = DFF // FC
DR = DFF // NDEV
LW = 512
GW = 512
HD = 64
HEADS = 8
CHUNK = 128
PC = 2 * (LW + GW) // NDEV
OR = D // NDEV
NMOD = 9
AC = NMOD * D // NDEV
LC = 128
EPS = 1e-6
RG_LRU_C = 8.0
CONV_WIDTH = 4

ADAM_LR = 0.001
ADAM_B1 = 0.9
ADAM_B2 = 0.999
ADAM_EPS = 1e-08
ADAM_WD = 0.01
ADAM_STEP = 10

VMEM_LIMIT_BYTES = 60 * 1024 * 1024
MESH = pl.DeviceIdType.MESH
ANY = pl.BlockSpec(memory_space=pl.ANY)


def _cparams(*sem):
    return pltpu.CompilerParams(dimension_semantics=tuple(sem) if sem else None,
                                vmem_limit_bytes=VMEM_LIMIT_BYTES)


def _dot(a, b):
    return jnp.dot(a, b, preferred_element_type=F32)


def _dot_nt(a, b):
    return lax.dot_general(a, b, (((1,), (1,)), ((), ())), preferred_element_type=F32)


def _dot_tn(a, b):
    return lax.dot_general(a, b, (((0,), (0,)), ((), ())), preferred_element_type=F32)


def _split(a):
    hi = a.astype(BF16)
    lo = (a - hi.astype(F32)).astype(BF16)
    return hi, lo


def _dot3(a, b):
    ah, al = _split(a)
    bh, bl = _split(b)
    return _dot(ah, bh) + (_dot(ah, bl) + _dot(al, bh))


def _csum(a):
    return jnp.sum(a, axis=0, keepdims=True)


def _rmean(a):
    return jnp.mean(a, axis=-1, keepdims=True)


def _sigmoid(a):
    return 1.0 / (1.0 + jnp.exp(-a))


_GELU_K = 0.7978845608028654
_GELU_C = 0.044715


def _gelu(a):
    return 0.5 * a * (1.0 + jnp.tanh(_GELU_K * (a + _GELU_C * a * a * a)))


def _gelu_grad(a):
    t = jnp.tanh(_GELU_K * (a + _GELU_C * a * a * a))
    return 0.5 * (1.0 + t) + 0.5 * a * (1.0 - t * t) * (_GELU_K * (1.0 + 3.0 * _GELU_C * a * a))


def _norm_mod(x, gain, scale, shift):
    rstd = lax.rsqrt(_rmean(x * x) + EPS)
    return (x * rstd * gain) * (1.0 + scale) + shift


def _norm_mod_bwd(dh, x, gain, scale):
    rstd = lax.rsqrt(_rmean(x * x) + EPS)
    xhat = x * rstd
    dshift = _csum(dh)
    dscale = _csum(dh * (xhat * gain))
    dhn = dh * (1.0 + scale)
    dgain = _csum(dhn * xhat)
    dxhat = dhn * gain
    dx = rstd * (dxhat - xhat * _rmean(dxhat * xhat))
    return dx, dshift, dscale, dgain


def _rms(x, gain):
    rstd = lax.rsqrt(_rmean(x * x) + EPS)
    return x * rstd * gain


def _rms_bwd(dy, x, gain):
    rstd = lax.rsqrt(_rmean(x * x) + EPS)
    xhat = x * rstd
    dgain = _csum(dy * xhat)
    dxhat = dy * gain
    return rstd * (dxhat - xhat * _rmean(dxhat * xhat)), dgain


PAIR = 2 * HD


def _seg_mean(a, pavg):
    hi, lo = _split(a)
    return jnp.concatenate([_dot(hi[:, p:p + PAIR], pavg) + _dot(lo[:, p:p + PAIR], pavg)
                            for p in range(0, a.shape[1], PAIR)], axis=1)


def _block_copies(src_hbm, dst_vmem, sems, rows):
    copies = []
    for k in range(NDEV):
        dst = dst_vmem.at[k] if rows is None else dst_vmem.at[pl.ds(k * rows, rows)]
        copies.append(pltpu.make_async_copy(src_hbm.at[k], dst, sems.at[k]))
    return copies


def _ffn_weight_fetch(wgu_hbm, wd_hbm, wgu_v, wd_v, sems):
    @pl.when(pl.program_id(0) == 0)
    def _():
        copies = _block_copies(wgu_hbm, wgu_v, sems.at[0], None) + _block_copies(wd_hbm, wd_v, sems.at[1], DR)
        for cp in copies:
            cp.start()
        for cp in copies:
            cp.wait()


def _place():
    return lax.axis_index("x"), lax.axis_index("y"), lax.axis_index("c")


def _slot(p):
    return 4 * p[0] + 2 * p[1] + p[2]


class GatherRide:
    def __init__(self, srcs):
        self.n = len(srcs)
        self.index = [i for _, i in srcs]
        self.args = [a for a, _ in srcs]
        self.out_shape = [jax.ShapeDtypeStruct((NDEV,) + (a.shape if i is None else a.shape[1:]), a.dtype)
                          for a, i in srcs]
        self.scratch = [pltpu.SemaphoreType.DMA((self.n, NDEV - 1)), pltpu.SemaphoreType.DMA((self.n, NDEV - 1)),
                        pltpu.SemaphoreType.DMA((self.n,))]

    def hooks(self, ins, outs, sems):
        send_sems, recv_sems, local_sems = sems
        n = self.n
        x, y, c = _place()
        me, sibling = (x, y, c), (x, y, 1 - c)
        chips = [(1 - x, y), (x, 1 - y), (1 - x, 1 - y)]

        def local(a):
            return ins[a] if self.index[a] is None else ins[a].at[self.index[a]]

        def copy(a, k, block, to, src=None):
            dst = outs[a].at[_slot(block)]
            return pltpu.make_async_remote_copy(
                src_ref=dst if src is None else src, dst_ref=dst,
                send_sem=send_sems.at[a, k], recv_sem=recv_sems.at[a, k],
                device_id=to, device_id_type=MESH)

        def mine():
            return [pltpu.make_async_copy(local(a), outs[a].at[_slot(me)], local_sems.at[a]) for a in range(n)]

        def first():
            cps = []
            for a in range(n):
                cps.append(copy(a, 0, me, sibling, src=local(a)))
                cps += [copy(a, 1 + j, me, (*chip, c), src=local(a)) for j, chip in enumerate(chips)]
            return cps

        def passed():
            return [copy(a, 4 + j, (*chip, c), sibling) for j, chip in enumerate(chips) for a in range(n)]

        def start():
            for cp in mine() + first():
                cp.start()

        def mid():
            for j, chip in enumerate(chips):
                for a in range(n):
                    copy(a, 1 + j, (*chip, c), me).wait_recv()
                    copy(a, 4 + j, (*chip, c), sibling).start()

        def finish():
            for a in range(n):
                copy(a, 0, sibling, me).wait_recv()
                for j, chip in enumerate(chips):
                    copy(a, 4 + j, (*chip, 1 - c), me).wait_recv()
            for cp in first() + passed():
                cp.wait_send()
            for cp in mine():
                cp.wait()

        return start, mid, finish


def all_gather(srcs, name):
    ride = GatherRide(srcs)
    n = ride.n

    def body(*refs):
        start, mid, finish = ride.hooks(refs[:n], refs[n:2 * n], refs[2 * n:])
        start()
        mid()
        finish()

    return pl.pallas_call(
        body, name=name,
        in_specs=[ANY] * n, out_specs=[ANY] * n, out_shape=ride.out_shape, scratch_shapes=ride.scratch,
    )(*ride.args)


def _call(core, ride, *, name, grid, in_specs, out_specs, out_shape, scratch_shapes, args):
    if ride is None:
        outs = pl.pallas_call(core, name=name, grid=grid, in_specs=in_specs, out_specs=out_specs,
                              out_shape=out_shape, scratch_shapes=scratch_shapes,
                              compiler_params=_cparams("arbitrary"))(*args)
        return outs, []
    n_in, n_out, n_sc, n = len(in_specs), len(out_shape), len(scratch_shapes), ride.n
    nsteps = grid[0]
    mid_step = max(nsteps - 2, 0)

    def body(*refs):
        cuts = [n_in, n_in + n, n_in + n + n_out, n_in + 2 * n + n_out, n_in + 2 * n + n_out + n_sc]
        ci, ri, co, ro, cs, rs = (refs[a:b] for a, b in zip([0] + cuts, cuts + [len(refs)]))
        start, mid, finish = ride.hooks(ri, ro, rs)
        i = pl.program_id(0)
        pl.when(i == 0)(start)
        core(*ci, *co, *cs)
        pl.when(i == mid_step)(mid)
        pl.when(i == nsteps - 1)(finish)

    outs = pl.pallas_call(
        body, name=name, grid=grid,
        in_specs=list(in_specs) + [ANY] * n, out_specs=list(out_specs) + [ANY] * n,
        out_shape=list(out_shape) + ride.out_shape, scratch_shapes=list(scratch_shapes) + ride.scratch,
        compiler_params=_cparams("arbitrary"))(*args, *ride.args)
    return outs[:n_out], outs[n_out:]


def all_to_all(arrs, name):
    n = len(arrs)

    def body(*refs):
        ins, outs = refs[:n], refs[n:2 * n]
        send_sems, recv_sems, local_sems = refs[2 * n:]
        x, y, c = _place()
        me = (x, y, c)

        def peer(k):
            return (1 - x if k & 4 else x, 1 - y if k & 2 else y, 1 - c if k & 1 else c)

        def copy(a, k):
            return pltpu.make_async_remote_copy(
                src_ref=ins[a].at[_slot(peer(k))], dst_ref=outs[a].at[_slot(me)],
                send_sem=send_sems.at[a, k - 1], recv_sem=recv_sems.at[a, k - 1],
                device_id=peer(k), device_id_type=MESH)

        def landing(a, k):
            return pltpu.make_async_remote_copy(
                src_ref=outs[a].at[_slot(peer(k))], dst_ref=outs[a].at[_slot(peer(k))],
                send_sem=send_sems.at[a, k - 1], recv_sem=recv_sems.at[a, k - 1],
                device_id=me, device_id_type=MESH)

        mine = [pltpu.make_async_copy(ins[a].at[_slot(me)], outs[a].at[_slot(me)], local_sems.at[a]) for a in range(n)]
        for cp in mine:
            cp.start()
        sends = [copy(a, k) for a in range(n) for k in range(1, NDEV)]
        for cp in sends:
            cp.start()
        for a in range(n):
            for k in range(1, NDEV):
                landing(a, k).wait_recv()
        for cp in sends:
            cp.wait_send()
        for cp in mine:
            cp.wait()

    return pl.pallas_call(
        body, name=name,
        in_specs=[ANY] * n, out_specs=[ANY] * n,
        out_shape=[jax.ShapeDtypeStruct(a.shape, a.dtype) for a in arrs],
        scratch_shapes=[pltpu.SemaphoreType.DMA((n, NDEV - 1)), pltpu.SemaphoreType.DMA((n, NDEV - 1)),
                        pltpu.SemaphoreType.DMA((n,))],
    )(*arrs)


FFN_TS = 256


def ffn_fwd(x, vec, wgu_g, wdown_g, tag, ride=None):
    S = x.shape[0]
    ts = min(FFN_TS, S)

    def body(x_ref, vec_ref, wgu_hbm, wd_hbm, xo_ref, ht_ref, gu_ref, f_ref, wgu_v, wd_v, sems):
        _ffn_weight_fetch(wgu_hbm, wd_hbm, wgu_v, wd_v, sems)
        xv = x_ref[...]
        h = _norm_mod(xv, vec_ref[3:4, :], vec_ref[1:2, :], vec_ref[0:1, :]).astype(BF16)
        ht_ref[...] = h.T
        acc = jnp.zeros((ts, D), F32)
        for j in range(NCHUNK):
            g = _dot(h, wgu_v[j])
            u = _dot(h, wgu_v[NCHUNK + j])
            gu_ref[j] = g.astype(BF16)
            gu_ref[NCHUNK + j] = u.astype(BF16)
            a = (g * _sigmoid(g) * u).astype(BF16)
            acc = acc + _dot(a, wd_v[pl.ds(j * FC, FC), :])
        f_ref[...] = acc.astype(BF16)
        xo_ref[...] = xv + (0.5 * vec_ref[2:3, :]) * acc

    return _call(
        body, ride, name=f"ffn_fwd_{tag}",
        grid=(S // ts,),
        in_specs=[pl.BlockSpec((ts, D), lambda i: (i, 0)),
                  pl.BlockSpec((8, D), lambda i: (0, 0)), ANY, ANY],
        out_specs=[pl.BlockSpec((ts, D), lambda i: (i, 0)),
                   pl.BlockSpec((D, ts), lambda i: (0, i)),
                   pl.BlockSpec((NDEV, ts, FC), lambda i: (0, i, 0)),
                   pl.BlockSpec((ts, D), lambda i: (i, 0))],
        out_shape=[jax.ShapeDtypeStruct((S, D), F32), jax.ShapeDtypeStruct((D, S), BF16),
                   jax.ShapeDtypeStruct((NDEV, S, FC), BF16), jax.ShapeDtypeStruct((S, D), BF16)],
        scratch_shapes=[pltpu.VMEM((NDEV, D, FC), BF16), pltpu.VMEM((DFF, D), BF16),
                        pltpu.SemaphoreType.DMA((2, NDEV))],
        args=(x, vec, wgu_g, wdown_g))


def ffn_bwd(dxo, x, gu, f, vec, wgu_g, wdown_g, tag):
    S = x.shape[0]
    ts = min(FFN_TS, S)

    def body(dxo_ref, x_ref, gu_ref, f_ref, vec_ref, wgu_hbm, wd_hbm,
             dx_ref, dgu_ref, a_ref, df_ref, acc_ref, wgu_v, wd_v, sems):
        _ffn_weight_fetch(wgu_hbm, wd_hbm, wgu_v, wd_v, sems)

        @pl.when(pl.program_id(0) == 0)
        def _():
            acc_ref[...] = jnp.zeros_like(acc_ref)

        dxo_v = dxo_ref[...]
        dgate = 0.5 * _csum(dxo_v * f_ref[...].astype(F32))
        df = ((0.5 * vec_ref[2:3, :]) * dxo_v).astype(BF16)
        df_ref[...] = df
        dh = jnp.zeros((ts, D), F32)
        for j in range(NCHUNK):
            da = _dot_nt(df, wd_v[pl.ds(j * FC, FC), :])
            g = gu_ref[j].astype(F32)
            u = gu_ref[NCHUNK + j].astype(F32)
            sg = _sigmoid(g)
            si = g * sg
            a_ref[j] = (si * u).astype(BF16)
            dg = (da * u * (sg * (1.0 + g * (1.0 - sg)))).astype(BF16)
            du = (da * si).astype(BF16)
            dgu_ref[j] = dg
            dgu_ref[NCHUNK + j] = du
            dh = dh + _dot_nt(dg, wgu_v[j]) + _dot_nt(du, wgu_v[NCHUNK + j])
        dx, dshift, dscale, dgain = _norm_mod_bwd(dh, x_ref[...], vec_ref[3:4, :], vec_ref[1:2, :])
        dx_ref[...] = dx + dxo_v
        acc_ref[0:1, :] += dshift
        acc_ref[1:2, :] += dscale
        acc_ref[2:3, :] += dgate
        acc_ref[3:4, :] += dgain

    row = pl.BlockSpec((ts, D), lambda i: (i, 0))
    return pl.pallas_call(
        body, name=f"ffn_bwd_{tag}",
        grid=(S // ts,),
        in_specs=[row, row, pl.BlockSpec((NDEV, ts, FC), lambda i: (0, i, 0)), row,
                  pl.BlockSpec((8, D), lambda i: (0, 0)), ANY, ANY],
        out_specs=[row, pl.BlockSpec((NDEV, ts, FC), lambda i: (0, i, 0)),
                   pl.BlockSpec((NCHUNK, ts, FC), lambda i: (0, i, 0)), row,
                   pl.BlockSpec((8, D), lambda i: (0, 0))],
        out_shape=[jax.ShapeDtypeStruct((S, D), F32), jax.ShapeDtypeStruct((NDEV, S, FC), BF16),
                   jax.ShapeDtypeStruct((NCHUNK, S, FC), BF16), jax.ShapeDtypeStruct((S, D), BF16),
                   jax.ShapeDtypeStruct((8, D), F32)],
        scratch_shapes=[pltpu.VMEM((NDEV, D, FC), BF16), pltpu.VMEM((DFF, D), BF16),
                        pltpu.SemaphoreType.DMA((2, NDEV))],
        compiler_params=_cparams("arbitrary"),
    )(dxo, x, gu, f, vec, wgu_g, wdown_g)


NCHIP = NDEV // 2


def tn_matmul_scatter(me_arr, a, b, slot, nslots, prev, name, split=1, a_transposed=False):
    if a_transposed:
        na, M, S = a.shape
    else:
        na, S, M = a.shape
    nb, _, N = b.shape
    contract = _dot if a_transposed else _dot_tn
    ncall = NDEV // split
    ts = min(4096, S)
    nsteps = S // ts
    mp = M // split
    other_step = {1: lambda j: 2 * j, 2: lambda j: j, 8: lambda j: 0}[split]
    mine_step = {1: lambda j: 2 * j + 1, 2: lambda j: j, 8: lambda j: 0}[split]

    def group(k, me_ref):
        if split == 1:
            return jnp.bitwise_xor(me_ref[0], NDEV - 1 - k)
        if split == 2:
            return jnp.bitwise_xor(me_ref[0] // 2, NCHIP - 1 - k)
        return 0

    def body(me_ref, *refs):
        a_ref, b_ref = refs[0], refs[1]
        recv_ref, acc, sb_other, sb_mine, land, d2d_send, d2d_recv, ici_send, ici_recv = refs[-9:]
        k = pl.program_id(0)
        s = pl.program_id(1)
        x, y, c = _place()
        my_chip = 2 * x + y

        def chip_of(j):
            if split == 8:
                cx, cy = j // 2, j % 2
            else:
                flip = NCHIP - 1 - j
                cx, cy = (1 - x if flip & 2 else x), (1 - y if flip & 1 else y)
            return cx, cy, 2 * cx + cy

        def piece(j, core):
            if split == 1:
                return acc[...]
            start = core * mp if split == 2 else (2 * j + core) * mp
            return acc[pl.ds(pl.multiple_of(start, 8), mp), :]

        def to_sibling(j):
            return pltpu.make_async_remote_copy(
                src_ref=sb_other.at[j], dst_ref=land.at[j], send_sem=d2d_send.at[j], recv_sem=d2d_recv.at[j],
                device_id=(x, y, 1 - c), device_id_type=MESH)

        def to_owner(j):
            cx, cy, ci = chip_of(j)
            dst = recv_ref.at[my_chip, slot]
            return ci, pltpu.make_async_copy(sb_mine.at[j], dst, ici_send.at[j]), pltpu.make_async_remote_copy(
                src_ref=sb_mine.at[j], dst_ref=dst, send_sem=ici_send.at[j], recv_sem=ici_recv.at[my_chip],
                device_id=(cx, cy, c), device_id_type=MESH)

        if nsteps == 1:
            acc[...] = contract(a_ref[...], b_ref[...])
        else:
            @pl.when(s == 0)
            def _():
                acc[...] = jnp.zeros_like(acc)

            acc[...] += contract(a_ref[...], b_ref[...])

        for kk in range(ncall):
            @pl.when((s == nsteps - 1) & (k == kk))
            def _():
                for j in range(NCHIP):
                    if other_step(j) == kk:
                        sb_other[j] = piece(j, 1 - c).astype(BF16)
                        to_sibling(j).start()
                for j in range(NCHIP):
                    if mine_step(j) == kk:
                        to_sibling(j).wait_recv()
                        sb_mine[j] = (piece(j, c) + land[j].astype(F32)).astype(BF16)
                        ci, loc, rem = to_owner(j)
                        pl.when(ci == my_chip)(loc.start)
                        pl.when(ci != my_chip)(rem.start)

        @pl.when((s == nsteps - 1) & (k == ncall - 1))
        def _():
            for j in range(NCHIP):
                to_sibling(j).wait_send()
                ci, loc, rem = to_owner(j)
                pl.when(ci == my_chip)(loc.wait)
                pl.when(ci != my_chip)(rem.wait_send)
            for src in range(NCHIP):
                @pl.when(my_chip != src)
                def _():
                    pltpu.make_async_remote_copy(
                        src_ref=recv_ref.at[src, slot], dst_ref=recv_ref.at[src, slot],
                        send_sem=ici_send.at[src], recv_sem=ici_recv.at[src],
                        device_id=(src // 2, src % 2, c), device_id_type=MESH).wait_recv()

    if a_transposed:
        a_spec = pl.BlockSpec((None, M, ts), (lambda k, s, me: (group(k, me), 0, s)) if na > 1 else (lambda k, s, me: (0, 0, s)))
    else:
        a_spec = pl.BlockSpec((None, ts, M), (lambda k, s, me: (group(k, me), s, 0)) if na > 1 else (lambda k, s, me: (0, s, 0)))
    in_specs = [a_spec,
                pl.BlockSpec((None, ts, N), (lambda k, s, me: (group(k, me), s, 0)) if nb > 1 else (lambda k, s, me: (0, s, 0)))]
    args = [me_arr, a, b]
    aliases = {}
    if prev is not None:
        in_specs.append(ANY)
        args.append(prev)
        aliases = {3: 0}
    return pl.pallas_call(
        body, name=name,
        grid_spec=pltpu.PrefetchScalarGridSpec(
            num_scalar_prefetch=1, grid=(ncall, nsteps), in_specs=in_specs, out_specs=ANY,
            scratch_shapes=[pltpu.VMEM((M, N), F32), pltpu.VMEM((NCHIP, mp, N), BF16), pltpu.VMEM((NCHIP, mp, N), BF16),
                            pltpu.VMEM((NCHIP, mp, N), BF16), pltpu.SemaphoreType.DMA((NCHIP,)),
                            pltpu.SemaphoreType.DMA((NCHIP,)), pltpu.SemaphoreType.DMA((NCHIP,)),
                            pltpu.SemaphoreType.DMA((NCHIP,))]),
        out_shape=jax.ShapeDtypeStruct((NCHIP, nslots, mp, N), BF16),
        input_output_aliases=aliases,
        compiler_params=_cparams("arbitrary", "arbitrary"),
    )(*args)


MIX_TS = 256


def mix_in_fwd(x, vec, win_g, tag, ride=None):
    S = x.shape[0]
    ts = min(MIX_TS, S)

    def body(x_ref, vec_ref, win_ref, hmt_ref, proj_ref):
        h = _norm_mod(x_ref[...], vec_ref[3:4, :], vec_ref[1:2, :], vec_ref[0:1, :]).astype(BF16)
        hmt_ref[...] = h.T
        for k in range(NDEV):
            proj_ref[k] = _dot(h, win_ref[k])

    return _call(
        body, ride, name=f"mix_in_fwd_{tag}",
        grid=(S // ts,),
        in_specs=[pl.BlockSpec((ts, D), lambda i: (i, 0)), pl.BlockSpec((8, D), lambda i: (0, 0)),
                  pl.BlockSpec((NDEV, D, PC), lambda i: (0, 0, 0))],
        out_specs=[pl.BlockSpec((D, ts), lambda i: (0, i)),
                   pl.BlockSpec((NDEV, ts, PC), lambda i: (0, i, 0))],
        out_shape=[jax.ShapeDtypeStruct((D, S), BF16), jax.ShapeDtypeStruct((NDEV, S, PC), F32)],
        scratch_shapes=[], args=(x, vec, win_g))


def mix_in_bwd(dproj, x, dxo, vec, win_g, tag):
    S = x.shape[0]
    ts = min(MIX_TS, S)

    def body(dp_ref, x_ref, dxo_ref, vec_ref, win_ref, dx_ref, acc_ref):
        @pl.when(pl.program_id(0) == 0)
        def _():
            acc_ref[...] = jnp.zeros_like(acc_ref)

        dh = jnp.zeros((ts, D), F32)
        for k in range(NDEV):
            dh = dh + _dot_nt(dp_ref[k], win_ref[k])
        dx, dshift, dscale, dgain = _norm_mod_bwd(dh, x_ref[...], vec_ref[3:4, :], vec_ref[1:2, :])
        dx_ref[...] = dx + dxo_ref[...]
        acc_ref[0:1, :] += dshift
        acc_ref[1:2, :] += dscale
        acc_ref[3:4, :] += dgain

    row = pl.BlockSpec((ts, D), lambda i: (i, 0))
    return pl.pallas_call(
        body, name=f"mix_in_bwd_{tag}",
        grid=(S // ts,),
        in_specs=[pl.BlockSpec((NDEV, ts, PC), lambda i: (0, i, 0)), row, row,
                  pl.BlockSpec((8, D), lambda i: (0, 0)),
                  pl.BlockSpec((NDEV, D, PC), lambda i: (0, 0, 0))],
        out_specs=[row, pl.BlockSpec((8, D), lambda i: (0, 0))],
        out_shape=[jax.ShapeDtypeStruct((S, D), F32), jax.ShapeDtypeStruct((8, D), F32)],
        compiler_params=_cparams("arbitrary"),
    )(dproj, x, dxo, vec, win_g)


SCAN_UNROLL = 4


def _shift_down(z, k, row):
    return jnp.where(row >= k, pltpu.roll(z, k, 0), 0.0)


def _shift_up(z, k, row, n):
    return jnp.where(row < n - k, pltpu.roll(z, n - k, 0), 0.0)


def _lru_gates(xc, lp_ref, wa_ref, wx_ref):
    xcb = xc.astype(BF16)
    ra = _sigmoid(_dot(xcb, wa_ref[...]) + lp_ref[5:6, :])
    ix = _sigmoid(_dot(xcb, wx_ref[...]) + lp_ref[6:7, :])
    lam = lp_ref[7:8, :]
    ls = jnp.minimum(lam, 0.0) - jnp.log(1.0 + jnp.exp(-jnp.abs(lam)))
    log_a = (RG_LRU_C * ls) * ra
    a = jnp.exp(log_a)
    mult = jnp.sqrt(-jnp.tanh(log_a) * (a * a + 1.0))
    return ra, ix, ls, a, mult


def _conv(x, lp_ref, row):
    return (lp_ref[4:5, :] + lp_ref[3:4, :] * x + lp_ref[2:3, :] * _shift_down(x, 1, row)
            + lp_ref[1:2, :] * _shift_down(x, 2, row) + lp_ref[0:1, :] * _shift_down(x, 3, row))


def lru_fwd(proj, lp, wa_t, wx_t, tag, ride=None):
    S = proj.shape[1]
    nblk = S // 8

    def body(x_ref, g_ref, lp_ref, wa_ref, wx_ref, y_ref, xc_ref, h_ref, a_s, b_s):
        x = x_ref[...]
        row = lax.broadcasted_iota(jnp.int32, x.shape, 0)
        xc = _conv(x, lp_ref, row)
        xc_ref[...] = xc
        ra, ix, ls, a, mult = _lru_gates(xc, lp_ref, wa_ref, wx_ref)
        a_s[...] = a
        b_s[...] = mult * (ix * xc)
        rowb = lax.broadcasted_iota(jnp.int32, (8, LC), 0)

        def step(i, carry):
            for q in range(SCAN_UNROLL):
                r0 = pl.multiple_of((i * SCAN_UNROLL + q) * 8, 8)
                A = a_s[pl.ds(r0, 8), :]
                B = b_s[pl.ds(r0, 8), :]
                for d in (1, 2, 4):
                    m = rowb >= d
                    As = jnp.where(m, pltpu.roll(A, d, 0), 1.0)
                    Bs = jnp.where(m, pltpu.roll(B, d, 0), 0.0)
                    B = A * Bs + B
                    A = A * As
                H = B + A * carry
                h_ref[pl.ds(r0, 8), :] = H
                carry = H[7:8, :]
            return carry

        lax.fori_loop(0, nblk // SCAN_UNROLL, step, jnp.zeros((1, LC), F32))
        y_ref[...] = h_ref[...] * _gelu(g_ref[...])

    col = pl.BlockSpec((S, LC), lambda c: (0, c))
    return _call(
        body, ride, name=f"lru_fwd_{tag}",
        grid=(LW // LC,),
        in_specs=[pl.BlockSpec((None, S, LC), lambda c: (c // 2, 0, c % 2)),
                  pl.BlockSpec((None, S, LC), lambda c: (2 + c // 2, 0, c % 2)),
                  pl.BlockSpec((8, LC), lambda c: (0, c)),
                  pl.BlockSpec((None, LC, LC), lambda c: (c, 0, 0)),
                  pl.BlockSpec((None, LC, LC), lambda c: (c, 0, 0))],
        out_specs=[col, col, col],
        out_shape=[jax.ShapeDtypeStruct((S, LW), F32)] * 3,
        scratch_shapes=[pltpu.VMEM((S, LC), F32), pltpu.VMEM((S, LC), F32)],
        args=(proj, proj, lp, wa_t, wx_t))


def lru_bwd(dy, proj, xc_all, hst, lp, wa_t, wx_t, tag):
    S = proj.shape[1]
    nblk = S // 8

    def body(dy_ref, x_ref, g_ref, xc_ref, h_ref, lp_ref, wa_ref, wx_ref,
             dx_ref, dg_ref, dlp_ref, dwa_ref, dwx_ref, c_s, l_s):
        xc = xc_ref[...]
        row = lax.broadcasted_iota(jnp.int32, xc.shape, 0)
        ra, ix, ls, a, mult = _lru_gates(xc, lp_ref, wa_ref, wx_ref)
        g = g_ref[...]
        dyv = dy_ref[...]
        h = h_ref[...]
        dg_ref[...] = (dyv * h * _gelu_grad(g)).astype(BF16)
        c_s[...] = _shift_up(a, 1, row, S)
        l_s[...] = dyv * _gelu(g)
        rowb = lax.broadcasted_iota(jnp.int32, (8, LC), 0)

        def step(i, carry):
            for q in range(SCAN_UNROLL):
                r0 = pl.multiple_of((nblk - 1 - (i * SCAN_UNROLL + q)) * 8, 8)
                C = c_s[pl.ds(r0, 8), :]
                L = l_s[pl.ds(r0, 8), :]
                for d in (1, 2, 4):
                    m = rowb < 8 - d
                    Cs = jnp.where(m, pltpu.roll(C, 8 - d, 0), 1.0)
                    Ls = jnp.where(m, pltpu.roll(L, 8 - d, 0), 0.0)
                    L = C * Ls + L
                    C = C * Cs
                L = L + C * carry
                l_s[pl.ds(r0, 8), :] = L
                carry = L[0:1, :]
            return carry

        lax.fori_loop(0, nblk // SCAN_UNROLL, step, jnp.zeros((1, LC), F32))
        db = l_s[...]
        da = db * _shift_down(h, 1, row)
        ixc = ix * xc
        dmult = db * ixc
        dix = db * (mult * xc)
        dxc = db * (mult * ix)
        dlog_a = da * a - dmult * (a * a) / mult
        dra = dlog_a * (RG_LRU_C * ls)
        dls = _csum(dlog_a * ra) * RG_LRU_C
        lam = lp_ref[7:8, :]
        dlam = dls * _sigmoid(-lam)
        dpa = dra * ra * (1.0 - ra)
        dpx = dix * ix * (1.0 - ix)
        dpab = dpa.astype(BF16)
        dpxb = dpx.astype(BF16)
        xcb = xc.astype(BF16)
        dwa_ref[...] = _dot_tn(xcb, dpab)
        dwx_ref[...] = _dot_tn(xcb, dpxb)
        dxc = dxc + _dot_nt(dpab, wa_ref[...]) + _dot_nt(dpxb, wx_ref[...])
        x = x_ref[...]
        dlp_ref[0:1, :] = _csum(dxc * _shift_down(x, 3, row))
        dlp_ref[1:2, :] = _csum(dxc * _shift_down(x, 2, row))
        dlp_ref[2:3, :] = _csum(dxc * _shift_down(x, 1, row))
        dlp_ref[3:4, :] = _csum(dxc * x)
        dlp_ref[4:5, :] = _csum(dxc)
        dlp_ref[5:6, :] = _csum(dpa)
        dlp_ref[6:7, :] = _csum(dpx)
        dlp_ref[7:8, :] = dlam
        dx = (lp_ref[3:4, :] * dxc + lp_ref[2:3, :] * _shift_up(dxc, 1, row, S)
              + lp_ref[1:2, :] * _shift_up(dxc, 2, row, S) + lp_ref[0:1, :] * _shift_up(dxc, 3, row, S))
        dx_ref[...] = dx.astype(BF16)

    col = pl.BlockSpec((S, LC), lambda c: (0, c))
    pcol = pl.BlockSpec((None, S, LC), lambda c: (c // 2, 0, c % 2))
    return pl.pallas_call(
        body, name=f"lru_bwd_{tag}",
        grid=(LW // LC,),
        in_specs=[col, pcol, pl.BlockSpec((None, S, LC), lambda c: (2 + c // 2, 0, c % 2)), col, col,
                  pl.BlockSpec((8, LC), lambda c: (0, c)),
                  pl.BlockSpec((None, LC, LC), lambda c: (c, 0, 0)),
                  pl.BlockSpec((None, LC, LC), lambda c: (c, 0, 0))],
        out_specs=[pcol, pcol, pl.BlockSpec((8, LC), lambda c: (0, c)),
                   pl.BlockSpec((None, LC, LC), lambda c: (c, 0, 0)),
                   pl.BlockSpec((None, LC, LC), lambda c: (c, 0, 0))],
        out_shape=[jax.ShapeDtypeStruct((2, S, PC), BF16), jax.ShapeDtypeStruct((2, S, PC), BF16),
                   jax.ShapeDtypeStruct((8, LW), F32),
                   jax.ShapeDtypeStruct((LW // LC, LC, LC), F32), jax.ShapeDtypeStruct((LW // LC, LC, LC), F32)],
        scratch_shapes=[pltpu.VMEM((S, LC), F32), pltpu.VMEM((S, LC), F32)],
        compiler_params=_cparams("arbitrary"),
    )(dy, proj, proj, xc_all, hst, lp, wa_t, wx_t)


def _pair_stack(zp, low):
    return jnp.concatenate([jnp.where(low, zp, 0.0), jnp.where(low, 0.0, zp)], axis=0).astype(BF16)


def _spatial(w_ref, zc, low):
    return jnp.concatenate(
        [_dot(w_ref[:, 2 * p * CHUNK:2 * (p + 1) * CHUNK], _pair_stack(zc[:, p * PAIR:(p + 1) * PAIR], low))
         for p in range(GW // PAIR)], axis=1)


def _gmlp_fwd_parts(u, v, gp_ref, wcat_ref, bz_ref, pavg_ref, ts):
    ug = _gelu(u)
    vg = _gelu(v)
    pavg = pavg_ref[...]
    vc = vg - _seg_mean(vg, pavg)
    rs = lax.rsqrt(_seg_mean(vc * vc, pavg) + EPS)
    vhat = vc * rs
    vh = vhat * gp_ref[0:1, :]
    low = lax.broadcasted_iota(jnp.int32, (CHUNK, PAIR), 1) < HD
    zs = [_spatial(wcat_ref, vh[n * CHUNK:(n + 1) * CHUNK, :], low) + bz_ref[...] for n in range(ts // CHUNK)]
    z = jnp.concatenate(zs, axis=0) if len(zs) > 1 else zs[0]
    return ug, rs, vhat, vh, z


def mix_out_fwd(proj, ylru, x, vec, gp, wcat, bz, pavg, wout_g, tag, ride=None):
    S = x.shape[0]
    ts = min(MIX_TS, S)

    def body(u_ref, v_ref, yl_ref, x_ref, vec_ref, gp_ref, wcat_ref, bz_ref, pavg_ref, wout_ref,
             xo_ref, yt_ref, fo_ref):
        u = jnp.concatenate([u_ref[0], u_ref[1]], axis=1)
        v = jnp.concatenate([v_ref[0], v_ref[1]], axis=1)
        ug, _, _, _, z = _gmlp_fwd_parts(u, v, gp_ref, wcat_ref, bz_ref, pavg_ref, ts)
        n1 = _rms(yl_ref[...], gp_ref[1:2, :])
        n2 = _rms(ug * z, gp_ref[2:3, :])
        y = jnp.concatenate([n1, n2], axis=1).astype(BF16)
        yt_ref[...] = y.T
        fo = jnp.zeros((ts, D), F32)
        for k in range(NDEV):
            fo = fo + _dot(y[:, k * OR:(k + 1) * OR], wout_ref[k])
        fo_ref[...] = fo.astype(BF16)
        xo_ref[...] = x_ref[...] + vec_ref[2:3, :] * fo

    row = pl.BlockSpec((ts, D), lambda i: (i, 0))
    full = lambda shp: pl.BlockSpec(shp, lambda i: tuple(0 for _ in shp))
    return _call(
        body, ride, name=f"mix_out_fwd_{tag}",
        grid=(S // ts,),
        in_specs=[pl.BlockSpec((2, ts, PC), lambda i: (2, i, 0)), pl.BlockSpec((2, ts, PC), lambda i: (3, i, 0)),
                  pl.BlockSpec((ts, LW), lambda i: (i, 0)), row, full((8, D)), full((8, GW)),
                  full((CHUNK, HEADS * CHUNK)), full((CHUNK, GW)), full((PAIR, PAIR)),
                  pl.BlockSpec((NDEV, OR, D), lambda i: (0, 0, 0))],
        out_specs=[row, pl.BlockSpec((D, ts), lambda i: (0, i)), row],
        out_shape=[jax.ShapeDtypeStruct((S, D), F32), jax.ShapeDtypeStruct((D, S), BF16),
                   jax.ShapeDtypeStruct((S, D), BF16)],
        scratch_shapes=[], args=(proj, proj, ylru, x, vec, gp, wcat, bz, pavg, wout_g))


def mix_out_bwd(dxo, proj, ylru, fo, vec, gp, wcat, wcat_t, bz, pavg, wout_g, tag):
    S = dxo.shape[0]
    ts = min(MIX_TS, S)

    def body(dxo_ref, u_ref, v_ref, yl_ref, fo_ref, vec_ref, gp_ref, wcat_ref, wcatt_ref, bz_ref, pavg_ref,
             wout_ref, dyo_ref, dyl_ref, duv_ref, acc_ref, dgp_ref, dwm_ref, dbz_ref):
        @pl.when(pl.program_id(0) == 0)
        def _():
            acc_ref[...] = jnp.zeros_like(acc_ref)
            dgp_ref[...] = jnp.zeros_like(dgp_ref)
            dwm_ref[...] = jnp.zeros_like(dwm_ref)
            dbz_ref[...] = jnp.zeros_like(dbz_ref)

        dxo_v = dxo_ref[...]
        acc_ref[2:3, :] += _csum(dxo_v * fo_ref[...].astype(F32))
        dyo = (vec_ref[2:3, :] * dxo_v).astype(BF16)
        dyo_ref[...] = dyo
        dn = [_dot_nt(dyo, wout_ref[k]) for k in range(NDEV)]
        dn1 = jnp.concatenate(dn[:NDEV // 2], axis=1)
        dn2 = jnp.concatenate(dn[NDEV // 2:], axis=1)
        dyl, dg1 = _rms_bwd(dn1, yl_ref[...], gp_ref[1:2, :])
        dyl_ref[...] = dyl
        u = jnp.concatenate([u_ref[0], u_ref[1]], axis=1)
        v = jnp.concatenate([v_ref[0], v_ref[1]], axis=1)
        ug, rs, vhat, vh, z = _gmlp_fwd_parts(u, v, gp_ref, wcat_ref, bz_ref, pavg_ref, ts)
        dyg, dg2 = _rms_bwd(dn2, ug * z, gp_ref[2:3, :])
        du = (dyg * z) * _gelu_grad(u)
        dz = dyg * ug
        low = lax.broadcasted_iota(jnp.int32, (CHUNK, PAIR), 1) < HD
        vhb = vh.astype(BF16)
        dvhs = []
        dbz = jnp.zeros((CHUNK, GW), F32)
        dwm = [jnp.zeros((2 * CHUNK, CHUNK), F32) for _ in range(GW // PAIR)]
        for n in range(ts // CHUNK):
            dzc = dz[n * CHUNK:(n + 1) * CHUNK, :]
            dbz = dbz + dzc
            for p in range(GW // PAIR):
                stack = _pair_stack(dzc[:, p * PAIR:(p + 1) * PAIR], low)
                dwm[p] = dwm[p] + _dot_nt(stack, vhb[n * CHUNK:(n + 1) * CHUNK, p * PAIR:(p + 1) * PAIR])
            dvhs.append(_spatial(wcatt_ref, dzc, low))
        dbz_ref[...] += dbz
        for p in range(GW // PAIR):
            dwm_ref[2 * p * CHUNK:2 * (p + 1) * CHUNK, :] += dwm[p]
        dvh = jnp.concatenate(dvhs, axis=0) if len(dvhs) > 1 else dvhs[0]
        pavg = pavg_ref[...]
        dvn = _csum(dvh * vhat)
        dvhat = dvh * gp_ref[0:1, :]
        dvg = rs * (dvhat - _seg_mean(dvhat, pavg) - vhat * _seg_mean(dvhat * vhat, pavg))
        dv = dvg * _gelu_grad(v)
        duv_ref[0] = du[:, :PC].astype(BF16)
        duv_ref[1] = du[:, PC:].astype(BF16)
        duv_ref[2] = dv[:, :PC].astype(BF16)
        duv_ref[3] = dv[:, PC:].astype(BF16)
        dgp_ref[0:1, :] += dvn
        dgp_ref[1:2, :] += dg1
        dgp_ref[2:3, :] += dg2

    row = pl.BlockSpec((ts, D), lambda i: (i, 0))
    full = lambda shp: pl.BlockSpec(shp, lambda i: tuple(0 for _ in shp))
    return pl.pallas_call(
        body, name=f"mix_out_bwd_{tag}",
        grid=(S // ts,),
        in_specs=[row, pl.BlockSpec((2, ts, PC), lambda i: (2, i, 0)), pl.BlockSpec((2, ts, PC), lambda i: (3, i, 0)),
                  pl.BlockSpec((ts, LW), lambda i: (i, 0)), row, full((8, D)), full((8, GW)),
                  full((CHUNK, HEADS * CHUNK)), full((CHUNK, HEADS * CHUNK)), full((CHUNK, GW)), full((PAIR, PAIR)),
                  pl.BlockSpec((NDEV, OR, D), lambda i: (0, 0, 0))],
        out_specs=[row, pl.BlockSpec((ts, LW), lambda i: (i, 0)), pl.BlockSpec((4, ts, PC), lambda i: (0, i, 0)),
                   full((8, D)), full((8, GW)), full((HEADS * CHUNK, CHUNK)), full((CHUNK, GW))],
        out_shape=[jax.ShapeDtypeStruct((S, D), BF16), jax.ShapeDtypeStruct((S, LW), F32),
                   jax.ShapeDtypeStruct((4, S, PC), BF16), jax.ShapeDtypeStruct((8, D), F32),
                   jax.ShapeDtypeStruct((8, GW), F32), jax.ShapeDtypeStruct((HEADS * CHUNK, CHUNK), F32),
                   jax.ShapeDtypeStruct((CHUNK, GW), F32)],
        compiler_params=_cparams("arbitrary"),
    )(dxo, proj, proj, ylru, fo, vec, gp, wcat, wcat_t, bz, pavg, wout_g)


def final_loss(x, target, gain):
    S = x.shape[0]
    ts = min(512, S)

    def body(x_ref, t_ref, g_ref, loss_ref, dx_ref, dg_ref):
        @pl.when(pl.program_id(0) == 0)
        def _():
            loss_ref[...] = jnp.zeros_like(loss_ref)
            dg_ref[...] = jnp.zeros_like(dg_ref)

        xv = x_ref[...]
        gain_v = g_ref[0:1, :]
        rstd = lax.rsqrt(_rmean(xv * xv) + EPS)
        xhat = xv * rstd
        err = xhat * gain_v - t_ref[...]
        loss_ref[...] += 0.5 * _csum(_rmean(err * err))
        dy = err * (1.0 / D)
        dg_ref[0:1, :] += _csum(dy * xhat)
        dxhat = dy * gain_v
        dx_ref[...] = rstd * (dxhat - xhat * _rmean(dxhat * xhat))

    row = pl.BlockSpec((ts, D), lambda i: (i, 0))
    return pl.pallas_call(
        body, name="final_loss",
        grid=(S // ts,),
        in_specs=[row, row, pl.BlockSpec((8, D), lambda i: (0, 0))],
        out_specs=[pl.BlockSpec((8, 128), lambda i: (0, 0)), row, pl.BlockSpec((8, D), lambda i: (0, 0))],
        out_shape=[jax.ShapeDtypeStruct((8, 128), F32), jax.ShapeDtypeStruct((S, D), F32),
                   jax.ShapeDtypeStruct((8, D), F32)],
        compiler_params=_cparams("arbitrary"),
    )(x, target, gain)


def _vec(mod_l, j, gain):
    return jnp.concatenate([mod_l[3 * j:3 * j + 3], gain[None, :], jnp.zeros((4, D), F32)], axis=0)


def _block_diag_tiles(w):
    w4 = w.reshape(LW // LC, 2, HD, HD)
    eye2 = jnp.eye(2, dtype=w.dtype)
    return (w4[:, :, :, None, :] * eye2[None, :, None, :, None]).reshape(LW // LC, LC, LC).astype(BF16)


def _block_diag_extract(dw):
    d5 = dw.reshape(LW // LC, 2, HD, 2, HD)
    return jnp.einsum('cihkj,ik->cihj', d5, jnp.eye(2, dtype=dw.dtype)).reshape(HEADS, HD, HD)


def _layer_params(l, p, conv_w_full):
    lp = jnp.concatenate([conv_w_full[l], p['conv_b'][l][None], p['gate_a_b'][l].reshape(1, LW),
                          p['gate_x_b'][l].reshape(1, LW), p['lru_lambda'][l][None]], axis=0)
    gp = jnp.concatenate([p['v_norm'][l][None], p['lru_out_norm'][l][None], p['gmlp_out_norm'][l][None],
                          jnp.zeros((5, GW), F32)], axis=0)
    ws = p['spatial_w'][l] * jnp.tril(jnp.ones((CHUNK, CHUNK), F32))
    wcat = ws.transpose(1, 0, 2).reshape(CHUNK, HEADS * CHUNK).astype(BF16)
    wcat_t = ws.transpose(2, 0, 1).reshape(CHUNK, HEADS * CHUNK).astype(BF16)
    bz = jnp.repeat(p['spatial_b'][l].T, HD, axis=1)
    return dict(lp=lp, gp=gp, wcat=wcat, wcat_t=wcat_t, bz=bz,
                wa_t=_block_diag_tiles(p['gate_a_w'][l]), wx_t=_block_diag_tiles(p['gate_x_w'][l]))


def _pavg():
    return jnp.kron(jnp.eye(2, dtype=F32), jnp.full((HD, HD), 1.0 / HD, F32)).astype(BF16)


GATHER_RIDES = {
    ('ffn_a', 0): [('w_in', 0), ('gu', DEPTH)],
    ('mix_in', 0): [('w_out', 0)],
    ('lru', 0): [('down', DEPTH)],
    ('mix_out', 0): [('down', 1)],
    ('ffn_b', 0): [('gu', 1), ('w_in', 1)],
    ('ffn_a', 1): [('gu', DEPTH + 1), ('w_out', 1)],
    ('mix_in', 1): [('down', DEPTH + 1)],
}


def local_fwd_bwd(me_arr, x, target, mod, p, loc, gathered, conv_w_full):
    pavg = _pavg()
    g = dict(gathered)

    def ride(call, l):
        todo = GATHER_RIDES.get((call, l))
        return None if todo is None else (todo, GatherRide([(loc[kind], slot) for kind, slot in todo]))

    def run(fn, call, l, *args):
        r = ride(call, l)
        outs, got = fn(*args, ride=None if r is None else r[1])
        if r is not None:
            g.update(dict(zip(r[0], got)))
        return outs

    saved = []
    h = x
    for l in range(DEPTH):
        q = _layer_params(l, p, conv_w_full)
        v1 = _vec(mod[l], 0, p['ffn1_norm'][l])
        vm = _vec(mod[l], 1, p['mix_norm'][l])
        v2 = _vec(mod[l], 2, p['ffn2_norm'][l])
        x0 = h
        x1, h1, gu1, f1 = run(ffn_fwd, 'ffn_a', l, x0, v1, g['gu', l], g['down', l], f"a{l}")
        hm, proj = run(mix_in_fwd, 'mix_in', l, x1, vm, g['w_in', l], f"{l}")
        ylru, xc, hst = run(lru_fwd, 'lru', l, proj, q['lp'], q['wa_t'], q['wx_t'], f"{l}")
        x2, y, fo = run(mix_out_fwd, 'mix_out', l, proj, ylru, x1, vm, q['gp'], q['wcat'], q['bz'], pavg,
                        g['w_out', l], f"{l}")
        x3, h2, gu2, f2 = run(ffn_fwd, 'ffn_b', l, x2, v2, g['gu', DEPTH + l], g['down', DEPTH + l], f"b{l}")
        saved.append(dict(q=q, v1=v1, vm=vm, v2=v2, x0=x0, x1=x1, x2=x2, h1=h1, gu1=gu1, f1=f1, hm=hm, proj=proj,
                          ylru=ylru, xc=xc, hst=hst, y=y, fo=fo, h2=h2, gu2=gu2, f2=f2))
        h = x3
    fin = jnp.concatenate([p['final_norm'][None], jnp.zeros((7, D), F32)], axis=0)
    loss8, dx, dfin = final_loss(h, target, fin)
    loss = loss8[0, 0]

    big = dict(gu=None, down=None, w_in=None, w_out=None)
    small = {k: [None] * DEPTH for k in ('ffn1_norm', 'mix_norm', 'ffn2_norm', 'conv_w', 'conv_b', 'gate_a_w',
                                         'gate_a_b', 'gate_x_w', 'gate_x_b', 'lru_lambda', 'v_norm', 'spatial_w',
                                         'spatial_b', 'lru_out_norm', 'gmlp_out_norm')}
    dmod = [None] * DEPTH
    tril = jnp.tril(jnp.ones((CHUNK, CHUNK), F32))
    for l in reversed(range(DEPTH)):
        sv = saved[l]
        q = sv['q']
        dx2, dgu, a, df, acc2 = ffn_bwd(dx, sv['x2'], sv['gu2'], sv['f2'], sv['v2'],
                                        g['gu', DEPTH + l], g['down', DEPTH + l], f"b{l}")
        big['gu'] = tn_matmul_scatter(me_arr, sv['h2'][None], dgu, DEPTH + l, 2 * DEPTH, big['gu'], f"dw_gu_b{l}",
                                      a_transposed=True)
        big['down'] = tn_matmul_scatter(me_arr, a, df[None], DEPTH + l, 2 * DEPTH, big['down'], f"dw_down_b{l}", split=2)
        dyo, dylru, duv, accmo, dgp, dwm, dbz = mix_out_bwd(dx2, sv['proj'], sv['ylru'], sv['fo'], sv['vm'], q['gp'],
                                                             q['wcat'], q['wcat_t'], q['bz'], pavg, g['w_out', l], f"{l}")
        big['w_out'] = tn_matmul_scatter(me_arr, sv['y'][None], dyo[None], l, DEPTH, big['w_out'], f"dw_out_{l}",
                                         split=NDEV, a_transposed=True)
        dxl, dgl, dlp, dwa, dwx = lru_bwd(dylru, sv['proj'], sv['xc'], sv['hst'], q['lp'], q['wa_t'], q['wx_t'], f"{l}")
        dproj = jnp.concatenate([dxl, dgl, duv], axis=0)
        dx1, accmi = mix_in_bwd(dproj, sv['x1'], dx2, sv['vm'], g['w_in', l], f"{l}")
        big['w_in'] = tn_matmul_scatter(me_arr, sv['hm'][None], dproj, l, DEPTH, big['w_in'], f"dw_in_{l}",
                                        a_transposed=True)
        dx0, dgu, a, df, acc1 = ffn_bwd(dx1, sv['x0'], sv['gu1'], sv['f1'], sv['v1'],
                                        g['gu', l], g['down', l], f"a{l}")
        big['gu'] = tn_matmul_scatter(me_arr, sv['h1'][None], dgu, l, 2 * DEPTH, big['gu'], f"dw_gu_a{l}",
                                      a_transposed=True)
        big['down'] = tn_matmul_scatter(me_arr, a, df[None], l, 2 * DEPTH, big['down'], f"dw_down_a{l}", split=2)
        dx = dx0
        dmod[l] = jnp.concatenate([acc1[0:3], accmi[0:2], accmo[2:3], acc2[0:3]], axis=0)
        small['ffn1_norm'][l] = acc1[3]
        small['mix_norm'][l] = accmi[3]
        small['ffn2_norm'][l] = acc2[3]
        small['conv_w'][l] = dlp[0:4]
        small['conv_b'][l] = dlp[4]
        small['gate_a_b'][l] = dlp[5].reshape(HEADS, HD)
        small['gate_x_b'][l] = dlp[6].reshape(HEADS, HD)
        small['lru_lambda'][l] = dlp[7]
        small['gate_a_w'][l] = _block_diag_extract(dwa)
        small['gate_x_w'][l] = _block_diag_extract(dwx)
        small['v_norm'][l] = dgp[0]
        small['lru_out_norm'][l] = dgp[1]
        small['gmlp_out_norm'][l] = dgp[2]
        small['spatial_w'][l] = dwm.reshape(HEADS, CHUNK, CHUNK) * tril
        small['spatial_b'][l] = dbz.reshape(CHUNK, HEADS, HD).sum(-1).T
    small = {k: jnp.stack(v) for k, v in small.items()}
    small['final_norm'] = dfin[0]
    return loss, dx, big, small, jnp.stack(dmod)


def ada_fwd(c_all, w_ada, b_loc):
    def body(c_ref, w_ref, b_ref, mod_ref, sc_ref):
        cv = c_ref[...]
        sc = cv * _sigmoid(cv)
        sc_ref[...] = sc
        mod_ref[...] = _dot3(sc, w_ref[...]) + b_ref[...]

    return pl.pallas_call(
        body, name="ada_fwd",
        grid=(DEPTH,),
        in_specs=[pl.BlockSpec((NDEV, D), lambda l: (0, 0)), pl.BlockSpec((None, D, AC), lambda l: (l, 0, 0)),
                  pl.BlockSpec((None, 1, AC), lambda l: (l, 0, 0))],
        out_specs=[pl.BlockSpec((None, NDEV, AC), lambda l: (l, 0, 0)), pl.BlockSpec((NDEV, D), lambda l: (0, 0))],
        out_shape=[jax.ShapeDtypeStruct((DEPTH, NDEV, AC), F32), jax.ShapeDtypeStruct((NDEV, D), F32)],
        compiler_params=_cparams("arbitrary"),
    )(c_all, w_ada, b_loc)


def ada_bwd(sc_t, dmod_cols):
    def body(sc_ref, dm_ref, g_ref):
        sc = sc_ref[...]
        dm = dm_ref[...]
        acc = sc[:, 0:1] * dm[0:1, :]
        for b in range(1, NDEV):
            acc = acc + sc[:, b:b + 1] * dm[b:b + 1, :]
        g_ref[...] = acc

    return pl.pallas_call(
        body, name="ada_bwd",
        grid=(DEPTH,),
        in_specs=[pl.BlockSpec((D, NDEV), lambda l: (0, 0)), pl.BlockSpec((None, NDEV, AC), lambda l: (l, 0, 0))],
        out_specs=pl.BlockSpec((None, None, D, AC), lambda l: (0, l, 0, 0)),
        out_shape=jax.ShapeDtypeStruct((1, DEPTH, D, AC), F32),
        compiler_params=_cparams("arbitrary"),
    )(sc_t, dmod_cols)


def _row_tile(rows, cols):
    if rows * cols <= 512 * 1024:
        return rows
    for tr in (512, 384, 352, 256, 128, 64, 32, 16, 8):
        if rows % tr == 0:
            return tr
    return rows


def adamw(gparts, slot0, w, m, v, name):
    P, _, R, C = gparts.shape
    L = w.shape[0]
    tr = _row_tile(R, C)

    def body(g_ref, w_ref, m_ref, v_ref, go_ref, do_ref, mo_ref, vo_ref):
        g = g_ref[0].astype(F32)
        for p in range(1, P):
            g = g + g_ref[p].astype(F32)
        go_ref[...] = g
        mn = ADAM_B1 * m_ref[...] + (1.0 - ADAM_B1) * g
        vn = ADAM_B2 * v_ref[...] + (1.0 - ADAM_B2) * (g * g)
        mo_ref[...] = mn
        vo_ref[...] = vn
        m_hat = mn / (1.0 - ADAM_B1 ** ADAM_STEP)
        v_hat = vn / (1.0 - ADAM_B2 ** ADAM_STEP)
        do_ref[...] = -ADAM_LR * (m_hat / (jnp.sqrt(v_hat) + ADAM_EPS) + ADAM_WD * w_ref[...])

    blk = pl.BlockSpec((None, tr, C), lambda l, i: (l, i, 0))
    return pl.pallas_call(
        body, name=name,
        grid=(L, R // tr),
        in_specs=[pl.BlockSpec((P, None, tr, C), lambda l, i: (0, slot0 + l, i, 0)), blk, blk, blk],
        out_specs=[blk, blk, blk, blk],
        out_shape=[jax.ShapeDtypeStruct((L, R, C), F32)] * 4,
        compiler_params=_cparams("arbitrary", "arbitrary"),
    )(gparts, w, m, v)


def sum_parts(parts):
    P, R, C = parts.shape

    def body(p_ref, o_ref):
        acc = p_ref[0]
        for p in range(1, P):
            acc = acc + p_ref[p]
        o_ref[...] = acc

    return pl.pallas_call(
        body, name="sum_parts",
        in_specs=[pl.BlockSpec(memory_space=pltpu.VMEM)],
        out_specs=pl.BlockSpec(memory_space=pltpu.VMEM),
        out_shape=jax.ShapeDtypeStruct((R, C), F32),
    )(parts)


WEIGHTS = ['w_ada', 'b_ada', 'ffn1_norm', 'ffn1_w_gu', 'ffn1_w_down', 'mix_norm', 'w_in', 'conv_w', 'conv_b',
           'gate_a_w', 'gate_a_b', 'gate_x_w', 'gate_x_b', 'lru_lambda', 'v_norm', 'spatial_w', 'spatial_b',
           'lru_out_norm', 'gmlp_out_norm', 'w_out', 'ffn2_norm', 'ffn2_w_gu', 'ffn2_w_down', 'final_norm']
PACKED = ['b_ada', 'ffn1_norm', 'mix_norm', 'conv_b', 'gate_a_w', 'gate_a_b', 'gate_x_w', 'gate_x_b', 'lru_lambda',
          'v_norm', 'spatial_w', 'spatial_b', 'lru_out_norm', 'gmlp_out_norm', 'ffn2_norm', 'final_norm', 'conv_w']
PACK_LANES = 128
PACK_ROW_ALIGN = 8 * NDEV


def _pack(d):
    parts = [d[k].reshape(-1, PACK_LANES).astype(F32) for k in PACKED]
    used = sum(p.shape[0] for p in parts)
    rows = -(-used // PACK_ROW_ALIGN) * PACK_ROW_ALIGN
    return jnp.concatenate(parts + [jnp.zeros((rows - used, PACK_LANES), F32)], axis=0)


def _unpack(buf, shapes):
    out, off = {}, 0
    for k in PACKED:
        size = 1
        for s in shapes[k]:
            size *= s
        nrows = size // PACK_LANES
        out[k] = buf[off:off + nrows].reshape(shapes[k])
        off += nrows
    return out


def kernel(x, c, w_ada, b_ada, ffn1_norm, ffn1_w_gu, ffn1_w_down, mix_norm, w_in, conv_w, conv_b, gate_a_w, gate_a_b, gate_x_w, gate_x_b, lru_lambda, v_norm, spatial_w, spatial_b, lru_out_norm, gmlp_out_norm, w_out, ffn2_norm, ffn2_w_gu, ffn2_w_down, final_norm, loss_target, m_w_ada, m_b_ada, m_ffn1_norm, m_ffn1_w_gu, m_ffn1_w_down, m_mix_norm, m_w_in, m_conv_w, m_conv_b, m_gate_a_w, m_gate_a_b, m_gate_x_w, m_gate_x_b, m_lru_lambda, m_v_norm, m_spatial_w, m_spatial_b, m_lru_out_norm, m_gmlp_out_norm, m_w_out, m_ffn2_norm, m_ffn2_w_gu, m_ffn2_w_down, m_final_norm, v_w_ada, v_b_ada, v_ffn1_norm, v_ffn1_w_gu, v_ffn1_w_down, v_mix_norm, v_w_in, v_conv_w, v_conv_b, v_gate_a_w, v_gate_a_b, v_gate_x_w, v_gate_x_b, v_lru_lambda, v_v_norm, v_spatial_w, v_spatial_b, v_lru_out_norm, v_gmlp_out_norm, v_w_out, v_ffn2_norm, v_ffn2_w_gu, v_ffn2_w_down, v_final_norm):
    w = dict(w_ada=w_ada, b_ada=b_ada, ffn1_norm=ffn1_norm, ffn1_w_gu=ffn1_w_gu, ffn1_w_down=ffn1_w_down, mix_norm=mix_norm, w_in=w_in, conv_w=conv_w, conv_b=conv_b, gate_a_w=gate_a_w, gate_a_b=gate_a_b, gate_x_w=gate_x_w, gate_x_b=gate_x_b, lru_lambda=lru_lambda, v_norm=v_norm, spatial_w=spatial_w, spatial_b=spatial_b, lru_out_norm=lru_out_norm, gmlp_out_norm=gmlp_out_norm, w_out=w_out, ffn2_norm=ffn2_norm, ffn2_w_gu=ffn2_w_gu, ffn2_w_down=ffn2_w_down, final_norm=final_norm)
    m = dict(w_ada=m_w_ada, b_ada=m_b_ada, ffn1_norm=m_ffn1_norm, ffn1_w_gu=m_ffn1_w_gu, ffn1_w_down=m_ffn1_w_down, mix_norm=m_mix_norm, w_in=m_w_in, conv_w=m_conv_w, conv_b=m_conv_b, gate_a_w=m_gate_a_w, gate_a_b=m_gate_a_b, gate_x_w=m_gate_x_w, gate_x_b=m_gate_x_b, lru_lambda=m_lru_lambda, v_norm=m_v_norm, spatial_w=m_spatial_w, spatial_b=m_spatial_b, lru_out_norm=m_lru_out_norm, gmlp_out_norm=m_gmlp_out_norm, w_out=m_w_out, ffn2_norm=m_ffn2_norm, ffn2_w_gu=m_ffn2_w_gu, ffn2_w_down=m_ffn2_w_down, final_norm=m_final_norm)
    v = dict(w_ada=v_w_ada, b_ada=v_b_ada, ffn1_norm=v_ffn1_norm, ffn1_w_gu=v_ffn1_w_gu, ffn1_w_down=v_ffn1_w_down, mix_norm=v_mix_norm, w_in=v_w_in, conv_w=v_conv_w, conv_b=v_conv_b, gate_a_w=v_gate_a_w, gate_a_b=v_gate_a_b, gate_x_w=v_gate_x_w, gate_x_b=v_gate_x_b, lru_lambda=v_lru_lambda, v_norm=v_v_norm, spatial_w=v_spatial_w, spatial_b=v_spatial_b, lru_out_norm=v_lru_out_norm, gmlp_out_norm=v_gmlp_out_norm, w_out=v_w_out, ffn2_norm=v_ffn2_norm, ffn2_w_gu=v_ffn2_w_gu, ffn2_w_down=v_ffn2_w_down, final_norm=v_final_norm)
    me = 4 * lax.axis_index("x") + 2 * lax.axis_index("y") + lax.axis_index("c")

    loc = dict(gu=jnp.concatenate([ffn1_w_gu, ffn2_w_gu], axis=0).astype(BF16),
               down=jnp.concatenate([ffn1_w_down, ffn2_w_down], axis=0).astype(BF16),
               w_in=w_in.astype(BF16), w_out=w_out.astype(BF16))
    c_g, conv_g, gu0, down0 = all_gather([(c, None), (conv_w, None), (loc['gu'], 0), (loc['down'], 0)], "gather_first")
    conv_w_full = conv_g.transpose(1, 2, 0, 3).reshape(DEPTH, CONV_WIDTH, LW)

    b_loc = lax.dynamic_slice(b_ada, (0, me * AC), (DEPTH, AC)).reshape(DEPTH, 1, AC)
    mod_cols, sc_all = ada_fwd(c_g.reshape(NDEV, D), w_ada, b_loc)
    (mod_rows,) = all_to_all([mod_cols.transpose(1, 0, 2)], "scatter_mod")
    mod = mod_rows.transpose(1, 0, 2).reshape(DEPTH, NMOD, D)

    small_w = {k: w[k] for k in PACKED if k != 'conv_w'}
    me_arr = jnp.reshape(me, (1,)).astype(jnp.int32)
    loss_loc, dx, big, small_g, dmod = local_fwd_bwd(me_arr, x[0], loss_target[0], mod, small_w, loc,
                                                     {('gu', 0): gu0, ('down', 0): down0}, conv_w_full)
    loss = lax.psum(loss_loc, ("x", "y", "c"))

    small_g['b_ada'] = dmod.reshape(DEPTH, NMOD * D)
    gpack = _pack(small_g)
    rows = gpack.shape[0]
    dmod_out = dmod.reshape(DEPTH, NDEV, AC).transpose(1, 0, 2)
    dmod_r, pack_r = all_to_all([dmod_out, gpack.reshape(NDEV, rows // NDEV, PACK_LANES)], "scatter_grads")
    (gsum_g,) = all_gather([(sum_parts(pack_r), None)], "gather_small_grads")
    gsum = gsum_g.reshape(1, 1, rows, PACK_LANES)

    res = {}
    t = lambda a: a.transpose(0, 2, 1)
    gu_t = big['gu'].transpose(0, 1, 3, 2)
    res['ffn1_w_gu'] = tuple(t(r) for r in adamw(gu_t, 0, t(w['ffn1_w_gu']), t(m['ffn1_w_gu']), t(v['ffn1_w_gu']),
                                                 "adamw_gu_a"))
    res['ffn2_w_gu'] = tuple(t(r) for r in adamw(gu_t, DEPTH, t(w['ffn2_w_gu']), t(m['ffn2_w_gu']),
                                                 t(v['ffn2_w_gu']), "adamw_gu_b"))
    res['ffn1_w_down'] = adamw(big['down'], 0, w['ffn1_w_down'], m['ffn1_w_down'], v['ffn1_w_down'], "adamw_down_a")
    res['ffn2_w_down'] = adamw(big['down'], DEPTH, w['ffn2_w_down'], m['ffn2_w_down'], v['ffn2_w_down'], "adamw_down_b")
    res['w_in'] = adamw(big['w_in'], 0, w['w_in'], m['w_in'], v['w_in'], "adamw_w_in")
    res['w_out'] = adamw(big['w_out'], 0, w['w_out'], m['w_out'], v['w_out'], "adamw_w_out")
    g_ada = ada_bwd(sc_all.T, dmod_r.transpose(1, 0, 2))
    res['w_ada'] = adamw(g_ada, 0, w['w_ada'], m['w_ada'], v['w_ada'], "adamw_w_ada")
    shapes = {k: w[k].shape for k in PACKED}
    shapes['conv_w'] = (DEPTH, CONV_WIDTH, LW)
    dummy = jnp.zeros(shapes['conv_w'], F32)
    packs = adamw(gsum, 0, _pack({**small_w, 'conv_w': dummy})[None], _pack({**{k: m[k] for k in small_w}, 'conv_w': dummy})[None],
                  _pack({**{k: v[k] for k in small_w}, 'conv_w': dummy})[None], "adamw_small")
    unpacked = [_unpack(b[0], shapes) for b in packs]
    for k in small_w:
        res[k] = tuple(u[k] for u in unpacked)
    gconv = lax.dynamic_slice(unpacked[0]['conv_w'], (0, 0, me * (LW // NDEV)), (DEPTH, CONV_WIDTH, LW // NDEV))
    cshape = (1, DEPTH * CONV_WIDTH, LW // NDEV)
    rc = adamw(gconv.reshape((1,) + cshape), 0, conv_w.reshape(cshape), m['conv_w'].reshape(cshape),
               v['conv_w'].reshape(cshape), "adamw_conv_w")
    res['conv_w'] = tuple(r.reshape(conv_w.shape) for r in rc)

    return (loss, dx[None], *[res[k][0] for k in WEIGHTS], *[res[k][1] for k in WEIGHTS],
            *[res[k][2] for k in WEIGHTS], *[res[k][3] for k in WEIGHTS])
```

```python
import jax
import jax.numpy as jnp
from jax import lax
from jax.experimental import pallas as pl
from jax.experimental.pallas import tpu as pltpu

F32 = jnp.float32
BF16 = jnp.bfloat16

NDEV = 8
DEPTH = 2
D = 1024
DFF = 2816
FC = 2 * DFF // NDEV
NCHUNK = DFF // FC
DR = DFF // NDEV
LW = 512
GW = 512
HD = 64
HEADS = 8
CHUNK = 128
PC = 2 * (LW + GW) // NDEV
OR = D // NDEV
NMOD = 9
AC = NMOD * D // NDEV
LC = 128
EPS = 1e-6
RG_LRU_C = 8.0
CONV_WIDTH = 4

ADAM_LR = 0.001
ADAM_B1 = 0.9
ADAM_B2 = 0.999
ADAM_EPS = 1e-08
ADAM_WD = 0.01
ADAM_STEP = 10

VMEM_LIMIT_BYTES = 60 * 1024 * 1024
MESH = pl.DeviceIdType.MESH
ANY = pl.BlockSpec(memory_space=pl.ANY)


def _cparams(*sem):
    return pltpu.CompilerParams(dimension_semantics=tuple(sem) if sem else None,
                                vmem_limit_bytes=VMEM_LIMIT_BYTES)


def _dot(a, b):
    return jnp.dot(a, b, preferred_element_type=F32)


def _dot_nt(a, b):
    return lax.dot_general(a, b, (((1,), (1,)), ((), ())), preferred_element_type=F32)


def _dot_tn(a, b):
    return lax.dot_general(a, b, (((0,), (0,)), ((), ())), preferred_element_type=F32)


def _split(a):
    hi = a.astype(BF16)
    lo = (a - hi.astype(F32)).astype(BF16)
    return hi, lo


def _dot3(a, b):
    ah, al = _split(a)
    bh, bl = _split(b)
    return _dot(ah, bh) + (_dot(ah, bl) + _dot(al, bh))


def _csum(a):
    return jnp.sum(a, axis=0, keepdims=True)


def _rmean(a):
    return jnp.mean(a, axis=-1, keepdims=True)


def _sigmoid(a):
    return 1.0 / (1.0 + jnp.exp(-a))


_GELU_K = 0.7978845608028654
_GELU_C = 0.044715


def _gelu(a):
    return 0.5 * a * (1.0 + jnp.tanh(_GELU_K * (a + _GELU_C * a * a * a)))


def _gelu_and_grad(a):
    a2 = a * a
    t = jnp.tanh(_GELU_K * (a + _GELU_C * a2 * a))
    half = 0.5 * (1.0 + t)
    return a * half, half + 0.5 * a * (1.0 - t * t) * (_GELU_K * (1.0 + 3.0 * _GELU_C * a2))


def _norm_mod(x, gain, scale, shift):
    rstd = lax.rsqrt(_rmean(x * x) + EPS)
    return (x * rstd * gain) * (1.0 + scale) + shift


def _norm_mod_bwd(dh, x, gain, scale):
    rstd = lax.rsqrt(_rmean(x * x) + EPS)
    xhat = x * rstd
    dshift = _csum(dh)
    dscale = _csum(dh * (xhat * gain))
    dhn = dh * (1.0 + scale)
    dgain = _csum(dhn * xhat)
    dxhat = dhn * gain
    dx = rstd * (dxhat - xhat * _rmean(dxhat * xhat))
    return dx, dshift, dscale, dgain


def _rms(x, gain):
    rstd = lax.rsqrt(_rmean(x * x) + EPS)
    return x * rstd * gain


def _rms_bwd(dy, x, gain):
    rstd = lax.rsqrt(_rmean(x * x) + EPS)
    xhat = x * rstd
    dgain = _csum(dy * xhat)
    dxhat = dy * gain
    return rstd * (dxhat - xhat * _rmean(dxhat * xhat)), dgain


PAIR = 2 * HD


def _seg_mean(a, pavg):
    hi, lo = _split(a)
    return jnp.concatenate([_dot(hi[:, p:p + PAIR], pavg) + _dot(lo[:, p:p + PAIR], pavg)
                            for p in range(0, a.shape[1], PAIR)], axis=1)


def _block_copies(src_hbm, dst_vmem, sems, rows):
    copies = []
    for k in range(NDEV):
        dst = dst_vmem.at[k] if rows is None else dst_vmem.at[pl.ds(k * rows, rows)]
        copies.append(pltpu.make_async_copy(src_hbm.at[k], dst, sems.at[k]))
    return copies


def _ffn_weight_fetch(wgu_hbm, wd_hbm, wgu_v, wd_v, sems):
    @pl.when(pl.program_id(0) == 0)
    def _():
        copies = _block_copies(wgu_hbm, wgu_v, sems.at[0], None) + _block_copies(wd_hbm, wd_v, sems.at[1], DR)
        for cp in copies:
            cp.start()
        for cp in copies:
            cp.wait()


def _place():
    return lax.axis_index("x"), lax.axis_index("y"), lax.axis_index("c")


def _slot(p):
    return 4 * p[0] + 2 * p[1] + p[2]


class GatherRide:
    def __init__(self, srcs):
        self.n = len(srcs)
        self.index = [i for _, i in srcs]
        self.args = [a for a, _ in srcs]
        self.out_shape = [jax.ShapeDtypeStruct((NDEV,) + (a.shape if i is None else a.shape[1:]), a.dtype)
                          for a, i in srcs]
        self.scratch = [pltpu.SemaphoreType.DMA((self.n, NDEV - 1)), pltpu.SemaphoreType.DMA((self.n, NDEV - 1)),
                        pltpu.SemaphoreType.DMA((self.n,))]

    def hooks(self, ins, outs, sems):
        send_sems, recv_sems, local_sems = sems
        n = self.n
        x, y, c = _place()
        me, sibling = (x, y, c), (x, y, 1 - c)
        chips = [(1 - x, y), (x, 1 - y), (1 - x, 1 - y)]

        def local(a):
            return ins[a] if self.index[a] is None else ins[a].at[self.index[a]]

        def copy(a, k, block, to, src=None):
            dst = outs[a].at[_slot(block)]
            return pltpu.make_async_remote_copy(
                src_ref=dst if src is None else src, dst_ref=dst,
                send_sem=send_sems.at[a, k], recv_sem=recv_sems.at[a, k],
                device_id=to, device_id_type=MESH)

        def mine():
            return [pltpu.make_async_copy(local(a), outs[a].at[_slot(me)], local_sems.at[a]) for a in range(n)]

        def first():
            cps = []
            for a in range(n):
                cps.append(copy(a, 0, me, sibling, src=local(a)))
                cps += [copy(a, 1 + j, me, (*chip, c), src=local(a)) for j, chip in enumerate(chips)]
            return cps

        def passed():
            return [copy(a, 4 + j, (*chip, c), sibling) for j, chip in enumerate(chips) for a in range(n)]

        def start():
            for cp in mine() + first():
                cp.start()

        def mid():
            for j, chip in enumerate(chips):
                for a in range(n):
                    copy(a, 1 + j, (*chip, c), me).wait_recv()
                    copy(a, 4 + j, (*chip, c), sibling).start()

        def finish():
            for a in range(n):
                copy(a, 0, sibling, me).wait_recv()
                for j, chip in enumerate(chips):
                    copy(a, 4 + j, (*chip, 1 - c), me).wait_recv()
            for cp in first() + passed():
                cp.wait_send()
            for cp in mine():
                cp.wait()

        return start, mid, finish


def all_gather(srcs, name):
    ride = GatherRide(srcs)
    n = ride.n

    def body(*refs):
        start, mid, finish = ride.hooks(refs[:n], refs[n:2 * n], refs[2 * n:])
        start()
        mid()
        finish()

    return pl.pallas_call(
        body, name=name,
        in_specs=[ANY] * n, out_specs=[ANY] * n, out_shape=ride.out_shape, scratch_shapes=ride.scratch,
    )(*ride.args)


def _call(core, ride, *, name, grid, in_specs, out_specs, out_shape, scratch_shapes, args):
    if ride is None:
        outs = pl.pallas_call(core, name=name, grid=grid, in_specs=in_specs, out_specs=out_specs,
                              out_shape=out_shape, scratch_shapes=scratch_shapes,
                              compiler_params=_cparams("arbitrary"))(*args)
        return outs, []
    n_in, n_out, n_sc, n = len(in_specs), len(out_shape), len(scratch_shapes), ride.n
    nsteps = grid[0]
    mid_step = max(nsteps - 2, 0)

    def body(*refs):
        cuts = [n_in, n_in + n, n_in + n + n_out, n_in + 2 * n + n_out, n_in + 2 * n + n_out + n_sc]
        ci, ri, co, ro, cs, rs = (refs[a:b] for a, b in zip([0] + cuts, cuts + [len(refs)]))
        start, mid, finish = ride.hooks(ri, ro, rs)
        i = pl.program_id(0)
        pl.when(i == 0)(start)
        core(*ci, *co, *cs)
        pl.when(i == mid_step)(mid)
        pl.when(i == nsteps - 1)(finish)

    outs = pl.pallas_call(
        body, name=name, grid=grid,
        in_specs=list(in_specs) + [ANY] * n, out_specs=list(out_specs) + [ANY] * n,
        out_shape=list(out_shape) + ride.out_shape, scratch_shapes=list(scratch_shapes) + ride.scratch,
        compiler_params=_cparams("arbitrary"))(*args, *ride.args)
    return outs[:n_out], outs[n_out:]


def all_to_all(arrs, name):
    n = len(arrs)

    def body(*refs):
        ins, outs = refs[:n], refs[n:2 * n]
        send_sems, recv_sems, local_sems = refs[2 * n:]
        x, y, c = _place()
        me = (x, y, c)

        def peer(k):
            return (1 - x if k & 4 else x, 1 - y if k & 2 else y, 1 - c if k & 1 else c)

        def copy(a, k):
            return pltpu.make_async_remote_copy(
                src_ref=ins[a].at[_slot(peer(k))], dst_ref=outs[a].at[_slot(me)],
                send_sem=send_sems.at[a, k - 1], recv_sem=recv_sems.at[a, k - 1],
                device_id=peer(k), device_id_type=MESH)

        def landing(a, k):
            return pltpu.make_async_remote_copy(
                src_ref=outs[a].at[_slot(peer(k))], dst_ref=outs[a].at[_slot(peer(k))],
                send_sem=send_sems.at[a, k - 1], recv_sem=recv_sems.at[a, k - 1],
                device_id=me, device_id_type=MESH)

        mine = [pltpu.make_async_copy(ins[a].at[_slot(me)], outs[a].at[_slot(me)], local_sems.at[a]) for a in range(n)]
        for cp in mine:
            cp.start()
        sends = [copy(a, k) for a in range(n) for k in range(1, NDEV)]
        for cp in sends:
            cp.start()
        for a in range(n):
            for k in range(1, NDEV):
                landing(a, k).wait_recv()
        for cp in sends:
            cp.wait_send()
        for cp in mine:
            cp.wait()

    return pl.pallas_call(
        body, name=name,
        in_specs=[ANY] * n, out_specs=[ANY] * n,
        out_shape=[jax.ShapeDtypeStruct(a.shape, a.dtype) for a in arrs],
        scratch_shapes=[pltpu.SemaphoreType.DMA((n, NDEV - 1)), pltpu.SemaphoreType.DMA((n, NDEV - 1)),
                        pltpu.SemaphoreType.DMA((n,))],
    )(*arrs)


FFN_TS = 256
FFN_FWD_TS = 512


def ffn_fwd(x, vec, wgu_g, wdown_g, tag, ride=None):
    S = x.shape[0]
    ts = min(FFN_FWD_TS, S)

    def body(x_ref, vec_ref, wgu_hbm, wd_hbm, xo_ref, h_ref, gu_ref, f_ref, wgu_v, wd_v, sems):
        _ffn_weight_fetch(wgu_hbm, wd_hbm, wgu_v, wd_v, sems)
        xv = x_ref[...]
        h = _norm_mod(xv, vec_ref[3:4, :], vec_ref[1:2, :], vec_ref[0:1, :]).astype(BF16)
        h_ref[...] = h
        acc = jnp.zeros((ts, D), F32)
        for j in range(NCHUNK):
            g = _dot(h, wgu_v[j])
            u = _dot(h, wgu_v[NCHUNK + j])
            gu_ref[j] = g.astype(BF16)
            gu_ref[NCHUNK + j] = u.astype(BF16)
            a = (g * _sigmoid(g) * u).astype(BF16)
            acc = acc + _dot(a, wd_v[pl.ds(j * FC, FC), :])
        f_ref[...] = acc.astype(BF16)
        xo_ref[...] = xv + (0.5 * vec_ref[2:3, :]) * acc

    return _call(
        body, ride, name=f"ffn_fwd_{tag}",
        grid=(S // ts,),
        in_specs=[pl.BlockSpec((ts, D), lambda i: (i, 0)),
                  pl.BlockSpec((8, D), lambda i: (0, 0)), ANY, ANY],
        out_specs=[pl.BlockSpec((ts, D), lambda i: (i, 0)),
                   pl.BlockSpec((ts, D), lambda i: (i, 0)),
                   pl.BlockSpec((NDEV, ts, FC), lambda i: (0, i, 0)),
                   pl.BlockSpec((ts, D), lambda i: (i, 0))],
        out_shape=[jax.ShapeDtypeStruct((S, D), F32), jax.ShapeDtypeStruct((S, D), BF16),
                   jax.ShapeDtypeStruct((NDEV, S, FC), BF16), jax.ShapeDtypeStruct((S, D), BF16)],
        scratch_shapes=[pltpu.VMEM((NDEV, D, FC), BF16), pltpu.VMEM((DFF, D), BF16),
                        pltpu.SemaphoreType.DMA((2, NDEV))],
        args=(x, vec, wgu_g, wdown_g))


def ffn_bwd(dxo, x, gu, f, vec, wgu_g, wdown_g, tag):
    S = x.shape[0]
    ts = min(FFN_TS, S)

    def body(dxo_ref, x_ref, gu_ref, f_ref, vec_ref, wgu_hbm, wd_hbm,
             dx_ref, dgu_ref, a_ref, df_ref, acc_ref, wgu_v, wd_v, sems):
        _ffn_weight_fetch(wgu_hbm, wd_hbm, wgu_v, wd_v, sems)

        @pl.when(pl.program_id(0) == 0)
        def _():
            acc_ref[...] = jnp.zeros_like(acc_ref)

        dxo_v = dxo_ref[...]
        dgate = 0.5 * _csum(dxo_v * f_ref[...].astype(F32))
        df = ((0.5 * vec_ref[2:3, :]) * dxo_v).astype(BF16)
        df_ref[...] = df
        dh = jnp.zeros((ts, D), F32)
        for j in range(NCHUNK):
            da = _dot_nt(df, wd_v[pl.ds(j * FC, FC), :])
            g = gu_ref[j].astype(F32)
            u = gu_ref[NCHUNK + j].astype(F32)
            sg = _sigmoid(g)
            si = g * sg
            a_ref[j] = (si * u).astype(BF16)
            dg = (da * u * (sg * (1.0 + g * (1.0 - sg)))).astype(BF16)
            du = (da * si).astype(BF16)
            dgu_ref[j] = dg
            dgu_ref[NCHUNK + j] = du
            dh = dh + _dot_nt(dg, wgu_v[j]) + _dot_nt(du, wgu_v[NCHUNK + j])
        dx, dshift, dscale, dgain = _norm_mod_bwd(dh, x_ref[...], vec_ref[3:4, :], vec_ref[1:2, :])
        dx_ref[...] = dx + dxo_v
        acc_ref[0:1, :] += dshift
        acc_ref[1:2, :] += dscale
        acc_ref[2:3, :] += dgate
        acc_ref[3:4, :] += dgain

    row = pl.BlockSpec((ts, D), lambda i: (i, 0))
    return pl.pallas_call(
        body, name=f"ffn_bwd_{tag}",
        grid=(S // ts,),
        in_specs=[row, row, pl.BlockSpec((NDEV, ts, FC), lambda i: (0, i, 0)), row,
                  pl.BlockSpec((8, D), lambda i: (0, 0)), ANY, ANY],
        out_specs=[row, pl.BlockSpec((NDEV, ts, FC), lambda i: (0, i, 0)),
                   pl.BlockSpec((NCHUNK, ts, FC), lambda i: (0, i, 0)), row,
                   pl.BlockSpec((8, D), lambda i: (0, 0))],
        out_shape=[jax.ShapeDtypeStruct((S, D), F32), jax.ShapeDtypeStruct((NDEV, S, FC), BF16),
                   jax.ShapeDtypeStruct((NCHUNK, S, FC), BF16), jax.ShapeDtypeStruct((S, D), BF16),
                   jax.ShapeDtypeStruct((8, D), F32)],
        scratch_shapes=[pltpu.VMEM((NDEV, D, FC), BF16), pltpu.VMEM((DFF, D), BF16),
                        pltpu.SemaphoreType.DMA((2, NDEV))],
        compiler_params=_cparams("arbitrary"),
    )(dxo, x, gu, f, vec, wgu_g, wdown_g)


NCHIP = NDEV // 2


def tn_matmul_scatter(me_arr, a, b, slot, nslots, prev, name, split=1):
    na, S, M = a.shape
    nb, _, N = b.shape
    ncall = NDEV // split
    ts = min(4096, S)
    nsteps = S // ts
    mp = M // split
    other_step = {1: lambda j: 2 * j, 2: lambda j: j, 8: lambda j: 0}[split]
    mine_step = {1: lambda j: 2 * j + 1, 2: lambda j: j, 8: lambda j: 0}[split]

    def group(k, me_ref):
        if split == 1:
            return jnp.bitwise_xor(me_ref[0], NDEV - 1 - k)
        if split == 2:
            return jnp.bitwise_xor(me_ref[0] // 2, NCHIP - 1 - k)
        return 0

    def body(me_ref, *refs):
        a_ref, b_ref = refs[0], refs[1]
        recv_ref, acc, sb_other, sb_mine, land, d2d_send, d2d_recv, ici_send, ici_recv = refs[-9:]
        k = pl.program_id(0)
        s = pl.program_id(1)
        x, y, c = _place()
        my_chip = 2 * x + y

        def chip_of(j):
            if split == 8:
                cx, cy = j // 2, j % 2
            else:
                flip = NCHIP - 1 - j
                cx, cy = (1 - x if flip & 2 else x), (1 - y if flip & 1 else y)
            return cx, cy, 2 * cx + cy

        def piece(j, core):
            if split == 1:
                return acc[...]
            start = core * mp if split == 2 else (2 * j + core) * mp
            return acc[pl.ds(pl.multiple_of(start, 8), mp), :]

        def to_sibling(j):
            return pltpu.make_async_remote_copy(
                src_ref=sb_other.at[j], dst_ref=land.at[j], send_sem=d2d_send.at[j], recv_sem=d2d_recv.at[j],
                device_id=(x, y, 1 - c), device_id_type=MESH)

        def to_owner(j):
            cx, cy, ci = chip_of(j)
            dst = recv_ref.at[my_chip, slot]
            return ci, pltpu.make_async_copy(sb_mine.at[j], dst, ici_send.at[j]), pltpu.make_async_remote_copy(
                src_ref=sb_mine.at[j], dst_ref=dst, send_sem=ici_send.at[j], recv_sem=ici_recv.at[my_chip],
                device_id=(cx, cy, c), device_id_type=MESH)

        if nsteps == 1:
            acc[...] = _dot_tn(a_ref[...], b_ref[...])
        else:
            @pl.when(s == 0)
            def _():
                acc[...] = jnp.zeros_like(acc)

            acc[...] += _dot_tn(a_ref[...], b_ref[...])

        for kk in range(ncall):
            @pl.when((s == nsteps - 1) & (k == kk))
            def _():
                for j in range(NCHIP):
                    if other_step(j) == kk:
                        sb_other[j] = piece(j, 1 - c).astype(BF16)
                        to_sibling(j).start()
                for j in range(NCHIP):
                    if mine_step(j) == kk:
                        to_sibling(j).wait_recv()
                        sb_mine[j] = (piece(j, c) + land[j].astype(F32)).astype(BF16)
                        ci, loc, rem = to_owner(j)
                        pl.when(ci == my_chip)(loc.start)
                        pl.when(ci != my_chip)(rem.start)

        @pl.when((s == nsteps - 1) & (k == ncall - 1))
        def _():
            for j in range(NCHIP):
                to_sibling(j).wait_send()
                ci, loc, rem = to_owner(j)
                pl.when(ci == my_chip)(loc.wait)
                pl.when(ci != my_chip)(rem.wait_send)
            for src in range(NCHIP):
                @pl.when(my_chip != src)
                def _():
                    pltpu.make_async_remote_copy(
                        src_ref=recv_ref.at[src, slot], dst_ref=recv_ref.at[src, slot],
                        send_sem=ici_send.at[src], recv_sem=ici_recv.at[src],
                        device_id=(src // 2, src % 2, c), device_id_type=MESH).wait_recv()

    in_specs = [pl.BlockSpec((None, ts, M), (lambda k, s, me: (group(k, me), s, 0)) if na > 1 else (lambda k, s, me: (0, s, 0))),
                pl.BlockSpec((None, ts, N), (lambda k, s, me: (group(k, me), s, 0)) if nb > 1 else (lambda k, s, me: (0, s, 0)))]
    args = [me_arr, a, b]
    aliases = {}
    if prev is not None:
        in_specs.append(ANY)
        args.append(prev)
        aliases = {3: 0}
    return pl.pallas_call(
        body, name=name,
        grid_spec=pltpu.PrefetchScalarGridSpec(
            num_scalar_prefetch=1, grid=(ncall, nsteps), in_specs=in_specs, out_specs=ANY,
            scratch_shapes=[pltpu.VMEM((M, N), F32), pltpu.VMEM((NCHIP, mp, N), BF16), pltpu.VMEM((NCHIP, mp, N), BF16),
                            pltpu.VMEM((NCHIP, mp, N), BF16), pltpu.SemaphoreType.DMA((NCHIP,)),
                            pltpu.SemaphoreType.DMA((NCHIP,)), pltpu.SemaphoreType.DMA((NCHIP,)),
                            pltpu.SemaphoreType.DMA((NCHIP,))]),
        out_shape=jax.ShapeDtypeStruct((NCHIP, nslots, mp, N), BF16),
        input_output_aliases=aliases,
        compiler_params=_cparams("arbitrary", "arbitrary"),
    )(*args)


MIX_TS = 256
MIX_IN_TS = 512


def mix_in_fwd(x, vec, win_g, tag, ride=None):
    S = x.shape[0]
    ts = min(MIX_IN_TS, S)

    def body(x_ref, vec_ref, win_ref, hm_ref, proj_ref):
        h = _norm_mod(x_ref[...], vec_ref[3:4, :], vec_ref[1:2, :], vec_ref[0:1, :]).astype(BF16)
        hm_ref[...] = h
        for k in range(NDEV):
            proj_ref[k] = _dot(h, win_ref[k])

    return _call(
        body, ride, name=f"mix_in_fwd_{tag}",
        grid=(S // ts,),
        in_specs=[pl.BlockSpec((ts, D), lambda i: (i, 0)), pl.BlockSpec((8, D), lambda i: (0, 0)),
                  pl.BlockSpec((NDEV, D, PC), lambda i: (0, 0, 0))],
        out_specs=[pl.BlockSpec((ts, D), lambda i: (i, 0)),
                   pl.BlockSpec((NDEV, ts, PC), lambda i: (0, i, 0))],
        out_shape=[jax.ShapeDtypeStruct((S, D), BF16), jax.ShapeDtypeStruct((NDEV, S, PC), F32)],
        scratch_shapes=[], args=(x, vec, win_g))


def mix_in_bwd(dproj, x, dxo, vec, win_g, tag):
    S = x.shape[0]
    ts = min(MIX_IN_TS, S)

    def body(dp_ref, x_ref, dxo_ref, vec_ref, win_ref, dx_ref, acc_ref):
        @pl.when(pl.program_id(0) == 0)
        def _():
            acc_ref[...] = jnp.zeros_like(acc_ref)

        dh = jnp.zeros((ts, D), F32)
        for k in range(NDEV):
            dh = dh + _dot_nt(dp_ref[k], win_ref[k])
        dx, dshift, dscale, dgain = _norm_mod_bwd(dh, x_ref[...], vec_ref[3:4, :], vec_ref[1:2, :])
        dx_ref[...] = dx + dxo_ref[...]
        acc_ref[0:1, :] += dshift
        acc_ref[1:2, :] += dscale
        acc_ref[3:4, :] += dgain

    row = pl.BlockSpec((ts, D), lambda i: (i, 0))
    return pl.pallas_call(
        body, name=f"mix_in_bwd_{tag}",
        grid=(S // ts,),
        in_specs=[pl.BlockSpec((NDEV, ts, PC), lambda i: (0, i, 0)), row, row,
                  pl.BlockSpec((8, D), lambda i: (0, 0)),
                  pl.BlockSpec((NDEV, D, PC), lambda i: (0, 0, 0))],
        out_specs=[row, pl.BlockSpec((8, D), lambda i: (0, 0))],
        out_shape=[jax.ShapeDtypeStruct((S, D), F32), jax.ShapeDtypeStruct((8, D), F32)],
        compiler_params=_cparams("arbitrary"),
    )(dproj, x, dxo, vec, win_g)


SCAN_UNROLL = 4


def _shift_down(z, k, row):
    return jnp.where(row >= k, pltpu.roll(z, k, 0), 0.0)


def _shift_up(z, k, row, n):
    return jnp.where(row < n - k, pltpu.roll(z, n - k, 0), 0.0)


def _lru_gates(xc, lp_ref, wa_ref, wx_ref):
    xcb = xc.astype(BF16)
    ra = _sigmoid(_dot(xcb, wa_ref[...]) + lp_ref[5:6, :])
    ix = _sigmoid(_dot(xcb, wx_ref[...]) + lp_ref[6:7, :])
    lam = lp_ref[7:8, :]
    ls = jnp.minimum(lam, 0.0) - jnp.log(1.0 + jnp.exp(-jnp.abs(lam)))
    log_a = (RG_LRU_C * ls) * ra
    a = jnp.exp(log_a)
    mult = jnp.sqrt(-jnp.tanh(log_a) * (a * a + 1.0))
    return ra, ix, ls, a, mult


def _conv(x, lp_ref, row):
    return (lp_ref[4:5, :] + lp_ref[3:4, :] * x + lp_ref[2:3, :] * _shift_down(x, 1, row)
            + lp_ref[1:2, :] * _shift_down(x, 2, row) + lp_ref[0:1, :] * _shift_down(x, 3, row))


def lru_fwd(proj, lp, wa_t, wx_t, tag, ride=None):
    S = proj.shape[1]
    nblk = S // 8

    def body(x_ref, g_ref, lp_ref, wa_ref, wx_ref, y_ref, xc_ref, h_ref, a_s, b_s):
        x = x_ref[...]
        row = lax.broadcasted_iota(jnp.int32, x.shape, 0)
        xc = _conv(x, lp_ref, row)
        xc_ref[...] = xc
        ra, ix, ls, a, mult = _lru_gates(xc, lp_ref, wa_ref, wx_ref)
        a_s[...] = a
        b_s[...] = mult * (ix * xc)
        rowb = lax.broadcasted_iota(jnp.int32, (8, LC), 0)

        def step(i, carry):
            for q in range(SCAN_UNROLL):
                r0 = pl.multiple_of((i * SCAN_UNROLL + q) * 8, 8)
                A = a_s[pl.ds(r0, 8), :]
                B = b_s[pl.ds(r0, 8), :]
                for d in (1, 2, 4):
                    m = rowb >= d
                    As = jnp.where(m, pltpu.roll(A, d, 0), 1.0)
                    Bs = jnp.where(m, pltpu.roll(B, d, 0), 0.0)
                    B = A * Bs + B
                    A = A * As
                H = B + A * carry
                h_ref[pl.ds(r0, 8), :] = H
                carry = H[7:8, :]
            return carry

        lax.fori_loop(0, nblk // SCAN_UNROLL, step, jnp.zeros((1, LC), F32))
        y_ref[...] = h_ref[...] * _gelu(g_ref[...])

    col = pl.BlockSpec((S, LC), lambda c: (0, c))
    return _call(
        body, ride, name=f"lru_fwd_{tag}",
        grid=(LW // LC,),
        in_specs=[pl.BlockSpec((None, S, LC), lambda c: (c // 2, 0, c % 2)),
                  pl.BlockSpec((None, S, LC), lambda c: (2 + c // 2, 0, c % 2)),
                  pl.BlockSpec((8, LC), lambda c: (0, c)),
                  pl.BlockSpec((None, LC, LC), lambda c: (c, 0, 0)),
                  pl.BlockSpec((None, LC, LC), lambda c: (c, 0, 0))],
        out_specs=[col, col, col],
        out_shape=[jax.ShapeDtypeStruct((S, LW), F32)] * 3,
        scratch_shapes=[pltpu.VMEM((S, LC), F32), pltpu.VMEM((S, LC), F32)],
        args=(proj, proj, lp, wa_t, wx_t))


def lru_bwd(dy, proj, xc_all, hst, lp, wa_t, wx_t, tag):
    S = proj.shape[1]
    nblk = S // 8

    def body(dy_ref, x_ref, g_ref, xc_ref, h_ref, lp_ref, wa_ref, wx_ref,
             dx_ref, dg_ref, dlp_ref, dwa_ref, dwx_ref, c_s, l_s):
        xc = xc_ref[...]
        row = lax.broadcasted_iota(jnp.int32, xc.shape, 0)
        ra, ix, ls, a, mult = _lru_gates(xc, lp_ref, wa_ref, wx_ref)
        g = g_ref[...]
        dyv = dy_ref[...]
        h = h_ref[...]
        gelu_g, gelu_grad_g = _gelu_and_grad(g)
        dg_ref[...] = (dyv * h * gelu_grad_g).astype(BF16)
        c_s[...] = _shift_up(a, 1, row, S)
        l_s[...] = dyv * gelu_g
        rowb = lax.broadcasted_iota(jnp.int32, (8, LC), 0)

        def step(i, carry):
            for q in range(SCAN_UNROLL):
                r0 = pl.multiple_of((nblk - 1 - (i * SCAN_UNROLL + q)) * 8, 8)
                C = c_s[pl.ds(r0, 8), :]
                L = l_s[pl.ds(r0, 8), :]
                for d in (1, 2, 4):
                    m = rowb < 8 - d
                    Cs = jnp.where(m, pltpu.roll(C, 8 - d, 0), 1.0)
                    Ls = jnp.where(m, pltpu.roll(L, 8 - d, 0), 0.0)
                    L = C * Ls + L
                    C = C * Cs
                L = L + C * carry
                l_s[pl.ds(r0, 8), :] = L
                carry = L[0:1, :]
            return carry

        lax.fori_loop(0, nblk // SCAN_UNROLL, step, jnp.zeros((1, LC), F32))
        db = l_s[...]
        da = db * _shift_down(h, 1, row)
        ixc = ix * xc
        dmult = db * ixc
        dix = db * (mult * xc)
        dxc = db * (mult * ix)
        dlog_a = da * a - dmult * (a * a) / mult
        dra = dlog_a * (RG_LRU_C * ls)
        dls = _csum(dlog_a * ra) * RG_LRU_C
        lam = lp_ref[7:8, :]
        dlam = dls * _sigmoid(-lam)
        dpa = dra * ra * (1.0 - ra)
        dpx = dix * ix * (1.0 - ix)
        dpab = dpa.astype(BF16)
        dpxb = dpx.astype(BF16)
        xcb = xc.astype(BF16)
        dwa_ref[...] = _dot_tn(xcb, dpab)
        dwx_ref[...] = _dot_tn(xcb, dpxb)
        dxc = dxc + _dot_nt(dpab, wa_ref[...]) + _dot_nt(dpxb, wx_ref[...])
        x = x_ref[...]
        dlp_ref[0:1, :] = _csum(dxc * _shift_down(x, 3, row))
        dlp_ref[1:2, :] = _csum(dxc * _shift_down(x, 2, row))
        dlp_ref[2:3, :] = _csum(dxc * _shift_down(x, 1, row))
        dlp_ref[3:4, :] = _csum(dxc * x)
        dlp_ref[4:5, :] = _csum(dxc)
        dlp_ref[5:6, :] = _csum(dpa)
        dlp_ref[6:7, :] = _csum(dpx)
        dlp_ref[7:8, :] = dlam
        dx = (lp_ref[3:4, :] * dxc + lp_ref[2:3, :] * _shift_up(dxc, 1, row, S)
              + lp_ref[1:2, :] * _shift_up(dxc, 2, row, S) + lp_ref[0:1, :] * _shift_up(dxc, 3, row, S))
        dx_ref[...] = dx.astype(BF16)

    col = pl.BlockSpec((S, LC), lambda c: (0, c))
    pcol = pl.BlockSpec((None, S, LC), lambda c: (c // 2, 0, c % 2))
    return pl.pallas_call(
        body, name=f"lru_bwd_{tag}",
        grid=(LW // LC,),
        in_specs=[col, pcol, pl.BlockSpec((None, S, LC), lambda c: (2 + c // 2, 0, c % 2)), col, col,
                  pl.BlockSpec((8, LC), lambda c: (0, c)),
                  pl.BlockSpec((None, LC, LC), lambda c: (c, 0, 0)),
                  pl.BlockSpec((None, LC, LC), lambda c: (c, 0, 0))],
        out_specs=[pcol, pcol, pl.BlockSpec((8, LC), lambda c: (0, c)),
                   pl.BlockSpec((None, LC, LC), lambda c: (c, 0, 0)),
                   pl.BlockSpec((None, LC, LC), lambda c: (c, 0, 0))],
        out_shape=[jax.ShapeDtypeStruct((2, S, PC), BF16), jax.ShapeDtypeStruct((2, S, PC), BF16),
                   jax.ShapeDtypeStruct((8, LW), F32),
                   jax.ShapeDtypeStruct((LW // LC, LC, LC), F32), jax.ShapeDtypeStruct((LW // LC, LC, LC), F32)],
        scratch_shapes=[pltpu.VMEM((S, LC), F32), pltpu.VMEM((S, LC), F32)],
        compiler_params=_cparams("arbitrary"),
    )(dy, proj, proj, xc_all, hst, lp, wa_t, wx_t)


def _pair_stack(zp, low):
    return jnp.concatenate([jnp.where(low, zp, 0.0), jnp.where(low, 0.0, zp)], axis=0).astype(BF16)


def _spatial(w_ref, zc, low):
    return jnp.concatenate(
        [_dot(w_ref[:, 2 * p * CHUNK:2 * (p + 1) * CHUNK], _pair_stack(zc[:, p * PAIR:(p + 1) * PAIR], low))
         for p in range(GW // PAIR)], axis=1)


def _gmlp_fwd_parts(u, v, gp_ref, wcat_ref, bz_ref, pavg_ref, ts, with_grad=False):
    if with_grad:
        ug, ugrad = _gelu_and_grad(u)
        vg, vgrad = _gelu_and_grad(v)
    else:
        ug, vg, ugrad, vgrad = _gelu(u), _gelu(v), None, None
    pavg = pavg_ref[...]
    vc = vg - _seg_mean(vg, pavg)
    rs = lax.rsqrt(_seg_mean(vc * vc, pavg) + EPS)
    vhat = vc * rs
    vh = vhat * gp_ref[0:1, :]
    low = lax.broadcasted_iota(jnp.int32, (CHUNK, PAIR), 1) < HD
    zs = [_spatial(wcat_ref, vh[n * CHUNK:(n + 1) * CHUNK, :], low) + bz_ref[...] for n in range(ts // CHUNK)]
    z = jnp.concatenate(zs, axis=0) if len(zs) > 1 else zs[0]
    return ug, rs, vhat, vh, z, ugrad, vgrad


def mix_out_fwd(proj, ylru, x, vec, gp, wcat, bz, pavg, wout_g, tag, ride=None):
    S = x.shape[0]
    ts = min(MIX_TS, S)

    def body(u_ref, v_ref, yl_ref, x_ref, vec_ref, gp_ref, wcat_ref, bz_ref, pavg_ref, wout_ref,
             xo_ref, y_ref, fo_ref):
        u = jnp.concatenate([u_ref[0], u_ref[1]], axis=1)
        v = jnp.concatenate([v_ref[0], v_ref[1]], axis=1)
        ug, _, _, _, z, _, _ = _gmlp_fwd_parts(u, v, gp_ref, wcat_ref, bz_ref, pavg_ref, ts)
        n1 = _rms(yl_ref[...], gp_ref[1:2, :])
        n2 = _rms(ug * z, gp_ref[2:3, :])
        y = jnp.concatenate([n1, n2], axis=1).astype(BF16)
        y_ref[...] = y
        fo = jnp.zeros((ts, D), F32)
        for k in range(NDEV):
            fo = fo + _dot(y[:, k * OR:(k + 1) * OR], wout_ref[k])
        fo_ref[...] = fo.astype(BF16)
        xo_ref[...] = x_ref[...] + vec_ref[2:3, :] * fo

    row = pl.BlockSpec((ts, D), lambda i: (i, 0))
    full = lambda shp: pl.BlockSpec(shp, lambda i: tuple(0 for _ in shp))
    return _call(
        body, ride, name=f"mix_out_fwd_{tag}",
        grid=(S // ts,),
        in_specs=[pl.BlockSpec((2, ts, PC), lambda i: (2, i, 0)), pl.BlockSpec((2, ts, PC), lambda i: (3, i, 0)),
                  pl.BlockSpec((ts, LW), lambda i: (i, 0)), row, full((8, D)), full((8, GW)),
                  full((CHUNK, HEADS * CHUNK)), full((CHUNK, GW)), full((PAIR, PAIR)),
                  pl.BlockSpec((NDEV, OR, D), lambda i: (0, 0, 0))],
        out_specs=[row, row, row],
        out_shape=[jax.ShapeDtypeStruct((S, D), F32), jax.ShapeDtypeStruct((S, D), BF16),
                   jax.ShapeDtypeStruct((S, D), BF16)],
        scratch_shapes=[], args=(proj, proj, ylru, x, vec, gp, wcat, bz, pavg, wout_g))


def mix_out_bwd(dxo, proj, ylru, fo, vec, gp, wcat, wcat_t, bz, pavg, wout_g, tag):
    S = dxo.shape[0]
    ts = min(MIX_TS, S)

    def body(dxo_ref, u_ref, v_ref, yl_ref, fo_ref, vec_ref, gp_ref, wcat_ref, wcatt_ref, bz_ref, pavg_ref,
             wout_ref, dyo_ref, dyl_ref, duv_ref, acc_ref, dgp_ref, dwm_ref, dbz_ref):
        @pl.when(pl.program_id(0) == 0)
        def _():
            acc_ref[...] = jnp.zeros_like(acc_ref)
            dgp_ref[...] = jnp.zeros_like(dgp_ref)
            dwm_ref[...] = jnp.zeros_like(dwm_ref)
            dbz_ref[...] = jnp.zeros_like(dbz_ref)

        dxo_v = dxo_ref[...]
        acc_ref[2:3, :] += _csum(dxo_v * fo_ref[...].astype(F32))
        dyo = (vec_ref[2:3, :] * dxo_v).astype(BF16)
        dyo_ref[...] = dyo
        dn = [_dot_nt(dyo, wout_ref[k]) for k in range(NDEV)]
        dn1 = jnp.concatenate(dn[:NDEV // 2], axis=1)
        dn2 = jnp.concatenate(dn[NDEV // 2:], axis=1)
        dyl, dg1 = _rms_bwd(dn1, yl_ref[...], gp_ref[1:2, :])
        dyl_ref[...] = dyl
        u = jnp.concatenate([u_ref[0], u_ref[1]], axis=1)
        v = jnp.concatenate([v_ref[0], v_ref[1]], axis=1)
        ug, rs, vhat, vh, z, ugrad, vgrad = _gmlp_fwd_parts(u, v, gp_ref, wcat_ref, bz_ref, pavg_ref, ts,
                                                            with_grad=True)
        dyg, dg2 = _rms_bwd(dn2, ug * z, gp_ref[2:3, :])
        du = (dyg * z) * ugrad
        dz = dyg * ug
        low = lax.broadcasted_iota(jnp.int32, (CHUNK, PAIR), 1) < HD
        vhb = vh.astype(BF16)
        dvhs = []
        dbz = jnp.zeros((CHUNK, GW), F32)
        dwm = [jnp.zeros((2 * CHUNK, CHUNK), F32) for _ in range(GW // PAIR)]
        for n in range(ts // CHUNK):
            dzc = dz[n * CHUNK:(n + 1) * CHUNK, :]
            dbz = dbz + dzc
            for p in range(GW // PAIR):
                stack = _pair_stack(dzc[:, p * PAIR:(p + 1) * PAIR], low)
                dwm[p] = dwm[p] + _dot_nt(stack, vhb[n * CHUNK:(n + 1) * CHUNK, p * PAIR:(p + 1) * PAIR])
            dvhs.append(_spatial(wcatt_ref, dzc, low))
        dbz_ref[...] += dbz
        for p in range(GW // PAIR):
            dwm_ref[2 * p * CHUNK:2 * (p + 1) * CHUNK, :] += dwm[p]
        dvh = jnp.concatenate(dvhs, axis=0) if len(dvhs) > 1 else dvhs[0]
        pavg = pavg_ref[...]
        dvn = _csum(dvh * vhat)
        dvhat = dvh * gp_ref[0:1, :]
        dvg = rs * (dvhat - _seg_mean(dvhat, pavg) - vhat * _seg_mean(dvhat * vhat, pavg))
        dv = dvg * vgrad
        duv_ref[0] = du[:, :PC].astype(BF16)
        duv_ref[1] = du[:, PC:].astype(BF16)
        duv_ref[2] = dv[:, :PC].astype(BF16)
        duv_ref[3] = dv[:, PC:].astype(BF16)
        dgp_ref[0:1, :] += dvn
        dgp_ref[1:2, :] += dg1
        dgp_ref[2:3, :] += dg2

    row = pl.BlockSpec((ts, D), lambda i: (i, 0))
    full = lambda shp: pl.BlockSpec(shp, lambda i: tuple(0 for _ in shp))
    return pl.pallas_call(
        body, name=f"mix_out_bwd_{tag}",
        grid=(S // ts,),
        in_specs=[row, pl.BlockSpec((2, ts, PC), lambda i: (2, i, 0)), pl.BlockSpec((2, ts, PC), lambda i: (3, i, 0)),
                  pl.BlockSpec((ts, LW), lambda i: (i, 0)), row, full((8, D)), full((8, GW)),
                  full((CHUNK, HEADS * CHUNK)), full((CHUNK, HEADS * CHUNK)), full((CHUNK, GW)), full((PAIR, PAIR)),
                  pl.BlockSpec((NDEV, OR, D), lambda i: (0, 0, 0))],
        out_specs=[row, pl.BlockSpec((ts, LW), lambda i: (i, 0)), pl.BlockSpec((4, ts, PC), lambda i: (0, i, 0)),
                   full((8, D)), full((8, GW)), full((HEADS * CHUNK, CHUNK)), full((CHUNK, GW))],
        out_shape=[jax.ShapeDtypeStruct((S, D), BF16), jax.ShapeDtypeStruct((S, LW), F32),
                   jax.ShapeDtypeStruct((4, S, PC), BF16), jax.ShapeDtypeStruct((8, D), F32),
                   jax.ShapeDtypeStruct((8, GW), F32), jax.ShapeDtypeStruct((HEADS * CHUNK, CHUNK), F32),
                   jax.ShapeDtypeStruct((CHUNK, GW), F32)],
        compiler_params=_cparams("arbitrary"),
    )(dxo, proj, proj, ylru, fo, vec, gp, wcat, wcat_t, bz, pavg, wout_g)


def final_loss(x, target, gain):
    S = x.shape[0]
    ts = min(512, S)

    def body(x_ref, t_ref, g_ref, loss_ref, dx_ref, dg_ref):
        @pl.when(pl.program_id(0) == 0)
        def _():
            loss_ref[...] = jnp.zeros_like(loss_ref)
            dg_ref[...] = jnp.zeros_like(dg_ref)

        xv = x_ref[...]
        gain_v = g_ref[0:1, :]
        rstd = lax.rsqrt(_rmean(xv * xv) + EPS)
        xhat = xv * rstd
        err = xhat * gain_v - t_ref[...]
        loss_ref[...] += 0.5 * _csum(_rmean(err * err))
        dy = err * (1.0 / D)
        dg_ref[0:1, :] += _csum(dy * xhat)
        dxhat = dy * gain_v
        dx_ref[...] = rstd * (dxhat - xhat * _rmean(dxhat * xhat))

    row = pl.BlockSpec((ts, D), lambda i: (i, 0))
    return pl.pallas_call(
        body, name="final_loss",
        grid=(S // ts,),
        in_specs=[row, row, pl.BlockSpec((8, D), lambda i: (0, 0))],
        out_specs=[pl.BlockSpec((8, 128), lambda i: (0, 0)), row, pl.BlockSpec((8, D), lambda i: (0, 0))],
        out_shape=[jax.ShapeDtypeStruct((8, 128), F32), jax.ShapeDtypeStruct((S, D), F32),
                   jax.ShapeDtypeStruct((8, D), F32)],
        compiler_params=_cparams("arbitrary"),
    )(x, target, gain)


def _vec(mod_l, j, gain):
    return jnp.concatenate([mod_l[3 * j:3 * j + 3], gain[None, :], jnp.zeros((4, D), F32)], axis=0)


def _block_diag_tiles(w):
    w4 = w.reshape(LW // LC, 2, HD, HD)
    eye2 = jnp.eye(2, dtype=w.dtype)
    return (w4[:, :, :, None, :] * eye2[None, :, None, :, None]).reshape(LW // LC, LC, LC).astype(BF16)


def _block_diag_extract(dw):
    d5 = dw.reshape(LW // LC, 2, HD, 2, HD)
    return jnp.einsum('cihkj,ik->cihj', d5, jnp.eye(2, dtype=dw.dtype)).reshape(HEADS, HD, HD)


def _layer_params(l, p, conv_w_full):
    lp = jnp.concatenate([conv_w_full[l], p['conv_b'][l][None], p['gate_a_b'][l].reshape(1, LW),
                          p['gate_x_b'][l].reshape(1, LW), p['lru_lambda'][l][None]], axis=0)
    gp = jnp.concatenate([p['v_norm'][l][None], p['lru_out_norm'][l][None], p['gmlp_out_norm'][l][None],
                          jnp.zeros((5, GW), F32)], axis=0)
    ws = p['spatial_w'][l] * jnp.tril(jnp.ones((CHUNK, CHUNK), F32))
    wcat = ws.transpose(1, 0, 2).reshape(CHUNK, HEADS * CHUNK).astype(BF16)
    wcat_t = ws.transpose(2, 0, 1).reshape(CHUNK, HEADS * CHUNK).astype(BF16)
    bz = jnp.repeat(p['spatial_b'][l].T, HD, axis=1)
    return dict(lp=lp, gp=gp, wcat=wcat, wcat_t=wcat_t, bz=bz,
                wa_t=_block_diag_tiles(p['gate_a_w'][l]), wx_t=_block_diag_tiles(p['gate_x_w'][l]))


def _pavg():
    return jnp.kron(jnp.eye(2, dtype=F32), jnp.full((HD, HD), 1.0 / HD, F32)).astype(BF16)


GATHER_RIDES = {
    ('ffn_a', 0): [('w_in', 0), ('gu', DEPTH)],
    ('mix_in', 0): [('w_out', 0)],
    ('lru', 0): [('down', DEPTH)],
    ('mix_out', 0): [('down', 1)],
    ('ffn_b', 0): [('gu', 1), ('w_in', 1)],
    ('ffn_a', 1): [('gu', DEPTH + 1), ('w_out', 1)],
    ('mix_in', 1): [('down', DEPTH + 1)],
}


def local_fwd_bwd(me_arr, x, target, mod, p, loc, gathered, conv_w_full):
    pavg = _pavg()
    g = dict(gathered)

    def ride(call, l):
        todo = GATHER_RIDES.get((call, l))
        return None if todo is None else (todo, GatherRide([(loc[kind], slot) for kind, slot in todo]))

    def run(fn, call, l, *args):
        r = ride(call, l)
        outs, got = fn(*args, ride=None if r is None else r[1])
        if r is not None:
            g.update(dict(zip(r[0], got)))
        return outs

    saved = []
    h = x
    for l in range(DEPTH):
        q = _layer_params(l, p, conv_w_full)
        v1 = _vec(mod[l], 0, p['ffn1_norm'][l])
        vm = _vec(mod[l], 1, p['mix_norm'][l])
        v2 = _vec(mod[l], 2, p['ffn2_norm'][l])
        x0 = h
        x1, h1, gu1, f1 = run(ffn_fwd, 'ffn_a', l, x0, v1, g['gu', l], g['down', l], f"a{l}")
        hm, proj = run(mix_in_fwd, 'mix_in', l, x1, vm, g['w_in', l], f"{l}")
        ylru, xc, hst = run(lru_fwd, 'lru', l, proj, q['lp'], q['wa_t'], q['wx_t'], f"{l}")
        x2, y, fo = run(mix_out_fwd, 'mix_out', l, proj, ylru, x1, vm, q['gp'], q['wcat'], q['bz'], pavg,
                        g['w_out', l], f"{l}")
        x3, h2, gu2, f2 = run(ffn_fwd, 'ffn_b', l, x2, v2, g['gu', DEPTH + l], g['down', DEPTH + l], f"b{l}")
        saved.append(dict(q=q, v1=v1, vm=vm, v2=v2, x0=x0, x1=x1, x2=x2, h1=h1, gu1=gu1, f1=f1, hm=hm, proj=proj,
                          ylru=ylru, xc=xc, hst=hst, y=y, fo=fo, h2=h2, gu2=gu2, f2=f2))
        h = x3
    fin = jnp.concatenate([p['final_norm'][None], jnp.zeros((7, D), F32)], axis=0)
    loss8, dx, dfin = final_loss(h, target, fin)
    loss = loss8[0, 0]

    big = dict(gu=None, down=None, w_in=None, w_out=None)
    small = {k: [None] * DEPTH for k in ('ffn1_norm', 'mix_norm', 'ffn2_norm', 'conv_w', 'conv_b', 'gate_a_w',
                                         'gate_a_b', 'gate_x_w', 'gate_x_b', 'lru_lambda', 'v_norm', 'spatial_w',
                                         'spatial_b', 'lru_out_norm', 'gmlp_out_norm')}
    dmod = [None] * DEPTH
    tril = jnp.tril(jnp.ones((CHUNK, CHUNK), F32))
    for l in reversed(range(DEPTH)):
        sv = saved[l]
        q = sv['q']
        dx2, dgu, a, df, acc2 = ffn_bwd(dx, sv['x2'], sv['gu2'], sv['f2'], sv['v2'],
                                        g['gu', DEPTH + l], g['down', DEPTH + l], f"b{l}")
        big['gu'] = tn_matmul_scatter(me_arr, dgu, sv['h2'][None], DEPTH + l, 2 * DEPTH, big['gu'], f"dw_gu_b{l}")
        big['down'] = tn_matmul_scatter(me_arr, a, df[None], DEPTH + l, 2 * DEPTH, big['down'], f"dw_down_b{l}", split=2)
        dyo, dylru, duv, accmo, dgp, dwm, dbz = mix_out_bwd(dx2, sv['proj'], sv['ylru'], sv['fo'], sv['vm'], q['gp'],
                                                             q['wcat'], q['wcat_t'], q['bz'], pavg, g['w_out', l], f"{l}")
        big['w_out'] = tn_matmul_scatter(me_arr, sv['y'][None], dyo[None], l, DEPTH, big['w_out'], f"dw_out_{l}",
                                         split=NDEV)
        dxl, dgl, dlp, dwa, dwx = lru_bwd(dylru, sv['proj'], sv['xc'], sv['hst'], q['lp'], q['wa_t'], q['wx_t'], f"{l}")
        dproj = jnp.concatenate([dxl, dgl, duv], axis=0)
        dx1, accmi = mix_in_bwd(dproj, sv['x1'], dx2, sv['vm'], g['w_in', l], f"{l}")
        big['w_in'] = tn_matmul_scatter(me_arr, sv['hm'][None], dproj, l, DEPTH, big['w_in'], f"dw_in_{l}")
        dx0, dgu, a, df, acc1 = ffn_bwd(dx1, sv['x0'], sv['gu1'], sv['f1'], sv['v1'],
                                        g['gu', l], g['down', l], f"a{l}")
        big['gu'] = tn_matmul_scatter(me_arr, dgu, sv['h1'][None], l, 2 * DEPTH, big['gu'], f"dw_gu_a{l}")
        big['down'] = tn_matmul_scatter(me_arr, a, df[None], l, 2 * DEPTH, big['down'], f"dw_down_a{l}", split=2)
        dx = dx0
        dmod[l] = jnp.concatenate([acc1[0:3], accmi[0:2], accmo[2:3], acc2[0:3]], axis=0)
        small['ffn1_norm'][l] = acc1[3]
        small['mix_norm'][l] = accmi[3]
        small['ffn2_norm'][l] = acc2[3]
        small['conv_w'][l] = dlp[0:4]
        small['conv_b'][l] = dlp[4]
        small['gate_a_b'][l] = dlp[5].reshape(HEADS, HD)
        small['gate_x_b'][l] = dlp[6].reshape(HEADS, HD)
        small['lru_lambda'][l] = dlp[7]
        small['gate_a_w'][l] = _block_diag_extract(dwa)
        small['gate_x_w'][l] = _block_diag_extract(dwx)
        small['v_norm'][l] = dgp[0]
        small['lru_out_norm'][l] = dgp[1]
        small['gmlp_out_norm'][l] = dgp[2]
        small['spatial_w'][l] = dwm.reshape(HEADS, CHUNK, CHUNK) * tril
        small['spatial_b'][l] = dbz.reshape(CHUNK, HEADS, HD).sum(-1).T
    small = {k: jnp.stack(v) for k, v in small.items()}
    small['final_norm'] = dfin[0]
    return loss, dx, big, small, jnp.stack(dmod)


def ada_fwd(c_all, w_ada, b_loc):
    def body(c_ref, w_ref, b_ref, mod_ref, sc_ref):
        cv = c_ref[...]
        sc = cv * _sigmoid(cv)
        sc_ref[...] = sc
        mod_ref[...] = _dot3(sc, w_ref[...]) + b_ref[...]

    return pl.pallas_call(
        body, name="ada_fwd",
        grid=(DEPTH,),
        in_specs=[pl.BlockSpec((NDEV, D), lambda l: (0, 0)), pl.BlockSpec((None, D, AC), lambda l: (l, 0, 0)),
                  pl.BlockSpec((None, 1, AC), lambda l: (l, 0, 0))],
        out_specs=[pl.BlockSpec((None, NDEV, AC), lambda l: (l, 0, 0)), pl.BlockSpec((NDEV, D), lambda l: (0, 0))],
        out_shape=[jax.ShapeDtypeStruct((DEPTH, NDEV, AC), F32), jax.ShapeDtypeStruct((NDEV, D), F32)],
        compiler_params=_cparams("arbitrary"),
    )(c_all, w_ada, b_loc)


def ada_bwd(sc_t, dmod_cols):
    def body(sc_ref, dm_ref, g_ref):
        sc = sc_ref[...]
        dm = dm_ref[...]
        acc = sc[:, 0:1] * dm[0:1, :]
        for b in range(1, NDEV):
            acc = acc + sc[:, b:b + 1] * dm[b:b + 1, :]
        g_ref[...] = acc

    return pl.pallas_call(
        body, name="ada_bwd",
        grid=(DEPTH,),
        in_specs=[pl.BlockSpec((D, NDEV), lambda l: (0, 0)), pl.BlockSpec((None, NDEV, AC), lambda l: (l, 0, 0))],
        out_specs=pl.BlockSpec((None, None, D, AC), lambda l: (0, l, 0, 0)),
        out_shape=jax.ShapeDtypeStruct((1, DEPTH, D, AC), F32),
        compiler_params=_cparams("arbitrary"),
    )(sc_t, dmod_cols)


def _row_tile(rows, cols):
    if rows * cols <= 512 * 1024:
        return rows
    for tr in (512, 384, 352, 256, 128, 64, 32, 16, 8):
        if rows % tr == 0:
            return tr
    return rows


def adamw(gparts, slot0, w, m, v, name):
    P, _, R, C = gparts.shape
    L = w.shape[0]
    tr = _row_tile(R, C)

    def body(g_ref, w_ref, m_ref, v_ref, go_ref, do_ref, mo_ref, vo_ref):
        g = g_ref[0].astype(F32)
        for p in range(1, P):
            g = g + g_ref[p].astype(F32)
        go_ref[...] = g
        mn = ADAM_B1 * m_ref[...] + (1.0 - ADAM_B1) * g
        vn = ADAM_B2 * v_ref[...] + (1.0 - ADAM_B2) * (g * g)
        mo_ref[...] = mn
        vo_ref[...] = vn
        m_hat = mn / (1.0 - ADAM_B1 ** ADAM_STEP)
        v_hat = vn / (1.0 - ADAM_B2 ** ADAM_STEP)
        do_ref[...] = -ADAM_LR * (m_hat / (jnp.sqrt(v_hat) + ADAM_EPS) + ADAM_WD * w_ref[...])

    blk = pl.BlockSpec((None, tr, C), lambda l, i: (l, i, 0))
    return pl.pallas_call(
        body, name=name,
        grid=(L, R // tr),
        in_specs=[pl.BlockSpec((P, None, tr, C), lambda l, i: (0, slot0 + l, i, 0)), blk, blk, blk],
        out_specs=[blk, blk, blk, blk],
        out_shape=[jax.ShapeDtypeStruct((L, R, C), F32)] * 4,
        compiler_params=_cparams("arbitrary", "arbitrary"),
    )(gparts, w, m, v)


def sum_parts(parts):
    P, R, C = parts.shape

    def body(p_ref, o_ref):
        acc = p_ref[0]
        for p in range(1, P):
            acc = acc + p_ref[p]
        o_ref[...] = acc

    return pl.pallas_call(
        body, name="sum_parts",
        in_specs=[pl.BlockSpec(memory_space=pltpu.VMEM)],
        out_specs=pl.BlockSpec(memory_space=pltpu.VMEM),
        out_shape=jax.ShapeDtypeStruct((R, C), F32),
    )(parts)


WEIGHTS = ['w_ada', 'b_ada', 'ffn1_norm', 'ffn1_w_gu', 'ffn1_w_down', 'mix_norm', 'w_in', 'conv_w', 'conv_b',
           'gate_a_w', 'gate_a_b', 'gate_x_w', 'gate_x_b', 'lru_lambda', 'v_norm', 'spatial_w', 'spatial_b',
           'lru_out_norm', 'gmlp_out_norm', 'w_out', 'ffn2_norm', 'ffn2_w_gu', 'ffn2_w_down', 'final_norm']
PACKED = ['b_ada', 'ffn1_norm', 'mix_norm', 'conv_b', 'gate_a_w', 'gate_a_b', 'gate_x_w', 'gate_x_b', 'lru_lambda',
          'v_norm', 'spatial_w', 'spatial_b', 'lru_out_norm', 'gmlp_out_norm', 'ffn2_norm', 'final_norm', 'conv_w']
PACK_LANES = 128
PACK_ROW_ALIGN = 8 * NDEV


def _pack(d):
    parts = [d[k].reshape(-1, PACK_LANES).astype(F32) for k in PACKED]
    used = sum(p.shape[0] for p in parts)
    rows = -(-used // PACK_ROW_ALIGN) * PACK_ROW_ALIGN
    return jnp.concatenate(parts + [jnp.zeros((rows - used, PACK_LANES), F32)], axis=0)


def _unpack(buf, shapes):
    out, off = {}, 0
    for k in PACKED:
        size = 1
        for s in shapes[k]:
            size *= s
        nrows = size // PACK_LANES
        out[k] = buf[off:off + nrows].reshape(shapes[k])
        off += nrows
    return out


def kernel(x, c, w_ada, b_ada, ffn1_norm, ffn1_w_gu, ffn1_w_down, mix_norm, w_in, conv_w, conv_b, gate_a_w, gate_a_b, gate_x_w, gate_x_b, lru_lambda, v_norm, spatial_w, spatial_b, lru_out_norm, gmlp_out_norm, w_out, ffn2_norm, ffn2_w_gu, ffn2_w_down, final_norm, loss_target, m_w_ada, m_b_ada, m_ffn1_norm, m_ffn1_w_gu, m_ffn1_w_down, m_mix_norm, m_w_in, m_conv_w, m_conv_b, m_gate_a_w, m_gate_a_b, m_gate_x_w, m_gate_x_b, m_lru_lambda, m_v_norm, m_spatial_w, m_spatial_b, m_lru_out_norm, m_gmlp_out_norm, m_w_out, m_ffn2_norm, m_ffn2_w_gu, m_ffn2_w_down, m_final_norm, v_w_ada, v_b_ada, v_ffn1_norm, v_ffn1_w_gu, v_ffn1_w_down, v_mix_norm, v_w_in, v_conv_w, v_conv_b, v_gate_a_w, v_gate_a_b, v_gate_x_w, v_gate_x_b, v_lru_lambda, v_v_norm, v_spatial_w, v_spatial_b, v_lru_out_norm, v_gmlp_out_norm, v_w_out, v_ffn2_norm, v_ffn2_w_gu, v_ffn2_w_down, v_final_norm):
    w = dict(w_ada=w_ada, b_ada=b_ada, ffn1_norm=ffn1_norm, ffn1_w_gu=ffn1_w_gu, ffn1_w_down=ffn1_w_down, mix_norm=mix_norm, w_in=w_in, conv_w=conv_w, conv_b=conv_b, gate_a_w=gate_a_w, gate_a_b=gate_a_b, gate_x_w=gate_x_w, gate_x_b=gate_x_b, lru_lambda=lru_lambda, v_norm=v_norm, spatial_w=spatial_w, spatial_b=spatial_b, lru_out_norm=lru_out_norm, gmlp_out_norm=gmlp_out_norm, w_out=w_out, ffn2_norm=ffn2_norm, ffn2_w_gu=ffn2_w_gu, ffn2_w_down=ffn2_w_down, final_norm=final_norm)
    m = dict(w_ada=m_w_ada, b_ada=m_b_ada, ffn1_norm=m_ffn1_norm, ffn1_w_gu=m_ffn1_w_gu, ffn1_w_down=m_ffn1_w_down, mix_norm=m_mix_norm, w_in=m_w_in, conv_w=m_conv_w, conv_b=m_conv_b, gate_a_w=m_gate_a_w, gate_a_b=m_gate_a_b, gate_x_w=m_gate_x_w, gate_x_b=m_gate_x_b, lru_lambda=m_lru_lambda, v_norm=m_v_norm, spatial_w=m_spatial_w, spatial_b=m_spatial_b, lru_out_norm=m_lru_out_norm, gmlp_out_norm=m_gmlp_out_norm, w_out=m_w_out, ffn2_norm=m_ffn2_norm, ffn2_w_gu=m_ffn2_w_gu, ffn2_w_down=m_ffn2_w_down, final_norm=m_final_norm)
    v = dict(w_ada=v_w_ada, b_ada=v_b_ada, ffn1_norm=v_ffn1_norm, ffn1_w_gu=v_ffn1_w_gu, ffn1_w_down=v_ffn1_w_down, mix_norm=v_mix_norm, w_in=v_w_in, conv_w=v_conv_w, conv_b=v_conv_b, gate_a_w=v_gate_a_w, gate_a_b=v_gate_a_b, gate_x_w=v_gate_x_w, gate_x_b=v_gate_x_b, lru_lambda=v_lru_lambda, v_norm=v_v_norm, spatial_w=v_spatial_w, spatial_b=v_spatial_b, lru_out_norm=v_lru_out_norm, gmlp_out_norm=v_gmlp_out_norm, w_out=v_w_out, ffn2_norm=v_ffn2_norm, ffn2_w_gu=v_ffn2_w_gu, ffn2_w_down=v_ffn2_w_down, final_norm=v_final_norm)
    me = 4 * lax.axis_index("x") + 2 * lax.axis_index("y") + lax.axis_index("c")

    loc = dict(gu=jnp.concatenate([ffn1_w_gu, ffn2_w_gu], axis=0).astype(BF16),
               down=jnp.concatenate([ffn1_w_down, ffn2_w_down], axis=0).astype(BF16),
               w_in=w_in.astype(BF16), w_out=w_out.astype(BF16))
    c_g, conv_g, gu0, down0 = all_gather([(c, None), (conv_w, None), (loc['gu'], 0), (loc['down'], 0)], "gather_first")
    conv_w_full = conv_g.transpose(1, 2, 0, 3).reshape(DEPTH, CONV_WIDTH, LW)

    b_loc = lax.dynamic_slice(b_ada, (0, me * AC), (DEPTH, AC)).reshape(DEPTH, 1, AC)
    mod_cols, sc_all = ada_fwd(c_g.reshape(NDEV, D), w_ada, b_loc)
    (mod_rows,) = all_to_all([mod_cols.transpose(1, 0, 2)], "scatter_mod")
    mod = mod_rows.transpose(1, 0, 2).reshape(DEPTH, NMOD, D)

    small_w = {k: w[k] for k in PACKED if k != 'conv_w'}
    me_arr = jnp.reshape(me, (1,)).astype(jnp.int32)
    loss_loc, dx, big, small_g, dmod = local_fwd_bwd(me_arr, x[0], loss_target[0], mod, small_w, loc,
                                                     {('gu', 0): gu0, ('down', 0): down0}, conv_w_full)
    loss = lax.psum(loss_loc, ("x", "y", "c"))

    small_g['b_ada'] = dmod.reshape(DEPTH, NMOD * D)
    gpack = _pack(small_g)
    rows = gpack.shape[0]
    dmod_out = dmod.reshape(DEPTH, NDEV, AC).transpose(1, 0, 2)
    dmod_r, pack_r = all_to_all([dmod_out, gpack.reshape(NDEV, rows // NDEV, PACK_LANES)], "scatter_grads")
    (gsum_g,) = all_gather([(sum_parts(pack_r), None)], "gather_small_grads")
    gsum = gsum_g.reshape(1, 1, rows, PACK_LANES)

    res = {}
    t = lambda a: a.transpose(0, 2, 1)
    gu_t = big['gu']
    res['ffn1_w_gu'] = tuple(t(r) for r in adamw(gu_t, 0, t(w['ffn1_w_gu']), t(m['ffn1_w_gu']), t(v['ffn1_w_gu']),
                                                 "adamw_gu_a"))
    res['ffn2_w_gu'] = tuple(t(r) for r in adamw(gu_t, DEPTH, t(w['ffn2_w_gu']), t(m['ffn2_w_gu']),
                                                 t(v['ffn2_w_gu']), "adamw_gu_b"))
    res['ffn1_w_down'] = adamw(big['down'], 0, w['ffn1_w_down'], m['ffn1_w_down'], v['ffn1_w_down'], "adamw_down_a")
    res['ffn2_w_down'] = adamw(big['down'], DEPTH, w['ffn2_w_down'], m['ffn2_w_down'], v['ffn2_w_down'], "adamw_down_b")
    res['w_in'] = adamw(big['w_in'], 0, w['w_in'], m['w_in'], v['w_in'], "adamw_w_in")
    res['w_out'] = adamw(big['w_out'], 0, w['w_out'], m['w_out'], v['w_out'], "adamw_w_out")
    g_ada = ada_bwd(sc_all.T, dmod_r.transpose(1, 0, 2))
    res['w_ada'] = adamw(g_ada, 0, w['w_ada'], m['w_ada'], v['w_ada'], "adamw_w_ada")
    shapes = {k: w[k].shape for k in PACKED}
    shapes['conv_w'] = (DEPTH, CONV_WIDTH, LW)
    dummy = jnp.zeros(shapes['conv_w'], F32)
    packs = adamw(gsum, 0, _pack({**small_w, 'conv_w': dummy})[None], _pack({**{k: m[k] for k in small_w}, 'conv_w': dummy})[None],
                  _pack({**{k: v[k] for k in small_w}, 'conv_w': dummy})[None], "adamw_small")
    unpacked = [_unpack(b[0], shapes) for b in packs]
    for k in small_w:
        res[k] = tuple(u[k] for u in unpacked)
    gconv = lax.dynamic_slice(unpacked[0]['conv_w'], (0, 0, me * (LW // NDEV)), (DEPTH, CONV_WIDTH, LW // NDEV))
    cshape = (1, DEPTH * CONV_WIDTH, LW // NDEV)
    rc = adamw(gconv.reshape((1,) + cshape), 0, conv_w.reshape(cshape), m['conv_w'].reshape(cshape),
               v['conv_w'].reshape(cshape), "adamw_conv_w")
    res['conv_w'] = tuple(r.reshape(conv_w.shape) for r in rc)

    return (loss, dx[None], *[res[k][0] for k in WEIGHTS], *[res[k][1] for k in WEIGHTS],
            *[res[k][2] for k in WEIGHTS], *[res[k][3] for k in WEIGHTS])
```

```python
import jax
import jax.numpy as jnp
from jax import lax
from jax.experimental import pallas as pl
from jax.experimental.pallas import tpu as pltpu

F32 = jnp.float32
BF16 = jnp.bfloat16

NDEV = 8
DEPTH = 2
D = 1024
DFF = 2816
FC = 2 * DFF // NDEV
NCHUNK = DFF // FC
DR = DFF // NDEV
LW = 512
GW = 512
HD = 64
HEADS = 8
CHUNK = 128
PC = 2 * (LW + GW) // NDEV
OR = D // NDEV
NMOD = 9
AC = NMOD * D // NDEV
LC = 128
EPS = 1e-6
RG_LRU_C = 8.0
CONV_WIDTH = 4

ADAM_LR = 0.001
ADAM_B1 = 0.9
ADAM_B2 = 0.999
ADAM_EPS = 1e-08
ADAM_WD = 0.01
ADAM_STEP = 10

VMEM_LIMIT_BYTES = 60 * 1024 * 1024
MESH = pl.DeviceIdType.MESH
ANY = pl.BlockSpec(memory_space=pl.ANY)


def _cparams(*sem):
    return pltpu.CompilerParams(dimension_semantics=tuple(sem) if sem else None,
                                vmem_limit_bytes=VMEM_LIMIT_BYTES)


def _dot(a, b):
    return jnp.dot(a, b, preferred_element_type=F32)


def _dot_nt(a, b):
    return lax.dot_general(a, b, (((1,), (1,)), ((), ())), preferred_element_type=F32)


def _dot_tn(a, b):
    return lax.dot_general(a, b, (((0,), (0,)), ((), ())), preferred_element_type=F32)


def _split(a):
    hi = a.astype(BF16)
    lo = (a - hi.astype(F32)).astype(BF16)
    return hi, lo


def _dot3(a, b):
    ah, al = _split(a)
    bh, bl = _split(b)
    return _dot(ah, bh) + (_dot(ah, bl) + _dot(al, bh))


def _csum(a):
    return jnp.sum(a, axis=0, keepdims=True)


def _rmean(a):
    return jnp.mean(a, axis=-1, keepdims=True)


def _sigmoid(a):
    return 1.0 / (1.0 + jnp.exp(-a))


_GELU_K = 0.7978845608028654
_GELU_C = 0.044715


def _gelu(a):
    return 0.5 * a * (1.0 + jnp.tanh(_GELU_K * (a + _GELU_C * a * a * a)))


def _gelu_and_grad(a):
    a2 = a * a
    t = jnp.tanh(_GELU_K * (a + _GELU_C * a2 * a))
    half = 0.5 * (1.0 + t)
    return a * half, half + 0.5 * a * (1.0 - t * t) * (_GELU_K * (1.0 + 3.0 * _GELU_C * a2))


def _norm_mod(x, gain, scale, shift):
    rstd = lax.rsqrt(_rmean(x * x) + EPS)
    return (x * rstd * gain) * (1.0 + scale) + shift


def _norm_mod_bwd(dh, x, gain, scale):
    rstd = lax.rsqrt(_rmean(x * x) + EPS)
    xhat = x * rstd
    dshift = _csum(dh)
    dscale = _csum(dh * (xhat * gain))
    dhn = dh * (1.0 + scale)
    dgain = _csum(dhn * xhat)
    dxhat = dhn * gain
    dx = rstd * (dxhat - xhat * _rmean(dxhat * xhat))
    return dx, dshift, dscale, dgain


def _rms(x, gain):
    rstd = lax.rsqrt(_rmean(x * x) + EPS)
    return x * rstd * gain


def _rms_bwd(dy, x, gain):
    rstd = lax.rsqrt(_rmean(x * x) + EPS)
    xhat = x * rstd
    dgain = _csum(dy * xhat)
    dxhat = dy * gain
    return rstd * (dxhat - xhat * _rmean(dxhat * xhat)), dgain


PAIR = 2 * HD


def _seg_mean(a, pavg):
    hi, lo = _split(a)
    return jnp.concatenate([_dot(hi[:, p:p + PAIR], pavg) + _dot(lo[:, p:p + PAIR], pavg)
                            for p in range(0, a.shape[1], PAIR)], axis=1)


def _block_copies(src_hbm, dst_vmem, sems, rows):
    copies = []
    for k in range(NDEV):
        dst = dst_vmem.at[k] if rows is None else dst_vmem.at[pl.ds(k * rows, rows)]
        copies.append(pltpu.make_async_copy(src_hbm.at[k], dst, sems.at[k]))
    return copies


def _ffn_weight_fetch(wgu_hbm, wd_hbm, wgu_v, wd_v, sems):
    @pl.when(pl.program_id(0) == 0)
    def _():
        copies = _block_copies(wgu_hbm, wgu_v, sems.at[0], None) + _block_copies(wd_hbm, wd_v, sems.at[1], DR)
        for cp in copies:
            cp.start()
        for cp in copies:
            cp.wait()


def _place():
    return lax.axis_index("x"), lax.axis_index("y"), lax.axis_index("c")


def _slot(p):
    return 4 * p[0] + 2 * p[1] + p[2]


class GatherRide:
    def __init__(self, srcs):
        self.n = len(srcs)
        self.index = [i for _, i in srcs]
        self.args = [a for a, _ in srcs]
        self.out_shape = [jax.ShapeDtypeStruct((NDEV,) + (a.shape if i is None else a.shape[1:]), a.dtype)
                          for a, i in srcs]
        self.scratch = [pltpu.SemaphoreType.DMA((self.n, NDEV - 1)), pltpu.SemaphoreType.DMA((self.n, NDEV - 1)),
                        pltpu.SemaphoreType.DMA((self.n,))]

    def hooks(self, ins, outs, sems):
        send_sems, recv_sems, local_sems = sems
        n = self.n
        x, y, c = _place()
        me, sibling = (x, y, c), (x, y, 1 - c)
        chips = [(1 - x, y), (x, 1 - y), (1 - x, 1 - y)]

        def local(a):
            return ins[a] if self.index[a] is None else ins[a].at[self.index[a]]

        def copy(a, k, block, to, src=None):
            dst = outs[a].at[_slot(block)]
            return pltpu.make_async_remote_copy(
                src_ref=dst if src is None else src, dst_ref=dst,
                send_sem=send_sems.at[a, k], recv_sem=recv_sems.at[a, k],
                device_id=to, device_id_type=MESH)

        def mine():
            return [pltpu.make_async_copy(local(a), outs[a].at[_slot(me)], local_sems.at[a]) for a in range(n)]

        def first():
            cps = []
            for a in range(n):
                cps.append(copy(a, 0, me, sibling, src=local(a)))
                cps += [copy(a, 1 + j, me, (*chip, c), src=local(a)) for j, chip in enumerate(chips)]
            return cps

        def passed():
            return [copy(a, 4 + j, (*chip, c), sibling) for j, chip in enumerate(chips) for a in range(n)]

        def start():
            for cp in mine() + first():
                cp.start()

        def mid():
            for j, chip in enumerate(chips):
                for a in range(n):
                    copy(a, 1 + j, (*chip, c), me).wait_recv()
                    copy(a, 4 + j, (*chip, c), sibling).start()

        def finish():
            for a in range(n):
                copy(a, 0, sibling, me).wait_recv()
                for j, chip in enumerate(chips):
                    copy(a, 4 + j, (*chip, 1 - c), me).wait_recv()
            for cp in first() + passed():
                cp.wait_send()
            for cp in mine():
                cp.wait()

        return start, mid, finish


def all_gather(srcs, name):
    ride = GatherRide(srcs)
    n = ride.n

    def body(*refs):
        start, mid, finish = ride.hooks(refs[:n], refs[n:2 * n], refs[2 * n:])
        start()
        mid()
        finish()

    return pl.pallas_call(
        body, name=name,
        in_specs=[ANY] * n, out_specs=[ANY] * n, out_shape=ride.out_shape, scratch_shapes=ride.scratch,
    )(*ride.args)


def _call(core, ride, *, name, grid, in_specs, out_specs, out_shape, scratch_shapes, args):
    if ride is None:
        outs = pl.pallas_call(core, name=name, grid=grid, in_specs=in_specs, out_specs=out_specs,
                              out_shape=out_shape, scratch_shapes=scratch_shapes,
                              compiler_params=_cparams("arbitrary"))(*args)
        return outs, []
    n_in, n_out, n_sc, n = len(in_specs), len(out_shape), len(scratch_shapes), ride.n
    nsteps = grid[0]
    mid_step = max(nsteps - 2, 0)

    def body(*refs):
        cuts = [n_in, n_in + n, n_in + n + n_out, n_in + 2 * n + n_out, n_in + 2 * n + n_out + n_sc]
        ci, ri, co, ro, cs, rs = (refs[a:b] for a, b in zip([0] + cuts, cuts + [len(refs)]))
        start, mid, finish = ride.hooks(ri, ro, rs)
        i = pl.program_id(0)
        pl.when(i == 0)(start)
        core(*ci, *co, *cs)
        pl.when(i == mid_step)(mid)
        pl.when(i == nsteps - 1)(finish)

    outs = pl.pallas_call(
        body, name=name, grid=grid,
        in_specs=list(in_specs) + [ANY] * n, out_specs=list(out_specs) + [ANY] * n,
        out_shape=list(out_shape) + ride.out_shape, scratch_shapes=list(scratch_shapes) + ride.scratch,
        compiler_params=_cparams("arbitrary"))(*args, *ride.args)
    return outs[:n_out], outs[n_out:]


def all_to_all(arrs, name):
    n = len(arrs)

    def body(*refs):
        ins, outs = refs[:n], refs[n:2 * n]
        send_sems, recv_sems, local_sems = refs[2 * n:]
        x, y, c = _place()
        me = (x, y, c)

        def peer(k):
            return (1 - x if k & 4 else x, 1 - y if k & 2 else y, 1 - c if k & 1 else c)

        def copy(a, k):
            return pltpu.make_async_remote_copy(
                src_ref=ins[a].at[_slot(peer(k))], dst_ref=outs[a].at[_slot(me)],
                send_sem=send_sems.at[a, k - 1], recv_sem=recv_sems.at[a, k - 1],
                device_id=peer(k), device_id_type=MESH)

        def landing(a, k):
            return pltpu.make_async_remote_copy(
                src_ref=outs[a].at[_slot(peer(k))], dst_ref=outs[a].at[_slot(peer(k))],
                send_sem=send_sems.at[a, k - 1], recv_sem=recv_sems.at[a, k - 1],
                device_id=me, device_id_type=MESH)

        mine = [pltpu.make_async_copy(ins[a].at[_slot(me)], outs[a].at[_slot(me)], local_sems.at[a]) for a in range(n)]
        for cp in mine:
            cp.start()
        sends = [copy(a, k) for a in range(n) for k in range(1, NDEV)]
        for cp in sends:
            cp.start()
        for a in range(n):
            for k in range(1, NDEV):
                landing(a, k).wait_recv()
        for cp in sends:
            cp.wait_send()
        for cp in mine:
            cp.wait()

    return pl.pallas_call(
        body, name=name,
        in_specs=[ANY] * n, out_specs=[ANY] * n,
        out_shape=[jax.ShapeDtypeStruct(a.shape, a.dtype) for a in arrs],
        scratch_shapes=[pltpu.SemaphoreType.DMA((n, NDEV - 1)), pltpu.SemaphoreType.DMA((n, NDEV - 1)),
                        pltpu.SemaphoreType.DMA((n,))],
    )(*arrs)


FFN_TS = 256
FFN_FWD_TS = 512


def ffn_fwd(x, vec, wgu_g, wdown_g, tag, ride=None):
    S = x.shape[0]
    ts = min(FFN_FWD_TS, S)

    def body(x_ref, vec_ref, wgu_hbm, wd_hbm, xo_ref, h_ref, gu_ref, f_ref, wgu_v, wd_v, sems):
        _ffn_weight_fetch(wgu_hbm, wd_hbm, wgu_v, wd_v, sems)
        xv = x_ref[...]
        h = _norm_mod(xv, vec_ref[3:4, :], vec_ref[1:2, :], vec_ref[0:1, :]).astype(BF16)
        h_ref[...] = h
        acc = jnp.zeros((ts, D), F32)
        for j in range(NCHUNK):
            g = _dot(h, wgu_v[j])
            u = _dot(h, wgu_v[NCHUNK + j])
            gu_ref[j] = g.astype(BF16)
            gu_ref[NCHUNK + j] = u.astype(BF16)
            a = (g * _sigmoid(g) * u).astype(BF16)
            acc = acc + _dot(a, wd_v[pl.ds(j * FC, FC), :])
        f_ref[...] = acc.astype(BF16)
        xo_ref[...] = xv + (0.5 * vec_ref[2:3, :]) * acc

    return _call(
        body, ride, name=f"ffn_fwd_{tag}",
        grid=(S // ts,),
        in_specs=[pl.BlockSpec((ts, D), lambda i: (i, 0)),
                  pl.BlockSpec((8, D), lambda i: (0, 0)), ANY, ANY],
        out_specs=[pl.BlockSpec((ts, D), lambda i: (i, 0)),
                   pl.BlockSpec((ts, D), lambda i: (i, 0)),
                   pl.BlockSpec((NDEV, ts, FC), lambda i: (0, i, 0)),
                   pl.BlockSpec((ts, D), lambda i: (i, 0))],
        out_shape=[jax.ShapeDtypeStruct((S, D), F32), jax.ShapeDtypeStruct((S, D), BF16),
                   jax.ShapeDtypeStruct((NDEV, S, FC), BF16), jax.ShapeDtypeStruct((S, D), BF16)],
        scratch_shapes=[pltpu.VMEM((NDEV, D, FC), BF16), pltpu.VMEM((DFF, D), BF16),
                        pltpu.SemaphoreType.DMA((2, NDEV))],
        args=(x, vec, wgu_g, wdown_g))


def ffn_bwd(dxo, x, gu, f, vec, wgu_g, wdown_g, tag):
    S = x.shape[0]
    ts = min(FFN_TS, S)

    def body(dxo_ref, x_ref, gu_ref, f_ref, vec_ref, wgu_hbm, wd_hbm,
             dx_ref, dgu_ref, a_ref, df_ref, acc_ref, wgu_v, wd_v, sems):
        _ffn_weight_fetch(wgu_hbm, wd_hbm, wgu_v, wd_v, sems)

        @pl.when(pl.program_id(0) == 0)
        def _():
            acc_ref[...] = jnp.zeros_like(acc_ref)

        dxo_v = dxo_ref[...]
        dgate = 0.5 * _csum(dxo_v * f_ref[...].astype(F32))
        df = ((0.5 * vec_ref[2:3, :]) * dxo_v).astype(BF16)
        df_ref[...] = df
        dh = jnp.zeros((ts, D), F32)
        for j in range(NCHUNK):
            da = _dot_nt(df, wd_v[pl.ds(j * FC, FC), :])
            g = gu_ref[j].astype(F32)
            u = gu_ref[NCHUNK + j].astype(F32)
            sg = _sigmoid(g)
            si = g * sg
            a_ref[j] = (si * u).astype(BF16)
            dg = (da * u * (sg * (1.0 + g * (1.0 - sg)))).astype(BF16)
            du = (da * si).astype(BF16)
            dgu_ref[j] = dg
            dgu_ref[NCHUNK + j] = du
            dh = dh + _dot_nt(dg, wgu_v[j]) + _dot_nt(du, wgu_v[NCHUNK + j])
        dx, dshift, dscale, dgain = _norm_mod_bwd(dh, x_ref[...], vec_ref[3:4, :], vec_ref[1:2, :])
        dx_ref[...] = dx + dxo_v
        acc_ref[0:1, :] += dshift
        acc_ref[1:2, :] += dscale
        acc_ref[2:3, :] += dgate
        acc_ref[3:4, :] += dgain

    row = pl.BlockSpec((ts, D), lambda i: (i, 0))
    return pl.pallas_call(
        body, name=f"ffn_bwd_{tag}",
        grid=(S // ts,),
        in_specs=[row, row, pl.BlockSpec((NDEV, ts, FC), lambda i: (0, i, 0)), row,
                  pl.BlockSpec((8, D), lambda i: (0, 0)), ANY, ANY],
        out_specs=[row, pl.BlockSpec((NDEV, ts, FC), lambda i: (0, i, 0)),
                   pl.BlockSpec((NCHUNK, ts, FC), lambda i: (0, i, 0)), row,
                   pl.BlockSpec((8, D), lambda i: (0, 0))],
        out_shape=[jax.ShapeDtypeStruct((S, D), F32), jax.ShapeDtypeStruct((NDEV, S, FC), BF16),
                   jax.ShapeDtypeStruct((NCHUNK, S, FC), BF16), jax.ShapeDtypeStruct((S, D), BF16),
                   jax.ShapeDtypeStruct((8, D), F32)],
        scratch_shapes=[pltpu.VMEM((NDEV, D, FC), BF16), pltpu.VMEM((DFF, D), BF16),
                        pltpu.SemaphoreType.DMA((2, NDEV))],
        compiler_params=_cparams("arbitrary"),
    )(dxo, x, gu, f, vec, wgu_g, wdown_g)


NCHIP = NDEV // 2


def tn_matmul_scatter(me_arr, a, b, slot, nslots, prev, name, split=1):
    na, S, M = a.shape
    nb, _, N = b.shape
    ncall = NDEV // split
    ts = min(4096, S)
    nsteps = S // ts
    mp = M // split
    other_step = {1: lambda j: 2 * j, 2: lambda j: j, 8: lambda j: 0}[split]
    mine_step = {1: lambda j: 2 * j + 1, 2: lambda j: min(j + 1, ncall - 1), 8: lambda j: 0}[split]
    keeps = split == 2

    def group(k, me_ref):
        if split == 1:
            return jnp.bitwise_xor(me_ref[0], NDEV - 1 - k)
        if split == 2:
            return jnp.bitwise_xor(me_ref[0] // 2, NCHIP - 1 - k)
        return 0

    def body(me_ref, *refs):
        a_ref, b_ref = refs[0], refs[1]
        recv_ref, acc, sb_other, sb_mine, land, keep, d2d_send, d2d_recv, ici_send, ici_recv = refs[-10:]
        k = pl.program_id(0)
        s = pl.program_id(1)
        x, y, c = _place()
        my_chip = 2 * x + y

        def chip_of(j):
            if split == 8:
                cx, cy = j // 2, j % 2
            else:
                flip = NCHIP - 1 - j
                cx, cy = (1 - x if flip & 2 else x), (1 - y if flip & 1 else y)
            return cx, cy, 2 * cx + cy

        def piece(j, core):
            if split == 1:
                return acc[...]
            start = core * mp if split == 2 else (2 * j + core) * mp
            return acc[pl.ds(pl.multiple_of(start, 8), mp), :]

        def to_sibling(j):
            return pltpu.make_async_remote_copy(
                src_ref=sb_other.at[j], dst_ref=land.at[j], send_sem=d2d_send.at[j], recv_sem=d2d_recv.at[j],
                device_id=(x, y, 1 - c), device_id_type=MESH)

        def to_owner(j):
            cx, cy, ci = chip_of(j)
            dst = recv_ref.at[my_chip, slot]
            return ci, pltpu.make_async_copy(sb_mine.at[j], dst, ici_send.at[j]), pltpu.make_async_remote_copy(
                src_ref=sb_mine.at[j], dst_ref=dst, send_sem=ici_send.at[j], recv_sem=ici_recv.at[my_chip],
                device_id=(cx, cy, c), device_id_type=MESH)

        if nsteps == 1:
            acc[...] = _dot_tn(a_ref[...], b_ref[...])
        else:
            @pl.when(s == 0)
            def _():
                acc[...] = jnp.zeros_like(acc)

            acc[...] += _dot_tn(a_ref[...], b_ref[...])

        for kk in range(ncall):
            @pl.when((s == nsteps - 1) & (k == kk))
            def _():
                for j in range(NCHIP):
                    if other_step(j) == kk:
                        sb_other[j] = piece(j, 1 - c).astype(BF16)
                        to_sibling(j).start()
                        if keeps:
                            keep[j] = piece(j, c)
                for j in range(NCHIP):
                    if mine_step(j) == kk:
                        to_sibling(j).wait_recv()
                        own = keep[j] if keeps else piece(j, c)
                        sb_mine[j] = (own + land[j].astype(F32)).astype(BF16)
                        ci, loc, rem = to_owner(j)
                        pl.when(ci == my_chip)(loc.start)
                        pl.when(ci != my_chip)(rem.start)

        @pl.when((s == nsteps - 1) & (k == ncall - 1))
        def _():
            for j in range(NCHIP):
                to_sibling(j).wait_send()
                ci, loc, rem = to_owner(j)
                pl.when(ci == my_chip)(loc.wait)
                pl.when(ci != my_chip)(rem.wait_send)
            for src in range(NCHIP):
                @pl.when(my_chip != src)
                def _():
                    pltpu.make_async_remote_copy(
                        src_ref=recv_ref.at[src, slot], dst_ref=recv_ref.at[src, slot],
                        send_sem=ici_send.at[src], recv_sem=ici_recv.at[src],
                        device_id=(src // 2, src % 2, c), device_id_type=MESH).wait_recv()

    in_specs = [pl.BlockSpec((None, ts, M), (lambda k, s, me: (group(k, me), s, 0)) if na > 1 else (lambda k, s, me: (0, s, 0))),
                pl.BlockSpec((None, ts, N), (lambda k, s, me: (group(k, me), s, 0)) if nb > 1 else (lambda k, s, me: (0, s, 0)))]
    args = [me_arr, a, b]
    aliases = {}
    if prev is not None:
        in_specs.append(ANY)
        args.append(prev)
        aliases = {3: 0}
    return pl.pallas_call(
        body, name=name,
        grid_spec=pltpu.PrefetchScalarGridSpec(
            num_scalar_prefetch=1, grid=(ncall, nsteps), in_specs=in_specs, out_specs=ANY,
            scratch_shapes=[pltpu.VMEM((M, N), F32), pltpu.VMEM((NCHIP, mp, N), BF16), pltpu.VMEM((NCHIP, mp, N), BF16),
                            pltpu.VMEM((NCHIP, mp, N), BF16), pltpu.VMEM((NCHIP, mp, N) if keeps else (1, 8, 128), F32),
                            pltpu.SemaphoreType.DMA((NCHIP,)),
                            pltpu.SemaphoreType.DMA((NCHIP,)), pltpu.SemaphoreType.DMA((NCHIP,)),
                            pltpu.SemaphoreType.DMA((NCHIP,))]),
        out_shape=jax.ShapeDtypeStruct((NCHIP, nslots, mp, N), BF16),
        input_output_aliases=aliases,
        compiler_params=_cparams("arbitrary", "arbitrary"),
    )(*args)


MIX_TS = 256
MIX_IN_TS = 512


def mix_in_fwd(x, vec, win_g, tag, ride=None):
    S = x.shape[0]
    ts = min(MIX_IN_TS, S)

    def body(x_ref, vec_ref, win_ref, hm_ref, proj_ref):
        h = _norm_mod(x_ref[...], vec_ref[3:4, :], vec_ref[1:2, :], vec_ref[0:1, :]).astype(BF16)
        hm_ref[...] = h
        for k in range(NDEV):
            proj_ref[k] = _dot(h, win_ref[k])

    return _call(
        body, ride, name=f"mix_in_fwd_{tag}",
        grid=(S // ts,),
        in_specs=[pl.BlockSpec((ts, D), lambda i: (i, 0)), pl.BlockSpec((8, D), lambda i: (0, 0)),
                  pl.BlockSpec((NDEV, D, PC), lambda i: (0, 0, 0))],
        out_specs=[pl.BlockSpec((ts, D), lambda i: (i, 0)),
                   pl.BlockSpec((NDEV, ts, PC), lambda i: (0, i, 0))],
        out_shape=[jax.ShapeDtypeStruct((S, D), BF16), jax.ShapeDtypeStruct((NDEV, S, PC), F32)],
        scratch_shapes=[], args=(x, vec, win_g))


def mix_in_bwd(dproj, x, dxo, vec, win_g, tag):
    S = x.shape[0]
    ts = min(MIX_IN_TS, S)

    def body(dp_ref, x_ref, dxo_ref, vec_ref, win_ref, dx_ref, acc_ref):
        @pl.when(pl.program_id(0) == 0)
        def _():
            acc_ref[...] = jnp.zeros_like(acc_ref)

        dh = jnp.zeros((ts, D), F32)
        for k in range(NDEV):
            dh = dh + _dot_nt(dp_ref[k], win_ref[k])
        dx, dshift, dscale, dgain = _norm_mod_bwd(dh, x_ref[...], vec_ref[3:4, :], vec_ref[1:2, :])
        dx_ref[...] = dx + dxo_ref[...]
        acc_ref[0:1, :] += dshift
        acc_ref[1:2, :] += dscale
        acc_ref[3:4, :] += dgain

    row = pl.BlockSpec((ts, D), lambda i: (i, 0))
    return pl.pallas_call(
        body, name=f"mix_in_bwd_{tag}",
        grid=(S // ts,),
        in_specs=[pl.BlockSpec((NDEV, ts, PC), lambda i: (0, i, 0)), row, row,
                  pl.BlockSpec((8, D), lambda i: (0, 0)),
                  pl.BlockSpec((NDEV, D, PC), lambda i: (0, 0, 0))],
        out_specs=[row, pl.BlockSpec((8, D), lambda i: (0, 0))],
        out_shape=[jax.ShapeDtypeStruct((S, D), F32), jax.ShapeDtypeStruct((8, D), F32)],
        compiler_params=_cparams("arbitrary"),
    )(dproj, x, dxo, vec, win_g)


SCAN_UNROLL = 4


def _shift_down(z, k, row):
    return jnp.where(row >= k, pltpu.roll(z, k, 0), 0.0)


def _shift_up(z, k, row, n):
    return jnp.where(row < n - k, pltpu.roll(z, n - k, 0), 0.0)


def _lru_gates(xc, lp_ref, wa_ref, wx_ref):
    xcb = xc.astype(BF16)
    ra = _sigmoid(_dot(xcb, wa_ref[...]) + lp_ref[5:6, :])
    ix = _sigmoid(_dot(xcb, wx_ref[...]) + lp_ref[6:7, :])
    lam = lp_ref[7:8, :]
    ls = jnp.minimum(lam, 0.0) - jnp.log(1.0 + jnp.exp(-jnp.abs(lam)))
    log_a = (RG_LRU_C * ls) * ra
    a = jnp.exp(log_a)
    mult = jnp.sqrt(-jnp.tanh(log_a) * (a * a + 1.0))
    return ra, ix, ls, a, mult


def _conv(x, lp_ref, row):
    return (lp_ref[4:5, :] + lp_ref[3:4, :] * x + lp_ref[2:3, :] * _shift_down(x, 1, row)
            + lp_ref[1:2, :] * _shift_down(x, 2, row) + lp_ref[0:1, :] * _shift_down(x, 3, row))


def lru_fwd(proj, lp, wa_t, wx_t, tag, ride=None):
    S = proj.shape[1]
    nblk = S // 8

    def body(x_ref, g_ref, lp_ref, wa_ref, wx_ref, y_ref, xc_ref, h_ref, a_s, b_s):
        x = x_ref[...]
        row = lax.broadcasted_iota(jnp.int32, x.shape, 0)
        xc = _conv(x, lp_ref, row)
        xc_ref[...] = xc
        ra, ix, ls, a, mult = _lru_gates(xc, lp_ref, wa_ref, wx_ref)
        a_s[...] = a
        b_s[...] = mult * (ix * xc)
        rowb = lax.broadcasted_iota(jnp.int32, (8, LC), 0)

        def step(i, carry):
            for q in range(SCAN_UNROLL):
                r0 = pl.multiple_of((i * SCAN_UNROLL + q) * 8, 8)
                A = a_s[pl.ds(r0, 8), :]
                B = b_s[pl.ds(r0, 8), :]
                for d in (1, 2, 4):
                    m = rowb >= d
                    As = jnp.where(m, pltpu.roll(A, d, 0), 1.0)
                    Bs = jnp.where(m, pltpu.roll(B, d, 0), 0.0)
                    B = A * Bs + B
                    A = A * As
                H = B + A * carry
                h_ref[pl.ds(r0, 8), :] = H
                carry = H[7:8, :]
            return carry

        lax.fori_loop(0, nblk // SCAN_UNROLL, step, jnp.zeros((1, LC), F32))
        y_ref[...] = h_ref[...] * _gelu(g_ref[...])

    col = pl.BlockSpec((S, LC), lambda c: (0, c))
    return _call(
        body, ride, name=f"lru_fwd_{tag}",
        grid=(LW // LC,),
        in_specs=[pl.BlockSpec((None, S, LC), lambda c: (c // 2, 0, c % 2)),
                  pl.BlockSpec((None, S, LC), lambda c: (2 + c // 2, 0, c % 2)),
                  pl.BlockSpec((8, LC), lambda c: (0, c)),
                  pl.BlockSpec((None, LC, LC), lambda c: (c, 0, 0)),
                  pl.BlockSpec((None, LC, LC), lambda c: (c, 0, 0))],
        out_specs=[col, col, col],
        out_shape=[jax.ShapeDtypeStruct((S, LW), F32)] * 3,
        scratch_shapes=[pltpu.VMEM((S, LC), F32), pltpu.VMEM((S, LC), F32)],
        args=(proj, proj, lp, wa_t, wx_t))


def lru_bwd(dy, proj, xc_all, hst, lp, wa_t, wx_t, tag):
    S = proj.shape[1]
    nblk = S // 8

    def body(dy_ref, x_ref, g_ref, xc_ref, h_ref, lp_ref, wa_ref, wx_ref,
             dx_ref, dg_ref, dlp_ref, dwa_ref, dwx_ref, c_s, l_s):
        xc = xc_ref[...]
        row = lax.broadcasted_iota(jnp.int32, xc.shape, 0)
        ra, ix, ls, a, mult = _lru_gates(xc, lp_ref, wa_ref, wx_ref)
        g = g_ref[...]
        dyv = dy_ref[...]
        h = h_ref[...]
        gelu_g, gelu_grad_g = _gelu_and_grad(g)
        dg_ref[...] = (dyv * h * gelu_grad_g).astype(BF16)
        c_s[...] = _shift_up(a, 1, row, S)
        l_s[...] = dyv * gelu_g
        rowb = lax.broadcasted_iota(jnp.int32, (8, LC), 0)

        def step(i, carry):
            for q in range(SCAN_UNROLL):
                r0 = pl.multiple_of((nblk - 1 - (i * SCAN_UNROLL + q)) * 8, 8)
                C = c_s[pl.ds(r0, 8), :]
                L = l_s[pl.ds(r0, 8), :]
                for d in (1, 2, 4):
                    m = rowb < 8 - d
                    Cs = jnp.where(m, pltpu.roll(C, 8 - d, 0), 1.0)
                    Ls = jnp.where(m, pltpu.roll(L, 8 - d, 0), 0.0)
                    L = C * Ls + L
                    C = C * Cs
                L = L + C * carry
                l_s[pl.ds(r0, 8), :] = L
                carry = L[0:1, :]
            return carry

        lax.fori_loop(0, nblk // SCAN_UNROLL, step, jnp.zeros((1, LC), F32))
        db = l_s[...]
        da = db * _shift_down(h, 1, row)
        ixc = ix * xc
        dmult = db * ixc
        dix = db * (mult * xc)
        dxc = db * (mult * ix)
        dlog_a = da * a - dmult * (a * a) / mult
        dra = dlog_a * (RG_LRU_C * ls)
        dls = _csum(dlog_a * ra) * RG_LRU_C
        lam = lp_ref[7:8, :]
        dlam = dls * _sigmoid(-lam)
        dpa = dra * ra * (1.0 - ra)
        dpx = dix * ix * (1.0 - ix)
        dpab = dpa.astype(BF16)
        dpxb = dpx.astype(BF16)
        xcb = xc.astype(BF16)
        dwa_ref[...] = _dot_tn(xcb, dpab)
        dwx_ref[...] = _dot_tn(xcb, dpxb)
        dxc = dxc + _dot_nt(dpab, wa_ref[...]) + _dot_nt(dpxb, wx_ref[...])
        x = x_ref[...]
        dlp_ref[0:1, :] = _csum(dxc * _shift_down(x, 3, row))
        dlp_ref[1:2, :] = _csum(dxc * _shift_down(x, 2, row))
        dlp_ref[2:3, :] = _csum(dxc * _shift_down(x, 1, row))
        dlp_ref[3:4, :] = _csum(dxc * x)
        dlp_ref[4:5, :] = _csum(dxc)
        dlp_ref[5:6, :] = _csum(dpa)
        dlp_ref[6:7, :] = _csum(dpx)
        dlp_ref[7:8, :] = dlam
        dx = (lp_ref[3:4, :] * dxc + lp_ref[2:3, :] * _shift_up(dxc, 1, row, S)
              + lp_ref[1:2, :] * _shift_up(dxc, 2, row, S) + lp_ref[0:1, :] * _shift_up(dxc, 3, row, S))
        dx_ref[...] = dx.astype(BF16)

    col = pl.BlockSpec((S, LC), lambda c: (0, c))
    pcol = pl.BlockSpec((None, S, LC), lambda c: (c // 2, 0, c % 2))
    return pl.pallas_call(
        body, name=f"lru_bwd_{tag}",
        grid=(LW // LC,),
        in_specs=[col, pcol, pl.BlockSpec((None, S, LC), lambda c: (2 + c // 2, 0, c % 2)), col, col,
                  pl.BlockSpec((8, LC), lambda c: (0, c)),
                  pl.BlockSpec((None, LC, LC), lambda c: (c, 0, 0)),
                  pl.BlockSpec((None, LC, LC), lambda c: (c, 0, 0))],
        out_specs=[pcol, pcol, pl.BlockSpec((8, LC), lambda c: (0, c)),
                   pl.BlockSpec((None, LC, LC), lambda c: (c, 0, 0)),
                   pl.BlockSpec((None, LC, LC), lambda c: (c, 0, 0))],
        out_shape=[jax.ShapeDtypeStruct((2, S, PC), BF16), jax.ShapeDtypeStruct((2, S, PC), BF16),
                   jax.ShapeDtypeStruct((8, LW), F32),
                   jax.ShapeDtypeStruct((LW // LC, LC, LC), F32), jax.ShapeDtypeStruct((LW // LC, LC, LC), F32)],
        scratch_shapes=[pltpu.VMEM((S, LC), F32), pltpu.VMEM((S, LC), F32)],
        compiler_params=_cparams("arbitrary"),
    )(dy, proj, proj, xc_all, hst, lp, wa_t, wx_t)


def _pair_stack(zp, low):
    return jnp.concatenate([jnp.where(low, zp, 0.0), jnp.where(low, 0.0, zp)], axis=0).astype(BF16)


def _spatial(w_ref, zc, low):
    return jnp.concatenate(
        [_dot(w_ref[:, 2 * p * CHUNK:2 * (p + 1) * CHUNK], _pair_stack(zc[:, p * PAIR:(p + 1) * PAIR], low))
         for p in range(GW // PAIR)], axis=1)


def _gmlp_fwd_parts(u, v, gp_ref, wcat_ref, bz_ref, pavg_ref, ts, with_grad=False):
    if with_grad:
        ug, ugrad = _gelu_and_grad(u)
        vg, vgrad = _gelu_and_grad(v)
    else:
        ug, vg, ugrad, vgrad = _gelu(u), _gelu(v), None, None
    pavg = pavg_ref[...]
    vc = vg - _seg_mean(vg, pavg)
    rs = lax.rsqrt(_seg_mean(vc * vc, pavg) + EPS)
    vhat = vc * rs
    vh = vhat * gp_ref[0:1, :]
    low = lax.broadcasted_iota(jnp.int32, (CHUNK, PAIR), 1) < HD
    zs = [_spatial(wcat_ref, vh[n * CHUNK:(n + 1) * CHUNK, :], low) + bz_ref[...] for n in range(ts // CHUNK)]
    z = jnp.concatenate(zs, axis=0) if len(zs) > 1 else zs[0]
    return ug, rs, vhat, vh, z, ugrad, vgrad


def mix_out_fwd(proj, ylru, x, vec, gp, wcat, bz, pavg, wout_g, tag, ride=None):
    S = x.shape[0]
    ts = min(MIX_TS, S)

    def body(u_ref, v_ref, yl_ref, x_ref, vec_ref, gp_ref, wcat_ref, bz_ref, pavg_ref, wout_ref,
             xo_ref, y_ref, fo_ref):
        u = jnp.concatenate([u_ref[0], u_ref[1]], axis=1)
        v = jnp.concatenate([v_ref[0], v_ref[1]], axis=1)
        ug, _, _, _, z, _, _ = _gmlp_fwd_parts(u, v, gp_ref, wcat_ref, bz_ref, pavg_ref, ts)
        n1 = _rms(yl_ref[...], gp_ref[1:2, :])
        n2 = _rms(ug * z, gp_ref[2:3, :])
        y = jnp.concatenate([n1, n2], axis=1).astype(BF16)
        y_ref[...] = y
        fo = jnp.zeros((ts, D), F32)
        for k in range(NDEV):
            fo = fo + _dot(y[:, k * OR:(k + 1) * OR], wout_ref[k])
        fo_ref[...] = fo.astype(BF16)
        xo_ref[...] = x_ref[...] + vec_ref[2:3, :] * fo

    row = pl.BlockSpec((ts, D), lambda i: (i, 0))
    full = lambda shp: pl.BlockSpec(shp, lambda i: tuple(0 for _ in shp))
    return _call(
        body, ride, name=f"mix_out_fwd_{tag}",
        grid=(S // ts,),
        in_specs=[pl.BlockSpec((2, ts, PC), lambda i: (2, i, 0)), pl.BlockSpec((2, ts, PC), lambda i: (3, i, 0)),
                  pl.BlockSpec((ts, LW), lambda i: (i, 0)), row, full((8, D)), full((8, GW)),
                  full((CHUNK, HEADS * CHUNK)), full((CHUNK, GW)), full((PAIR, PAIR)),
                  pl.BlockSpec((NDEV, OR, D), lambda i: (0, 0, 0))],
        out_specs=[row, row, row],
        out_shape=[jax.ShapeDtypeStruct((S, D), F32), jax.ShapeDtypeStruct((S, D), BF16),
                   jax.ShapeDtypeStruct((S, D), BF16)],
        scratch_shapes=[], args=(proj, proj, ylru, x, vec, gp, wcat, bz, pavg, wout_g))


def mix_out_bwd(dxo, proj, ylru, fo, vec, gp, wcat, wcat_t, bz, pavg, wout_g, tag):
    S = dxo.shape[0]
    ts = min(MIX_TS, S)

    def body(dxo_ref, u_ref, v_ref, yl_ref, fo_ref, vec_ref, gp_ref, wcat_ref, wcatt_ref, bz_ref, pavg_ref,
             wout_ref, dyo_ref, dyl_ref, duv_ref, acc_ref, dgp_ref, dwm_ref, dbz_ref):
        @pl.when(pl.program_id(0) == 0)
        def _():
            acc_ref[...] = jnp.zeros_like(acc_ref)
            dgp_ref[...] = jnp.zeros_like(dgp_ref)
            dwm_ref[...] = jnp.zeros_like(dwm_ref)
            dbz_ref[...] = jnp.zeros_like(dbz_ref)

        dxo_v = dxo_ref[...]
        acc_ref[2:3, :] += _csum(dxo_v * fo_ref[...].astype(F32))
        dyo = (vec_ref[2:3, :] * dxo_v).astype(BF16)
        dyo_ref[...] = dyo
        dn = [_dot_nt(dyo, wout_ref[k]) for k in range(NDEV)]
        dn1 = jnp.concatenate(dn[:NDEV // 2], axis=1)
        dn2 = jnp.concatenate(dn[NDEV // 2:], axis=1)
        dyl, dg1 = _rms_bwd(dn1, yl_ref[...], gp_ref[1:2, :])
        dyl_ref[...] = dyl
        u = jnp.concatenate([u_ref[0], u_ref[1]], axis=1)
        v = jnp.concatenate([v_ref[0], v_ref[1]], axis=1)
        ug, rs, vhat, vh, z, ugrad, vgrad = _gmlp_fwd_parts(u, v, gp_ref, wcat_ref, bz_ref, pavg_ref, ts,
                                                            with_grad=True)
        dyg, dg2 = _rms_bwd(dn2, ug * z, gp_ref[2:3, :])
        du = (dyg * z) * ugrad
        dz = dyg * ug
        low = lax.broadcasted_iota(jnp.int32, (CHUNK, PAIR), 1) < HD
        vhb = vh.astype(BF16)
        dvhs = []
        dbz = jnp.zeros((CHUNK, GW), F32)
        dwm = [jnp.zeros((2 * CHUNK, CHUNK), F32) for _ in range(GW // PAIR)]
        for n in range(ts // CHUNK):
            dzc = dz[n * CHUNK:(n + 1) * CHUNK, :]
            dbz = dbz + dzc
            for p in range(GW // PAIR):
                stack = _pair_stack(dzc[:, p * PAIR:(p + 1) * PAIR], low)
                dwm[p] = dwm[p] + _dot_nt(stack, vhb[n * CHUNK:(n + 1) * CHUNK, p * PAIR:(p + 1) * PAIR])
            dvhs.append(_spatial(wcatt_ref, dzc, low))
        dbz_ref[...] += dbz
        for p in range(GW // PAIR):
            dwm_ref[2 * p * CHUNK:2 * (p + 1) * CHUNK, :] += dwm[p]
        dvh = jnp.concatenate(dvhs, axis=0) if len(dvhs) > 1 else dvhs[0]
        pavg = pavg_ref[...]
        dvn = _csum(dvh * vhat)
        dvhat = dvh * gp_ref[0:1, :]
        dvg = rs * (dvhat - _seg_mean(dvhat, pavg) - vhat * _seg_mean(dvhat * vhat, pavg))
        dv = dvg * vgrad
        duv_ref[0] = du[:, :PC].astype(BF16)
        duv_ref[1] = du[:, PC:].astype(BF16)
        duv_ref[2] = dv[:, :PC].astype(BF16)
        duv_ref[3] = dv[:, PC:].astype(BF16)
        dgp_ref[0:1, :] += dvn
        dgp_ref[1:2, :] += dg1
        dgp_ref[2:3, :] += dg2

    row = pl.BlockSpec((ts, D), lambda i: (i, 0))
    full = lambda shp: pl.BlockSpec(shp, lambda i: tuple(0 for _ in shp))
    return pl.pallas_call(
        body, name=f"mix_out_bwd_{tag}",
        grid=(S // ts,),
        in_specs=[row, pl.BlockSpec((2, ts, PC), lambda i: (2, i, 0)), pl.BlockSpec((2, ts, PC), lambda i: (3, i, 0)),
                  pl.BlockSpec((ts, LW), lambda i: (i, 0)), row, full((8, D)), full((8, GW)),
                  full((CHUNK, HEADS * CHUNK)), full((CHUNK, HEADS * CHUNK)), full((CHUNK, GW)), full((PAIR, PAIR)),
                  pl.BlockSpec((NDEV, OR, D), lambda i: (0, 0, 0))],
        out_specs=[row, pl.BlockSpec((ts, LW), lambda i: (i, 0)), pl.BlockSpec((4, ts, PC), lambda i: (0, i, 0)),
                   full((8, D)), full((8, GW)), full((HEADS * CHUNK, CHUNK)), full((CHUNK, GW))],
        out_shape=[jax.ShapeDtypeStruct((S, D), BF16), jax.ShapeDtypeStruct((S, LW), F32),
                   jax.ShapeDtypeStruct((4, S, PC), BF16), jax.ShapeDtypeStruct((8, D), F32),
                   jax.ShapeDtypeStruct((8, GW), F32), jax.ShapeDtypeStruct((HEADS * CHUNK, CHUNK), F32),
                   jax.ShapeDtypeStruct((CHUNK, GW), F32)],
        compiler_params=_cparams("arbitrary"),
    )(dxo, proj, proj, ylru, fo, vec, gp, wcat, wcat_t, bz, pavg, wout_g)


def final_loss(x, target, gain):
    S = x.shape[0]
    ts = min(512, S)

    def body(x_ref, t_ref, g_ref, loss_ref, dx_ref, dg_ref):
        @pl.when(pl.program_id(0) == 0)
        def _():
            loss_ref[...] = jnp.zeros_like(loss_ref)
            dg_ref[...] = jnp.zeros_like(dg_ref)

        xv = x_ref[...]
        gain_v = g_ref[0:1, :]
        rstd = lax.rsqrt(_rmean(xv * xv) + EPS)
        xhat = xv * rstd
        err = xhat * gain_v - t_ref[...]
        loss_ref[...] += 0.5 * _csum(_rmean(err * err))
        dy = err * (1.0 / D)
        dg_ref[0:1, :] += _csum(dy * xhat)
        dxhat = dy * gain_v
        dx_ref[...] = rstd * (dxhat - xhat * _rmean(dxhat * xhat))

    row = pl.BlockSpec((ts, D), lambda i: (i, 0))
    return pl.pallas_call(
        body, name="final_loss",
        grid=(S // ts,),
        in_specs=[row, row, pl.BlockSpec((8, D), lambda i: (0, 0))],
        out_specs=[pl.BlockSpec((8, 128), lambda i: (0, 0)), row, pl.BlockSpec((8, D), lambda i: (0, 0))],
        out_shape=[jax.ShapeDtypeStruct((8, 128), F32), jax.ShapeDtypeStruct((S, D), F32),
                   jax.ShapeDtypeStruct((8, D), F32)],
        compiler_params=_cparams("arbitrary"),
    )(x, target, gain)


def _vec(mod_l, j, gain):
    return jnp.concatenate([mod_l[3 * j:3 * j + 3], gain[None, :], jnp.zeros((4, D), F32)], axis=0)


def _block_diag_tiles(w):
    w4 = w.reshape(LW // LC, 2, HD, HD)
    eye2 = jnp.eye(2, dtype=w.dtype)
    return (w4[:, :, :, None, :] * eye2[None, :, None, :, None]).reshape(LW // LC, LC, LC).astype(BF16)


def _block_diag_extract(dw):
    d5 = dw.reshape(LW // LC, 2, HD, 2, HD)
    return jnp.einsum('cihkj,ik->cihj', d5, jnp.eye(2, dtype=dw.dtype)).reshape(HEADS, HD, HD)


def _layer_params(l, p, conv_w_full):
    lp = jnp.concatenate([conv_w_full[l], p['conv_b'][l][None], p['gate_a_b'][l].reshape(1, LW),
                          p['gate_x_b'][l].reshape(1, LW), p['lru_lambda'][l][None]], axis=0)
    gp = jnp.concatenate([p['v_norm'][l][None], p['lru_out_norm'][l][None], p['gmlp_out_norm'][l][None],
                          jnp.zeros((5, GW), F32)], axis=0)
    ws = p['spatial_w'][l] * jnp.tril(jnp.ones((CHUNK, CHUNK), F32))
    wcat = ws.transpose(1, 0, 2).reshape(CHUNK, HEADS * CHUNK).astype(BF16)
    wcat_t = ws.transpose(2, 0, 1).reshape(CHUNK, HEADS * CHUNK).astype(BF16)
    bz = jnp.repeat(p['spatial_b'][l].T, HD, axis=1)
    return dict(lp=lp, gp=gp, wcat=wcat, wcat_t=wcat_t, bz=bz,
                wa_t=_block_diag_tiles(p['gate_a_w'][l]), wx_t=_block_diag_tiles(p['gate_x_w'][l]))


def _pavg():
    return jnp.kron(jnp.eye(2, dtype=F32), jnp.full((HD, HD), 1.0 / HD, F32)).astype(BF16)


GATHER_RIDES = {
    ('ffn_a', 0): [('w_in', 0), ('gu', DEPTH)],
    ('mix_in', 0): [('w_out', 0)],
    ('lru', 0): [('down', DEPTH)],
    ('mix_out', 0): [('down', 1)],
    ('ffn_b', 0): [('gu', 1), ('w_in', 1)],
    ('ffn_a', 1): [('gu', DEPTH + 1), ('w_out', 1)],
    ('mix_in', 1): [('down', DEPTH + 1)],
}


def local_fwd_bwd(me_arr, x, target, mod, p, loc, gathered, conv_w_full):
    pavg = _pavg()
    g = dict(gathered)

    def ride(call, l):
        todo = GATHER_RIDES.get((call, l))
        return None if todo is None else (todo, GatherRide([(loc[kind], slot) for kind, slot in todo]))

    def run(fn, call, l, *args):
        r = ride(call, l)
        outs, got = fn(*args, ride=None if r is None else r[1])
        if r is not None:
            g.update(dict(zip(r[0], got)))
        return outs

    saved = []
    h = x
    for l in range(DEPTH):
        q = _layer_params(l, p, conv_w_full)
        v1 = _vec(mod[l], 0, p['ffn1_norm'][l])
        vm = _vec(mod[l], 1, p['mix_norm'][l])
        v2 = _vec(mod[l], 2, p['ffn2_norm'][l])
        x0 = h
        x1, h1, gu1, f1 = run(ffn_fwd, 'ffn_a', l, x0, v1, g['gu', l], g['down', l], f"a{l}")
        hm, proj = run(mix_in_fwd, 'mix_in', l, x1, vm, g['w_in', l], f"{l}")
        ylru, xc, hst = run(lru_fwd, 'lru', l, proj, q['lp'], q['wa_t'], q['wx_t'], f"{l}")
        x2, y, fo = run(mix_out_fwd, 'mix_out', l, proj, ylru, x1, vm, q['gp'], q['wcat'], q['bz'], pavg,
                        g['w_out', l], f"{l}")
        x3, h2, gu2, f2 = run(ffn_fwd, 'ffn_b', l, x2, v2, g['gu', DEPTH + l], g['down', DEPTH + l], f"b{l}")
        saved.append(dict(q=q, v1=v1, vm=vm, v2=v2, x0=x0, x1=x1, x2=x2, h1=h1, gu1=gu1, f1=f1, hm=hm, proj=proj,
                          ylru=ylru, xc=xc, hst=hst, y=y, fo=fo, h2=h2, gu2=gu2, f2=f2))
        h = x3
    fin = jnp.concatenate([p['final_norm'][None], jnp.zeros((7, D), F32)], axis=0)
    loss8, dx, dfin = final_loss(h, target, fin)
    loss = loss8[0, 0]

    big = dict(gu=None, down=None, w_in=None, w_out=None)
    small = {k: [None] * DEPTH for k in ('ffn1_norm', 'mix_norm', 'ffn2_norm', 'conv_w', 'conv_b', 'gate_a_w',
                                         'gate_a_b', 'gate_x_w', 'gate_x_b', 'lru_lambda', 'v_norm', 'spatial_w',
                                         'spatial_b', 'lru_out_norm', 'gmlp_out_norm')}
    dmod = [None] * DEPTH
    tril = jnp.tril(jnp.ones((CHUNK, CHUNK), F32))
    for l in reversed(range(DEPTH)):
        sv = saved[l]
        q = sv['q']
        dx2, dgu, a, df, acc2 = ffn_bwd(dx, sv['x2'], sv['gu2'], sv['f2'], sv['v2'],
                                        g['gu', DEPTH + l], g['down', DEPTH + l], f"b{l}")
        big['gu'] = tn_matmul_scatter(me_arr, dgu, sv['h2'][None], DEPTH + l, 2 * DEPTH, big['gu'], f"dw_gu_b{l}")
        big['down'] = tn_matmul_scatter(me_arr, a, df[None], DEPTH + l, 2 * DEPTH, big['down'], f"dw_down_b{l}", split=2)
        dyo, dylru, duv, accmo, dgp, dwm, dbz = mix_out_bwd(dx2, sv['proj'], sv['ylru'], sv['fo'], sv['vm'], q['gp'],
                                                             q['wcat'], q['wcat_t'], q['bz'], pavg, g['w_out', l], f"{l}")
        big['w_out'] = tn_matmul_scatter(me_arr, sv['y'][None], dyo[None], l, DEPTH, big['w_out'], f"dw_out_{l}",
                                         split=NDEV)
        dxl, dgl, dlp, dwa, dwx = lru_bwd(dylru, sv['proj'], sv['xc'], sv['hst'], q['lp'], q['wa_t'], q['wx_t'], f"{l}")
        dproj = jnp.concatenate([dxl, dgl, duv], axis=0)
        dx1, accmi = mix_in_bwd(dproj, sv['x1'], dx2, sv['vm'], g['w_in', l], f"{l}")
        big['w_in'] = tn_matmul_scatter(me_arr, sv['hm'][None], dproj, l, DEPTH, big['w_in'], f"dw_in_{l}")
        dx0, dgu, a, df, acc1 = ffn_bwd(dx1, sv['x0'], sv['gu1'], sv['f1'], sv['v1'],
                                        g['gu', l], g['down', l], f"a{l}")
        big['gu'] = tn_matmul_scatter(me_arr, dgu, sv['h1'][None], l, 2 * DEPTH, big['gu'], f"dw_gu_a{l}")
        big['down'] = tn_matmul_scatter(me_arr, a, df[None], l, 2 * DEPTH, big['down'], f"dw_down_a{l}", split=2)
        dx = dx0
        dmod[l] = jnp.concatenate([acc1[0:3], accmi[0:2], accmo[2:3], acc2[0:3]], axis=0)
        small['ffn1_norm'][l] = acc1[3]
        small['mix_norm'][l] = accmi[3]
        small['ffn2_norm'][l] = acc2[3]
        small['conv_w'][l] = dlp[0:4]
        small['conv_b'][l] = dlp[4]
        small['gate_a_b'][l] = dlp[5].reshape(HEADS, HD)
        small['gate_x_b'][l] = dlp[6].reshape(HEADS, HD)
        small['lru_lambda'][l] = dlp[7]
        small['gate_a_w'][l] = _block_diag_extract(dwa)
        small['gate_x_w'][l] = _block_diag_extract(dwx)
        small['v_norm'][l] = dgp[0]
        small['lru_out_norm'][l] = dgp[1]
        small['gmlp_out_norm'][l] = dgp[2]
        small['spatial_w'][l] = dwm.reshape(HEADS, CHUNK, CHUNK) * tril
        small['spatial_b'][l] = dbz.reshape(CHUNK, HEADS, HD).sum(-1).T
    small = {k: jnp.stack(v) for k, v in small.items()}
    small['final_norm'] = dfin[0]
    return loss, dx, big, small, jnp.stack(dmod)


def ada_fwd(c_all, w_ada, b_loc):
    def body(c_ref, w_ref, b_ref, mod_ref, sc_ref):
        cv = c_ref[...]
        sc = cv * _sigmoid(cv)
        sc_ref[...] = sc
        mod_ref[...] = _dot3(sc, w_ref[...]) + b_ref[...]

    return pl.pallas_call(
        body, name="ada_fwd",
        grid=(DEPTH,),
        in_specs=[pl.BlockSpec((NDEV, D), lambda l: (0, 0)), pl.BlockSpec((None, D, AC), lambda l: (l, 0, 0)),
                  pl.BlockSpec((None, 1, AC), lambda l: (l, 0, 0))],
        out_specs=[pl.BlockSpec((None, NDEV, AC), lambda l: (l, 0, 0)), pl.BlockSpec((NDEV, D), lambda l: (0, 0))],
        out_shape=[jax.ShapeDtypeStruct((DEPTH, NDEV, AC), F32), jax.ShapeDtypeStruct((NDEV, D), F32)],
        compiler_params=_cparams("arbitrary"),
    )(c_all, w_ada, b_loc)


def ada_bwd(sc_t, dmod_cols):
    def body(sc_ref, dm_ref, g_ref):
        sc = sc_ref[...]
        dm = dm_ref[...]
        acc = sc[:, 0:1] * dm[0:1, :]
        for b in range(1, NDEV):
            acc = acc + sc[:, b:b + 1] * dm[b:b + 1, :]
        g_ref[...] = acc

    return pl.pallas_call(
        body, name="ada_bwd",
        grid=(DEPTH,),
        in_specs=[pl.BlockSpec((D, NDEV), lambda l: (0, 0)), pl.BlockSpec((None, NDEV, AC), lambda l: (l, 0, 0))],
        out_specs=pl.BlockSpec((None, None, D, AC), lambda l: (0, l, 0, 0)),
        out_shape=jax.ShapeDtypeStruct((1, DEPTH, D, AC), F32),
        compiler_params=_cparams("arbitrary"),
    )(sc_t, dmod_cols)


def _row_tile(rows, cols):
    if rows * cols <= 512 * 1024:
        return rows
    for tr in (512, 384, 352, 256, 128, 64, 32, 16, 8):
        if rows % tr == 0:
            return tr
    return rows


def adamw(gparts, slot0, w, m, v, name):
    P, _, R, C = gparts.shape
    L = w.shape[0]
    tr = _row_tile(R, C)

    def body(g_ref, w_ref, m_ref, v_ref, go_ref, do_ref, mo_ref, vo_ref):
        g = g_ref[0].astype(F32)
        for p in range(1, P):
            g = g + g_ref[p].astype(F32)
        go_ref[...] = g
        mn = ADAM_B1 * m_ref[...] + (1.0 - ADAM_B1) * g
        vn = ADAM_B2 * v_ref[...] + (1.0 - ADAM_B2) * (g * g)
        mo_ref[...] = mn
        vo_ref[...] = vn
        m_hat = mn / (1.0 - ADAM_B1 ** ADAM_STEP)
        v_hat = vn / (1.0 - ADAM_B2 ** ADAM_STEP)
        do_ref[...] = -ADAM_LR * (m_hat / (jnp.sqrt(v_hat) + ADAM_EPS) + ADAM_WD * w_ref[...])

    blk = pl.BlockSpec((None, tr, C), lambda l, i: (l, i, 0))
    return pl.pallas_call(
        body, name=name,
        grid=(L, R // tr),
        in_specs=[pl.BlockSpec((P, None, tr, C), lambda l, i: (0, slot0 + l, i, 0)), blk, blk, blk],
        out_specs=[blk, blk, blk, blk],
        out_shape=[jax.ShapeDtypeStruct((L, R, C), F32)] * 4,
        compiler_params=_cparams("arbitrary", "arbitrary"),
    )(gparts, w, m, v)


def sum_parts(parts):
    P, R, C = parts.shape

    def body(p_ref, o_ref):
        acc = p_ref[0]
        for p in range(1, P):
            acc = acc + p_ref[p]
        o_ref[...] = acc

    return pl.pallas_call(
        body, name="sum_parts",
        in_specs=[pl.BlockSpec(memory_space=pltpu.VMEM)],
        out_specs=pl.BlockSpec(memory_space=pltpu.VMEM),
        out_shape=jax.ShapeDtypeStruct((R, C), F32),
    )(parts)


WEIGHTS = ['w_ada', 'b_ada', 'ffn1_norm', 'ffn1_w_gu', 'ffn1_w_down', 'mix_norm', 'w_in', 'conv_w', 'conv_b',
           'gate_a_w', 'gate_a_b', 'gate_x_w', 'gate_x_b', 'lru_lambda', 'v_norm', 'spatial_w', 'spatial_b',
           'lru_out_norm', 'gmlp_out_norm', 'w_out', 'ffn2_norm', 'ffn2_w_gu', 'ffn2_w_down', 'final_norm']
PACKED = ['b_ada', 'ffn1_norm', 'mix_norm', 'conv_b', 'gate_a_w', 'gate_a_b', 'gate_x_w', 'gate_x_b', 'lru_lambda',
          'v_norm', 'spatial_w', 'spatial_b', 'lru_out_norm', 'gmlp_out_norm', 'ffn2_norm', 'final_norm', 'conv_w']
PACK_LANES = 128
PACK_ROW_ALIGN = 8 * NDEV


def _pack(d):
    parts = [d[k].reshape(-1, PACK_LANES).astype(F32) for k in PACKED]
    used = sum(p.shape[0] for p in parts)
    rows = -(-used // PACK_ROW_ALIGN) * PACK_ROW_ALIGN
    return jnp.concatenate(parts + [jnp.zeros((rows - used, PACK_LANES), F32)], axis=0)


def _unpack(buf, shapes):
    out, off = {}, 0
    for k in PACKED:
        size = 1
        for s in shapes[k]:
            size *= s
        nrows = size // PACK_LANES
        out[k] = buf[off:off + nrows].reshape(shapes[k])
        off += nrows
    return out


def kernel(x, c, w_ada, b_ada, ffn1_norm, ffn1_w_gu, ffn1_w_down, mix_norm, w_in, conv_w, conv_b, gate_a_w, gate_a_b, gate_x_w, gate_x_b, lru_lambda, v_norm, spatial_w, spatial_b, lru_out_norm, gmlp_out_norm, w_out, ffn2_norm, ffn2_w_gu, ffn2_w_down, final_norm, loss_target, m_w_ada, m_b_ada, m_ffn1_norm, m_ffn1_w_gu, m_ffn1_w_down, m_mix_norm, m_w_in, m_conv_w, m_conv_b, m_gate_a_w, m_gate_a_b, m_gate_x_w, m_gate_x_b, m_lru_lambda, m_v_norm, m_spatial_w, m_spatial_b, m_lru_out_norm, m_gmlp_out_norm, m_w_out, m_ffn2_norm, m_ffn2_w_gu, m_ffn2_w_down, m_final_norm, v_w_ada, v_b_ada, v_ffn1_norm, v_ffn1_w_gu, v_ffn1_w_down, v_mix_norm, v_w_in, v_conv_w, v_conv_b, v_gate_a_w, v_gate_a_b, v_gate_x_w, v_gate_x_b, v_lru_lambda, v_v_norm, v_spatial_w, v_spatial_b, v_lru_out_norm, v_gmlp_out_norm, v_w_out, v_ffn2_norm, v_ffn2_w_gu, v_ffn2_w_down, v_final_norm):
    w = dict(w_ada=w_ada, b_ada=b_ada, ffn1_norm=ffn1_norm, ffn1_w_gu=ffn1_w_gu, ffn1_w_down=ffn1_w_down, mix_norm=mix_norm, w_in=w_in, conv_w=conv_w, conv_b=conv_b, gate_a_w=gate_a_w, gate_a_b=gate_a_b, gate_x_w=gate_x_w, gate_x_b=gate_x_b, lru_lambda=lru_lambda, v_norm=v_norm, spatial_w=spatial_w, spatial_b=spatial_b, lru_out_norm=lru_out_norm, gmlp_out_norm=gmlp_out_norm, w_out=w_out, ffn2_norm=ffn2_norm, ffn2_w_gu=ffn2_w_gu, ffn2_w_down=ffn2_w_down, final_norm=final_norm)
    m = dict(w_ada=m_w_ada, b_ada=m_b_ada, ffn1_norm=m_ffn1_norm, ffn1_w_gu=m_ffn1_w_gu, ffn1_w_down=m_ffn1_w_down, mix_norm=m_mix_norm, w_in=m_w_in, conv_w=m_conv_w, conv_b=m_conv_b, gate_a_w=m_gate_a_w, gate_a_b=m_gate_a_b, gate_x_w=m_gate_x_w, gate_x_b=m_gate_x_b, lru_lambda=m_lru_lambda, v_norm=m_v_norm, spatial_w=m_spatial_w, spatial_b=m_spatial_b, lru_out_norm=m_lru_out_norm, gmlp_out_norm=m_gmlp_out_norm, w_out=m_w_out, ffn2_norm=m_ffn2_norm, ffn2_w_gu=m_ffn2_w_gu, ffn2_w_down=m_ffn2_w_down, final_norm=m_final_norm)
    v = dict(w_ada=v_w_ada, b_ada=v_b_ada, ffn1_norm=v_ffn1_norm, ffn1_w_gu=v_ffn1_w_gu, ffn1_w_down=v_ffn1_w_down, mix_norm=v_mix_norm, w_in=v_w_in, conv_w=v_conv_w, conv_b=v_conv_b, gate_a_w=v_gate_a_w, gate_a_b=v_gate_a_b, gate_x_w=v_gate_x_w, gate_x_b=v_gate_x_b, lru_lambda=v_lru_lambda, v_norm=v_v_norm, spatial_w=v_spatial_w, spatial_b=v_spatial_b, lru_out_norm=v_lru_out_norm, gmlp_out_norm=v_gmlp_out_norm, w_out=v_w_out, ffn2_norm=v_ffn2_norm, ffn2_w_gu=v_ffn2_w_gu, ffn2_w_down=v_ffn2_w_down, final_norm=v_final_norm)
    me = 4 * lax.axis_index("x") + 2 * lax.axis_index("y") + lax.axis_index("c")

    loc = dict(gu=jnp.concatenate([ffn1_w_gu, ffn2_w_gu], axis=0).astype(BF16),
               down=jnp.concatenate([ffn1_w_down, ffn2_w_down], axis=0).astype(BF16),
               w_in=w_in.astype(BF16), w_out=w_out.astype(BF16))
    c_g, conv_g, gu0, down0 = all_gather([(c, None), (conv_w, None), (loc['gu'], 0), (loc['down'], 0)], "gather_first")
    conv_w_full = conv_g.transpose(1, 2, 0, 3).reshape(DEPTH, CONV_WIDTH, LW)

    b_loc = lax.dynamic_slice(b_ada, (0, me * AC), (DEPTH, AC)).reshape(DEPTH, 1, AC)
    mod_cols, sc_all = ada_fwd(c_g.reshape(NDEV, D), w_ada, b_loc)
    (mod_rows,) = all_to_all([mod_cols.transpose(1, 0, 2)], "scatter_mod")
    mod = mod_rows.transpose(1, 0, 2).reshape(DEPTH, NMOD, D)

    small_w = {k: w[k] for k in PACKED if k != 'conv_w'}
    me_arr = jnp.reshape(me, (1,)).astype(jnp.int32)
    loss_loc, dx, big, small_g, dmod = local_fwd_bwd(me_arr, x[0], loss_target[0], mod, small_w, loc,
                                                     {('gu', 0): gu0, ('down', 0): down0}, conv_w_full)
    loss = lax.psum(loss_loc, ("x", "y", "c"))

    small_g['b_ada'] = dmod.reshape(DEPTH, NMOD * D)
    gpack = _pack(small_g)
    rows = gpack.shape[0]
    dmod_out = dmod.reshape(DEPTH, NDEV, AC).transpose(1, 0, 2)
    dmod_r, pack_r = all_to_all([dmod_out, gpack.reshape(NDEV, rows // NDEV, PACK_LANES)], "scatter_grads")
    (gsum_g,) = all_gather([(sum_parts(pack_r), None)], "gather_small_grads")
    gsum = gsum_g.reshape(1, 1, rows, PACK_LANES)

    res = {}
    t = lambda a: a.transpose(0, 2, 1)
    gu_t = big['gu']
    res['ffn1_w_gu'] = tuple(t(r) for r in adamw(gu_t, 0, t(w['ffn1_w_gu']), t(m['ffn1_w_gu']), t(v['ffn1_w_gu']),
                                                 "adamw_gu_a"))
    res['ffn2_w_gu'] = tuple(t(r) for r in adamw(gu_t, DEPTH, t(w['ffn2_w_gu']), t(m['ffn2_w_gu']),
                                                 t(v['ffn2_w_gu']), "adamw_gu_b"))
    res['ffn1_w_down'] = adamw(big['down'], 0, w['ffn1_w_down'], m['ffn1_w_down'], v['ffn1_w_down'], "adamw_down_a")
    res['ffn2_w_down'] = adamw(big['down'], DEPTH, w['ffn2_w_down'], m['ffn2_w_down'], v['ffn2_w_down'], "adamw_down_b")
    res['w_in'] = adamw(big['w_in'], 0, w['w_in'], m['w_in'], v['w_in'], "adamw_w_in")
    res['w_out'] = adamw(big['w_out'], 0, w['w_out'], m['w_out'], v['w_out'], "adamw_w_out")
    g_ada = ada_bwd(sc_all.T, dmod_r.transpose(1, 0, 2))
    res['w_ada'] = adamw(g_ada, 0, w['w_ada'], m['w_ada'], v['w_ada'], "adamw_w_ada")
    shapes = {k: w[k].shape for k in PACKED}
    shapes['conv_w'] = (DEPTH, CONV_WIDTH, LW)
    dummy = jnp.zeros(shapes['conv_w'], F32)
    packs = adamw(gsum, 0, _pack({**small_w, 'conv_w': dummy})[None], _pack({**{k: m[k] for k in small_w}, 'conv_w': dummy})[None],
                  _pack({**{k: v[k] for k in small_w}, 'conv_w': dummy})[None], "adamw_small")
    unpacked = [_unpack(b[0], shapes) for b in packs]
    for k in small_w:
        res[k] = tuple(u[k] for u in unpacked)
    gconv = lax.dynamic_slice(unpacked[0]['conv_w'], (0, 0, me * (LW // NDEV)), (DEPTH, CONV_WIDTH, LW // NDEV))
    cshape = (1, DEPTH * CONV_WIDTH, LW // NDEV)
    rc = adamw(gconv.reshape((1,) + cshape), 0, conv_w.reshape(cshape), m['conv_w'].reshape(cshape),
               v['conv_w'].reshape(cshape), "adamw_conv_w")
    res['conv_w'] = tuple(r.reshape(conv_w.shape) for r in rc)

    return (loss, dx[None], *[res[k][0] for k in WEIGHTS], *[res[k][1] for k in WEIGHTS],
            *[res[k][2] for k in WEIGHTS], *[res[k][3] for k in WEIGHTS])
```

```python
import jax
import jax.numpy as jnp
from jax import lax
from jax.experimental import pallas as pl
from jax.experimental.pallas import tpu as pltpu

F32 = jnp.float32
BF16 = jnp.bfloat16

NDEV = 8
DEPTH = 2
D = 1024
DFF = 2816
FC = 2 * DFF // NDEV
NCHUNK = DFF // FC
DR = DFF // NDEV
LW = 512
GW = 512
HD = 64
HEADS = 8
CHUNK = 128
PC = 2 * (LW + GW) // NDEV
OR = D // NDEV
NMOD = 9
AC = NMOD * D // NDEV
LC = 128
EPS = 1e-6
RG_LRU_C = 8.0
CONV_WIDTH = 4

ADAM_LR = 0.001
ADAM_B1 = 0.9
ADAM_B2 = 0.999
ADAM_EPS = 1e-08
ADAM_WD = 0.01
ADAM_STEP = 10

VMEM_LIMIT_BYTES = 60 * 1024 * 1024
MESH = pl.DeviceIdType.MESH
ANY = pl.BlockSpec(memory_space=pl.ANY)


def _cparams(*sem):
    return pltpu.CompilerParams(dimension_semantics=tuple(sem) if sem else None,
                                vmem_limit_bytes=VMEM_LIMIT_BYTES)


def _dot(a, b):
    return jnp.dot(a, b, preferred_element_type=F32)


def _dot_nt(a, b):
    return lax.dot_general(a, b, (((1,), (1,)), ((), ())), preferred_element_type=F32)


def _dot_tn(a, b):
    return lax.dot_general(a, b, (((0,), (0,)), ((), ())), preferred_element_type=F32)


def _split(a):
    hi = a.astype(BF16)
    lo = (a - hi.astype(F32)).astype(BF16)
    return hi, lo


def _dot3(a, b):
    ah, al = _split(a)
    bh, bl = _split(b)
    return _dot(ah, bh) + (_dot(ah, bl) + _dot(al, bh))


def _csum(a):
    return jnp.sum(a, axis=0, keepdims=True)


def _rmean(a):
    return jnp.mean(a, axis=-1, keepdims=True)


def _sigmoid(a):
    return 1.0 / (1.0 + jnp.exp(-a))


_GELU_K = 0.7978845608028654
_GELU_C = 0.044715


def _gelu(a):
    return 0.5 * a * (1.0 + jnp.tanh(_GELU_K * (a + _GELU_C * a * a * a)))


def _gelu_and_grad(a):
    a2 = a * a
    t = jnp.tanh(_GELU_K * (a + _GELU_C * a2 * a))
    half = 0.5 * (1.0 + t)
    return a * half, half + 0.5 * a * (1.0 - t * t) * (_GELU_K * (1.0 + 3.0 * _GELU_C * a2))


def _norm_mod(x, gain, scale, shift):
    rstd = lax.rsqrt(_rmean(x * x) + EPS)
    return (x * rstd * gain) * (1.0 + scale) + shift


def _norm_mod_bwd(dh, x, gain, scale):
    rstd = lax.rsqrt(_rmean(x * x) + EPS)
    xhat = x * rstd
    dshift = _csum(dh)
    dscale = _csum(dh * (xhat * gain))
    dhn = dh * (1.0 + scale)
    dgain = _csum(dhn * xhat)
    dxhat = dhn * gain
    dx = rstd * (dxhat - xhat * _rmean(dxhat * xhat))
    return dx, dshift, dscale, dgain


def _rms(x, gain):
    rstd = lax.rsqrt(_rmean(x * x) + EPS)
    return x * rstd * gain


def _rms_bwd(dy, x, gain):
    rstd = lax.rsqrt(_rmean(x * x) + EPS)
    xhat = x * rstd
    dgain = _csum(dy * xhat)
    dxhat = dy * gain
    return rstd * (dxhat - xhat * _rmean(dxhat * xhat)), dgain


PAIR = 2 * HD


def _seg_mean(a, pavg):
    hi, lo = _split(a)
    return jnp.concatenate([_dot(hi[:, p:p + PAIR], pavg) + _dot(lo[:, p:p + PAIR], pavg)
                            for p in range(0, a.shape[1], PAIR)], axis=1)


def _block_copies(src_hbm, dst_vmem, sems, rows):
    copies = []
    for k in range(NDEV):
        dst = dst_vmem.at[k] if rows is None else dst_vmem.at[pl.ds(k * rows, rows)]
        copies.append(pltpu.make_async_copy(src_hbm.at[k], dst, sems.at[k]))
    return copies


def _ffn_weight_fetch(wgu_hbm, wd_hbm, wgu_v, wd_v, sems):
    @pl.when(pl.program_id(0) == 0)
    def _():
        copies = [] if wgu_hbm is None else _block_copies(wgu_hbm, wgu_v, sems.at[0], None)
        copies += [] if wd_hbm is None else _block_copies(wd_hbm, wd_v, sems.at[1], DR)
        for cp in copies:
            cp.start()
        for cp in copies:
            cp.wait()


def _place():
    return lax.axis_index("x"), lax.axis_index("y"), lax.axis_index("c")


def _slot(p):
    return 4 * p[0] + 2 * p[1] + p[2]


class GatherRide:
    def __init__(self, srcs):
        self.n = len(srcs)
        self.index = [i for _, i in srcs]
        self.args = [a for a, _ in srcs]
        self.out_shape = [jax.ShapeDtypeStruct((NDEV,) + (a.shape if i is None else a.shape[1:]), a.dtype)
                          for a, i in srcs]
        self.scratch = [pltpu.SemaphoreType.DMA((self.n, NDEV - 1)), pltpu.SemaphoreType.DMA((self.n, NDEV - 1)),
                        pltpu.SemaphoreType.DMA((self.n,))]

    def hooks(self, ins, outs, sems):
        send_sems, recv_sems, local_sems = sems
        n = self.n
        x, y, c = _place()
        me, sibling = (x, y, c), (x, y, 1 - c)
        chips = [(1 - x, y), (x, 1 - y), (1 - x, 1 - y)]

        def local(a):
            return ins[a] if self.index[a] is None else ins[a].at[self.index[a]]

        def copy(a, k, block, to, src=None):
            dst = outs[a].at[_slot(block)]
            return pltpu.make_async_remote_copy(
                src_ref=dst if src is None else src, dst_ref=dst,
                send_sem=send_sems.at[a, k], recv_sem=recv_sems.at[a, k],
                device_id=to, device_id_type=MESH)

        def mine():
            return [pltpu.make_async_copy(local(a), outs[a].at[_slot(me)], local_sems.at[a]) for a in range(n)]

        def first():
            cps = []
            for a in range(n):
                cps.append(copy(a, 0, me, sibling, src=local(a)))
                cps += [copy(a, 1 + j, me, (*chip, c), src=local(a)) for j, chip in enumerate(chips)]
            return cps

        def passed():
            return [copy(a, 4 + j, (*chip, c), sibling) for j, chip in enumerate(chips) for a in range(n)]

        def start():
            for cp in mine() + first():
                cp.start()

        def mid():
            for j, chip in enumerate(chips):
                for a in range(n):
                    copy(a, 1 + j, (*chip, c), me).wait_recv()
                    copy(a, 4 + j, (*chip, c), sibling).start()

        def finish():
            for a in range(n):
                copy(a, 0, sibling, me).wait_recv()
                for j, chip in enumerate(chips):
                    copy(a, 4 + j, (*chip, 1 - c), me).wait_recv()
            for cp in first() + passed():
                cp.wait_send()
            for cp in mine():
                cp.wait()

        return start, mid, finish


def all_gather(srcs, name):
    ride = GatherRide(srcs)
    n = ride.n

    def body(*refs):
        start, mid, finish = ride.hooks(refs[:n], refs[n:2 * n], refs[2 * n:])
        start()
        mid()
        finish()

    return pl.pallas_call(
        body, name=name,
        in_specs=[ANY] * n, out_specs=[ANY] * n, out_shape=ride.out_shape, scratch_shapes=ride.scratch,
    )(*ride.args)


def _call(core, ride, *, name, grid, in_specs, out_specs, out_shape, scratch_shapes, args):
    if ride is None:
        outs = pl.pallas_call(core, name=name, grid=grid, in_specs=in_specs, out_specs=out_specs,
                              out_shape=out_shape, scratch_shapes=scratch_shapes,
                              compiler_params=_cparams("arbitrary"))(*args)
        return outs, []
    n_in, n_out, n_sc, n = len(in_specs), len(out_shape), len(scratch_shapes), ride.n
    nsteps = grid[0]
    mid_step = max(nsteps - 2, 0)

    def body(*refs):
        cuts = [n_in, n_in + n, n_in + n + n_out, n_in + 2 * n + n_out, n_in + 2 * n + n_out + n_sc]
        ci, ri, co, ro, cs, rs = (refs[a:b] for a, b in zip([0] + cuts, cuts + [len(refs)]))
        start, mid, finish = ride.hooks(ri, ro, rs)
        i = pl.program_id(0)
        pl.when(i == 0)(start)
        core(*ci, *co, *cs)
        pl.when(i == mid_step)(mid)
        pl.when(i == nsteps - 1)(finish)

    outs = pl.pallas_call(
        body, name=name, grid=grid,
        in_specs=list(in_specs) + [ANY] * n, out_specs=list(out_specs) + [ANY] * n,
        out_shape=list(out_shape) + ride.out_shape, scratch_shapes=list(scratch_shapes) + ride.scratch,
        compiler_params=_cparams("arbitrary"))(*args, *ride.args)
    return outs[:n_out], outs[n_out:]


def all_to_all(arrs, name):
    n = len(arrs)

    def body(*refs):
        ins, outs = refs[:n], refs[n:2 * n]
        send_sems, recv_sems, local_sems = refs[2 * n:]
        x, y, c = _place()
        me = (x, y, c)

        def peer(k):
            return (1 - x if k & 4 else x, 1 - y if k & 2 else y, 1 - c if k & 1 else c)

        def copy(a, k):
            return pltpu.make_async_remote_copy(
                src_ref=ins[a].at[_slot(peer(k))], dst_ref=outs[a].at[_slot(me)],
                send_sem=send_sems.at[a, k - 1], recv_sem=recv_sems.at[a, k - 1],
                device_id=peer(k), device_id_type=MESH)

        def landing(a, k):
            return pltpu.make_async_remote_copy(
                src_ref=outs[a].at[_slot(peer(k))], dst_ref=outs[a].at[_slot(peer(k))],
                send_sem=send_sems.at[a, k - 1], recv_sem=recv_sems.at[a, k - 1],
                device_id=me, device_id_type=MESH)

        mine = [pltpu.make_async_copy(ins[a].at[_slot(me)], outs[a].at[_slot(me)], local_sems.at[a]) for a in range(n)]
        for cp in mine:
            cp.start()
        sends = [copy(a, k) for a in range(n) for k in range(1, NDEV)]
        for cp in sends:
            cp.start()
        for a in range(n):
            for k in range(1, NDEV):
                landing(a, k).wait_recv()
        for cp in sends:
            cp.wait_send()
        for cp in mine:
            cp.wait()

    return pl.pallas_call(
        body, name=name,
        in_specs=[ANY] * n, out_specs=[ANY] * n,
        out_shape=[jax.ShapeDtypeStruct(a.shape, a.dtype) for a in arrs],
        scratch_shapes=[pltpu.SemaphoreType.DMA((n, NDEV - 1)), pltpu.SemaphoreType.DMA((n, NDEV - 1)),
                        pltpu.SemaphoreType.DMA((n,))],
    )(*arrs)


FFN_TS = 512


def ffn_fwd(x, vec, wgu_g, wdown_g, tag, ride=None):
    S = x.shape[0]
    ts = min(FFN_TS, S)

    def body(x_ref, vec_ref, wgu_hbm, wd_hbm, xo_ref, h_ref, gu_ref, f_ref, wgu_v, wd_v, sems):
        _ffn_weight_fetch(wgu_hbm, wd_hbm, wgu_v, wd_v, sems)
        xv = x_ref[...]
        h = _norm_mod(xv, vec_ref[3:4, :], vec_ref[1:2, :], vec_ref[0:1, :]).astype(BF16)
        h_ref[...] = h
        acc = jnp.zeros((ts, D), F32)
        for j in range(NCHUNK):
            g = _dot(h, wgu_v[j])
            u = _dot(h, wgu_v[NCHUNK + j])
            gu_ref[j] = g.astype(BF16)
            gu_ref[NCHUNK + j] = u.astype(BF16)
            a = (g * _sigmoid(g) * u).astype(BF16)
            acc = acc + _dot(a, wd_v[pl.ds(j * FC, FC), :])
        f_ref[...] = acc.astype(BF16)
        xo_ref[...] = xv + (0.5 * vec_ref[2:3, :]) * acc

    return _call(
        body, ride, name=f"ffn_fwd_{tag}",
        grid=(S // ts,),
        in_specs=[pl.BlockSpec((ts, D), lambda i: (i, 0)),
                  pl.BlockSpec((8, D), lambda i: (0, 0)), ANY, ANY],
        out_specs=[pl.BlockSpec((ts, D), lambda i: (i, 0)),
                   pl.BlockSpec((ts, D), lambda i: (i, 0)),
                   pl.BlockSpec((NDEV, ts, FC), lambda i: (0, i, 0)),
                   pl.BlockSpec((ts, D), lambda i: (i, 0))],
        out_shape=[jax.ShapeDtypeStruct((S, D), F32), jax.ShapeDtypeStruct((S, D), BF16),
                   jax.ShapeDtypeStruct((NDEV, S, FC), BF16), jax.ShapeDtypeStruct((S, D), BF16)],
        scratch_shapes=[pltpu.VMEM((NDEV, D, FC), BF16), pltpu.VMEM((DFF, D), BF16),
                        pltpu.SemaphoreType.DMA((2, NDEV))],
        args=(x, vec, wgu_g, wdown_g))


def ffn_bwd_hidden(dxo, gu, f, vec, wdown_g, tag):
    S = dxo.shape[0]
    ts = min(FFN_TS, S)

    def body(dxo_ref, gu_ref, f_ref, vec_ref, wd_hbm, dgu_ref, a_ref, df_ref, acc_ref, wd_v, sems):
        _ffn_weight_fetch(None, wd_hbm, None, wd_v, sems)

        @pl.when(pl.program_id(0) == 0)
        def _():
            acc_ref[...] = jnp.zeros_like(acc_ref)

        dxo_v = dxo_ref[...]
        acc_ref[2:3, :] += 0.5 * _csum(dxo_v * f_ref[...].astype(F32))
        df = ((0.5 * vec_ref[2:3, :]) * dxo_v).astype(BF16)
        df_ref[...] = df
        for j in range(NCHUNK):
            da = _dot_nt(df, wd_v[pl.ds(j * FC, FC), :])
            g = gu_ref[j].astype(F32)
            u = gu_ref[NCHUNK + j].astype(F32)
            sg = _sigmoid(g)
            si = g * sg
            a_ref[j] = (si * u).astype(BF16)
            dgu_ref[j] = (da * u * (sg * (1.0 + g * (1.0 - sg)))).astype(BF16)
            dgu_ref[NCHUNK + j] = (da * si).astype(BF16)

    row = pl.BlockSpec((ts, D), lambda i: (i, 0))
    return pl.pallas_call(
        body, name=f"ffn_bwd_hidden_{tag}",
        grid=(S // ts,),
        in_specs=[row, pl.BlockSpec((NDEV, ts, FC), lambda i: (0, i, 0)), row,
                  pl.BlockSpec((8, D), lambda i: (0, 0)), ANY],
        out_specs=[pl.BlockSpec((NDEV, ts, FC), lambda i: (0, i, 0)),
                   pl.BlockSpec((NCHUNK, ts, FC), lambda i: (0, i, 0)), row,
                   pl.BlockSpec((8, D), lambda i: (0, 0))],
        out_shape=[jax.ShapeDtypeStruct((NDEV, S, FC), BF16), jax.ShapeDtypeStruct((NCHUNK, S, FC), BF16),
                   jax.ShapeDtypeStruct((S, D), BF16), jax.ShapeDtypeStruct((8, D), F32)],
        scratch_shapes=[pltpu.VMEM((DFF, D), BF16), pltpu.SemaphoreType.DMA((2, NDEV))],
        compiler_params=_cparams("arbitrary"),
    )(dxo, gu, f, vec, wdown_g)


def ffn_bwd_input(dgu, x, dxo, vec, wgu_g, tag):
    S = x.shape[0]
    ts = min(FFN_TS, S)

    def body(dgu_ref, x_ref, dxo_ref, vec_ref, wgu_hbm, dx_ref, acc_ref, wgu_v, sems):
        _ffn_weight_fetch(wgu_hbm, None, wgu_v, None, sems)

        @pl.when(pl.program_id(0) == 0)
        def _():
            acc_ref[...] = jnp.zeros_like(acc_ref)

        dh = jnp.zeros((ts, D), F32)
        for k in range(NDEV):
            dh = dh + _dot_nt(dgu_ref[k], wgu_v[k])
        dx, dshift, dscale, dgain = _norm_mod_bwd(dh, x_ref[...], vec_ref[3:4, :], vec_ref[1:2, :])
        dx_ref[...] = dx + dxo_ref[...]
        acc_ref[0:1, :] += dshift
        acc_ref[1:2, :] += dscale
        acc_ref[3:4, :] += dgain

    row = pl.BlockSpec((ts, D), lambda i: (i, 0))
    return pl.pallas_call(
        body, name=f"ffn_bwd_input_{tag}",
        grid=(S // ts,),
        in_specs=[pl.BlockSpec((NDEV, ts, FC), lambda i: (0, i, 0)), row, row,
                  pl.BlockSpec((8, D), lambda i: (0, 0)), ANY],
        out_specs=[row, pl.BlockSpec((8, D), lambda i: (0, 0))],
        out_shape=[jax.ShapeDtypeStruct((S, D), F32), jax.ShapeDtypeStruct((8, D), F32)],
        scratch_shapes=[pltpu.VMEM((NDEV, D, FC), BF16), pltpu.SemaphoreType.DMA((2, NDEV))],
        compiler_params=_cparams("arbitrary"),
    )(dgu, x, dxo, vec, wgu_g)


def ffn_bwd(dxo, x, gu, f, vec, wgu_g, wdown_g, tag):
    dgu, a, df, acc_gate = ffn_bwd_hidden(dxo, gu, f, vec, wdown_g, tag)
    dx, acc = ffn_bwd_input(dgu, x, dxo, vec, wgu_g, tag)
    return dx, dgu, a, df, acc + acc_gate


NCHIP = NDEV // 2


def tn_matmul_scatter(me_arr, a, b, slot, nslots, prev, name, split=1):
    na, S, M = a.shape
    nb, _, N = b.shape
    ncall = NDEV // split
    ts = min(4096, S)
    nsteps = S // ts
    mp = M // split
    other_step = {1: lambda j: 2 * j, 2: lambda j: j, 8: lambda j: 0}[split]
    mine_step = {1: lambda j: 2 * j + 1, 2: lambda j: j, 8: lambda j: 0}[split]

    def group(k, me_ref):
        if split == 1:
            return jnp.bitwise_xor(me_ref[0], NDEV - 1 - k)
        if split == 2:
            return jnp.bitwise_xor(me_ref[0] // 2, NCHIP - 1 - k)
        return 0

    def body(me_ref, *refs):
        a_ref, b_ref = refs[0], refs[1]
        recv_ref, acc, sb_other, sb_mine, land, d2d_send, d2d_recv, ici_send, ici_recv = refs[-9:]
        k = pl.program_id(0)
        s = pl.program_id(1)
        x, y, c = _place()
        my_chip = 2 * x + y

        def chip_of(j):
            if split == 8:
                cx, cy = j // 2, j % 2
            else:
                flip = NCHIP - 1 - j
                cx, cy = (1 - x if flip & 2 else x), (1 - y if flip & 1 else y)
            return cx, cy, 2 * cx + cy

        def piece(j, core):
            if split == 1:
                return acc[...]
            start = core * mp if split == 2 else (2 * j + core) * mp
            return acc[pl.ds(pl.multiple_of(start, 8), mp), :]

        def to_sibling(j):
            return pltpu.make_async_remote_copy(
                src_ref=sb_other.at[j], dst_ref=land.at[j], send_sem=d2d_send.at[j], recv_sem=d2d_recv.at[j],
                device_id=(x, y, 1 - c), device_id_type=MESH)

        def to_owner(j):
            cx, cy, ci = chip_of(j)
            dst = recv_ref.at[my_chip, slot]
            return ci, pltpu.make_async_copy(sb_mine.at[j], dst, ici_send.at[j]), pltpu.make_async_remote_copy(
                src_ref=sb_mine.at[j], dst_ref=dst, send_sem=ici_send.at[j], recv_sem=ici_recv.at[my_chip],
                device_id=(cx, cy, c), device_id_type=MESH)

        if nsteps == 1:
            acc[...] = _dot_tn(a_ref[...], b_ref[...])
        else:
            @pl.when(s == 0)
            def _():
                acc[...] = jnp.zeros_like(acc)

            acc[...] += _dot_tn(a_ref[...], b_ref[...])

        for kk in range(ncall):
            @pl.when((s == nsteps - 1) & (k == kk))
            def _():
                for j in range(NCHIP):
                    if other_step(j) == kk:
                        sb_other[j] = piece(j, 1 - c).astype(BF16)
                        to_sibling(j).start()
                for j in range(NCHIP):
                    if mine_step(j) == kk:
                        to_sibling(j).wait_recv()
                        sb_mine[j] = (piece(j, c) + land[j].astype(F32)).astype(BF16)
                        ci, loc, rem = to_owner(j)
                        pl.when(ci == my_chip)(loc.start)
                        pl.when(ci != my_chip)(rem.start)

        @pl.when((s == nsteps - 1) & (k == ncall - 1))
        def _():
            for j in range(NCHIP):
                to_sibling(j).wait_send()
                ci, loc, rem = to_owner(j)
                pl.when(ci == my_chip)(loc.wait)
                pl.when(ci != my_chip)(rem.wait_send)
            for src in range(NCHIP):
                @pl.when(my_chip != src)
                def _():
                    pltpu.make_async_remote_copy(
                        src_ref=recv_ref.at[src, slot], dst_ref=recv_ref.at[src, slot],
                        send_sem=ici_send.at[src], recv_sem=ici_recv.at[src],
                        device_id=(src // 2, src % 2, c), device_id_type=MESH).wait_recv()

    in_specs = [pl.BlockSpec((None, ts, M), (lambda k, s, me: (group(k, me), s, 0)) if na > 1 else (lambda k, s, me: (0, s, 0))),
                pl.BlockSpec((None, ts, N), (lambda k, s, me: (group(k, me), s, 0)) if nb > 1 else (lambda k, s, me: (0, s, 0)))]
    args = [me_arr, a, b]
    aliases = {}
    if prev is not None:
        in_specs.append(ANY)
        args.append(prev)
        aliases = {3: 0}
    return pl.pallas_call(
        body, name=name,
        grid_spec=pltpu.PrefetchScalarGridSpec(
            num_scalar_prefetch=1, grid=(ncall, nsteps), in_specs=in_specs, out_specs=ANY,
            scratch_shapes=[pltpu.VMEM((M, N), F32), pltpu.VMEM((NCHIP, mp, N), BF16), pltpu.VMEM((NCHIP, mp, N), BF16),
                            pltpu.VMEM((NCHIP, mp, N), BF16), pltpu.SemaphoreType.DMA((NCHIP,)),
                            pltpu.SemaphoreType.DMA((NCHIP,)), pltpu.SemaphoreType.DMA((NCHIP,)),
                            pltpu.SemaphoreType.DMA((NCHIP,))]),
        out_shape=jax.ShapeDtypeStruct((NCHIP, nslots, mp, N), BF16),
        input_output_aliases=aliases,
        compiler_params=_cparams("arbitrary", "arbitrary"),
    )(*args)


MIX_TS = 256
MIX_IN_TS = 512


def mix_in_fwd(x, vec, win_g, tag, ride=None):
    S = x.shape[0]
    ts = min(MIX_IN_TS, S)

    def body(x_ref, vec_ref, win_ref, hm_ref, proj_ref):
        h = _norm_mod(x_ref[...], vec_ref[3:4, :], vec_ref[1:2, :], vec_ref[0:1, :]).astype(BF16)
        hm_ref[...] = h
        for k in range(NDEV):
            proj_ref[k] = _dot(h, win_ref[k])

    return _call(
        body, ride, name=f"mix_in_fwd_{tag}",
        grid=(S // ts,),
        in_specs=[pl.BlockSpec((ts, D), lambda i: (i, 0)), pl.BlockSpec((8, D), lambda i: (0, 0)),
                  pl.BlockSpec((NDEV, D, PC), lambda i: (0, 0, 0))],
        out_specs=[pl.BlockSpec((ts, D), lambda i: (i, 0)),
                   pl.BlockSpec((NDEV, ts, PC), lambda i: (0, i, 0))],
        out_shape=[jax.ShapeDtypeStruct((S, D), BF16), jax.ShapeDtypeStruct((NDEV, S, PC), F32)],
        scratch_shapes=[], args=(x, vec, win_g))


def mix_in_bwd(dproj, x, dxo, vec, win_g, tag):
    S = x.shape[0]
    ts = min(MIX_IN_TS, S)

    def body(dp_ref, x_ref, dxo_ref, vec_ref, win_ref, dx_ref, acc_ref):
        @pl.when(pl.program_id(0) == 0)
        def _():
            acc_ref[...] = jnp.zeros_like(acc_ref)

        dh = jnp.zeros((ts, D), F32)
        for k in range(NDEV):
            dh = dh + _dot_nt(dp_ref[k], win_ref[k])
        dx, dshift, dscale, dgain = _norm_mod_bwd(dh, x_ref[...], vec_ref[3:4, :], vec_ref[1:2, :])
        dx_ref[...] = dx + dxo_ref[...]
        acc_ref[0:1, :] += dshift
        acc_ref[1:2, :] += dscale
        acc_ref[3:4, :] += dgain

    row = pl.BlockSpec((ts, D), lambda i: (i, 0))
    return pl.pallas_call(
        body, name=f"mix_in_bwd_{tag}",
        grid=(S // ts,),
        in_specs=[pl.BlockSpec((NDEV, ts, PC), lambda i: (0, i, 0)), row, row,
                  pl.BlockSpec((8, D), lambda i: (0, 0)),
                  pl.BlockSpec((NDEV, D, PC), lambda i: (0, 0, 0))],
        out_specs=[row, pl.BlockSpec((8, D), lambda i: (0, 0))],
        out_shape=[jax.ShapeDtypeStruct((S, D), F32), jax.ShapeDtypeStruct((8, D), F32)],
        compiler_params=_cparams("arbitrary"),
    )(dproj, x, dxo, vec, win_g)


SCAN_UNROLL = 4


def _shift_down(z, k, row):
    return jnp.where(row >= k, pltpu.roll(z, k, 0), 0.0)


def _shift_up(z, k, row, n):
    return jnp.where(row < n - k, pltpu.roll(z, n - k, 0), 0.0)


def _lru_gates(xc, lp_ref, wa_ref, wx_ref):
    xcb = xc.astype(BF16)
    ra = _sigmoid(_dot(xcb, wa_ref[...]) + lp_ref[5:6, :])
    ix = _sigmoid(_dot(xcb, wx_ref[...]) + lp_ref[6:7, :])
    lam = lp_ref[7:8, :]
    ls = jnp.minimum(lam, 0.0) - jnp.log(1.0 + jnp.exp(-jnp.abs(lam)))
    log_a = (RG_LRU_C * ls) * ra
    a = jnp.exp(log_a)
    mult = jnp.sqrt(-jnp.tanh(log_a) * (a * a + 1.0))
    return ra, ix, ls, a, mult


def _conv(x, lp_ref, row):
    return (lp_ref[4:5, :] + lp_ref[3:4, :] * x + lp_ref[2:3, :] * _shift_down(x, 1, row)
            + lp_ref[1:2, :] * _shift_down(x, 2, row) + lp_ref[0:1, :] * _shift_down(x, 3, row))


def lru_fwd(proj, lp, wa_t, wx_t, tag, ride=None):
    S = proj.shape[1]
    nblk = S // 8

    def body(x_ref, g_ref, lp_ref, wa_ref, wx_ref, y_ref, xc_ref, h_ref, a_s, b_s):
        x = x_ref[...]
        row = lax.broadcasted_iota(jnp.int32, x.shape, 0)
        xc = _conv(x, lp_ref, row)
        xc_ref[...] = xc
        ra, ix, ls, a, mult = _lru_gates(xc, lp_ref, wa_ref, wx_ref)
        a_s[...] = a
        b_s[...] = mult * (ix * xc)
        rowb = lax.broadcasted_iota(jnp.int32, (8, LC), 0)

        def step(i, carry):
            for q in range(SCAN_UNROLL):
                r0 = pl.multiple_of((i * SCAN_UNROLL + q) * 8, 8)
                A = a_s[pl.ds(r0, 8), :]
                B = b_s[pl.ds(r0, 8), :]
                for d in (1, 2, 4):
                    m = rowb >= d
                    As = jnp.where(m, pltpu.roll(A, d, 0), 1.0)
                    Bs = jnp.where(m, pltpu.roll(B, d, 0), 0.0)
                    B = A * Bs + B
                    A = A * As
                H = B + A * carry
                h_ref[pl.ds(r0, 8), :] = H
                carry = H[7:8, :]
            return carry

        lax.fori_loop(0, nblk // SCAN_UNROLL, step, jnp.zeros((1, LC), F32))
        y_ref[...] = h_ref[...] * _gelu(g_ref[...])

    col = pl.BlockSpec((S, LC), lambda c: (0, c))
    return _call(
        body, ride, name=f"lru_fwd_{tag}",
        grid=(LW // LC,),
        in_specs=[pl.BlockSpec((None, S, LC), lambda c: (c // 2, 0, c % 2)),
                  pl.BlockSpec((None, S, LC), lambda c: (2 + c // 2, 0, c % 2)),
                  pl.BlockSpec((8, LC), lambda c: (0, c)),
                  pl.BlockSpec((None, LC, LC), lambda c: (c, 0, 0)),
                  pl.BlockSpec((None, LC, LC), lambda c: (c, 0, 0))],
        out_specs=[col, col, col],
        out_shape=[jax.ShapeDtypeStruct((S, LW), F32)] * 3,
        scratch_shapes=[pltpu.VMEM((S, LC), F32), pltpu.VMEM((S, LC), F32)],
        args=(proj, proj, lp, wa_t, wx_t))


def lru_bwd(dy, proj, xc_all, hst, lp, wa_t, wx_t, tag):
    S = proj.shape[1]
    nblk = S // 8

    def body(dy_ref, x_ref, g_ref, xc_ref, h_ref, lp_ref, wa_ref, wx_ref,
             dx_ref, dg_ref, dlp_ref, dwa_ref, dwx_ref, c_s, l_s):
        xc = xc_ref[...]
        row = lax.broadcasted_iota(jnp.int32, xc.shape, 0)
        ra, ix, ls, a, mult = _lru_gates(xc, lp_ref, wa_ref, wx_ref)
        g = g_ref[...]
        dyv = dy_ref[...]
        h = h_ref[...]
        gelu_g, gelu_grad_g = _gelu_and_grad(g)
        dg_ref[...] = (dyv * h * gelu_grad_g).astype(BF16)
        c_s[...] = _shift_up(a, 1, row, S)
        l_s[...] = dyv * gelu_g
        rowb = lax.broadcasted_iota(jnp.int32, (8, LC), 0)

        def step(i, carry):
            for q in range(SCAN_UNROLL):
                r0 = pl.multiple_of((nblk - 1 - (i * SCAN_UNROLL + q)) * 8, 8)
                C = c_s[pl.ds(r0, 8), :]
                L = l_s[pl.ds(r0, 8), :]
                for d in (1, 2, 4):
                    m = rowb < 8 - d
                    Cs = jnp.where(m, pltpu.roll(C, 8 - d, 0), 1.0)
                    Ls = jnp.where(m, pltpu.roll(L, 8 - d, 0), 0.0)
                    L = C * Ls + L
                    C = C * Cs
                L = L + C * carry
                l_s[pl.ds(r0, 8), :] = L
                carry = L[0:1, :]
            return carry

        lax.fori_loop(0, nblk // SCAN_UNROLL, step, jnp.zeros((1, LC), F32))
        db = l_s[...]
        da = db * _shift_down(h, 1, row)
        ixc = ix * xc
        dmult = db * ixc
        dix = db * (mult * xc)
        dxc = db * (mult * ix)
        dlog_a = da * a - dmult * (a * a) / mult
        dra = dlog_a * (RG_LRU_C * ls)
        dls = _csum(dlog_a * ra) * RG_LRU_C
        lam = lp_ref[7:8, :]
        dlam = dls * _sigmoid(-lam)
        dpa = dra * ra * (1.0 - ra)
        dpx = dix * ix * (1.0 - ix)
        dpab = dpa.astype(BF16)
        dpxb = dpx.astype(BF16)
        xcb = xc.astype(BF16)
        dwa_ref[...] = _dot_tn(xcb, dpab)
        dwx_ref[...] = _dot_tn(xcb, dpxb)
        dxc = dxc + _dot_nt(dpab, wa_ref[...]) + _dot_nt(dpxb, wx_ref[...])
        x = x_ref[...]
        dlp_ref[0:1, :] = _csum(dxc * _shift_down(x, 3, row))
        dlp_ref[1:2, :] = _csum(dxc * _shift_down(x, 2, row))
        dlp_ref[2:3, :] = _csum(dxc * _shift_down(x, 1, row))
        dlp_ref[3:4, :] = _csum(dxc * x)
        dlp_ref[4:5, :] = _csum(dxc)
        dlp_ref[5:6, :] = _csum(dpa)
        dlp_ref[6:7, :] = _csum(dpx)
        dlp_ref[7:8, :] = dlam
        dx = (lp_ref[3:4, :] * dxc + lp_ref[2:3, :] * _shift_up(dxc, 1, row, S)
              + lp_ref[1:2, :] * _shift_up(dxc, 2, row, S) + lp_ref[0:1, :] * _shift_up(dxc, 3, row, S))
        dx_ref[...] = dx.astype(BF16)

    col = pl.BlockSpec((S, LC), lambda c: (0, c))
    pcol = pl.BlockSpec((None, S, LC), lambda c: (c // 2, 0, c % 2))
    return pl.pallas_call(
        body, name=f"lru_bwd_{tag}",
        grid=(LW // LC,),
        in_specs=[col, pcol, pl.BlockSpec((None, S, LC), lambda c: (2 + c // 2, 0, c % 2)), col, col,
                  pl.BlockSpec((8, LC), lambda c: (0, c)),
                  pl.BlockSpec((None, LC, LC), lambda c: (c, 0, 0)),
                  pl.BlockSpec((None, LC, LC), lambda c: (c, 0, 0))],
        out_specs=[pcol, pcol, pl.BlockSpec((8, LC), lambda c: (0, c)),
                   pl.BlockSpec((None, LC, LC), lambda c: (c, 0, 0)),
                   pl.BlockSpec((None, LC, LC), lambda c: (c, 0, 0))],
        out_shape=[jax.ShapeDtypeStruct((2, S, PC), BF16), jax.ShapeDtypeStruct((2, S, PC), BF16),
                   jax.ShapeDtypeStruct((8, LW), F32),
                   jax.ShapeDtypeStruct((LW // LC, LC, LC), F32), jax.ShapeDtypeStruct((LW // LC, LC, LC), F32)],
        scratch_shapes=[pltpu.VMEM((S, LC), F32), pltpu.VMEM((S, LC), F32)],
        compiler_params=_cparams("arbitrary"),
    )(dy, proj, proj, xc_all, hst, lp, wa_t, wx_t)


def _pair_stack(zp, low):
    return jnp.concatenate([jnp.where(low, zp, 0.0), jnp.where(low, 0.0, zp)], axis=0).astype(BF16)


def _spatial(w_ref, zc, low):
    return jnp.concatenate(
        [_dot(w_ref[:, 2 * p * CHUNK:2 * (p + 1) * CHUNK], _pair_stack(zc[:, p * PAIR:(p + 1) * PAIR], low))
         for p in range(GW // PAIR)], axis=1)


def _gmlp_fwd_parts(u, v, gp_ref, wcat_ref, bz_ref, pavg_ref, ts, with_grad=False):
    if with_grad:
        ug, ugrad = _gelu_and_grad(u)
        vg, vgrad = _gelu_and_grad(v)
    else:
        ug, vg, ugrad, vgrad = _gelu(u), _gelu(v), None, None
    pavg = pavg_ref[...]
    vc = vg - _seg_mean(vg, pavg)
    rs = lax.rsqrt(_seg_mean(vc * vc, pavg) + EPS)
    vhat = vc * rs
    vh = vhat * gp_ref[0:1, :]
    low = lax.broadcasted_iota(jnp.int32, (CHUNK, PAIR), 1) < HD
    zs = [_spatial(wcat_ref, vh[n * CHUNK:(n + 1) * CHUNK, :], low) + bz_ref[...] for n in range(ts // CHUNK)]
    z = jnp.concatenate(zs, axis=0) if len(zs) > 1 else zs[0]
    return ug, rs, vhat, vh, z, ugrad, vgrad


def mix_out_fwd(proj, ylru, x, vec, gp, wcat, bz, pavg, wout_g, tag, ride=None):
    S = x.shape[0]
    ts = min(MIX_TS, S)

    def body(u_ref, v_ref, yl_ref, x_ref, vec_ref, gp_ref, wcat_ref, bz_ref, pavg_ref, wout_ref,
             xo_ref, y_ref, fo_ref):
        u = jnp.concatenate([u_ref[0], u_ref[1]], axis=1)
        v = jnp.concatenate([v_ref[0], v_ref[1]], axis=1)
        ug, _, _, _, z, _, _ = _gmlp_fwd_parts(u, v, gp_ref, wcat_ref, bz_ref, pavg_ref, ts)
        n1 = _rms(yl_ref[...], gp_ref[1:2, :])
        n2 = _rms(ug * z, gp_ref[2:3, :])
        y = jnp.concatenate([n1, n2], axis=1).astype(BF16)
        y_ref[...] = y
        fo = jnp.zeros((ts, D), F32)
        for k in range(NDEV):
            fo = fo + _dot(y[:, k * OR:(k + 1) * OR], wout_ref[k])
        fo_ref[...] = fo.astype(BF16)
        xo_ref[...] = x_ref[...] + vec_ref[2:3, :] * fo

    row = pl.BlockSpec((ts, D), lambda i: (i, 0))
    full = lambda shp: pl.BlockSpec(shp, lambda i: tuple(0 for _ in shp))
    return _call(
        body, ride, name=f"mix_out_fwd_{tag}",
        grid=(S // ts,),
        in_specs=[pl.BlockSpec((2, ts, PC), lambda i: (2, i, 0)), pl.BlockSpec((2, ts, PC), lambda i: (3, i, 0)),
                  pl.BlockSpec((ts, LW), lambda i: (i, 0)), row, full((8, D)), full((8, GW)),
                  full((CHUNK, HEADS * CHUNK)), full((CHUNK, GW)), full((PAIR, PAIR)),
                  pl.BlockSpec((NDEV, OR, D), lambda i: (0, 0, 0))],
        out_specs=[row, row, row],
        out_shape=[jax.ShapeDtypeStruct((S, D), F32), jax.ShapeDtypeStruct((S, D), BF16),
                   jax.ShapeDtypeStruct((S, D), BF16)],
        scratch_shapes=[], args=(proj, proj, ylru, x, vec, gp, wcat, bz, pavg, wout_g))


def mix_out_bwd(dxo, proj, ylru, fo, vec, gp, wcat, wcat_t, bz, pavg, wout_g, tag):
    S = dxo.shape[0]
    ts = min(MIX_TS, S)

    def body(dxo_ref, u_ref, v_ref, yl_ref, fo_ref, vec_ref, gp_ref, wcat_ref, wcatt_ref, bz_ref, pavg_ref,
             wout_ref, dyo_ref, dyl_ref, duv_ref, acc_ref, dgp_ref, dwm_ref, dbz_ref):
        @pl.when(pl.program_id(0) == 0)
        def _():
            acc_ref[...] = jnp.zeros_like(acc_ref)
            dgp_ref[...] = jnp.zeros_like(dgp_ref)
            dwm_ref[...] = jnp.zeros_like(dwm_ref)
            dbz_ref[...] = jnp.zeros_like(dbz_ref)

        dxo_v = dxo_ref[...]
        acc_ref[2:3, :] += _csum(dxo_v * fo_ref[...].astype(F32))
        dyo = (vec_ref[2:3, :] * dxo_v).astype(BF16)
        dyo_ref[...] = dyo
        dn = [_dot_nt(dyo, wout_ref[k]) for k in range(NDEV)]
        dn1 = jnp.concatenate(dn[:NDEV // 2], axis=1)
        dn2 = jnp.concatenate(dn[NDEV // 2:], axis=1)
        dyl, dg1 = _rms_bwd(dn1, yl_ref[...], gp_ref[1:2, :])
        dyl_ref[...] = dyl
        u = jnp.concatenate([u_ref[0], u_ref[1]], axis=1)
        v = jnp.concatenate([v_ref[0], v_ref[1]], axis=1)
        ug, rs, vhat, vh, z, ugrad, vgrad = _gmlp_fwd_parts(u, v, gp_ref, wcat_ref, bz_ref, pavg_ref, ts,
                                                            with_grad=True)
        dyg, dg2 = _rms_bwd(dn2, ug * z, gp_ref[2:3, :])
        du = (dyg * z) * ugrad
        dz = dyg * ug
        low = lax.broadcasted_iota(jnp.int32, (CHUNK, PAIR), 1) < HD
        vhb = vh.astype(BF16)
        dvhs = []
        dbz = jnp.zeros((CHUNK, GW), F32)
        dwm = [jnp.zeros((2 * CHUNK, CHUNK), F32) for _ in range(GW // PAIR)]
        for n in range(ts // CHUNK):
            dzc = dz[n * CHUNK:(n + 1) * CHUNK, :]
            dbz = dbz + dzc
            for p in range(GW // PAIR):
                stack = _pair_stack(dzc[:, p * PAIR:(p + 1) * PAIR], low)
                dwm[p] = dwm[p] + _dot_nt(stack, vhb[n * CHUNK:(n + 1) * CHUNK, p * PAIR:(p + 1) * PAIR])
            dvhs.append(_spatial(wcatt_ref, dzc, low))
        dbz_ref[...] += dbz
        for p in range(GW // PAIR):
            dwm_ref[2 * p * CHUNK:2 * (p + 1) * CHUNK, :] += dwm[p]
        dvh = jnp.concatenate(dvhs, axis=0) if len(dvhs) > 1 else dvhs[0]
        pavg = pavg_ref[...]
        dvn = _csum(dvh * vhat)
        dvhat = dvh * gp_ref[0:1, :]
        dvg = rs * (dvhat - _seg_mean(dvhat, pavg) - vhat * _seg_mean(dvhat * vhat, pavg))
        dv = dvg * vgrad
        duv_ref[0] = du[:, :PC].astype(BF16)
        duv_ref[1] = du[:, PC:].astype(BF16)
        duv_ref[2] = dv[:, :PC].astype(BF16)
        duv_ref[3] = dv[:, PC:].astype(BF16)
        dgp_ref[0:1, :] += dvn
        dgp_ref[1:2, :] += dg1
        dgp_ref[2:3, :] += dg2

    row = pl.BlockSpec((ts, D), lambda i: (i, 0))
    full = lambda shp: pl.BlockSpec(shp, lambda i: tuple(0 for _ in shp))
    return pl.pallas_call(
        body, name=f"mix_out_bwd_{tag}",
        grid=(S // ts,),
        in_specs=[row, pl.BlockSpec((2, ts, PC), lambda i: (2, i, 0)), pl.BlockSpec((2, ts, PC), lambda i: (3, i, 0)),
                  pl.BlockSpec((ts, LW), lambda i: (i, 0)), row, full((8, D)), full((8, GW)),
                  full((CHUNK, HEADS * CHUNK)), full((CHUNK, HEADS * CHUNK)), full((CHUNK, GW)), full((PAIR, PAIR)),
                  pl.BlockSpec((NDEV, OR, D), lambda i: (0, 0, 0))],
        out_specs=[row, pl.BlockSpec((ts, LW), lambda i: (i, 0)), pl.BlockSpec((4, ts, PC), lambda i: (0, i, 0)),
                   full((8, D)), full((8, GW)), full((HEADS * CHUNK, CHUNK)), full((CHUNK, GW))],
        out_shape=[jax.ShapeDtypeStruct((S, D), BF16), jax.ShapeDtypeStruct((S, LW), F32),
                   jax.ShapeDtypeStruct((4, S, PC), BF16), jax.ShapeDtypeStruct((8, D), F32),
                   jax.ShapeDtypeStruct((8, GW), F32), jax.ShapeDtypeStruct((HEADS * CHUNK, CHUNK), F32),
                   jax.ShapeDtypeStruct((CHUNK, GW), F32)],
        compiler_params=_cparams("arbitrary"),
    )(dxo, proj, proj, ylru, fo, vec, gp, wcat, wcat_t, bz, pavg, wout_g)


def final_loss(x, target, gain):
    S = x.shape[0]
    ts = min(512, S)

    def body(x_ref, t_ref, g_ref, loss_ref, dx_ref, dg_ref):
        @pl.when(pl.program_id(0) == 0)
        def _():
            loss_ref[...] = jnp.zeros_like(loss_ref)
            dg_ref[...] = jnp.zeros_like(dg_ref)

        xv = x_ref[...]
        gain_v = g_ref[0:1, :]
        rstd = lax.rsqrt(_rmean(xv * xv) + EPS)
        xhat = xv * rstd
        err = xhat * gain_v - t_ref[...]
        loss_ref[...] += 0.5 * _csum(_rmean(err * err))
        dy = err * (1.0 / D)
        dg_ref[0:1, :] += _csum(dy * xhat)
        dxhat = dy * gain_v
        dx_ref[...] = rstd * (dxhat - xhat * _rmean(dxhat * xhat))

    row = pl.BlockSpec((ts, D), lambda i: (i, 0))
    return pl.pallas_call(
        body, name="final_loss",
        grid=(S // ts,),
        in_specs=[row, row, pl.BlockSpec((8, D), lambda i: (0, 0))],
        out_specs=[pl.BlockSpec((8, 128), lambda i: (0, 0)), row, pl.BlockSpec((8, D), lambda i: (0, 0))],
        out_shape=[jax.ShapeDtypeStruct((8, 128), F32), jax.ShapeDtypeStruct((S, D), F32),
                   jax.ShapeDtypeStruct((8, D), F32)],
        compiler_params=_cparams("arbitrary"),
    )(x, target, gain)


def _vec(mod_l, j, gain):
    return jnp.concatenate([mod_l[3 * j:3 * j + 3], gain[None, :], jnp.zeros((4, D), F32)], axis=0)


def _block_diag_tiles(w):
    w4 = w.reshape(LW // LC, 2, HD, HD)
    eye2 = jnp.eye(2, dtype=w.dtype)
    return (w4[:, :, :, None, :] * eye2[None, :, None, :, None]).reshape(LW // LC, LC, LC).astype(BF16)


def _block_diag_extract(dw):
    d5 = dw.reshape(LW // LC, 2, HD, 2, HD)
    return jnp.einsum('cihkj,ik->cihj', d5, jnp.eye(2, dtype=dw.dtype)).reshape(HEADS, HD, HD)


def _layer_params(l, p, conv_w_full):
    lp = jnp.concatenate([conv_w_full[l], p['conv_b'][l][None], p['gate_a_b'][l].reshape(1, LW),
                          p['gate_x_b'][l].reshape(1, LW), p['lru_lambda'][l][None]], axis=0)
    gp = jnp.concatenate([p['v_norm'][l][None], p['lru_out_norm'][l][None], p['gmlp_out_norm'][l][None],
                          jnp.zeros((5, GW), F32)], axis=0)
    ws = p['spatial_w'][l] * jnp.tril(jnp.ones((CHUNK, CHUNK), F32))
    wcat = ws.transpose(1, 0, 2).reshape(CHUNK, HEADS * CHUNK).astype(BF16)
    wcat_t = ws.transpose(2, 0, 1).reshape(CHUNK, HEADS * CHUNK).astype(BF16)
    bz = jnp.repeat(p['spatial_b'][l].T, HD, axis=1)
    return dict(lp=lp, gp=gp, wcat=wcat, wcat_t=wcat_t, bz=bz,
                wa_t=_block_diag_tiles(p['gate_a_w'][l]), wx_t=_block_diag_tiles(p['gate_x_w'][l]))


def _pavg():
    return jnp.kron(jnp.eye(2, dtype=F32), jnp.full((HD, HD), 1.0 / HD, F32)).astype(BF16)


GATHER_RIDES = {
    ('ffn_a', 0): [('w_in', 0), ('gu', DEPTH)],
    ('mix_in', 0): [('w_out', 0)],
    ('lru', 0): [('down', DEPTH)],
    ('mix_out', 0): [('down', 1)],
    ('ffn_b', 0): [('gu', 1), ('w_in', 1)],
    ('ffn_a', 1): [('gu', DEPTH + 1), ('w_out', 1)],
    ('mix_in', 1): [('down', DEPTH + 1)],
}


def local_fwd_bwd(me_arr, x, target, mod, p, loc, gathered, conv_w_full):
    pavg = _pavg()
    g = dict(gathered)

    def ride(call, l):
        todo = GATHER_RIDES.get((call, l))
        return None if todo is None else (todo, GatherRide([(loc[kind], slot) for kind, slot in todo]))

    def run(fn, call, l, *args):
        r = ride(call, l)
        outs, got = fn(*args, ride=None if r is None else r[1])
        if r is not None:
            g.update(dict(zip(r[0], got)))
        return outs

    saved = []
    h = x
    for l in range(DEPTH):
        q = _layer_params(l, p, conv_w_full)
        v1 = _vec(mod[l], 0, p['ffn1_norm'][l])
        vm = _vec(mod[l], 1, p['mix_norm'][l])
        v2 = _vec(mod[l], 2, p['ffn2_norm'][l])
        x0 = h
        x1, h1, gu1, f1 = run(ffn_fwd, 'ffn_a', l, x0, v1, g['gu', l], g['down', l], f"a{l}")
        hm, proj = run(mix_in_fwd, 'mix_in', l, x1, vm, g['w_in', l], f"{l}")
        ylru, xc, hst = run(lru_fwd, 'lru', l, proj, q['lp'], q['wa_t'], q['wx_t'], f"{l}")
        x2, y, fo = run(mix_out_fwd, 'mix_out', l, proj, ylru, x1, vm, q['gp'], q['wcat'], q['bz'], pavg,
                        g['w_out', l], f"{l}")
        x3, h2, gu2, f2 = run(ffn_fwd, 'ffn_b', l, x2, v2, g['gu', DEPTH + l], g['down', DEPTH + l], f"b{l}")
        saved.append(dict(q=q, v1=v1, vm=vm, v2=v2, x0=x0, x1=x1, x2=x2, h1=h1, gu1=gu1, f1=f1, hm=hm, proj=proj,
                          ylru=ylru, xc=xc, hst=hst, y=y, fo=fo, h2=h2, gu2=gu2, f2=f2))
        h = x3
    fin = jnp.concatenate([p['final_norm'][None], jnp.zeros((7, D), F32)], axis=0)
    loss8, dx, dfin = final_loss(h, target, fin)
    loss = loss8[0, 0]

    big = dict(gu=None, down=None, w_in=None, w_out=None)
    small = {k: [None] * DEPTH for k in ('ffn1_norm', 'mix_norm', 'ffn2_norm', 'conv_w', 'conv_b', 'gate_a_w',
                                         'gate_a_b', 'gate_x_w', 'gate_x_b', 'lru_lambda', 'v_norm', 'spatial_w',
                                         'spatial_b', 'lru_out_norm', 'gmlp_out_norm')}
    dmod = [None] * DEPTH
    tril = jnp.tril(jnp.ones((CHUNK, CHUNK), F32))
    for l in reversed(range(DEPTH)):
        sv = saved[l]
        q = sv['q']
        dx2, dgu, a, df, acc2 = ffn_bwd(dx, sv['x2'], sv['gu2'], sv['f2'], sv['v2'],
                                        g['gu', DEPTH + l], g['down', DEPTH + l], f"b{l}")
        big['gu'] = tn_matmul_scatter(me_arr, dgu, sv['h2'][None], DEPTH + l, 2 * DEPTH, big['gu'], f"dw_gu_b{l}")
        big['down'] = tn_matmul_scatter(me_arr, a, df[None], DEPTH + l, 2 * DEPTH, big['down'], f"dw_down_b{l}", split=2)
        dyo, dylru, duv, accmo, dgp, dwm, dbz = mix_out_bwd(dx2, sv['proj'], sv['ylru'], sv['fo'], sv['vm'], q['gp'],
                                                             q['wcat'], q['wcat_t'], q['bz'], pavg, g['w_out', l], f"{l}")
        big['w_out'] = tn_matmul_scatter(me_arr, sv['y'][None], dyo[None], l, DEPTH, big['w_out'], f"dw_out_{l}",
                                         split=NDEV)
        dxl, dgl, dlp, dwa, dwx = lru_bwd(dylru, sv['proj'], sv['xc'], sv['hst'], q['lp'], q['wa_t'], q['wx_t'], f"{l}")
        dproj = jnp.concatenate([dxl, dgl, duv], axis=0)
        dx1, accmi = mix_in_bwd(dproj, sv['x1'], dx2, sv['vm'], g['w_in', l], f"{l}")
        big['w_in'] = tn_matmul_scatter(me_arr, sv['hm'][None], dproj, l, DEPTH, big['w_in'], f"dw_in_{l}")
        dx0, dgu, a, df, acc1 = ffn_bwd(dx1, sv['x0'], sv['gu1'], sv['f1'], sv['v1'],
                                        g['gu', l], g['down', l], f"a{l}")
        big['gu'] = tn_matmul_scatter(me_arr, dgu, sv['h1'][None], l, 2 * DEPTH, big['gu'], f"dw_gu_a{l}")
        big['down'] = tn_matmul_scatter(me_arr, a, df[None], l, 2 * DEPTH, big['down'], f"dw_down_a{l}", split=2)
        dx = dx0
        dmod[l] = jnp.concatenate([acc1[0:3], accmi[0:2], accmo[2:3], acc2[0:3]], axis=0)
        small['ffn1_norm'][l] = acc1[3]
        small['mix_norm'][l] = accmi[3]
        small['ffn2_norm'][l] = acc2[3]
        small['conv_w'][l] = dlp[0:4]
        small['conv_b'][l] = dlp[4]
        small['gate_a_b'][l] = dlp[5].reshape(HEADS, HD)
        small['gate_x_b'][l] = dlp[6].reshape(HEADS, HD)
        small['lru_lambda'][l] = dlp[7]
        small['gate_a_w'][l] = _block_diag_extract(dwa)
        small['gate_x_w'][l] = _block_diag_extract(dwx)
        small['v_norm'][l] = dgp[0]
        small['lru_out_norm'][l] = dgp[1]
        small['gmlp_out_norm'][l] = dgp[2]
        small['spatial_w'][l] = dwm.reshape(HEADS, CHUNK, CHUNK) * tril
        small['spatial_b'][l] = dbz.reshape(CHUNK, HEADS, HD).sum(-1).T
    small = {k: jnp.stack(v) for k, v in small.items()}
    small['final_norm'] = dfin[0]
    return loss, dx, big, small, jnp.stack(dmod)


def ada_fwd(c_all, w_ada, b_loc):
    def body(c_ref, w_ref, b_ref, mod_ref, sc_ref):
        cv = c_ref[...]
        sc = cv * _sigmoid(cv)
        sc_ref[...] = sc
        mod_ref[...] = _dot3(sc, w_ref[...]) + b_ref[...]

    return pl.pallas_call(
        body, name="ada_fwd",
        grid=(DEPTH,),
        in_specs=[pl.BlockSpec((NDEV, D), lambda l: (0, 0)), pl.BlockSpec((None, D, AC), lambda l: (l, 0, 0)),
                  pl.BlockSpec((None, 1, AC), lambda l: (l, 0, 0))],
        out_specs=[pl.BlockSpec((None, NDEV, AC), lambda l: (l, 0, 0)), pl.BlockSpec((NDEV, D), lambda l: (0, 0))],
        out_shape=[jax.ShapeDtypeStruct((DEPTH, NDEV, AC), F32), jax.ShapeDtypeStruct((NDEV, D), F32)],
        compiler_params=_cparams("arbitrary"),
    )(c_all, w_ada, b_loc)


def ada_bwd(sc_t, dmod_cols):
    def body(sc_ref, dm_ref, g_ref):
        sc = sc_ref[...]
        dm = dm_ref[...]
        acc = sc[:, 0:1] * dm[0:1, :]
        for b in range(1, NDEV):
            acc = acc + sc[:, b:b + 1] * dm[b:b + 1, :]
        g_ref[...] = acc

    return pl.pallas_call(
        body, name="ada_bwd",
        grid=(DEPTH,),
        in_specs=[pl.BlockSpec((D, NDEV), lambda l: (0, 0)), pl.BlockSpec((None, NDEV, AC), lambda l: (l, 0, 0))],
        out_specs=pl.BlockSpec((None, None, D, AC), lambda l: (0, l, 0, 0)),
        out_shape=jax.ShapeDtypeStruct((1, DEPTH, D, AC), F32),
        compiler_params=_cparams("arbitrary"),
    )(sc_t, dmod_cols)


def _row_tile(rows, cols):
    if rows * cols <= 512 * 1024:
        return rows
    for tr in (512, 384, 352, 256, 128, 64, 32, 16, 8):
        if rows % tr == 0:
            return tr
    return rows


def adamw(gparts, slot0, w, m, v, name):
    P, _, R, C = gparts.shape
    L = w.shape[0]
    tr = _row_tile(R, C)

    def body(g_ref, w_ref, m_ref, v_ref, go_ref, do_ref, mo_ref, vo_ref):
        g = g_ref[0].astype(F32)
        for p in range(1, P):
            g = g + g_ref[p].astype(F32)
        go_ref[...] = g
        mn = ADAM_B1 * m_ref[...] + (1.0 - ADAM_B1) * g
        vn = ADAM_B2 * v_ref[...] + (1.0 - ADAM_B2) * (g * g)
        mo_ref[...] = mn
        vo_ref[...] = vn
        m_hat = mn / (1.0 - ADAM_B1 ** ADAM_STEP)
        v_hat = vn / (1.0 - ADAM_B2 ** ADAM_STEP)
        do_ref[...] = -ADAM_LR * (m_hat / (jnp.sqrt(v_hat) + ADAM_EPS) + ADAM_WD * w_ref[...])

    blk = pl.BlockSpec((None, tr, C), lambda l, i: (l, i, 0))
    return pl.pallas_call(
        body, name=name,
        grid=(L, R // tr),
        in_specs=[pl.BlockSpec((P, None, tr, C), lambda l, i: (0, slot0 + l, i, 0)), blk, blk, blk],
        out_specs=[blk, blk, blk, blk],
        out_shape=[jax.ShapeDtypeStruct((L, R, C), F32)] * 4,
        compiler_params=_cparams("arbitrary", "arbitrary"),
    )(gparts, w, m, v)


def sum_parts(parts):
    P, R, C = parts.shape

    def body(p_ref, o_ref):
        acc = p_ref[0]
        for p in range(1, P):
            acc = acc + p_ref[p]
        o_ref[...] = acc

    return pl.pallas_call(
        body, name="sum_parts",
        in_specs=[pl.BlockSpec(memory_space=pltpu.VMEM)],
        out_specs=pl.BlockSpec(memory_space=pltpu.VMEM),
        out_shape=jax.ShapeDtypeStruct((R, C), F32),
    )(parts)


WEIGHTS = ['w_ada', 'b_ada', 'ffn1_norm', 'ffn1_w_gu', 'ffn1_w_down', 'mix_norm', 'w_in', 'conv_w', 'conv_b',
           'gate_a_w', 'gate_a_b', 'gate_x_w', 'gate_x_b', 'lru_lambda', 'v_norm', 'spatial_w', 'spatial_b',
           'lru_out_norm', 'gmlp_out_norm', 'w_out', 'ffn2_norm', 'ffn2_w_gu', 'ffn2_w_down', 'final_norm']
PACKED = ['b_ada', 'ffn1_norm', 'mix_norm', 'conv_b', 'gate_a_w', 'gate_a_b', 'gate_x_w', 'gate_x_b', 'lru_lambda',
          'v_norm', 'spatial_w', 'spatial_b', 'lru_out_norm', 'gmlp_out_norm', 'ffn2_norm', 'final_norm', 'conv_w']
PACK_LANES = 128
PACK_ROW_ALIGN = 8 * NDEV


def _pack(d):
    parts = [d[k].reshape(-1, PACK_LANES).astype(F32) for k in PACKED]
    used = sum(p.shape[0] for p in parts)
    rows = -(-used // PACK_ROW_ALIGN) * PACK_ROW_ALIGN
    return jnp.concatenate(parts + [jnp.zeros((rows - used, PACK_LANES), F32)], axis=0)


def _unpack(buf, shapes):
    out, off = {}, 0
    for k in PACKED:
        size = 1
        for s in shapes[k]:
            size *= s
        nrows = size // PACK_LANES
        out[k] = buf[off:off + nrows].reshape(shapes[k])
        off += nrows
    return out


def kernel(x, c, w_ada, b_ada, ffn1_norm, ffn1_w_gu, ffn1_w_down, mix_norm, w_in, conv_w, conv_b, gate_a_w, gate_a_b, gate_x_w, gate_x_b, lru_lambda, v_norm, spatial_w, spatial_b, lru_out_norm, gmlp_out_norm, w_out, ffn2_norm, ffn2_w_gu, ffn2_w_down, final_norm, loss_target, m_w_ada, m_b_ada, m_ffn1_norm, m_ffn1_w_gu, m_ffn1_w_down, m_mix_norm, m_w_in, m_conv_w, m_conv_b, m_gate_a_w, m_gate_a_b, m_gate_x_w, m_gate_x_b, m_lru_lambda, m_v_norm, m_spatial_w, m_spatial_b, m_lru_out_norm, m_gmlp_out_norm, m_w_out, m_ffn2_norm, m_ffn2_w_gu, m_ffn2_w_down, m_final_norm, v_w_ada, v_b_ada, v_ffn1_norm, v_ffn1_w_gu, v_ffn1_w_down, v_mix_norm, v_w_in, v_conv_w, v_conv_b, v_gate_a_w, v_gate_a_b, v_gate_x_w, v_gate_x_b, v_lru_lambda, v_v_norm, v_spatial_w, v_spatial_b, v_lru_out_norm, v_gmlp_out_norm, v_w_out, v_ffn2_norm, v_ffn2_w_gu, v_ffn2_w_down, v_final_norm):
    w = dict(w_ada=w_ada, b_ada=b_ada, ffn1_norm=ffn1_norm, ffn1_w_gu=ffn1_w_gu, ffn1_w_down=ffn1_w_down, mix_norm=mix_norm, w_in=w_in, conv_w=conv_w, conv_b=conv_b, gate_a_w=gate_a_w, gate_a_b=gate_a_b, gate_x_w=gate_x_w, gate_x_b=gate_x_b, lru_lambda=lru_lambda, v_norm=v_norm, spatial_w=spatial_w, spatial_b=spatial_b, lru_out_norm=lru_out_norm, gmlp_out_norm=gmlp_out_norm, w_out=w_out, ffn2_norm=ffn2_norm, ffn2_w_gu=ffn2_w_gu, ffn2_w_down=ffn2_w_down, final_norm=final_norm)
    m = dict(w_ada=m_w_ada, b_ada=m_b_ada, ffn1_norm=m_ffn1_norm, ffn1_w_gu=m_ffn1_w_gu, ffn1_w_down=m_ffn1_w_down, mix_norm=m_mix_norm, w_in=m_w_in, conv_w=m_conv_w, conv_b=m_conv_b, gate_a_w=m_gate_a_w, gate_a_b=m_gate_a_b, gate_x_w=m_gate_x_w, gate_x_b=m_gate_x_b, lru_lambda=m_lru_lambda, v_norm=m_v_norm, spatial_w=m_spatial_w, spatial_b=m_spatial_b, lru_out_norm=m_lru_out_norm, gmlp_out_norm=m_gmlp_out_norm, w_out=m_w_out, ffn2_norm=m_ffn2_norm, ffn2_w_gu=m_ffn2_w_gu, ffn2_w_down=m_ffn2_w_down, final_norm=m_final_norm)
    v = dict(w_ada=v_w_ada, b_ada=v_b_ada, ffn1_norm=v_ffn1_norm, ffn1_w_gu=v_ffn1_w_gu, ffn1_w_down=v_ffn1_w_down, mix_norm=v_mix_norm, w_in=v_w_in, conv_w=v_conv_w, conv_b=v_conv_b, gate_a_w=v_gate_a_w, gate_a_b=v_gate_a_b, gate_x_w=v_gate_x_w, gate_x_b=v_gate_x_b, lru_lambda=v_lru_lambda, v_norm=v_v_norm, spatial_w=v_spatial_w, spatial_b=v_spatial_b, lru_out_norm=v_lru_out_norm, gmlp_out_norm=v_gmlp_out_norm, w_out=v_w_out, ffn2_norm=v_ffn2_norm, ffn2_w_gu=v_ffn2_w_gu, ffn2_w_down=v_ffn2_w_down, final_norm=v_final_norm)
    me = 4 * lax.axis_index("x") + 2 * lax.axis_index("y") + lax.axis_index("c")

    loc = dict(gu=jnp.concatenate([ffn1_w_gu, ffn2_w_gu], axis=0).astype(BF16),
               down=jnp.concatenate([ffn1_w_down, ffn2_w_down], axis=0).astype(BF16),
               w_in=w_in.astype(BF16), w_out=w_out.astype(BF16))
    c_g, conv_g, gu0, down0 = all_gather([(c, None), (conv_w, None), (loc['gu'], 0), (loc['down'], 0)], "gather_first")
    conv_w_full = conv_g.transpose(1, 2, 0, 3).reshape(DEPTH, CONV_WIDTH, LW)

    b_loc = lax.dynamic_slice(b_ada, (0, me * AC), (DEPTH, AC)).reshape(DEPTH, 1, AC)
    mod_cols, sc_all = ada_fwd(c_g.reshape(NDEV, D), w_ada, b_loc)
    (mod_rows,) = all_to_all([mod_cols.transpose(1, 0, 2)], "scatter_mod")
    mod = mod_rows.transpose(1, 0, 2).reshape(DEPTH, NMOD, D)

    small_w = {k: w[k] for k in PACKED if k != 'conv_w'}
    me_arr = jnp.reshape(me, (1,)).astype(jnp.int32)
    loss_loc, dx, big, small_g, dmod = local_fwd_bwd(me_arr, x[0], loss_target[0], mod, small_w, loc,
                                                     {('gu', 0): gu0, ('down', 0): down0}, conv_w_full)
    loss = lax.psum(loss_loc, ("x", "y", "c"))

    small_g['b_ada'] = dmod.reshape(DEPTH, NMOD * D)
    gpack = _pack(small_g)
    rows = gpack.shape[0]
    dmod_out = dmod.reshape(DEPTH, NDEV, AC).transpose(1, 0, 2)
    dmod_r, pack_r = all_to_all([dmod_out, gpack.reshape(NDEV, rows // NDEV, PACK_LANES)], "scatter_grads")
    (gsum_g,) = all_gather([(sum_parts(pack_r), None)], "gather_small_grads")
    gsum = gsum_g.reshape(1, 1, rows, PACK_LANES)

    res = {}
    t = lambda a: a.transpose(0, 2, 1)
    gu_t = big['gu']
    res['ffn1_w_gu'] = tuple(t(r) for r in adamw(gu_t, 0, t(w['ffn1_w_gu']), t(m['ffn1_w_gu']), t(v['ffn1_w_gu']),
                                                 "adamw_gu_a"))
    res['ffn2_w_gu'] = tuple(t(r) for r in adamw(gu_t, DEPTH, t(w['ffn2_w_gu']), t(m['ffn2_w_gu']),
                                                 t(v['ffn2_w_gu']), "adamw_gu_b"))
    res['ffn1_w_down'] = adamw(big['down'], 0, w['ffn1_w_down'], m['ffn1_w_down'], v['ffn1_w_down'], "adamw_down_a")
    res['ffn2_w_down'] = adamw(big['down'], DEPTH, w['ffn2_w_down'], m['ffn2_w_down'], v['ffn2_w_down'], "adamw_down_b")
    res['w_in'] = adamw(big['w_in'], 0, w['w_in'], m['w_in'], v['w_in'], "adamw_w_in")
    res['w_out'] = adamw(big['w_out'], 0, w['w_out'], m['w_out'], v['w_out'], "adamw_w_out")
    g_ada = ada_bwd(sc_all.T, dmod_r.transpose(1, 0, 2))
    res['w_ada'] = adamw(g_ada, 0, w['w_ada'], m['w_ada'], v['w_ada'], "adamw_w_ada")
    shapes = {k: w[k].shape for k in PACKED}
    shapes['conv_w'] = (DEPTH, CONV_WIDTH, LW)
    dummy = jnp.zeros(shapes['conv_w'], F32)
    packs = adamw(gsum, 0, _pack({**small_w, 'conv_w': dummy})[None], _pack({**{k: m[k] for k in small_w}, 'conv_w': dummy})[None],
                  _pack({**{k: v[k] for k in small_w}, 'conv_w': dummy})[None], "adamw_small")
    unpacked = [_unpack(b[0], shapes) for b in packs]
    for k in small_w:
        res[k] = tuple(u[k] for u in unpacked)
    gconv = lax.dynamic_slice(unpacked[0]['conv_w'], (0, 0, me * (LW // NDEV)), (DEPTH, CONV_WIDTH, LW // NDEV))
    cshape = (1, DEPTH * CONV_WIDTH, LW // NDEV)
    rc = adamw(gconv.reshape((1,) + cshape), 0, conv_w.reshape(cshape), m['conv_w'].reshape(cshape),
               v['conv_w'].reshape(cshape), "adamw_conv_w")
    res['conv_w'] = tuple(r.reshape(conv_w.shape) for r in rc)

    return (loss, dx[None], *[res[k][0] for k in WEIGHTS], *[res[k][1] for k in WEIGHTS],
            *[res[k][2] for k in WEIGHTS], *[res[k][3] for k in WEIGHTS])
```

```python
import jax
import jax.numpy as jnp
from jax import lax
from jax.experimental import pallas as pl
from jax.experimental.pallas import tpu as pltpu

F32 = jnp.float32
BF16 = jnp.bfloat16

NDEV = 8
DEPTH = 2
D = 1024
DFF = 2816
FC = 2 * DFF // NDEV
NCHUNK = DFF // FC
DR = DFF // NDEV
LW = 512
GW = 512
HD = 64
HEADS = 8
CHUNK = 128
PC = 2 * (LW + GW) // NDEV
OR = D // NDEV
NMOD = 9
AC = NMOD * D // NDEV
LC = 128
EPS = 1e-6
RG_LRU_C = 8.0
CONV_WIDTH = 4

ADAM_LR = 0.001
ADAM_B1 = 0.9
ADAM_B2 = 0.999
ADAM_EPS = 1e-08
ADAM_WD = 0.01
ADAM_STEP = 10

VMEM_LIMIT_BYTES = 60 * 1024 * 1024
MESH = pl.DeviceIdType.MESH
ANY = pl.BlockSpec(memory_space=pl.ANY)


def _cparams(*sem):
    return pltpu.CompilerParams(dimension_semantics=tuple(sem) if sem else None,
                                vmem_limit_bytes=VMEM_LIMIT_BYTES)


def _dot(a, b):
    return jnp.dot(a, b, preferred_element_type=F32)


def _dot_nt(a, b):
    return lax.dot_general(a, b, (((1,), (1,)), ((), ())), preferred_element_type=F32)


def _dot_tn(a, b):
    return lax.dot_general(a, b, (((0,), (0,)), ((), ())), preferred_element_type=F32)


def _split(a):
    hi = a.astype(BF16)
    lo = (a - hi.astype(F32)).astype(BF16)
    return hi, lo


def _dot3(a, b):
    ah, al = _split(a)
    bh, bl = _split(b)
    return _dot(ah, bh) + (_dot(ah, bl) + _dot(al, bh))


def _csum(a):
    return jnp.sum(a, axis=0, keepdims=True)


def _rmean(a):
    return jnp.mean(a, axis=-1, keepdims=True)


def _sigmoid(a):
    return 1.0 / (1.0 + jnp.exp(-a))


_GELU_K = 0.7978845608028654
_GELU_C = 0.044715


def _gelu(a):
    return 0.5 * a * (1.0 + jnp.tanh(_GELU_K * (a + _GELU_C * a * a * a)))


def _gelu_and_grad(a):
    a2 = a * a
    t = jnp.tanh(_GELU_K * (a + _GELU_C * a2 * a))
    half = 0.5 * (1.0 + t)
    return a * half, half + 0.5 * a * (1.0 - t * t) * (_GELU_K * (1.0 + 3.0 * _GELU_C * a2))


def _norm_mod(x, gain, scale, shift):
    rstd = lax.rsqrt(_rmean(x * x) + EPS)
    return (x * rstd * gain) * (1.0 + scale) + shift


def _norm_mod_bwd(dh, x, gain, scale):
    rstd = lax.rsqrt(_rmean(x * x) + EPS)
    xhat = x * rstd
    dshift = _csum(dh)
    dscale = _csum(dh * (xhat * gain))
    dhn = dh * (1.0 + scale)
    dgain = _csum(dhn * xhat)
    dxhat = dhn * gain
    dx = rstd * (dxhat - xhat * _rmean(dxhat * xhat))
    return dx, dshift, dscale, dgain


def _rms(x, gain):
    rstd = lax.rsqrt(_rmean(x * x) + EPS)
    return x * rstd * gain


def _rms_bwd(dy, x, gain):
    rstd = lax.rsqrt(_rmean(x * x) + EPS)
    xhat = x * rstd
    dgain = _csum(dy * xhat)
    dxhat = dy * gain
    return rstd * (dxhat - xhat * _rmean(dxhat * xhat)), dgain


PAIR = 2 * HD


def _seg_mean(a, pavg):
    hi, lo = _split(a)
    return jnp.concatenate([_dot(hi[:, p:p + PAIR], pavg) + _dot(lo[:, p:p + PAIR], pavg)
                            for p in range(0, a.shape[1], PAIR)], axis=1)


def _block_copies(src_hbm, dst_vmem, sems, rows):
    copies = []
    for k in range(NDEV):
        dst = dst_vmem.at[k] if rows is None else dst_vmem.at[pl.ds(k * rows, rows)]
        copies.append(pltpu.make_async_copy(src_hbm.at[k], dst, sems.at[k]))
    return copies


def _ffn_weight_fetch(wgu_hbm, wd_hbm, wgu_v, wd_v, sems):
    @pl.when(pl.program_id(0) == 0)
    def _():
        copies = _block_copies(wgu_hbm, wgu_v, sems.at[0], None) + _block_copies(wd_hbm, wd_v, sems.at[1], DR)
        for cp in copies:
            cp.start()
        for cp in copies:
            cp.wait()


def _place():
    return lax.axis_index("x"), lax.axis_index("y"), lax.axis_index("c")


def _slot(p):
    return 4 * p[0] + 2 * p[1] + p[2]


class GatherRide:
    def __init__(self, srcs):
        self.n = len(srcs)
        self.index = [i for _, i in srcs]
        self.args = [a for a, _ in srcs]
        self.out_shape = [jax.ShapeDtypeStruct((NDEV,) + (a.shape if i is None else a.shape[1:]), a.dtype)
                          for a, i in srcs]
        self.scratch = [pltpu.SemaphoreType.DMA((self.n, NDEV - 1)), pltpu.SemaphoreType.DMA((self.n, NDEV - 1)),
                        pltpu.SemaphoreType.DMA((self.n,))]

    def hooks(self, ins, outs, sems):
        send_sems, recv_sems, local_sems = sems
        n = self.n
        x, y, c = _place()
        me, sibling = (x, y, c), (x, y, 1 - c)
        chips = [(1 - x, y), (x, 1 - y), (1 - x, 1 - y)]

        def local(a):
            return ins[a] if self.index[a] is None else ins[a].at[self.index[a]]

        def copy(a, k, block, to, src=None):
            dst = outs[a].at[_slot(block)]
            return pltpu.make_async_remote_copy(
                src_ref=dst if src is None else src, dst_ref=dst,
                send_sem=send_sems.at[a, k], recv_sem=recv_sems.at[a, k],
                device_id=to, device_id_type=MESH)

        def mine():
            return [pltpu.make_async_copy(local(a), outs[a].at[_slot(me)], local_sems.at[a]) for a in range(n)]

        def first():
            cps = []
            for a in range(n):
                cps.append(copy(a, 0, me, sibling, src=local(a)))
                cps += [copy(a, 1 + j, me, (*chip, c), src=local(a)) for j, chip in enumerate(chips)]
            return cps

        def passed():
            return [copy(a, 4 + j, (*chip, c), sibling) for j, chip in enumerate(chips) for a in range(n)]

        def start():
            for cp in mine() + first():
                cp.start()

        def mid():
            for j, chip in enumerate(chips):
                for a in range(n):
                    copy(a, 1 + j, (*chip, c), me).wait_recv()
                    copy(a, 4 + j, (*chip, c), sibling).start()

        def finish():
            for a in range(n):
                copy(a, 0, sibling, me).wait_recv()
                for j, chip in enumerate(chips):
                    copy(a, 4 + j, (*chip, 1 - c), me).wait_recv()
            for cp in first() + passed():
                cp.wait_send()
            for cp in mine():
                cp.wait()

        return start, mid, finish


def all_gather(srcs, name):
    ride = GatherRide(srcs)
    n = ride.n

    def body(*refs):
        start, mid, finish = ride.hooks(refs[:n], refs[n:2 * n], refs[2 * n:])
        start()
        mid()
        finish()

    return pl.pallas_call(
        body, name=name,
        in_specs=[ANY] * n, out_specs=[ANY] * n, out_shape=ride.out_shape, scratch_shapes=ride.scratch,
    )(*ride.args)


def _call(core, ride, *, name, grid, in_specs, out_specs, out_shape, scratch_shapes, args):
    if ride is None:
        outs = pl.pallas_call(core, name=name, grid=grid, in_specs=in_specs, out_specs=out_specs,
                              out_shape=out_shape, scratch_shapes=scratch_shapes,
                              compiler_params=_cparams("arbitrary"))(*args)
        return outs, []
    n_in, n_out, n_sc, n = len(in_specs), len(out_shape), len(scratch_shapes), ride.n
    nsteps = grid[0]
    mid_step = max(nsteps - 2, 0)

    def body(*refs):
        cuts = [n_in, n_in + n, n_in + n + n_out, n_in + 2 * n + n_out, n_in + 2 * n + n_out + n_sc]
        ci, ri, co, ro, cs, rs = (refs[a:b] for a, b in zip([0] + cuts, cuts + [len(refs)]))
        start, mid, finish = ride.hooks(ri, ro, rs)
        i = pl.program_id(0)
        pl.when(i == 0)(start)
        core(*ci, *co, *cs)
        pl.when(i == mid_step)(mid)
        pl.when(i == nsteps - 1)(finish)

    outs = pl.pallas_call(
        body, name=name, grid=grid,
        in_specs=list(in_specs) + [ANY] * n, out_specs=list(out_specs) + [ANY] * n,
        out_shape=list(out_shape) + ride.out_shape, scratch_shapes=list(scratch_shapes) + ride.scratch,
        compiler_params=_cparams("arbitrary"))(*args, *ride.args)
    return outs[:n_out], outs[n_out:]


def all_to_all(arrs, name):
    n = len(arrs)

    def body(*refs):
        ins, outs = refs[:n], refs[n:2 * n]
        send_sems, recv_sems, local_sems = refs[2 * n:]
        x, y, c = _place()
        me = (x, y, c)

        def peer(k):
            return (1 - x if k & 4 else x, 1 - y if k & 2 else y, 1 - c if k & 1 else c)

        def copy(a, k):
            return pltpu.make_async_remote_copy(
                src_ref=ins[a].at[_slot(peer(k))], dst_ref=outs[a].at[_slot(me)],
                send_sem=send_sems.at[a, k - 1], recv_sem=recv_sems.at[a, k - 1],
                device_id=peer(k), device_id_type=MESH)

        def landing(a, k):
            return pltpu.make_async_remote_copy(
                src_ref=outs[a].at[_slot(peer(k))], dst_ref=outs[a].at[_slot(peer(k))],
                send_sem=send_sems.at[a, k - 1], recv_sem=recv_sems.at[a, k - 1],
                device_id=me, device_id_type=MESH)

        mine = [pltpu.make_async_copy(ins[a].at[_slot(me)], outs[a].at[_slot(me)], local_sems.at[a]) for a in range(n)]
        for cp in mine:
            cp.start()
        sends = [copy(a, k) for a in range(n) for k in range(1, NDEV)]
        for cp in sends:
            cp.start()
        for a in range(n):
            for k in range(1, NDEV):
                landing(a, k).wait_recv()
        for cp in sends:
            cp.wait_send()
        for cp in mine:
            cp.wait()

    return pl.pallas_call(
        body, name=name,
        in_specs=[ANY] * n, out_specs=[ANY] * n,
        out_shape=[jax.ShapeDtypeStruct(a.shape, a.dtype) for a in arrs],
        scratch_shapes=[pltpu.SemaphoreType.DMA((n, NDEV - 1)), pltpu.SemaphoreType.DMA((n, NDEV - 1)),
                        pltpu.SemaphoreType.DMA((n,))],
    )(*arrs)


FFN_TS = 256
FFN_FWD_TS = 512


def ffn_fwd(x, vec, wgu_g, wdown_g, tag, ride=None):
    S = x.shape[0]
    ts = min(FFN_FWD_TS, S)

    def body(x_ref, vec_ref, wgu_hbm, wd_hbm, xo_ref, h_ref, gu_ref, f_ref, wgu_v, wd_v, sems):
        _ffn_weight_fetch(wgu_hbm, wd_hbm, wgu_v, wd_v, sems)
        xv = x_ref[...]
        h = _norm_mod(xv, vec_ref[3:4, :], vec_ref[1:2, :], vec_ref[0:1, :]).astype(BF16)
        h_ref[...] = h
        acc = jnp.zeros((ts, D), F32)
        for j in range(NCHUNK):
            g = _dot(h, wgu_v[j])
            u = _dot(h, wgu_v[NCHUNK + j])
            gu_ref[j] = g.astype(BF16)
            gu_ref[NCHUNK + j] = u.astype(BF16)
            a = (g * _sigmoid(g) * u).astype(BF16)
            acc = acc + _dot(a, wd_v[pl.ds(j * FC, FC), :])
        f_ref[...] = acc.astype(BF16)
        xo_ref[...] = xv + (0.5 * vec_ref[2:3, :]) * acc

    return _call(
        body, ride, name=f"ffn_fwd_{tag}",
        grid=(S // ts,),
        in_specs=[pl.BlockSpec((ts, D), lambda i: (i, 0)),
                  pl.BlockSpec((8, D), lambda i: (0, 0)), ANY, ANY],
        out_specs=[pl.BlockSpec((ts, D), lambda i: (i, 0)),
                   pl.BlockSpec((ts, D), lambda i: (i, 0)),
                   pl.BlockSpec((NDEV, ts, FC), lambda i: (0, i, 0)),
                   pl.BlockSpec((ts, D), lambda i: (i, 0))],
        out_shape=[jax.ShapeDtypeStruct((S, D), F32), jax.ShapeDtypeStruct((S, D), BF16),
                   jax.ShapeDtypeStruct((NDEV, S, FC), BF16), jax.ShapeDtypeStruct((S, D), BF16)],
        scratch_shapes=[pltpu.VMEM((NDEV, D, FC), BF16), pltpu.VMEM((DFF, D), BF16),
                        pltpu.SemaphoreType.DMA((2, NDEV))],
        args=(x, vec, wgu_g, wdown_g))


def ffn_bwd(dxo, x, gu, f, vec, wgu_g, wdown_g, tag):
    S = x.shape[0]
    ts = min(FFN_TS, S)

    def body(dxo_ref, x_ref, gu_ref, f_ref, vec_ref, wgu_hbm, wd_hbm,
             dx_ref, dgu_ref, a_ref, df_ref, acc_ref, wgu_v, wd_v, sems):
        _ffn_weight_fetch(wgu_hbm, wd_hbm, wgu_v, wd_v, sems)

        @pl.when(pl.program_id(0) == 0)
        def _():
            acc_ref[...] = jnp.zeros_like(acc_ref)

        dxo_v = dxo_ref[...]
        dgate = 0.5 * _csum(dxo_v * f_ref[...].astype(F32))
        df = ((0.5 * vec_ref[2:3, :]) * dxo_v).astype(BF16)
        df_ref[...] = df
        dh = jnp.zeros((ts, D), F32)
        for j in range(NCHUNK):
            da = _dot_nt(df, wd_v[pl.ds(j * FC, FC), :])
            g = gu_ref[j].astype(F32)
            u = gu_ref[NCHUNK + j].astype(F32)
            sg = _sigmoid(g)
            si = g * sg
            a_ref[j] = (si * u).astype(BF16)
            dg = (da * u * (sg * (1.0 + g * (1.0 - sg)))).astype(BF16)
            du = (da * si).astype(BF16)
            dgu_ref[j] = dg
            dgu_ref[NCHUNK + j] = du
            dh = dh + _dot_nt(dg, wgu_v[j]) + _dot_nt(du, wgu_v[NCHUNK + j])
        dx, dshift, dscale, dgain = _norm_mod_bwd(dh, x_ref[...], vec_ref[3:4, :], vec_ref[1:2, :])
        dx_ref[...] = dx + dxo_v
        acc_ref[0:1, :] += dshift
        acc_ref[1:2, :] += dscale
        acc_ref[2:3, :] += dgate
        acc_ref[3:4, :] += dgain

    row = pl.BlockSpec((ts, D), lambda i: (i, 0))
    return pl.pallas_call(
        body, name=f"ffn_bwd_{tag}",
        grid=(S // ts,),
        in_specs=[row, row, pl.BlockSpec((NDEV, ts, FC), lambda i: (0, i, 0)), row,
                  pl.BlockSpec((8, D), lambda i: (0, 0)), ANY, ANY],
        out_specs=[row, pl.BlockSpec((NDEV, ts, FC), lambda i: (0, i, 0)),
                   pl.BlockSpec((NCHUNK, ts, FC), lambda i: (0, i, 0)), row,
                   pl.BlockSpec((8, D), lambda i: (0, 0))],
        out_shape=[jax.ShapeDtypeStruct((S, D), F32), jax.ShapeDtypeStruct((NDEV, S, FC), BF16),
                   jax.ShapeDtypeStruct((NCHUNK, S, FC), BF16), jax.ShapeDtypeStruct((S, D), BF16),
                   jax.ShapeDtypeStruct((8, D), F32)],
        scratch_shapes=[pltpu.VMEM((NDEV, D, FC), BF16), pltpu.VMEM((DFF, D), BF16),
                        pltpu.SemaphoreType.DMA((2, NDEV))],
        compiler_params=_cparams("arbitrary"),
    )(dxo, x, gu, f, vec, wgu_g, wdown_g)


NCHIP = NDEV // 2


def tn_matmul_scatter(me_arr, a, b, slot, nslots, prev, name, split=1):
    na, S, M = a.shape
    nb, _, N = b.shape
    ncall = NDEV // split
    ts = min(4096, S)
    nsteps = S // ts
    mp = M // split
    other_step = {1: lambda j: 2 * j, 2: lambda j: j, 8: lambda j: 0}[split]
    mine_step = {1: lambda j: 2 * j + 1, 2: lambda j: j, 8: lambda j: 0}[split]

    def group(k, me_ref):
        if split == 1:
            return jnp.bitwise_xor(me_ref[0], NDEV - 1 - k)
        if split == 2:
            return jnp.bitwise_xor(me_ref[0] // 2, NCHIP - 1 - k)
        return 0

    def body(me_ref, *refs):
        a_ref, b_ref = refs[0], refs[1]
        recv_ref, acc, sb_other, sb_mine, land, d2d_send, d2d_recv, ici_send, ici_recv = refs[-9:]
        k = pl.program_id(0)
        s = pl.program_id(1)
        x, y, c = _place()
        my_chip = 2 * x + y

        def chip_of(j):
            if split == 8:
                cx, cy = j // 2, j % 2
            else:
                flip = NCHIP - 1 - j
                cx, cy = (1 - x if flip & 2 else x), (1 - y if flip & 1 else y)
            return cx, cy, 2 * cx + cy

        def piece(j, core):
            if split == 1:
                return acc[...]
            start = core * mp if split == 2 else (2 * j + core) * mp
            return acc[pl.ds(pl.multiple_of(start, 8), mp), :]

        def to_sibling(j):
            return pltpu.make_async_remote_copy(
                src_ref=sb_other.at[j], dst_ref=land.at[j], send_sem=d2d_send.at[j], recv_sem=d2d_recv.at[j],
                device_id=(x, y, 1 - c), device_id_type=MESH)

        def to_owner(j):
            cx, cy, ci = chip_of(j)
            dst = recv_ref.at[my_chip, slot]
            return ci, pltpu.make_async_copy(sb_mine.at[j], dst, ici_send.at[j]), pltpu.make_async_remote_copy(
                src_ref=sb_mine.at[j], dst_ref=dst, send_sem=ici_send.at[j], recv_sem=ici_recv.at[my_chip],
                device_id=(cx, cy, c), device_id_type=MESH)

        if nsteps == 1:
            acc[...] = _dot_tn(a_ref[...], b_ref[...])
        else:
            @pl.when(s == 0)
            def _():
                acc[...] = jnp.zeros_like(acc)

            acc[...] += _dot_tn(a_ref[...], b_ref[...])

        for kk in range(ncall):
            @pl.when((s == nsteps - 1) & (k == kk))
            def _():
                for j in range(NCHIP):
                    if other_step(j) == kk:
                        sb_other[j] = piece(j, 1 - c).astype(BF16)
                        to_sibling(j).start()
                for j in range(NCHIP):
                    if mine_step(j) == kk:
                        to_sibling(j).wait_recv()
                        sb_mine[j] = (piece(j, c) + land[j].astype(F32)).astype(BF16)
                        ci, loc, rem = to_owner(j)
                        pl.when(ci == my_chip)(loc.start)
                        pl.when(ci != my_chip)(rem.start)

        @pl.when((s == nsteps - 1) & (k == ncall - 1))
        def _():
            for j in range(NCHIP):
                to_sibling(j).wait_send()
                ci, loc, rem = to_owner(j)
                pl.when(ci == my_chip)(loc.wait)
                pl.when(ci != my_chip)(rem.wait_send)
            for src in range(NCHIP):
                @pl.when(my_chip != src)
                def _():
                    pltpu.make_async_remote_copy(
                        src_ref=recv_ref.at[src, slot], dst_ref=recv_ref.at[src, slot],
                        send_sem=ici_send.at[src], recv_sem=ici_recv.at[src],
                        device_id=(src // 2, src % 2, c), device_id_type=MESH).wait_recv()

    in_specs = [pl.BlockSpec((None, ts, M), (lambda k, s, me: (group(k, me), s, 0)) if na > 1 else (lambda k, s, me: (0, s, 0))),
                pl.BlockSpec((None, ts, N), (lambda k, s, me: (group(k, me), s, 0)) if nb > 1 else (lambda k, s, me: (0, s, 0)))]
    args = [me_arr, a, b]
    aliases = {}
    if prev is not None:
        in_specs.append(ANY)
        args.append(prev)
        aliases = {3: 0}
    return pl.pallas_call(
        body, name=name,
        grid_spec=pltpu.PrefetchScalarGridSpec(
            num_scalar_prefetch=1, grid=(ncall, nsteps), in_specs=in_specs, out_specs=ANY,
            scratch_shapes=[pltpu.VMEM((M, N), F32), pltpu.VMEM((NCHIP, mp, N), BF16), pltpu.VMEM((NCHIP, mp, N), BF16),
                            pltpu.VMEM((NCHIP, mp, N), BF16), pltpu.SemaphoreType.DMA((NCHIP,)),
                            pltpu.SemaphoreType.DMA((NCHIP,)), pltpu.SemaphoreType.DMA((NCHIP,)),
                            pltpu.SemaphoreType.DMA((NCHIP,))]),
        out_shape=jax.ShapeDtypeStruct((NCHIP, nslots, mp, N), BF16),
        input_output_aliases=aliases,
        compiler_params=_cparams("arbitrary", "arbitrary"),
    )(*args)


MIX_TS = 256
MIX_IN_TS = 512


def mix_in_fwd(x, vec, win_g, tag, ride=None):
    S = x.shape[0]
    ts = min(MIX_IN_TS, S)

    def body(x_ref, vec_ref, win_ref, hm_ref, proj_ref):
        h = _norm_mod(x_ref[...], vec_ref[3:4, :], vec_ref[1:2, :], vec_ref[0:1, :]).astype(BF16)
        hm_ref[...] = h
        for k in range(NDEV):
            proj_ref[k] = _dot(h, win_ref[k])

    return _call(
        body, ride, name=f"mix_in_fwd_{tag}",
        grid=(S // ts,),
        in_specs=[pl.BlockSpec((ts, D), lambda i: (i, 0)), pl.BlockSpec((8, D), lambda i: (0, 0)),
                  pl.BlockSpec((NDEV, D, PC), lambda i: (0, 0, 0))],
        out_specs=[pl.BlockSpec((ts, D), lambda i: (i, 0)),
                   pl.BlockSpec((NDEV, ts, PC), lambda i: (0, i, 0))],
        out_shape=[jax.ShapeDtypeStruct((S, D), BF16), jax.ShapeDtypeStruct((NDEV, S, PC), F32)],
        scratch_shapes=[], args=(x, vec, win_g))


def mix_in_bwd(dproj, x, dxo, vec, win_t, tag):
    S = x.shape[0]
    ts = min(MIX_IN_TS, S)

    def body(dp_ref, x_ref, dxo_ref, vec_ref, win_ref, dx_ref, acc_ref):
        @pl.when(pl.program_id(0) == 0)
        def _():
            acc_ref[...] = jnp.zeros_like(acc_ref)

        dh = jnp.zeros((ts, D), F32)
        for k in range(NDEV):
            dh = dh + _dot(dp_ref[k], win_ref[k])
        dx, dshift, dscale, dgain = _norm_mod_bwd(dh, x_ref[...], vec_ref[3:4, :], vec_ref[1:2, :])
        dx_ref[...] = dx + dxo_ref[...]
        acc_ref[0:1, :] += dshift
        acc_ref[1:2, :] += dscale
        acc_ref[3:4, :] += dgain

    row = pl.BlockSpec((ts, D), lambda i: (i, 0))
    return pl.pallas_call(
        body, name=f"mix_in_bwd_{tag}",
        grid=(S // ts,),
        in_specs=[pl.BlockSpec((NDEV, ts, PC), lambda i: (0, i, 0)), row, row,
                  pl.BlockSpec((8, D), lambda i: (0, 0)),
                  pl.BlockSpec((NDEV, PC, D), lambda i: (0, 0, 0))],
        out_specs=[row, pl.BlockSpec((8, D), lambda i: (0, 0))],
        out_shape=[jax.ShapeDtypeStruct((S, D), F32), jax.ShapeDtypeStruct((8, D), F32)],
        compiler_params=_cparams("arbitrary"),
    )(dproj, x, dxo, vec, win_t)


SCAN_UNROLL = 4


def _shift_down(z, k, row):
    return jnp.where(row >= k, pltpu.roll(z, k, 0), 0.0)


def _shift_up(z, k, row, n):
    return jnp.where(row < n - k, pltpu.roll(z, n - k, 0), 0.0)


def _lru_gates(xc, lp_ref, wa_ref, wx_ref):
    xcb = xc.astype(BF16)
    ra = _sigmoid(_dot(xcb, wa_ref[...]) + lp_ref[5:6, :])
    ix = _sigmoid(_dot(xcb, wx_ref[...]) + lp_ref[6:7, :])
    lam = lp_ref[7:8, :]
    ls = jnp.minimum(lam, 0.0) - jnp.log(1.0 + jnp.exp(-jnp.abs(lam)))
    log_a = (RG_LRU_C * ls) * ra
    a = jnp.exp(log_a)
    mult = jnp.sqrt(-jnp.tanh(log_a) * (a * a + 1.0))
    return ra, ix, ls, a, mult


def _conv(x, lp_ref, row):
    return (lp_ref[4:5, :] + lp_ref[3:4, :] * x + lp_ref[2:3, :] * _shift_down(x, 1, row)
            + lp_ref[1:2, :] * _shift_down(x, 2, row) + lp_ref[0:1, :] * _shift_down(x, 3, row))


def lru_fwd(proj, lp, wa_t, wx_t, tag, ride=None):
    S = proj.shape[1]
    nblk = S // 8

    def body(x_ref, g_ref, lp_ref, wa_ref, wx_ref, y_ref, xc_ref, h_ref, a_s, b_s):
        x = x_ref[...]
        row = lax.broadcasted_iota(jnp.int32, x.shape, 0)
        xc = _conv(x, lp_ref, row)
        xc_ref[...] = xc
        ra, ix, ls, a, mult = _lru_gates(xc, lp_ref, wa_ref, wx_ref)
        a_s[...] = a
        b_s[...] = mult * (ix * xc)
        rowb = lax.broadcasted_iota(jnp.int32, (8, LC), 0)

        def step(i, carry):
            for q in range(SCAN_UNROLL):
                r0 = pl.multiple_of((i * SCAN_UNROLL + q) * 8, 8)
                A = a_s[pl.ds(r0, 8), :]
                B = b_s[pl.ds(r0, 8), :]
                for d in (1, 2, 4):
                    m = rowb >= d
                    As = jnp.where(m, pltpu.roll(A, d, 0), 1.0)
                    Bs = jnp.where(m, pltpu.roll(B, d, 0), 0.0)
                    B = A * Bs + B
                    A = A * As
                H = B + A * carry
                h_ref[pl.ds(r0, 8), :] = H
                carry = H[7:8, :]
            return carry

        lax.fori_loop(0, nblk // SCAN_UNROLL, step, jnp.zeros((1, LC), F32))
        y_ref[...] = h_ref[...] * _gelu(g_ref[...])

    col = pl.BlockSpec((S, LC), lambda c: (0, c))
    return _call(
        body, ride, name=f"lru_fwd_{tag}",
        grid=(LW // LC,),
        in_specs=[pl.BlockSpec((None, S, LC), lambda c: (c // 2, 0, c % 2)),
                  pl.BlockSpec((None, S, LC), lambda c: (2 + c // 2, 0, c % 2)),
                  pl.BlockSpec((8, LC), lambda c: (0, c)),
                  pl.BlockSpec((None, LC, LC), lambda c: (c, 0, 0)),
                  pl.BlockSpec((None, LC, LC), lambda c: (c, 0, 0))],
        out_specs=[col, col, col],
        out_shape=[jax.ShapeDtypeStruct((S, LW), F32)] * 3,
        scratch_shapes=[pltpu.VMEM((S, LC), F32), pltpu.VMEM((S, LC), F32)],
        args=(proj, proj, lp, wa_t, wx_t))


def lru_bwd(dy, proj, xc_all, hst, lp, wa_t, wx_t, tag):
    S = proj.shape[1]
    nblk = S // 8

    def body(dy_ref, x_ref, g_ref, xc_ref, h_ref, lp_ref, wa_ref, wx_ref,
             dx_ref, dg_ref, dlp_ref, dwa_ref, dwx_ref, c_s, l_s):
        xc = xc_ref[...]
        row = lax.broadcasted_iota(jnp.int32, xc.shape, 0)
        ra, ix, ls, a, mult = _lru_gates(xc, lp_ref, wa_ref, wx_ref)
        g = g_ref[...]
        dyv = dy_ref[...]
        h = h_ref[...]
        gelu_g, gelu_grad_g = _gelu_and_grad(g)
        dg_ref[...] = (dyv * h * gelu_grad_g).astype(BF16)
        c_s[...] = _shift_up(a, 1, row, S)
        l_s[...] = dyv * gelu_g
        rowb = lax.broadcasted_iota(jnp.int32, (8, LC), 0)

        def step(i, carry):
            for q in range(SCAN_UNROLL):
                r0 = pl.multiple_of((nblk - 1 - (i * SCAN_UNROLL + q)) * 8, 8)
                C = c_s[pl.ds(r0, 8), :]
                L = l_s[pl.ds(r0, 8), :]
                for d in (1, 2, 4):
                    m = rowb < 8 - d
                    Cs = jnp.where(m, pltpu.roll(C, 8 - d, 0), 1.0)
                    Ls = jnp.where(m, pltpu.roll(L, 8 - d, 0), 0.0)
                    L = C * Ls + L
                    C = C * Cs
                L = L + C * carry
                l_s[pl.ds(r0, 8), :] = L
                carry = L[0:1, :]
            return carry

        lax.fori_loop(0, nblk // SCAN_UNROLL, step, jnp.zeros((1, LC), F32))
        db = l_s[...]
        da = db * _shift_down(h, 1, row)
        ixc = ix * xc
        dmult = db * ixc
        dix = db * (mult * xc)
        dxc = db * (mult * ix)
        dlog_a = da * a - dmult * (a * a) / mult
        dra = dlog_a * (RG_LRU_C * ls)
        dls = _csum(dlog_a * ra) * RG_LRU_C
        lam = lp_ref[7:8, :]
        dlam = dls * _sigmoid(-lam)
        dpa = dra * ra * (1.0 - ra)
        dpx = dix * ix * (1.0 - ix)
        dpab = dpa.astype(BF16)
        dpxb = dpx.astype(BF16)
        xcb = xc.astype(BF16)
        dwa_ref[...] = _dot_tn(xcb, dpab)
        dwx_ref[...] = _dot_tn(xcb, dpxb)
        dxc = dxc + _dot_nt(dpab, wa_ref[...]) + _dot_nt(dpxb, wx_ref[...])
        x = x_ref[...]
        dlp_ref[0:1, :] = _csum(dxc * _shift_down(x, 3, row))
        dlp_ref[1:2, :] = _csum(dxc * _shift_down(x, 2, row))
        dlp_ref[2:3, :] = _csum(dxc * _shift_down(x, 1, row))
        dlp_ref[3:4, :] = _csum(dxc * x)
        dlp_ref[4:5, :] = _csum(dxc)
        dlp_ref[5:6, :] = _csum(dpa)
        dlp_ref[6:7, :] = _csum(dpx)
        dlp_ref[7:8, :] = dlam
        dx = (lp_ref[3:4, :] * dxc + lp_ref[2:3, :] * _shift_up(dxc, 1, row, S)
              + lp_ref[1:2, :] * _shift_up(dxc, 2, row, S) + lp_ref[0:1, :] * _shift_up(dxc, 3, row, S))
        dx_ref[...] = dx.astype(BF16)

    col = pl.BlockSpec((S, LC), lambda c: (0, c))
    pcol = pl.BlockSpec((None, S, LC), lambda c: (c // 2, 0, c % 2))
    return pl.pallas_call(
        body, name=f"lru_bwd_{tag}",
        grid=(LW // LC,),
        in_specs=[col, pcol, pl.BlockSpec((None, S, LC), lambda c: (2 + c // 2, 0, c % 2)), col, col,
                  pl.BlockSpec((8, LC), lambda c: (0, c)),
                  pl.BlockSpec((None, LC, LC), lambda c: (c, 0, 0)),
                  pl.BlockSpec((None, LC, LC), lambda c: (c, 0, 0))],
        out_specs=[pcol, pcol, pl.BlockSpec((8, LC), lambda c: (0, c)),
                   pl.BlockSpec((None, LC, LC), lambda c: (c, 0, 0)),
                   pl.BlockSpec((None, LC, LC), lambda c: (c, 0, 0))],
        out_shape=[jax.ShapeDtypeStruct((2, S, PC), BF16), jax.ShapeDtypeStruct((2, S, PC), BF16),
                   jax.ShapeDtypeStruct((8, LW), F32),
                   jax.ShapeDtypeStruct((LW // LC, LC, LC), F32), jax.ShapeDtypeStruct((LW // LC, LC, LC), F32)],
        scratch_shapes=[pltpu.VMEM((S, LC), F32), pltpu.VMEM((S, LC), F32)],
        compiler_params=_cparams("arbitrary"),
    )(dy, proj, proj, xc_all, hst, lp, wa_t, wx_t)


def _pair_stack(zp, low):
    return jnp.concatenate([jnp.where(low, zp, 0.0), jnp.where(low, 0.0, zp)], axis=0).astype(BF16)


def _spatial(w_ref, zc, low):
    return jnp.concatenate(
        [_dot(w_ref[:, 2 * p * CHUNK:2 * (p + 1) * CHUNK], _pair_stack(zc[:, p * PAIR:(p + 1) * PAIR], low))
         for p in range(GW // PAIR)], axis=1)


def _gmlp_fwd_parts(u, v, gp_ref, wcat_ref, bz_ref, pavg_ref, ts, with_grad=False):
    if with_grad:
        ug, ugrad = _gelu_and_grad(u)
        vg, vgrad = _gelu_and_grad(v)
    else:
        ug, vg, ugrad, vgrad = _gelu(u), _gelu(v), None, None
    pavg = pavg_ref[...]
    vc = vg - _seg_mean(vg, pavg)
    rs = lax.rsqrt(_seg_mean(vc * vc, pavg) + EPS)
    vhat = vc * rs
    vh = vhat * gp_ref[0:1, :]
    low = lax.broadcasted_iota(jnp.int32, (CHUNK, PAIR), 1) < HD
    zs = [_spatial(wcat_ref, vh[n * CHUNK:(n + 1) * CHUNK, :], low) + bz_ref[...] for n in range(ts // CHUNK)]
    z = jnp.concatenate(zs, axis=0) if len(zs) > 1 else zs[0]
    return ug, rs, vhat, vh, z, ugrad, vgrad


def mix_out_fwd(proj, ylru, x, vec, gp, wcat, bz, pavg, wout_g, tag, ride=None):
    S = x.shape[0]
    ts = min(MIX_TS, S)

    def body(u_ref, v_ref, yl_ref, x_ref, vec_ref, gp_ref, wcat_ref, bz_ref, pavg_ref, wout_ref,
             xo_ref, y_ref, fo_ref):
        u = jnp.concatenate([u_ref[0], u_ref[1]], axis=1)
        v = jnp.concatenate([v_ref[0], v_ref[1]], axis=1)
        ug, _, _, _, z, _, _ = _gmlp_fwd_parts(u, v, gp_ref, wcat_ref, bz_ref, pavg_ref, ts)
        n1 = _rms(yl_ref[...], gp_ref[1:2, :])
        n2 = _rms(ug * z, gp_ref[2:3, :])
        y = jnp.concatenate([n1, n2], axis=1).astype(BF16)
        y_ref[...] = y
        fo = jnp.zeros((ts, D), F32)
        for k in range(NDEV):
            fo = fo + _dot(y[:, k * OR:(k + 1) * OR], wout_ref[k])
        fo_ref[...] = fo.astype(BF16)
        xo_ref[...] = x_ref[...] + vec_ref[2:3, :] * fo

    row = pl.BlockSpec((ts, D), lambda i: (i, 0))
    full = lambda shp: pl.BlockSpec(shp, lambda i: tuple(0 for _ in shp))
    return _call(
        body, ride, name=f"mix_out_fwd_{tag}",
        grid=(S // ts,),
        in_specs=[pl.BlockSpec((2, ts, PC), lambda i: (2, i, 0)), pl.BlockSpec((2, ts, PC), lambda i: (3, i, 0)),
                  pl.BlockSpec((ts, LW), lambda i: (i, 0)), row, full((8, D)), full((8, GW)),
                  full((CHUNK, HEADS * CHUNK)), full((CHUNK, GW)), full((PAIR, PAIR)),
                  pl.BlockSpec((NDEV, OR, D), lambda i: (0, 0, 0))],
        out_specs=[row, row, row],
        out_shape=[jax.ShapeDtypeStruct((S, D), F32), jax.ShapeDtypeStruct((S, D), BF16),
                   jax.ShapeDtypeStruct((S, D), BF16)],
        scratch_shapes=[], args=(proj, proj, ylru, x, vec, gp, wcat, bz, pavg, wout_g))


def mix_out_bwd(dxo, proj, ylru, fo, vec, gp, wcat, wcat_t, bz, pavg, wout_t, tag):
    S = dxo.shape[0]
    ts = min(MIX_TS, S)

    def body(dxo_ref, u_ref, v_ref, yl_ref, fo_ref, vec_ref, gp_ref, wcat_ref, wcatt_ref, bz_ref, pavg_ref,
             wout_ref, dyo_ref, dyl_ref, duv_ref, acc_ref, dgp_ref, dwm_ref, dbz_ref):
        @pl.when(pl.program_id(0) == 0)
        def _():
            acc_ref[...] = jnp.zeros_like(acc_ref)
            dgp_ref[...] = jnp.zeros_like(dgp_ref)
            dwm_ref[...] = jnp.zeros_like(dwm_ref)
            dbz_ref[...] = jnp.zeros_like(dbz_ref)

        dxo_v = dxo_ref[...]
        acc_ref[2:3, :] += _csum(dxo_v * fo_ref[...].astype(F32))
        dyo = (vec_ref[2:3, :] * dxo_v).astype(BF16)
        dyo_ref[...] = dyo
        dn = _dot(dyo, wout_ref[...])
        dn1, dn2 = dn[:, :LW], dn[:, LW:]
        dyl, dg1 = _rms_bwd(dn1, yl_ref[...], gp_ref[1:2, :])
        dyl_ref[...] = dyl
        u = jnp.concatenate([u_ref[0], u_ref[1]], axis=1)
        v = jnp.concatenate([v_ref[0], v_ref[1]], axis=1)
        ug, rs, vhat, vh, z, ugrad, vgrad = _gmlp_fwd_parts(u, v, gp_ref, wcat_ref, bz_ref, pavg_ref, ts,
                                                            with_grad=True)
        dyg, dg2 = _rms_bwd(dn2, ug * z, gp_ref[2:3, :])
        du = (dyg * z) * ugrad
        dz = dyg * ug
        low = lax.broadcasted_iota(jnp.int32, (CHUNK, PAIR), 1) < HD
        vhb = vh.astype(BF16)
        dvhs = []
        dbz = jnp.zeros((CHUNK, GW), F32)
        dwm = [jnp.zeros((2 * CHUNK, CHUNK), F32) for _ in range(GW // PAIR)]
        for n in range(ts // CHUNK):
            dzc = dz[n * CHUNK:(n + 1) * CHUNK, :]
            dbz = dbz + dzc
            for p in range(GW // PAIR):
                stack = _pair_stack(dzc[:, p * PAIR:(p + 1) * PAIR], low)
                dwm[p] = dwm[p] + _dot_nt(stack, vhb[n * CHUNK:(n + 1) * CHUNK, p * PAIR:(p + 1) * PAIR])
            dvhs.append(_spatial(wcatt_ref, dzc, low))
        dbz_ref[...] += dbz
        for p in range(GW // PAIR):
            dwm_ref[2 * p * CHUNK:2 * (p + 1) * CHUNK, :] += dwm[p]
        dvh = jnp.concatenate(dvhs, axis=0) if len(dvhs) > 1 else dvhs[0]
        pavg = pavg_ref[...]
        dvn = _csum(dvh * vhat)
        dvhat = dvh * gp_ref[0:1, :]
        dvg = rs * (dvhat - _seg_mean(dvhat, pavg) - vhat * _seg_mean(dvhat * vhat, pavg))
        dv = dvg * vgrad
        duv_ref[0] = du[:, :PC].astype(BF16)
        duv_ref[1] = du[:, PC:].astype(BF16)
        duv_ref[2] = dv[:, :PC].astype(BF16)
        duv_ref[3] = dv[:, PC:].astype(BF16)
        dgp_ref[0:1, :] += dvn
        dgp_ref[1:2, :] += dg1
        dgp_ref[2:3, :] += dg2

    row = pl.BlockSpec((ts, D), lambda i: (i, 0))
    full = lambda shp: pl.BlockSpec(shp, lambda i: tuple(0 for _ in shp))
    return pl.pallas_call(
        body, name=f"mix_out_bwd_{tag}",
        grid=(S // ts,),
        in_specs=[row, pl.BlockSpec((2, ts, PC), lambda i: (2, i, 0)), pl.BlockSpec((2, ts, PC), lambda i: (3, i, 0)),
                  pl.BlockSpec((ts, LW), lambda i: (i, 0)), row, full((8, D)), full((8, GW)),
                  full((CHUNK, HEADS * CHUNK)), full((CHUNK, HEADS * CHUNK)), full((CHUNK, GW)), full((PAIR, PAIR)),
                  full((D, D))],
        out_specs=[row, pl.BlockSpec((ts, LW), lambda i: (i, 0)), pl.BlockSpec((4, ts, PC), lambda i: (0, i, 0)),
                   full((8, D)), full((8, GW)), full((HEADS * CHUNK, CHUNK)), full((CHUNK, GW))],
        out_shape=[jax.ShapeDtypeStruct((S, D), BF16), jax.ShapeDtypeStruct((S, LW), F32),
                   jax.ShapeDtypeStruct((4, S, PC), BF16), jax.ShapeDtypeStruct((8, D), F32),
                   jax.ShapeDtypeStruct((8, GW), F32), jax.ShapeDtypeStruct((HEADS * CHUNK, CHUNK), F32),
                   jax.ShapeDtypeStruct((CHUNK, GW), F32)],
        compiler_params=_cparams("arbitrary"),
    )(dxo, proj, proj, ylru, fo, vec, gp, wcat, wcat_t, bz, pavg, wout_t)


def final_loss(x, target, gain):
    S = x.shape[0]
    ts = min(512, S)

    def body(x_ref, t_ref, g_ref, loss_ref, dx_ref, dg_ref):
        @pl.when(pl.program_id(0) == 0)
        def _():
            loss_ref[...] = jnp.zeros_like(loss_ref)
            dg_ref[...] = jnp.zeros_like(dg_ref)

        xv = x_ref[...]
        gain_v = g_ref[0:1, :]
        rstd = lax.rsqrt(_rmean(xv * xv) + EPS)
        xhat = xv * rstd
        err = xhat * gain_v - t_ref[...]
        loss_ref[...] += 0.5 * _csum(_rmean(err * err))
        dy = err * (1.0 / D)
        dg_ref[0:1, :] += _csum(dy * xhat)
        dxhat = dy * gain_v
        dx_ref[...] = rstd * (dxhat - xhat * _rmean(dxhat * xhat))

    row = pl.BlockSpec((ts, D), lambda i: (i, 0))
    return pl.pallas_call(
        body, name="final_loss",
        grid=(S // ts,),
        in_specs=[row, row, pl.BlockSpec((8, D), lambda i: (0, 0))],
        out_specs=[pl.BlockSpec((8, 128), lambda i: (0, 0)), row, pl.BlockSpec((8, D), lambda i: (0, 0))],
        out_shape=[jax.ShapeDtypeStruct((8, 128), F32), jax.ShapeDtypeStruct((S, D), F32),
                   jax.ShapeDtypeStruct((8, D), F32)],
        compiler_params=_cparams("arbitrary"),
    )(x, target, gain)


def _vec(mod_l, j, gain):
    return jnp.concatenate([mod_l[3 * j:3 * j + 3], gain[None, :], jnp.zeros((4, D), F32)], axis=0)


def _block_diag_tiles(w):
    w4 = w.reshape(LW // LC, 2, HD, HD)
    eye2 = jnp.eye(2, dtype=w.dtype)
    return (w4[:, :, :, None, :] * eye2[None, :, None, :, None]).reshape(LW // LC, LC, LC).astype(BF16)


def _block_diag_extract(dw):
    d5 = dw.reshape(LW // LC, 2, HD, 2, HD)
    return jnp.einsum('cihkj,ik->cihj', d5, jnp.eye(2, dtype=dw.dtype)).reshape(HEADS, HD, HD)


def _layer_params(l, p, conv_w_full):
    lp = jnp.concatenate([conv_w_full[l], p['conv_b'][l][None], p['gate_a_b'][l].reshape(1, LW),
                          p['gate_x_b'][l].reshape(1, LW), p['lru_lambda'][l][None]], axis=0)
    gp = jnp.concatenate([p['v_norm'][l][None], p['lru_out_norm'][l][None], p['gmlp_out_norm'][l][None],
                          jnp.zeros((5, GW), F32)], axis=0)
    ws = p['spatial_w'][l] * jnp.tril(jnp.ones((CHUNK, CHUNK), F32))
    wcat = ws.transpose(1, 0, 2).reshape(CHUNK, HEADS * CHUNK).astype(BF16)
    wcat_t = ws.transpose(2, 0, 1).reshape(CHUNK, HEADS * CHUNK).astype(BF16)
    bz = jnp.repeat(p['spatial_b'][l].T, HD, axis=1)
    return dict(lp=lp, gp=gp, wcat=wcat, wcat_t=wcat_t, bz=bz,
                wa_t=_block_diag_tiles(p['gate_a_w'][l]), wx_t=_block_diag_tiles(p['gate_x_w'][l]))


def _pavg():
    return jnp.kron(jnp.eye(2, dtype=F32), jnp.full((HD, HD), 1.0 / HD, F32)).astype(BF16)


GATHER_RIDES = {
    ('ffn_a', 0): [('w_in', 0), ('gu', DEPTH)],
    ('mix_in', 0): [('w_out', 0)],
    ('lru', 0): [('down', DEPTH)],
    ('mix_out', 0): [('down', 1)],
    ('ffn_b', 0): [('gu', 1), ('w_in', 1)],
    ('ffn_a', 1): [('gu', DEPTH + 1), ('w_out', 1)],
    ('mix_in', 1): [('down', DEPTH + 1)],
}


def local_fwd_bwd(me_arr, x, target, mod, p, loc, gathered, conv_w_full):
    pavg = _pavg()
    g = dict(gathered)

    def ride(call, l):
        todo = GATHER_RIDES.get((call, l))
        return None if todo is None else (todo, GatherRide([(loc[kind], slot) for kind, slot in todo]))

    def run(fn, call, l, *args):
        r = ride(call, l)
        outs, got = fn(*args, ride=None if r is None else r[1])
        if r is not None:
            g.update(dict(zip(r[0], got)))
        return outs

    saved = []
    h = x
    for l in range(DEPTH):
        q = _layer_params(l, p, conv_w_full)
        v1 = _vec(mod[l], 0, p['ffn1_norm'][l])
        vm = _vec(mod[l], 1, p['mix_norm'][l])
        v2 = _vec(mod[l], 2, p['ffn2_norm'][l])
        x0 = h
        x1, h1, gu1, f1 = run(ffn_fwd, 'ffn_a', l, x0, v1, g['gu', l], g['down', l], f"a{l}")
        hm, proj = run(mix_in_fwd, 'mix_in', l, x1, vm, g['w_in', l], f"{l}")
        ylru, xc, hst = run(lru_fwd, 'lru', l, proj, q['lp'], q['wa_t'], q['wx_t'], f"{l}")
        x2, y, fo = run(mix_out_fwd, 'mix_out', l, proj, ylru, x1, vm, q['gp'], q['wcat'], q['bz'], pavg,
                        g['w_out', l], f"{l}")
        x3, h2, gu2, f2 = run(ffn_fwd, 'ffn_b', l, x2, v2, g['gu', DEPTH + l], g['down', DEPTH + l], f"b{l}")
        saved.append(dict(q=q, v1=v1, vm=vm, v2=v2, x0=x0, x1=x1, x2=x2, h1=h1, gu1=gu1, f1=f1, hm=hm, proj=proj,
                          ylru=ylru, xc=xc, hst=hst, y=y, fo=fo, h2=h2, gu2=gu2, f2=f2))
        h = x3
    fin = jnp.concatenate([p['final_norm'][None], jnp.zeros((7, D), F32)], axis=0)
    loss8, dx, dfin = final_loss(h, target, fin)
    loss = loss8[0, 0]

    big = dict(gu=None, down=None, w_in=None, w_out=None)
    small = {k: [None] * DEPTH for k in ('ffn1_norm', 'mix_norm', 'ffn2_norm', 'conv_w', 'conv_b', 'gate_a_w',
                                         'gate_a_b', 'gate_x_w', 'gate_x_b', 'lru_lambda', 'v_norm', 'spatial_w',
                                         'spatial_b', 'lru_out_norm', 'gmlp_out_norm')}
    dmod = [None] * DEPTH
    tril = jnp.tril(jnp.ones((CHUNK, CHUNK), F32))
    for l in reversed(range(DEPTH)):
        sv = saved[l]
        q = sv['q']
        dx2, dgu, a, df, acc2 = ffn_bwd(dx, sv['x2'], sv['gu2'], sv['f2'], sv['v2'],
                                        g['gu', DEPTH + l], g['down', DEPTH + l], f"b{l}")
        big['gu'] = tn_matmul_scatter(me_arr, dgu, sv['h2'][None], DEPTH + l, 2 * DEPTH, big['gu'], f"dw_gu_b{l}")
        big['down'] = tn_matmul_scatter(me_arr, a, df[None], DEPTH + l, 2 * DEPTH, big['down'], f"dw_down_b{l}", split=2)
        dyo, dylru, duv, accmo, dgp, dwm, dbz = mix_out_bwd(dx2, sv['proj'], sv['ylru'], sv['fo'], sv['vm'], q['gp'],
                                                             q['wcat'], q['wcat_t'], q['bz'], pavg,
                                                             g['w_out', l].reshape(D, D).T, f"{l}")
        big['w_out'] = tn_matmul_scatter(me_arr, sv['y'][None], dyo[None], l, DEPTH, big['w_out'], f"dw_out_{l}",
                                         split=NDEV)
        dxl, dgl, dlp, dwa, dwx = lru_bwd(dylru, sv['proj'], sv['xc'], sv['hst'], q['lp'], q['wa_t'], q['wx_t'], f"{l}")
        dproj = jnp.concatenate([dxl, dgl, duv], axis=0)
        dx1, accmi = mix_in_bwd(dproj, sv['x1'], dx2, sv['vm'], g['w_in', l].transpose(0, 2, 1), f"{l}")
        big['w_in'] = tn_matmul_scatter(me_arr, sv['hm'][None], dproj, l, DEPTH, big['w_in'], f"dw_in_{l}")
        dx0, dgu, a, df, acc1 = ffn_bwd(dx1, sv['x0'], sv['gu1'], sv['f1'], sv['v1'],
                                        g['gu', l], g['down', l], f"a{l}")
        big['gu'] = tn_matmul_scatter(me_arr, dgu, sv['h1'][None], l, 2 * DEPTH, big['gu'], f"dw_gu_a{l}")
        big['down'] = tn_matmul_scatter(me_arr, a, df[None], l, 2 * DEPTH, big['down'], f"dw_down_a{l}", split=2)
        dx = dx0
        dmod[l] = jnp.concatenate([acc1[0:3], accmi[0:2], accmo[2:3], acc2[0:3]], axis=0)
        small['ffn1_norm'][l] = acc1[3]
        small['mix_norm'][l] = accmi[3]
        small['ffn2_norm'][l] = acc2[3]
        small['conv_w'][l] = dlp[0:4]
        small['conv_b'][l] = dlp[4]
        small['gate_a_b'][l] = dlp[5].reshape(HEADS, HD)
        small['gate_x_b'][l] = dlp[6].reshape(HEADS, HD)
        small['lru_lambda'][l] = dlp[7]
        small['gate_a_w'][l] = _block_diag_extract(dwa)
        small['gate_x_w'][l] = _block_diag_extract(dwx)
        small['v_norm'][l] = dgp[0]
        small['lru_out_norm'][l] = dgp[1]
        small['gmlp_out_norm'][l] = dgp[2]
        small['spatial_w'][l] = dwm.reshape(HEADS, CHUNK, CHUNK) * tril
        small['spatial_b'][l] = dbz.reshape(CHUNK, HEADS, HD).sum(-1).T
    small = {k: jnp.stack(v) for k, v in small.items()}
    small['final_norm'] = dfin[0]
    return loss, dx, big, small, jnp.stack(dmod)


def ada_fwd(c_all, w_ada, b_loc):
    def body(c_ref, w_ref, b_ref, mod_ref, sc_ref):
        cv = c_ref[...]
        sc = cv * _sigmoid(cv)
        sc_ref[...] = sc
        mod_ref[...] = _dot3(sc, w_ref[...]) + b_ref[...]

    return pl.pallas_call(
        body, name="ada_fwd",
        grid=(DEPTH,),
        in_specs=[pl.BlockSpec((NDEV, D), lambda l: (0, 0)), pl.BlockSpec((None, D, AC), lambda l: (l, 0, 0)),
                  pl.BlockSpec((None, 1, AC), lambda l: (l, 0, 0))],
        out_specs=[pl.BlockSpec((None, NDEV, AC), lambda l: (l, 0, 0)), pl.BlockSpec((NDEV, D), lambda l: (0, 0))],
        out_shape=[jax.ShapeDtypeStruct((DEPTH, NDEV, AC), F32), jax.ShapeDtypeStruct((NDEV, D), F32)],
        compiler_params=_cparams("arbitrary"),
    )(c_all, w_ada, b_loc)


def ada_bwd(sc_t, dmod_cols):
    def body(sc_ref, dm_ref, g_ref):
        sc = sc_ref[...]
        dm = dm_ref[...]
        acc = sc[:, 0:1] * dm[0:1, :]
        for b in range(1, NDEV):
            acc = acc + sc[:, b:b + 1] * dm[b:b + 1, :]
        g_ref[...] = acc

    return pl.pallas_call(
        body, name="ada_bwd",
        grid=(DEPTH,),
        in_specs=[pl.BlockSpec((D, NDEV), lambda l: (0, 0)), pl.BlockSpec((None, NDEV, AC), lambda l: (l, 0, 0))],
        out_specs=pl.BlockSpec((None, None, D, AC), lambda l: (0, l, 0, 0)),
        out_shape=jax.ShapeDtypeStruct((1, DEPTH, D, AC), F32),
        compiler_params=_cparams("arbitrary"),
    )(sc_t, dmod_cols)


def _row_tile(rows, cols):
    if rows * cols <= 512 * 1024:
        return rows
    for tr in (512, 384, 352, 256, 128, 64, 32, 16, 8):
        if rows % tr == 0:
            return tr
    return rows


def adamw(gparts, slot0, w, m, v, name):
    P, _, R, C = gparts.shape
    L = w.shape[0]
    tr = _row_tile(R, C)

    def body(g_ref, w_ref, m_ref, v_ref, go_ref, do_ref, mo_ref, vo_ref):
        g = g_ref[0].astype(F32)
        for p in range(1, P):
            g = g + g_ref[p].astype(F32)
        go_ref[...] = g
        mn = ADAM_B1 * m_ref[...] + (1.0 - ADAM_B1) * g
        vn = ADAM_B2 * v_ref[...] + (1.0 - ADAM_B2) * (g * g)
        mo_ref[...] = mn
        vo_ref[...] = vn
        m_hat = mn / (1.0 - ADAM_B1 ** ADAM_STEP)
        v_hat = vn / (1.0 - ADAM_B2 ** ADAM_STEP)
        do_ref[...] = -ADAM_LR * (m_hat / (jnp.sqrt(v_hat) + ADAM_EPS) + ADAM_WD * w_ref[...])

    blk = pl.BlockSpec((None, tr, C), lambda l, i: (l, i, 0))
    return pl.pallas_call(
        body, name=name,
        grid=(L, R // tr),
        in_specs=[pl.BlockSpec((P, None, tr, C), lambda l, i: (0, slot0 + l, i, 0)), blk, blk, blk],
        out_specs=[blk, blk, blk, blk],
        out_shape=[jax.ShapeDtypeStruct((L, R, C), F32)] * 4,
        compiler_params=_cparams("arbitrary", "arbitrary"),
    )(gparts, w, m, v)


def sum_parts(parts):
    P, R, C = parts.shape

    def body(p_ref, o_ref):
        acc = p_ref[0]
        for p in range(1, P):
            acc = acc + p_ref[p]
        o_ref[...] = acc

    return pl.pallas_call(
        body, name="sum_parts",
        in_specs=[pl.BlockSpec(memory_space=pltpu.VMEM)],
        out_specs=pl.BlockSpec(memory_space=pltpu.VMEM),
        out_shape=jax.ShapeDtypeStruct((R, C), F32),
    )(parts)


WEIGHTS = ['w_ada', 'b_ada', 'ffn1_norm', 'ffn1_w_gu', 'ffn1_w_down', 'mix_norm', 'w_in', 'conv_w', 'conv_b',
           'gate_a_w', 'gate_a_b', 'gate_x_w', 'gate_x_b', 'lru_lambda', 'v_norm', 'spatial_w', 'spatial_b',
           'lru_out_norm', 'gmlp_out_norm', 'w_out', 'ffn2_norm', 'ffn2_w_gu', 'ffn2_w_down', 'final_norm']
PACKED = ['b_ada', 'ffn1_norm', 'mix_norm', 'conv_b', 'gate_a_w', 'gate_a_b', 'gate_x_w', 'gate_x_b', 'lru_lambda',
          'v_norm', 'spatial_w', 'spatial_b', 'lru_out_norm', 'gmlp_out_norm', 'ffn2_norm', 'final_norm', 'conv_w']
PACK_LANES = 128
PACK_ROW_ALIGN = 8 * NDEV


def _pack_rows(shapes):
    used = 0
    for k in PACKED:
        size = 1
        for s in shapes[k]:
            size *= s
        used += size // PACK_LANES
    return used, -(-(used + 1) // PACK_ROW_ALIGN) * PACK_ROW_ALIGN


def _pack(d, tail=None):
    parts = [d[k].reshape(-1, PACK_LANES).astype(F32) for k in PACKED]
    used, rows = _pack_rows({k: d[k].shape for k in PACKED})
    parts.append(jnp.zeros((1, PACK_LANES), F32) if tail is None else tail)
    return jnp.concatenate(parts + [jnp.zeros((rows - used - 1, PACK_LANES), F32)], axis=0)


def _unpack(buf, shapes):
    out, off = {}, 0
    for k in PACKED:
        size = 1
        for s in shapes[k]:
            size *= s
        nrows = size // PACK_LANES
        out[k] = buf[off:off + nrows].reshape(shapes[k])
        off += nrows
    return out


def kernel(x, c, w_ada, b_ada, ffn1_norm, ffn1_w_gu, ffn1_w_down, mix_norm, w_in, conv_w, conv_b, gate_a_w, gate_a_b, gate_x_w, gate_x_b, lru_lambda, v_norm, spatial_w, spatial_b, lru_out_norm, gmlp_out_norm, w_out, ffn2_norm, ffn2_w_gu, ffn2_w_down, final_norm, loss_target, m_w_ada, m_b_ada, m_ffn1_norm, m_ffn1_w_gu, m_ffn1_w_down, m_mix_norm, m_w_in, m_conv_w, m_conv_b, m_gate_a_w, m_gate_a_b, m_gate_x_w, m_gate_x_b, m_lru_lambda, m_v_norm, m_spatial_w, m_spatial_b, m_lru_out_norm, m_gmlp_out_norm, m_w_out, m_ffn2_norm, m_ffn2_w_gu, m_ffn2_w_down, m_final_norm, v_w_ada, v_b_ada, v_ffn1_norm, v_ffn1_w_gu, v_ffn1_w_down, v_mix_norm, v_w_in, v_conv_w, v_conv_b, v_gate_a_w, v_gate_a_b, v_gate_x_w, v_gate_x_b, v_lru_lambda, v_v_norm, v_spatial_w, v_spatial_b, v_lru_out_norm, v_gmlp_out_norm, v_w_out, v_ffn2_norm, v_ffn2_w_gu, v_ffn2_w_down, v_final_norm):
    w = dict(w_ada=w_ada, b_ada=b_ada, ffn1_norm=ffn1_norm, ffn1_w_gu=ffn1_w_gu, ffn1_w_down=ffn1_w_down, mix_norm=mix_norm, w_in=w_in, conv_w=conv_w, conv_b=conv_b, gate_a_w=gate_a_w, gate_a_b=gate_a_b, gate_x_w=gate_x_w, gate_x_b=gate_x_b, lru_lambda=lru_lambda, v_norm=v_norm, spatial_w=spatial_w, spatial_b=spatial_b, lru_out_norm=lru_out_norm, gmlp_out_norm=gmlp_out_norm, w_out=w_out, ffn2_norm=ffn2_norm, ffn2_w_gu=ffn2_w_gu, ffn2_w_down=ffn2_w_down, final_norm=final_norm)
    m = dict(w_ada=m_w_ada, b_ada=m_b_ada, ffn1_norm=m_ffn1_norm, ffn1_w_gu=m_ffn1_w_gu, ffn1_w_down=m_ffn1_w_down, mix_norm=m_mix_norm, w_in=m_w_in, conv_w=m_conv_w, conv_b=m_conv_b, gate_a_w=m_gate_a_w, gate_a_b=m_gate_a_b, gate_x_w=m_gate_x_w, gate_x_b=m_gate_x_b, lru_lambda=m_lru_lambda, v_norm=m_v_norm, spatial_w=m_spatial_w, spatial_b=m_spatial_b, lru_out_norm=m_lru_out_norm, gmlp_out_norm=m_gmlp_out_norm, w_out=m_w_out, ffn2_norm=m_ffn2_norm, ffn2_w_gu=m_ffn2_w_gu, ffn2_w_down=m_ffn2_w_down, final_norm=m_final_norm)
    v = dict(w_ada=v_w_ada, b_ada=v_b_ada, ffn1_norm=v_ffn1_norm, ffn1_w_gu=v_ffn1_w_gu, ffn1_w_down=v_ffn1_w_down, mix_norm=v_mix_norm, w_in=v_w_in, conv_w=v_conv_w, conv_b=v_conv_b, gate_a_w=v_gate_a_w, gate_a_b=v_gate_a_b, gate_x_w=v_gate_x_w, gate_x_b=v_gate_x_b, lru_lambda=v_lru_lambda, v_norm=v_v_norm, spatial_w=v_spatial_w, spatial_b=v_spatial_b, lru_out_norm=v_lru_out_norm, gmlp_out_norm=v_gmlp_out_norm, w_out=v_w_out, ffn2_norm=v_ffn2_norm, ffn2_w_gu=v_ffn2_w_gu, ffn2_w_down=v_ffn2_w_down, final_norm=v_final_norm)
    me = 4 * lax.axis_index("x") + 2 * lax.axis_index("y") + lax.axis_index("c")

    loc = dict(gu=jnp.concatenate([ffn1_w_gu, ffn2_w_gu], axis=0).astype(BF16),
               down=jnp.concatenate([ffn1_w_down, ffn2_w_down], axis=0).astype(BF16),
               w_in=w_in.astype(BF16), w_out=w_out.astype(BF16))
    c_g, conv_g, gu0, down0 = all_gather([(c, None), (conv_w, None), (loc['gu'], 0), (loc['down'], 0)], "gather_first")
    conv_w_full = conv_g.transpose(1, 2, 0, 3).reshape(DEPTH, CONV_WIDTH, LW)

    b_loc = lax.dynamic_slice(b_ada, (0, me * AC), (DEPTH, AC)).reshape(DEPTH, 1, AC)
    mod_cols, sc_all = ada_fwd(c_g.reshape(NDEV, D), w_ada, b_loc)
    (mod_rows,) = all_to_all([mod_cols.transpose(1, 0, 2)], "scatter_mod")
    mod = mod_rows.transpose(1, 0, 2).reshape(DEPTH, NMOD, D)

    small_w = {k: w[k] for k in PACKED if k != 'conv_w'}
    me_arr = jnp.reshape(me, (1,)).astype(jnp.int32)
    loss_loc, dx, big, small_g, dmod = local_fwd_bwd(me_arr, x[0], loss_target[0], mod, small_w, loc,
                                                     {('gu', 0): gu0, ('down', 0): down0}, conv_w_full)

    small_g['b_ada'] = dmod.reshape(DEPTH, NMOD * D)
    gpack = _pack(small_g, jnp.zeros((1, PACK_LANES), F32).at[0, 0].set(loss_loc))
    rows = gpack.shape[0]
    dmod_out = dmod.reshape(DEPTH, NDEV, AC).transpose(1, 0, 2)
    dmod_r, pack_r = all_to_all([dmod_out, gpack.reshape(NDEV, rows // NDEV, PACK_LANES)], "scatter_grads")
    (gsum_g,) = all_gather([(sum_parts(pack_r), None)], "gather_small_grads")
    gsum = gsum_g.reshape(1, 1, rows, PACK_LANES)
    loss = gsum[0, 0, _pack_rows({k: small_g[k].shape for k in PACKED})[0], 0]

    res = {}
    t = lambda a: a.transpose(0, 2, 1)
    gu_t = big['gu']
    res['ffn1_w_gu'] = tuple(t(r) for r in adamw(gu_t, 0, t(w['ffn1_w_gu']), t(m['ffn1_w_gu']), t(v['ffn1_w_gu']),
                                                 "adamw_gu_a"))
    res['ffn2_w_gu'] = tuple(t(r) for r in adamw(gu_t, DEPTH, t(w['ffn2_w_gu']), t(m['ffn2_w_gu']),
                                                 t(v['ffn2_w_gu']), "adamw_gu_b"))
    res['ffn1_w_down'] = adamw(big['down'], 0, w['ffn1_w_down'], m['ffn1_w_down'], v['ffn1_w_down'], "adamw_down_a")
    res['ffn2_w_down'] = adamw(big['down'], DEPTH, w['ffn2_w_down'], m['ffn2_w_down'], v['ffn2_w_down'], "adamw_down_b")
    res['w_in'] = adamw(big['w_in'], 0, w['w_in'], m['w_in'], v['w_in'], "adamw_w_in")
    res['w_out'] = adamw(big['w_out'], 0, w['w_out'], m['w_out'], v['w_out'], "adamw_w_out")
    g_ada = ada_bwd(sc_all.T, dmod_r.transpose(1, 0, 2))
    res['w_ada'] = adamw(g_ada, 0, w['w_ada'], m['w_ada'], v['w_ada'], "adamw_w_ada")
    shapes = {k: w[k].shape for k in PACKED}
    shapes['conv_w'] = (DEPTH, CONV_WIDTH, LW)
    dummy = jnp.zeros(shapes['conv_w'], F32)
    packs = adamw(gsum, 0, _pack({**small_w, 'conv_w': dummy})[None], _pack({**{k: m[k] for k in small_w}, 'conv_w': dummy})[None],
                  _pack({**{k: v[k] for k in small_w}, 'conv_w': dummy})[None], "adamw_small")
    unpacked = [_unpack(b[0], shapes) for b in packs]
    for k in small_w:
        res[k] = tuple(u[k] for u in unpacked)
    gconv = lax.dynamic_slice(unpacked[0]['conv_w'], (0, 0, me * (LW // NDEV)), (DEPTH, CONV_WIDTH, LW // NDEV))
    cshape = (1, DEPTH * CONV_WIDTH, LW // NDEV)
    rc = adamw(gconv.reshape((1,) + cshape), 0, conv_w.reshape(cshape), m['conv_w'].reshape(cshape),
               v['conv_w'].reshape(cshape), "adamw_conv_w")
    res['conv_w'] = tuple(r.reshape(conv_w.shape) for r in rc)

    return (loss, dx[None], *[res[k][0] for k in WEIGHTS], *[res[k][1] for k in WEIGHTS],
            *[res[k][2] for k in WEIGHTS], *[res[k][3] for k in WEIGHTS])
```

```python
import jax
import jax.numpy as jnp
from jax import lax
from jax.experimental import pallas as pl
from jax.experimental.pallas import tpu as pltpu

F32 = jnp.float32
BF16 = jnp.bfloat16

NDEV = 8
DEPTH = 2
D = 1024
DFF = 2816
FC = 2 * DFF // NDEV
NCHUNK = DFF // FC
DR = DFF // NDEV
LW = 512
GW = 512
HD = 64
HEADS = 8
CHUNK = 128
PC = 2 * (LW + GW) // NDEV
OR = D // NDEV
NMOD = 9
AC = NMOD * D // NDEV
LC = 128
EPS = 1e-6
RG_LRU_C = 8.0
CONV_WIDTH = 4

ADAM_LR = 0.001
ADAM_B1 = 0.9
ADAM_B2 = 0.999
ADAM_EPS = 1e-08
ADAM_WD = 0.01
ADAM_STEP = 10

VMEM_LIMIT_BYTES = 60 * 1024 * 1024
MESH = pl.DeviceIdType.MESH
ANY = pl.BlockSpec(memory_space=pl.ANY)


def _cparams(*sem):
    return pltpu.CompilerParams(dimension_semantics=tuple(sem) if sem else None,
                                vmem_limit_bytes=VMEM_LIMIT_BYTES)


def _dot(a, b):
    return jnp.dot(a, b, preferred_element_type=F32)


def _dot_nt(a, b):
    return lax.dot_general(a, b, (((1,), (1,)), ((), ())), preferred_element_type=F32)


def _dot_tn(a, b):
    return lax.dot_general(a, b, (((0,), (0,)), ((), ())), preferred_element_type=F32)


def _split(a):
    hi = a.astype(BF16)
    lo = (a - hi.astype(F32)).astype(BF16)
    return hi, lo


def _dot3(a, b):
    ah, al = _split(a)
    bh, bl = _split(b)
    return _dot(ah, bh) + (_dot(ah, bl) + _dot(al, bh))


def _csum(a):
    return jnp.sum(a, axis=0, keepdims=True)


def _rmean(a):
    return jnp.mean(a, axis=-1, keepdims=True)


def _sigmoid(a):
    return 1.0 / (1.0 + jnp.exp(-a))


_GELU_K = 0.7978845608028654
_GELU_C = 0.044715


def _gelu(a):
    return 0.5 * a * (1.0 + jnp.tanh(_GELU_K * (a + _GELU_C * a * a * a)))


def _gelu_and_grad(a):
    a2 = a * a
    t = jnp.tanh(_GELU_K * (a + _GELU_C * a2 * a))
    half = 0.5 * (1.0 + t)
    return a * half, half + 0.5 * a * (1.0 - t * t) * (_GELU_K * (1.0 + 3.0 * _GELU_C * a2))


def _norm_mod(x, gain, scale, shift):
    rstd = lax.rsqrt(_rmean(x * x) + EPS)
    return (x * rstd * gain) * (1.0 + scale) + shift


def _norm_mod_bwd(dh, x, gain, scale):
    rstd = lax.rsqrt(_rmean(x * x) + EPS)
    xhat = x * rstd
    dshift = _csum(dh)
    dscale = _csum(dh * (xhat * gain))
    dhn = dh * (1.0 + scale)
    dgain = _csum(dhn * xhat)
    dxhat = dhn * gain
    dx = rstd * (dxhat - xhat * _rmean(dxhat * xhat))
    return dx, dshift, dscale, dgain


def _rms(x, gain):
    rstd = lax.rsqrt(_rmean(x * x) + EPS)
    return x * rstd * gain


def _rms_bwd(dy, x, gain):
    rstd = lax.rsqrt(_rmean(x * x) + EPS)
    xhat = x * rstd
    dgain = _csum(dy * xhat)
    dxhat = dy * gain
    return rstd * (dxhat - xhat * _rmean(dxhat * xhat)), dgain


PAIR = 2 * HD


def _seg_mean(a, pavg):
    hi, lo = _split(a)
    return jnp.concatenate([_dot(hi[:, p:p + PAIR], pavg) + _dot(lo[:, p:p + PAIR], pavg)
                            for p in range(0, a.shape[1], PAIR)], axis=1)


def _block_copies(src_hbm, dst_vmem, sems, rows):
    copies = []
    for k in range(NDEV):
        dst = dst_vmem.at[k] if rows is None else dst_vmem.at[pl.ds(k * rows, rows)]
        copies.append(pltpu.make_async_copy(src_hbm.at[k], dst, sems.at[k]))
    return copies


def _ffn_weight_fetch(wgu_hbm, wd_hbm, wgu_v, wd_v, sems):
    @pl.when(pl.program_id(0) == 0)
    def _():
        copies = _block_copies(wgu_hbm, wgu_v, sems.at[0], None) + _block_copies(wd_hbm, wd_v, sems.at[1], DR)
        for cp in copies:
            cp.start()
        for cp in copies:
            cp.wait()


def _place():
    return lax.axis_index("x"), lax.axis_index("y"), lax.axis_index("c")


def _slot(p):
    return 4 * p[0] + 2 * p[1] + p[2]


class GatherRide:
    def __init__(self, srcs):
        self.n = len(srcs)
        self.index = [i for _, i in srcs]
        self.args = [a for a, _ in srcs]
        self.out_shape = [jax.ShapeDtypeStruct((NDEV,) + (a.shape if i is None else a.shape[1:]), a.dtype)
                          for a, i in srcs]
        self.scratch = [pltpu.SemaphoreType.DMA((self.n, NDEV - 1)), pltpu.SemaphoreType.DMA((self.n, NDEV - 1)),
                        pltpu.SemaphoreType.DMA((self.n,))]

    def hooks(self, ins, outs, sems):
        send_sems, recv_sems, local_sems = sems
        n = self.n
        x, y, c = _place()
        me, sibling = (x, y, c), (x, y, 1 - c)
        chips = [(1 - x, y), (x, 1 - y), (1 - x, 1 - y)]

        def local(a):
            return ins[a] if self.index[a] is None else ins[a].at[self.index[a]]

        def copy(a, k, block, to, src=None):
            dst = outs[a].at[_slot(block)]
            return pltpu.make_async_remote_copy(
                src_ref=dst if src is None else src, dst_ref=dst,
                send_sem=send_sems.at[a, k], recv_sem=recv_sems.at[a, k],
                device_id=to, device_id_type=MESH)

        def mine():
            return [pltpu.make_async_copy(local(a), outs[a].at[_slot(me)], local_sems.at[a]) for a in range(n)]

        def first():
            cps = []
            for a in range(n):
                cps.append(copy(a, 0, me, sibling, src=local(a)))
                cps += [copy(a, 1 + j, me, (*chip, c), src=local(a)) for j, chip in enumerate(chips)]
            return cps

        def passed():
            return [copy(a, 4 + j, (*chip, c), sibling) for j, chip in enumerate(chips) for a in range(n)]

        def start():
            for cp in mine() + first():
                cp.start()

        def mid():
            for j, chip in enumerate(chips):
                for a in range(n):
                    copy(a, 1 + j, (*chip, c), me).wait_recv()
                    copy(a, 4 + j, (*chip, c), sibling).start()

        def finish():
            for a in range(n):
                copy(a, 0, sibling, me).wait_recv()
                for j, chip in enumerate(chips):
                    copy(a, 4 + j, (*chip, 1 - c), me).wait_recv()
            for cp in first() + passed():
                cp.wait_send()
            for cp in mine():
                cp.wait()

        return start, mid, finish


def all_gather(srcs, name):
    ride = GatherRide(srcs)
    n = ride.n

    def body(*refs):
        start, mid, finish = ride.hooks(refs[:n], refs[n:2 * n], refs[2 * n:])
        start()
        mid()
        finish()

    return pl.pallas_call(
        body, name=name,
        in_specs=[ANY] * n, out_specs=[ANY] * n, out_shape=ride.out_shape, scratch_shapes=ride.scratch,
    )(*ride.args)


def _call(core, ride, *, name, grid, in_specs, out_specs, out_shape, scratch_shapes, args):
    if ride is None:
        outs = pl.pallas_call(core, name=name, grid=grid, in_specs=in_specs, out_specs=out_specs,
                              out_shape=out_shape, scratch_shapes=scratch_shapes,
                              compiler_params=_cparams("arbitrary"))(*args)
        return outs, []
    n_in, n_out, n_sc, n = len(in_specs), len(out_shape), len(scratch_shapes), ride.n
    nsteps = grid[0]
    mid_step = max(nsteps - 2, 0)

    def body(*refs):
        cuts = [n_in, n_in + n, n_in + n + n_out, n_in + 2 * n + n_out, n_in + 2 * n + n_out + n_sc]
        ci, ri, co, ro, cs, rs = (refs[a:b] for a, b in zip([0] + cuts, cuts + [len(refs)]))
        start, mid, finish = ride.hooks(ri, ro, rs)
        i = pl.program_id(0)
        pl.when(i == 0)(start)
        core(*ci, *co, *cs)
        pl.when(i == mid_step)(mid)
        pl.when(i == nsteps - 1)(finish)

    outs = pl.pallas_call(
        body, name=name, grid=grid,
        in_specs=list(in_specs) + [ANY] * n, out_specs=list(out_specs) + [ANY] * n,
        out_shape=list(out_shape) + ride.out_shape, scratch_shapes=list(scratch_shapes) + ride.scratch,
        compiler_params=_cparams("arbitrary"))(*args, *ride.args)
    return outs[:n_out], outs[n_out:]


def all_to_all(arrs, name):
    n = len(arrs)

    def body(*refs):
        ins, outs = refs[:n], refs[n:2 * n]
        send_sems, recv_sems, local_sems = refs[2 * n:]
        x, y, c = _place()
        me = (x, y, c)

        def peer(k):
            return (1 - x if k & 4 else x, 1 - y if k & 2 else y, 1 - c if k & 1 else c)

        def copy(a, k):
            return pltpu.make_async_remote_copy(
                src_ref=ins[a].at[_slot(peer(k))], dst_ref=outs[a].at[_slot(me)],
                send_sem=send_sems.at[a, k - 1], recv_sem=recv_sems.at[a, k - 1],
                device_id=peer(k), device_id_type=MESH)

        def landing(a, k):
            return pltpu.make_async_remote_copy(
                src_ref=outs[a].at[_slot(peer(k))], dst_ref=outs[a].at[_slot(peer(k))],
                send_sem=send_sems.at[a, k - 1], recv_sem=recv_sems.at[a, k - 1],
                device_id=me, device_id_type=MESH)

        mine = [pltpu.make_async_copy(ins[a].at[_slot(me)], outs[a].at[_slot(me)], local_sems.at[a]) for a in range(n)]
        for cp in mine:
            cp.start()
        sends = [copy(a, k) for a in range(n) for k in range(1, NDEV)]
        for cp in sends:
            cp.start()
        for a in range(n):
            for k in range(1, NDEV):
                landing(a, k).wait_recv()
        for cp in sends:
            cp.wait_send()
        for cp in mine:
            cp.wait()

    return pl.pallas_call(
        body, name=name,
        in_specs=[ANY] * n, out_specs=[ANY] * n,
        out_shape=[jax.ShapeDtypeStruct(a.shape, a.dtype) for a in arrs],
        scratch_shapes=[pltpu.SemaphoreType.DMA((n, NDEV - 1)), pltpu.SemaphoreType.DMA((n, NDEV - 1)),
                        pltpu.SemaphoreType.DMA((n,))],
    )(*arrs)


FFN_TS = 256
FFN_FWD_TS = 512


def ffn_fwd(x, vec, wgu_g, wdown_g, tag, ride=None):
    S = x.shape[0]
    ts = min(FFN_FWD_TS, S)

    def body(x_ref, vec_ref, wgu_hbm, wd_hbm, xo_ref, h_ref, gu_ref, f_ref, wgu_v, wd_v, sems):
        _ffn_weight_fetch(wgu_hbm, wd_hbm, wgu_v, wd_v, sems)
        xv = x_ref[...]
        h = _norm_mod(xv, vec_ref[3:4, :], vec_ref[1:2, :], vec_ref[0:1, :]).astype(BF16)
        h_ref[...] = h
        acc = jnp.zeros((ts, D), F32)
        for j in range(NCHUNK):
            g = _dot(h, wgu_v[j])
            u = _dot(h, wgu_v[NCHUNK + j])
            gu_ref[j] = g.astype(BF16)
            gu_ref[NCHUNK + j] = u.astype(BF16)
            a = (g * _sigmoid(g) * u).astype(BF16)
            acc = acc + _dot(a, wd_v[pl.ds(j * FC, FC), :])
        f_ref[...] = acc.astype(BF16)
        xo_ref[...] = xv + (0.5 * vec_ref[2:3, :]) * acc

    return _call(
        body, ride, name=f"ffn_fwd_{tag}",
        grid=(S // ts,),
        in_specs=[pl.BlockSpec((ts, D), lambda i: (i, 0)),
                  pl.BlockSpec((8, D), lambda i: (0, 0)), ANY, ANY],
        out_specs=[pl.BlockSpec((ts, D), lambda i: (i, 0)),
                   pl.BlockSpec((ts, D), lambda i: (i, 0)),
                   pl.BlockSpec((NDEV, ts, FC), lambda i: (0, i, 0)),
                   pl.BlockSpec((ts, D), lambda i: (i, 0))],
        out_shape=[jax.ShapeDtypeStruct((S, D), F32), jax.ShapeDtypeStruct((S, D), BF16),
                   jax.ShapeDtypeStruct((NDEV, S, FC), BF16), jax.ShapeDtypeStruct((S, D), BF16)],
        scratch_shapes=[pltpu.VMEM((NDEV, D, FC), BF16), pltpu.VMEM((DFF, D), BF16),
                        pltpu.SemaphoreType.DMA((2, NDEV))],
        args=(x, vec, wgu_g, wdown_g))


def ffn_bwd(dxo, x, gu, f, vec, wgu_g, wdown_g, tag):
    S = x.shape[0]
    ts = min(FFN_TS, S)

    def body(dxo_ref, x_ref, gu_ref, f_ref, vec_ref, wgu_hbm, wd_hbm,
             dx_ref, dgu_ref, a_ref, df_ref, acc_ref, wgu_v, wd_v, sems):
        _ffn_weight_fetch(wgu_hbm, wd_hbm, wgu_v, wd_v, sems)

        @pl.when(pl.program_id(0) == 0)
        def _():
            acc_ref[...] = jnp.zeros_like(acc_ref)

        dxo_v = dxo_ref[...]
        dgate = 0.5 * _csum(dxo_v * f_ref[...].astype(F32))
        df = ((0.5 * vec_ref[2:3, :]) * dxo_v).astype(BF16)
        df_ref[...] = df
        dh = jnp.zeros((ts, D), F32)
        for j in range(NCHUNK):
            da = _dot_nt(df, wd_v[pl.ds(j * FC, FC), :])
            g = gu_ref[j].astype(F32)
            u = gu_ref[NCHUNK + j].astype(F32)
            sg = _sigmoid(g)
            si = g * sg
            a_ref[j] = (si * u).astype(BF16)
            dg = (da * u * (sg * (1.0 + g * (1.0 - sg)))).astype(BF16)
            du = (da * si).astype(BF16)
            dgu_ref[j] = dg
            dgu_ref[NCHUNK + j] = du
            dh = dh + _dot_nt(dg, wgu_v[j]) + _dot_nt(du, wgu_v[NCHUNK + j])
        dx, dshift, dscale, dgain = _norm_mod_bwd(dh, x_ref[...], vec_ref[3:4, :], vec_ref[1:2, :])
        dx_ref[...] = dx + dxo_v
        acc_ref[0:1, :] += dshift
        acc_ref[1:2, :] += dscale
        acc_ref[2:3, :] += dgate
        acc_ref[3:4, :] += dgain

    row = pl.BlockSpec((ts, D), lambda i: (i, 0))
    return pl.pallas_call(
        body, name=f"ffn_bwd_{tag}",
        grid=(S // ts,),
        in_specs=[row, row, pl.BlockSpec((NDEV, ts, FC), lambda i: (0, i, 0)), row,
                  pl.BlockSpec((8, D), lambda i: (0, 0)), ANY, ANY],
        out_specs=[row, pl.BlockSpec((NDEV, ts, FC), lambda i: (0, i, 0)),
                   pl.BlockSpec((NCHUNK, ts, FC), lambda i: (0, i, 0)), row,
                   pl.BlockSpec((8, D), lambda i: (0, 0))],
        out_shape=[jax.ShapeDtypeStruct((S, D), F32), jax.ShapeDtypeStruct((NDEV, S, FC), BF16),
                   jax.ShapeDtypeStruct((NCHUNK, S, FC), BF16), jax.ShapeDtypeStruct((S, D), BF16),
                   jax.ShapeDtypeStruct((8, D), F32)],
        scratch_shapes=[pltpu.VMEM((NDEV, D, FC), BF16), pltpu.VMEM((DFF, D), BF16),
                        pltpu.SemaphoreType.DMA((2, NDEV))],
        compiler_params=_cparams("arbitrary"),
    )(dxo, x, gu, f, vec, wgu_g, wdown_g)


NCHIP = NDEV // 2


def tn_matmul_scatter(me_arr, a, b, slot, nslots, prev, name, split=1):
    na, S, M = a.shape
    nb, _, N = b.shape
    ncall = NDEV // split
    ts = min(4096, S)
    nsteps = S // ts
    mp = M // split
    other_step = {1: lambda j: 2 * j, 2: lambda j: j, 8: lambda j: 0}[split]
    mine_step = {1: lambda j: 2 * j + 1, 2: lambda j: j, 8: lambda j: 0}[split]

    def group(k, me_ref):
        if split == 1:
            return jnp.bitwise_xor(me_ref[0], NDEV - 1 - k)
        if split == 2:
            return jnp.bitwise_xor(me_ref[0] // 2, NCHIP - 1 - k)
        return 0

    def body(me_ref, *refs):
        a_ref, b_ref = refs[0], refs[1]
        recv_ref, acc, sb_other, sb_mine, land, d2d_send, d2d_recv, ici_send, ici_recv = refs[-9:]
        k = pl.program_id(0)
        s = pl.program_id(1)
        x, y, c = _place()
        my_chip = 2 * x + y

        def chip_of(j):
            if split == 8:
                cx, cy = j // 2, j % 2
            else:
                flip = NCHIP - 1 - j
                cx, cy = (1 - x if flip & 2 else x), (1 - y if flip & 1 else y)
            return cx, cy, 2 * cx + cy

        def piece(j, core):
            if split == 1:
                return acc[...]
            start = core * mp if split == 2 else (2 * j + core) * mp
            return acc[pl.ds(pl.multiple_of(start, 8), mp), :]

        def to_sibling(j):
            return pltpu.make_async_remote_copy(
                src_ref=sb_other.at[j], dst_ref=land.at[j], send_sem=d2d_send.at[j], recv_sem=d2d_recv.at[j],
                device_id=(x, y, 1 - c), device_id_type=MESH)

        def to_owner(j):
            cx, cy, ci = chip_of(j)
            dst = recv_ref.at[my_chip, slot]
            return ci, pltpu.make_async_copy(sb_mine.at[j], dst, ici_send.at[j]), pltpu.make_async_remote_copy(
                src_ref=sb_mine.at[j], dst_ref=dst, send_sem=ici_send.at[j], recv_sem=ici_recv.at[my_chip],
                device_id=(cx, cy, c), device_id_type=MESH)

        if nsteps == 1:
            acc[...] = _dot_tn(a_ref[...], b_ref[...])
        else:
            @pl.when(s == 0)
            def _():
                acc[...] = jnp.zeros_like(acc)

            acc[...] += _dot_tn(a_ref[...], b_ref[...])

        for kk in range(ncall):
            @pl.when((s == nsteps - 1) & (k == kk))
            def _():
                for j in range(NCHIP):
                    if other_step(j) == kk:
                        sb_other[j] = piece(j, 1 - c).astype(BF16)
                        to_sibling(j).start()
                for j in range(NCHIP):
                    if mine_step(j) == kk:
                        to_sibling(j).wait_recv()
                        sb_mine[j] = (piece(j, c) + land[j].astype(F32)).astype(BF16)
                        ci, loc, rem = to_owner(j)
                        pl.when(ci == my_chip)(loc.start)
                        pl.when(ci != my_chip)(rem.start)

        @pl.when((s == nsteps - 1) & (k == ncall - 1))
        def _():
            for j in range(NCHIP):
                to_sibling(j).wait_send()
                ci, loc, rem = to_owner(j)
                pl.when(ci == my_chip)(loc.wait)
                pl.when(ci != my_chip)(rem.wait_send)
            for src in range(NCHIP):
                @pl.when(my_chip != src)
                def _():
                    pltpu.make_async_remote_copy(
                        src_ref=recv_ref.at[src, slot], dst_ref=recv_ref.at[src, slot],
                        send_sem=ici_send.at[src], recv_sem=ici_recv.at[src],
                        device_id=(src // 2, src % 2, c), device_id_type=MESH).wait_recv()

    in_specs = [pl.BlockSpec((None, ts, M), (lambda k, s, me: (group(k, me), s, 0)) if na > 1 else (lambda k, s, me: (0, s, 0))),
                pl.BlockSpec((None, ts, N), (lambda k, s, me: (group(k, me), s, 0)) if nb > 1 else (lambda k, s, me: (0, s, 0)))]
    args = [me_arr, a, b]
    aliases = {}
    if prev is not None:
        in_specs.append(ANY)
        args.append(prev)
        aliases = {3: 0}
    return pl.pallas_call(
        body, name=name,
        grid_spec=pltpu.PrefetchScalarGridSpec(
            num_scalar_prefetch=1, grid=(ncall, nsteps), in_specs=in_specs, out_specs=ANY,
            scratch_shapes=[pltpu.VMEM((M, N), F32), pltpu.VMEM((NCHIP, mp, N), BF16), pltpu.VMEM((NCHIP, mp, N), BF16),
                            pltpu.VMEM((NCHIP, mp, N), BF16), pltpu.SemaphoreType.DMA((NCHIP,)),
                            pltpu.SemaphoreType.DMA((NCHIP,)), pltpu.SemaphoreType.DMA((NCHIP,)),
                            pltpu.SemaphoreType.DMA((NCHIP,))]),
        out_shape=jax.ShapeDtypeStruct((NCHIP, nslots, mp, N), BF16),
        input_output_aliases=aliases,
        compiler_params=_cparams("arbitrary", "arbitrary"),
    )(*args)


MIX_TS = 256
MIX_IN_TS = 512


def mix_in_fwd(x, vec, win_g, tag, ride=None):
    S = x.shape[0]
    ts = min(MIX_IN_TS, S)

    def body(x_ref, vec_ref, win_ref, hm_ref, proj_ref):
        h = _norm_mod(x_ref[...], vec_ref[3:4, :], vec_ref[1:2, :], vec_ref[0:1, :]).astype(BF16)
        hm_ref[...] = h
        for k in range(NDEV):
            proj_ref[k] = _dot(h, win_ref[k])

    return _call(
        body, ride, name=f"mix_in_fwd_{tag}",
        grid=(S // ts,),
        in_specs=[pl.BlockSpec((ts, D), lambda i: (i, 0)), pl.BlockSpec((8, D), lambda i: (0, 0)),
                  pl.BlockSpec((NDEV, D, PC), lambda i: (0, 0, 0))],
        out_specs=[pl.BlockSpec((ts, D), lambda i: (i, 0)),
                   pl.BlockSpec((NDEV, ts, PC), lambda i: (0, i, 0))],
        out_shape=[jax.ShapeDtypeStruct((S, D), BF16), jax.ShapeDtypeStruct((NDEV, S, PC), F32)],
        scratch_shapes=[], args=(x, vec, win_g))


def mix_in_bwd(dproj, x, dxo, vec, win_t, tag):
    S = x.shape[0]
    ts = min(MIX_IN_TS, S)

    def body(dp_ref, x_ref, dxo_ref, vec_ref, win_ref, dx_ref, acc_ref):
        @pl.when(pl.program_id(0) == 0)
        def _():
            acc_ref[...] = jnp.zeros_like(acc_ref)

        dh = jnp.zeros((ts, D), F32)
        for k in range(NDEV):
            dh = dh + _dot(dp_ref[k], win_ref[k])
        dx, dshift, dscale, dgain = _norm_mod_bwd(dh, x_ref[...], vec_ref[3:4, :], vec_ref[1:2, :])
        dx_ref[...] = dx + dxo_ref[...]
        acc_ref[0:1, :] += dshift
        acc_ref[1:2, :] += dscale
        acc_ref[3:4, :] += dgain

    row = pl.BlockSpec((ts, D), lambda i: (i, 0))
    return pl.pallas_call(
        body, name=f"mix_in_bwd_{tag}",
        grid=(S // ts,),
        in_specs=[pl.BlockSpec((NDEV, ts, PC), lambda i: (0, i, 0)), row, row,
                  pl.BlockSpec((8, D), lambda i: (0, 0)),
                  pl.BlockSpec((NDEV, PC, D), lambda i: (0, 0, 0))],
        out_specs=[row, pl.BlockSpec((8, D), lambda i: (0, 0))],
        out_shape=[jax.ShapeDtypeStruct((S, D), F32), jax.ShapeDtypeStruct((8, D), F32)],
        compiler_params=_cparams("arbitrary"),
    )(dproj, x, dxo, vec, win_t)


SCAN_UNROLL = 4


def _shift_down(z, k, row):
    return jnp.where(row >= k, pltpu.roll(z, k, 0), 0.0)


def _shift_up(z, k, row, n):
    return jnp.where(row < n - k, pltpu.roll(z, n - k, 0), 0.0)


def _lru_gates(xc, lp_ref, wa_ref, wx_ref):
    xcb = xc.astype(BF16)
    ra = _sigmoid(_dot(xcb, wa_ref[...]) + lp_ref[5:6, :])
    ix = _sigmoid(_dot(xcb, wx_ref[...]) + lp_ref[6:7, :])
    lam = lp_ref[7:8, :]
    ls = jnp.minimum(lam, 0.0) - jnp.log(1.0 + jnp.exp(-jnp.abs(lam)))
    log_a = (RG_LRU_C * ls) * ra
    a = jnp.exp(log_a)
    mult = jnp.sqrt(-jnp.tanh(log_a) * (a * a + 1.0))
    return ra, ix, ls, a, mult


def _conv(x, lp_ref, row):
    return (lp_ref[4:5, :] + lp_ref[3:4, :] * x + lp_ref[2:3, :] * _shift_down(x, 1, row)
            + lp_ref[1:2, :] * _shift_down(x, 2, row) + lp_ref[0:1, :] * _shift_down(x, 3, row))


def lru_fwd(proj, lp, wa_t, wx_t, tag, ride=None):
    S = proj.shape[1]
    nblk = S // 8

    def body(x_ref, g_ref, lp_ref, wa_ref, wx_ref, y_ref, xc_ref, h_ref, a_s, b_s):
        x = x_ref[...]
        row = lax.broadcasted_iota(jnp.int32, x.shape, 0)
        xc = _conv(x, lp_ref, row)
        xc_ref[...] = xc
        ra, ix, ls, a, mult = _lru_gates(xc, lp_ref, wa_ref, wx_ref)
        a_s[...] = a
        b_s[...] = mult * (ix * xc)
        rowb = lax.broadcasted_iota(jnp.int32, (8, LC), 0)

        def step(i, carry):
            for q in range(SCAN_UNROLL):
                r0 = pl.multiple_of((i * SCAN_UNROLL + q) * 8, 8)
                A = a_s[pl.ds(r0, 8), :]
                B = b_s[pl.ds(r0, 8), :]
                for d in (1, 2, 4):
                    m = rowb >= d
                    As = jnp.where(m, pltpu.roll(A, d, 0), 1.0)
                    Bs = jnp.where(m, pltpu.roll(B, d, 0), 0.0)
                    B = A * Bs + B
                    A = A * As
                H = B + A * carry
                h_ref[pl.ds(r0, 8), :] = H
                carry = H[7:8, :]
            return carry

        lax.fori_loop(0, nblk // SCAN_UNROLL, step, jnp.zeros((1, LC), F32))
        y_ref[...] = h_ref[...] * _gelu(g_ref[...])

    col = pl.BlockSpec((S, LC), lambda c: (0, c))
    return _call(
        body, ride, name=f"lru_fwd_{tag}",
        grid=(LW // LC,),
        in_specs=[pl.BlockSpec((None, S, LC), lambda c: (c // 2, 0, c % 2)),
                  pl.BlockSpec((None, S, LC), lambda c: (2 + c // 2, 0, c % 2)),
                  pl.BlockSpec((8, LC), lambda c: (0, c)),
                  pl.BlockSpec((None, LC, LC), lambda c: (c, 0, 0)),
                  pl.BlockSpec((None, LC, LC), lambda c: (c, 0, 0))],
        out_specs=[col, col, col],
        out_shape=[jax.ShapeDtypeStruct((S, LW), F32)] * 3,
        scratch_shapes=[pltpu.VMEM((S, LC), F32), pltpu.VMEM((S, LC), F32)],
        args=(proj, proj, lp, wa_t, wx_t))


def lru_bwd(dy, proj, xc_all, hst, lp, wa_t, wx_t, tag):
    S = proj.shape[1]
    nblk = S // 8

    def body(dy_ref, x_ref, g_ref, xc_ref, h_ref, lp_ref, wa_ref, wx_ref,
             dx_ref, dg_ref, dlp_ref, dwa_ref, dwx_ref, c_s, l_s):
        xc = xc_ref[...]
        row = lax.broadcasted_iota(jnp.int32, xc.shape, 0)
        ra, ix, ls, a, mult = _lru_gates(xc, lp_ref, wa_ref, wx_ref)
        g = g_ref[...]
        dyv = dy_ref[...]
        h = h_ref[...]
        gelu_g, gelu_grad_g = _gelu_and_grad(g)
        dg_ref[...] = (dyv * h * gelu_grad_g).astype(BF16)
        c_s[...] = _shift_up(a, 1, row, S)
        l_s[...] = dyv * gelu_g
        rowb = lax.broadcasted_iota(jnp.int32, (8, LC), 0)

        def step(i, carry):
            for q in range(SCAN_UNROLL):
                r0 = pl.multiple_of((nblk - 1 - (i * SCAN_UNROLL + q)) * 8, 8)
                C = c_s[pl.ds(r0, 8), :]
                L = l_s[pl.ds(r0, 8), :]
                for d in (1, 2, 4):
                    m = rowb < 8 - d
                    Cs = jnp.where(m, pltpu.roll(C, 8 - d, 0), 1.0)
                    Ls = jnp.where(m, pltpu.roll(L, 8 - d, 0), 0.0)
                    L = C * Ls + L
                    C = C * Cs
                L = L + C * carry
                l_s[pl.ds(r0, 8), :] = L
                carry = L[0:1, :]
            return carry

        lax.fori_loop(0, nblk // SCAN_UNROLL, step, jnp.zeros((1, LC), F32))
        db = l_s[...]
        da = db * _shift_down(h, 1, row)
        ixc = ix * xc
        dmult = db * ixc
        dix = db * (mult * xc)
        dxc = db * (mult * ix)
        dlog_a = da * a - dmult * (a * a) / mult
        dra = dlog_a * (RG_LRU_C * ls)
        dls = _csum(dlog_a * ra) * RG_LRU_C
        lam = lp_ref[7:8, :]
        dlam = dls * _sigmoid(-lam)
        dpa = dra * ra * (1.0 - ra)
        dpx = dix * ix * (1.0 - ix)
        dpab = dpa.astype(BF16)
        dpxb = dpx.astype(BF16)
        xcb = xc.astype(BF16)
        dwa_ref[...] = _dot_tn(xcb, dpab)
        dwx_ref[...] = _dot_tn(xcb, dpxb)
        dxc = dxc + _dot_nt(dpab, wa_ref[...]) + _dot_nt(dpxb, wx_ref[...])
        x = x_ref[...]
        dlp_ref[0:1, :] = _csum(dxc * _shift_down(x, 3, row))
        dlp_ref[1:2, :] = _csum(dxc * _shift_down(x, 2, row))
        dlp_ref[2:3, :] = _csum(dxc * _shift_down(x, 1, row))
        dlp_ref[3:4, :] = _csum(dxc * x)
        dlp_ref[4:5, :] = _csum(dxc)
        dlp_ref[5:6, :] = _csum(dpa)
        dlp_ref[6:7, :] = _csum(dpx)
        dlp_ref[7:8, :] = dlam
        dx = (lp_ref[3:4, :] * dxc + lp_ref[2:3, :] * _shift_up(dxc, 1, row, S)
              + lp_ref[1:2, :] * _shift_up(dxc, 2, row, S) + lp_ref[0:1, :] * _shift_up(dxc, 3, row, S))
        dx_ref[...] = dx.astype(BF16)

    col = pl.BlockSpec((S, LC), lambda c: (0, c))
    pcol = pl.BlockSpec((None, S, LC), lambda c: (c // 2, 0, c % 2))
    return pl.pallas_call(
        body, name=f"lru_bwd_{tag}",
        grid=(LW // LC,),
        in_specs=[col, pcol, pl.BlockSpec((None, S, LC), lambda c: (2 + c // 2, 0, c % 2)), col, col,
                  pl.BlockSpec((8, LC), lambda c: (0, c)),
                  pl.BlockSpec((None, LC, LC), lambda c: (c, 0, 0)),
                  pl.BlockSpec((None, LC, LC), lambda c: (c, 0, 0))],
        out_specs=[pcol, pcol, pl.BlockSpec((8, LC), lambda c: (0, c)),
                   pl.BlockSpec((None, LC, LC), lambda c: (c, 0, 0)),
                   pl.BlockSpec((None, LC, LC), lambda c: (c, 0, 0))],
        out_shape=[jax.ShapeDtypeStruct((2, S, PC), BF16), jax.ShapeDtypeStruct((2, S, PC), BF16),
                   jax.ShapeDtypeStruct((8, LW), F32),
                   jax.ShapeDtypeStruct((LW // LC, LC, LC), F32), jax.ShapeDtypeStruct((LW // LC, LC, LC), F32)],
        scratch_shapes=[pltpu.VMEM((S, LC), F32), pltpu.VMEM((S, LC), F32)],
        compiler_params=_cparams("arbitrary"),
    )(dy, proj, proj, xc_all, hst, lp, wa_t, wx_t)


def _pair_stack(zp, low):
    return jnp.concatenate([jnp.where(low, zp, 0.0), jnp.where(low, 0.0, zp)], axis=0).astype(BF16)


def _spatial(w_ref, zc, low):
    return jnp.concatenate(
        [_dot(w_ref[:, 2 * p * CHUNK:2 * (p + 1) * CHUNK], _pair_stack(zc[:, p * PAIR:(p + 1) * PAIR], low))
         for p in range(GW // PAIR)], axis=1)


def _gmlp_fwd_parts(u, v, gp_ref, wcat_ref, bz_ref, pavg_ref, ts, with_grad=False):
    if with_grad:
        ug, ugrad = _gelu_and_grad(u)
        vg, vgrad = _gelu_and_grad(v)
    else:
        ug, vg, ugrad, vgrad = _gelu(u), _gelu(v), None, None
    pavg = pavg_ref[...]
    vc = vg - _seg_mean(vg, pavg)
    rs = lax.rsqrt(_seg_mean(vc * vc, pavg) + EPS)
    vhat = vc * rs
    vh = vhat * gp_ref[0:1, :]
    low = lax.broadcasted_iota(jnp.int32, (CHUNK, PAIR), 1) < HD
    zs = [_spatial(wcat_ref, vh[n * CHUNK:(n + 1) * CHUNK, :], low) + bz_ref[...] for n in range(ts // CHUNK)]
    z = jnp.concatenate(zs, axis=0) if len(zs) > 1 else zs[0]
    return ug, rs, vhat, vh, z, ugrad, vgrad


def mix_out_fwd(proj, ylru, x, vec, gp, wcat, bz, pavg, wout_g, tag, ride=None):
    S = x.shape[0]
    ts = min(MIX_TS, S)

    def body(u_ref, v_ref, yl_ref, x_ref, vec_ref, gp_ref, wcat_ref, bz_ref, pavg_ref, wout_ref,
             xo_ref, y_ref, fo_ref):
        u = jnp.concatenate([u_ref[0], u_ref[1]], axis=1)
        v = jnp.concatenate([v_ref[0], v_ref[1]], axis=1)
        ug, _, _, _, z, _, _ = _gmlp_fwd_parts(u, v, gp_ref, wcat_ref, bz_ref, pavg_ref, ts)
        n1 = _rms(yl_ref[...], gp_ref[1:2, :])
        n2 = _rms(ug * z, gp_ref[2:3, :])
        y = jnp.concatenate([n1, n2], axis=1).astype(BF16)
        y_ref[...] = y
        fo = jnp.zeros((ts, D), F32)
        for k in range(NDEV):
            fo = fo + _dot(y[:, k * OR:(k + 1) * OR], wout_ref[k])
        fo_ref[...] = fo.astype(BF16)
        xo_ref[...] = x_ref[...] + vec_ref[2:3, :] * fo

    row = pl.BlockSpec((ts, D), lambda i: (i, 0))
    full = lambda shp: pl.BlockSpec(shp, lambda i: tuple(0 for _ in shp))
    return _call(
        body, ride, name=f"mix_out_fwd_{tag}",
        grid=(S // ts,),
        in_specs=[pl.BlockSpec((2, ts, PC), lambda i: (2, i, 0)), pl.BlockSpec((2, ts, PC), lambda i: (3, i, 0)),
                  pl.BlockSpec((ts, LW), lambda i: (i, 0)), row, full((8, D)), full((8, GW)),
                  full((CHUNK, HEADS * CHUNK)), full((CHUNK, GW)), full((PAIR, PAIR)),
                  pl.BlockSpec((NDEV, OR, D), lambda i: (0, 0, 0))],
        out_specs=[row, row, row],
        out_shape=[jax.ShapeDtypeStruct((S, D), F32), jax.ShapeDtypeStruct((S, D), BF16),
                   jax.ShapeDtypeStruct((S, D), BF16)],
        scratch_shapes=[], args=(proj, proj, ylru, x, vec, gp, wcat, bz, pavg, wout_g))


def mix_out_bwd(dxo, proj, ylru, fo, vec, gp, wcat, wcat_t, bz, pavg, wout_t, tag):
    S = dxo.shape[0]
    ts = min(MIX_TS, S)

    def body(dxo_ref, u_ref, v_ref, yl_ref, fo_ref, vec_ref, gp_ref, wcat_ref, wcatt_ref, bz_ref, pavg_ref,
             wout_ref, dyo_ref, dyl_ref, duv_ref, acc_ref, dgp_ref, dwm_ref, dbz_ref):
        @pl.when(pl.program_id(0) == 0)
        def _():
            acc_ref[...] = jnp.zeros_like(acc_ref)
            dgp_ref[...] = jnp.zeros_like(dgp_ref)
            dwm_ref[...] = jnp.zeros_like(dwm_ref)
            dbz_ref[...] = jnp.zeros_like(dbz_ref)

        dxo_v = dxo_ref[...]
        acc_ref[2:3, :] += _csum(dxo_v * fo_ref[...].astype(F32))
        dyo = (vec_ref[2:3, :] * dxo_v).astype(BF16)
        dyo_ref[...] = dyo
        dn = _dot(dyo, wout_ref[...])
        dn1, dn2 = dn[:, :LW], dn[:, LW:]
        dyl, dg1 = _rms_bwd(dn1, yl_ref[...], gp_ref[1:2, :])
        dyl_ref[...] = dyl
        u = jnp.concatenate([u_ref[0], u_ref[1]], axis=1)
        v = jnp.concatenate([v_ref[0], v_ref[1]], axis=1)
        ug, rs, vhat, vh, z, ugrad, vgrad = _gmlp_fwd_parts(u, v, gp_ref, wcat_ref, bz_ref, pavg_ref, ts,
                                                            with_grad=True)
        dyg, dg2 = _rms_bwd(dn2, ug * z, gp_ref[2:3, :])
        du = (dyg * z) * ugrad
        dz = dyg * ug
        low = lax.broadcasted_iota(jnp.int32, (CHUNK, PAIR), 1) < HD
        vhb = vh.astype(BF16)
        dvhs = []
        dbz = jnp.zeros((CHUNK, GW), F32)
        dwm = [jnp.zeros((2 * CHUNK, CHUNK), F32) for _ in range(GW // PAIR)]
        for n in range(ts // CHUNK):
            dzc = dz[n * CHUNK:(n + 1) * CHUNK, :]
            dbz = dbz + dzc
            for p in range(GW // PAIR):
                stack = _pair_stack(dzc[:, p * PAIR:(p + 1) * PAIR], low)
                dwm[p] = dwm[p] + _dot_nt(stack, vhb[n * CHUNK:(n + 1) * CHUNK, p * PAIR:(p + 1) * PAIR])
            dvhs.append(_spatial(wcatt_ref, dzc, low))
        dbz_ref[...] += dbz
        for p in range(GW // PAIR):
            dwm_ref[2 * p * CHUNK:2 * (p + 1) * CHUNK, :] += dwm[p]
        dvh = jnp.concatenate(dvhs, axis=0) if len(dvhs) > 1 else dvhs[0]
        pavg = pavg_ref[...]
        dvn = _csum(dvh * vhat)
        dvhat = dvh * gp_ref[0:1, :]
        dvg = rs * (dvhat - _seg_mean(dvhat, pavg) - vhat * _seg_mean(dvhat * vhat, pavg))
        dv = dvg * vgrad
        duv_ref[0] = du[:, :PC].astype(BF16)
        duv_ref[1] = du[:, PC:].astype(BF16)
        duv_ref[2] = dv[:, :PC].astype(BF16)
        duv_ref[3] = dv[:, PC:].astype(BF16)
        dgp_ref[0:1, :] += dvn
        dgp_ref[1:2, :] += dg1
        dgp_ref[2:3, :] += dg2

    row = pl.BlockSpec((ts, D), lambda i: (i, 0))
    full = lambda shp: pl.BlockSpec(shp, lambda i: tuple(0 for _ in shp))
    return pl.pallas_call(
        body, name=f"mix_out_bwd_{tag}",
        grid=(S // ts,),
        in_specs=[row, pl.BlockSpec((2, ts, PC), lambda i: (2, i, 0)), pl.BlockSpec((2, ts, PC), lambda i: (3, i, 0)),
                  pl.BlockSpec((ts, LW), lambda i: (i, 0)), row, full((8, D)), full((8, GW)),
                  full((CHUNK, HEADS * CHUNK)), full((CHUNK, HEADS * CHUNK)), full((CHUNK, GW)), full((PAIR, PAIR)),
                  full((D, D))],
        out_specs=[row, pl.BlockSpec((ts, LW), lambda i: (i, 0)), pl.BlockSpec((4, ts, PC), lambda i: (0, i, 0)),
                   full((8, D)), full((8, GW)), full((HEADS * CHUNK, CHUNK)), full((CHUNK, GW))],
        out_shape=[jax.ShapeDtypeStruct((S, D), BF16), jax.ShapeDtypeStruct((S, LW), F32),
                   jax.ShapeDtypeStruct((4, S, PC), BF16), jax.ShapeDtypeStruct((8, D), F32),
                   jax.ShapeDtypeStruct((8, GW), F32), jax.ShapeDtypeStruct((HEADS * CHUNK, CHUNK), F32),
                   jax.ShapeDtypeStruct((CHUNK, GW), F32)],
        compiler_params=_cparams("arbitrary"),
    )(dxo, proj, proj, ylru, fo, vec, gp, wcat, wcat_t, bz, pavg, wout_t)


def final_loss(x, target, gain):
    S = x.shape[0]
    ts = min(512, S)

    def body(x_ref, t_ref, g_ref, loss_ref, dx_ref, dg_ref):
        @pl.when(pl.program_id(0) == 0)
        def _():
            loss_ref[...] = jnp.zeros_like(loss_ref)
            dg_ref[...] = jnp.zeros_like(dg_ref)

        xv = x_ref[...]
        gain_v = g_ref[0:1, :]
        rstd = lax.rsqrt(_rmean(xv * xv) + EPS)
        xhat = xv * rstd
        err = xhat * gain_v - t_ref[...]
        loss_ref[...] += 0.5 * _csum(_rmean(err * err))
        dy = err * (1.0 / D)
        dg_ref[0:1, :] += _csum(dy * xhat)
        dxhat = dy * gain_v
        dx_ref[...] = rstd * (dxhat - xhat * _rmean(dxhat * xhat))

    row = pl.BlockSpec((ts, D), lambda i: (i, 0))
    return pl.pallas_call(
        body, name="final_loss",
        grid=(S // ts,),
        in_specs=[row, row, pl.BlockSpec((8, D), lambda i: (0, 0))],
        out_specs=[pl.BlockSpec((8, 128), lambda i: (0, 0)), row, pl.BlockSpec((8, D), lambda i: (0, 0))],
        out_shape=[jax.ShapeDtypeStruct((8, 128), F32), jax.ShapeDtypeStruct((S, D), F32),
                   jax.ShapeDtypeStruct((8, D), F32)],
        compiler_params=_cparams("arbitrary"),
    )(x, target, gain)


def _vec(mod_l, j, gain):
    return jnp.concatenate([mod_l[3 * j:3 * j + 3], gain[None, :], jnp.zeros((4, D), F32)], axis=0)


def _block_diag_tiles(w):
    w4 = w.reshape(LW // LC, 2, HD, HD)
    eye2 = jnp.eye(2, dtype=w.dtype)
    return (w4[:, :, :, None, :] * eye2[None, :, None, :, None]).reshape(LW // LC, LC, LC).astype(BF16)


def _block_diag_extract(dw):
    d5 = dw.reshape(LW // LC, 2, HD, 2, HD)
    return jnp.einsum('cihkj,ik->cihj', d5, jnp.eye(2, dtype=dw.dtype)).reshape(HEADS, HD, HD)


def _layer_params(l, p, conv_w_full):
    lp = jnp.concatenate([conv_w_full[l], p['conv_b'][l][None], p['gate_a_b'][l].reshape(1, LW),
                          p['gate_x_b'][l].reshape(1, LW), p['lru_lambda'][l][None]], axis=0)
    gp = jnp.concatenate([p['v_norm'][l][None], p['lru_out_norm'][l][None], p['gmlp_out_norm'][l][None],
                          jnp.zeros((5, GW), F32)], axis=0)
    ws = p['spatial_w'][l] * jnp.tril(jnp.ones((CHUNK, CHUNK), F32))
    wcat = ws.transpose(1, 0, 2).reshape(CHUNK, HEADS * CHUNK).astype(BF16)
    wcat_t = ws.transpose(2, 0, 1).reshape(CHUNK, HEADS * CHUNK).astype(BF16)
    bz = jnp.repeat(p['spatial_b'][l].T, HD, axis=1)
    return dict(lp=lp, gp=gp, wcat=wcat, wcat_t=wcat_t, bz=bz,
                wa_t=_block_diag_tiles(p['gate_a_w'][l]), wx_t=_block_diag_tiles(p['gate_x_w'][l]))


def _pavg():
    return jnp.kron(jnp.eye(2, dtype=F32), jnp.full((HD, HD), 1.0 / HD, F32)).astype(BF16)


GATHER_RIDES = {
    ('ffn_a', 0): [('w_in', 0), ('gu', DEPTH)],
    ('mix_in', 0): [('w_out', 0)],
    ('lru', 0): [('down', DEPTH)],
    ('mix_out', 0): [('down', 1)],
    ('ffn_b', 0): [('gu', 1), ('w_in', 1)],
    ('ffn_a', 1): [('gu', DEPTH + 1), ('w_out', 1)],
    ('mix_in', 1): [('down', DEPTH + 1)],
}


def local_fwd_bwd(me_arr, x, target, mod, p, loc, gathered, conv_w_full):
    pavg = _pavg()
    g = dict(gathered)

    def ride(call, l):
        todo = GATHER_RIDES.get((call, l))
        return None if todo is None else (todo, GatherRide([(loc[kind], slot) for kind, slot in todo]))

    def run(fn, call, l, *args):
        r = ride(call, l)
        outs, got = fn(*args, ride=None if r is None else r[1])
        if r is not None:
            g.update(dict(zip(r[0], got)))
        return outs

    saved = []
    h = x
    for l in range(DEPTH):
        q = _layer_params(l, p, conv_w_full)
        v1 = _vec(mod[l], 0, p['ffn1_norm'][l])
        vm = _vec(mod[l], 1, p['mix_norm'][l])
        v2 = _vec(mod[l], 2, p['ffn2_norm'][l])
        x0 = h
        x1, h1, gu1, f1 = run(ffn_fwd, 'ffn_a', l, x0, v1, g['gu', l], g['down', l], f"a{l}")
        hm, proj = run(mix_in_fwd, 'mix_in', l, x1, vm, g['w_in', l], f"{l}")
        ylru, xc, hst = run(lru_fwd, 'lru', l, proj, q['lp'], q['wa_t'], q['wx_t'], f"{l}")
        x2, y, fo = run(mix_out_fwd, 'mix_out', l, proj, ylru, x1, vm, q['gp'], q['wcat'], q['bz'], pavg,
                        g['w_out', l], f"{l}")
        x3, h2, gu2, f2 = run(ffn_fwd, 'ffn_b', l, x2, v2, g['gu', DEPTH + l], g['down', DEPTH + l], f"b{l}")
        saved.append(dict(q=q, v1=v1, vm=vm, v2=v2, x0=x0, x1=x1, x2=x2, h1=h1, gu1=gu1, f1=f1, hm=hm, proj=proj,
                          ylru=ylru, xc=xc, hst=hst, y=y, fo=fo, h2=h2, gu2=gu2, f2=f2))
        h = x3
    fin = jnp.concatenate([p['final_norm'][None], jnp.zeros((7, D), F32)], axis=0)
    loss8, dx, dfin = final_loss(h, target, fin)
    loss = loss8[0, 0]

    big = dict(gu=None, down=None, w_in=None, w_out=None)
    small = {k: [None] * DEPTH for k in ('ffn1_norm', 'mix_norm', 'ffn2_norm', 'conv_w', 'conv_b', 'gate_a_w',
                                         'gate_a_b', 'gate_x_w', 'gate_x_b', 'lru_lambda', 'v_norm', 'spatial_w',
                                         'spatial_b', 'lru_out_norm', 'gmlp_out_norm')}
    dmod = [None] * DEPTH
    tril = jnp.tril(jnp.ones((CHUNK, CHUNK), F32))
    for l in reversed(range(DEPTH)):
        sv = saved[l]
        q = sv['q']
        dx2, dgu, a, df, acc2 = ffn_bwd(dx, sv['x2'], sv['gu2'], sv['f2'], sv['v2'],
                                        g['gu', DEPTH + l], g['down', DEPTH + l], f"b{l}")
        big['gu'] = tn_matmul_scatter(me_arr, dgu, sv['h2'][None], DEPTH + l, 2 * DEPTH, big['gu'], f"dw_gu_b{l}")
        big['down'] = tn_matmul_scatter(me_arr, a, df[None], DEPTH + l, 2 * DEPTH, big['down'], f"dw_down_b{l}", split=2)
        dyo, dylru, duv, accmo, dgp, dwm, dbz = mix_out_bwd(dx2, sv['proj'], sv['ylru'], sv['fo'], sv['vm'], q['gp'],
                                                             q['wcat'], q['wcat_t'], q['bz'], pavg,
                                                             g['w_out', l].reshape(D, D).T, f"{l}")
        big['w_out'] = tn_matmul_scatter(me_arr, sv['y'][None], dyo[None], l, DEPTH, big['w_out'], f"dw_out_{l}",
                                         split=NDEV)
        dxl, dgl, dlp, dwa, dwx = lru_bwd(dylru, sv['proj'], sv['xc'], sv['hst'], q['lp'], q['wa_t'], q['wx_t'], f"{l}")
        dproj = jnp.concatenate([dxl, dgl, duv], axis=0)
        dx1, accmi = mix_in_bwd(dproj, sv['x1'], dx2, sv['vm'], g['w_in', l].transpose(0, 2, 1), f"{l}")
        big['w_in'] = tn_matmul_scatter(me_arr, sv['hm'][None], dproj, l, DEPTH, big['w_in'], f"dw_in_{l}")
        dx0, dgu, a, df, acc1 = ffn_bwd(dx1, sv['x0'], sv['gu1'], sv['f1'], sv['v1'],
                                        g['gu', l], g['down', l], f"a{l}")
        big['gu'] = tn_matmul_scatter(me_arr, dgu, sv['h1'][None], l, 2 * DEPTH, big['gu'], f"dw_gu_a{l}")
        big['down'] = tn_matmul_scatter(me_arr, a, df[None], l, 2 * DEPTH, big['down'], f"dw_down_a{l}", split=2)
        dx = dx0
        dmod[l] = jnp.concatenate([acc1[0:3], accmi[0:2], accmo[2:3], acc2[0:3]], axis=0)
        small['ffn1_norm'][l] = acc1[3]
        small['mix_norm'][l] = accmi[3]
        small['ffn2_norm'][l] = acc2[3]
        small['conv_w'][l] = dlp[0:4]
        small['conv_b'][l] = dlp[4]
        small['gate_a_b'][l] = dlp[5].reshape(HEADS, HD)
        small['gate_x_b'][l] = dlp[6].reshape(HEADS, HD)
        small['lru_lambda'][l] = dlp[7]
        small['gate_a_w'][l] = _block_diag_extract(dwa)
        small['gate_x_w'][l] = _block_diag_extract(dwx)
        small['v_norm'][l] = dgp[0]
        small['lru_out_norm'][l] = dgp[1]
        small['gmlp_out_norm'][l] = dgp[2]
        small['spatial_w'][l] = dwm.reshape(HEADS, CHUNK, CHUNK) * tril
        small['spatial_b'][l] = dbz.reshape(CHUNK, HEADS, HD).sum(-1).T
    small = {k: jnp.stack(v) for k, v in small.items()}
    small['final_norm'] = dfin[0]
    return loss, dx, big, small, jnp.stack(dmod)


def ada_fwd(c_all, w_ada, b_loc):
    def body(c_ref, w_ref, b_ref, mod_ref, sc_ref):
        cv = c_ref[...]
        sc = cv * _sigmoid(cv)
        sc_ref[...] = sc
        mod_ref[...] = _dot3(sc, w_ref[...]) + b_ref[...]

    return pl.pallas_call(
        body, name="ada_fwd",
        grid=(DEPTH,),
        in_specs=[pl.BlockSpec((NDEV, D), lambda l: (0, 0)), pl.BlockSpec((None, D, AC), lambda l: (l, 0, 0)),
                  pl.BlockSpec((None, 1, AC), lambda l: (l, 0, 0))],
        out_specs=[pl.BlockSpec((None, NDEV, AC), lambda l: (l, 0, 0)), pl.BlockSpec((NDEV, D), lambda l: (0, 0))],
        out_shape=[jax.ShapeDtypeStruct((DEPTH, NDEV, AC), F32), jax.ShapeDtypeStruct((NDEV, D), F32)],
        compiler_params=_cparams("arbitrary"),
    )(c_all, w_ada, b_loc)


def ada_bwd(sc_t, dmod_cols):
    def body(sc_ref, dm_ref, g_ref):
        sc = sc_ref[...]
        dm = dm_ref[...]
        acc = sc[:, 0:1] * dm[0:1, :]
        for b in range(1, NDEV):
            acc = acc + sc[:, b:b + 1] * dm[b:b + 1, :]
        g_ref[...] = acc

    return pl.pallas_call(
        body, name="ada_bwd",
        grid=(DEPTH,),
        in_specs=[pl.BlockSpec((D, NDEV), lambda l: (0, 0)), pl.BlockSpec((None, NDEV, AC), lambda l: (l, 0, 0))],
        out_specs=pl.BlockSpec((None, None, D, AC), lambda l: (0, l, 0, 0)),
        out_shape=jax.ShapeDtypeStruct((1, DEPTH, D, AC), F32),
        compiler_params=_cparams("arbitrary"),
    )(sc_t, dmod_cols)


def _row_tile(rows, cols):
    if rows * cols <= 512 * 1024:
        return rows
    for tr in (512, 384, 352, 256, 128, 64, 32, 16, 8):
        if rows % tr == 0:
            return tr
    return rows


def adamw(gparts, slot0, w, m, v, name):
    P, _, R, C = gparts.shape
    L = w.shape[0]
    tr = _row_tile(R, C)

    def body(g_ref, w_ref, m_ref, v_ref, go_ref, do_ref, mo_ref, vo_ref):
        g = g_ref[0].astype(F32)
        for p in range(1, P):
            g = g + g_ref[p].astype(F32)
        go_ref[...] = g
        mn = ADAM_B1 * m_ref[...] + (1.0 - ADAM_B1) * g
        vn = ADAM_B2 * v_ref[...] + (1.0 - ADAM_B2) * (g * g)
        mo_ref[...] = mn
        vo_ref[...] = vn
        m_hat = mn / (1.0 - ADAM_B1 ** ADAM_STEP)
        v_hat = vn / (1.0 - ADAM_B2 ** ADAM_STEP)
        do_ref[...] = -ADAM_LR * (m_hat / (jnp.sqrt(v_hat) + ADAM_EPS) + ADAM_WD * w_ref[...])

    blk = pl.BlockSpec((None, tr, C), lambda l, i: (l, i, 0))
    return pl.pallas_call(
        body, name=name,
        grid=(L, R // tr),
        in_specs=[pl.BlockSpec((P, None, tr, C), lambda l, i: (0, slot0 + l, i, 0)), blk, blk, blk],
        out_specs=[blk, blk, blk, blk],
        out_shape=[jax.ShapeDtypeStruct((L, R, C), F32)] * 4,
        compiler_params=_cparams("arbitrary", "arbitrary"),
    )(gparts, w, m, v)


def sum_parts(parts):
    P, R, C = parts.shape

    def body(p_ref, o_ref):
        acc = p_ref[0]
        for p in range(1, P):
            acc = acc + p_ref[p]
        o_ref[...] = acc

    return pl.pallas_call(
        body, name="sum_parts",
        in_specs=[pl.BlockSpec(memory_space=pltpu.VMEM)],
        out_specs=pl.BlockSpec(memory_space=pltpu.VMEM),
        out_shape=jax.ShapeDtypeStruct((R, C), F32),
    )(parts)


WEIGHTS = ['w_ada', 'b_ada', 'ffn1_norm', 'ffn1_w_gu', 'ffn1_w_down', 'mix_norm', 'w_in', 'conv_w', 'conv_b',
           'gate_a_w', 'gate_a_b', 'gate_x_w', 'gate_x_b', 'lru_lambda', 'v_norm', 'spatial_w', 'spatial_b',
           'lru_out_norm', 'gmlp_out_norm', 'w_out', 'ffn2_norm', 'ffn2_w_gu', 'ffn2_w_down', 'final_norm']
PACKED = ['b_ada', 'ffn1_norm', 'mix_norm', 'conv_b', 'gate_a_w', 'gate_a_b', 'gate_x_w', 'gate_x_b', 'lru_lambda',
          'v_norm', 'spatial_w', 'spatial_b', 'lru_out_norm', 'gmlp_out_norm', 'ffn2_norm', 'final_norm', 'conv_w']
PACK_LANES = 128
PACK_ROW_ALIGN = 8 * NDEV


PACK_TAIL = 8


def _pack_rows(shapes):
    used = 0
    for k in PACKED:
        size = 1
        for s in shapes[k]:
            size *= s
        used += size // PACK_LANES
    return used, -(-(used + PACK_TAIL) // PACK_ROW_ALIGN) * PACK_ROW_ALIGN


def _pack(d, tail=None):
    parts = [d[k].reshape(-1, PACK_LANES).astype(F32) for k in PACKED]
    used, rows = _pack_rows({k: d[k].shape for k in PACKED})
    parts.append(jnp.zeros((PACK_TAIL, PACK_LANES), F32) if tail is None else tail)
    return jnp.concatenate(parts + [jnp.zeros((rows - used - PACK_TAIL, PACK_LANES), F32)], axis=0)


def _unpack(buf, shapes):
    out, off = {}, 0
    for k in PACKED:
        size = 1
        for s in shapes[k]:
            size *= s
        nrows = size // PACK_LANES
        out[k] = buf[off:off + nrows].reshape(shapes[k])
        off += nrows
    return out


def kernel(x, c, w_ada, b_ada, ffn1_norm, ffn1_w_gu, ffn1_w_down, mix_norm, w_in, conv_w, conv_b, gate_a_w, gate_a_b, gate_x_w, gate_x_b, lru_lambda, v_norm, spatial_w, spatial_b, lru_out_norm, gmlp_out_norm, w_out, ffn2_norm, ffn2_w_gu, ffn2_w_down, final_norm, loss_target, m_w_ada, m_b_ada, m_ffn1_norm, m_ffn1_w_gu, m_ffn1_w_down, m_mix_norm, m_w_in, m_conv_w, m_conv_b, m_gate_a_w, m_gate_a_b, m_gate_x_w, m_gate_x_b, m_lru_lambda, m_v_norm, m_spatial_w, m_spatial_b, m_lru_out_norm, m_gmlp_out_norm, m_w_out, m_ffn2_norm, m_ffn2_w_gu, m_ffn2_w_down, m_final_norm, v_w_ada, v_b_ada, v_ffn1_norm, v_ffn1_w_gu, v_ffn1_w_down, v_mix_norm, v_w_in, v_conv_w, v_conv_b, v_gate_a_w, v_gate_a_b, v_gate_x_w, v_gate_x_b, v_lru_lambda, v_v_norm, v_spatial_w, v_spatial_b, v_lru_out_norm, v_gmlp_out_norm, v_w_out, v_ffn2_norm, v_ffn2_w_gu, v_ffn2_w_down, v_final_norm):
    w = dict(w_ada=w_ada, b_ada=b_ada, ffn1_norm=ffn1_norm, ffn1_w_gu=ffn1_w_gu, ffn1_w_down=ffn1_w_down, mix_norm=mix_norm, w_in=w_in, conv_w=conv_w, conv_b=conv_b, gate_a_w=gate_a_w, gate_a_b=gate_a_b, gate_x_w=gate_x_w, gate_x_b=gate_x_b, lru_lambda=lru_lambda, v_norm=v_norm, spatial_w=spatial_w, spatial_b=spatial_b, lru_out_norm=lru_out_norm, gmlp_out_norm=gmlp_out_norm, w_out=w_out, ffn2_norm=ffn2_norm, ffn2_w_gu=ffn2_w_gu, ffn2_w_down=ffn2_w_down, final_norm=final_norm)
    m = dict(w_ada=m_w_ada, b_ada=m_b_ada, ffn1_norm=m_ffn1_norm, ffn1_w_gu=m_ffn1_w_gu, ffn1_w_down=m_ffn1_w_down, mix_norm=m_mix_norm, w_in=m_w_in, conv_w=m_conv_w, conv_b=m_conv_b, gate_a_w=m_gate_a_w, gate_a_b=m_gate_a_b, gate_x_w=m_gate_x_w, gate_x_b=m_gate_x_b, lru_lambda=m_lru_lambda, v_norm=m_v_norm, spatial_w=m_spatial_w, spatial_b=m_spatial_b, lru_out_norm=m_lru_out_norm, gmlp_out_norm=m_gmlp_out_norm, w_out=m_w_out, ffn2_norm=m_ffn2_norm, ffn2_w_gu=m_ffn2_w_gu, ffn2_w_down=m_ffn2_w_down, final_norm=m_final_norm)
    v = dict(w_ada=v_w_ada, b_ada=v_b_ada, ffn1_norm=v_ffn1_norm, ffn1_w_gu=v_ffn1_w_gu, ffn1_w_down=v_ffn1_w_down, mix_norm=v_mix_norm, w_in=v_w_in, conv_w=v_conv_w, conv_b=v_conv_b, gate_a_w=v_gate_a_w, gate_a_b=v_gate_a_b, gate_x_w=v_gate_x_w, gate_x_b=v_gate_x_b, lru_lambda=v_lru_lambda, v_norm=v_v_norm, spatial_w=v_spatial_w, spatial_b=v_spatial_b, lru_out_norm=v_lru_out_norm, gmlp_out_norm=v_gmlp_out_norm, w_out=v_w_out, ffn2_norm=v_ffn2_norm, ffn2_w_gu=v_ffn2_w_gu, ffn2_w_down=v_ffn2_w_down, final_norm=v_final_norm)
    me = 4 * lax.axis_index("x") + 2 * lax.axis_index("y") + lax.axis_index("c")

    loc = dict(gu=jnp.concatenate([ffn1_w_gu, ffn2_w_gu], axis=0).astype(BF16),
               down=jnp.concatenate([ffn1_w_down, ffn2_w_down], axis=0).astype(BF16),
               w_in=w_in.astype(BF16), w_out=w_out.astype(BF16))
    c_g, conv_g, gu0, down0 = all_gather([(c, None), (conv_w, None), (loc['gu'], 0), (loc['down'], 0)], "gather_first")
    conv_w_full = conv_g.transpose(1, 2, 0, 3).reshape(DEPTH, CONV_WIDTH, LW)

    b_loc = lax.dynamic_slice(b_ada, (0, me * AC), (DEPTH, AC)).reshape(DEPTH, 1, AC)
    mod_cols, sc_all = ada_fwd(c_g.reshape(NDEV, D), w_ada, b_loc)
    (mod_rows,) = all_to_all([mod_cols.transpose(1, 0, 2)], "scatter_mod")
    mod = mod_rows.transpose(1, 0, 2).reshape(DEPTH, NMOD, D)

    small_w = {k: w[k] for k in PACKED if k != 'conv_w'}
    me_arr = jnp.reshape(me, (1,)).astype(jnp.int32)
    loss_loc, dx, big, small_g, dmod = local_fwd_bwd(me_arr, x[0], loss_target[0], mod, small_w, loc,
                                                     {('gu', 0): gu0, ('down', 0): down0}, conv_w_full)

    small_g['b_ada'] = dmod.reshape(DEPTH, NMOD * D)
    first = (lax.broadcasted_iota(jnp.int32, (PACK_TAIL, PACK_LANES), 0)
             + lax.broadcasted_iota(jnp.int32, (PACK_TAIL, PACK_LANES), 1)) == 0
    gpack = _pack(small_g, jnp.where(first, loss_loc, 0.0))
    rows = gpack.shape[0]
    dmod_out = dmod.reshape(DEPTH, NDEV, AC).transpose(1, 0, 2)
    dmod_r, pack_r = all_to_all([dmod_out, gpack.reshape(NDEV, rows // NDEV, PACK_LANES)], "scatter_grads")
    (gsum_g,) = all_gather([(sum_parts(pack_r), None)], "gather_small_grads")
    gsum = gsum_g.reshape(1, 1, rows, PACK_LANES)
    loss = gsum[0, 0, _pack_rows({k: small_g[k].shape for k in PACKED})[0], 0]

    res = {}
    t = lambda a: a.transpose(0, 2, 1)
    gu_t = big['gu']
    res['ffn1_w_gu'] = tuple(t(r) for r in adamw(gu_t, 0, t(w['ffn1_w_gu']), t(m['ffn1_w_gu']), t(v['ffn1_w_gu']),
                                                 "adamw_gu_a"))
    res['ffn2_w_gu'] = tuple(t(r) for r in adamw(gu_t, DEPTH, t(w['ffn2_w_gu']), t(m['ffn2_w_gu']),
                                                 t(v['ffn2_w_gu']), "adamw_gu_b"))
    res['ffn1_w_down'] = adamw(big['down'], 0, w['ffn1_w_down'], m['ffn1_w_down'], v['ffn1_w_down'], "adamw_down_a")
    res['ffn2_w_down'] = adamw(big['down'], DEPTH, w['ffn2_w_down'], m['ffn2_w_down'], v['ffn2_w_down'], "adamw_down_b")
    res['w_in'] = adamw(big['w_in'], 0, w['w_in'], m['w_in'], v['w_in'], "adamw_w_in")
    res['w_out'] = adamw(big['w_out'], 0, w['w_out'], m['w_out'], v['w_out'], "adamw_w_out")
    g_ada = ada_bwd(sc_all.T, dmod_r.transpose(1, 0, 2))
    res['w_ada'] = adamw(g_ada, 0, w['w_ada'], m['w_ada'], v['w_ada'], "adamw_w_ada")
    shapes = {k: w[k].shape for k in PACKED}
    shapes['conv_w'] = (DEPTH, CONV_WIDTH, LW)
    dummy = jnp.zeros(shapes['conv_w'], F32)
    packs = adamw(gsum, 0, _pack({**small_w, 'conv_w': dummy})[None], _pack({**{k: m[k] for k in small_w}, 'conv_w': dummy})[None],
                  _pack({**{k: v[k] for k in small_w}, 'conv_w': dummy})[None], "adamw_small")
    unpacked = [_unpack(b[0], shapes) for b in packs]
    for k in small_w:
        res[k] = tuple(u[k] for u in unpacked)
    gconv = lax.dynamic_slice(unpacked[0]['conv_w'], (0, 0, me * (LW // NDEV)), (DEPTH, CONV_WIDTH, LW // NDEV))
    cshape = (1, DEPTH * CONV_WIDTH, LW // NDEV)
    rc = adamw(gconv.reshape((1,) + cshape), 0, conv_w.reshape(cshape), m['conv_w'].reshape(cshape),
               v['conv_w'].reshape(cshape), "adamw_conv_w")
    res['conv_w'] = tuple(r.reshape(conv_w.shape) for r in rc)

    return (loss, dx[None], *[res[k][0] for k in WEIGHTS], *[res[k][1] for k in WEIGHTS],
            *[res[k][2] for k in WEIGHTS], *[res[k][3] for k in WEIGHTS])
```

```python
import jax
import jax.numpy as jnp
from jax import lax
from jax.experimental import pallas as pl
from jax.experimental.pallas import tpu as pltpu

F32 = jnp.float32
BF16 = jnp.bfloat16

NDEV = 8
DEPTH = 2
D = 1024
DFF = 2816
FC = 2 * DFF // NDEV
NCHUNK = DFF // FC
DR = DFF // NDEV
LW = 512
GW = 512
HD = 64
HEADS = 8
CHUNK = 128
PC = 2 * (LW + GW) // NDEV
OR = D // NDEV
NMOD = 9
AC = NMOD * D // NDEV
LC = 128
EPS = 1e-6
RG_LRU_C = 8.0
CONV_WIDTH = 4

ADAM_LR = 0.001
ADAM_B1 = 0.9
ADAM_B2 = 0.999
ADAM_EPS = 1e-08
ADAM_WD = 0.01
ADAM_STEP = 10

VMEM_LIMIT_BYTES = 60 * 1024 * 1024
MESH = pl.DeviceIdType.MESH
ANY = pl.BlockSpec(memory_space=pl.ANY)


def _cparams(*sem):
    return pltpu.CompilerParams(dimension_semantics=tuple(sem) if sem else None,
                                vmem_limit_bytes=VMEM_LIMIT_BYTES)


def _dot(a, b):
    return jnp.dot(a, b, preferred_element_type=F32)


def _dot_nt(a, b):
    return lax.dot_general(a, b, (((1,), (1,)), ((), ())), preferred_element_type=F32)


def _dot_tn(a, b):
    return lax.dot_general(a, b, (((0,), (0,)), ((), ())), preferred_element_type=F32)


def _split(a):
    hi = a.astype(BF16)
    lo = (a - hi.astype(F32)).astype(BF16)
    return hi, lo


def _dot3(a, b):
    ah, al = _split(a)
    bh, bl = _split(b)
    return _dot(ah, bh) + (_dot(ah, bl) + _dot(al, bh))


def _csum(a):
    return jnp.sum(a, axis=0, keepdims=True)


def _rmean(a):
    return jnp.mean(a, axis=-1, keepdims=True)


def _sigmoid(a):
    return 1.0 / (1.0 + jnp.exp(-a))


_GELU_K = 0.7978845608028654
_GELU_C = 0.044715


def _gelu(a):
    return 0.5 * a * (1.0 + jnp.tanh(_GELU_K * (a + _GELU_C * a * a * a)))


def _gelu_and_grad(a):
    a2 = a * a
    t = jnp.tanh(_GELU_K * (a + _GELU_C * a2 * a))
    half = 0.5 * (1.0 + t)
    return a * half, half + 0.5 * a * (1.0 - t * t) * (_GELU_K * (1.0 + 3.0 * _GELU_C * a2))


def _norm_mod(x, gain, scale, shift):
    rstd = lax.rsqrt(_rmean(x * x) + EPS)
    return (x * rstd * gain) * (1.0 + scale) + shift


def _norm_mod_bwd(dh, x, gain, scale):
    rstd = lax.rsqrt(_rmean(x * x) + EPS)
    xhat = x * rstd
    dshift = _csum(dh)
    dscale = _csum(dh * (xhat * gain))
    dhn = dh * (1.0 + scale)
    dgain = _csum(dhn * xhat)
    dxhat = dhn * gain
    dx = rstd * (dxhat - xhat * _rmean(dxhat * xhat))
    return dx, dshift, dscale, dgain


def _rms(x, gain):
    rstd = lax.rsqrt(_rmean(x * x) + EPS)
    return x * rstd * gain


def _rms_bwd(dy, x, gain):
    rstd = lax.rsqrt(_rmean(x * x) + EPS)
    xhat = x * rstd
    dgain = _csum(dy * xhat)
    dxhat = dy * gain
    return rstd * (dxhat - xhat * _rmean(dxhat * xhat)), dgain


PAIR = 2 * HD


def _seg_mean(a, pavg):
    hi, lo = _split(a)
    return jnp.concatenate([_dot(hi[:, p:p + PAIR], pavg) + _dot(lo[:, p:p + PAIR], pavg)
                            for p in range(0, a.shape[1], PAIR)], axis=1)


def _block_copies(src_hbm, dst_vmem, sems, rows):
    copies = []
    for k in range(NDEV):
        dst = dst_vmem.at[k] if rows is None else dst_vmem.at[pl.ds(k * rows, rows)]
        copies.append(pltpu.make_async_copy(src_hbm.at[k], dst, sems.at[k]))
    return copies


def _ffn_weight_fetch(wgu_hbm, wd_hbm, wgu_v, wd_v, sems):
    @pl.when(pl.program_id(0) == 0)
    def _():
        copies = _block_copies(wgu_hbm, wgu_v, sems.at[0], None) + _block_copies(wd_hbm, wd_v, sems.at[1], DR)
        for cp in copies:
            cp.start()
        for cp in copies:
            cp.wait()


def _place():
    return lax.axis_index("x"), lax.axis_index("y"), lax.axis_index("c")


def _slot(p):
    return 4 * p[0] + 2 * p[1] + p[2]


class GatherRide:
    def __init__(self, srcs):
        self.n = len(srcs)
        self.index = [i for _, i in srcs]
        self.args = [a for a, _ in srcs]
        self.out_shape = [jax.ShapeDtypeStruct((NDEV,) + (a.shape if i is None else a.shape[1:]), a.dtype)
                          for a, i in srcs]
        self.scratch = [pltpu.SemaphoreType.DMA((self.n, NDEV - 1)), pltpu.SemaphoreType.DMA((self.n, NDEV - 1)),
                        pltpu.SemaphoreType.DMA((self.n,))]

    def hooks(self, ins, outs, sems):
        send_sems, recv_sems, local_sems = sems
        n = self.n
        x, y, c = _place()
        me, sibling = (x, y, c), (x, y, 1 - c)
        chips = [(1 - x, y), (x, 1 - y), (1 - x, 1 - y)]

        def local(a):
            return ins[a] if self.index[a] is None else ins[a].at[self.index[a]]

        def copy(a, k, block, to, src=None):
            dst = outs[a].at[_slot(block)]
            return pltpu.make_async_remote_copy(
                src_ref=dst if src is None else src, dst_ref=dst,
                send_sem=send_sems.at[a, k], recv_sem=recv_sems.at[a, k],
                device_id=to, device_id_type=MESH)

        def mine():
            return [pltpu.make_async_copy(local(a), outs[a].at[_slot(me)], local_sems.at[a]) for a in range(n)]

        def first():
            cps = []
            for a in range(n):
                cps.append(copy(a, 0, me, sibling, src=local(a)))
                cps += [copy(a, 1 + j, me, (*chip, c), src=local(a)) for j, chip in enumerate(chips)]
            return cps

        def passed():
            return [copy(a, 4 + j, (*chip, c), sibling) for j, chip in enumerate(chips) for a in range(n)]

        def start():
            for cp in mine() + first():
                cp.start()

        def mid():
            for j, chip in enumerate(chips):
                for a in range(n):
                    copy(a, 1 + j, (*chip, c), me).wait_recv()
                    copy(a, 4 + j, (*chip, c), sibling).start()

        def finish():
            for a in range(n):
                copy(a, 0, sibling, me).wait_recv()
                for j, chip in enumerate(chips):
                    copy(a, 4 + j, (*chip, 1 - c), me).wait_recv()
            for cp in first() + passed():
                cp.wait_send()
            for cp in mine():
                cp.wait()

        return start, mid, finish


def all_gather(srcs, name):
    ride = GatherRide(srcs)
    n = ride.n

    def body(*refs):
        start, mid, finish = ride.hooks(refs[:n], refs[n:2 * n], refs[2 * n:])
        start()
        mid()
        finish()

    return pl.pallas_call(
        body, name=name,
        in_specs=[ANY] * n, out_specs=[ANY] * n, out_shape=ride.out_shape, scratch_shapes=ride.scratch,
    )(*ride.args)


def _call(core, ride, *, name, grid, in_specs, out_specs, out_shape, scratch_shapes, args):
    if ride is None:
        outs = pl.pallas_call(core, name=name, grid=grid, in_specs=in_specs, out_specs=out_specs,
                              out_shape=out_shape, scratch_shapes=scratch_shapes,
                              compiler_params=_cparams("arbitrary"))(*args)
        return outs, []
    n_in, n_out, n_sc, n = len(in_specs), len(out_shape), len(scratch_shapes), ride.n
    nsteps = grid[0]
    mid_step = max(nsteps - 2, 0)

    def body(*refs):
        cuts = [n_in, n_in + n, n_in + n + n_out, n_in + 2 * n + n_out, n_in + 2 * n + n_out + n_sc]
        ci, ri, co, ro, cs, rs = (refs[a:b] for a, b in zip([0] + cuts, cuts + [len(refs)]))
        start, mid, finish = ride.hooks(ri, ro, rs)
        i = pl.program_id(0)
        pl.when(i == 0)(start)
        core(*ci, *co, *cs)
        pl.when(i == mid_step)(mid)
        pl.when(i == nsteps - 1)(finish)

    outs = pl.pallas_call(
        body, name=name, grid=grid,
        in_specs=list(in_specs) + [ANY] * n, out_specs=list(out_specs) + [ANY] * n,
        out_shape=list(out_shape) + ride.out_shape, scratch_shapes=list(scratch_shapes) + ride.scratch,
        compiler_params=_cparams("arbitrary"))(*args, *ride.args)
    return outs[:n_out], outs[n_out:]


def all_to_all(arrs, name):
    n = len(arrs)

    def body(*refs):
        ins, outs = refs[:n], refs[n:2 * n]
        send_sems, recv_sems, local_sems = refs[2 * n:]
        x, y, c = _place()
        me = (x, y, c)

        def peer(k):
            return (1 - x if k & 4 else x, 1 - y if k & 2 else y, 1 - c if k & 1 else c)

        def copy(a, k):
            return pltpu.make_async_remote_copy(
                src_ref=ins[a].at[_slot(peer(k))], dst_ref=outs[a].at[_slot(me)],
                send_sem=send_sems.at[a, k - 1], recv_sem=recv_sems.at[a, k - 1],
                device_id=peer(k), device_id_type=MESH)

        def landing(a, k):
            return pltpu.make_async_remote_copy(
                src_ref=outs[a].at[_slot(peer(k))], dst_ref=outs[a].at[_slot(peer(k))],
                send_sem=send_sems.at[a, k - 1], recv_sem=recv_sems.at[a, k - 1],
                device_id=me, device_id_type=MESH)

        mine = [pltpu.make_async_copy(ins[a].at[_slot(me)], outs[a].at[_slot(me)], local_sems.at[a]) for a in range(n)]
        for cp in mine:
            cp.start()
        sends = [copy(a, k) for a in range(n) for k in range(1, NDEV)]
        for cp in sends:
            cp.start()
        for a in range(n):
            for k in range(1, NDEV):
                landing(a, k).wait_recv()
        for cp in sends:
            cp.wait_send()
        for cp in mine:
            cp.wait()

    return pl.pallas_call(
        body, name=name,
        in_specs=[ANY] * n, out_specs=[ANY] * n,
        out_shape=[jax.ShapeDtypeStruct(a.shape, a.dtype) for a in arrs],
        scratch_shapes=[pltpu.SemaphoreType.DMA((n, NDEV - 1)), pltpu.SemaphoreType.DMA((n, NDEV - 1)),
                        pltpu.SemaphoreType.DMA((n,))],
    )(*arrs)


FFN_TS = 256
FFN_FWD_TS = 512


def ffn_fwd(x, vec, wgu_g, wdown_g, tag, ride=None):
    S = x.shape[0]
    ts = min(FFN_FWD_TS, S)

    def body(x_ref, vec_ref, wgu_hbm, wd_hbm, xo_ref, h_ref, gu_ref, f_ref, wgu_v, wd_v, sems):
        _ffn_weight_fetch(wgu_hbm, wd_hbm, wgu_v, wd_v, sems)
        xv = x_ref[...]
        h = _norm_mod(xv, vec_ref[3:4, :], vec_ref[1:2, :], vec_ref[0:1, :]).astype(BF16)
        h_ref[...] = h
        acc = jnp.zeros((ts, D), F32)
        for j in range(NCHUNK):
            g = _dot(h, wgu_v[j])
            u = _dot(h, wgu_v[NCHUNK + j])
            gu_ref[j] = g.astype(BF16)
            gu_ref[NCHUNK + j] = u.astype(BF16)
            a = (g * _sigmoid(g) * u).astype(BF16)
            acc = acc + _dot(a, wd_v[pl.ds(j * FC, FC), :])
        f_ref[...] = acc.astype(BF16)
        xo_ref[...] = xv + (0.5 * vec_ref[2:3, :]) * acc

    return _call(
        body, ride, name=f"ffn_fwd_{tag}",
        grid=(S // ts,),
        in_specs=[pl.BlockSpec((ts, D), lambda i: (i, 0)),
                  pl.BlockSpec((8, D), lambda i: (0, 0)), ANY, ANY],
        out_specs=[pl.BlockSpec((ts, D), lambda i: (i, 0)),
                   pl.BlockSpec((ts, D), lambda i: (i, 0)),
                   pl.BlockSpec((NDEV, ts, FC), lambda i: (0, i, 0)),
                   pl.BlockSpec((ts, D), lambda i: (i, 0))],
        out_shape=[jax.ShapeDtypeStruct((S, D), F32), jax.ShapeDtypeStruct((S, D), BF16),
                   jax.ShapeDtypeStruct((NDEV, S, FC), BF16), jax.ShapeDtypeStruct((S, D), BF16)],
        scratch_shapes=[pltpu.VMEM((NDEV, D, FC), BF16), pltpu.VMEM((DFF, D), BF16),
                        pltpu.SemaphoreType.DMA((2, NDEV))],
        args=(x, vec, wgu_g, wdown_g))


def ffn_bwd(dxo, x, gu, f, vec, wgu_g, wdown_g, tag):
    S = x.shape[0]
    ts = min(FFN_TS, S)

    def body(dxo_ref, x_ref, gu_ref, f_ref, vec_ref, wgu_hbm, wd_hbm,
             dx_ref, dgu_ref, a_ref, df_ref, acc_ref, wgu_v, wd_v, sems):
        _ffn_weight_fetch(wgu_hbm, wd_hbm, wgu_v, wd_v, sems)

        @pl.when(pl.program_id(0) == 0)
        def _():
            acc_ref[...] = jnp.zeros_like(acc_ref)

        dxo_v = dxo_ref[...]
        dgate = 0.5 * _csum(dxo_v * f_ref[...].astype(F32))
        df = ((0.5 * vec_ref[2:3, :]) * dxo_v).astype(BF16)
        df_ref[...] = df
        dh = jnp.zeros((ts, D), F32)
        for j in range(NCHUNK):
            da = _dot_nt(df, wd_v[pl.ds(j * FC, FC), :])
            g = gu_ref[j].astype(F32)
            u = gu_ref[NCHUNK + j].astype(F32)
            sg = _sigmoid(g)
            si = g * sg
            a_ref[j] = (si * u).astype(BF16)
            dg = (da * u * (sg * (1.0 + g * (1.0 - sg)))).astype(BF16)
            du = (da * si).astype(BF16)
            dgu_ref[j] = dg
            dgu_ref[NCHUNK + j] = du
            dh = dh + _dot_nt(dg, wgu_v[j]) + _dot_nt(du, wgu_v[NCHUNK + j])
        dx, dshift, dscale, dgain = _norm_mod_bwd(dh, x_ref[...], vec_ref[3:4, :], vec_ref[1:2, :])
        dx_ref[...] = dx + dxo_v
        acc_ref[0:1, :] += dshift
        acc_ref[1:2, :] += dscale
        acc_ref[2:3, :] += dgate
        acc_ref[3:4, :] += dgain

    row = pl.BlockSpec((ts, D), lambda i: (i, 0))
    return pl.pallas_call(
        body, name=f"ffn_bwd_{tag}",
        grid=(S // ts,),
        in_specs=[row, row, pl.BlockSpec((NDEV, ts, FC), lambda i: (0, i, 0)), row,
                  pl.BlockSpec((8, D), lambda i: (0, 0)), ANY, ANY],
        out_specs=[row, pl.BlockSpec((NDEV, ts, FC), lambda i: (0, i, 0)),
                   pl.BlockSpec((NCHUNK, ts, FC), lambda i: (0, i, 0)), row,
                   pl.BlockSpec((8, D), lambda i: (0, 0))],
        out_shape=[jax.ShapeDtypeStruct((S, D), F32), jax.ShapeDtypeStruct((NDEV, S, FC), BF16),
                   jax.ShapeDtypeStruct((NCHUNK, S, FC), BF16), jax.ShapeDtypeStruct((S, D), BF16),
                   jax.ShapeDtypeStruct((8, D), F32)],
        scratch_shapes=[pltpu.VMEM((NDEV, D, FC), BF16), pltpu.VMEM((DFF, D), BF16),
                        pltpu.SemaphoreType.DMA((2, NDEV))],
        compiler_params=_cparams("arbitrary"),
    )(dxo, x, gu, f, vec, wgu_g, wdown_g)


NCHIP = NDEV // 2


def tn_matmul_scatter(me_arr, a, b, slot, nslots, prev, name, split=1):
    na, S, M = a.shape
    nb, _, N = b.shape
    ncall = NDEV // split
    ts = min(4096, S)
    nsteps = S // ts
    mp = M // split
    other_step = {1: lambda j: 2 * j, 2: lambda j: j, 8: lambda j: 0}[split]
    mine_step = {1: lambda j: 2 * j + 1, 2: lambda j: j, 8: lambda j: 0}[split]

    def group(k, me_ref):
        if split == 1:
            return jnp.bitwise_xor(me_ref[0], NDEV - 1 - k)
        if split == 2:
            return jnp.bitwise_xor(me_ref[0] // 2, NCHIP - 1 - k)
        return 0

    def body(me_ref, *refs):
        a_ref, b_ref = refs[0], refs[1]
        recv_ref, acc, sb_other, sb_mine, land, d2d_send, d2d_recv, ici_send, ici_recv = refs[-9:]
        k = pl.program_id(0)
        s = pl.program_id(1)
        x, y, c = _place()
        my_chip = 2 * x + y

        def chip_of(j):
            if split == 8:
                cx, cy = j // 2, j % 2
            else:
                flip = NCHIP - 1 - j
                cx, cy = (1 - x if flip & 2 else x), (1 - y if flip & 1 else y)
            return cx, cy, 2 * cx + cy

        def piece(j, core):
            if split == 1:
                return acc[...]
            start = core * mp if split == 2 else (2 * j + core) * mp
            return acc[pl.ds(pl.multiple_of(start, 8), mp), :]

        def to_sibling(j):
            return pltpu.make_async_remote_copy(
                src_ref=sb_other.at[j], dst_ref=land.at[j], send_sem=d2d_send.at[j], recv_sem=d2d_recv.at[j],
                device_id=(x, y, 1 - c), device_id_type=MESH)

        def to_owner(j):
            cx, cy, ci = chip_of(j)
            dst = recv_ref.at[my_chip, slot]
            return ci, pltpu.make_async_copy(sb_mine.at[j], dst, ici_send.at[j]), pltpu.make_async_remote_copy(
                src_ref=sb_mine.at[j], dst_ref=dst, send_sem=ici_send.at[j], recv_sem=ici_recv.at[my_chip],
                device_id=(cx, cy, c), device_id_type=MESH)

        if nsteps == 1:
            acc[...] = _dot_tn(a_ref[...], b_ref[...])
        else:
            @pl.when(s == 0)
            def _():
                acc[...] = jnp.zeros_like(acc)

            acc[...] += _dot_tn(a_ref[...], b_ref[...])

        for kk in range(ncall):
            @pl.when((s == nsteps - 1) & (k == kk))
            def _():
                for j in range(NCHIP):
                    if other_step(j) == kk:
                        sb_other[j] = piece(j, 1 - c).astype(BF16)
                        to_sibling(j).start()
                for j in range(NCHIP):
                    if mine_step(j) == kk:
                        to_sibling(j).wait_recv()
                        sb_mine[j] = (piece(j, c) + land[j].astype(F32)).astype(BF16)
                        ci, loc, rem = to_owner(j)
                        pl.when(ci == my_chip)(loc.start)
                        pl.when(ci != my_chip)(rem.start)

        @pl.when((s == nsteps - 1) & (k == ncall - 1))
        def _():
            for j in range(NCHIP):
                to_sibling(j).wait_send()
                ci, loc, rem = to_owner(j)
                pl.when(ci == my_chip)(loc.wait)
                pl.when(ci != my_chip)(rem.wait_send)
            for src in range(NCHIP):
                @pl.when(my_chip != src)
                def _():
                    pltpu.make_async_remote_copy(
                        src_ref=recv_ref.at[src, slot], dst_ref=recv_ref.at[src, slot],
                        send_sem=ici_send.at[src], recv_sem=ici_recv.at[src],
                        device_id=(src // 2, src % 2, c), device_id_type=MESH).wait_recv()

    in_specs = [pl.BlockSpec((None, ts, M), (lambda k, s, me: (group(k, me), s, 0)) if na > 1 else (lambda k, s, me: (0, s, 0))),
                pl.BlockSpec((None, ts, N), (lambda k, s, me: (group(k, me), s, 0)) if nb > 1 else (lambda k, s, me: (0, s, 0)))]
    args = [me_arr, a, b]
    aliases = {}
    if prev is not None:
        in_specs.append(ANY)
        args.append(prev)
        aliases = {3: 0}
    return pl.pallas_call(
        body, name=name,
        grid_spec=pltpu.PrefetchScalarGridSpec(
            num_scalar_prefetch=1, grid=(ncall, nsteps), in_specs=in_specs, out_specs=ANY,
            scratch_shapes=[pltpu.VMEM((M, N), F32), pltpu.VMEM((NCHIP, mp, N), BF16), pltpu.VMEM((NCHIP, mp, N), BF16),
                            pltpu.VMEM((NCHIP, mp, N), BF16), pltpu.SemaphoreType.DMA((NCHIP,)),
                            pltpu.SemaphoreType.DMA((NCHIP,)), pltpu.SemaphoreType.DMA((NCHIP,)),
                            pltpu.SemaphoreType.DMA((NCHIP,))]),
        out_shape=jax.ShapeDtypeStruct((NCHIP, nslots, mp, N), BF16),
        input_output_aliases=aliases,
        compiler_params=_cparams("arbitrary", "arbitrary"),
    )(*args)


MIX_TS = 256
MIX_IN_TS = 512


def mix_in_fwd(x, vec, win_g, tag, ride=None):
    S = x.shape[0]
    ts = min(MIX_IN_TS, S)

    def body(x_ref, vec_ref, win_ref, hm_ref, proj_ref):
        h = _norm_mod(x_ref[...], vec_ref[3:4, :], vec_ref[1:2, :], vec_ref[0:1, :]).astype(BF16)
        hm_ref[...] = h
        for k in range(NDEV):
            proj_ref[k] = _dot(h, win_ref[k])

    return _call(
        body, ride, name=f"mix_in_fwd_{tag}",
        grid=(S // ts,),
        in_specs=[pl.BlockSpec((ts, D), lambda i: (i, 0)), pl.BlockSpec((8, D), lambda i: (0, 0)),
                  pl.BlockSpec((NDEV, D, PC), lambda i: (0, 0, 0))],
        out_specs=[pl.BlockSpec((ts, D), lambda i: (i, 0)),
                   pl.BlockSpec((NDEV, ts, PC), lambda i: (0, i, 0))],
        out_shape=[jax.ShapeDtypeStruct((S, D), BF16), jax.ShapeDtypeStruct((NDEV, S, PC), F32)],
        scratch_shapes=[], args=(x, vec, win_g))


def mix_in_bwd(dproj, x, dxo, vec, win_g, tag):
    S = x.shape[0]
    ts = min(MIX_IN_TS, S)

    def body(dp_ref, x_ref, dxo_ref, vec_ref, win_ref, dx_ref, acc_ref):
        @pl.when(pl.program_id(0) == 0)
        def _():
            acc_ref[...] = jnp.zeros_like(acc_ref)

        dh = jnp.zeros((ts, D), F32)
        for k in range(NDEV):
            dh = dh + _dot_nt(dp_ref[k], win_ref[k])
        dx, dshift, dscale, dgain = _norm_mod_bwd(dh, x_ref[...], vec_ref[3:4, :], vec_ref[1:2, :])
        dx_ref[...] = dx + dxo_ref[...]
        acc_ref[0:1, :] += dshift
        acc_ref[1:2, :] += dscale
        acc_ref[3:4, :] += dgain

    row = pl.BlockSpec((ts, D), lambda i: (i, 0))
    return pl.pallas_call(
        body, name=f"mix_in_bwd_{tag}",
        grid=(S // ts,),
        in_specs=[pl.BlockSpec((NDEV, ts, PC), lambda i: (0, i, 0)), row, row,
                  pl.BlockSpec((8, D), lambda i: (0, 0)),
                  pl.BlockSpec((NDEV, D, PC), lambda i: (0, 0, 0))],
        out_specs=[row, pl.BlockSpec((8, D), lambda i: (0, 0))],
        out_shape=[jax.ShapeDtypeStruct((S, D), F32), jax.ShapeDtypeStruct((8, D), F32)],
        compiler_params=_cparams("arbitrary"),
    )(dproj, x, dxo, vec, win_g)


SCAN_UNROLL = 4


def _shift_down(z, k, row):
    return jnp.where(row >= k, pltpu.roll(z, k, 0), 0.0)


def _shift_up(z, k, row, n):
    return jnp.where(row < n - k, pltpu.roll(z, n - k, 0), 0.0)


def _lru_gates(xc, lp_ref, wa_ref, wx_ref):
    xcb = xc.astype(BF16)
    ra = _sigmoid(_dot(xcb, wa_ref[...]) + lp_ref[5:6, :])
    ix = _sigmoid(_dot(xcb, wx_ref[...]) + lp_ref[6:7, :])
    lam = lp_ref[7:8, :]
    ls = jnp.minimum(lam, 0.0) - jnp.log(1.0 + jnp.exp(-jnp.abs(lam)))
    log_a = (RG_LRU_C * ls) * ra
    a = jnp.exp(log_a)
    mult = jnp.sqrt(-jnp.tanh(log_a) * (a * a + 1.0))
    return ra, ix, ls, a, mult


def _conv(x, lp_ref, row):
    return (lp_ref[4:5, :] + lp_ref[3:4, :] * x + lp_ref[2:3, :] * _shift_down(x, 1, row)
            + lp_ref[1:2, :] * _shift_down(x, 2, row) + lp_ref[0:1, :] * _shift_down(x, 3, row))


def lru_fwd(proj, lp, wa_t, wx_t, tag, ride=None):
    S = proj.shape[1]
    nblk = S // 8

    def body(x_ref, g_ref, lp_ref, wa_ref, wx_ref, y_ref, xc_ref, h_ref, a_s, b_s):
        x = x_ref[...]
        row = lax.broadcasted_iota(jnp.int32, x.shape, 0)
        xc = _conv(x, lp_ref, row)
        xc_ref[...] = xc
        ra, ix, ls, a, mult = _lru_gates(xc, lp_ref, wa_ref, wx_ref)
        a_s[...] = a
        b_s[...] = mult * (ix * xc)
        rowb = lax.broadcasted_iota(jnp.int32, (8, LC), 0)

        def step(i, carry):
            for q in range(SCAN_UNROLL):
                r0 = pl.multiple_of((i * SCAN_UNROLL + q) * 8, 8)
                A = a_s[pl.ds(r0, 8), :]
                B = b_s[pl.ds(r0, 8), :]
                for d in (1, 2, 4):
                    m = rowb >= d
                    As = jnp.where(m, pltpu.roll(A, d, 0), 1.0)
                    Bs = jnp.where(m, pltpu.roll(B, d, 0), 0.0)
                    B = A * Bs + B
                    A = A * As
                H = B + A * carry
                h_ref[pl.ds(r0, 8), :] = H
                carry = H[7:8, :]
            return carry

        lax.fori_loop(0, nblk // SCAN_UNROLL, step, jnp.zeros((1, LC), F32))
        y_ref[...] = h_ref[...] * _gelu(g_ref[...])

    col = pl.BlockSpec((S, LC), lambda c: (0, c))
    return _call(
        body, ride, name=f"lru_fwd_{tag}",
        grid=(LW // LC,),
        in_specs=[pl.BlockSpec((None, S, LC), lambda c: (c // 2, 0, c % 2)),
                  pl.BlockSpec((None, S, LC), lambda c: (2 + c // 2, 0, c % 2)),
                  pl.BlockSpec((8, LC), lambda c: (0, c)),
                  pl.BlockSpec((None, LC, LC), lambda c: (c, 0, 0)),
                  pl.BlockSpec((None, LC, LC), lambda c: (c, 0, 0))],
        out_specs=[col, col, col],
        out_shape=[jax.ShapeDtypeStruct((S, LW), F32)] * 3,
        scratch_shapes=[pltpu.VMEM((S, LC), F32), pltpu.VMEM((S, LC), F32)],
        args=(proj, proj, lp, wa_t, wx_t))


def lru_bwd(dy, proj, xc_all, hst, lp, wa_t, wx_t, tag):
    S = proj.shape[1]
    nblk = S // 8

    def body(dy_ref, x_ref, g_ref, xc_ref, h_ref, lp_ref, wa_ref, wx_ref,
             dx_ref, dg_ref, dlp_ref, dwa_ref, dwx_ref, c_s, l_s):
        xc = xc_ref[...]
        row = lax.broadcasted_iota(jnp.int32, xc.shape, 0)
        ra, ix, ls, a, mult = _lru_gates(xc, lp_ref, wa_ref, wx_ref)
        g = g_ref[...]
        dyv = dy_ref[...]
        h = h_ref[...]
        gelu_g, gelu_grad_g = _gelu_and_grad(g)
        dg_ref[...] = (dyv * h * gelu_grad_g).astype(BF16)
        c_s[...] = _shift_up(a, 1, row, S)
        l_s[...] = dyv * gelu_g
        rowb = lax.broadcasted_iota(jnp.int32, (8, LC), 0)

        def step(i, carry):
            for q in range(SCAN_UNROLL):
                r0 = pl.multiple_of((nblk - 1 - (i * SCAN_UNROLL + q)) * 8, 8)
                C = c_s[pl.ds(r0, 8), :]
                L = l_s[pl.ds(r0, 8), :]
                for d in (1, 2, 4):
                    m = rowb < 8 - d
                    Cs = jnp.where(m, pltpu.roll(C, 8 - d, 0), 1.0)
                    Ls = jnp.where(m, pltpu.roll(L, 8 - d, 0), 0.0)
                    L = C * Ls + L
                    C = C * Cs
                L = L + C * carry
                l_s[pl.ds(r0, 8), :] = L
                carry = L[0:1, :]
            return carry

        lax.fori_loop(0, nblk // SCAN_UNROLL, step, jnp.zeros((1, LC), F32))
        db = l_s[...]
        da = db * _shift_down(h, 1, row)
        ixc = ix * xc
        dmult = db * ixc
        dix = db * (mult * xc)
        dxc = db * (mult * ix)
        dlog_a = da * a - dmult * (a * a) / mult
        dra = dlog_a * (RG_LRU_C * ls)
        dls = _csum(dlog_a * ra) * RG_LRU_C
        lam = lp_ref[7:8, :]
        dlam = dls * _sigmoid(-lam)
        dpa = dra * ra * (1.0 - ra)
        dpx = dix * ix * (1.0 - ix)
        dpab = dpa.astype(BF16)
        dpxb = dpx.astype(BF16)
        xcb = xc.astype(BF16)
        dwa_ref[...] = _dot_tn(xcb, dpab)
        dwx_ref[...] = _dot_tn(xcb, dpxb)
        dxc = dxc + _dot_nt(dpab, wa_ref[...]) + _dot_nt(dpxb, wx_ref[...])
        x = x_ref[...]
        dlp_ref[0:1, :] = _csum(dxc * _shift_down(x, 3, row))
        dlp_ref[1:2, :] = _csum(dxc * _shift_down(x, 2, row))
        dlp_ref[2:3, :] = _csum(dxc * _shift_down(x, 1, row))
        dlp_ref[3:4, :] = _csum(dxc * x)
        dlp_ref[4:5, :] = _csum(dxc)
        dlp_ref[5:6, :] = _csum(dpa)
        dlp_ref[6:7, :] = _csum(dpx)
        dlp_ref[7:8, :] = dlam
        dx = (lp_ref[3:4, :] * dxc + lp_ref[2:3, :] * _shift_up(dxc, 1, row, S)
              + lp_ref[1:2, :] * _shift_up(dxc, 2, row, S) + lp_ref[0:1, :] * _shift_up(dxc, 3, row, S))
        dx_ref[...] = dx.astype(BF16)

    col = pl.BlockSpec((S, LC), lambda c: (0, c))
    pcol = pl.BlockSpec((None, S, LC), lambda c: (c // 2, 0, c % 2))
    return pl.pallas_call(
        body, name=f"lru_bwd_{tag}",
        grid=(LW // LC,),
        in_specs=[col, pcol, pl.BlockSpec((None, S, LC), lambda c: (2 + c // 2, 0, c % 2)), col, col,
                  pl.BlockSpec((8, LC), lambda c: (0, c)),
                  pl.BlockSpec((None, LC, LC), lambda c: (c, 0, 0)),
                  pl.BlockSpec((None, LC, LC), lambda c: (c, 0, 0))],
        out_specs=[pcol, pcol, pl.BlockSpec((8, LC), lambda c: (0, c)),
                   pl.BlockSpec((None, LC, LC), lambda c: (c, 0, 0)),
                   pl.BlockSpec((None, LC, LC), lambda c: (c, 0, 0))],
        out_shape=[jax.ShapeDtypeStruct((2, S, PC), BF16), jax.ShapeDtypeStruct((2, S, PC), BF16),
                   jax.ShapeDtypeStruct((8, LW), F32),
                   jax.ShapeDtypeStruct((LW // LC, LC, LC), F32), jax.ShapeDtypeStruct((LW // LC, LC, LC), F32)],
        scratch_shapes=[pltpu.VMEM((S, LC), F32), pltpu.VMEM((S, LC), F32)],
        compiler_params=_cparams("arbitrary"),
    )(dy, proj, proj, xc_all, hst, lp, wa_t, wx_t)


def _pair_stack(zp, low):
    return jnp.concatenate([jnp.where(low, zp, 0.0), jnp.where(low, 0.0, zp)], axis=0).astype(BF16)


def _spatial(w_ref, zc, low):
    return jnp.concatenate(
        [_dot(w_ref[:, 2 * p * CHUNK:2 * (p + 1) * CHUNK], _pair_stack(zc[:, p * PAIR:(p + 1) * PAIR], low))
         for p in range(GW // PAIR)], axis=1)


def _gmlp_fwd_parts(u, v, gp_ref, wcat_ref, bz_ref, pavg_ref, ts, with_grad=False):
    if with_grad:
        ug, ugrad = _gelu_and_grad(u)
        vg, vgrad = _gelu_and_grad(v)
    else:
        ug, vg, ugrad, vgrad = _gelu(u), _gelu(v), None, None
    pavg = pavg_ref[...]
    vc = vg - _seg_mean(vg, pavg)
    rs = lax.rsqrt(_seg_mean(vc * vc, pavg) + EPS)
    vhat = vc * rs
    vh = vhat * gp_ref[0:1, :]
    low = lax.broadcasted_iota(jnp.int32, (CHUNK, PAIR), 1) < HD
    zs = [_spatial(wcat_ref, vh[n * CHUNK:(n + 1) * CHUNK, :], low) + bz_ref[...] for n in range(ts // CHUNK)]
    z = jnp.concatenate(zs, axis=0) if len(zs) > 1 else zs[0]
    return ug, rs, vhat, vh, z, ugrad, vgrad


def mix_out_fwd(proj, ylru, x, vec, gp, wcat, bz, pavg, wout_g, tag, ride=None):
    S = x.shape[0]
    ts = min(MIX_TS, S)

    def body(u_ref, v_ref, yl_ref, x_ref, vec_ref, gp_ref, wcat_ref, bz_ref, pavg_ref, wout_ref,
             xo_ref, y_ref, fo_ref):
        u = jnp.concatenate([u_ref[0], u_ref[1]], axis=1)
        v = jnp.concatenate([v_ref[0], v_ref[1]], axis=1)
        ug, _, _, _, z, _, _ = _gmlp_fwd_parts(u, v, gp_ref, wcat_ref, bz_ref, pavg_ref, ts)
        n1 = _rms(yl_ref[...], gp_ref[1:2, :])
        n2 = _rms(ug * z, gp_ref[2:3, :])
        y = jnp.concatenate([n1, n2], axis=1).astype(BF16)
        y_ref[...] = y
        fo = jnp.zeros((ts, D), F32)
        for k in range(NDEV):
            fo = fo + _dot(y[:, k * OR:(k + 1) * OR], wout_ref[k])
        fo_ref[...] = fo.astype(BF16)
        xo_ref[...] = x_ref[...] + vec_ref[2:3, :] * fo

    row = pl.BlockSpec((ts, D), lambda i: (i, 0))
    full = lambda shp: pl.BlockSpec(shp, lambda i: tuple(0 for _ in shp))
    return _call(
        body, ride, name=f"mix_out_fwd_{tag}",
        grid=(S // ts,),
        in_specs=[pl.BlockSpec((2, ts, PC), lambda i: (2, i, 0)), pl.BlockSpec((2, ts, PC), lambda i: (3, i, 0)),
                  pl.BlockSpec((ts, LW), lambda i: (i, 0)), row, full((8, D)), full((8, GW)),
                  full((CHUNK, HEADS * CHUNK)), full((CHUNK, GW)), full((PAIR, PAIR)),
                  pl.BlockSpec((NDEV, OR, D), lambda i: (0, 0, 0))],
        out_specs=[row, row, row],
        out_shape=[jax.ShapeDtypeStruct((S, D), F32), jax.ShapeDtypeStruct((S, D), BF16),
                   jax.ShapeDtypeStruct((S, D), BF16)],
        scratch_shapes=[], args=(proj, proj, ylru, x, vec, gp, wcat, bz, pavg, wout_g))


def mix_out_bwd(dxo, proj, ylru, fo, vec, gp, wcat, wcat_t, bz, pavg, wout_t, tag):
    S = dxo.shape[0]
    ts = min(MIX_TS, S)

    def body(dxo_ref, u_ref, v_ref, yl_ref, fo_ref, vec_ref, gp_ref, wcat_ref, wcatt_ref, bz_ref, pavg_ref,
             wout_ref, dyo_ref, dyl_ref, duv_ref, acc_ref, dgp_ref, dwm_ref, dbz_ref):
        @pl.when(pl.program_id(0) == 0)
        def _():
            acc_ref[...] = jnp.zeros_like(acc_ref)
            dgp_ref[...] = jnp.zeros_like(dgp_ref)
            dwm_ref[...] = jnp.zeros_like(dwm_ref)
            dbz_ref[...] = jnp.zeros_like(dbz_ref)

        dxo_v = dxo_ref[...]
        acc_ref[2:3, :] += _csum(dxo_v * fo_ref[...].astype(F32))
        dyo = (vec_ref[2:3, :] * dxo_v).astype(BF16)
        dyo_ref[...] = dyo
        dn = _dot(dyo, wout_ref[...])
        dn1, dn2 = dn[:, :LW], dn[:, LW:]
        dyl, dg1 = _rms_bwd(dn1, yl_ref[...], gp_ref[1:2, :])
        dyl_ref[...] = dyl
        u = jnp.concatenate([u_ref[0], u_ref[1]], axis=1)
        v = jnp.concatenate([v_ref[0], v_ref[1]], axis=1)
        ug, rs, vhat, vh, z, ugrad, vgrad = _gmlp_fwd_parts(u, v, gp_ref, wcat_ref, bz_ref, pavg_ref, ts,
                                                            with_grad=True)
        dyg, dg2 = _rms_bwd(dn2, ug * z, gp_ref[2:3, :])
        du = (dyg * z) * ugrad
        dz = dyg * ug
        low = lax.broadcasted_iota(jnp.int32, (CHUNK, PAIR), 1) < HD
        vhb = vh.astype(BF16)
        dvhs = []
        dbz = jnp.zeros((CHUNK, GW), F32)
        dwm = [jnp.zeros((2 * CHUNK, CHUNK), F32) for _ in range(GW // PAIR)]
        for n in range(ts // CHUNK):
            dzc = dz[n * CHUNK:(n + 1) * CHUNK, :]
            dbz = dbz + dzc
            for p in range(GW // PAIR):
                stack = _pair_stack(dzc[:, p * PAIR:(p + 1) * PAIR], low)
                dwm[p] = dwm[p] + _dot_nt(stack, vhb[n * CHUNK:(n + 1) * CHUNK, p * PAIR:(p + 1) * PAIR])
            dvhs.append(_spatial(wcatt_ref, dzc, low))
        dbz_ref[...] += dbz
        for p in range(GW // PAIR):
            dwm_ref[2 * p * CHUNK:2 * (p + 1) * CHUNK, :] += dwm[p]
        dvh = jnp.concatenate(dvhs, axis=0) if len(dvhs) > 1 else dvhs[0]
        pavg = pavg_ref[...]
        dvn = _csum(dvh * vhat)
        dvhat = dvh * gp_ref[0:1, :]
        dvg = rs * (dvhat - _seg_mean(dvhat, pavg) - vhat * _seg_mean(dvhat * vhat, pavg))
        dv = dvg * vgrad
        duv_ref[0] = du[:, :PC].astype(BF16)
        duv_ref[1] = du[:, PC:].astype(BF16)
        duv_ref[2] = dv[:, :PC].astype(BF16)
        duv_ref[3] = dv[:, PC:].astype(BF16)
        dgp_ref[0:1, :] += dvn
        dgp_ref[1:2, :] += dg1
        dgp_ref[2:3, :] += dg2

    row = pl.BlockSpec((ts, D), lambda i: (i, 0))
    full = lambda shp: pl.BlockSpec(shp, lambda i: tuple(0 for _ in shp))
    return pl.pallas_call(
        body, name=f"mix_out_bwd_{tag}",
        grid=(S // ts,),
        in_specs=[row, pl.BlockSpec((2, ts, PC), lambda i: (2, i, 0)), pl.BlockSpec((2, ts, PC), lambda i: (3, i, 0)),
                  pl.BlockSpec((ts, LW), lambda i: (i, 0)), row, full((8, D)), full((8, GW)),
                  full((CHUNK, HEADS * CHUNK)), full((CHUNK, HEADS * CHUNK)), full((CHUNK, GW)), full((PAIR, PAIR)),
                  full((D, D))],
        out_specs=[row, pl.BlockSpec((ts, LW), lambda i: (i, 0)), pl.BlockSpec((4, ts, PC), lambda i: (0, i, 0)),
                   full((8, D)), full((8, GW)), full((HEADS * CHUNK, CHUNK)), full((CHUNK, GW))],
        out_shape=[jax.ShapeDtypeStruct((S, D), BF16), jax.ShapeDtypeStruct((S, LW), F32),
                   jax.ShapeDtypeStruct((4, S, PC), BF16), jax.ShapeDtypeStruct((8, D), F32),
                   jax.ShapeDtypeStruct((8, GW), F32), jax.ShapeDtypeStruct((HEADS * CHUNK, CHUNK), F32),
                   jax.ShapeDtypeStruct((CHUNK, GW), F32)],
        compiler_params=_cparams("arbitrary"),
    )(dxo, proj, proj, ylru, fo, vec, gp, wcat, wcat_t, bz, pavg, wout_t)


def final_loss(x, target, gain):
    S = x.shape[0]
    ts = min(512, S)

    def body(x_ref, t_ref, g_ref, loss_ref, dx_ref, dg_ref):
        @pl.when(pl.program_id(0) == 0)
        def _():
            loss_ref[...] = jnp.zeros_like(loss_ref)
            dg_ref[...] = jnp.zeros_like(dg_ref)

        xv = x_ref[...]
        gain_v = g_ref[0:1, :]
        rstd = lax.rsqrt(_rmean(xv * xv) + EPS)
        xhat = xv * rstd
        err = xhat * gain_v - t_ref[...]
        loss_ref[...] += 0.5 * _csum(_rmean(err * err))
        dy = err * (1.0 / D)
        dg_ref[0:1, :] += _csum(dy * xhat)
        dxhat = dy * gain_v
        dx_ref[...] = rstd * (dxhat - xhat * _rmean(dxhat * xhat))

    row = pl.BlockSpec((ts, D), lambda i: (i, 0))
    return pl.pallas_call(
        body, name="final_loss",
        grid=(S // ts,),
        in_specs=[row, row, pl.BlockSpec((8, D), lambda i: (0, 0))],
        out_specs=[pl.BlockSpec((8, 128), lambda i: (0, 0)), row, pl.BlockSpec((8, D), lambda i: (0, 0))],
        out_shape=[jax.ShapeDtypeStruct((8, 128), F32), jax.ShapeDtypeStruct((S, D), F32),
                   jax.ShapeDtypeStruct((8, D), F32)],
        compiler_params=_cparams("arbitrary"),
    )(x, target, gain)


def _vec(mod_l, j, gain):
    return jnp.concatenate([mod_l[3 * j:3 * j + 3], gain[None, :], jnp.zeros((4, D), F32)], axis=0)


def _block_diag_tiles(w):
    w4 = w.reshape(LW // LC, 2, HD, HD)
    eye2 = jnp.eye(2, dtype=w.dtype)
    return (w4[:, :, :, None, :] * eye2[None, :, None, :, None]).reshape(LW // LC, LC, LC).astype(BF16)


def _block_diag_extract(dw):
    d5 = dw.reshape(LW // LC, 2, HD, 2, HD)
    return jnp.einsum('cihkj,ik->cihj', d5, jnp.eye(2, dtype=dw.dtype)).reshape(HEADS, HD, HD)


def _layer_params(l, p, conv_w_full):
    lp = jnp.concatenate([conv_w_full[l], p['conv_b'][l][None], p['gate_a_b'][l].reshape(1, LW),
                          p['gate_x_b'][l].reshape(1, LW), p['lru_lambda'][l][None]], axis=0)
    gp = jnp.concatenate([p['v_norm'][l][None], p['lru_out_norm'][l][None], p['gmlp_out_norm'][l][None],
                          jnp.zeros((5, GW), F32)], axis=0)
    ws = p['spatial_w'][l] * jnp.tril(jnp.ones((CHUNK, CHUNK), F32))
    wcat = ws.transpose(1, 0, 2).reshape(CHUNK, HEADS * CHUNK).astype(BF16)
    wcat_t = ws.transpose(2, 0, 1).reshape(CHUNK, HEADS * CHUNK).astype(BF16)
    bz = jnp.repeat(p['spatial_b'][l].T, HD, axis=1)
    return dict(lp=lp, gp=gp, wcat=wcat, wcat_t=wcat_t, bz=bz,
                wa_t=_block_diag_tiles(p['gate_a_w'][l]), wx_t=_block_diag_tiles(p['gate_x_w'][l]))


def _pavg():
    return jnp.kron(jnp.eye(2, dtype=F32), jnp.full((HD, HD), 1.0 / HD, F32)).astype(BF16)


GATHER_RIDES = {
    ('ffn_a', 0): [('w_in', 0), ('gu', DEPTH)],
    ('mix_in', 0): [('w_out', 0)],
    ('lru', 0): [('down', DEPTH)],
    ('mix_out', 0): [('down', 1)],
    ('ffn_b', 0): [('gu', 1), ('w_in', 1)],
    ('ffn_a', 1): [('gu', DEPTH + 1), ('w_out', 1)],
    ('mix_in', 1): [('down', DEPTH + 1)],
}


def local_fwd_bwd(me_arr, x, target, mod, p, loc, gathered, conv_w_full):
    pavg = _pavg()
    g = dict(gathered)

    def ride(call, l):
        todo = GATHER_RIDES.get((call, l))
        return None if todo is None else (todo, GatherRide([(loc[kind], slot) for kind, slot in todo]))

    def run(fn, call, l, *args):
        r = ride(call, l)
        outs, got = fn(*args, ride=None if r is None else r[1])
        if r is not None:
            g.update(dict(zip(r[0], got)))
        return outs

    saved = []
    h = x
    for l in range(DEPTH):
        q = _layer_params(l, p, conv_w_full)
        v1 = _vec(mod[l], 0, p['ffn1_norm'][l])
        vm = _vec(mod[l], 1, p['mix_norm'][l])
        v2 = _vec(mod[l], 2, p['ffn2_norm'][l])
        x0 = h
        x1, h1, gu1, f1 = run(ffn_fwd, 'ffn_a', l, x0, v1, g['gu', l], g['down', l], f"a{l}")
        hm, proj = run(mix_in_fwd, 'mix_in', l, x1, vm, g['w_in', l], f"{l}")
        ylru, xc, hst = run(lru_fwd, 'lru', l, proj, q['lp'], q['wa_t'], q['wx_t'], f"{l}")
        x2, y, fo = run(mix_out_fwd, 'mix_out', l, proj, ylru, x1, vm, q['gp'], q['wcat'], q['bz'], pavg,
                        g['w_out', l], f"{l}")
        x3, h2, gu2, f2 = run(ffn_fwd, 'ffn_b', l, x2, v2, g['gu', DEPTH + l], g['down', DEPTH + l], f"b{l}")
        saved.append(dict(q=q, v1=v1, vm=vm, v2=v2, x0=x0, x1=x1, x2=x2, h1=h1, gu1=gu1, f1=f1, hm=hm, proj=proj,
                          ylru=ylru, xc=xc, hst=hst, y=y, fo=fo, h2=h2, gu2=gu2, f2=f2))
        h = x3
    fin = jnp.concatenate([p['final_norm'][None], jnp.zeros((7, D), F32)], axis=0)
    loss8, dx, dfin = final_loss(h, target, fin)
    loss = loss8[0, 0]

    big = dict(gu=None, down=None, w_in=None, w_out=None)
    small = {k: [None] * DEPTH for k in ('ffn1_norm', 'mix_norm', 'ffn2_norm', 'conv_w', 'conv_b', 'gate_a_w',
                                         'gate_a_b', 'gate_x_w', 'gate_x_b', 'lru_lambda', 'v_norm', 'spatial_w',
                                         'spatial_b', 'lru_out_norm', 'gmlp_out_norm')}
    dmod = [None] * DEPTH
    tril = jnp.tril(jnp.ones((CHUNK, CHUNK), F32))
    for l in reversed(range(DEPTH)):
        sv = saved[l]
        q = sv['q']
        dx2, dgu, a, df, acc2 = ffn_bwd(dx, sv['x2'], sv['gu2'], sv['f2'], sv['v2'],
                                        g['gu', DEPTH + l], g['down', DEPTH + l], f"b{l}")
        big['gu'] = tn_matmul_scatter(me_arr, dgu, sv['h2'][None], DEPTH + l, 2 * DEPTH, big['gu'], f"dw_gu_b{l}")
        big['down'] = tn_matmul_scatter(me_arr, a, df[None], DEPTH + l, 2 * DEPTH, big['down'], f"dw_down_b{l}", split=2)
        dyo, dylru, duv, accmo, dgp, dwm, dbz = mix_out_bwd(dx2, sv['proj'], sv['ylru'], sv['fo'], sv['vm'], q['gp'],
                                                             q['wcat'], q['wcat_t'], q['bz'], pavg,
                                                             g['w_out', l].reshape(D, D).T, f"{l}")
        big['w_out'] = tn_matmul_scatter(me_arr, sv['y'][None], dyo[None], l, DEPTH, big['w_out'], f"dw_out_{l}",
                                         split=NDEV)
        dxl, dgl, dlp, dwa, dwx = lru_bwd(dylru, sv['proj'], sv['xc'], sv['hst'], q['lp'], q['wa_t'], q['wx_t'], f"{l}")
        dproj = jnp.concatenate([dxl, dgl, duv], axis=0)
        dx1, accmi = mix_in_bwd(dproj, sv['x1'], dx2, sv['vm'], g['w_in', l], f"{l}")
        big['w_in'] = tn_matmul_scatter(me_arr, sv['hm'][None], dproj, l, DEPTH, big['w_in'], f"dw_in_{l}")
        dx0, dgu, a, df, acc1 = ffn_bwd(dx1, sv['x0'], sv['gu1'], sv['f1'], sv['v1'],
                                        g['gu', l], g['down', l], f"a{l}")
        big['gu'] = tn_matmul_scatter(me_arr, dgu, sv['h1'][None], l, 2 * DEPTH, big['gu'], f"dw_gu_a{l}")
        big['down'] = tn_matmul_scatter(me_arr, a, df[None], l, 2 * DEPTH, big['down'], f"dw_down_a{l}", split=2)
        dx = dx0
        dmod[l] = jnp.concatenate([acc1[0:3], accmi[0:2], accmo[2:3], acc2[0:3]], axis=0)
        small['ffn1_norm'][l] = acc1[3]
        small['mix_norm'][l] = accmi[3]
        small['ffn2_norm'][l] = acc2[3]
        small['conv_w'][l] = dlp[0:4]
        small['conv_b'][l] = dlp[4]
        small['gate_a_b'][l] = dlp[5].reshape(HEADS, HD)
        small['gate_x_b'][l] = dlp[6].reshape(HEADS, HD)
        small['lru_lambda'][l] = dlp[7]
        small['gate_a_w'][l] = _block_diag_extract(dwa)
        small['gate_x_w'][l] = _block_diag_extract(dwx)
        small['v_norm'][l] = dgp[0]
        small['lru_out_norm'][l] = dgp[1]
        small['gmlp_out_norm'][l] = dgp[2]
        small['spatial_w'][l] = dwm.reshape(HEADS, CHUNK, CHUNK) * tril
        small['spatial_b'][l] = dbz.reshape(CHUNK, HEADS, HD).sum(-1).T
    small = {k: jnp.stack(v) for k, v in small.items()}
    small['final_norm'] = dfin[0]
    return loss, dx, big, small, jnp.stack(dmod)


def ada_fwd(c_all, w_ada, b_loc):
    def body(c_ref, w_ref, b_ref, mod_ref, sc_ref):
        cv = c_ref[...]
        sc = cv * _sigmoid(cv)
        sc_ref[...] = sc
        mod_ref[...] = _dot3(sc, w_ref[...]) + b_ref[...]

    return pl.pallas_call(
        body, name="ada_fwd",
        grid=(DEPTH,),
        in_specs=[pl.BlockSpec((NDEV, D), lambda l: (0, 0)), pl.BlockSpec((None, D, AC), lambda l: (l, 0, 0)),
                  pl.BlockSpec((None, 1, AC), lambda l: (l, 0, 0))],
        out_specs=[pl.BlockSpec((None, NDEV, AC), lambda l: (l, 0, 0)), pl.BlockSpec((NDEV, D), lambda l: (0, 0))],
        out_shape=[jax.ShapeDtypeStruct((DEPTH, NDEV, AC), F32), jax.ShapeDtypeStruct((NDEV, D), F32)],
        compiler_params=_cparams("arbitrary"),
    )(c_all, w_ada, b_loc)


def ada_bwd(sc_t, dmod_cols):
    def body(sc_ref, dm_ref, g_ref):
        sc = sc_ref[...]
        dm = dm_ref[...]
        acc = sc[:, 0:1] * dm[0:1, :]
        for b in range(1, NDEV):
            acc = acc + sc[:, b:b + 1] * dm[b:b + 1, :]
        g_ref[...] = acc

    return pl.pallas_call(
        body, name="ada_bwd",
        grid=(DEPTH,),
        in_specs=[pl.BlockSpec((D, NDEV), lambda l: (0, 0)), pl.BlockSpec((None, NDEV, AC), lambda l: (l, 0, 0))],
        out_specs=pl.BlockSpec((None, None, D, AC), lambda l: (0, l, 0, 0)),
        out_shape=jax.ShapeDtypeStruct((1, DEPTH, D, AC), F32),
        compiler_params=_cparams("arbitrary"),
    )(sc_t, dmod_cols)


def _row_tile(rows, cols):
    if rows * cols <= 512 * 1024:
        return rows
    for tr in (512, 384, 352, 256, 128, 64, 32, 16, 8):
        if rows % tr == 0:
            return tr
    return rows


def adamw(gparts, slot0, w, m, v, name):
    P, _, R, C = gparts.shape
    L = w.shape[0]
    tr = _row_tile(R, C)

    def body(g_ref, w_ref, m_ref, v_ref, go_ref, do_ref, mo_ref, vo_ref):
        g = g_ref[0].astype(F32)
        for p in range(1, P):
            g = g + g_ref[p].astype(F32)
        go_ref[...] = g
        mn = ADAM_B1 * m_ref[...] + (1.0 - ADAM_B1) * g
        vn = ADAM_B2 * v_ref[...] + (1.0 - ADAM_B2) * (g * g)
        mo_ref[...] = mn
        vo_ref[...] = vn
        m_hat = mn / (1.0 - ADAM_B1 ** ADAM_STEP)
        v_hat = vn / (1.0 - ADAM_B2 ** ADAM_STEP)
        do_ref[...] = -ADAM_LR * (m_hat / (jnp.sqrt(v_hat) + ADAM_EPS) + ADAM_WD * w_ref[...])

    blk = pl.BlockSpec((None, tr, C), lambda l, i: (l, i, 0))
    return pl.pallas_call(
        body, name=name,
        grid=(L, R // tr),
        in_specs=[pl.BlockSpec((P, None, tr, C), lambda l, i: (0, slot0 + l, i, 0)), blk, blk, blk],
        out_specs=[blk, blk, blk, blk],
        out_shape=[jax.ShapeDtypeStruct((L, R, C), F32)] * 4,
        compiler_params=_cparams("arbitrary", "arbitrary"),
    )(gparts, w, m, v)


def sum_parts(parts):
    P, R, C = parts.shape

    def body(p_ref, o_ref):
        acc = p_ref[0]
        for p in range(1, P):
            acc = acc + p_ref[p]
        o_ref[...] = acc

    return pl.pallas_call(
        body, name="sum_parts",
        in_specs=[pl.BlockSpec(memory_space=pltpu.VMEM)],
        out_specs=pl.BlockSpec(memory_space=pltpu.VMEM),
        out_shape=jax.ShapeDtypeStruct((R, C), F32),
    )(parts)


WEIGHTS = ['w_ada', 'b_ada', 'ffn1_norm', 'ffn1_w_gu', 'ffn1_w_down', 'mix_norm', 'w_in', 'conv_w', 'conv_b',
           'gate_a_w', 'gate_a_b', 'gate_x_w', 'gate_x_b', 'lru_lambda', 'v_norm', 'spatial_w', 'spatial_b',
           'lru_out_norm', 'gmlp_out_norm', 'w_out', 'ffn2_norm', 'ffn2_w_gu', 'ffn2_w_down', 'final_norm']
PACKED = ['b_ada', 'ffn1_norm', 'mix_norm', 'conv_b', 'gate_a_w', 'gate_a_b', 'gate_x_w', 'gate_x_b', 'lru_lambda',
          'v_norm', 'spatial_w', 'spatial_b', 'lru_out_norm', 'gmlp_out_norm', 'ffn2_norm', 'final_norm', 'conv_w']
PACK_LANES = 128
PACK_ROW_ALIGN = 8 * NDEV


PACK_TAIL = 8


def _pack_rows(shapes):
    used = 0
    for k in PACKED:
        size = 1
        for s in shapes[k]:
            size *= s
        used += size // PACK_LANES
    return used, -(-(used + PACK_TAIL) // PACK_ROW_ALIGN) * PACK_ROW_ALIGN


def _pack(d, tail=None):
    parts = [d[k].reshape(-1, PACK_LANES).astype(F32) for k in PACKED]
    used, rows = _pack_rows({k: d[k].shape for k in PACKED})
    parts.append(jnp.zeros((PACK_TAIL, PACK_LANES), F32) if tail is None else tail)
    return jnp.concatenate(parts + [jnp.zeros((rows - used - PACK_TAIL, PACK_LANES), F32)], axis=0)


def _unpack(buf, shapes):
    out, off = {}, 0
    for k in PACKED:
        size = 1
        for s in shapes[k]:
            size *= s
        nrows = size // PACK_LANES
        out[k] = buf[off:off + nrows].reshape(shapes[k])
        off += nrows
    return out


def kernel(x, c, w_ada, b_ada, ffn1_norm, ffn1_w_gu, ffn1_w_down, mix_norm, w_in, conv_w, conv_b, gate_a_w, gate_a_b, gate_x_w, gate_x_b, lru_lambda, v_norm, spatial_w, spatial_b, lru_out_norm, gmlp_out_norm, w_out, ffn2_norm, ffn2_w_gu, ffn2_w_down, final_norm, loss_target, m_w_ada, m_b_ada, m_ffn1_norm, m_ffn1_w_gu, m_ffn1_w_down, m_mix_norm, m_w_in, m_conv_w, m_conv_b, m_gate_a_w, m_gate_a_b, m_gate_x_w, m_gate_x_b, m_lru_lambda, m_v_norm, m_spatial_w, m_spatial_b, m_lru_out_norm, m_gmlp_out_norm, m_w_out, m_ffn2_norm, m_ffn2_w_gu, m_ffn2_w_down, m_final_norm, v_w_ada, v_b_ada, v_ffn1_norm, v_ffn1_w_gu, v_ffn1_w_down, v_mix_norm, v_w_in, v_conv_w, v_conv_b, v_gate_a_w, v_gate_a_b, v_gate_x_w, v_gate_x_b, v_lru_lambda, v_v_norm, v_spatial_w, v_spatial_b, v_lru_out_norm, v_gmlp_out_norm, v_w_out, v_ffn2_norm, v_ffn2_w_gu, v_ffn2_w_down, v_final_norm):
    w = dict(w_ada=w_ada, b_ada=b_ada, ffn1_norm=ffn1_norm, ffn1_w_gu=ffn1_w_gu, ffn1_w_down=ffn1_w_down, mix_norm=mix_norm, w_in=w_in, conv_w=conv_w, conv_b=conv_b, gate_a_w=gate_a_w, gate_a_b=gate_a_b, gate_x_w=gate_x_w, gate_x_b=gate_x_b, lru_lambda=lru_lambda, v_norm=v_norm, spatial_w=spatial_w, spatial_b=spatial_b, lru_out_norm=lru_out_norm, gmlp_out_norm=gmlp_out_norm, w_out=w_out, ffn2_norm=ffn2_norm, ffn2_w_gu=ffn2_w_gu, ffn2_w_down=ffn2_w_down, final_norm=final_norm)
    m = dict(w_ada=m_w_ada, b_ada=m_b_ada, ffn1_norm=m_ffn1_norm, ffn1_w_gu=m_ffn1_w_gu, ffn1_w_down=m_ffn1_w_down, mix_norm=m_mix_norm, w_in=m_w_in, conv_w=m_conv_w, conv_b=m_conv_b, gate_a_w=m_gate_a_w, gate_a_b=m_gate_a_b, gate_x_w=m_gate_x_w, gate_x_b=m_gate_x_b, lru_lambda=m_lru_lambda, v_norm=m_v_norm, spatial_w=m_spatial_w, spatial_b=m_spatial_b, lru_out_norm=m_lru_out_norm, gmlp_out_norm=m_gmlp_out_norm, w_out=m_w_out, ffn2_norm=m_ffn2_norm, ffn2_w_gu=m_ffn2_w_gu, ffn2_w_down=m_ffn2_w_down, final_norm=m_final_norm)
    v = dict(w_ada=v_w_ada, b_ada=v_b_ada, ffn1_norm=v_ffn1_norm, ffn1_w_gu=v_ffn1_w_gu, ffn1_w_down=v_ffn1_w_down, mix_norm=v_mix_norm, w_in=v_w_in, conv_w=v_conv_w, conv_b=v_conv_b, gate_a_w=v_gate_a_w, gate_a_b=v_gate_a_b, gate_x_w=v_gate_x_w, gate_x_b=v_gate_x_b, lru_lambda=v_lru_lambda, v_norm=v_v_norm, spatial_w=v_spatial_w, spatial_b=v_spatial_b, lru_out_norm=v_lru_out_norm, gmlp_out_norm=v_gmlp_out_norm, w_out=v_w_out, ffn2_norm=v_ffn2_norm, ffn2_w_gu=v_ffn2_w_gu, ffn2_w_down=v_ffn2_w_down, final_norm=v_final_norm)
    me = 4 * lax.axis_index("x") + 2 * lax.axis_index("y") + lax.axis_index("c")

    loc = dict(gu=jnp.concatenate([ffn1_w_gu, ffn2_w_gu], axis=0).astype(BF16),
               down=jnp.concatenate([ffn1_w_down, ffn2_w_down], axis=0).astype(BF16),
               w_in=w_in.astype(BF16), w_out=w_out.astype(BF16))
    c_g, conv_g, gu0, down0 = all_gather([(c, None), (conv_w, None), (loc['gu'], 0), (loc['down'], 0)], "gather_first")
    conv_w_full = conv_g.transpose(1, 2, 0, 3).reshape(DEPTH, CONV_WIDTH, LW)

    b_loc = lax.dynamic_slice(b_ada, (0, me * AC), (DEPTH, AC)).reshape(DEPTH, 1, AC)
    mod_cols, sc_all = ada_fwd(c_g.reshape(NDEV, D), w_ada, b_loc)
    (mod_rows,) = all_to_all([mod_cols.transpose(1, 0, 2)], "scatter_mod")
    mod = mod_rows.transpose(1, 0, 2).reshape(DEPTH, NMOD, D)

    small_w = {k: w[k] for k in PACKED if k != 'conv_w'}
    me_arr = jnp.reshape(me, (1,)).astype(jnp.int32)
    loss_loc, dx, big, small_g, dmod = local_fwd_bwd(me_arr, x[0], loss_target[0], mod, small_w, loc,
                                                     {('gu', 0): gu0, ('down', 0): down0}, conv_w_full)

    small_g['b_ada'] = dmod.reshape(DEPTH, NMOD * D)
    first = (lax.broadcasted_iota(jnp.int32, (PACK_TAIL, PACK_LANES), 0)
             + lax.broadcasted_iota(jnp.int32, (PACK_TAIL, PACK_LANES), 1)) == 0
    gpack = _pack(small_g, jnp.where(first, loss_loc, 0.0))
    rows = gpack.shape[0]
    dmod_out = dmod.reshape(DEPTH, NDEV, AC).transpose(1, 0, 2)
    dmod_r, pack_r = all_to_all([dmod_out, gpack.reshape(NDEV, rows // NDEV, PACK_LANES)], "scatter_grads")
    (gsum_g,) = all_gather([(sum_parts(pack_r), None)], "gather_small_grads")
    gsum = gsum_g.reshape(1, 1, rows, PACK_LANES)
    loss = gsum[0, 0, _pack_rows({k: small_g[k].shape for k in PACKED})[0], 0]

    res = {}
    t = lambda a: a.transpose(0, 2, 1)
    gu_t = big['gu']
    res['ffn1_w_gu'] = tuple(t(r) for r in adamw(gu_t, 0, t(w['ffn1_w_gu']), t(m['ffn1_w_gu']), t(v['ffn1_w_gu']),
                                                 "adamw_gu_a"))
    res['ffn2_w_gu'] = tuple(t(r) for r in adamw(gu_t, DEPTH, t(w['ffn2_w_gu']), t(m['ffn2_w_gu']),
                                                 t(v['ffn2_w_gu']), "adamw_gu_b"))
    res['ffn1_w_down'] = adamw(big['down'], 0, w['ffn1_w_down'], m['ffn1_w_down'], v['ffn1_w_down'], "adamw_down_a")
    res['ffn2_w_down'] = adamw(big['down'], DEPTH, w['ffn2_w_down'], m['ffn2_w_down'], v['ffn2_w_down'], "adamw_down_b")
    res['w_in'] = adamw(big['w_in'], 0, w['w_in'], m['w_in'], v['w_in'], "adamw_w_in")
    res['w_out'] = adamw(big['w_out'], 0, w['w_out'], m['w_out'], v['w_out'], "adamw_w_out")
    g_ada = ada_bwd(sc_all.T, dmod_r.transpose(1, 0, 2))
    res['w_ada'] = adamw(g_ada, 0, w['w_ada'], m['w_ada'], v['w_ada'], "adamw_w_ada")
    shapes = {k: w[k].shape for k in PACKED}
    shapes['conv_w'] = (DEPTH, CONV_WIDTH, LW)
    dummy = jnp.zeros(shapes['conv_w'], F32)
    packs = adamw(gsum, 0, _pack({**small_w, 'conv_w': dummy})[None], _pack({**{k: m[k] for k in small_w}, 'conv_w': dummy})[None],
                  _pack({**{k: v[k] for k in small_w}, 'conv_w': dummy})[None], "adamw_small")
    unpacked = [_unpack(b[0], shapes) for b in packs]
    for k in small_w:
        res[k] = tuple(u[k] for u in unpacked)
    gconv = lax.dynamic_slice(unpacked[0]['conv_w'], (0, 0, me * (LW // NDEV)), (DEPTH, CONV_WIDTH, LW // NDEV))
    cshape = (1, DEPTH * CONV_WIDTH, LW // NDEV)
    rc = adamw(gconv.reshape((1,) + cshape), 0, conv_w.reshape(cshape), m['conv_w'].reshape(cshape),
               v['conv_w'].reshape(cshape), "adamw_conv_w")
    res['conv_w'] = tuple(r.reshape(conv_w.shape) for r in rc)

    return (loss, dx[None], *[res[k][0] for k in WEIGHTS], *[res[k][1] for k in WEIGHTS],
            *[res[k][2] for k in WEIGHTS], *[res[k][3] for k in WEIGHTS])
```

```python
import jax
import jax.numpy as jnp
from jax import lax
from jax.experimental import pallas as pl
from jax.experimental.pallas import tpu as pltpu

F32 = jnp.float32
BF16 = jnp.bfloat16

NDEV = 8
DEPTH = 2
D = 1024
DFF = 2816
FC = 2 * DFF // NDEV
NCHUNK = DFF // FC
DR = DFF // NDEV
LW = 512
GW = 512
HD = 64
HEADS = 8
CHUNK = 128
PC = 2 * (LW + GW) // NDEV
OR = D // NDEV
NMOD = 9
AC = NMOD * D // NDEV
LC = 128
EPS = 1e-6
RG_LRU_C = 8.0
CONV_WIDTH = 4

ADAM_LR = 0.001
ADAM_B1 = 0.9
ADAM_B2 = 0.999
ADAM_EPS = 1e-08
ADAM_WD = 0.01
ADAM_STEP = 10

VMEM_LIMIT_BYTES = 60 * 1024 * 1024
MESH = pl.DeviceIdType.MESH
ANY = pl.BlockSpec(memory_space=pl.ANY)


def _cparams(*sem):
    return pltpu.CompilerParams(dimension_semantics=tuple(sem) if sem else None,
                                vmem_limit_bytes=VMEM_LIMIT_BYTES)


def _dot(a, b):
    return jnp.dot(a, b, preferred_element_type=F32)


def _dot_nt(a, b):
    return lax.dot_general(a, b, (((1,), (1,)), ((), ())), preferred_element_type=F32)


def _dot_tn(a, b):
    return lax.dot_general(a, b, (((0,), (0,)), ((), ())), preferred_element_type=F32)


def _split(a):
    hi = a.astype(BF16)
    lo = (a - hi.astype(F32)).astype(BF16)
    return hi, lo


def _dot3(a, b):
    ah, al = _split(a)
    bh, bl = _split(b)
    return _dot(ah, bh) + (_dot(ah, bl) + _dot(al, bh))


def _csum(a):
    return jnp.sum(a, axis=0, keepdims=True)


def _rmean(a):
    return jnp.mean(a, axis=-1, keepdims=True)


def _sigmoid(a):
    return 1.0 / (1.0 + jnp.exp(-a))


_GELU_K = 0.7978845608028654
_GELU_C = 0.044715


def _gelu(a):
    return 0.5 * a * (1.0 + jnp.tanh(_GELU_K * (a + _GELU_C * a * a * a)))


def _gelu_and_grad(a):
    a2 = a * a
    t = jnp.tanh(_GELU_K * (a + _GELU_C * a2 * a))
    half = 0.5 * (1.0 + t)
    return a * half, half + 0.5 * a * (1.0 - t * t) * (_GELU_K * (1.0 + 3.0 * _GELU_C * a2))


def _norm_mod(x, gain, scale, shift):
    rstd = lax.rsqrt(_rmean(x * x) + EPS)
    return (x * rstd * gain) * (1.0 + scale) + shift


def _norm_mod_bwd(dh, x, gain, scale):
    rstd = lax.rsqrt(_rmean(x * x) + EPS)
    xhat = x * rstd
    dshift = _csum(dh)
    dscale = _csum(dh * (xhat * gain))
    dhn = dh * (1.0 + scale)
    dgain = _csum(dhn * xhat)
    dxhat = dhn * gain
    dx = rstd * (dxhat - xhat * _rmean(dxhat * xhat))
    return dx, dshift, dscale, dgain


def _rms(x, gain):
    rstd = lax.rsqrt(_rmean(x * x) + EPS)
    return x * rstd * gain


def _rms_bwd(dy, x, gain):
    rstd = lax.rsqrt(_rmean(x * x) + EPS)
    xhat = x * rstd
    dgain = _csum(dy * xhat)
    dxhat = dy * gain
    return rstd * (dxhat - xhat * _rmean(dxhat * xhat)), dgain


PAIR = 2 * HD


def _seg_mean(a, pavg):
    hi, lo = _split(a)
    return jnp.concatenate([_dot(hi[:, p:p + PAIR], pavg) + _dot(lo[:, p:p + PAIR], pavg)
                            for p in range(0, a.shape[1], PAIR)], axis=1)


def _block_copies(src_hbm, dst_vmem, sems, rows):
    copies = []
    for k in range(NDEV):
        dst = dst_vmem.at[k] if rows is None else dst_vmem.at[pl.ds(k * rows, rows)]
        copies.append(pltpu.make_async_copy(src_hbm.at[k], dst, sems.at[k]))
    return copies


def _ffn_weight_fetch(wgu_hbm, wd_hbm, wgu_v, wd_v, sems):
    @pl.when(pl.program_id(0) == 0)
    def _():
        copies = _block_copies(wgu_hbm, wgu_v, sems.at[0], None) + _block_copies(wd_hbm, wd_v, sems.at[1], DR)
        for cp in copies:
            cp.start()
        for cp in copies:
            cp.wait()


def _place():
    return lax.axis_index("x"), lax.axis_index("y"), lax.axis_index("c")


def _slot(p):
    return 4 * p[0] + 2 * p[1] + p[2]


class GatherRide:
    def __init__(self, srcs):
        self.n = len(srcs)
        self.index = [i for _, i in srcs]
        self.args = [a for a, _ in srcs]
        self.out_shape = [jax.ShapeDtypeStruct((NDEV,) + (a.shape if i is None else a.shape[1:]), a.dtype)
                          for a, i in srcs]
        self.scratch = [pltpu.SemaphoreType.DMA((self.n, NDEV - 1)), pltpu.SemaphoreType.DMA((self.n, NDEV - 1)),
                        pltpu.SemaphoreType.DMA((self.n,))]

    def hooks(self, ins, outs, sems):
        send_sems, recv_sems, local_sems = sems
        n = self.n
        x, y, c = _place()
        me, sibling = (x, y, c), (x, y, 1 - c)
        chips = [(1 - x, y), (x, 1 - y), (1 - x, 1 - y)]

        def local(a):
            return ins[a] if self.index[a] is None else ins[a].at[self.index[a]]

        def copy(a, k, block, to, src=None):
            dst = outs[a].at[_slot(block)]
            return pltpu.make_async_remote_copy(
                src_ref=dst if src is None else src, dst_ref=dst,
                send_sem=send_sems.at[a, k], recv_sem=recv_sems.at[a, k],
                device_id=to, device_id_type=MESH)

        def mine():
            return [pltpu.make_async_copy(local(a), outs[a].at[_slot(me)], local_sems.at[a]) for a in range(n)]

        def first():
            cps = []
            for a in range(n):
                cps.append(copy(a, 0, me, sibling, src=local(a)))
                cps += [copy(a, 1 + j, me, (*chip, c), src=local(a)) for j, chip in enumerate(chips)]
            return cps

        def passed():
            return [copy(a, 4 + j, (*chip, c), sibling) for j, chip in enumerate(chips) for a in range(n)]

        def start():
            for cp in mine() + first():
                cp.start()

        def mid():
            for j, chip in enumerate(chips):
                for a in range(n):
                    copy(a, 1 + j, (*chip, c), me).wait_recv()
                    copy(a, 4 + j, (*chip, c), sibling).start()

        def finish():
            for a in range(n):
                copy(a, 0, sibling, me).wait_recv()
                for j, chip in enumerate(chips):
                    copy(a, 4 + j, (*chip, 1 - c), me).wait_recv()
            for cp in first() + passed():
                cp.wait_send()
            for cp in mine():
                cp.wait()

        return start, mid, finish


def all_gather(srcs, name):
    ride = GatherRide(srcs)
    n = ride.n

    def body(*refs):
        start, mid, finish = ride.hooks(refs[:n], refs[n:2 * n], refs[2 * n:])
        start()
        mid()
        finish()

    return pl.pallas_call(
        body, name=name,
        in_specs=[ANY] * n, out_specs=[ANY] * n, out_shape=ride.out_shape, scratch_shapes=ride.scratch,
    )(*ride.args)


def _call(core, ride, *, name, grid, in_specs, out_specs, out_shape, scratch_shapes, args):
    if ride is None:
        outs = pl.pallas_call(core, name=name, grid=grid, in_specs=in_specs, out_specs=out_specs,
                              out_shape=out_shape, scratch_shapes=scratch_shapes,
                              compiler_params=_cparams("arbitrary"))(*args)
        return outs, []
    n_in, n_out, n_sc, n = len(in_specs), len(out_shape), len(scratch_shapes), ride.n
    nsteps = grid[0]
    mid_step = max(nsteps - 2, 0)

    def body(*refs):
        cuts = [n_in, n_in + n, n_in + n + n_out, n_in + 2 * n + n_out, n_in + 2 * n + n_out + n_sc]
        ci, ri, co, ro, cs, rs = (refs[a:b] for a, b in zip([0] + cuts, cuts + [len(refs)]))
        start, mid, finish = ride.hooks(ri, ro, rs)
        i = pl.program_id(0)
        pl.when(i == 0)(start)
        core(*ci, *co, *cs)
        pl.when(i == mid_step)(mid)
        pl.when(i == nsteps - 1)(finish)

    outs = pl.pallas_call(
        body, name=name, grid=grid,
        in_specs=list(in_specs) + [ANY] * n, out_specs=list(out_specs) + [ANY] * n,
        out_shape=list(out_shape) + ride.out_shape, scratch_shapes=list(scratch_shapes) + ride.scratch,
        compiler_params=_cparams("arbitrary"))(*args, *ride.args)
    return outs[:n_out], outs[n_out:]


def all_to_all(arrs, name):
    n = len(arrs)

    def body(*refs):
        ins, outs = refs[:n], refs[n:2 * n]
        send_sems, recv_sems, local_sems = refs[2 * n:]
        x, y, c = _place()
        me = (x, y, c)

        def peer(k):
            return (1 - x if k & 4 else x, 1 - y if k & 2 else y, 1 - c if k & 1 else c)

        def copy(a, k):
            return pltpu.make_async_remote_copy(
                src_ref=ins[a].at[_slot(peer(k))], dst_ref=outs[a].at[_slot(me)],
                send_sem=send_sems.at[a, k - 1], recv_sem=recv_sems.at[a, k - 1],
                device_id=peer(k), device_id_type=MESH)

        def landing(a, k):
            return pltpu.make_async_remote_copy(
                src_ref=outs[a].at[_slot(peer(k))], dst_ref=outs[a].at[_slot(peer(k))],
                send_sem=send_sems.at[a, k - 1], recv_sem=recv_sems.at[a, k - 1],
                device_id=me, device_id_type=MESH)

        mine = [pltpu.make_async_copy(ins[a].at[_slot(me)], outs[a].at[_slot(me)], local_sems.at[a]) for a in range(n)]
        for cp in mine:
            cp.start()
        sends = [copy(a, k) for a in range(n) for k in range(1, NDEV)]
        for cp in sends:
            cp.start()
        for a in range(n):
            for k in range(1, NDEV):
                landing(a, k).wait_recv()
        for cp in sends:
            cp.wait_send()
        for cp in mine:
            cp.wait()

    return pl.pallas_call(
        body, name=name,
        in_specs=[ANY] * n, out_specs=[ANY] * n,
        out_shape=[jax.ShapeDtypeStruct(a.shape, a.dtype) for a in arrs],
        scratch_shapes=[pltpu.SemaphoreType.DMA((n, NDEV - 1)), pltpu.SemaphoreType.DMA((n, NDEV - 1)),
                        pltpu.SemaphoreType.DMA((n,))],
    )(*arrs)


FFN_TS = 256
FFN_FWD_TS = 512


def ffn_fwd(x, vec, wgu_g, wdown_g, tag, ride=None):
    S = x.shape[0]
    ts = min(FFN_FWD_TS, S)

    def body(x_ref, vec_ref, wgu_hbm, wd_hbm, xo_ref, h_ref, gu_ref, f_ref, wgu_v, wd_v, sems):
        _ffn_weight_fetch(wgu_hbm, wd_hbm, wgu_v, wd_v, sems)
        xv = x_ref[...]
        h = _norm_mod(xv, vec_ref[3:4, :], vec_ref[1:2, :], vec_ref[0:1, :]).astype(BF16)
        h_ref[...] = h
        acc = jnp.zeros((ts, D), F32)
        for j in range(NCHUNK):
            g = _dot(h, wgu_v[j])
            u = _dot(h, wgu_v[NCHUNK + j])
            gu_ref[j] = g.astype(BF16)
            gu_ref[NCHUNK + j] = u.astype(BF16)
            a = (g * _sigmoid(g) * u).astype(BF16)
            acc = acc + _dot(a, wd_v[pl.ds(j * FC, FC), :])
        f_ref[...] = acc.astype(BF16)
        xo_ref[...] = xv + (0.5 * vec_ref[2:3, :]) * acc

    return _call(
        body, ride, name=f"ffn_fwd_{tag}",
        grid=(S // ts,),
        in_specs=[pl.BlockSpec((ts, D), lambda i: (i, 0)),
                  pl.BlockSpec((8, D), lambda i: (0, 0)), ANY, ANY],
        out_specs=[pl.BlockSpec((ts, D), lambda i: (i, 0)),
                   pl.BlockSpec((ts, D), lambda i: (i, 0)),
                   pl.BlockSpec((NDEV, ts, FC), lambda i: (0, i, 0)),
                   pl.BlockSpec((ts, D), lambda i: (i, 0))],
        out_shape=[jax.ShapeDtypeStruct((S, D), F32), jax.ShapeDtypeStruct((S, D), BF16),
                   jax.ShapeDtypeStruct((NDEV, S, FC), BF16), jax.ShapeDtypeStruct((S, D), BF16)],
        scratch_shapes=[pltpu.VMEM((NDEV, D, FC), BF16), pltpu.VMEM((DFF, D), BF16),
                        pltpu.SemaphoreType.DMA((2, NDEV))],
        args=(x, vec, wgu_g, wdown_g))


def ffn_bwd(dxo, x, gu, f, vec, wgu_g, wdown_g, tag):
    S = x.shape[0]
    ts = min(FFN_TS, S)

    def body(dxo_ref, x_ref, gu_ref, f_ref, vec_ref, wgu_hbm, wd_hbm,
             dx_ref, dgu_ref, a_ref, df_ref, acc_ref, wgu_v, wd_v, sems):
        _ffn_weight_fetch(wgu_hbm, wd_hbm, wgu_v, wd_v, sems)

        @pl.when(pl.program_id(0) == 0)
        def _():
            acc_ref[...] = jnp.zeros_like(acc_ref)

        dxo_v = dxo_ref[...]
        dgate = 0.5 * _csum(dxo_v * f_ref[...].astype(F32))
        df = ((0.5 * vec_ref[2:3, :]) * dxo_v).astype(BF16)
        df_ref[...] = df
        dh = jnp.zeros((ts, D), F32)
        for j in range(NCHUNK):
            da = _dot_nt(df, wd_v[pl.ds(j * FC, FC), :])
            g = gu_ref[j].astype(F32)
            u = gu_ref[NCHUNK + j].astype(F32)
            sg = _sigmoid(g)
            si = g * sg
            a_ref[j] = (si * u).astype(BF16)
            dg = (da * u * (sg * (1.0 + g * (1.0 - sg)))).astype(BF16)
            du = (da * si).astype(BF16)
            dgu_ref[j] = dg
            dgu_ref[NCHUNK + j] = du
            dh = dh + _dot_nt(dg, wgu_v[j]) + _dot_nt(du, wgu_v[NCHUNK + j])
        dx, dshift, dscale, dgain = _norm_mod_bwd(dh, x_ref[...], vec_ref[3:4, :], vec_ref[1:2, :])
        dx_ref[...] = dx + dxo_v
        acc_ref[0:1, :] += dshift
        acc_ref[1:2, :] += dscale
        acc_ref[2:3, :] += dgate
        acc_ref[3:4, :] += dgain

    row = pl.BlockSpec((ts, D), lambda i: (i, 0))
    return pl.pallas_call(
        body, name=f"ffn_bwd_{tag}",
        grid=(S // ts,),
        in_specs=[row, row, pl.BlockSpec((NDEV, ts, FC), lambda i: (0, i, 0)), row,
                  pl.BlockSpec((8, D), lambda i: (0, 0)), ANY, ANY],
        out_specs=[row, pl.BlockSpec((NDEV, ts, FC), lambda i: (0, i, 0)),
                   pl.BlockSpec((NCHUNK, ts, FC), lambda i: (0, i, 0)), row,
                   pl.BlockSpec((8, D), lambda i: (0, 0))],
        out_shape=[jax.ShapeDtypeStruct((S, D), F32), jax.ShapeDtypeStruct((NDEV, S, FC), BF16),
                   jax.ShapeDtypeStruct((NCHUNK, S, FC), BF16), jax.ShapeDtypeStruct((S, D), BF16),
                   jax.ShapeDtypeStruct((8, D), F32)],
        scratch_shapes=[pltpu.VMEM((NDEV, D, FC), BF16), pltpu.VMEM((DFF, D), BF16),
                        pltpu.SemaphoreType.DMA((2, NDEV))],
        compiler_params=_cparams("arbitrary"),
    )(dxo, x, gu, f, vec, wgu_g, wdown_g)


NCHIP = NDEV // 2


def tn_matmul_scatter(me_arr, a, b, slot, nslots, prev, name, split=1):
    na, S, M = a.shape
    nb, _, N = b.shape
    ncall = NDEV // split
    ts = min(4096, S)
    nsteps = S // ts
    mp = M // split
    other_step = {1: lambda j: 2 * j, 2: lambda j: j, 8: lambda j: 0}[split]
    mine_step = {1: lambda j: 2 * j + 1, 2: lambda j: j, 8: lambda j: 0}[split]

    def group(k, me_ref):
        if split == 1:
            return jnp.bitwise_xor(me_ref[0], NDEV - 1 - k)
        if split == 2:
            return jnp.bitwise_xor(me_ref[0] // 2, NCHIP - 1 - k)
        return 0

    def body(me_ref, *refs):
        a_ref, b_ref = refs[0], refs[1]
        recv_ref, acc, sb_other, sb_mine, land, d2d_send, d2d_recv, ici_send, ici_recv = refs[-9:]
        k = pl.program_id(0)
        s = pl.program_id(1)
        x, y, c = _place()
        my_chip = 2 * x + y

        def chip_of(j):
            if split == 8:
                cx, cy = j // 2, j % 2
            else:
                flip = NCHIP - 1 - j
                cx, cy = (1 - x if flip & 2 else x), (1 - y if flip & 1 else y)
            return cx, cy, 2 * cx + cy

        def piece(j, core):
            if split == 1:
                return acc[...]
            start = core * mp if split == 2 else (2 * j + core) * mp
            return acc[pl.ds(pl.multiple_of(start, 8), mp), :]

        def to_sibling(j):
            return pltpu.make_async_remote_copy(
                src_ref=sb_other.at[j], dst_ref=land.at[j], send_sem=d2d_send.at[j], recv_sem=d2d_recv.at[j],
                device_id=(x, y, 1 - c), device_id_type=MESH)

        def to_owner(j):
            cx, cy, ci = chip_of(j)
            dst = recv_ref.at[my_chip, slot]
            return ci, pltpu.make_async_copy(sb_mine.at[j], dst, ici_send.at[j]), pltpu.make_async_remote_copy(
                src_ref=sb_mine.at[j], dst_ref=dst, send_sem=ici_send.at[j], recv_sem=ici_recv.at[my_chip],
                device_id=(cx, cy, c), device_id_type=MESH)

        if nsteps == 1:
            acc[...] = _dot_tn(a_ref[...], b_ref[...])
        else:
            @pl.when(s == 0)
            def _():
                acc[...] = jnp.zeros_like(acc)

            acc[...] += _dot_tn(a_ref[...], b_ref[...])

        for kk in range(ncall):
            @pl.when((s == nsteps - 1) & (k == kk))
            def _():
                for j in range(NCHIP):
                    if other_step(j) == kk:
                        sb_other[j] = piece(j, 1 - c).astype(BF16)
                        to_sibling(j).start()
                for j in range(NCHIP):
                    if mine_step(j) == kk:
                        to_sibling(j).wait_recv()
                        sb_mine[j] = (piece(j, c) + land[j].astype(F32)).astype(BF16)
                        ci, loc, rem = to_owner(j)
                        pl.when(ci == my_chip)(loc.start)
                        pl.when(ci != my_chip)(rem.start)

        @pl.when((s == nsteps - 1) & (k == ncall - 1))
        def _():
            for j in range(NCHIP):
                to_sibling(j).wait_send()
                ci, loc, rem = to_owner(j)
                pl.when(ci == my_chip)(loc.wait)
                pl.when(ci != my_chip)(rem.wait_send)
            for src in range(NCHIP):
                @pl.when(my_chip != src)
                def _():
                    pltpu.make_async_remote_copy(
                        src_ref=recv_ref.at[src, slot], dst_ref=recv_ref.at[src, slot],
                        send_sem=ici_send.at[src], recv_sem=ici_recv.at[src],
                        device_id=(src // 2, src % 2, c), device_id_type=MESH).wait_recv()

    in_specs = [pl.BlockSpec((None, ts, M), (lambda k, s, me: (group(k, me), s, 0)) if na > 1 else (lambda k, s, me: (0, s, 0))),
                pl.BlockSpec((None, ts, N), (lambda k, s, me: (group(k, me), s, 0)) if nb > 1 else (lambda k, s, me: (0, s, 0)))]
    args = [me_arr, a, b]
    aliases = {}
    if prev is not None:
        in_specs.append(ANY)
        args.append(prev)
        aliases = {3: 0}
    return pl.pallas_call(
        body, name=name,
        grid_spec=pltpu.PrefetchScalarGridSpec(
            num_scalar_prefetch=1, grid=(ncall, nsteps), in_specs=in_specs, out_specs=ANY,
            scratch_shapes=[pltpu.VMEM((M, N), F32), pltpu.VMEM((NCHIP, mp, N), BF16), pltpu.VMEM((NCHIP, mp, N), BF16),
                            pltpu.VMEM((NCHIP, mp, N), BF16), pltpu.SemaphoreType.DMA((NCHIP,)),
                            pltpu.SemaphoreType.DMA((NCHIP,)), pltpu.SemaphoreType.DMA((NCHIP,)),
                            pltpu.SemaphoreType.DMA((NCHIP,))]),
        out_shape=jax.ShapeDtypeStruct((NCHIP, nslots, mp, N), BF16),
        input_output_aliases=aliases,
        compiler_params=_cparams("arbitrary", "arbitrary"),
    )(*args)


MIX_TS = 512
MIX_IN_TS = 512


def mix_in_fwd(x, vec, win_g, tag, ride=None):
    S = x.shape[0]
    ts = min(MIX_IN_TS, S)

    def body(x_ref, vec_ref, win_ref, hm_ref, proj_ref):
        h = _norm_mod(x_ref[...], vec_ref[3:4, :], vec_ref[1:2, :], vec_ref[0:1, :]).astype(BF16)
        hm_ref[...] = h
        for k in range(NDEV):
            proj_ref[k] = _dot(h, win_ref[k])

    return _call(
        body, ride, name=f"mix_in_fwd_{tag}",
        grid=(S // ts,),
        in_specs=[pl.BlockSpec((ts, D), lambda i: (i, 0)), pl.BlockSpec((8, D), lambda i: (0, 0)),
                  pl.BlockSpec((NDEV, D, PC), lambda i: (0, 0, 0))],
        out_specs=[pl.BlockSpec((ts, D), lambda i: (i, 0)),
                   pl.BlockSpec((NDEV, ts, PC), lambda i: (0, i, 0))],
        out_shape=[jax.ShapeDtypeStruct((S, D), BF16), jax.ShapeDtypeStruct((NDEV, S, PC), F32)],
        scratch_shapes=[], args=(x, vec, win_g))


def mix_in_bwd(dproj, x, dxo, vec, win_g, tag):
    S = x.shape[0]
    ts = min(MIX_IN_TS, S)

    def body(dp_ref, x_ref, dxo_ref, vec_ref, win_ref, dx_ref, acc_ref):
        @pl.when(pl.program_id(0) == 0)
        def _():
            acc_ref[...] = jnp.zeros_like(acc_ref)

        dh = jnp.zeros((ts, D), F32)
        for k in range(NDEV):
            dh = dh + _dot_nt(dp_ref[k], win_ref[k])
        dx, dshift, dscale, dgain = _norm_mod_bwd(dh, x_ref[...], vec_ref[3:4, :], vec_ref[1:2, :])
        dx_ref[...] = dx + dxo_ref[...]
        acc_ref[0:1, :] += dshift
        acc_ref[1:2, :] += dscale
        acc_ref[3:4, :] += dgain

    row = pl.BlockSpec((ts, D), lambda i: (i, 0))
    return pl.pallas_call(
        body, name=f"mix_in_bwd_{tag}",
        grid=(S // ts,),
        in_specs=[pl.BlockSpec((NDEV, ts, PC), lambda i: (0, i, 0)), row, row,
                  pl.BlockSpec((8, D), lambda i: (0, 0)),
                  pl.BlockSpec((NDEV, D, PC), lambda i: (0, 0, 0))],
        out_specs=[row, pl.BlockSpec((8, D), lambda i: (0, 0))],
        out_shape=[jax.ShapeDtypeStruct((S, D), F32), jax.ShapeDtypeStruct((8, D), F32)],
        compiler_params=_cparams("arbitrary"),
    )(dproj, x, dxo, vec, win_g)


SCAN_UNROLL = 4


def _shift_down(z, k, row):
    return jnp.where(row >= k, pltpu.roll(z, k, 0), 0.0)


def _shift_up(z, k, row, n):
    return jnp.where(row < n - k, pltpu.roll(z, n - k, 0), 0.0)


def _lru_gates(xc, lp_ref, wa_ref, wx_ref):
    xcb = xc.astype(BF16)
    ra = _sigmoid(_dot(xcb, wa_ref[...]) + lp_ref[5:6, :])
    ix = _sigmoid(_dot(xcb, wx_ref[...]) + lp_ref[6:7, :])
    lam = lp_ref[7:8, :]
    ls = jnp.minimum(lam, 0.0) - jnp.log(1.0 + jnp.exp(-jnp.abs(lam)))
    log_a = (RG_LRU_C * ls) * ra
    a = jnp.exp(log_a)
    mult = jnp.sqrt(-jnp.tanh(log_a) * (a * a + 1.0))
    return ra, ix, ls, a, mult


def _conv(x, lp_ref, row):
    return (lp_ref[4:5, :] + lp_ref[3:4, :] * x + lp_ref[2:3, :] * _shift_down(x, 1, row)
            + lp_ref[1:2, :] * _shift_down(x, 2, row) + lp_ref[0:1, :] * _shift_down(x, 3, row))


def lru_fwd(proj, lp, wa_t, wx_t, tag, ride=None):
    S = proj.shape[1]
    nblk = S // 8

    def body(x_ref, g_ref, lp_ref, wa_ref, wx_ref, y_ref, xc_ref, h_ref, a_s, b_s):
        x = x_ref[...]
        row = lax.broadcasted_iota(jnp.int32, x.shape, 0)
        xc = _conv(x, lp_ref, row)
        xc_ref[...] = xc
        ra, ix, ls, a, mult = _lru_gates(xc, lp_ref, wa_ref, wx_ref)
        a_s[...] = a
        b_s[...] = mult * (ix * xc)
        rowb = lax.broadcasted_iota(jnp.int32, (8, LC), 0)

        def step(i, carry):
            for q in range(SCAN_UNROLL):
                r0 = pl.multiple_of((i * SCAN_UNROLL + q) * 8, 8)
                A = a_s[pl.ds(r0, 8), :]
                B = b_s[pl.ds(r0, 8), :]
                for d in (1, 2, 4):
                    m = rowb >= d
                    As = jnp.where(m, pltpu.roll(A, d, 0), 1.0)
                    Bs = jnp.where(m, pltpu.roll(B, d, 0), 0.0)
                    B = A * Bs + B
                    A = A * As
                H = B + A * carry
                h_ref[pl.ds(r0, 8), :] = H
                carry = H[7:8, :]
            return carry

        lax.fori_loop(0, nblk // SCAN_UNROLL, step, jnp.zeros((1, LC), F32))
        y_ref[...] = h_ref[...] * _gelu(g_ref[...])

    col = pl.BlockSpec((S, LC), lambda c: (0, c))
    return _call(
        body, ride, name=f"lru_fwd_{tag}",
        grid=(LW // LC,),
        in_specs=[pl.BlockSpec((None, S, LC), lambda c: (c // 2, 0, c % 2)),
                  pl.BlockSpec((None, S, LC), lambda c: (2 + c // 2, 0, c % 2)),
                  pl.BlockSpec((8, LC), lambda c: (0, c)),
                  pl.BlockSpec((None, LC, LC), lambda c: (c, 0, 0)),
                  pl.BlockSpec((None, LC, LC), lambda c: (c, 0, 0))],
        out_specs=[col, col, col],
        out_shape=[jax.ShapeDtypeStruct((S, LW), F32)] * 3,
        scratch_shapes=[pltpu.VMEM((S, LC), F32), pltpu.VMEM((S, LC), F32)],
        args=(proj, proj, lp, wa_t, wx_t))


def lru_bwd(dy, proj, xc_all, hst, lp, wa_t, wx_t, tag):
    S = proj.shape[1]
    nblk = S // 8

    def body(dy_ref, x_ref, g_ref, xc_ref, h_ref, lp_ref, wa_ref, wx_ref,
             dx_ref, dg_ref, dlp_ref, dwa_ref, dwx_ref, c_s, l_s):
        xc = xc_ref[...]
        row = lax.broadcasted_iota(jnp.int32, xc.shape, 0)
        ra, ix, ls, a, mult = _lru_gates(xc, lp_ref, wa_ref, wx_ref)
        g = g_ref[...]
        dyv = dy_ref[...]
        h = h_ref[...]
        gelu_g, gelu_grad_g = _gelu_and_grad(g)
        dg_ref[...] = (dyv * h * gelu_grad_g).astype(BF16)
        c_s[...] = _shift_up(a, 1, row, S)
        l_s[...] = dyv * gelu_g
        rowb = lax.broadcasted_iota(jnp.int32, (8, LC), 0)

        def step(i, carry):
            for q in range(SCAN_UNROLL):
                r0 = pl.multiple_of((nblk - 1 - (i * SCAN_UNROLL + q)) * 8, 8)
                C = c_s[pl.ds(r0, 8), :]
                L = l_s[pl.ds(r0, 8), :]
                for d in (1, 2, 4):
                    m = rowb < 8 - d
                    Cs = jnp.where(m, pltpu.roll(C, 8 - d, 0), 1.0)
                    Ls = jnp.where(m, pltpu.roll(L, 8 - d, 0), 0.0)
                    L = C * Ls + L
                    C = C * Cs
                L = L + C * carry
                l_s[pl.ds(r0, 8), :] = L
                carry = L[0:1, :]
            return carry

        lax.fori_loop(0, nblk // SCAN_UNROLL, step, jnp.zeros((1, LC), F32))
        db = l_s[...]
        da = db * _shift_down(h, 1, row)
        ixc = ix * xc
        dmult = db * ixc
        dix = db * (mult * xc)
        dxc = db * (mult * ix)
        dlog_a = da * a - dmult * (a * a) / mult
        dra = dlog_a * (RG_LRU_C * ls)
        dls = _csum(dlog_a * ra) * RG_LRU_C
        lam = lp_ref[7:8, :]
        dlam = dls * _sigmoid(-lam)
        dpa = dra * ra * (1.0 - ra)
        dpx = dix * ix * (1.0 - ix)
        dpab = dpa.astype(BF16)
        dpxb = dpx.astype(BF16)
        xcb = xc.astype(BF16)
        dwa_ref[...] = _dot_tn(xcb, dpab)
        dwx_ref[...] = _dot_tn(xcb, dpxb)
        dxc = dxc + _dot_nt(dpab, wa_ref[...]) + _dot_nt(dpxb, wx_ref[...])
        x = x_ref[...]
        dlp_ref[0:1, :] = _csum(dxc * _shift_down(x, 3, row))
        dlp_ref[1:2, :] = _csum(dxc * _shift_down(x, 2, row))
        dlp_ref[2:3, :] = _csum(dxc * _shift_down(x, 1, row))
        dlp_ref[3:4, :] = _csum(dxc * x)
        dlp_ref[4:5, :] = _csum(dxc)
        dlp_ref[5:6, :] = _csum(dpa)
        dlp_ref[6:7, :] = _csum(dpx)
        dlp_ref[7:8, :] = dlam
        dx = (lp_ref[3:4, :] * dxc + lp_ref[2:3, :] * _shift_up(dxc, 1, row, S)
              + lp_ref[1:2, :] * _shift_up(dxc, 2, row, S) + lp_ref[0:1, :] * _shift_up(dxc, 3, row, S))
        dx_ref[...] = dx.astype(BF16)

    col = pl.BlockSpec((S, LC), lambda c: (0, c))
    pcol = pl.BlockSpec((None, S, LC), lambda c: (c // 2, 0, c % 2))
    return pl.pallas_call(
        body, name=f"lru_bwd_{tag}",
        grid=(LW // LC,),
        in_specs=[col, pcol, pl.BlockSpec((None, S, LC), lambda c: (2 + c // 2, 0, c % 2)), col, col,
                  pl.BlockSpec((8, LC), lambda c: (0, c)),
                  pl.BlockSpec((None, LC, LC), lambda c: (c, 0, 0)),
                  pl.BlockSpec((None, LC, LC), lambda c: (c, 0, 0))],
        out_specs=[pcol, pcol, pl.BlockSpec((8, LC), lambda c: (0, c)),
                   pl.BlockSpec((None, LC, LC), lambda c: (c, 0, 0)),
                   pl.BlockSpec((None, LC, LC), lambda c: (c, 0, 0))],
        out_shape=[jax.ShapeDtypeStruct((2, S, PC), BF16), jax.ShapeDtypeStruct((2, S, PC), BF16),
                   jax.ShapeDtypeStruct((8, LW), F32),
                   jax.ShapeDtypeStruct((LW // LC, LC, LC), F32), jax.ShapeDtypeStruct((LW // LC, LC, LC), F32)],
        scratch_shapes=[pltpu.VMEM((S, LC), F32), pltpu.VMEM((S, LC), F32)],
        compiler_params=_cparams("arbitrary"),
    )(dy, proj, proj, xc_all, hst, lp, wa_t, wx_t)


def _pair_stack(zp, low):
    return jnp.concatenate([jnp.where(low, zp, 0.0), jnp.where(low, 0.0, zp)], axis=0).astype(BF16)


def _spatial(w_ref, zc, low):
    return jnp.concatenate(
        [_dot(w_ref[:, 2 * p * CHUNK:2 * (p + 1) * CHUNK], _pair_stack(zc[:, p * PAIR:(p + 1) * PAIR], low))
         for p in range(GW // PAIR)], axis=1)


def _gmlp_fwd_parts(u, v, gp_ref, wcat_ref, bz_ref, pavg_ref, ts, with_grad=False):
    if with_grad:
        ug, ugrad = _gelu_and_grad(u)
        vg, vgrad = _gelu_and_grad(v)
    else:
        ug, vg, ugrad, vgrad = _gelu(u), _gelu(v), None, None
    pavg = pavg_ref[...]
    vc = vg - _seg_mean(vg, pavg)
    rs = lax.rsqrt(_seg_mean(vc * vc, pavg) + EPS)
    vhat = vc * rs
    vh = vhat * gp_ref[0:1, :]
    low = lax.broadcasted_iota(jnp.int32, (CHUNK, PAIR), 1) < HD
    zs = [_spatial(wcat_ref, vh[n * CHUNK:(n + 1) * CHUNK, :], low) + bz_ref[...] for n in range(ts // CHUNK)]
    z = jnp.concatenate(zs, axis=0) if len(zs) > 1 else zs[0]
    return ug, rs, vhat, vh, z, ugrad, vgrad


def mix_out_fwd(proj, ylru, x, vec, gp, wcat, bz, pavg, wout_g, tag, ride=None):
    S = x.shape[0]
    ts = min(MIX_TS, S)

    def body(u_ref, v_ref, yl_ref, x_ref, vec_ref, gp_ref, wcat_ref, bz_ref, pavg_ref, wout_ref,
             xo_ref, y_ref, fo_ref):
        u = jnp.concatenate([u_ref[0], u_ref[1]], axis=1)
        v = jnp.concatenate([v_ref[0], v_ref[1]], axis=1)
        ug, _, _, _, z, _, _ = _gmlp_fwd_parts(u, v, gp_ref, wcat_ref, bz_ref, pavg_ref, ts)
        n1 = _rms(yl_ref[...], gp_ref[1:2, :])
        n2 = _rms(ug * z, gp_ref[2:3, :])
        y = jnp.concatenate([n1, n2], axis=1).astype(BF16)
        y_ref[...] = y
        fo = jnp.zeros((ts, D), F32)
        for k in range(NDEV):
            fo = fo + _dot(y[:, k * OR:(k + 1) * OR], wout_ref[k])
        fo_ref[...] = fo.astype(BF16)
        xo_ref[...] = x_ref[...] + vec_ref[2:3, :] * fo

    row = pl.BlockSpec((ts, D), lambda i: (i, 0))
    full = lambda shp: pl.BlockSpec(shp, lambda i: tuple(0 for _ in shp))
    return _call(
        body, ride, name=f"mix_out_fwd_{tag}",
        grid=(S // ts,),
        in_specs=[pl.BlockSpec((2, ts, PC), lambda i: (2, i, 0)), pl.BlockSpec((2, ts, PC), lambda i: (3, i, 0)),
                  pl.BlockSpec((ts, LW), lambda i: (i, 0)), row, full((8, D)), full((8, GW)),
                  full((CHUNK, HEADS * CHUNK)), full((CHUNK, GW)), full((PAIR, PAIR)),
                  pl.BlockSpec((NDEV, OR, D), lambda i: (0, 0, 0))],
        out_specs=[row, row, row],
        out_shape=[jax.ShapeDtypeStruct((S, D), F32), jax.ShapeDtypeStruct((S, D), BF16),
                   jax.ShapeDtypeStruct((S, D), BF16)],
        scratch_shapes=[], args=(proj, proj, ylru, x, vec, gp, wcat, bz, pavg, wout_g))


def mix_out_bwd(dxo, proj, ylru, fo, vec, gp, wcat, wcat_t, bz, pavg, wout_t, tag):
    S = dxo.shape[0]
    ts = min(MIX_TS, S)

    def body(dxo_ref, u_ref, v_ref, yl_ref, fo_ref, vec_ref, gp_ref, wcat_ref, wcatt_ref, bz_ref, pavg_ref,
             wout_ref, dyo_ref, dyl_ref, duv_ref, acc_ref, dgp_ref, dwm_ref, dbz_ref):
        @pl.when(pl.program_id(0) == 0)
        def _():
            acc_ref[...] = jnp.zeros_like(acc_ref)
            dgp_ref[...] = jnp.zeros_like(dgp_ref)
            dwm_ref[...] = jnp.zeros_like(dwm_ref)
            dbz_ref[...] = jnp.zeros_like(dbz_ref)

        dxo_v = dxo_ref[...]
        acc_ref[2:3, :] += _csum(dxo_v * fo_ref[...].astype(F32))
        dyo = (vec_ref[2:3, :] * dxo_v).astype(BF16)
        dyo_ref[...] = dyo
        dn = _dot(dyo, wout_ref[...])
        dn1, dn2 = dn[:, :LW], dn[:, LW:]
        dyl, dg1 = _rms_bwd(dn1, yl_ref[...], gp_ref[1:2, :])
        dyl_ref[...] = dyl
        u = jnp.concatenate([u_ref[0], u_ref[1]], axis=1)
        v = jnp.concatenate([v_ref[0], v_ref[1]], axis=1)
        ug, rs, vhat, vh, z, ugrad, vgrad = _gmlp_fwd_parts(u, v, gp_ref, wcat_ref, bz_ref, pavg_ref, ts,
                                                            with_grad=True)
        dyg, dg2 = _rms_bwd(dn2, ug * z, gp_ref[2:3, :])
        du = (dyg * z) * ugrad
        dz = dyg * ug
        low = lax.broadcasted_iota(jnp.int32, (CHUNK, PAIR), 1) < HD
        vhb = vh.astype(BF16)
        dvhs = []
        dbz = jnp.zeros((CHUNK, GW), F32)
        dwm = [jnp.zeros((2 * CHUNK, CHUNK), F32) for _ in range(GW // PAIR)]
        for n in range(ts // CHUNK):
            dzc = dz[n * CHUNK:(n + 1) * CHUNK, :]
            dbz = dbz + dzc
            for p in range(GW // PAIR):
                stack = _pair_stack(dzc[:, p * PAIR:(p + 1) * PAIR], low)
                dwm[p] = dwm[p] + _dot_nt(stack, vhb[n * CHUNK:(n + 1) * CHUNK, p * PAIR:(p + 1) * PAIR])
            dvhs.append(_spatial(wcatt_ref, dzc, low))
        dbz_ref[...] += dbz
        for p in range(GW // PAIR):
            dwm_ref[2 * p * CHUNK:2 * (p + 1) * CHUNK, :] += dwm[p]
        dvh = jnp.concatenate(dvhs, axis=0) if len(dvhs) > 1 else dvhs[0]
        pavg = pavg_ref[...]
        dvn = _csum(dvh * vhat)
        dvhat = dvh * gp_ref[0:1, :]
        dvg = rs * (dvhat - _seg_mean(dvhat, pavg) - vhat * _seg_mean(dvhat * vhat, pavg))
        dv = dvg * vgrad
        duv_ref[0] = du[:, :PC].astype(BF16)
        duv_ref[1] = du[:, PC:].astype(BF16)
        duv_ref[2] = dv[:, :PC].astype(BF16)
        duv_ref[3] = dv[:, PC:].astype(BF16)
        dgp_ref[0:1, :] += dvn
        dgp_ref[1:2, :] += dg1
        dgp_ref[2:3, :] += dg2

    row = pl.BlockSpec((ts, D), lambda i: (i, 0))
    full = lambda shp: pl.BlockSpec(shp, lambda i: tuple(0 for _ in shp))
    return pl.pallas_call(
        body, name=f"mix_out_bwd_{tag}",
        grid=(S // ts,),
        in_specs=[row, pl.BlockSpec((2, ts, PC), lambda i: (2, i, 0)), pl.BlockSpec((2, ts, PC), lambda i: (3, i, 0)),
                  pl.BlockSpec((ts, LW), lambda i: (i, 0)), row, full((8, D)), full((8, GW)),
                  full((CHUNK, HEADS * CHUNK)), full((CHUNK, HEADS * CHUNK)), full((CHUNK, GW)), full((PAIR, PAIR)),
                  full((D, D))],
        out_specs=[row, pl.BlockSpec((ts, LW), lambda i: (i, 0)), pl.BlockSpec((4, ts, PC), lambda i: (0, i, 0)),
                   full((8, D)), full((8, GW)), full((HEADS * CHUNK, CHUNK)), full((CHUNK, GW))],
        out_shape=[jax.ShapeDtypeStruct((S, D), BF16), jax.ShapeDtypeStruct((S, LW), F32),
                   jax.ShapeDtypeStruct((4, S, PC), BF16), jax.ShapeDtypeStruct((8, D), F32),
                   jax.ShapeDtypeStruct((8, GW), F32), jax.ShapeDtypeStruct((HEADS * CHUNK, CHUNK), F32),
                   jax.ShapeDtypeStruct((CHUNK, GW), F32)],
        compiler_params=_cparams("arbitrary"),
    )(dxo, proj, proj, ylru, fo, vec, gp, wcat, wcat_t, bz, pavg, wout_t)


def final_loss(x, target, gain):
    S = x.shape[0]
    ts = min(512, S)

    def body(x_ref, t_ref, g_ref, loss_ref, dx_ref, dg_ref):
        @pl.when(pl.program_id(0) == 0)
        def _():
            loss_ref[...] = jnp.zeros_like(loss_ref)
            dg_ref[...] = jnp.zeros_like(dg_ref)

        xv = x_ref[...]
        gain_v = g_ref[0:1, :]
        rstd = lax.rsqrt(_rmean(xv * xv) + EPS)
        xhat = xv * rstd
        err = xhat * gain_v - t_ref[...]
        loss_ref[...] += 0.5 * _csum(_rmean(err * err))
        dy = err * (1.0 / D)
        dg_ref[0:1, :] += _csum(dy * xhat)
        dxhat = dy * gain_v
        dx_ref[...] = rstd * (dxhat - xhat * _rmean(dxhat * xhat))

    row = pl.BlockSpec((ts, D), lambda i: (i, 0))
    return pl.pallas_call(
        body, name="final_loss",
        grid=(S // ts,),
        in_specs=[row, row, pl.BlockSpec((8, D), lambda i: (0, 0))],
        out_specs=[pl.BlockSpec((8, 128), lambda i: (0, 0)), row, pl.BlockSpec((8, D), lambda i: (0, 0))],
        out_shape=[jax.ShapeDtypeStruct((8, 128), F32), jax.ShapeDtypeStruct((S, D), F32),
                   jax.ShapeDtypeStruct((8, D), F32)],
        compiler_params=_cparams("arbitrary"),
    )(x, target, gain)


def _vec(mod_l, j, gain):
    return jnp.concatenate([mod_l[3 * j:3 * j + 3], gain[None, :], jnp.zeros((4, D), F32)], axis=0)


def _block_diag_tiles(w):
    w4 = w.reshape(LW // LC, 2, HD, HD)
    eye2 = jnp.eye(2, dtype=w.dtype)
    return (w4[:, :, :, None, :] * eye2[None, :, None, :, None]).reshape(LW // LC, LC, LC).astype(BF16)


def _block_diag_extract(dw):
    d5 = dw.reshape(LW // LC, 2, HD, 2, HD)
    return jnp.einsum('cihkj,ik->cihj', d5, jnp.eye(2, dtype=dw.dtype)).reshape(HEADS, HD, HD)


def _layer_params(l, p, conv_w_full):
    lp = jnp.concatenate([conv_w_full[l], p['conv_b'][l][None], p['gate_a_b'][l].reshape(1, LW),
                          p['gate_x_b'][l].reshape(1, LW), p['lru_lambda'][l][None]], axis=0)
    gp = jnp.concatenate([p['v_norm'][l][None], p['lru_out_norm'][l][None], p['gmlp_out_norm'][l][None],
                          jnp.zeros((5, GW), F32)], axis=0)
    ws = p['spatial_w'][l] * jnp.tril(jnp.ones((CHUNK, CHUNK), F32))
    wcat = ws.transpose(1, 0, 2).reshape(CHUNK, HEADS * CHUNK).astype(BF16)
    wcat_t = ws.transpose(2, 0, 1).reshape(CHUNK, HEADS * CHUNK).astype(BF16)
    bz = jnp.repeat(p['spatial_b'][l].T, HD, axis=1)
    return dict(lp=lp, gp=gp, wcat=wcat, wcat_t=wcat_t, bz=bz,
                wa_t=_block_diag_tiles(p['gate_a_w'][l]), wx_t=_block_diag_tiles(p['gate_x_w'][l]))


def _pavg():
    return jnp.kron(jnp.eye(2, dtype=F32), jnp.full((HD, HD), 1.0 / HD, F32)).astype(BF16)


GATHER_RIDES = {
    ('ffn_a', 0): [('w_in', 0), ('gu', DEPTH)],
    ('mix_in', 0): [('w_out', 0)],
    ('lru', 0): [('down', DEPTH)],
    ('mix_out', 0): [('down', 1)],
    ('ffn_b', 0): [('gu', 1), ('w_in', 1)],
    ('ffn_a', 1): [('gu', DEPTH + 1), ('w_out', 1)],
    ('mix_in', 1): [('down', DEPTH + 1)],
}


def local_fwd_bwd(me_arr, x, target, mod, p, loc, gathered, conv_w_full):
    pavg = _pavg()
    g = dict(gathered)

    def ride(call, l):
        todo = GATHER_RIDES.get((call, l))
        return None if todo is None else (todo, GatherRide([(loc[kind], slot) for kind, slot in todo]))

    def run(fn, call, l, *args):
        r = ride(call, l)
        outs, got = fn(*args, ride=None if r is None else r[1])
        if r is not None:
            g.update(dict(zip(r[0], got)))
        return outs

    saved = []
    h = x
    for l in range(DEPTH):
        q = _layer_params(l, p, conv_w_full)
        v1 = _vec(mod[l], 0, p['ffn1_norm'][l])
        vm = _vec(mod[l], 1, p['mix_norm'][l])
        v2 = _vec(mod[l], 2, p['ffn2_norm'][l])
        x0 = h
        x1, h1, gu1, f1 = run(ffn_fwd, 'ffn_a', l, x0, v1, g['gu', l], g['down', l], f"a{l}")
        hm, proj = run(mix_in_fwd, 'mix_in', l, x1, vm, g['w_in', l], f"{l}")
        ylru, xc, hst = run(lru_fwd, 'lru', l, proj, q['lp'], q['wa_t'], q['wx_t'], f"{l}")
        x2, y, fo = run(mix_out_fwd, 'mix_out', l, proj, ylru, x1, vm, q['gp'], q['wcat'], q['bz'], pavg,
                        g['w_out', l], f"{l}")
        x3, h2, gu2, f2 = run(ffn_fwd, 'ffn_b', l, x2, v2, g['gu', DEPTH + l], g['down', DEPTH + l], f"b{l}")
        saved.append(dict(q=q, v1=v1, vm=vm, v2=v2, x0=x0, x1=x1, x2=x2, h1=h1, gu1=gu1, f1=f1, hm=hm, proj=proj,
                          ylru=ylru, xc=xc, hst=hst, y=y, fo=fo, h2=h2, gu2=gu2, f2=f2))
        h = x3
    fin = jnp.concatenate([p['final_norm'][None], jnp.zeros((7, D), F32)], axis=0)
    loss8, dx, dfin = final_loss(h, target, fin)
    loss = loss8[0, 0]

    big = dict(gu=None, down=None, w_in=None, w_out=None)
    small = {k: [None] * DEPTH for k in ('ffn1_norm', 'mix_norm', 'ffn2_norm', 'conv_w', 'conv_b', 'gate_a_w',
                                         'gate_a_b', 'gate_x_w', 'gate_x_b', 'lru_lambda', 'v_norm', 'spatial_w',
                                         'spatial_b', 'lru_out_norm', 'gmlp_out_norm')}
    dmod = [None] * DEPTH
    tril = jnp.tril(jnp.ones((CHUNK, CHUNK), F32))
    for l in reversed(range(DEPTH)):
        sv = saved[l]
        q = sv['q']
        dx2, dgu, a, df, acc2 = ffn_bwd(dx, sv['x2'], sv['gu2'], sv['f2'], sv['v2'],
                                        g['gu', DEPTH + l], g['down', DEPTH + l], f"b{l}")
        big['gu'] = tn_matmul_scatter(me_arr, dgu, sv['h2'][None], DEPTH + l, 2 * DEPTH, big['gu'], f"dw_gu_b{l}")
        big['down'] = tn_matmul_scatter(me_arr, a, df[None], DEPTH + l, 2 * DEPTH, big['down'], f"dw_down_b{l}", split=2)
        dyo, dylru, duv, accmo, dgp, dwm, dbz = mix_out_bwd(dx2, sv['proj'], sv['ylru'], sv['fo'], sv['vm'], q['gp'],
                                                             q['wcat'], q['wcat_t'], q['bz'], pavg,
                                                             g['w_out', l].reshape(D, D).T, f"{l}")
        big['w_out'] = tn_matmul_scatter(me_arr, sv['y'][None], dyo[None], l, DEPTH, big['w_out'], f"dw_out_{l}",
                                         split=NDEV)
        dxl, dgl, dlp, dwa, dwx = lru_bwd(dylru, sv['proj'], sv['xc'], sv['hst'], q['lp'], q['wa_t'], q['wx_t'], f"{l}")
        dproj = jnp.concatenate([dxl, dgl, duv], axis=0)
        dx1, accmi = mix_in_bwd(dproj, sv['x1'], dx2, sv['vm'], g['w_in', l], f"{l}")
        big['w_in'] = tn_matmul_scatter(me_arr, sv['hm'][None], dproj, l, DEPTH, big['w_in'], f"dw_in_{l}")
        dx0, dgu, a, df, acc1 = ffn_bwd(dx1, sv['x0'], sv['gu1'], sv['f1'], sv['v1'],
                                        g['gu', l], g['down', l], f"a{l}")
        big['gu'] = tn_matmul_scatter(me_arr, dgu, sv['h1'][None], l, 2 * DEPTH, big['gu'], f"dw_gu_a{l}")
        big['down'] = tn_matmul_scatter(me_arr, a, df[None], l, 2 * DEPTH, big['down'], f"dw_down_a{l}", split=2)
        dx = dx0
        dmod[l] = jnp.concatenate([acc1[0:3], accmi[0:2], accmo[2:3], acc2[0:3]], axis=0)
        small['ffn1_norm'][l] = acc1[3]
        small['mix_norm'][l] = accmi[3]
        small['ffn2_norm'][l] = acc2[3]
        small['conv_w'][l] = dlp[0:4]
        small['conv_b'][l] = dlp[4]
        small['gate_a_b'][l] = dlp[5].reshape(HEADS, HD)
        small['gate_x_b'][l] = dlp[6].reshape(HEADS, HD)
        small['lru_lambda'][l] = dlp[7]
        small['gate_a_w'][l] = _block_diag_extract(dwa)
        small['gate_x_w'][l] = _block_diag_extract(dwx)
        small['v_norm'][l] = dgp[0]
        small['lru_out_norm'][l] = dgp[1]
        small['gmlp_out_norm'][l] = dgp[2]
        small['spatial_w'][l] = dwm.reshape(HEADS, CHUNK, CHUNK) * tril
        small['spatial_b'][l] = dbz.reshape(CHUNK, HEADS, HD).sum(-1).T
    small = {k: jnp.stack(v) for k, v in small.items()}
    small['final_norm'] = dfin[0]
    return loss, dx, big, small, jnp.stack(dmod)


def ada_fwd(c_all, w_ada, b_loc):
    def body(c_ref, w_ref, b_ref, mod_ref, sc_ref):
        cv = c_ref[...]
        sc = cv * _sigmoid(cv)
        sc_ref[...] = sc
        mod_ref[...] = _dot3(sc, w_ref[...]) + b_ref[...]

    return pl.pallas_call(
        body, name="ada_fwd",
        grid=(DEPTH,),
        in_specs=[pl.BlockSpec((NDEV, D), lambda l: (0, 0)), pl.BlockSpec((None, D, AC), lambda l: (l, 0, 0)),
                  pl.BlockSpec((None, 1, AC), lambda l: (l, 0, 0))],
        out_specs=[pl.BlockSpec((None, NDEV, AC), lambda l: (l, 0, 0)), pl.BlockSpec((NDEV, D), lambda l: (0, 0))],
        out_shape=[jax.ShapeDtypeStruct((DEPTH, NDEV, AC), F32), jax.ShapeDtypeStruct((NDEV, D), F32)],
        compiler_params=_cparams("arbitrary"),
    )(c_all, w_ada, b_loc)


def ada_bwd(sc_t, dmod_cols):
    def body(sc_ref, dm_ref, g_ref):
        sc = sc_ref[...]
        dm = dm_ref[...]
        acc = sc[:, 0:1] * dm[0:1, :]
        for b in range(1, NDEV):
            acc = acc + sc[:, b:b + 1] * dm[b:b + 1, :]
        g_ref[...] = acc

    return pl.pallas_call(
        body, name="ada_bwd",
        grid=(DEPTH,),
        in_specs=[pl.BlockSpec((D, NDEV), lambda l: (0, 0)), pl.BlockSpec((None, NDEV, AC), lambda l: (l, 0, 0))],
        out_specs=pl.BlockSpec((None, None, D, AC), lambda l: (0, l, 0, 0)),
        out_shape=jax.ShapeDtypeStruct((1, DEPTH, D, AC), F32),
        compiler_params=_cparams("arbitrary"),
    )(sc_t, dmod_cols)


def _row_tile(rows, cols):
    if rows * cols <= 512 * 1024:
        return rows
    for tr in (512, 384, 352, 256, 128, 64, 32, 16, 8):
        if rows % tr == 0:
            return tr
    return rows


def adamw(gparts, slot0, w, m, v, name):
    P, _, R, C = gparts.shape
    L = w.shape[0]
    tr = _row_tile(R, C)

    def body(g_ref, w_ref, m_ref, v_ref, go_ref, do_ref, mo_ref, vo_ref):
        g = g_ref[0].astype(F32)
        for p in range(1, P):
            g = g + g_ref[p].astype(F32)
        go_ref[...] = g
        mn = ADAM_B1 * m_ref[...] + (1.0 - ADAM_B1) * g
        vn = ADAM_B2 * v_ref[...] + (1.0 - ADAM_B2) * (g * g)
        mo_ref[...] = mn
        vo_ref[...] = vn
        m_hat = mn / (1.0 - ADAM_B1 ** ADAM_STEP)
        v_hat = vn / (1.0 - ADAM_B2 ** ADAM_STEP)
        do_ref[...] = -ADAM_LR * (m_hat / (jnp.sqrt(v_hat) + ADAM_EPS) + ADAM_WD * w_ref[...])

    blk = pl.BlockSpec((None, tr, C), lambda l, i: (l, i, 0))
    return pl.pallas_call(
        body, name=name,
        grid=(L, R // tr),
        in_specs=[pl.BlockSpec((P, None, tr, C), lambda l, i: (0, slot0 + l, i, 0)), blk, blk, blk],
        out_specs=[blk, blk, blk, blk],
        out_shape=[jax.ShapeDtypeStruct((L, R, C), F32)] * 4,
        compiler_params=_cparams("arbitrary", "arbitrary"),
    )(gparts, w, m, v)


def sum_parts(parts):
    P, R, C = parts.shape

    def body(p_ref, o_ref):
        acc = p_ref[0]
        for p in range(1, P):
            acc = acc + p_ref[p]
        o_ref[...] = acc

    return pl.pallas_call(
        body, name="sum_parts",
        in_specs=[pl.BlockSpec(memory_space=pltpu.VMEM)],
        out_specs=pl.BlockSpec(memory_space=pltpu.VMEM),
        out_shape=jax.ShapeDtypeStruct((R, C), F32),
    )(parts)


WEIGHTS = ['w_ada', 'b_ada', 'ffn1_norm', 'ffn1_w_gu', 'ffn1_w_down', 'mix_norm', 'w_in', 'conv_w', 'conv_b',
           'gate_a_w', 'gate_a_b', 'gate_x_w', 'gate_x_b', 'lru_lambda', 'v_norm', 'spatial_w', 'spatial_b',
           'lru_out_norm', 'gmlp_out_norm', 'w_out', 'ffn2_norm', 'ffn2_w_gu', 'ffn2_w_down', 'final_norm']
PACKED = ['b_ada', 'ffn1_norm', 'mix_norm', 'conv_b', 'gate_a_w', 'gate_a_b', 'gate_x_w', 'gate_x_b', 'lru_lambda',
          'v_norm', 'spatial_w', 'spatial_b', 'lru_out_norm', 'gmlp_out_norm', 'ffn2_norm', 'final_norm', 'conv_w']
PACK_LANES = 128
PACK_ROW_ALIGN = 8 * NDEV


PACK_TAIL = 8


def _pack_rows(shapes):
    used = 0
    for k in PACKED:
        size = 1
        for s in shapes[k]:
            size *= s
        used += size // PACK_LANES
    return used, -(-(used + PACK_TAIL) // PACK_ROW_ALIGN) * PACK_ROW_ALIGN


def _pack(d, tail=None):
    parts = [d[k].reshape(-1, PACK_LANES).astype(F32) for k in PACKED]
    used, rows = _pack_rows({k: d[k].shape for k in PACKED})
    parts.append(jnp.zeros((PACK_TAIL, PACK_LANES), F32) if tail is None else tail)
    return jnp.concatenate(parts + [jnp.zeros((rows - used - PACK_TAIL, PACK_LANES), F32)], axis=0)


def _unpack(buf, shapes):
    out, off = {}, 0
    for k in PACKED:
        size = 1
        for s in shapes[k]:
            size *= s
        nrows = size // PACK_LANES
        out[k] = buf[off:off + nrows].reshape(shapes[k])
        off += nrows
    return out


def kernel(x, c, w_ada, b_ada, ffn1_norm, ffn1_w_gu, ffn1_w_down, mix_norm, w_in, conv_w, conv_b, gate_a_w, gate_a_b, gate_x_w, gate_x_b, lru_lambda, v_norm, spatial_w, spatial_b, lru_out_norm, gmlp_out_norm, w_out, ffn2_norm, ffn2_w_gu, ffn2_w_down, final_norm, loss_target, m_w_ada, m_b_ada, m_ffn1_norm, m_ffn1_w_gu, m_ffn1_w_down, m_mix_norm, m_w_in, m_conv_w, m_conv_b, m_gate_a_w, m_gate_a_b, m_gate_x_w, m_gate_x_b, m_lru_lambda, m_v_norm, m_spatial_w, m_spatial_b, m_lru_out_norm, m_gmlp_out_norm, m_w_out, m_ffn2_norm, m_ffn2_w_gu, m_ffn2_w_down, m_final_norm, v_w_ada, v_b_ada, v_ffn1_norm, v_ffn1_w_gu, v_ffn1_w_down, v_mix_norm, v_w_in, v_conv_w, v_conv_b, v_gate_a_w, v_gate_a_b, v_gate_x_w, v_gate_x_b, v_lru_lambda, v_v_norm, v_spatial_w, v_spatial_b, v_lru_out_norm, v_gmlp_out_norm, v_w_out, v_ffn2_norm, v_ffn2_w_gu, v_ffn2_w_down, v_final_norm):
    w = dict(w_ada=w_ada, b_ada=b_ada, ffn1_norm=ffn1_norm, ffn1_w_gu=ffn1_w_gu, ffn1_w_down=ffn1_w_down, mix_norm=mix_norm, w_in=w_in, conv_w=conv_w, conv_b=conv_b, gate_a_w=gate_a_w, gate_a_b=gate_a_b, gate_x_w=gate_x_w, gate_x_b=gate_x_b, lru_lambda=lru_lambda, v_norm=v_norm, spatial_w=spatial_w, spatial_b=spatial_b, lru_out_norm=lru_out_norm, gmlp_out_norm=gmlp_out_norm, w_out=w_out, ffn2_norm=ffn2_norm, ffn2_w_gu=ffn2_w_gu, ffn2_w_down=ffn2_w_down, final_norm=final_norm)
    m = dict(w_ada=m_w_ada, b_ada=m_b_ada, ffn1_norm=m_ffn1_norm, ffn1_w_gu=m_ffn1_w_gu, ffn1_w_down=m_ffn1_w_down, mix_norm=m_mix_norm, w_in=m_w_in, conv_w=m_conv_w, conv_b=m_conv_b, gate_a_w=m_gate_a_w, gate_a_b=m_gate_a_b, gate_x_w=m_gate_x_w, gate_x_b=m_gate_x_b, lru_lambda=m_lru_lambda, v_norm=m_v_norm, spatial_w=m_spatial_w, spatial_b=m_spatial_b, lru_out_norm=m_lru_out_norm, gmlp_out_norm=m_gmlp_out_norm, w_out=m_w_out, ffn2_norm=m_ffn2_norm, ffn2_w_gu=m_ffn2_w_gu, ffn2_w_down=m_ffn2_w_down, final_norm=m_final_norm)
    v = dict(w_ada=v_w_ada, b_ada=v_b_ada, ffn1_norm=v_ffn1_norm, ffn1_w_gu=v_ffn1_w_gu, ffn1_w_down=v_ffn1_w_down, mix_norm=v_mix_norm, w_in=v_w_in, conv_w=v_conv_w, conv_b=v_conv_b, gate_a_w=v_gate_a_w, gate_a_b=v_gate_a_b, gate_x_w=v_gate_x_w, gate_x_b=v_gate_x_b, lru_lambda=v_lru_lambda, v_norm=v_v_norm, spatial_w=v_spatial_w, spatial_b=v_spatial_b, lru_out_norm=v_lru_out_norm, gmlp_out_norm=v_gmlp_out_norm, w_out=v_w_out, ffn2_norm=v_ffn2_norm, ffn2_w_gu=v_ffn2_w_gu, ffn2_w_down=v_ffn2_w_down, final_norm=v_final_norm)
    me = 4 * lax.axis_index("x") + 2 * lax.axis_index("y") + lax.axis_index("c")

    loc = dict(gu=jnp.concatenate([ffn1_w_gu, ffn2_w_gu], axis=0).astype(BF16),
               down=jnp.concatenate([ffn1_w_down, ffn2_w_down], axis=0).astype(BF16),
               w_in=w_in.astype(BF16), w_out=w_out.astype(BF16))
    c_g, conv_g, gu0, down0 = all_gather([(c, None), (conv_w, None), (loc['gu'], 0), (loc['down'], 0)], "gather_first")
    conv_w_full = conv_g.transpose(1, 2, 0, 3).reshape(DEPTH, CONV_WIDTH, LW)

    b_loc = lax.dynamic_slice(b_ada, (0, me * AC), (DEPTH, AC)).reshape(DEPTH, 1, AC)
    mod_cols, sc_all = ada_fwd(c_g.reshape(NDEV, D), w_ada, b_loc)
    (mod_rows,) = all_to_all([mod_cols.transpose(1, 0, 2)], "scatter_mod")
    mod = mod_rows.transpose(1, 0, 2).reshape(DEPTH, NMOD, D)

    small_w = {k: w[k] for k in PACKED if k != 'conv_w'}
    me_arr = jnp.reshape(me, (1,)).astype(jnp.int32)
    loss_loc, dx, big, small_g, dmod = local_fwd_bwd(me_arr, x[0], loss_target[0], mod, small_w, loc,
                                                     {('gu', 0): gu0, ('down', 0): down0}, conv_w_full)

    small_g['b_ada'] = dmod.reshape(DEPTH, NMOD * D)
    first = (lax.broadcasted_iota(jnp.int32, (PACK_TAIL, PACK_LANES), 0)
             + lax.broadcasted_iota(jnp.int32, (PACK_TAIL, PACK_LANES), 1)) == 0
    gpack = _pack(small_g, jnp.where(first, loss_loc, 0.0))
    rows = gpack.shape[0]
    dmod_out = dmod.reshape(DEPTH, NDEV, AC).transpose(1, 0, 2)
    dmod_r, pack_r = all_to_all([dmod_out, gpack.reshape(NDEV, rows // NDEV, PACK_LANES)], "scatter_grads")
    (gsum_g,) = all_gather([(sum_parts(pack_r), None)], "gather_small_grads")
    gsum = gsum_g.reshape(1, 1, rows, PACK_LANES)
    loss = gsum[0, 0, _pack_rows({k: small_g[k].shape for k in PACKED})[0], 0]

    res = {}
    t = lambda a: a.transpose(0, 2, 1)
    gu_t = big['gu']
    res['ffn1_w_gu'] = tuple(t(r) for r in adamw(gu_t, 0, t(w['ffn1_w_gu']), t(m['ffn1_w_gu']), t(v['ffn1_w_gu']),
                                                 "adamw_gu_a"))
    res['ffn2_w_gu'] = tuple(t(r) for r in adamw(gu_t, DEPTH, t(w['ffn2_w_gu']), t(m['ffn2_w_gu']),
                                                 t(v['ffn2_w_gu']), "adamw_gu_b"))
    res['ffn1_w_down'] = adamw(big['down'], 0, w['ffn1_w_down'], m['ffn1_w_down'], v['ffn1_w_down'], "adamw_down_a")
    res['ffn2_w_down'] = adamw(big['down'], DEPTH, w['ffn2_w_down'], m['ffn2_w_down'], v['ffn2_w_down'], "adamw_down_b")
    res['w_in'] = adamw(big['w_in'], 0, w['w_in'], m['w_in'], v['w_in'], "adamw_w_in")
    res['w_out'] = adamw(big['w_out'], 0, w['w_out'], m['w_out'], v['w_out'], "adamw_w_out")
    g_ada = ada_bwd(sc_all.T, dmod_r.transpose(1, 0, 2))
    res['w_ada'] = adamw(g_ada, 0, w['w_ada'], m['w_ada'], v['w_ada'], "adamw_w_ada")
    shapes = {k: w[k].shape for k in PACKED}
    shapes['conv_w'] = (DEPTH, CONV_WIDTH, LW)
    dummy = jnp.zeros(shapes['conv_w'], F32)
    packs = adamw(gsum, 0, _pack({**small_w, 'conv_w': dummy})[None], _pack({**{k: m[k] for k in small_w}, 'conv_w': dummy})[None],
                  _pack({**{k: v[k] for k in small_w}, 'conv_w': dummy})[None], "adamw_small")
    unpacked = [_unpack(b[0], shapes) for b in packs]
    for k in small_w:
        res[k] = tuple(u[k] for u in unpacked)
    gconv = lax.dynamic_slice(unpacked[0]['conv_w'], (0, 0, me * (LW // NDEV)), (DEPTH, CONV_WIDTH, LW // NDEV))
    cshape = (1, DEPTH * CONV_WIDTH, LW // NDEV)
    rc = adamw(gconv.reshape((1,) + cshape), 0, conv_w.reshape(cshape), m['conv_w'].reshape(cshape),
               v['conv_w'].reshape(cshape), "adamw_conv_w")
    res['conv_w'] = tuple(r.reshape(conv_w.shape) for r in rc)

    return (loss, dx[None], *[res[k][0] for k in WEIGHTS], *[res[k][1] for k in WEIGHTS],
            *[res[k][2] for k in WEIGHTS], *[res[k][3] for k in WEIGHTS])
```

```python
import jax
import jax.numpy as jnp
from jax import lax
from jax.experimental import pallas as pl
from jax.experimental.pallas import tpu as pltpu

F32 = jnp.float32
BF16 = jnp.bfloat16

NDEV = 8
DEPTH = 2
D = 1024
DFF = 2816
FC = 2 * DFF // NDEV
NCHUNK = DFF // FC
DR = DFF // NDEV
LW = 512
GW = 512
HD = 64
HEADS = 8
CHUNK = 128
PC = 2 * (LW + GW) // NDEV
OR = D // NDEV
NMOD = 9
AC = NMOD * D // NDEV
LC = 128
EPS = 1e-6
RG_LRU_C = 8.0
CONV_WIDTH = 4

ADAM_LR = 0.001
ADAM_B1 = 0.9
ADAM_B2 = 0.999
ADAM_EPS = 1e-08
ADAM_WD = 0.01
ADAM_STEP = 10

VMEM_LIMIT_BYTES = 60 * 1024 * 1024
MESH = pl.DeviceIdType.MESH
ANY = pl.BlockSpec(memory_space=pl.ANY)


def _cparams(*sem):
    return pltpu.CompilerParams(dimension_semantics=tuple(sem) if sem else None,
                                vmem_limit_bytes=VMEM_LIMIT_BYTES)


def _dot(a, b):
    return jnp.dot(a, b, preferred_element_type=F32)


def _dot_nt(a, b):
    return lax.dot_general(a, b, (((1,), (1,)), ((), ())), preferred_element_type=F32)


def _dot_tn(a, b):
    return lax.dot_general(a, b, (((0,), (0,)), ((), ())), preferred_element_type=F32)


def _split(a):
    hi = a.astype(BF16)
    lo = (a - hi.astype(F32)).astype(BF16)
    return hi, lo


def _dot3(a, b):
    ah, al = _split(a)
    bh, bl = _split(b)
    return _dot(ah, bh) + (_dot(ah, bl) + _dot(al, bh))


def _csum(a):
    return jnp.sum(a, axis=0, keepdims=True)


def _rmean(a):
    return jnp.mean(a, axis=-1, keepdims=True)


def _sigmoid(a):
    return 1.0 / (1.0 + jnp.exp(-a))


_GELU_K = 0.7978845608028654
_GELU_C = 0.044715


def _gelu(a):
    return 0.5 * a * (1.0 + jnp.tanh(_GELU_K * (a + _GELU_C * a * a * a)))


def _gelu_and_grad(a):
    a2 = a * a
    t = jnp.tanh(_GELU_K * (a + _GELU_C * a2 * a))
    half = 0.5 * (1.0 + t)
    return a * half, half + 0.5 * a * (1.0 - t * t) * (_GELU_K * (1.0 + 3.0 * _GELU_C * a2))


def _norm_mod(x, gain, scale, shift):
    rstd = lax.rsqrt(_rmean(x * x) + EPS)
    return (x * rstd * gain) * (1.0 + scale) + shift


def _norm_mod_bwd(dh, x, gain, scale):
    rstd = lax.rsqrt(_rmean(x * x) + EPS)
    xhat = x * rstd
    dshift = _csum(dh)
    dscale = _csum(dh * (xhat * gain))
    dhn = dh * (1.0 + scale)
    dgain = _csum(dhn * xhat)
    dxhat = dhn * gain
    dx = rstd * (dxhat - xhat * _rmean(dxhat * xhat))
    return dx, dshift, dscale, dgain


def _rms(x, gain):
    rstd = lax.rsqrt(_rmean(x * x) + EPS)
    return x * rstd * gain


def _rms_bwd(dy, x, gain):
    rstd = lax.rsqrt(_rmean(x * x) + EPS)
    xhat = x * rstd
    dgain = _csum(dy * xhat)
    dxhat = dy * gain
    return rstd * (dxhat - xhat * _rmean(dxhat * xhat)), dgain


PAIR = 2 * HD


def _seg_mean(a, pavg):
    hi, lo = _split(a)
    return jnp.concatenate([_dot(hi[:, p:p + PAIR], pavg) + _dot(lo[:, p:p + PAIR], pavg)
                            for p in range(0, a.shape[1], PAIR)], axis=1)


def _block_copies(src_hbm, dst_vmem, sems, rows):
    copies = []
    for k in range(NDEV):
        dst = dst_vmem.at[k] if rows is None else dst_vmem.at[pl.ds(k * rows, rows)]
        copies.append(pltpu.make_async_copy(src_hbm.at[k], dst, sems.at[k]))
    return copies


def _ffn_weight_fetch(wgu_hbm, wd_hbm, wgu_v, wd_v, sems):
    @pl.when(pl.program_id(0) == 0)
    def _():
        copies = _block_copies(wgu_hbm, wgu_v, sems.at[0], None) + _block_copies(wd_hbm, wd_v, sems.at[1], DR)
        for cp in copies:
            cp.start()
        for cp in copies:
            cp.wait()


def _place():
    return lax.axis_index("x"), lax.axis_index("y"), lax.axis_index("c")


def _slot(p):
    return 4 * p[0] + 2 * p[1] + p[2]


class GatherRide:
    def __init__(self, srcs):
        self.n = len(srcs)
        self.index = [i for _, i in srcs]
        self.args = [a for a, _ in srcs]
        self.out_shape = [jax.ShapeDtypeStruct((NDEV,) + (a.shape if i is None else a.shape[1:]), a.dtype)
                          for a, i in srcs]
        self.scratch = [pltpu.SemaphoreType.DMA((self.n, NDEV - 1)), pltpu.SemaphoreType.DMA((self.n, NDEV - 1)),
                        pltpu.SemaphoreType.DMA((self.n,))]

    def hooks(self, ins, outs, sems):
        send_sems, recv_sems, local_sems = sems
        n = self.n
        x, y, c = _place()
        me, sibling = (x, y, c), (x, y, 1 - c)
        chips = [(1 - x, y), (x, 1 - y), (1 - x, 1 - y)]

        def local(a):
            return ins[a] if self.index[a] is None else ins[a].at[self.index[a]]

        def copy(a, k, block, to, src=None):
            dst = outs[a].at[_slot(block)]
            return pltpu.make_async_remote_copy(
                src_ref=dst if src is None else src, dst_ref=dst,
                send_sem=send_sems.at[a, k], recv_sem=recv_sems.at[a, k],
                device_id=to, device_id_type=MESH)

        def mine():
            return [pltpu.make_async_copy(local(a), outs[a].at[_slot(me)], local_sems.at[a]) for a in range(n)]

        def first():
            cps = []
            for a in range(n):
                cps.append(copy(a, 0, me, sibling, src=local(a)))
                cps += [copy(a, 1 + j, me, (*chip, c), src=local(a)) for j, chip in enumerate(chips)]
            return cps

        def passed():
            return [copy(a, 4 + j, (*chip, c), sibling) for j, chip in enumerate(chips) for a in range(n)]

        def start():
            for cp in mine() + first():
                cp.start()

        def mid():
            for j, chip in enumerate(chips):
                for a in range(n):
                    copy(a, 1 + j, (*chip, c), me).wait_recv()
                    copy(a, 4 + j, (*chip, c), sibling).start()

        def finish():
            for a in range(n):
                copy(a, 0, sibling, me).wait_recv()
                for j, chip in enumerate(chips):
                    copy(a, 4 + j, (*chip, 1 - c), me).wait_recv()
            for cp in first() + passed():
                cp.wait_send()
            for cp in mine():
                cp.wait()

        return start, mid, finish


def all_gather(srcs, name):
    ride = GatherRide(srcs)
    n = ride.n

    def body(*refs):
        start, mid, finish = ride.hooks(refs[:n], refs[n:2 * n], refs[2 * n:])
        start()
        mid()
        finish()

    return pl.pallas_call(
        body, name=name,
        in_specs=[ANY] * n, out_specs=[ANY] * n, out_shape=ride.out_shape, scratch_shapes=ride.scratch,
    )(*ride.args)


def _call(core, ride, *, name, grid, in_specs, out_specs, out_shape, scratch_shapes, args):
    if ride is None:
        outs = pl.pallas_call(core, name=name, grid=grid, in_specs=in_specs, out_specs=out_specs,
                              out_shape=out_shape, scratch_shapes=scratch_shapes,
                              compiler_params=_cparams("arbitrary"))(*args)
        return outs, []
    n_in, n_out, n_sc, n = len(in_specs), len(out_shape), len(scratch_shapes), ride.n
    nsteps = grid[0]
    mid_step = max(nsteps - 2, 0)

    def body(*refs):
        cuts = [n_in, n_in + n, n_in + n + n_out, n_in + 2 * n + n_out, n_in + 2 * n + n_out + n_sc]
        ci, ri, co, ro, cs, rs = (refs[a:b] for a, b in zip([0] + cuts, cuts + [len(refs)]))
        start, mid, finish = ride.hooks(ri, ro, rs)
        i = pl.program_id(0)
        pl.when(i == 0)(start)
        core(*ci, *co, *cs)
        pl.when(i == mid_step)(mid)
        pl.when(i == nsteps - 1)(finish)

    outs = pl.pallas_call(
        body, name=name, grid=grid,
        in_specs=list(in_specs) + [ANY] * n, out_specs=list(out_specs) + [ANY] * n,
        out_shape=list(out_shape) + ride.out_shape, scratch_shapes=list(scratch_shapes) + ride.scratch,
        compiler_params=_cparams("arbitrary"))(*args, *ride.args)
    return outs[:n_out], outs[n_out:]


def all_to_all(arrs, name):
    n = len(arrs)

    def body(*refs):
        ins, outs = refs[:n], refs[n:2 * n]
        send_sems, recv_sems, local_sems = refs[2 * n:]
        x, y, c = _place()
        me = (x, y, c)

        def peer(k):
            return (1 - x if k & 4 else x, 1 - y if k & 2 else y, 1 - c if k & 1 else c)

        def copy(a, k):
            return pltpu.make_async_remote_copy(
                src_ref=ins[a].at[_slot(peer(k))], dst_ref=outs[a].at[_slot(me)],
                send_sem=send_sems.at[a, k - 1], recv_sem=recv_sems.at[a, k - 1],
                device_id=peer(k), device_id_type=MESH)

        def landing(a, k):
            return pltpu.make_async_remote_copy(
                src_ref=outs[a].at[_slot(peer(k))], dst_ref=outs[a].at[_slot(peer(k))],
                send_sem=send_sems.at[a, k - 1], recv_sem=recv_sems.at[a, k - 1],
                device_id=me, device_id_type=MESH)

        mine = [pltpu.make_async_copy(ins[a].at[_slot(me)], outs[a].at[_slot(me)], local_sems.at[a]) for a in range(n)]
        for cp in mine:
            cp.start()
        sends = [copy(a, k) for a in range(n) for k in range(1, NDEV)]
        for cp in sends:
            cp.start()
        for a in range(n):
            for k in range(1, NDEV):
                landing(a, k).wait_recv()
        for cp in sends:
            cp.wait_send()
        for cp in mine:
            cp.wait()

    return pl.pallas_call(
        body, name=name,
        in_specs=[ANY] * n, out_specs=[ANY] * n,
        out_shape=[jax.ShapeDtypeStruct(a.shape, a.dtype) for a in arrs],
        scratch_shapes=[pltpu.SemaphoreType.DMA((n, NDEV - 1)), pltpu.SemaphoreType.DMA((n, NDEV - 1)),
                        pltpu.SemaphoreType.DMA((n,))],
    )(*arrs)


FFN_TS = 256
FFN_FWD_TS = 512


def ffn_fwd(x, vec, wgu_g, wdown_g, tag, ride=None):
    S = x.shape[0]
    ts = min(FFN_FWD_TS, S)

    def body(x_ref, vec_ref, wgu_hbm, wd_hbm, xo_ref, h_ref, gu_ref, f_ref, wgu_v, wd_v, sems):
        _ffn_weight_fetch(wgu_hbm, wd_hbm, wgu_v, wd_v, sems)
        xv = x_ref[...]
        h = _norm_mod(xv, vec_ref[3:4, :], vec_ref[1:2, :], vec_ref[0:1, :]).astype(BF16)
        h_ref[...] = h
        acc = jnp.zeros((ts, D), F32)
        for j in range(NCHUNK):
            g = _dot(h, wgu_v[j])
            u = _dot(h, wgu_v[NCHUNK + j])
            gu_ref[j] = g.astype(BF16)
            gu_ref[NCHUNK + j] = u.astype(BF16)
            a = (g * _sigmoid(g) * u).astype(BF16)
            acc = acc + _dot(a, wd_v[pl.ds(j * FC, FC), :])
        f_ref[...] = acc.astype(BF16)
        xo_ref[...] = xv + (0.5 * vec_ref[2:3, :]) * acc

    return _call(
        body, ride, name=f"ffn_fwd_{tag}",
        grid=(S // ts,),
        in_specs=[pl.BlockSpec((ts, D), lambda i: (i, 0)),
                  pl.BlockSpec((8, D), lambda i: (0, 0)), ANY, ANY],
        out_specs=[pl.BlockSpec((ts, D), lambda i: (i, 0)),
                   pl.BlockSpec((ts, D), lambda i: (i, 0)),
                   pl.BlockSpec((NDEV, ts, FC), lambda i: (0, i, 0)),
                   pl.BlockSpec((ts, D), lambda i: (i, 0))],
        out_shape=[jax.ShapeDtypeStruct((S, D), F32), jax.ShapeDtypeStruct((S, D), BF16),
                   jax.ShapeDtypeStruct((NDEV, S, FC), BF16), jax.ShapeDtypeStruct((S, D), BF16)],
        scratch_shapes=[pltpu.VMEM((NDEV, D, FC), BF16), pltpu.VMEM((DFF, D), BF16),
                        pltpu.SemaphoreType.DMA((2, NDEV))],
        args=(x, vec, wgu_g, wdown_g))


def ffn_bwd(dxo, x, gu, f, vec, wgu_g, wdown_g, tag):
    S = x.shape[0]
    ts = min(FFN_TS, S)

    def body(dxo_ref, x_ref, gu_ref, f_ref, vec_ref, wgu_hbm, wd_hbm,
             dx_ref, dgu_ref, a_ref, df_ref, acc_ref, wgu_v, wd_v, sems):
        _ffn_weight_fetch(wgu_hbm, wd_hbm, wgu_v, wd_v, sems)

        @pl.when(pl.program_id(0) == 0)
        def _():
            acc_ref[...] = jnp.zeros_like(acc_ref)

        dxo_v = dxo_ref[...]
        dgate = 0.5 * _csum(dxo_v * f_ref[...].astype(F32))
        df = ((0.5 * vec_ref[2:3, :]) * dxo_v).astype(BF16)
        df_ref[...] = df
        dh = jnp.zeros((ts, D), F32)
        for j in range(NCHUNK):
            da = _dot_nt(df, wd_v[pl.ds(j * FC, FC), :])
            g = gu_ref[j].astype(F32)
            u = gu_ref[NCHUNK + j].astype(F32)
            sg = _sigmoid(g)
            si = g * sg
            a_ref[j] = (si * u).astype(BF16)
            dg = (da * u * (sg * (1.0 + g * (1.0 - sg)))).astype(BF16)
            du = (da * si).astype(BF16)
            dgu_ref[j] = dg
            dgu_ref[NCHUNK + j] = du
            dh = dh + _dot_nt(dg, wgu_v[j]) + _dot_nt(du, wgu_v[NCHUNK + j])
        dx, dshift, dscale, dgain = _norm_mod_bwd(dh, x_ref[...], vec_ref[3:4, :], vec_ref[1:2, :])
        dx_ref[...] = dx + dxo_v
        acc_ref[0:1, :] += dshift
        acc_ref[1:2, :] += dscale
        acc_ref[2:3, :] += dgate
        acc_ref[3:4, :] += dgain

    row = pl.BlockSpec((ts, D), lambda i: (i, 0))
    return pl.pallas_call(
        body, name=f"ffn_bwd_{tag}",
        grid=(S // ts,),
        in_specs=[row, row, pl.BlockSpec((NDEV, ts, FC), lambda i: (0, i, 0)), row,
                  pl.BlockSpec((8, D), lambda i: (0, 0)), ANY, ANY],
        out_specs=[row, pl.BlockSpec((NDEV, ts, FC), lambda i: (0, i, 0)),
                   pl.BlockSpec((NCHUNK, ts, FC), lambda i: (0, i, 0)), row,
                   pl.BlockSpec((8, D), lambda i: (0, 0))],
        out_shape=[jax.ShapeDtypeStruct((S, D), F32), jax.ShapeDtypeStruct((NDEV, S, FC), BF16),
                   jax.ShapeDtypeStruct((NCHUNK, S, FC), BF16), jax.ShapeDtypeStruct((S, D), BF16),
                   jax.ShapeDtypeStruct((8, D), F32)],
        scratch_shapes=[pltpu.VMEM((NDEV, D, FC), BF16), pltpu.VMEM((DFF, D), BF16),
                        pltpu.SemaphoreType.DMA((2, NDEV))],
        compiler_params=_cparams("arbitrary"),
    )(dxo, x, gu, f, vec, wgu_g, wdown_g)


NCHIP = NDEV // 2


def tn_matmul_scatter(me_arr, a, b, slot, nslots, prev, name, split=1):
    na, S, M = a.shape
    nb, _, N = b.shape
    ncall = NDEV // split
    ts = min(4096, S)
    nsteps = S // ts
    mp = M // split
    other_step = {1: lambda j: 2 * j, 2: lambda j: j, 8: lambda j: 0}[split]
    mine_step = {1: lambda j: 2 * j + 1, 2: lambda j: j, 8: lambda j: 0}[split]

    def group(k, me_ref):
        if split == 1:
            return jnp.bitwise_xor(me_ref[0], NDEV - 1 - k)
        if split == 2:
            return jnp.bitwise_xor(me_ref[0] // 2, NCHIP - 1 - k)
        return 0

    def body(me_ref, *refs):
        a_ref, b_ref = refs[0], refs[1]
        recv_ref, acc, sb_other, sb_mine, land, d2d_send, d2d_recv, ici_send, ici_recv = refs[-9:]
        k = pl.program_id(0)
        s = pl.program_id(1)
        x, y, c = _place()
        my_chip = 2 * x + y

        def chip_of(j):
            if split == 8:
                cx, cy = j // 2, j % 2
            else:
                flip = NCHIP - 1 - j
                cx, cy = (1 - x if flip & 2 else x), (1 - y if flip & 1 else y)
            return cx, cy, 2 * cx + cy

        def piece(j, core):
            if split == 1:
                return acc[...]
            start = core * mp if split == 2 else (2 * j + core) * mp
            return acc[pl.ds(pl.multiple_of(start, 8), mp), :]

        def to_sibling(j):
            return pltpu.make_async_remote_copy(
                src_ref=sb_other.at[j], dst_ref=land.at[j], send_sem=d2d_send.at[j], recv_sem=d2d_recv.at[j],
                device_id=(x, y, 1 - c), device_id_type=MESH)

        def to_owner(j):
            cx, cy, ci = chip_of(j)
            dst = recv_ref.at[my_chip, slot]
            return ci, pltpu.make_async_copy(sb_mine.at[j], dst, ici_send.at[j]), pltpu.make_async_remote_copy(
                src_ref=sb_mine.at[j], dst_ref=dst, send_sem=ici_send.at[j], recv_sem=ici_recv.at[my_chip],
                device_id=(cx, cy, c), device_id_type=MESH)

        if nsteps == 1:
            acc[...] = _dot_tn(a_ref[...], b_ref[...])
        else:
            @pl.when(s == 0)
            def _():
                acc[...] = jnp.zeros_like(acc)

            acc[...] += _dot_tn(a_ref[...], b_ref[...])

        for kk in range(ncall):
            @pl.when((s == nsteps - 1) & (k == kk))
            def _():
                for j in range(NCHIP):
                    if other_step(j) == kk:
                        sb_other[j] = piece(j, 1 - c).astype(BF16)
                        to_sibling(j).start()
                for j in range(NCHIP):
                    if mine_step(j) == kk:
                        to_sibling(j).wait_recv()
                        sb_mine[j] = (piece(j, c) + land[j].astype(F32)).astype(BF16)
                        ci, loc, rem = to_owner(j)
                        pl.when(ci == my_chip)(loc.start)
                        pl.when(ci != my_chip)(rem.start)

        @pl.when((s == nsteps - 1) & (k == ncall - 1))
        def _():
            for j in range(NCHIP):
                to_sibling(j).wait_send()
                ci, loc, rem = to_owner(j)
                pl.when(ci == my_chip)(loc.wait)
                pl.when(ci != my_chip)(rem.wait_send)
            for src in range(NCHIP):
                @pl.when(my_chip != src)
                def _():
                    pltpu.make_async_remote_copy(
                        src_ref=recv_ref.at[src, slot], dst_ref=recv_ref.at[src, slot],
                        send_sem=ici_send.at[src], recv_sem=ici_recv.at[src],
                        device_id=(src // 2, src % 2, c), device_id_type=MESH).wait_recv()

    in_specs = [pl.BlockSpec((None, ts, M), (lambda k, s, me: (group(k, me), s, 0)) if na > 1 else (lambda k, s, me: (0, s, 0))),
                pl.BlockSpec((None, ts, N), (lambda k, s, me: (group(k, me), s, 0)) if nb > 1 else (lambda k, s, me: (0, s, 0)))]
    args = [me_arr, a, b]
    aliases = {}
    if prev is not None:
        in_specs.append(ANY)
        args.append(prev)
        aliases = {3: 0}
    return pl.pallas_call(
        body, name=name,
        grid_spec=pltpu.PrefetchScalarGridSpec(
            num_scalar_prefetch=1, grid=(ncall, nsteps), in_specs=in_specs, out_specs=ANY,
            scratch_shapes=[pltpu.VMEM((M, N), F32), pltpu.VMEM((NCHIP, mp, N), BF16), pltpu.VMEM((NCHIP, mp, N), BF16),
                            pltpu.VMEM((NCHIP, mp, N), BF16), pltpu.SemaphoreType.DMA((NCHIP,)),
                            pltpu.SemaphoreType.DMA((NCHIP,)), pltpu.SemaphoreType.DMA((NCHIP,)),
                            pltpu.SemaphoreType.DMA((NCHIP,))]),
        out_shape=jax.ShapeDtypeStruct((NCHIP, nslots, mp, N), BF16),
        input_output_aliases=aliases,
        compiler_params=_cparams("arbitrary", "arbitrary"),
    )(*args)


MIX_TS = 512
MIX_IN_TS = 512


def mix_in_fwd(x, vec, win_g, tag, ride=None):
    S = x.shape[0]
    ts = min(MIX_IN_TS, S)

    def body(x_ref, vec_ref, win_ref, hm_ref, proj_ref):
        h = _norm_mod(x_ref[...], vec_ref[3:4, :], vec_ref[1:2, :], vec_ref[0:1, :]).astype(BF16)
        hm_ref[...] = h
        for k in range(NDEV):
            proj_ref[k] = _dot(h, win_ref[k])

    return _call(
        body, ride, name=f"mix_in_fwd_{tag}",
        grid=(S // ts,),
        in_specs=[pl.BlockSpec((ts, D), lambda i: (i, 0)), pl.BlockSpec((8, D), lambda i: (0, 0)),
                  pl.BlockSpec((NDEV, D, PC), lambda i: (0, 0, 0))],
        out_specs=[pl.BlockSpec((ts, D), lambda i: (i, 0)),
                   pl.BlockSpec((NDEV, ts, PC), lambda i: (0, i, 0))],
        out_shape=[jax.ShapeDtypeStruct((S, D), BF16), jax.ShapeDtypeStruct((NDEV, S, PC), F32)],
        scratch_shapes=[], args=(x, vec, win_g))


def mix_in_bwd(dproj, x, dxo, vec, win_g, tag):
    S = x.shape[0]
    ts = min(MIX_IN_TS, S)

    def body(dp_ref, x_ref, dxo_ref, vec_ref, win_ref, dx_ref, acc_ref):
        @pl.when(pl.program_id(0) == 0)
        def _():
            acc_ref[...] = jnp.zeros_like(acc_ref)

        dh = jnp.zeros((ts, D), F32)
        for k in range(NDEV):
            dh = dh + _dot_nt(dp_ref[k], win_ref[k])
        dx, dshift, dscale, dgain = _norm_mod_bwd(dh, x_ref[...], vec_ref[3:4, :], vec_ref[1:2, :])
        dx_ref[...] = dx + dxo_ref[...]
        acc_ref[0:1, :] += dshift
        acc_ref[1:2, :] += dscale
        acc_ref[3:4, :] += dgain

    row = pl.BlockSpec((ts, D), lambda i: (i, 0))
    return pl.pallas_call(
        body, name=f"mix_in_bwd_{tag}",
        grid=(S // ts,),
        in_specs=[pl.BlockSpec((NDEV, ts, PC), lambda i: (0, i, 0)), row, row,
                  pl.BlockSpec((8, D), lambda i: (0, 0)),
                  pl.BlockSpec((NDEV, D, PC), lambda i: (0, 0, 0))],
        out_specs=[row, pl.BlockSpec((8, D), lambda i: (0, 0))],
        out_shape=[jax.ShapeDtypeStruct((S, D), F32), jax.ShapeDtypeStruct((8, D), F32)],
        compiler_params=_cparams("arbitrary"),
    )(dproj, x, dxo, vec, win_g)


SCAN_UNROLL = 8


def _shift_down(z, k, row):
    return jnp.where(row >= k, pltpu.roll(z, k, 0), 0.0)


def _shift_up(z, k, row, n):
    return jnp.where(row < n - k, pltpu.roll(z, n - k, 0), 0.0)


def _lru_gates(xc, lp_ref, wa_ref, wx_ref):
    xcb = xc.astype(BF16)
    ra = _sigmoid(_dot(xcb, wa_ref[...]) + lp_ref[5:6, :])
    ix = _sigmoid(_dot(xcb, wx_ref[...]) + lp_ref[6:7, :])
    lam = lp_ref[7:8, :]
    ls = jnp.minimum(lam, 0.0) - jnp.log(1.0 + jnp.exp(-jnp.abs(lam)))
    log_a = (RG_LRU_C * ls) * ra
    a = jnp.exp(log_a)
    mult = jnp.sqrt(-jnp.tanh(log_a) * (a * a + 1.0))
    return ra, ix, ls, a, mult


def _conv(x, lp_ref, row):
    return (lp_ref[4:5, :] + lp_ref[3:4, :] * x + lp_ref[2:3, :] * _shift_down(x, 1, row)
            + lp_ref[1:2, :] * _shift_down(x, 2, row) + lp_ref[0:1, :] * _shift_down(x, 3, row))


def lru_fwd(proj, lp, wa_t, wx_t, tag, ride=None):
    S = proj.shape[1]
    nblk = S // 8

    def body(x_ref, g_ref, lp_ref, wa_ref, wx_ref, y_ref, xc_ref, h_ref, a_s, b_s):
        x = x_ref[...]
        row = lax.broadcasted_iota(jnp.int32, x.shape, 0)
        xc = _conv(x, lp_ref, row)
        xc_ref[...] = xc
        ra, ix, ls, a, mult = _lru_gates(xc, lp_ref, wa_ref, wx_ref)
        a_s[...] = a
        b_s[...] = mult * (ix * xc)
        rowb = lax.broadcasted_iota(jnp.int32, (8, LC), 0)

        def step(i, carry):
            for q in range(SCAN_UNROLL):
                r0 = pl.multiple_of((i * SCAN_UNROLL + q) * 8, 8)
                A = a_s[pl.ds(r0, 8), :]
                B = b_s[pl.ds(r0, 8), :]
                for d in (1, 2, 4):
                    m = rowb >= d
                    As = jnp.where(m, pltpu.roll(A, d, 0), 1.0)
                    Bs = jnp.where(m, pltpu.roll(B, d, 0), 0.0)
                    B = A * Bs + B
                    A = A * As
                H = B + A * carry
                h_ref[pl.ds(r0, 8), :] = H
                carry = H[7:8, :]
            return carry

        lax.fori_loop(0, nblk // SCAN_UNROLL, step, jnp.zeros((1, LC), F32))
        y_ref[...] = h_ref[...] * _gelu(g_ref[...])

    col = pl.BlockSpec((S, LC), lambda c: (0, c))
    return _call(
        body, ride, name=f"lru_fwd_{tag}",
        grid=(LW // LC,),
        in_specs=[pl.BlockSpec((None, S, LC), lambda c: (c // 2, 0, c % 2)),
                  pl.BlockSpec((None, S, LC), lambda c: (2 + c // 2, 0, c % 2)),
                  pl.BlockSpec((8, LC), lambda c: (0, c)),
                  pl.BlockSpec((None, LC, LC), lambda c: (c, 0, 0)),
                  pl.BlockSpec((None, LC, LC), lambda c: (c, 0, 0))],
        out_specs=[col, col, col],
        out_shape=[jax.ShapeDtypeStruct((S, LW), F32)] * 3,
        scratch_shapes=[pltpu.VMEM((S, LC), F32), pltpu.VMEM((S, LC), F32)],
        args=(proj, proj, lp, wa_t, wx_t))


def lru_bwd(dy, proj, xc_all, hst, lp, wa_t, wx_t, tag):
    S = proj.shape[1]
    nblk = S // 8

    def body(dy_ref, x_ref, g_ref, xc_ref, h_ref, lp_ref, wa_ref, wx_ref,
             dx_ref, dg_ref, dlp_ref, dwa_ref, dwx_ref, c_s, l_s):
        xc = xc_ref[...]
        row = lax.broadcasted_iota(jnp.int32, xc.shape, 0)
        ra, ix, ls, a, mult = _lru_gates(xc, lp_ref, wa_ref, wx_ref)
        g = g_ref[...]
        dyv = dy_ref[...]
        h = h_ref[...]
        gelu_g, gelu_grad_g = _gelu_and_grad(g)
        dg_ref[...] = (dyv * h * gelu_grad_g).astype(BF16)
        c_s[...] = _shift_up(a, 1, row, S)
        l_s[...] = dyv * gelu_g
        rowb = lax.broadcasted_iota(jnp.int32, (8, LC), 0)

        def step(i, carry):
            for q in range(SCAN_UNROLL):
                r0 = pl.multiple_of((nblk - 1 - (i * SCAN_UNROLL + q)) * 8, 8)
                C = c_s[pl.ds(r0, 8), :]
                L = l_s[pl.ds(r0, 8), :]
                for d in (1, 2, 4):
                    m = rowb < 8 - d
                    Cs = jnp.where(m, pltpu.roll(C, 8 - d, 0), 1.0)
                    Ls = jnp.where(m, pltpu.roll(L, 8 - d, 0), 0.0)
                    L = C * Ls + L
                    C = C * Cs
                L = L + C * carry
                l_s[pl.ds(r0, 8), :] = L
                carry = L[0:1, :]
            return carry

        lax.fori_loop(0, nblk // SCAN_UNROLL, step, jnp.zeros((1, LC), F32))
        db = l_s[...]
        da = db * _shift_down(h, 1, row)
        ixc = ix * xc
        dmult = db * ixc
        dix = db * (mult * xc)
        dxc = db * (mult * ix)
        dlog_a = da * a - dmult * (a * a) / mult
        dra = dlog_a * (RG_LRU_C * ls)
        dls = _csum(dlog_a * ra) * RG_LRU_C
        lam = lp_ref[7:8, :]
        dlam = dls * _sigmoid(-lam)
        dpa = dra * ra * (1.0 - ra)
        dpx = dix * ix * (1.0 - ix)
        dpab = dpa.astype(BF16)
        dpxb = dpx.astype(BF16)
        xcb = xc.astype(BF16)
        dwa_ref[...] = _dot_tn(xcb, dpab)
        dwx_ref[...] = _dot_tn(xcb, dpxb)
        dxc = dxc + _dot_nt(dpab, wa_ref[...]) + _dot_nt(dpxb, wx_ref[...])
        x = x_ref[...]
        dlp_ref[0:1, :] = _csum(dxc * _shift_down(x, 3, row))
        dlp_ref[1:2, :] = _csum(dxc * _shift_down(x, 2, row))
        dlp_ref[2:3, :] = _csum(dxc * _shift_down(x, 1, row))
        dlp_ref[3:4, :] = _csum(dxc * x)
        dlp_ref[4:5, :] = _csum(dxc)
        dlp_ref[5:6, :] = _csum(dpa)
        dlp_ref[6:7, :] = _csum(dpx)
        dlp_ref[7:8, :] = dlam
        dx = (lp_ref[3:4, :] * dxc + lp_ref[2:3, :] * _shift_up(dxc, 1, row, S)
              + lp_ref[1:2, :] * _shift_up(dxc, 2, row, S) + lp_ref[0:1, :] * _shift_up(dxc, 3, row, S))
        dx_ref[...] = dx.astype(BF16)

    col = pl.BlockSpec((S, LC), lambda c: (0, c))
    pcol = pl.BlockSpec((None, S, LC), lambda c: (c // 2, 0, c % 2))
    return pl.pallas_call(
        body, name=f"lru_bwd_{tag}",
        grid=(LW // LC,),
        in_specs=[col, pcol, pl.BlockSpec((None, S, LC), lambda c: (2 + c // 2, 0, c % 2)), col, col,
                  pl.BlockSpec((8, LC), lambda c: (0, c)),
                  pl.BlockSpec((None, LC, LC), lambda c: (c, 0, 0)),
                  pl.BlockSpec((None, LC, LC), lambda c: (c, 0, 0))],
        out_specs=[pcol, pcol, pl.BlockSpec((8, LC), lambda c: (0, c)),
                   pl.BlockSpec((None, LC, LC), lambda c: (c, 0, 0)),
                   pl.BlockSpec((None, LC, LC), lambda c: (c, 0, 0))],
        out_shape=[jax.ShapeDtypeStruct((2, S, PC), BF16), jax.ShapeDtypeStruct((2, S, PC), BF16),
                   jax.ShapeDtypeStruct((8, LW), F32),
                   jax.ShapeDtypeStruct((LW // LC, LC, LC), F32), jax.ShapeDtypeStruct((LW // LC, LC, LC), F32)],
        scratch_shapes=[pltpu.VMEM((S, LC), F32), pltpu.VMEM((S, LC), F32)],
        compiler_params=_cparams("arbitrary"),
    )(dy, proj, proj, xc_all, hst, lp, wa_t, wx_t)


def _pair_stack(zp, low):
    return jnp.concatenate([jnp.where(low, zp, 0.0), jnp.where(low, 0.0, zp)], axis=0).astype(BF16)


def _spatial(w_ref, zc, low):
    return jnp.concatenate(
        [_dot(w_ref[:, 2 * p * CHUNK:2 * (p + 1) * CHUNK], _pair_stack(zc[:, p * PAIR:(p + 1) * PAIR], low))
         for p in range(GW // PAIR)], axis=1)


def _gmlp_fwd_parts(u, v, gp_ref, wcat_ref, bz_ref, pavg_ref, ts, with_grad=False):
    if with_grad:
        ug, ugrad = _gelu_and_grad(u)
        vg, vgrad = _gelu_and_grad(v)
    else:
        ug, vg, ugrad, vgrad = _gelu(u), _gelu(v), None, None
    pavg = pavg_ref[...]
    vc = vg - _seg_mean(vg, pavg)
    rs = lax.rsqrt(_seg_mean(vc * vc, pavg) + EPS)
    vhat = vc * rs
    vh = vhat * gp_ref[0:1, :]
    low = lax.broadcasted_iota(jnp.int32, (CHUNK, PAIR), 1) < HD
    zs = [_spatial(wcat_ref, vh[n * CHUNK:(n + 1) * CHUNK, :], low) + bz_ref[...] for n in range(ts // CHUNK)]
    z = jnp.concatenate(zs, axis=0) if len(zs) > 1 else zs[0]
    return ug, rs, vhat, vh, z, ugrad, vgrad


def mix_out_fwd(proj, ylru, x, vec, gp, wcat, bz, pavg, wout_g, tag, ride=None):
    S = x.shape[0]
    ts = min(MIX_TS, S)

    def body(u_ref, v_ref, yl_ref, x_ref, vec_ref, gp_ref, wcat_ref, bz_ref, pavg_ref, wout_ref,
             xo_ref, y_ref, fo_ref):
        u = jnp.concatenate([u_ref[0], u_ref[1]], axis=1)
        v = jnp.concatenate([v_ref[0], v_ref[1]], axis=1)
        ug, _, _, _, z, _, _ = _gmlp_fwd_parts(u, v, gp_ref, wcat_ref, bz_ref, pavg_ref, ts)
        n1 = _rms(yl_ref[...], gp_ref[1:2, :])
        n2 = _rms(ug * z, gp_ref[2:3, :])
        y = jnp.concatenate([n1, n2], axis=1).astype(BF16)
        y_ref[...] = y
        fo = _dot(y, wout_ref[...])
        fo_ref[...] = fo.astype(BF16)
        xo_ref[...] = x_ref[...] + vec_ref[2:3, :] * fo

    row = pl.BlockSpec((ts, D), lambda i: (i, 0))
    full = lambda shp: pl.BlockSpec(shp, lambda i: tuple(0 for _ in shp))
    return _call(
        body, ride, name=f"mix_out_fwd_{tag}",
        grid=(S // ts,),
        in_specs=[pl.BlockSpec((2, ts, PC), lambda i: (2, i, 0)), pl.BlockSpec((2, ts, PC), lambda i: (3, i, 0)),
                  pl.BlockSpec((ts, LW), lambda i: (i, 0)), row, full((8, D)), full((8, GW)),
                  full((CHUNK, HEADS * CHUNK)), full((CHUNK, GW)), full((PAIR, PAIR)), full((D, D))],
        out_specs=[row, row, row],
        out_shape=[jax.ShapeDtypeStruct((S, D), F32), jax.ShapeDtypeStruct((S, D), BF16),
                   jax.ShapeDtypeStruct((S, D), BF16)],
        scratch_shapes=[], args=(proj, proj, ylru, x, vec, gp, wcat, bz, pavg, wout_g))


def mix_out_bwd(dxo, proj, ylru, fo, vec, gp, wcat, wcat_t, bz, pavg, wout_t, tag):
    S = dxo.shape[0]
    ts = min(MIX_TS, S)

    def body(dxo_ref, u_ref, v_ref, yl_ref, fo_ref, vec_ref, gp_ref, wcat_ref, wcatt_ref, bz_ref, pavg_ref,
             wout_ref, dyo_ref, dyl_ref, duv_ref, acc_ref, dgp_ref, dwm_ref, dbz_ref):
        @pl.when(pl.program_id(0) == 0)
        def _():
            acc_ref[...] = jnp.zeros_like(acc_ref)
            dgp_ref[...] = jnp.zeros_like(dgp_ref)
            dwm_ref[...] = jnp.zeros_like(dwm_ref)
            dbz_ref[...] = jnp.zeros_like(dbz_ref)

        dxo_v = dxo_ref[...]
        acc_ref[2:3, :] += _csum(dxo_v * fo_ref[...].astype(F32))
        dyo = (vec_ref[2:3, :] * dxo_v).astype(BF16)
        dyo_ref[...] = dyo
        dn = _dot(dyo, wout_ref[...])
        dn1, dn2 = dn[:, :LW], dn[:, LW:]
        dyl, dg1 = _rms_bwd(dn1, yl_ref[...], gp_ref[1:2, :])
        dyl_ref[...] = dyl
        u = jnp.concatenate([u_ref[0], u_ref[1]], axis=1)
        v = jnp.concatenate([v_ref[0], v_ref[1]], axis=1)
        ug, rs, vhat, vh, z, ugrad, vgrad = _gmlp_fwd_parts(u, v, gp_ref, wcat_ref, bz_ref, pavg_ref, ts,
                                                            with_grad=True)
        dyg, dg2 = _rms_bwd(dn2, ug * z, gp_ref[2:3, :])
        du = (dyg * z) * ugrad
        dz = dyg * ug
        low = lax.broadcasted_iota(jnp.int32, (CHUNK, PAIR), 1) < HD
        vhb = vh.astype(BF16)
        dvhs = []
        dbz = jnp.zeros((CHUNK, GW), F32)
        dwm = [jnp.zeros((2 * CHUNK, CHUNK), F32) for _ in range(GW // PAIR)]
        for n in range(ts // CHUNK):
            dzc = dz[n * CHUNK:(n + 1) * CHUNK, :]
            dbz = dbz + dzc
            for p in range(GW // PAIR):
                stack = _pair_stack(dzc[:, p * PAIR:(p + 1) * PAIR], low)
                dwm[p] = dwm[p] + _dot_nt(stack, vhb[n * CHUNK:(n + 1) * CHUNK, p * PAIR:(p + 1) * PAIR])
            dvhs.append(_spatial(wcatt_ref, dzc, low))
        dbz_ref[...] += dbz
        for p in range(GW // PAIR):
            dwm_ref[2 * p * CHUNK:2 * (p + 1) * CHUNK, :] += dwm[p]
        dvh = jnp.concatenate(dvhs, axis=0) if len(dvhs) > 1 else dvhs[0]
        pavg = pavg_ref[...]
        dvn = _csum(dvh * vhat)
        dvhat = dvh * gp_ref[0:1, :]
        dvg = rs * (dvhat - _seg_mean(dvhat, pavg) - vhat * _seg_mean(dvhat * vhat, pavg))
        dv = dvg * vgrad
        duv_ref[0] = du[:, :PC].astype(BF16)
        duv_ref[1] = du[:, PC:].astype(BF16)
        duv_ref[2] = dv[:, :PC].astype(BF16)
        duv_ref[3] = dv[:, PC:].astype(BF16)
        dgp_ref[0:1, :] += dvn
        dgp_ref[1:2, :] += dg1
        dgp_ref[2:3, :] += dg2

    row = pl.BlockSpec((ts, D), lambda i: (i, 0))
    full = lambda shp: pl.BlockSpec(shp, lambda i: tuple(0 for _ in shp))
    return pl.pallas_call(
        body, name=f"mix_out_bwd_{tag}",
        grid=(S // ts,),
        in_specs=[row, pl.BlockSpec((2, ts, PC), lambda i: (2, i, 0)), pl.BlockSpec((2, ts, PC), lambda i: (3, i, 0)),
                  pl.BlockSpec((ts, LW), lambda i: (i, 0)), row, full((8, D)), full((8, GW)),
                  full((CHUNK, HEADS * CHUNK)), full((CHUNK, HEADS * CHUNK)), full((CHUNK, GW)), full((PAIR, PAIR)),
                  full((D, D))],
        out_specs=[row, pl.BlockSpec((ts, LW), lambda i: (i, 0)), pl.BlockSpec((4, ts, PC), lambda i: (0, i, 0)),
                   full((8, D)), full((8, GW)), full((HEADS * CHUNK, CHUNK)), full((CHUNK, GW))],
        out_shape=[jax.ShapeDtypeStruct((S, D), BF16), jax.ShapeDtypeStruct((S, LW), F32),
                   jax.ShapeDtypeStruct((4, S, PC), BF16), jax.ShapeDtypeStruct((8, D), F32),
                   jax.ShapeDtypeStruct((8, GW), F32), jax.ShapeDtypeStruct((HEADS * CHUNK, CHUNK), F32),
                   jax.ShapeDtypeStruct((CHUNK, GW), F32)],
        compiler_params=_cparams("arbitrary"),
    )(dxo, proj, proj, ylru, fo, vec, gp, wcat, wcat_t, bz, pavg, wout_t)


def final_loss(x, target, gain):
    S = x.shape[0]
    ts = min(512, S)

    def body(x_ref, t_ref, g_ref, loss_ref, dx_ref, dg_ref):
        @pl.when(pl.program_id(0) == 0)
        def _():
            loss_ref[...] = jnp.zeros_like(loss_ref)
            dg_ref[...] = jnp.zeros_like(dg_ref)

        xv = x_ref[...]
        gain_v = g_ref[0:1, :]
        rstd = lax.rsqrt(_rmean(xv * xv) + EPS)
        xhat = xv * rstd
        err = xhat * gain_v - t_ref[...]
        loss_ref[...] += 0.5 * _csum(_rmean(err * err))
        dy = err * (1.0 / D)
        dg_ref[0:1, :] += _csum(dy * xhat)
        dxhat = dy * gain_v
        dx_ref[...] = rstd * (dxhat - xhat * _rmean(dxhat * xhat))

    row = pl.BlockSpec((ts, D), lambda i: (i, 0))
    return pl.pallas_call(
        body, name="final_loss",
        grid=(S // ts,),
        in_specs=[row, row, pl.BlockSpec((8, D), lambda i: (0, 0))],
        out_specs=[pl.BlockSpec((8, 128), lambda i: (0, 0)), row, pl.BlockSpec((8, D), lambda i: (0, 0))],
        out_shape=[jax.ShapeDtypeStruct((8, 128), F32), jax.ShapeDtypeStruct((S, D), F32),
                   jax.ShapeDtypeStruct((8, D), F32)],
        compiler_params=_cparams("arbitrary"),
    )(x, target, gain)


def _vec(mod_l, j, gain):
    return jnp.concatenate([mod_l[3 * j:3 * j + 3], gain[None, :], jnp.zeros((4, D), F32)], axis=0)


def _block_diag_tiles(w):
    w4 = w.reshape(LW // LC, 2, HD, HD)
    eye2 = jnp.eye(2, dtype=w.dtype)
    return (w4[:, :, :, None, :] * eye2[None, :, None, :, None]).reshape(LW // LC, LC, LC).astype(BF16)


def _block_diag_extract(dw):
    d5 = dw.reshape(LW // LC, 2, HD, 2, HD)
    return jnp.einsum('cihkj,ik->cihj', d5, jnp.eye(2, dtype=dw.dtype)).reshape(HEADS, HD, HD)


def _layer_params(l, p, conv_w_full):
    lp = jnp.concatenate([conv_w_full[l], p['conv_b'][l][None], p['gate_a_b'][l].reshape(1, LW),
                          p['gate_x_b'][l].reshape(1, LW), p['lru_lambda'][l][None]], axis=0)
    gp = jnp.concatenate([p['v_norm'][l][None], p['lru_out_norm'][l][None], p['gmlp_out_norm'][l][None],
                          jnp.zeros((5, GW), F32)], axis=0)
    ws = p['spatial_w'][l] * jnp.tril(jnp.ones((CHUNK, CHUNK), F32))
    wcat = ws.transpose(1, 0, 2).reshape(CHUNK, HEADS * CHUNK).astype(BF16)
    wcat_t = ws.transpose(2, 0, 1).reshape(CHUNK, HEADS * CHUNK).astype(BF16)
    bz = jnp.repeat(p['spatial_b'][l].T, HD, axis=1)
    return dict(lp=lp, gp=gp, wcat=wcat, wcat_t=wcat_t, bz=bz,
                wa_t=_block_diag_tiles(p['gate_a_w'][l]), wx_t=_block_diag_tiles(p['gate_x_w'][l]))


def _pavg():
    return jnp.kron(jnp.eye(2, dtype=F32), jnp.full((HD, HD), 1.0 / HD, F32)).astype(BF16)


GATHER_RIDES = {
    ('ffn_a', 0): [('w_in', 0), ('gu', DEPTH)],
    ('mix_in', 0): [('w_out', 0)],
    ('lru', 0): [('down', DEPTH)],
    ('mix_out', 0): [('down', 1)],
    ('ffn_b', 0): [('gu', 1), ('w_in', 1)],
    ('ffn_a', 1): [('gu', DEPTH + 1), ('w_out', 1)],
    ('mix_in', 1): [('down', DEPTH + 1)],
}


def local_fwd_bwd(me_arr, x, target, mod, p, loc, gathered, conv_w_full):
    pavg = _pavg()
    g = dict(gathered)

    def ride(call, l):
        todo = GATHER_RIDES.get((call, l))
        return None if todo is None else (todo, GatherRide([(loc[kind], slot) for kind, slot in todo]))

    def run(fn, call, l, *args):
        r = ride(call, l)
        outs, got = fn(*args, ride=None if r is None else r[1])
        if r is not None:
            g.update(dict(zip(r[0], got)))
        return outs

    saved = []
    h = x
    for l in range(DEPTH):
        q = _layer_params(l, p, conv_w_full)
        v1 = _vec(mod[l], 0, p['ffn1_norm'][l])
        vm = _vec(mod[l], 1, p['mix_norm'][l])
        v2 = _vec(mod[l], 2, p['ffn2_norm'][l])
        x0 = h
        x1, h1, gu1, f1 = run(ffn_fwd, 'ffn_a', l, x0, v1, g['gu', l], g['down', l], f"a{l}")
        hm, proj = run(mix_in_fwd, 'mix_in', l, x1, vm, g['w_in', l], f"{l}")
        ylru, xc, hst = run(lru_fwd, 'lru', l, proj, q['lp'], q['wa_t'], q['wx_t'], f"{l}")
        x2, y, fo = run(mix_out_fwd, 'mix_out', l, proj, ylru, x1, vm, q['gp'], q['wcat'], q['bz'], pavg,
                        g['w_out', l].reshape(D, D), f"{l}")
        x3, h2, gu2, f2 = run(ffn_fwd, 'ffn_b', l, x2, v2, g['gu', DEPTH + l], g['down', DEPTH + l], f"b{l}")
        saved.append(dict(q=q, v1=v1, vm=vm, v2=v2, x0=x0, x1=x1, x2=x2, h1=h1, gu1=gu1, f1=f1, hm=hm, proj=proj,
                          ylru=ylru, xc=xc, hst=hst, y=y, fo=fo, h2=h2, gu2=gu2, f2=f2))
        h = x3
    fin = jnp.concatenate([p['final_norm'][None], jnp.zeros((7, D), F32)], axis=0)
    loss8, dx, dfin = final_loss(h, target, fin)
    loss = loss8[0, 0]

    big = dict(gu=None, down=None, w_in=None, w_out=None)
    small = {k: [None] * DEPTH for k in ('ffn1_norm', 'mix_norm', 'ffn2_norm', 'conv_w', 'conv_b', 'gate_a_w',
                                         'gate_a_b', 'gate_x_w', 'gate_x_b', 'lru_lambda', 'v_norm', 'spatial_w',
                                         'spatial_b', 'lru_out_norm', 'gmlp_out_norm')}
    dmod = [None] * DEPTH
    tril = jnp.tril(jnp.ones((CHUNK, CHUNK), F32))
    for l in reversed(range(DEPTH)):
        sv = saved[l]
        q = sv['q']
        dx2, dgu, a, df, acc2 = ffn_bwd(dx, sv['x2'], sv['gu2'], sv['f2'], sv['v2'],
                                        g['gu', DEPTH + l], g['down', DEPTH + l], f"b{l}")
        big['gu'] = tn_matmul_scatter(me_arr, dgu, sv['h2'][None], DEPTH + l, 2 * DEPTH, big['gu'], f"dw_gu_b{l}")
        big['down'] = tn_matmul_scatter(me_arr, a, df[None], DEPTH + l, 2 * DEPTH, big['down'], f"dw_down_b{l}", split=2)
        dyo, dylru, duv, accmo, dgp, dwm, dbz = mix_out_bwd(dx2, sv['proj'], sv['ylru'], sv['fo'], sv['vm'], q['gp'],
                                                             q['wcat'], q['wcat_t'], q['bz'], pavg,
                                                             g['w_out', l].reshape(D, D).T, f"{l}")
        big['w_out'] = tn_matmul_scatter(me_arr, sv['y'][None], dyo[None], l, DEPTH, big['w_out'], f"dw_out_{l}",
                                         split=NDEV)
        dxl, dgl, dlp, dwa, dwx = lru_bwd(dylru, sv['proj'], sv['xc'], sv['hst'], q['lp'], q['wa_t'], q['wx_t'], f"{l}")
        dproj = jnp.concatenate([dxl, dgl, duv], axis=0)
        dx1, accmi = mix_in_bwd(dproj, sv['x1'], dx2, sv['vm'], g['w_in', l], f"{l}")
        big['w_in'] = tn_matmul_scatter(me_arr, sv['hm'][None], dproj, l, DEPTH, big['w_in'], f"dw_in_{l}")
        dx0, dgu, a, df, acc1 = ffn_bwd(dx1, sv['x0'], sv['gu1'], sv['f1'], sv['v1'],
                                        g['gu', l], g['down', l], f"a{l}")
        big['gu'] = tn_matmul_scatter(me_arr, dgu, sv['h1'][None], l, 2 * DEPTH, big['gu'], f"dw_gu_a{l}")
        big['down'] = tn_matmul_scatter(me_arr, a, df[None], l, 2 * DEPTH, big['down'], f"dw_down_a{l}", split=2)
        dx = dx0
        dmod[l] = jnp.concatenate([acc1[0:3], accmi[0:2], accmo[2:3], acc2[0:3]], axis=0)
        small['ffn1_norm'][l] = acc1[3]
        small['mix_norm'][l] = accmi[3]
        small['ffn2_norm'][l] = acc2[3]
        small['conv_w'][l] = dlp[0:4]
        small['conv_b'][l] = dlp[4]
        small['gate_a_b'][l] = dlp[5].reshape(HEADS, HD)
        small['gate_x_b'][l] = dlp[6].reshape(HEADS, HD)
        small['lru_lambda'][l] = dlp[7]
        small['gate_a_w'][l] = _block_diag_extract(dwa)
        small['gate_x_w'][l] = _block_diag_extract(dwx)
        small['v_norm'][l] = dgp[0]
        small['lru_out_norm'][l] = dgp[1]
        small['gmlp_out_norm'][l] = dgp[2]
        small['spatial_w'][l] = dwm.reshape(HEADS, CHUNK, CHUNK) * tril
        small['spatial_b'][l] = dbz.reshape(CHUNK, HEADS, HD).sum(-1).T
    small = {k: jnp.stack(v) for k, v in small.items()}
    small['final_norm'] = dfin[0]
    return loss, dx, big, small, jnp.stack(dmod)


def ada_fwd(c_all, w_ada, b_loc):
    def body(c_ref, w_ref, b_ref, mod_ref, sc_ref):
        cv = c_ref[...]
        sc = cv * _sigmoid(cv)
        sc_ref[...] = sc
        mod_ref[...] = _dot3(sc, w_ref[...]) + b_ref[...]

    return pl.pallas_call(
        body, name="ada_fwd",
        grid=(DEPTH,),
        in_specs=[pl.BlockSpec((NDEV, D), lambda l: (0, 0)), pl.BlockSpec((None, D, AC), lambda l: (l, 0, 0)),
                  pl.BlockSpec((None, 1, AC), lambda l: (l, 0, 0))],
        out_specs=[pl.BlockSpec((None, NDEV, AC), lambda l: (l, 0, 0)), pl.BlockSpec((NDEV, D), lambda l: (0, 0))],
        out_shape=[jax.ShapeDtypeStruct((DEPTH, NDEV, AC), F32), jax.ShapeDtypeStruct((NDEV, D), F32)],
        compiler_params=_cparams("arbitrary"),
    )(c_all, w_ada, b_loc)


def ada_bwd(sc_t, dmod_cols):
    def body(sc_ref, dm_ref, g_ref):
        sc = sc_ref[...]
        dm = dm_ref[...]
        acc = sc[:, 0:1] * dm[0:1, :]
        for b in range(1, NDEV):
            acc = acc + sc[:, b:b + 1] * dm[b:b + 1, :]
        g_ref[...] = acc

    return pl.pallas_call(
        body, name="ada_bwd",
        grid=(DEPTH,),
        in_specs=[pl.BlockSpec((D, NDEV), lambda l: (0, 0)), pl.BlockSpec((None, NDEV, AC), lambda l: (l, 0, 0))],
        out_specs=pl.BlockSpec((None, None, D, AC), lambda l: (0, l, 0, 0)),
        out_shape=jax.ShapeDtypeStruct((1, DEPTH, D, AC), F32),
        compiler_params=_cparams("arbitrary"),
    )(sc_t, dmod_cols)


def _row_tile(rows, cols):
    if rows * cols <= 512 * 1024:
        return rows
    for tr in (512, 384, 352, 256, 128, 64, 32, 16, 8):
        if rows % tr == 0:
            return tr
    return rows


def adamw(gparts, slot0, w, m, v, name):
    P, _, R, C = gparts.shape
    L = w.shape[0]
    tr = _row_tile(R, C)

    def body(g_ref, w_ref, m_ref, v_ref, go_ref, do_ref, mo_ref, vo_ref):
        g = g_ref[0].astype(F32)
        for p in range(1, P):
            g = g + g_ref[p].astype(F32)
        go_ref[...] = g
        mn = ADAM_B1 * m_ref[...] + (1.0 - ADAM_B1) * g
        vn = ADAM_B2 * v_ref[...] + (1.0 - ADAM_B2) * (g * g)
        mo_ref[...] = mn
        vo_ref[...] = vn
        m_hat = mn / (1.0 - ADAM_B1 ** ADAM_STEP)
        v_hat = vn / (1.0 - ADAM_B2 ** ADAM_STEP)
        do_ref[...] = -ADAM_LR * (m_hat / (jnp.sqrt(v_hat) + ADAM_EPS) + ADAM_WD * w_ref[...])

    blk = pl.BlockSpec((None, tr, C), lambda l, i: (l, i, 0))
    return pl.pallas_call(
        body, name=name,
        grid=(L, R // tr),
        in_specs=[pl.BlockSpec((P, None, tr, C), lambda l, i: (0, slot0 + l, i, 0)), blk, blk, blk],
        out_specs=[blk, blk, blk, blk],
        out_shape=[jax.ShapeDtypeStruct((L, R, C), F32)] * 4,
        compiler_params=_cparams("arbitrary", "arbitrary"),
    )(gparts, w, m, v)


def sum_parts(parts):
    P, R, C = parts.shape

    def body(p_ref, o_ref):
        acc = p_ref[0]
        for p in range(1, P):
            acc = acc + p_ref[p]
        o_ref[...] = acc

    return pl.pallas_call(
        body, name="sum_parts",
        in_specs=[pl.BlockSpec(memory_space=pltpu.VMEM)],
        out_specs=pl.BlockSpec(memory_space=pltpu.VMEM),
        out_shape=jax.ShapeDtypeStruct((R, C), F32),
    )(parts)


WEIGHTS = ['w_ada', 'b_ada', 'ffn1_norm', 'ffn1_w_gu', 'ffn1_w_down', 'mix_norm', 'w_in', 'conv_w', 'conv_b',
           'gate_a_w', 'gate_a_b', 'gate_x_w', 'gate_x_b', 'lru_lambda', 'v_norm', 'spatial_w', 'spatial_b',
           'lru_out_norm', 'gmlp_out_norm', 'w_out', 'ffn2_norm', 'ffn2_w_gu', 'ffn2_w_down', 'final_norm']
PACKED = ['b_ada', 'ffn1_norm', 'mix_norm', 'conv_b', 'gate_a_w', 'gate_a_b', 'gate_x_w', 'gate_x_b', 'lru_lambda',
          'v_norm', 'spatial_w', 'spatial_b', 'lru_out_norm', 'gmlp_out_norm', 'ffn2_norm', 'final_norm', 'conv_w']
PACK_LANES = 128
PACK_ROW_ALIGN = 8 * NDEV


PACK_TAIL = 8


def _pack_rows(shapes):
    used = 0
    for k in PACKED:
        size = 1
        for s in shapes[k]:
            size *= s
        used += size // PACK_LANES
    return used, -(-(used + PACK_TAIL) // PACK_ROW_ALIGN) * PACK_ROW_ALIGN


def _pack(d, tail=None):
    parts = [d[k].reshape(-1, PACK_LANES).astype(F32) for k in PACKED]
    used, rows = _pack_rows({k: d[k].shape for k in PACKED})
    parts.append(jnp.zeros((PACK_TAIL, PACK_LANES), F32) if tail is None else tail)
    return jnp.concatenate(parts + [jnp.zeros((rows - used - PACK_TAIL, PACK_LANES), F32)], axis=0)


def _unpack(buf, shapes):
    out, off = {}, 0
    for k in PACKED:
        size = 1
        for s in shapes[k]:
            size *= s
        nrows = size // PACK_LANES
        out[k] = buf[off:off + nrows].reshape(shapes[k])
        off += nrows
    return out


def kernel(x, c, w_ada, b_ada, ffn1_norm, ffn1_w_gu, ffn1_w_down, mix_norm, w_in, conv_w, conv_b, gate_a_w, gate_a_b, gate_x_w, gate_x_b, lru_lambda, v_norm, spatial_w, spatial_b, lru_out_norm, gmlp_out_norm, w_out, ffn2_norm, ffn2_w_gu, ffn2_w_down, final_norm, loss_target, m_w_ada, m_b_ada, m_ffn1_norm, m_ffn1_w_gu, m_ffn1_w_down, m_mix_norm, m_w_in, m_conv_w, m_conv_b, m_gate_a_w, m_gate_a_b, m_gate_x_w, m_gate_x_b, m_lru_lambda, m_v_norm, m_spatial_w, m_spatial_b, m_lru_out_norm, m_gmlp_out_norm, m_w_out, m_ffn2_norm, m_ffn2_w_gu, m_ffn2_w_down, m_final_norm, v_w_ada, v_b_ada, v_ffn1_norm, v_ffn1_w_gu, v_ffn1_w_down, v_mix_norm, v_w_in, v_conv_w, v_conv_b, v_gate_a_w, v_gate_a_b, v_gate_x_w, v_gate_x_b, v_lru_lambda, v_v_norm, v_spatial_w, v_spatial_b, v_lru_out_norm, v_gmlp_out_norm, v_w_out, v_ffn2_norm, v_ffn2_w_gu, v_ffn2_w_down, v_final_norm):
    w = dict(w_ada=w_ada, b_ada=b_ada, ffn1_norm=ffn1_norm, ffn1_w_gu=ffn1_w_gu, ffn1_w_down=ffn1_w_down, mix_norm=mix_norm, w_in=w_in, conv_w=conv_w, conv_b=conv_b, gate_a_w=gate_a_w, gate_a_b=gate_a_b, gate_x_w=gate_x_w, gate_x_b=gate_x_b, lru_lambda=lru_lambda, v_norm=v_norm, spatial_w=spatial_w, spatial_b=spatial_b, lru_out_norm=lru_out_norm, gmlp_out_norm=gmlp_out_norm, w_out=w_out, ffn2_norm=ffn2_norm, ffn2_w_gu=ffn2_w_gu, ffn2_w_down=ffn2_w_down, final_norm=final_norm)
    m = dict(w_ada=m_w_ada, b_ada=m_b_ada, ffn1_norm=m_ffn1_norm, ffn1_w_gu=m_ffn1_w_gu, ffn1_w_down=m_ffn1_w_down, mix_norm=m_mix_norm, w_in=m_w_in, conv_w=m_conv_w, conv_b=m_conv_b, gate_a_w=m_gate_a_w, gate_a_b=m_gate_a_b, gate_x_w=m_gate_x_w, gate_x_b=m_gate_x_b, lru_lambda=m_lru_lambda, v_norm=m_v_norm, spatial_w=m_spatial_w, spatial_b=m_spatial_b, lru_out_norm=m_lru_out_norm, gmlp_out_norm=m_gmlp_out_norm, w_out=m_w_out, ffn2_norm=m_ffn2_norm, ffn2_w_gu=m_ffn2_w_gu, ffn2_w_down=m_ffn2_w_down, final_norm=m_final_norm)
    v = dict(w_ada=v_w_ada, b_ada=v_b_ada, ffn1_norm=v_ffn1_norm, ffn1_w_gu=v_ffn1_w_gu, ffn1_w_down=v_ffn1_w_down, mix_norm=v_mix_norm, w_in=v_w_in, conv_w=v_conv_w, conv_b=v_conv_b, gate_a_w=v_gate_a_w, gate_a_b=v_gate_a_b, gate_x_w=v_gate_x_w, gate_x_b=v_gate_x_b, lru_lambda=v_lru_lambda, v_norm=v_v_norm, spatial_w=v_spatial_w, spatial_b=v_spatial_b, lru_out_norm=v_lru_out_norm, gmlp_out_norm=v_gmlp_out_norm, w_out=v_w_out, ffn2_norm=v_ffn2_norm, ffn2_w_gu=v_ffn2_w_gu, ffn2_w_down=v_ffn2_w_down, final_norm=v_final_norm)
    me = 4 * lax.axis_index("x") + 2 * lax.axis_index("y") + lax.axis_index("c")

    loc = dict(gu=jnp.concatenate([ffn1_w_gu, ffn2_w_gu], axis=0).astype(BF16),
               down=jnp.concatenate([ffn1_w_down, ffn2_w_down], axis=0).astype(BF16),
               w_in=w_in.astype(BF16), w_out=w_out.astype(BF16))
    c_g, conv_g, gu0, down0 = all_gather([(c, None), (conv_w, None), (loc['gu'], 0), (loc['down'], 0)], "gather_first")
    conv_w_full = conv_g.transpose(1, 2, 0, 3).reshape(DEPTH, CONV_WIDTH, LW)

    b_loc = lax.dynamic_slice(b_ada, (0, me * AC), (DEPTH, AC)).reshape(DEPTH, 1, AC)
    mod_cols, sc_all = ada_fwd(c_g.reshape(NDEV, D), w_ada, b_loc)
    (mod_rows,) = all_to_all([mod_cols.transpose(1, 0, 2)], "scatter_mod")
    mod = mod_rows.transpose(1, 0, 2).reshape(DEPTH, NMOD, D)

    small_w = {k: w[k] for k in PACKED if k != 'conv_w'}
    me_arr = jnp.reshape(me, (1,)).astype(jnp.int32)
    loss_loc, dx, big, small_g, dmod = local_fwd_bwd(me_arr, x[0], loss_target[0], mod, small_w, loc,
                                                     {('gu', 0): gu0, ('down', 0): down0}, conv_w_full)

    small_g['b_ada'] = dmod.reshape(DEPTH, NMOD * D)
    first = (lax.broadcasted_iota(jnp.int32, (PACK_TAIL, PACK_LANES), 0)
             + lax.broadcasted_iota(jnp.int32, (PACK_TAIL, PACK_LANES), 1)) == 0
    gpack = _pack(small_g, jnp.where(first, loss_loc, 0.0))
    rows = gpack.shape[0]
    dmod_out = dmod.reshape(DEPTH, NDEV, AC).transpose(1, 0, 2)
    dmod_r, pack_r = all_to_all([dmod_out, gpack.reshape(NDEV, rows // NDEV, PACK_LANES)], "scatter_grads")
    (gsum_g,) = all_gather([(sum_parts(pack_r), None)], "gather_small_grads")
    gsum = gsum_g.reshape(1, 1, rows, PACK_LANES)
    loss = gsum[0, 0, _pack_rows({k: small_g[k].shape for k in PACKED})[0], 0]

    res = {}
    t = lambda a: a.transpose(0, 2, 1)
    gu_t = big['gu']
    res['ffn1_w_gu'] = tuple(t(r) for r in adamw(gu_t, 0, t(w['ffn1_w_gu']), t(m['ffn1_w_gu']), t(v['ffn1_w_gu']),
                                                 "adamw_gu_a"))
    res['ffn2_w_gu'] = tuple(t(r) for r in adamw(gu_t, DEPTH, t(w['ffn2_w_gu']), t(m['ffn2_w_gu']),
                                                 t(v['ffn2_w_gu']), "adamw_gu_b"))
    res['ffn1_w_down'] = adamw(big['down'], 0, w['ffn1_w_down'], m['ffn1_w_down'], v['ffn1_w_down'], "adamw_down_a")
    res['ffn2_w_down'] = adamw(big['down'], DEPTH, w['ffn2_w_down'], m['ffn2_w_down'], v['ffn2_w_down'], "adamw_down_b")
    res['w_in'] = adamw(big['w_in'], 0, w['w_in'], m['w_in'], v['w_in'], "adamw_w_in")
    res['w_out'] = adamw(big['w_out'], 0, w['w_out'], m['w_out'], v['w_out'], "adamw_w_out")
    g_ada = ada_bwd(sc_all.T, dmod_r.transpose(1, 0, 2))
    res['w_ada'] = adamw(g_ada, 0, w['w_ada'], m['w_ada'], v['w_ada'], "adamw_w_ada")
    shapes = {k: w[k].shape for k in PACKED}
    shapes['conv_w'] = (DEPTH, CONV_WIDTH, LW)
    dummy = jnp.zeros(shapes['conv_w'], F32)
    packs = adamw(gsum, 0, _pack({**small_w, 'conv_w': dummy})[None], _pack({**{k: m[k] for k in small_w}, 'conv_w': dummy})[None],
                  _pack({**{k: v[k] for k in small_w}, 'conv_w': dummy})[None], "adamw_small")
    unpacked = [_unpack(b[0], shapes) for b in packs]
    for k in small_w:
        res[k] = tuple(u[k] for u in unpacked)
    gconv = lax.dynamic_slice(unpacked[0]['conv_w'], (0, 0, me * (LW // NDEV)), (DEPTH, CONV_WIDTH, LW // NDEV))
    cshape = (1, DEPTH * CONV_WIDTH, LW // NDEV)
    rc = adamw(gconv.reshape((1,) + cshape), 0, conv_w.reshape(cshape), m['conv_w'].reshape(cshape),
               v['conv_w'].reshape(cshape), "adamw_conv_w")
    res['conv_w'] = tuple(r.reshape(conv_w.shape) for r in rc)

    return (loss, dx[None], *[res[k][0] for k in WEIGHTS], *[res[k][1] for k in WEIGHTS],
            *[res[k][2] for k in WEIGHTS], *[res[k][3] for k in WEIGHTS])
```

```python
import jax
import jax.numpy as jnp
from jax import lax
from jax.experimental import pallas as pl
from jax.experimental.pallas import tpu as pltpu

F32 = jnp.float32
BF16 = jnp.bfloat16

NDEV = 8
DEPTH = 2
D = 1024
DFF = 2816
FC = 2 * DFF // NDEV
NCHUNK = DFF // FC
DR = DFF // NDEV
LW = 512
GW = 512
HD = 64
HEADS = 8
CHUNK = 128
PC = 2 * (LW + GW) // NDEV
OR = D // NDEV
NMOD = 9
AC = NMOD * D // NDEV
LC = 128
EPS = 1e-6
RG_LRU_C = 8.0
CONV_WIDTH = 4

ADAM_LR = 0.001
ADAM_B1 = 0.9
ADAM_B2 = 0.999
ADAM_EPS = 1e-08
ADAM_WD = 0.01
ADAM_STEP = 10

VMEM_LIMIT_BYTES = 60 * 1024 * 1024
MESH = pl.DeviceIdType.MESH
ANY = pl.BlockSpec(memory_space=pl.ANY)


def _cparams(*sem):
    return pltpu.CompilerParams(dimension_semantics=tuple(sem) if sem else None,
                                vmem_limit_bytes=VMEM_LIMIT_BYTES)


def _dot(a, b):
    return jnp.dot(a, b, preferred_element_type=F32)


def _dot_nt(a, b):
    return lax.dot_general(a, b, (((1,), (1,)), ((), ())), preferred_element_type=F32)


def _dot_tn(a, b):
    return lax.dot_general(a, b, (((0,), (0,)), ((), ())), preferred_element_type=F32)


def _split(a):
    hi = a.astype(BF16)
    lo = (a - hi.astype(F32)).astype(BF16)
    return hi, lo


def _dot3(a, b):
    ah, al = _split(a)
    bh, bl = _split(b)
    return _dot(ah, bh) + (_dot(ah, bl) + _dot(al, bh))


def _csum(a):
    return jnp.sum(a, axis=0, keepdims=True)


def _rmean(a):
    return jnp.mean(a, axis=-1, keepdims=True)


def _sigmoid(a):
    return 1.0 / (1.0 + jnp.exp(-a))


_GELU_K = 0.7978845608028654
_GELU_C = 0.044715


def _gelu(a):
    return 0.5 * a * (1.0 + jnp.tanh(_GELU_K * (a + _GELU_C * a * a * a)))


def _gelu_and_grad(a):
    a2 = a * a
    t = jnp.tanh(_GELU_K * (a + _GELU_C * a2 * a))
    half = 0.5 * (1.0 + t)
    return a * half, half + 0.5 * a * (1.0 - t * t) * (_GELU_K * (1.0 + 3.0 * _GELU_C * a2))


def _norm_mod(x, gain, scale, shift):
    rstd = lax.rsqrt(_rmean(x * x) + EPS)
    return (x * rstd * gain) * (1.0 + scale) + shift


def _norm_mod_bwd(dh, x, gain, scale):
    rstd = lax.rsqrt(_rmean(x * x) + EPS)
    xhat = x * rstd
    dshift = _csum(dh)
    dscale = _csum(dh * (xhat * gain))
    dhn = dh * (1.0 + scale)
    dgain = _csum(dhn * xhat)
    dxhat = dhn * gain
    dx = rstd * (dxhat - xhat * _rmean(dxhat * xhat))
    return dx, dshift, dscale, dgain


def _rms(x, gain):
    rstd = lax.rsqrt(_rmean(x * x) + EPS)
    return x * rstd * gain


def _rms_bwd(dy, x, gain):
    rstd = lax.rsqrt(_rmean(x * x) + EPS)
    xhat = x * rstd
    dgain = _csum(dy * xhat)
    dxhat = dy * gain
    return rstd * (dxhat - xhat * _rmean(dxhat * xhat)), dgain


PAIR = 2 * HD


def _seg_mean(a, pavg):
    hi, lo = _split(a)
    return jnp.concatenate([_dot(hi[:, p:p + PAIR], pavg) + _dot(lo[:, p:p + PAIR], pavg)
                            for p in range(0, a.shape[1], PAIR)], axis=1)


def _block_copies(src_hbm, dst_vmem, sems, rows):
    copies = []
    for k in range(NDEV):
        dst = dst_vmem.at[k] if rows is None else dst_vmem.at[pl.ds(k * rows, rows)]
        copies.append(pltpu.make_async_copy(src_hbm.at[k], dst, sems.at[k]))
    return copies


def _ffn_weight_fetch(wgu_hbm, wd_hbm, wgu_v, wd_v, sems):
    @pl.when(pl.program_id(0) == 0)
    def _():
        copies = _block_copies(wgu_hbm, wgu_v, sems.at[0], None) + _block_copies(wd_hbm, wd_v, sems.at[1], DR)
        for cp in copies:
            cp.start()
        for cp in copies:
            cp.wait()


def _place():
    return lax.axis_index("x"), lax.axis_index("y"), lax.axis_index("c")


def _slot(p):
    return 4 * p[0] + 2 * p[1] + p[2]


class GatherRide:
    def __init__(self, srcs):
        self.n = len(srcs)
        self.index = [i for _, i in srcs]
        self.args = [a for a, _ in srcs]
        self.out_shape = [jax.ShapeDtypeStruct((NDEV,) + (a.shape if i is None else a.shape[1:]), a.dtype)
                          for a, i in srcs]
        self.scratch = [pltpu.SemaphoreType.DMA((self.n, NDEV - 1)), pltpu.SemaphoreType.DMA((self.n, NDEV - 1)),
                        pltpu.SemaphoreType.DMA((self.n,))]

    def hooks(self, ins, outs, sems):
        send_sems, recv_sems, local_sems = sems
        n = self.n
        x, y, c = _place()
        me, sibling = (x, y, c), (x, y, 1 - c)
        chips = [(1 - x, y), (x, 1 - y), (1 - x, 1 - y)]

        def local(a):
            return ins[a] if self.index[a] is None else ins[a].at[self.index[a]]

        def copy(a, k, block, to, src=None):
            dst = outs[a].at[_slot(block)]
            return pltpu.make_async_remote_copy(
                src_ref=dst if src is None else src, dst_ref=dst,
                send_sem=send_sems.at[a, k], recv_sem=recv_sems.at[a, k],
                device_id=to, device_id_type=MESH)

        def mine():
            return [pltpu.make_async_copy(local(a), outs[a].at[_slot(me)], local_sems.at[a]) for a in range(n)]

        def first():
            cps = []
            for a in range(n):
                cps.append(copy(a, 0, me, sibling, src=local(a)))
                cps += [copy(a, 1 + j, me, (*chip, c), src=local(a)) for j, chip in enumerate(chips)]
            return cps

        def passed():
            return [copy(a, 4 + j, (*chip, c), sibling) for j, chip in enumerate(chips) for a in range(n)]

        def start():
            for cp in mine() + first():
                cp.start()

        def mid():
            for j, chip in enumerate(chips):
                for a in range(n):
                    copy(a, 1 + j, (*chip, c), me).wait_recv()
                    copy(a, 4 + j, (*chip, c), sibling).start()

        def finish():
            for a in range(n):
                copy(a, 0, sibling, me).wait_recv()
                for j, chip in enumerate(chips):
                    copy(a, 4 + j, (*chip, 1 - c), me).wait_recv()
            for cp in first() + passed():
                cp.wait_send()
            for cp in mine():
                cp.wait()

        return start, mid, finish


def all_gather(srcs, name):
    ride = GatherRide(srcs)
    n = ride.n

    def body(*refs):
        start, mid, finish = ride.hooks(refs[:n], refs[n:2 * n], refs[2 * n:])
        start()
        mid()
        finish()

    return pl.pallas_call(
        body, name=name,
        in_specs=[ANY] * n, out_specs=[ANY] * n, out_shape=ride.out_shape, scratch_shapes=ride.scratch,
    )(*ride.args)


def _call(core, ride, *, name, grid, in_specs, out_specs, out_shape, scratch_shapes, args):
    if ride is None:
        outs = pl.pallas_call(core, name=name, grid=grid, in_specs=in_specs, out_specs=out_specs,
                              out_shape=out_shape, scratch_shapes=scratch_shapes,
                              compiler_params=_cparams("arbitrary"))(*args)
        return outs, []
    n_in, n_out, n_sc, n = len(in_specs), len(out_shape), len(scratch_shapes), ride.n
    nsteps = grid[0]
    mid_step = max(nsteps - 2, 0)

    def body(*refs):
        cuts = [n_in, n_in + n, n_in + n + n_out, n_in + 2 * n + n_out, n_in + 2 * n + n_out + n_sc]
        ci, ri, co, ro, cs, rs = (refs[a:b] for a, b in zip([0] + cuts, cuts + [len(refs)]))
        start, mid, finish = ride.hooks(ri, ro, rs)
        i = pl.program_id(0)
        pl.when(i == 0)(start)
        core(*ci, *co, *cs)
        pl.when(i == mid_step)(mid)
        pl.when(i == nsteps - 1)(finish)

    outs = pl.pallas_call(
        body, name=name, grid=grid,
        in_specs=list(in_specs) + [ANY] * n, out_specs=list(out_specs) + [ANY] * n,
        out_shape=list(out_shape) + ride.out_shape, scratch_shapes=list(scratch_shapes) + ride.scratch,
        compiler_params=_cparams("arbitrary"))(*args, *ride.args)
    return outs[:n_out], outs[n_out:]


def all_to_all(arrs, name):
    n = len(arrs)

    def body(*refs):
        ins, outs = refs[:n], refs[n:2 * n]
        send_sems, recv_sems, local_sems = refs[2 * n:]
        x, y, c = _place()
        me = (x, y, c)

        def peer(k):
            return (1 - x if k & 4 else x, 1 - y if k & 2 else y, 1 - c if k & 1 else c)

        def copy(a, k):
            return pltpu.make_async_remote_copy(
                src_ref=ins[a].at[_slot(peer(k))], dst_ref=outs[a].at[_slot(me)],
                send_sem=send_sems.at[a, k - 1], recv_sem=recv_sems.at[a, k - 1],
                device_id=peer(k), device_id_type=MESH)

        def landing(a, k):
            return pltpu.make_async_remote_copy(
                src_ref=outs[a].at[_slot(peer(k))], dst_ref=outs[a].at[_slot(peer(k))],
                send_sem=send_sems.at[a, k - 1], recv_sem=recv_sems.at[a, k - 1],
                device_id=me, device_id_type=MESH)

        mine = [pltpu.make_async_copy(ins[a].at[_slot(me)], outs[a].at[_slot(me)], local_sems.at[a]) for a in range(n)]
        for cp in mine:
            cp.start()
        sends = [copy(a, k) for a in range(n) for k in range(1, NDEV)]
        for cp in sends:
            cp.start()
        for a in range(n):
            for k in range(1, NDEV):
                landing(a, k).wait_recv()
        for cp in sends:
            cp.wait_send()
        for cp in mine:
            cp.wait()

    return pl.pallas_call(
        body, name=name,
        in_specs=[ANY] * n, out_specs=[ANY] * n,
        out_shape=[jax.ShapeDtypeStruct(a.shape, a.dtype) for a in arrs],
        scratch_shapes=[pltpu.SemaphoreType.DMA((n, NDEV - 1)), pltpu.SemaphoreType.DMA((n, NDEV - 1)),
                        pltpu.SemaphoreType.DMA((n,))],
    )(*arrs)


FFN_TS = 256
FFN_FWD_TS = 512


def ffn_fwd(x, vec, wgu_g, wdown_g, tag, ride=None):
    S = x.shape[0]
    ts = min(FFN_FWD_TS, S)

    def body(x_ref, vec_ref, wgu_hbm, wd_hbm, xo_ref, h_ref, gu_ref, f_ref, wgu_v, wd_v, sems):
        _ffn_weight_fetch(wgu_hbm, wd_hbm, wgu_v, wd_v, sems)
        xv = x_ref[...]
        h = _norm_mod(xv, vec_ref[3:4, :], vec_ref[1:2, :], vec_ref[0:1, :]).astype(BF16)
        h_ref[...] = h
        acc = jnp.zeros((ts, D), F32)
        for j in range(NCHUNK):
            g = _dot(h, wgu_v[j])
            u = _dot(h, wgu_v[NCHUNK + j])
            gu_ref[j] = g.astype(BF16)
            gu_ref[NCHUNK + j] = u.astype(BF16)
            a = (g * _sigmoid(g) * u).astype(BF16)
            acc = acc + _dot(a, wd_v[pl.ds(j * FC, FC), :])
        f_ref[...] = acc.astype(BF16)
        xo_ref[...] = xv + (0.5 * vec_ref[2:3, :]) * acc

    return _call(
        body, ride, name=f"ffn_fwd_{tag}",
        grid=(S // ts,),
        in_specs=[pl.BlockSpec((ts, D), lambda i: (i, 0)),
                  pl.BlockSpec((8, D), lambda i: (0, 0)), ANY, ANY],
        out_specs=[pl.BlockSpec((ts, D), lambda i: (i, 0)),
                   pl.BlockSpec((ts, D), lambda i: (i, 0)),
                   pl.BlockSpec((NDEV, ts, FC), lambda i: (0, i, 0)),
                   pl.BlockSpec((ts, D), lambda i: (i, 0))],
        out_shape=[jax.ShapeDtypeStruct((S, D), F32), jax.ShapeDtypeStruct((S, D), BF16),
                   jax.ShapeDtypeStruct((NDEV, S, FC), BF16), jax.ShapeDtypeStruct((S, D), BF16)],
        scratch_shapes=[pltpu.VMEM((NDEV, D, FC), BF16), pltpu.VMEM((DFF, D), BF16),
                        pltpu.SemaphoreType.DMA((2, NDEV))],
        args=(x, vec, wgu_g, wdown_g))


def ffn_bwd(dxo, x, gu, f, vec, wgu_g, wdown_g, tag):
    S = x.shape[0]
    ts = min(FFN_TS, S)

    def body(dxo_ref, x_ref, gu_ref, f_ref, vec_ref, wgu_hbm, wd_hbm,
             dx_ref, dgu_ref, a_ref, df_ref, acc_ref, wgu_v, wd_v, sems):
        _ffn_weight_fetch(wgu_hbm, wd_hbm, wgu_v, wd_v, sems)

        @pl.when(pl.program_id(0) == 0)
        def _():
            acc_ref[...] = jnp.zeros_like(acc_ref)

        dxo_v = dxo_ref[...]
        dgate = 0.5 * _csum(dxo_v * f_ref[...].astype(F32))
        df = ((0.5 * vec_ref[2:3, :]) * dxo_v).astype(BF16)
        df_ref[...] = df
        dh = jnp.zeros((ts, D), F32)
        for j in range(NCHUNK):
            da = _dot_nt(df, wd_v[pl.ds(j * FC, FC), :])
            g = gu_ref[j].astype(F32)
            u = gu_ref[NCHUNK + j].astype(F32)
            sg = _sigmoid(g)
            si = g * sg
            a_ref[j] = (si * u).astype(BF16)
            dg = (da * u * (sg * (1.0 + g * (1.0 - sg)))).astype(BF16)
            du = (da * si).astype(BF16)
            dgu_ref[j] = dg
            dgu_ref[NCHUNK + j] = du
            dh = dh + _dot_nt(dg, wgu_v[j]) + _dot_nt(du, wgu_v[NCHUNK + j])
        dx, dshift, dscale, dgain = _norm_mod_bwd(dh, x_ref[...], vec_ref[3:4, :], vec_ref[1:2, :])
        dx_ref[...] = dx + dxo_v
        acc_ref[0:1, :] += dshift
        acc_ref[1:2, :] += dscale
        acc_ref[2:3, :] += dgate
        acc_ref[3:4, :] += dgain

    row = pl.BlockSpec((ts, D), lambda i: (i, 0))
    return pl.pallas_call(
        body, name=f"ffn_bwd_{tag}",
        grid=(S // ts,),
        in_specs=[row, row, pl.BlockSpec((NDEV, ts, FC), lambda i: (0, i, 0)), row,
                  pl.BlockSpec((8, D), lambda i: (0, 0)), ANY, ANY],
        out_specs=[row, pl.BlockSpec((NDEV, ts, FC), lambda i: (0, i, 0)),
                   pl.BlockSpec((NCHUNK, ts, FC), lambda i: (0, i, 0)), row,
                   pl.BlockSpec((8, D), lambda i: (0, 0))],
        out_shape=[jax.ShapeDtypeStruct((S, D), F32), jax.ShapeDtypeStruct((NDEV, S, FC), BF16),
                   jax.ShapeDtypeStruct((NCHUNK, S, FC), BF16), jax.ShapeDtypeStruct((S, D), BF16),
                   jax.ShapeDtypeStruct((8, D), F32)],
        scratch_shapes=[pltpu.VMEM((NDEV, D, FC), BF16), pltpu.VMEM((DFF, D), BF16),
                        pltpu.SemaphoreType.DMA((2, NDEV))],
        compiler_params=_cparams("arbitrary"),
    )(dxo, x, gu, f, vec, wgu_g, wdown_g)


NCHIP = NDEV // 2


def tn_matmul_scatter(me_arr, a, b, slot, nslots, prev, name, split=1):
    na, S, M = a.shape
    nb, _, N = b.shape
    ncall = NDEV // split
    ts = min(4096, S)
    nsteps = S // ts
    mp = M // split
    other_step = {1: lambda j: 2 * j, 2: lambda j: j, 8: lambda j: 0}[split]
    mine_step = {1: lambda j: 2 * j + 1, 2: lambda j: j, 8: lambda j: 0}[split]

    def group(k, me_ref):
        if split == 1:
            return jnp.bitwise_xor(me_ref[0], NDEV - 1 - k)
        if split == 2:
            return jnp.bitwise_xor(me_ref[0] // 2, NCHIP - 1 - k)
        return 0

    def body(me_ref, *refs):
        a_ref, b_ref = refs[0], refs[1]
        recv_ref, acc, sb_other, sb_mine, land, d2d_send, d2d_recv, ici_send, ici_recv = refs[-9:]
        k = pl.program_id(0)
        s = pl.program_id(1)
        x, y, c = _place()
        my_chip = 2 * x + y

        def chip_of(j):
            if split == 8:
                cx, cy = j // 2, j % 2
            else:
                flip = NCHIP - 1 - j
                cx, cy = (1 - x if flip & 2 else x), (1 - y if flip & 1 else y)
            return cx, cy, 2 * cx + cy

        def piece(j, core):
            if split == 1:
                return acc[...]
            start = core * mp if split == 2 else (2 * j + core) * mp
            return acc[pl.ds(pl.multiple_of(start, 8), mp), :]

        def to_sibling(j):
            return pltpu.make_async_remote_copy(
                src_ref=sb_other.at[j], dst_ref=land.at[j], send_sem=d2d_send.at[j], recv_sem=d2d_recv.at[j],
                device_id=(x, y, 1 - c), device_id_type=MESH)

        def to_owner(j):
            cx, cy, ci = chip_of(j)
            dst = recv_ref.at[my_chip, slot]
            return ci, pltpu.make_async_copy(sb_mine.at[j], dst, ici_send.at[j]), pltpu.make_async_remote_copy(
                src_ref=sb_mine.at[j], dst_ref=dst, send_sem=ici_send.at[j], recv_sem=ici_recv.at[my_chip],
                device_id=(cx, cy, c), device_id_type=MESH)

        if nsteps == 1:
            acc[...] = _dot_tn(a_ref[...], b_ref[...])
        else:
            @pl.when(s == 0)
            def _():
                acc[...] = jnp.zeros_like(acc)

            acc[...] += _dot_tn(a_ref[...], b_ref[...])

        for kk in range(ncall):
            @pl.when((s == nsteps - 1) & (k == kk))
            def _():
                for j in range(NCHIP):
                    if other_step(j) == kk:
                        sb_other[j] = piece(j, 1 - c).astype(BF16)
                        to_sibling(j).start()
                for j in range(NCHIP):
                    if mine_step(j) == kk:
                        to_sibling(j).wait_recv()
                        sb_mine[j] = (piece(j, c) + land[j].astype(F32)).astype(BF16)
                        ci, loc, rem = to_owner(j)
                        pl.when(ci == my_chip)(loc.start)
                        pl.when(ci != my_chip)(rem.start)

        @pl.when((s == nsteps - 1) & (k == ncall - 1))
        def _():
            for j in range(NCHIP):
                to_sibling(j).wait_send()
                ci, loc, rem = to_owner(j)
                pl.when(ci == my_chip)(loc.wait)
                pl.when(ci != my_chip)(rem.wait_send)
            for src in range(NCHIP):
                @pl.when(my_chip != src)
                def _():
                    pltpu.make_async_remote_copy(
                        src_ref=recv_ref.at[src, slot], dst_ref=recv_ref.at[src, slot],
                        send_sem=ici_send.at[src], recv_sem=ici_recv.at[src],
                        device_id=(src // 2, src % 2, c), device_id_type=MESH).wait_recv()

    in_specs = [pl.BlockSpec((None, ts, M), (lambda k, s, me: (group(k, me), s, 0)) if na > 1 else (lambda k, s, me: (0, s, 0))),
                pl.BlockSpec((None, ts, N), (lambda k, s, me: (group(k, me), s, 0)) if nb > 1 else (lambda k, s, me: (0, s, 0)))]
    args = [me_arr, a, b]
    aliases = {}
    if prev is not None:
        in_specs.append(ANY)
        args.append(prev)
        aliases = {3: 0}
    return pl.pallas_call(
        body, name=name,
        grid_spec=pltpu.PrefetchScalarGridSpec(
            num_scalar_prefetch=1, grid=(ncall, nsteps), in_specs=in_specs, out_specs=ANY,
            scratch_shapes=[pltpu.VMEM((M, N), F32), pltpu.VMEM((NCHIP, mp, N), BF16), pltpu.VMEM((NCHIP, mp, N), BF16),
                            pltpu.VMEM((NCHIP, mp, N), BF16), pltpu.SemaphoreType.DMA((NCHIP,)),
                            pltpu.SemaphoreType.DMA((NCHIP,)), pltpu.SemaphoreType.DMA((NCHIP,)),
                            pltpu.SemaphoreType.DMA((NCHIP,))]),
        out_shape=jax.ShapeDtypeStruct((NCHIP, nslots, mp, N), BF16),
        input_output_aliases=aliases,
        compiler_params=_cparams("arbitrary", "arbitrary"),
    )(*args)


MIX_TS = 512
MIX_IN_TS = 512


def mix_in_fwd(x, vec, win_g, tag, ride=None):
    S = x.shape[0]
    ts = min(MIX_IN_TS, S)

    def body(x_ref, vec_ref, win_ref, hm_ref, proj_ref):
        h = _norm_mod(x_ref[...], vec_ref[3:4, :], vec_ref[1:2, :], vec_ref[0:1, :]).astype(BF16)
        hm_ref[...] = h
        for k in range(NDEV):
            proj_ref[k] = _dot(h, win_ref[k])

    return _call(
        body, ride, name=f"mix_in_fwd_{tag}",
        grid=(S // ts,),
        in_specs=[pl.BlockSpec((ts, D), lambda i: (i, 0)), pl.BlockSpec((8, D), lambda i: (0, 0)),
                  pl.BlockSpec((NDEV, D, PC), lambda i: (0, 0, 0))],
        out_specs=[pl.BlockSpec((ts, D), lambda i: (i, 0)),
                   pl.BlockSpec((NDEV, ts, PC), lambda i: (0, i, 0))],
        out_shape=[jax.ShapeDtypeStruct((S, D), BF16), jax.ShapeDtypeStruct((NDEV, S, PC), F32)],
        scratch_shapes=[], args=(x, vec, win_g))


def mix_in_bwd(dproj, x, dxo, vec, win_full, tag):
    S = x.shape[0]
    ts = min(MIX_IN_TS, S)

    def body(dp_ref, x_ref, dxo_ref, vec_ref, win_ref, dx_ref, acc_ref):
        @pl.when(pl.program_id(0) == 0)
        def _():
            acc_ref[...] = jnp.zeros_like(acc_ref)

        dp = jnp.concatenate([dp_ref[k] for k in range(NDEV)], axis=1)
        dh = _dot_nt(dp, win_ref[...])
        dx, dshift, dscale, dgain = _norm_mod_bwd(dh, x_ref[...], vec_ref[3:4, :], vec_ref[1:2, :])
        dx_ref[...] = dx + dxo_ref[...]
        acc_ref[0:1, :] += dshift
        acc_ref[1:2, :] += dscale
        acc_ref[3:4, :] += dgain

    row = pl.BlockSpec((ts, D), lambda i: (i, 0))
    return pl.pallas_call(
        body, name=f"mix_in_bwd_{tag}",
        grid=(S // ts,),
        in_specs=[pl.BlockSpec((NDEV, ts, PC), lambda i: (0, i, 0)), row, row,
                  pl.BlockSpec((8, D), lambda i: (0, 0)),
                  pl.BlockSpec((D, NDEV * PC), lambda i: (0, 0))],
        out_specs=[row, pl.BlockSpec((8, D), lambda i: (0, 0))],
        out_shape=[jax.ShapeDtypeStruct((S, D), F32), jax.ShapeDtypeStruct((8, D), F32)],
        compiler_params=_cparams("arbitrary"),
    )(dproj, x, dxo, vec, win_full)


SCAN_UNROLL = 8


def _shift_down(z, k, row):
    return jnp.where(row >= k, pltpu.roll(z, k, 0), 0.0)


def _shift_up(z, k, row, n):
    return jnp.where(row < n - k, pltpu.roll(z, n - k, 0), 0.0)


def _lru_gates(xc, lp_ref, wa_ref, wx_ref):
    xcb = xc.astype(BF16)
    ra = _sigmoid(_dot(xcb, wa_ref[...]) + lp_ref[5:6, :])
    ix = _sigmoid(_dot(xcb, wx_ref[...]) + lp_ref[6:7, :])
    lam = lp_ref[7:8, :]
    ls = jnp.minimum(lam, 0.0) - jnp.log(1.0 + jnp.exp(-jnp.abs(lam)))
    log_a = (RG_LRU_C * ls) * ra
    a = jnp.exp(log_a)
    mult = jnp.sqrt(-jnp.tanh(log_a) * (a * a + 1.0))
    return ra, ix, ls, a, mult


def _conv(x, lp_ref, row):
    return (lp_ref[4:5, :] + lp_ref[3:4, :] * x + lp_ref[2:3, :] * _shift_down(x, 1, row)
            + lp_ref[1:2, :] * _shift_down(x, 2, row) + lp_ref[0:1, :] * _shift_down(x, 3, row))


def lru_fwd(proj, lp, wa_t, wx_t, tag, ride=None):
    S = proj.shape[1]
    nblk = S // 8

    def body(x_ref, g_ref, lp_ref, wa_ref, wx_ref, y_ref, xc_ref, h_ref, a_s, b_s):
        x = x_ref[...]
        row = lax.broadcasted_iota(jnp.int32, x.shape, 0)
        xc = _conv(x, lp_ref, row)
        xc_ref[...] = xc
        ra, ix, ls, a, mult = _lru_gates(xc, lp_ref, wa_ref, wx_ref)
        a_s[...] = a
        b_s[...] = mult * (ix * xc)
        rowb = lax.broadcasted_iota(jnp.int32, (8, LC), 0)

        def step(i, carry):
            for q in range(SCAN_UNROLL):
                r0 = pl.multiple_of((i * SCAN_UNROLL + q) * 8, 8)
                A = a_s[pl.ds(r0, 8), :]
                B = b_s[pl.ds(r0, 8), :]
                for d in (1, 2, 4):
                    m = rowb >= d
                    As = jnp.where(m, pltpu.roll(A, d, 0), 1.0)
                    Bs = jnp.where(m, pltpu.roll(B, d, 0), 0.0)
                    B = A * Bs + B
                    A = A * As
                H = B + A * carry
                h_ref[pl.ds(r0, 8), :] = H
                carry = H[7:8, :]
            return carry

        lax.fori_loop(0, nblk // SCAN_UNROLL, step, jnp.zeros((1, LC), F32))
        y_ref[...] = h_ref[...] * _gelu(g_ref[...])

    col = pl.BlockSpec((S, LC), lambda c: (0, c))
    return _call(
        body, ride, name=f"lru_fwd_{tag}",
        grid=(LW // LC,),
        in_specs=[pl.BlockSpec((None, S, LC), lambda c: (c // 2, 0, c % 2)),
                  pl.BlockSpec((None, S, LC), lambda c: (2 + c // 2, 0, c % 2)),
                  pl.BlockSpec((8, LC), lambda c: (0, c)),
                  pl.BlockSpec((None, LC, LC), lambda c: (c, 0, 0)),
                  pl.BlockSpec((None, LC, LC), lambda c: (c, 0, 0))],
        out_specs=[col, col, col],
        out_shape=[jax.ShapeDtypeStruct((S, LW), F32)] * 3,
        scratch_shapes=[pltpu.VMEM((S, LC), F32), pltpu.VMEM((S, LC), F32)],
        args=(proj, proj, lp, wa_t, wx_t))


def lru_bwd(dy, proj, xc_all, hst, lp, wa_t, wx_t, tag):
    S = proj.shape[1]
    nblk = S // 8

    def body(dy_ref, x_ref, g_ref, xc_ref, h_ref, lp_ref, wa_ref, wx_ref,
             dx_ref, dg_ref, dlp_ref, dwa_ref, dwx_ref, c_s, l_s):
        xc = xc_ref[...]
        row = lax.broadcasted_iota(jnp.int32, xc.shape, 0)
        ra, ix, ls, a, mult = _lru_gates(xc, lp_ref, wa_ref, wx_ref)
        g = g_ref[...]
        dyv = dy_ref[...]
        h = h_ref[...]
        gelu_g, gelu_grad_g = _gelu_and_grad(g)
        dg_ref[...] = (dyv * h * gelu_grad_g).astype(BF16)
        c_s[...] = _shift_up(a, 1, row, S)
        l_s[...] = dyv * gelu_g
        rowb = lax.broadcasted_iota(jnp.int32, (8, LC), 0)

        def step(i, carry):
            for q in range(SCAN_UNROLL):
                r0 = pl.multiple_of((nblk - 1 - (i * SCAN_UNROLL + q)) * 8, 8)
                C = c_s[pl.ds(r0, 8), :]
                L = l_s[pl.ds(r0, 8), :]
                for d in (1, 2, 4):
                    m = rowb < 8 - d
                    Cs = jnp.where(m, pltpu.roll(C, 8 - d, 0), 1.0)
                    Ls = jnp.where(m, pltpu.roll(L, 8 - d, 0), 0.0)
                    L = C * Ls + L
                    C = C * Cs
                L = L + C * carry
                l_s[pl.ds(r0, 8), :] = L
                carry = L[0:1, :]
            return carry

        lax.fori_loop(0, nblk // SCAN_UNROLL, step, jnp.zeros((1, LC), F32))
        db = l_s[...]
        da = db * _shift_down(h, 1, row)
        ixc = ix * xc
        dmult = db * ixc
        dix = db * (mult * xc)
        dxc = db * (mult * ix)
        dlog_a = da * a - dmult * (a * a) / mult
        dra = dlog_a * (RG_LRU_C * ls)
        dls = _csum(dlog_a * ra) * RG_LRU_C
        lam = lp_ref[7:8, :]
        dlam = dls * _sigmoid(-lam)
        dpa = dra * ra * (1.0 - ra)
        dpx = dix * ix * (1.0 - ix)
        dpab = dpa.astype(BF16)
        dpxb = dpx.astype(BF16)
        xcb = xc.astype(BF16)
        dwa_ref[...] = _dot_tn(xcb, dpab)
        dwx_ref[...] = _dot_tn(xcb, dpxb)
        dxc = dxc + _dot_nt(dpab, wa_ref[...]) + _dot_nt(dpxb, wx_ref[...])
        x = x_ref[...]
        dlp_ref[0:1, :] = _csum(dxc * _shift_down(x, 3, row))
        dlp_ref[1:2, :] = _csum(dxc * _shift_down(x, 2, row))
        dlp_ref[2:3, :] = _csum(dxc * _shift_down(x, 1, row))
        dlp_ref[3:4, :] = _csum(dxc * x)
        dlp_ref[4:5, :] = _csum(dxc)
        dlp_ref[5:6, :] = _csum(dpa)
        dlp_ref[6:7, :] = _csum(dpx)
        dlp_ref[7:8, :] = dlam
        dx = (lp_ref[3:4, :] * dxc + lp_ref[2:3, :] * _shift_up(dxc, 1, row, S)
              + lp_ref[1:2, :] * _shift_up(dxc, 2, row, S) + lp_ref[0:1, :] * _shift_up(dxc, 3, row, S))
        dx_ref[...] = dx.astype(BF16)

    col = pl.BlockSpec((S, LC), lambda c: (0, c))
    pcol = pl.BlockSpec((None, S, LC), lambda c: (c // 2, 0, c % 2))
    return pl.pallas_call(
        body, name=f"lru_bwd_{tag}",
        grid=(LW // LC,),
        in_specs=[col, pcol, pl.BlockSpec((None, S, LC), lambda c: (2 + c // 2, 0, c % 2)), col, col,
                  pl.BlockSpec((8, LC), lambda c: (0, c)),
                  pl.BlockSpec((None, LC, LC), lambda c: (c, 0, 0)),
                  pl.BlockSpec((None, LC, LC), lambda c: (c, 0, 0))],
        out_specs=[pcol, pcol, pl.BlockSpec((8, LC), lambda c: (0, c)),
                   pl.BlockSpec((None, LC, LC), lambda c: (c, 0, 0)),
                   pl.BlockSpec((None, LC, LC), lambda c: (c, 0, 0))],
        out_shape=[jax.ShapeDtypeStruct((2, S, PC), BF16), jax.ShapeDtypeStruct((2, S, PC), BF16),
                   jax.ShapeDtypeStruct((8, LW), F32),
                   jax.ShapeDtypeStruct((LW // LC, LC, LC), F32), jax.ShapeDtypeStruct((LW // LC, LC, LC), F32)],
        scratch_shapes=[pltpu.VMEM((S, LC), F32), pltpu.VMEM((S, LC), F32)],
        compiler_params=_cparams("arbitrary"),
    )(dy, proj, proj, xc_all, hst, lp, wa_t, wx_t)


def _pair_stack(zp, low):
    return jnp.concatenate([jnp.where(low, zp, 0.0), jnp.where(low, 0.0, zp)], axis=0).astype(BF16)


def _spatial(w_ref, zc, low):
    return jnp.concatenate(
        [_dot(w_ref[:, 2 * p * CHUNK:2 * (p + 1) * CHUNK], _pair_stack(zc[:, p * PAIR:(p + 1) * PAIR], low))
         for p in range(GW // PAIR)], axis=1)


def _gmlp_fwd_parts(u, v, gp_ref, wcat_ref, bz_ref, pavg_ref, ts, with_grad=False):
    if with_grad:
        ug, ugrad = _gelu_and_grad(u)
        vg, vgrad = _gelu_and_grad(v)
    else:
        ug, vg, ugrad, vgrad = _gelu(u), _gelu(v), None, None
    pavg = pavg_ref[...]
    vc = vg - _seg_mean(vg, pavg)
    rs = lax.rsqrt(_seg_mean(vc * vc, pavg) + EPS)
    vhat = vc * rs
    vh = vhat * gp_ref[0:1, :]
    low = lax.broadcasted_iota(jnp.int32, (CHUNK, PAIR), 1) < HD
    zs = [_spatial(wcat_ref, vh[n * CHUNK:(n + 1) * CHUNK, :], low) + bz_ref[...] for n in range(ts // CHUNK)]
    z = jnp.concatenate(zs, axis=0) if len(zs) > 1 else zs[0]
    return ug, rs, vhat, vh, z, ugrad, vgrad


def mix_out_fwd(proj, ylru, x, vec, gp, wcat, bz, pavg, wout_g, tag, ride=None):
    S = x.shape[0]
    ts = min(MIX_TS, S)

    def body(u_ref, v_ref, yl_ref, x_ref, vec_ref, gp_ref, wcat_ref, bz_ref, pavg_ref, wout_ref,
             xo_ref, y_ref, fo_ref):
        u = jnp.concatenate([u_ref[0], u_ref[1]], axis=1)
        v = jnp.concatenate([v_ref[0], v_ref[1]], axis=1)
        ug, _, _, _, z, _, _ = _gmlp_fwd_parts(u, v, gp_ref, wcat_ref, bz_ref, pavg_ref, ts)
        n1 = _rms(yl_ref[...], gp_ref[1:2, :])
        n2 = _rms(ug * z, gp_ref[2:3, :])
        y = jnp.concatenate([n1, n2], axis=1).astype(BF16)
        y_ref[...] = y
        fo = _dot(y, wout_ref[...])
        fo_ref[...] = fo.astype(BF16)
        xo_ref[...] = x_ref[...] + vec_ref[2:3, :] * fo

    row = pl.BlockSpec((ts, D), lambda i: (i, 0))
    full = lambda shp: pl.BlockSpec(shp, lambda i: tuple(0 for _ in shp))
    return _call(
        body, ride, name=f"mix_out_fwd_{tag}",
        grid=(S // ts,),
        in_specs=[pl.BlockSpec((2, ts, PC), lambda i: (2, i, 0)), pl.BlockSpec((2, ts, PC), lambda i: (3, i, 0)),
                  pl.BlockSpec((ts, LW), lambda i: (i, 0)), row, full((8, D)), full((8, GW)),
                  full((CHUNK, HEADS * CHUNK)), full((CHUNK, GW)), full((PAIR, PAIR)), full((D, D))],
        out_specs=[row, row, row],
        out_shape=[jax.ShapeDtypeStruct((S, D), F32), jax.ShapeDtypeStruct((S, D), BF16),
                   jax.ShapeDtypeStruct((S, D), BF16)],
        scratch_shapes=[], args=(proj, proj, ylru, x, vec, gp, wcat, bz, pavg, wout_g))


def mix_out_bwd(dxo, proj, ylru, fo, vec, gp, wcat, wcat_t, bz, pavg, wout_t, tag):
    S = dxo.shape[0]
    ts = min(MIX_TS, S)

    def body(dxo_ref, u_ref, v_ref, yl_ref, fo_ref, vec_ref, gp_ref, wcat_ref, wcatt_ref, bz_ref, pavg_ref,
             wout_ref, dyo_ref, dyl_ref, duv_ref, acc_ref, dgp_ref, dwm_ref, dbz_ref):
        @pl.when(pl.program_id(0) == 0)
        def _():
            acc_ref[...] = jnp.zeros_like(acc_ref)
            dgp_ref[...] = jnp.zeros_like(dgp_ref)
            dwm_ref[...] = jnp.zeros_like(dwm_ref)
            dbz_ref[...] = jnp.zeros_like(dbz_ref)

        dxo_v = dxo_ref[...]
        acc_ref[2:3, :] += _csum(dxo_v * fo_ref[...].astype(F32))
        dyo = (vec_ref[2:3, :] * dxo_v).astype(BF16)
        dyo_ref[...] = dyo
        dn = _dot(dyo, wout_ref[...])
        dn1, dn2 = dn[:, :LW], dn[:, LW:]
        dyl, dg1 = _rms_bwd(dn1, yl_ref[...], gp_ref[1:2, :])
        dyl_ref[...] = dyl
        u = jnp.concatenate([u_ref[0], u_ref[1]], axis=1)
        v = jnp.concatenate([v_ref[0], v_ref[1]], axis=1)
        ug, rs, vhat, vh, z, ugrad, vgrad = _gmlp_fwd_parts(u, v, gp_ref, wcat_ref, bz_ref, pavg_ref, ts,
                                                            with_grad=True)
        dyg, dg2 = _rms_bwd(dn2, ug * z, gp_ref[2:3, :])
        du = (dyg * z) * ugrad
        dz = dyg * ug
        low = lax.broadcasted_iota(jnp.int32, (CHUNK, PAIR), 1) < HD
        vhb = vh.astype(BF16)
        dvhs = []
        dbz = jnp.zeros((CHUNK, GW), F32)
        dwm = [jnp.zeros((2 * CHUNK, CHUNK), F32) for _ in range(GW // PAIR)]
        for n in range(ts // CHUNK):
            dzc = dz[n * CHUNK:(n + 1) * CHUNK, :]
            dbz = dbz + dzc
            for p in range(GW // PAIR):
                stack = _pair_stack(dzc[:, p * PAIR:(p + 1) * PAIR], low)
                dwm[p] = dwm[p] + _dot_nt(stack, vhb[n * CHUNK:(n + 1) * CHUNK, p * PAIR:(p + 1) * PAIR])
            dvhs.append(_spatial(wcatt_ref, dzc, low))
        dbz_ref[...] += dbz
        for p in range(GW // PAIR):
            dwm_ref[2 * p * CHUNK:2 * (p + 1) * CHUNK, :] += dwm[p]
        dvh = jnp.concatenate(dvhs, axis=0) if len(dvhs) > 1 else dvhs[0]
        pavg = pavg_ref[...]
        dvn = _csum(dvh * vhat)
        dvhat = dvh * gp_ref[0:1, :]
        dvg = rs * (dvhat - _seg_mean(dvhat, pavg) - vhat * _seg_mean(dvhat * vhat, pavg))
        dv = dvg * vgrad
        duv_ref[0] = du[:, :PC].astype(BF16)
        duv_ref[1] = du[:, PC:].astype(BF16)
        duv_ref[2] = dv[:, :PC].astype(BF16)
        duv_ref[3] = dv[:, PC:].astype(BF16)
        dgp_ref[0:1, :] += dvn
        dgp_ref[1:2, :] += dg1
        dgp_ref[2:3, :] += dg2

    row = pl.BlockSpec((ts, D), lambda i: (i, 0))
    full = lambda shp: pl.BlockSpec(shp, lambda i: tuple(0 for _ in shp))
    return pl.pallas_call(
        body, name=f"mix_out_bwd_{tag}",
        grid=(S // ts,),
        in_specs=[row, pl.BlockSpec((2, ts, PC), lambda i: (2, i, 0)), pl.BlockSpec((2, ts, PC), lambda i: (3, i, 0)),
                  pl.BlockSpec((ts, LW), lambda i: (i, 0)), row, full((8, D)), full((8, GW)),
                  full((CHUNK, HEADS * CHUNK)), full((CHUNK, HEADS * CHUNK)), full((CHUNK, GW)), full((PAIR, PAIR)),
                  full((D, D))],
        out_specs=[row, pl.BlockSpec((ts, LW), lambda i: (i, 0)), pl.BlockSpec((4, ts, PC), lambda i: (0, i, 0)),
                   full((8, D)), full((8, GW)), full((HEADS * CHUNK, CHUNK)), full((CHUNK, GW))],
        out_shape=[jax.ShapeDtypeStruct((S, D), BF16), jax.ShapeDtypeStruct((S, LW), F32),
                   jax.ShapeDtypeStruct((4, S, PC), BF16), jax.ShapeDtypeStruct((8, D), F32),
                   jax.ShapeDtypeStruct((8, GW), F32), jax.ShapeDtypeStruct((HEADS * CHUNK, CHUNK), F32),
                   jax.ShapeDtypeStruct((CHUNK, GW), F32)],
        compiler_params=_cparams("arbitrary"),
    )(dxo, proj, proj, ylru, fo, vec, gp, wcat, wcat_t, bz, pavg, wout_t)


def final_loss(x, target, gain):
    S = x.shape[0]
    ts = min(512, S)

    def body(x_ref, t_ref, g_ref, loss_ref, dx_ref, dg_ref):
        @pl.when(pl.program_id(0) == 0)
        def _():
            loss_ref[...] = jnp.zeros_like(loss_ref)
            dg_ref[...] = jnp.zeros_like(dg_ref)

        xv = x_ref[...]
        gain_v = g_ref[0:1, :]
        rstd = lax.rsqrt(_rmean(xv * xv) + EPS)
        xhat = xv * rstd
        err = xhat * gain_v - t_ref[...]
        loss_ref[...] += 0.5 * _csum(_rmean(err * err))
        dy = err * (1.0 / D)
        dg_ref[0:1, :] += _csum(dy * xhat)
        dxhat = dy * gain_v
        dx_ref[...] = rstd * (dxhat - xhat * _rmean(dxhat * xhat))

    row = pl.BlockSpec((ts, D), lambda i: (i, 0))
    return pl.pallas_call(
        body, name="final_loss",
        grid=(S // ts,),
        in_specs=[row, row, pl.BlockSpec((8, D), lambda i: (0, 0))],
        out_specs=[pl.BlockSpec((8, 128), lambda i: (0, 0)), row, pl.BlockSpec((8, D), lambda i: (0, 0))],
        out_shape=[jax.ShapeDtypeStruct((8, 128), F32), jax.ShapeDtypeStruct((S, D), F32),
                   jax.ShapeDtypeStruct((8, D), F32)],
        compiler_params=_cparams("arbitrary"),
    )(x, target, gain)


def _vec(mod_l, j, gain):
    return jnp.concatenate([mod_l[3 * j:3 * j + 3], gain[None, :], jnp.zeros((4, D), F32)], axis=0)


def _block_diag_tiles(w):
    w4 = w.reshape(LW // LC, 2, HD, HD)
    eye2 = jnp.eye(2, dtype=w.dtype)
    return (w4[:, :, :, None, :] * eye2[None, :, None, :, None]).reshape(LW // LC, LC, LC).astype(BF16)


def _block_diag_extract(dw):
    d5 = dw.reshape(LW // LC, 2, HD, 2, HD)
    return jnp.einsum('cihkj,ik->cihj', d5, jnp.eye(2, dtype=dw.dtype)).reshape(HEADS, HD, HD)


def _layer_params(l, p, conv_w_full):
    lp = jnp.concatenate([conv_w_full[l], p['conv_b'][l][None], p['gate_a_b'][l].reshape(1, LW),
                          p['gate_x_b'][l].reshape(1, LW), p['lru_lambda'][l][None]], axis=0)
    gp = jnp.concatenate([p['v_norm'][l][None], p['lru_out_norm'][l][None], p['gmlp_out_norm'][l][None],
                          jnp.zeros((5, GW), F32)], axis=0)
    ws = p['spatial_w'][l] * jnp.tril(jnp.ones((CHUNK, CHUNK), F32))
    wcat = ws.transpose(1, 0, 2).reshape(CHUNK, HEADS * CHUNK).astype(BF16)
    wcat_t = ws.transpose(2, 0, 1).reshape(CHUNK, HEADS * CHUNK).astype(BF16)
    bz = jnp.repeat(p['spatial_b'][l].T, HD, axis=1)
    return dict(lp=lp, gp=gp, wcat=wcat, wcat_t=wcat_t, bz=bz,
                wa_t=_block_diag_tiles(p['gate_a_w'][l]), wx_t=_block_diag_tiles(p['gate_x_w'][l]))


def _pavg():
    return jnp.kron(jnp.eye(2, dtype=F32), jnp.full((HD, HD), 1.0 / HD, F32)).astype(BF16)


GATHER_RIDES = {
    ('ffn_a', 0): [('w_in', 0), ('gu', DEPTH)],
    ('mix_in', 0): [('w_out', 0)],
    ('lru', 0): [('down', DEPTH)],
    ('mix_out', 0): [('down', 1)],
    ('ffn_b', 0): [('gu', 1), ('w_in', 1)],
    ('ffn_a', 1): [('gu', DEPTH + 1), ('w_out', 1)],
    ('mix_in', 1): [('down', DEPTH + 1)],
}


def local_fwd_bwd(me_arr, x, target, mod, p, loc, gathered, conv_w_full):
    pavg = _pavg()
    g = dict(gathered)

    def ride(call, l):
        todo = GATHER_RIDES.get((call, l))
        return None if todo is None else (todo, GatherRide([(loc[kind], slot) for kind, slot in todo]))

    def run(fn, call, l, *args):
        r = ride(call, l)
        outs, got = fn(*args, ride=None if r is None else r[1])
        if r is not None:
            g.update(dict(zip(r[0], got)))
        return outs

    saved = []
    h = x
    for l in range(DEPTH):
        q = _layer_params(l, p, conv_w_full)
        v1 = _vec(mod[l], 0, p['ffn1_norm'][l])
        vm = _vec(mod[l], 1, p['mix_norm'][l])
        v2 = _vec(mod[l], 2, p['ffn2_norm'][l])
        x0 = h
        x1, h1, gu1, f1 = run(ffn_fwd, 'ffn_a', l, x0, v1, g['gu', l], g['down', l], f"a{l}")
        hm, proj = run(mix_in_fwd, 'mix_in', l, x1, vm, g['w_in', l], f"{l}")
        ylru, xc, hst = run(lru_fwd, 'lru', l, proj, q['lp'], q['wa_t'], q['wx_t'], f"{l}")
        x2, y, fo = run(mix_out_fwd, 'mix_out', l, proj, ylru, x1, vm, q['gp'], q['wcat'], q['bz'], pavg,
                        g['w_out', l].reshape(D, D), f"{l}")
        x3, h2, gu2, f2 = run(ffn_fwd, 'ffn_b', l, x2, v2, g['gu', DEPTH + l], g['down', DEPTH + l], f"b{l}")
        saved.append(dict(q=q, v1=v1, vm=vm, v2=v2, x0=x0, x1=x1, x2=x2, h1=h1, gu1=gu1, f1=f1, hm=hm, proj=proj,
                          ylru=ylru, xc=xc, hst=hst, y=y, fo=fo, h2=h2, gu2=gu2, f2=f2))
        h = x3
    fin = jnp.concatenate([p['final_norm'][None], jnp.zeros((7, D), F32)], axis=0)
    loss8, dx, dfin = final_loss(h, target, fin)
    loss = loss8[0, 0]

    big = dict(gu=None, down=None, w_in=None, w_out=None)
    small = {k: [None] * DEPTH for k in ('ffn1_norm', 'mix_norm', 'ffn2_norm', 'conv_w', 'conv_b', 'gate_a_w',
                                         'gate_a_b', 'gate_x_w', 'gate_x_b', 'lru_lambda', 'v_norm', 'spatial_w',
                                         'spatial_b', 'lru_out_norm', 'gmlp_out_norm')}
    dmod = [None] * DEPTH
    tril = jnp.tril(jnp.ones((CHUNK, CHUNK), F32))
    for l in reversed(range(DEPTH)):
        sv = saved[l]
        q = sv['q']
        dx2, dgu, a, df, acc2 = ffn_bwd(dx, sv['x2'], sv['gu2'], sv['f2'], sv['v2'],
                                        g['gu', DEPTH + l], g['down', DEPTH + l], f"b{l}")
        big['gu'] = tn_matmul_scatter(me_arr, dgu, sv['h2'][None], DEPTH + l, 2 * DEPTH, big['gu'], f"dw_gu_b{l}")
        big['down'] = tn_matmul_scatter(me_arr, a, df[None], DEPTH + l, 2 * DEPTH, big['down'], f"dw_down_b{l}", split=2)
        dyo, dylru, duv, accmo, dgp, dwm, dbz = mix_out_bwd(dx2, sv['proj'], sv['ylru'], sv['fo'], sv['vm'], q['gp'],
                                                             q['wcat'], q['wcat_t'], q['bz'], pavg,
                                                             g['w_out', l].reshape(D, D).T, f"{l}")
        big['w_out'] = tn_matmul_scatter(me_arr, sv['y'][None], dyo[None], l, DEPTH, big['w_out'], f"dw_out_{l}",
                                         split=NDEV)
        dxl, dgl, dlp, dwa, dwx = lru_bwd(dylru, sv['proj'], sv['xc'], sv['hst'], q['lp'], q['wa_t'], q['wx_t'], f"{l}")
        dproj = jnp.concatenate([dxl, dgl, duv], axis=0)
        dx1, accmi = mix_in_bwd(dproj, sv['x1'], dx2, sv['vm'],
                                g['w_in', l].transpose(1, 0, 2).reshape(D, NDEV * PC), f"{l}")
        big['w_in'] = tn_matmul_scatter(me_arr, sv['hm'][None], dproj, l, DEPTH, big['w_in'], f"dw_in_{l}")
        dx0, dgu, a, df, acc1 = ffn_bwd(dx1, sv['x0'], sv['gu1'], sv['f1'], sv['v1'],
                                        g['gu', l], g['down', l], f"a{l}")
        big['gu'] = tn_matmul_scatter(me_arr, dgu, sv['h1'][None], l, 2 * DEPTH, big['gu'], f"dw_gu_a{l}")
        big['down'] = tn_matmul_scatter(me_arr, a, df[None], l, 2 * DEPTH, big['down'], f"dw_down_a{l}", split=2)
        dx = dx0
        dmod[l] = jnp.concatenate([acc1[0:3], accmi[0:2], accmo[2:3], acc2[0:3]], axis=0)
        small['ffn1_norm'][l] = acc1[3]
        small['mix_norm'][l] = accmi[3]
        small['ffn2_norm'][l] = acc2[3]
        small['conv_w'][l] = dlp[0:4]
        small['conv_b'][l] = dlp[4]
        small['gate_a_b'][l] = dlp[5].reshape(HEADS, HD)
        small['gate_x_b'][l] = dlp[6].reshape(HEADS, HD)
        small['lru_lambda'][l] = dlp[7]
        small['gate_a_w'][l] = _block_diag_extract(dwa)
        small['gate_x_w'][l] = _block_diag_extract(dwx)
        small['v_norm'][l] = dgp[0]
        small['lru_out_norm'][l] = dgp[1]
        small['gmlp_out_norm'][l] = dgp[2]
        small['spatial_w'][l] = dwm.reshape(HEADS, CHUNK, CHUNK) * tril
        small['spatial_b'][l] = dbz.reshape(CHUNK, HEADS, HD).sum(-1).T
    small = {k: jnp.stack(v) for k, v in small.items()}
    small['final_norm'] = dfin[0]
    return loss, dx, big, small, jnp.stack(dmod)


def ada_fwd(c_all, w_ada, b_loc):
    def body(c_ref, w_ref, b_ref, mod_ref, sc_ref):
        cv = c_ref[...]
        sc = cv * _sigmoid(cv)
        sc_ref[...] = sc
        mod_ref[...] = _dot3(sc, w_ref[...]) + b_ref[...]

    return pl.pallas_call(
        body, name="ada_fwd",
        grid=(DEPTH,),
        in_specs=[pl.BlockSpec((NDEV, D), lambda l: (0, 0)), pl.BlockSpec((None, D, AC), lambda l: (l, 0, 0)),
                  pl.BlockSpec((None, 1, AC), lambda l: (l, 0, 0))],
        out_specs=[pl.BlockSpec((None, NDEV, AC), lambda l: (l, 0, 0)), pl.BlockSpec((NDEV, D), lambda l: (0, 0))],
        out_shape=[jax.ShapeDtypeStruct((DEPTH, NDEV, AC), F32), jax.ShapeDtypeStruct((NDEV, D), F32)],
        compiler_params=_cparams("arbitrary"),
    )(c_all, w_ada, b_loc)


def ada_bwd(sc_t, dmod_cols):
    def body(sc_ref, dm_ref, g_ref):
        sc = sc_ref[...]
        dm = dm_ref[...]
        acc = sc[:, 0:1] * dm[0:1, :]
        for b in range(1, NDEV):
            acc = acc + sc[:, b:b + 1] * dm[b:b + 1, :]
        g_ref[...] = acc

    return pl.pallas_call(
        body, name="ada_bwd",
        grid=(DEPTH,),
        in_specs=[pl.BlockSpec((D, NDEV), lambda l: (0, 0)), pl.BlockSpec((None, NDEV, AC), lambda l: (l, 0, 0))],
        out_specs=pl.BlockSpec((None, None, D, AC), lambda l: (0, l, 0, 0)),
        out_shape=jax.ShapeDtypeStruct((1, DEPTH, D, AC), F32),
        compiler_params=_cparams("arbitrary"),
    )(sc_t, dmod_cols)


def _row_tile(rows, cols):
    if rows * cols <= 512 * 1024:
        return rows
    for tr in (512, 384, 352, 256, 128, 64, 32, 16, 8):
        if rows % tr == 0:
            return tr
    return rows


def adamw(gparts, slot0, w, m, v, name):
    P, _, R, C = gparts.shape
    L = w.shape[0]
    tr = _row_tile(R, C)

    def body(g_ref, w_ref, m_ref, v_ref, go_ref, do_ref, mo_ref, vo_ref):
        g = g_ref[0].astype(F32)
        for p in range(1, P):
            g = g + g_ref[p].astype(F32)
        go_ref[...] = g
        mn = ADAM_B1 * m_ref[...] + (1.0 - ADAM_B1) * g
        vn = ADAM_B2 * v_ref[...] + (1.0 - ADAM_B2) * (g * g)
        mo_ref[...] = mn
        vo_ref[...] = vn
        m_hat = mn / (1.0 - ADAM_B1 ** ADAM_STEP)
        v_hat = vn / (1.0 - ADAM_B2 ** ADAM_STEP)
        do_ref[...] = -ADAM_LR * (m_hat / (jnp.sqrt(v_hat) + ADAM_EPS) + ADAM_WD * w_ref[...])

    blk = pl.BlockSpec((None, tr, C), lambda l, i: (l, i, 0))
    return pl.pallas_call(
        body, name=name,
        grid=(L, R // tr),
        in_specs=[pl.BlockSpec((P, None, tr, C), lambda l, i: (0, slot0 + l, i, 0)), blk, blk, blk],
        out_specs=[blk, blk, blk, blk],
        out_shape=[jax.ShapeDtypeStruct((L, R, C), F32)] * 4,
        compiler_params=_cparams("arbitrary", "arbitrary"),
    )(gparts, w, m, v)


def sum_parts(parts):
    P, R, C = parts.shape

    def body(p_ref, o_ref):
        acc = p_ref[0]
        for p in range(1, P):
            acc = acc + p_ref[p]
        o_ref[...] = acc

    return pl.pallas_call(
        body, name="sum_parts",
        in_specs=[pl.BlockSpec(memory_space=pltpu.VMEM)],
        out_specs=pl.BlockSpec(memory_space=pltpu.VMEM),
        out_shape=jax.ShapeDtypeStruct((R, C), F32),
    )(parts)


WEIGHTS = ['w_ada', 'b_ada', 'ffn1_norm', 'ffn1_w_gu', 'ffn1_w_down', 'mix_norm', 'w_in', 'conv_w', 'conv_b',
           'gate_a_w', 'gate_a_b', 'gate_x_w', 'gate_x_b', 'lru_lambda', 'v_norm', 'spatial_w', 'spatial_b',
           'lru_out_norm', 'gmlp_out_norm', 'w_out', 'ffn2_norm', 'ffn2_w_gu', 'ffn2_w_down', 'final_norm']
PACKED = ['b_ada', 'ffn1_norm', 'mix_norm', 'conv_b', 'gate_a_w', 'gate_a_b', 'gate_x_w', 'gate_x_b', 'lru_lambda',
          'v_norm', 'spatial_w', 'spatial_b', 'lru_out_norm', 'gmlp_out_norm', 'ffn2_norm', 'final_norm', 'conv_w']
PACK_LANES = 128
PACK_ROW_ALIGN = 8 * NDEV


PACK_TAIL = 8


def _pack_rows(shapes):
    used = 0
    for k in PACKED:
        size = 1
        for s in shapes[k]:
            size *= s
        used += size // PACK_LANES
    return used, -(-(used + PACK_TAIL) // PACK_ROW_ALIGN) * PACK_ROW_ALIGN


def _pack(d, tail=None):
    parts = [d[k].reshape(-1, PACK_LANES).astype(F32) for k in PACKED]
    used, rows = _pack_rows({k: d[k].shape for k in PACKED})
    parts.append(jnp.zeros((PACK_TAIL, PACK_LANES), F32) if tail is None else tail)
    return jnp.concatenate(parts + [jnp.zeros((rows - used - PACK_TAIL, PACK_LANES), F32)], axis=0)


def _unpack(buf, shapes):
    out, off = {}, 0
    for k in PACKED:
        size = 1
        for s in shapes[k]:
            size *= s
        nrows = size // PACK_LANES
        out[k] = buf[off:off + nrows].reshape(shapes[k])
        off += nrows
    return out


def kernel(x, c, w_ada, b_ada, ffn1_norm, ffn1_w_gu, ffn1_w_down, mix_norm, w_in, conv_w, conv_b, gate_a_w, gate_a_b, gate_x_w, gate_x_b, lru_lambda, v_norm, spatial_w, spatial_b, lru_out_norm, gmlp_out_norm, w_out, ffn2_norm, ffn2_w_gu, ffn2_w_down, final_norm, loss_target, m_w_ada, m_b_ada, m_ffn1_norm, m_ffn1_w_gu, m_ffn1_w_down, m_mix_norm, m_w_in, m_conv_w, m_conv_b, m_gate_a_w, m_gate_a_b, m_gate_x_w, m_gate_x_b, m_lru_lambda, m_v_norm, m_spatial_w, m_spatial_b, m_lru_out_norm, m_gmlp_out_norm, m_w_out, m_ffn2_norm, m_ffn2_w_gu, m_ffn2_w_down, m_final_norm, v_w_ada, v_b_ada, v_ffn1_norm, v_ffn1_w_gu, v_ffn1_w_down, v_mix_norm, v_w_in, v_conv_w, v_conv_b, v_gate_a_w, v_gate_a_b, v_gate_x_w, v_gate_x_b, v_lru_lambda, v_v_norm, v_spatial_w, v_spatial_b, v_lru_out_norm, v_gmlp_out_norm, v_w_out, v_ffn2_norm, v_ffn2_w_gu, v_ffn2_w_down, v_final_norm):
    w = dict(w_ada=w_ada, b_ada=b_ada, ffn1_norm=ffn1_norm, ffn1_w_gu=ffn1_w_gu, ffn1_w_down=ffn1_w_down, mix_norm=mix_norm, w_in=w_in, conv_w=conv_w, conv_b=conv_b, gate_a_w=gate_a_w, gate_a_b=gate_a_b, gate_x_w=gate_x_w, gate_x_b=gate_x_b, lru_lambda=lru_lambda, v_norm=v_norm, spatial_w=spatial_w, spatial_b=spatial_b, lru_out_norm=lru_out_norm, gmlp_out_norm=gmlp_out_norm, w_out=w_out, ffn2_norm=ffn2_norm, ffn2_w_gu=ffn2_w_gu, ffn2_w_down=ffn2_w_down, final_norm=final_norm)
    m = dict(w_ada=m_w_ada, b_ada=m_b_ada, ffn1_norm=m_ffn1_norm, ffn1_w_gu=m_ffn1_w_gu, ffn1_w_down=m_ffn1_w_down, mix_norm=m_mix_norm, w_in=m_w_in, conv_w=m_conv_w, conv_b=m_conv_b, gate_a_w=m_gate_a_w, gate_a_b=m_gate_a_b, gate_x_w=m_gate_x_w, gate_x_b=m_gate_x_b, lru_lambda=m_lru_lambda, v_norm=m_v_norm, spatial_w=m_spatial_w, spatial_b=m_spatial_b, lru_out_norm=m_lru_out_norm, gmlp_out_norm=m_gmlp_out_norm, w_out=m_w_out, ffn2_norm=m_ffn2_norm, ffn2_w_gu=m_ffn2_w_gu, ffn2_w_down=m_ffn2_w_down, final_norm=m_final_norm)
    v = dict(w_ada=v_w_ada, b_ada=v_b_ada, ffn1_norm=v_ffn1_norm, ffn1_w_gu=v_ffn1_w_gu, ffn1_w_down=v_ffn1_w_down, mix_norm=v_mix_norm, w_in=v_w_in, conv_w=v_conv_w, conv_b=v_conv_b, gate_a_w=v_gate_a_w, gate_a_b=v_gate_a_b, gate_x_w=v_gate_x_w, gate_x_b=v_gate_x_b, lru_lambda=v_lru_lambda, v_norm=v_v_norm, spatial_w=v_spatial_w, spatial_b=v_spatial_b, lru_out_norm=v_lru_out_norm, gmlp_out_norm=v_gmlp_out_norm, w_out=v_w_out, ffn2_norm=v_ffn2_norm, ffn2_w_gu=v_ffn2_w_gu, ffn2_w_down=v_ffn2_w_down, final_norm=v_final_norm)
    me = 4 * lax.axis_index("x") + 2 * lax.axis_index("y") + lax.axis_index("c")

    loc = dict(gu=jnp.concatenate([ffn1_w_gu, ffn2_w_gu], axis=0).astype(BF16),
               down=jnp.concatenate([ffn1_w_down, ffn2_w_down], axis=0).astype(BF16),
               w_in=w_in.astype(BF16), w_out=w_out.astype(BF16))
    c_g, conv_g, gu0, down0 = all_gather([(c, None), (conv_w, None), (loc['gu'], 0), (loc['down'], 0)], "gather_first")
    conv_w_full = conv_g.transpose(1, 2, 0, 3).reshape(DEPTH, CONV_WIDTH, LW)

    b_loc = lax.dynamic_slice(b_ada, (0, me * AC), (DEPTH, AC)).reshape(DEPTH, 1, AC)
    mod_cols, sc_all = ada_fwd(c_g.reshape(NDEV, D), w_ada, b_loc)
    (mod_rows,) = all_to_all([mod_cols.transpose(1, 0, 2)], "scatter_mod")
    mod = mod_rows.transpose(1, 0, 2).reshape(DEPTH, NMOD, D)

    small_w = {k: w[k] for k in PACKED if k != 'conv_w'}
    me_arr = jnp.reshape(me, (1,)).astype(jnp.int32)
    loss_loc, dx, big, small_g, dmod = local_fwd_bwd(me_arr, x[0], loss_target[0], mod, small_w, loc,
                                                     {('gu', 0): gu0, ('down', 0): down0}, conv_w_full)

    small_g['b_ada'] = dmod.reshape(DEPTH, NMOD * D)
    first = (lax.broadcasted_iota(jnp.int32, (PACK_TAIL, PACK_LANES), 0)
             + lax.broadcasted_iota(jnp.int32, (PACK_TAIL, PACK_LANES), 1)) == 0
    gpack = _pack(small_g, jnp.where(first, loss_loc, 0.0))
    rows = gpack.shape[0]
    dmod_out = dmod.reshape(DEPTH, NDEV, AC).transpose(1, 0, 2)
    dmod_r, pack_r = all_to_all([dmod_out, gpack.reshape(NDEV, rows // NDEV, PACK_LANES)], "scatter_grads")
    (gsum_g,) = all_gather([(sum_parts(pack_r), None)], "gather_small_grads")
    gsum = gsum_g.reshape(1, 1, rows, PACK_LANES)
    loss = gsum[0, 0, _pack_rows({k: small_g[k].shape for k in PACKED})[0], 0]

    res = {}
    t = lambda a: a.transpose(0, 2, 1)
    gu_t = big['gu']
    res['ffn1_w_gu'] = tuple(t(r) for r in adamw(gu_t, 0, t(w['ffn1_w_gu']), t(m['ffn1_w_gu']), t(v['ffn1_w_gu']),
                                                 "adamw_gu_a"))
    res['ffn2_w_gu'] = tuple(t(r) for r in adamw(gu_t, DEPTH, t(w['ffn2_w_gu']), t(m['ffn2_w_gu']),
                                                 t(v['ffn2_w_gu']), "adamw_gu_b"))
    res['ffn1_w_down'] = adamw(big['down'], 0, w['ffn1_w_down'], m['ffn1_w_down'], v['ffn1_w_down'], "adamw_down_a")
    res['ffn2_w_down'] = adamw(big['down'], DEPTH, w['ffn2_w_down'], m['ffn2_w_down'], v['ffn2_w_down'], "adamw_down_b")
    res['w_in'] = adamw(big['w_in'], 0, w['w_in'], m['w_in'], v['w_in'], "adamw_w_in")
    res['w_out'] = adamw(big['w_out'], 0, w['w_out'], m['w_out'], v['w_out'], "adamw_w_out")
    g_ada = ada_bwd(sc_all.T, dmod_r.transpose(1, 0, 2))
    res['w_ada'] = adamw(g_ada, 0, w['w_ada'], m['w_ada'], v['w_ada'], "adamw_w_ada")
    shapes = {k: w[k].shape for k in PACKED}
    shapes['conv_w'] = (DEPTH, CONV_WIDTH, LW)
    dummy = jnp.zeros(shapes['conv_w'], F32)
    packs = adamw(gsum, 0, _pack({**small_w, 'conv_w': dummy})[None], _pack({**{k: m[k] for k in small_w}, 'conv_w': dummy})[None],
                  _pack({**{k: v[k] for k in small_w}, 'conv_w': dummy})[None], "adamw_small")
    unpacked = [_unpack(b[0], shapes) for b in packs]
    for k in small_w:
        res[k] = tuple(u[k] for u in unpacked)
    gconv = lax.dynamic_slice(unpacked[0]['conv_w'], (0, 0, me * (LW // NDEV)), (DEPTH, CONV_WIDTH, LW // NDEV))
    cshape = (1, DEPTH * CONV_WIDTH, LW // NDEV)
    rc = adamw(gconv.reshape((1,) + cshape), 0, conv_w.reshape(cshape), m['conv_w'].reshape(cshape),
               v['conv_w'].reshape(cshape), "adamw_conv_w")
    res['conv_w'] = tuple(r.reshape(conv_w.shape) for r in rc)

    return (loss, dx[None], *[res[k][0] for k in WEIGHTS], *[res[k][1] for k in WEIGHTS],
            *[res[k][2] for k in WEIGHTS], *[res[k][3] for k in WEIGHTS])
```

```python
import jax
import jax.numpy as jnp
from jax import lax
from jax.experimental import pallas as pl
from jax.experimental.pallas import tpu as pltpu

F32 = jnp.float32
BF16 = jnp.bfloat16

NDEV = 8
DEPTH = 2
D = 1024
DFF = 2816
FC = 2 * DFF // NDEV
NCHUNK = DFF // FC
DR = DFF // NDEV
LW = 512
GW = 512
HD = 64
HEADS = 8
CHUNK = 128
PC = 2 * (LW + GW) // NDEV
OR = D // NDEV
NMOD = 9
AC = NMOD * D // NDEV
LC = 128
EPS = 1e-6
RG_LRU_C = 8.0
CONV_WIDTH = 4

ADAM_LR = 0.001
ADAM_B1 = 0.9
ADAM_B2 = 0.999
ADAM_EPS = 1e-08
ADAM_WD = 0.01
ADAM_STEP = 10

VMEM_LIMIT_BYTES = 60 * 1024 * 1024
MESH = pl.DeviceIdType.MESH
ANY = pl.BlockSpec(memory_space=pl.ANY)


def _cparams(*sem):
    return pltpu.CompilerParams(dimension_semantics=tuple(sem) if sem else None,
                                vmem_limit_bytes=VMEM_LIMIT_BYTES)


def _dot(a, b):
    return jnp.dot(a, b, preferred_element_type=F32)


def _dot_nt(a, b):
    return lax.dot_general(a, b, (((1,), (1,)), ((), ())), preferred_element_type=F32)


def _dot_tn(a, b):
    return lax.dot_general(a, b, (((0,), (0,)), ((), ())), preferred_element_type=F32)


def _split(a):
    hi = a.astype(BF16)
    lo = (a - hi.astype(F32)).astype(BF16)
    return hi, lo


def _dot3(a, b):
    ah, al = _split(a)
    bh, bl = _split(b)
    return _dot(ah, bh) + (_dot(ah, bl) + _dot(al, bh))


def _csum(a):
    return jnp.sum(a, axis=0, keepdims=True)


def _rmean(a):
    return jnp.mean(a, axis=-1, keepdims=True)


def _sigmoid(a):
    return 1.0 / (1.0 + jnp.exp(-a))


_GELU_K = 0.7978845608028654
_GELU_C = 0.044715


def _gelu(a):
    return 0.5 * a * (1.0 + jnp.tanh(_GELU_K * (a + _GELU_C * a * a * a)))


def _gelu_and_grad(a):
    a2 = a * a
    t = jnp.tanh(_GELU_K * (a + _GELU_C * a2 * a))
    half = 0.5 * (1.0 + t)
    return a * half, half + 0.5 * a * (1.0 - t * t) * (_GELU_K * (1.0 + 3.0 * _GELU_C * a2))


def _norm_mod(x, gain, scale, shift):
    rstd = lax.rsqrt(_rmean(x * x) + EPS)
    return (x * rstd * gain) * (1.0 + scale) + shift


def _norm_mod_bwd(dh, x, gain, scale):
    rstd = lax.rsqrt(_rmean(x * x) + EPS)
    xhat = x * rstd
    dshift = _csum(dh)
    dscale = _csum(dh * (xhat * gain))
    dhn = dh * (1.0 + scale)
    dgain = _csum(dhn * xhat)
    dxhat = dhn * gain
    dx = rstd * (dxhat - xhat * _rmean(dxhat * xhat))
    return dx, dshift, dscale, dgain


def _rms(x, gain):
    rstd = lax.rsqrt(_rmean(x * x) + EPS)
    return x * rstd * gain


def _rms_bwd(dy, x, gain):
    rstd = lax.rsqrt(_rmean(x * x) + EPS)
    xhat = x * rstd
    dgain = _csum(dy * xhat)
    dxhat = dy * gain
    return rstd * (dxhat - xhat * _rmean(dxhat * xhat)), dgain


PAIR = 2 * HD


def _seg_mean(a, pavg):
    hi, lo = _split(a)
    return jnp.concatenate([_dot(hi[:, p:p + PAIR], pavg) + _dot(lo[:, p:p + PAIR], pavg)
                            for p in range(0, a.shape[1], PAIR)], axis=1)


def _block_copies(src_hbm, dst_vmem, sems, rows):
    copies = []
    for k in range(NDEV):
        dst = dst_vmem.at[k] if rows is None else dst_vmem.at[pl.ds(k * rows, rows)]
        copies.append(pltpu.make_async_copy(src_hbm.at[k], dst, sems.at[k]))
    return copies


def _ffn_weight_fetch(wgu_hbm, wd_hbm, wgu_v, wd_v, sems):
    @pl.when(pl.program_id(0) == 0)
    def _():
        copies = _block_copies(wgu_hbm, wgu_v, sems.at[0], None) + _block_copies(wd_hbm, wd_v, sems.at[1], DR)
        for cp in copies:
            cp.start()
        for cp in copies:
            cp.wait()


def _place():
    return lax.axis_index("x"), lax.axis_index("y"), lax.axis_index("c")


def _slot(p):
    return 4 * p[0] + 2 * p[1] + p[2]


class GatherRide:
    def __init__(self, srcs):
        self.n = len(srcs)
        self.index = [i for _, i in srcs]
        self.args = [a for a, _ in srcs]
        self.out_shape = [jax.ShapeDtypeStruct((NDEV,) + (a.shape if i is None else a.shape[1:]), a.dtype)
                          for a, i in srcs]
        self.scratch = [pltpu.SemaphoreType.DMA((self.n, NDEV - 1)), pltpu.SemaphoreType.DMA((self.n, NDEV - 1)),
                        pltpu.SemaphoreType.DMA((self.n,))]

    def hooks(self, ins, outs, sems):
        send_sems, recv_sems, local_sems = sems
        n = self.n
        x, y, c = _place()
        me, sibling = (x, y, c), (x, y, 1 - c)
        chips = [(1 - x, y), (x, 1 - y), (1 - x, 1 - y)]

        def local(a):
            return ins[a] if self.index[a] is None else ins[a].at[self.index[a]]

        def copy(a, k, block, to, src=None):
            dst = outs[a].at[_slot(block)]
            return pltpu.make_async_remote_copy(
                src_ref=dst if src is None else src, dst_ref=dst,
                send_sem=send_sems.at[a, k], recv_sem=recv_sems.at[a, k],
                device_id=to, device_id_type=MESH)

        def mine():
            return [pltpu.make_async_copy(local(a), outs[a].at[_slot(me)], local_sems.at[a]) for a in range(n)]

        def first():
            cps = []
            for a in range(n):
                cps.append(copy(a, 0, me, sibling, src=local(a)))
                cps += [copy(a, 1 + j, me, (*chip, c), src=local(a)) for j, chip in enumerate(chips)]
            return cps

        def passed():
            return [copy(a, 4 + j, (*chip, c), sibling) for j, chip in enumerate(chips) for a in range(n)]

        def start():
            for cp in mine() + first():
                cp.start()

        def mid():
            for j, chip in enumerate(chips):
                for a in range(n):
                    copy(a, 1 + j, (*chip, c), me).wait_recv()
                    copy(a, 4 + j, (*chip, c), sibling).start()

        def finish():
            for a in range(n):
                copy(a, 0, sibling, me).wait_recv()
                for j, chip in enumerate(chips):
                    copy(a, 4 + j, (*chip, 1 - c), me).wait_recv()
            for cp in first() + passed():
                cp.wait_send()
            for cp in mine():
                cp.wait()

        return start, mid, finish


def all_gather(srcs, name):
    ride = GatherRide(srcs)
    n = ride.n

    def body(*refs):
        start, mid, finish = ride.hooks(refs[:n], refs[n:2 * n], refs[2 * n:])
        start()
        mid()
        finish()

    return pl.pallas_call(
        body, name=name,
        in_specs=[ANY] * n, out_specs=[ANY] * n, out_shape=ride.out_shape, scratch_shapes=ride.scratch,
    )(*ride.args)


def _call(core, ride, *, name, grid, in_specs, out_specs, out_shape, scratch_shapes, args):
    if ride is None:
        outs = pl.pallas_call(core, name=name, grid=grid, in_specs=in_specs, out_specs=out_specs,
                              out_shape=out_shape, scratch_shapes=scratch_shapes,
                              compiler_params=_cparams("arbitrary"))(*args)
        return outs, []
    n_in, n_out, n_sc, n = len(in_specs), len(out_shape), len(scratch_shapes), ride.n
    nsteps = grid[0]
    mid_step = max(nsteps - 2, 0)

    def body(*refs):
        cuts = [n_in, n_in + n, n_in + n + n_out, n_in + 2 * n + n_out, n_in + 2 * n + n_out + n_sc]
        ci, ri, co, ro, cs, rs = (refs[a:b] for a, b in zip([0] + cuts, cuts + [len(refs)]))
        start, mid, finish = ride.hooks(ri, ro, rs)
        i = pl.program_id(0)
        pl.when(i == 0)(start)
        core(*ci, *co, *cs)
        pl.when(i == mid_step)(mid)
        pl.when(i == nsteps - 1)(finish)

    outs = pl.pallas_call(
        body, name=name, grid=grid,
        in_specs=list(in_specs) + [ANY] * n, out_specs=list(out_specs) + [ANY] * n,
        out_shape=list(out_shape) + ride.out_shape, scratch_shapes=list(scratch_shapes) + ride.scratch,
        compiler_params=_cparams("arbitrary"))(*args, *ride.args)
    return outs[:n_out], outs[n_out:]


def all_reduce_rows(parts, name):
    _, R, C = parts.shape

    def body(in_ref, out_ref, land, acc, s1, r1, s2, r2):
        x, y, c = _place()
        me = (x, y, c)
        mine = _slot(me)

        def peer(k):
            return (1 - x if k & 4 else x, 1 - y if k & 2 else y, 1 - c if k & 1 else c)

        def scatter(k, to):
            src = in_ref.at[_slot(peer(k))] if to is None else land.at[_slot(peer(k))]
            dst = land.at[mine] if to is None else land.at[_slot(peer(k))]
            return pltpu.make_async_remote_copy(src_ref=src, dst_ref=dst, send_sem=s1.at[k - 1], recv_sem=r1.at[k - 1],
                                                device_id=peer(k) if to is None else me, device_id_type=MESH)

        def gather(k, to):
            dst = out_ref.at[mine] if to is None else out_ref.at[_slot(peer(k))]
            return pltpu.make_async_remote_copy(src_ref=acc if to is None else dst, dst_ref=dst,
                                                send_sem=s2.at[k - 1], recv_sem=r2.at[k - 1],
                                                device_id=peer(k) if to is None else me, device_id_type=MESH)

        for k in range(1, NDEV):
            scatter(k, None).start()
        land[mine] = in_ref[mine]
        for k in range(1, NDEV):
            scatter(k, me).wait_recv()
        total = land[0]
        for s in range(1, NDEV):
            total = total + land[s]
        acc[...] = total
        out_ref[mine] = total
        for k in range(1, NDEV):
            gather(k, None).start()
        for k in range(1, NDEV):
            gather(k, me).wait_recv()
        for k in range(1, NDEV):
            scatter(k, None).wait_send()
            gather(k, None).wait_send()

    vmem = pl.BlockSpec(memory_space=pltpu.VMEM)
    return pl.pallas_call(
        body, name=name, in_specs=[vmem], out_specs=vmem,
        out_shape=jax.ShapeDtypeStruct(parts.shape, F32),
        scratch_shapes=[pltpu.VMEM(parts.shape, F32), pltpu.VMEM((R, C), F32)]
        + [pltpu.SemaphoreType.DMA((NDEV - 1,))] * 4,
    )(parts)


def all_to_all(arrs, name):
    n = len(arrs)

    def body(*refs):
        ins, outs = refs[:n], refs[n:2 * n]
        send_sems, recv_sems, local_sems = refs[2 * n:]
        x, y, c = _place()
        me = (x, y, c)

        def peer(k):
            return (1 - x if k & 4 else x, 1 - y if k & 2 else y, 1 - c if k & 1 else c)

        def copy(a, k):
            return pltpu.make_async_remote_copy(
                src_ref=ins[a].at[_slot(peer(k))], dst_ref=outs[a].at[_slot(me)],
                send_sem=send_sems.at[a, k - 1], recv_sem=recv_sems.at[a, k - 1],
                device_id=peer(k), device_id_type=MESH)

        def landing(a, k):
            return pltpu.make_async_remote_copy(
                src_ref=outs[a].at[_slot(peer(k))], dst_ref=outs[a].at[_slot(peer(k))],
                send_sem=send_sems.at[a, k - 1], recv_sem=recv_sems.at[a, k - 1],
                device_id=me, device_id_type=MESH)

        mine = [pltpu.make_async_copy(ins[a].at[_slot(me)], outs[a].at[_slot(me)], local_sems.at[a]) for a in range(n)]
        for cp in mine:
            cp.start()
        sends = [copy(a, k) for a in range(n) for k in range(1, NDEV)]
        for cp in sends:
            cp.start()
        for a in range(n):
            for k in range(1, NDEV):
                landing(a, k).wait_recv()
        for cp in sends:
            cp.wait_send()
        for cp in mine:
            cp.wait()

    return pl.pallas_call(
        body, name=name,
        in_specs=[ANY] * n, out_specs=[ANY] * n,
        out_shape=[jax.ShapeDtypeStruct(a.shape, a.dtype) for a in arrs],
        scratch_shapes=[pltpu.SemaphoreType.DMA((n, NDEV - 1)), pltpu.SemaphoreType.DMA((n, NDEV - 1)),
                        pltpu.SemaphoreType.DMA((n,))],
    )(*arrs)


FFN_TS = 256
FFN_FWD_TS = 512


def ffn_fwd(x, vec, wgu_g, wdown_g, tag, ride=None):
    S = x.shape[0]
    ts = min(FFN_FWD_TS, S)

    def body(x_ref, vec_ref, wgu_hbm, wd_hbm, xo_ref, h_ref, gu_ref, f_ref, wgu_v, wd_v, sems):
        _ffn_weight_fetch(wgu_hbm, wd_hbm, wgu_v, wd_v, sems)
        xv = x_ref[...]
        h = _norm_mod(xv, vec_ref[3:4, :], vec_ref[1:2, :], vec_ref[0:1, :]).astype(BF16)
        h_ref[...] = h
        acc = jnp.zeros((ts, D), F32)
        for j in range(NCHUNK):
            g = _dot(h, wgu_v[j])
            u = _dot(h, wgu_v[NCHUNK + j])
            gu_ref[j] = g.astype(BF16)
            gu_ref[NCHUNK + j] = u.astype(BF16)
            a = (g * _sigmoid(g) * u).astype(BF16)
            acc = acc + _dot(a, wd_v[pl.ds(j * FC, FC), :])
        f_ref[...] = acc.astype(BF16)
        xo_ref[...] = xv + (0.5 * vec_ref[2:3, :]) * acc

    return _call(
        body, ride, name=f"ffn_fwd_{tag}",
        grid=(S // ts,),
        in_specs=[pl.BlockSpec((ts, D), lambda i: (i, 0)),
                  pl.BlockSpec((8, D), lambda i: (0, 0)), ANY, ANY],
        out_specs=[pl.BlockSpec((ts, D), lambda i: (i, 0)),
                   pl.BlockSpec((ts, D), lambda i: (i, 0)),
                   pl.BlockSpec((NDEV, ts, FC), lambda i: (0, i, 0)),
                   pl.BlockSpec((ts, D), lambda i: (i, 0))],
        out_shape=[jax.ShapeDtypeStruct((S, D), F32), jax.ShapeDtypeStruct((S, D), BF16),
                   jax.ShapeDtypeStruct((NDEV, S, FC), BF16), jax.ShapeDtypeStruct((S, D), BF16)],
        scratch_shapes=[pltpu.VMEM((NDEV, D, FC), BF16), pltpu.VMEM((DFF, D), BF16),
                        pltpu.SemaphoreType.DMA((2, NDEV))],
        args=(x, vec, wgu_g, wdown_g))


def ffn_bwd(dxo, x, gu, f, vec, wgu_g, wdown_g, tag):
    S = x.shape[0]
    ts = min(FFN_TS, S)

    def body(dxo_ref, x_ref, gu_ref, f_ref, vec_ref, wgu_hbm, wd_hbm,
             dx_ref, dgu_ref, a_ref, df_ref, acc_ref, wgu_v, wd_v, sems):
        _ffn_weight_fetch(wgu_hbm, wd_hbm, wgu_v, wd_v, sems)

        @pl.when(pl.program_id(0) == 0)
        def _():
            acc_ref[...] = jnp.zeros_like(acc_ref)

        dxo_v = dxo_ref[...]
        dgate = 0.5 * _csum(dxo_v * f_ref[...].astype(F32))
        df = ((0.5 * vec_ref[2:3, :]) * dxo_v).astype(BF16)
        df_ref[...] = df
        dh = jnp.zeros((ts, D), F32)
        for j in range(NCHUNK):
            da = _dot_nt(df, wd_v[pl.ds(j * FC, FC), :])
            g = gu_ref[j].astype(F32)
            u = gu_ref[NCHUNK + j].astype(F32)
            sg = _sigmoid(g)
            si = g * sg
            a_ref[j] = (si * u).astype(BF16)
            dg = (da * u * (sg * (1.0 + g * (1.0 - sg)))).astype(BF16)
            du = (da * si).astype(BF16)
            dgu_ref[j] = dg
            dgu_ref[NCHUNK + j] = du
            dh = dh + _dot_nt(dg, wgu_v[j]) + _dot_nt(du, wgu_v[NCHUNK + j])
        dx, dshift, dscale, dgain = _norm_mod_bwd(dh, x_ref[...], vec_ref[3:4, :], vec_ref[1:2, :])
        dx_ref[...] = dx + dxo_v
        acc_ref[0:1, :] += dshift
        acc_ref[1:2, :] += dscale
        acc_ref[2:3, :] += dgate
        acc_ref[3:4, :] += dgain

    row = pl.BlockSpec((ts, D), lambda i: (i, 0))
    return pl.pallas_call(
        body, name=f"ffn_bwd_{tag}",
        grid=(S // ts,),
        in_specs=[row, row, pl.BlockSpec((NDEV, ts, FC), lambda i: (0, i, 0)), row,
                  pl.BlockSpec((8, D), lambda i: (0, 0)), ANY, ANY],
        out_specs=[row, pl.BlockSpec((NDEV, ts, FC), lambda i: (0, i, 0)),
                   pl.BlockSpec((NCHUNK, ts, FC), lambda i: (0, i, 0)), row,
                   pl.BlockSpec((8, D), lambda i: (0, 0))],
        out_shape=[jax.ShapeDtypeStruct((S, D), F32), jax.ShapeDtypeStruct((NDEV, S, FC), BF16),
                   jax.ShapeDtypeStruct((NCHUNK, S, FC), BF16), jax.ShapeDtypeStruct((S, D), BF16),
                   jax.ShapeDtypeStruct((8, D), F32)],
        scratch_shapes=[pltpu.VMEM((NDEV, D, FC), BF16), pltpu.VMEM((DFF, D), BF16),
                        pltpu.SemaphoreType.DMA((2, NDEV))],
        compiler_params=_cparams("arbitrary"),
    )(dxo, x, gu, f, vec, wgu_g, wdown_g)


NCHIP = NDEV // 2


def tn_matmul_scatter(me_arr, a, b, slot, nslots, prev, name, split=1):
    na, S, M = a.shape
    nb, _, N = b.shape
    ncall = NDEV // split
    ts = min(4096, S)
    nsteps = S // ts
    mp = M // split
    other_step = {1: lambda j: 2 * j, 2: lambda j: j, 8: lambda j: 0}[split]
    mine_step = {1: lambda j: 2 * j + 1, 2: lambda j: j, 8: lambda j: 0}[split]

    def group(k, me_ref):
        if split == 1:
            return jnp.bitwise_xor(me_ref[0], NDEV - 1 - k)
        if split == 2:
            return jnp.bitwise_xor(me_ref[0] // 2, NCHIP - 1 - k)
        return 0

    def body(me_ref, *refs):
        a_ref, b_ref = refs[0], refs[1]
        recv_ref, acc, sb_other, sb_mine, land, d2d_send, d2d_recv, ici_send, ici_recv = refs[-9:]
        k = pl.program_id(0)
        s = pl.program_id(1)
        x, y, c = _place()
        my_chip = 2 * x + y

        def chip_of(j):
            if split == 8:
                cx, cy = j // 2, j % 2
            else:
                flip = NCHIP - 1 - j
                cx, cy = (1 - x if flip & 2 else x), (1 - y if flip & 1 else y)
            return cx, cy, 2 * cx + cy

        def piece(j, core):
            if split == 1:
                return acc[...]
            start = core * mp if split == 2 else (2 * j + core) * mp
            return acc[pl.ds(pl.multiple_of(start, 8), mp), :]

        def to_sibling(j):
            return pltpu.make_async_remote_copy(
                src_ref=sb_other.at[j], dst_ref=land.at[j], send_sem=d2d_send.at[j], recv_sem=d2d_recv.at[j],
                device_id=(x, y, 1 - c), device_id_type=MESH)

        def to_owner(j):
            cx, cy, ci = chip_of(j)
            dst = recv_ref.at[my_chip, slot]
            return ci, pltpu.make_async_copy(sb_mine.at[j], dst, ici_send.at[j]), pltpu.make_async_remote_copy(
                src_ref=sb_mine.at[j], dst_ref=dst, send_sem=ici_send.at[j], recv_sem=ici_recv.at[my_chip],
                device_id=(cx, cy, c), device_id_type=MESH)

        if nsteps == 1:
            acc[...] = _dot_tn(a_ref[...], b_ref[...])
        else:
            @pl.when(s == 0)
            def _():
                acc[...] = jnp.zeros_like(acc)

            acc[...] += _dot_tn(a_ref[...], b_ref[...])

        for kk in range(ncall):
            @pl.when((s == nsteps - 1) & (k == kk))
            def _():
                for j in range(NCHIP):
                    if other_step(j) == kk:
                        sb_other[j] = piece(j, 1 - c).astype(BF16)
                        to_sibling(j).start()
                for j in range(NCHIP):
                    if mine_step(j) == kk:
                        to_sibling(j).wait_recv()
                        sb_mine[j] = (piece(j, c) + land[j].astype(F32)).astype(BF16)
                        ci, loc, rem = to_owner(j)
                        pl.when(ci == my_chip)(loc.start)
                        pl.when(ci != my_chip)(rem.start)

        @pl.when((s == nsteps - 1) & (k == ncall - 1))
        def _():
            for j in range(NCHIP):
                to_sibling(j).wait_send()
                ci, loc, rem = to_owner(j)
                pl.when(ci == my_chip)(loc.wait)
                pl.when(ci != my_chip)(rem.wait_send)
            for src in range(NCHIP):
                @pl.when(my_chip != src)
                def _():
                    pltpu.make_async_remote_copy(
                        src_ref=recv_ref.at[src, slot], dst_ref=recv_ref.at[src, slot],
                        send_sem=ici_send.at[src], recv_sem=ici_recv.at[src],
                        device_id=(src // 2, src % 2, c), device_id_type=MESH).wait_recv()

    in_specs = [pl.BlockSpec((None, ts, M), (lambda k, s, me: (group(k, me), s, 0)) if na > 1 else (lambda k, s, me: (0, s, 0))),
                pl.BlockSpec((None, ts, N), (lambda k, s, me: (group(k, me), s, 0)) if nb > 1 else (lambda k, s, me: (0, s, 0)))]
    args = [me_arr, a, b]
    aliases = {}
    if prev is not None:
        in_specs.append(ANY)
        args.append(prev)
        aliases = {3: 0}
    return pl.pallas_call(
        body, name=name,
        grid_spec=pltpu.PrefetchScalarGridSpec(
            num_scalar_prefetch=1, grid=(ncall, nsteps), in_specs=in_specs, out_specs=ANY,
            scratch_shapes=[pltpu.VMEM((M, N), F32), pltpu.VMEM((NCHIP, mp, N), BF16), pltpu.VMEM((NCHIP, mp, N), BF16),
                            pltpu.VMEM((NCHIP, mp, N), BF16), pltpu.SemaphoreType.DMA((NCHIP,)),
                            pltpu.SemaphoreType.DMA((NCHIP,)), pltpu.SemaphoreType.DMA((NCHIP,)),
                            pltpu.SemaphoreType.DMA((NCHIP,))]),
        out_shape=jax.ShapeDtypeStruct((NCHIP, nslots, mp, N), BF16),
        input_output_aliases=aliases,
        compiler_params=_cparams("arbitrary", "arbitrary"),
    )(*args)


MIX_TS = 512
MIX_IN_TS = 512


def mix_in_fwd(x, vec, win_g, tag, ride=None):
    S = x.shape[0]
    ts = min(MIX_IN_TS, S)

    def body(x_ref, vec_ref, win_ref, hm_ref, proj_ref):
        h = _norm_mod(x_ref[...], vec_ref[3:4, :], vec_ref[1:2, :], vec_ref[0:1, :]).astype(BF16)
        hm_ref[...] = h
        for k in range(NDEV):
            proj_ref[k] = _dot(h, win_ref[k])

    return _call(
        body, ride, name=f"mix_in_fwd_{tag}",
        grid=(S // ts,),
        in_specs=[pl.BlockSpec((ts, D), lambda i: (i, 0)), pl.BlockSpec((8, D), lambda i: (0, 0)),
                  pl.BlockSpec((NDEV, D, PC), lambda i: (0, 0, 0))],
        out_specs=[pl.BlockSpec((ts, D), lambda i: (i, 0)),
                   pl.BlockSpec((NDEV, ts, PC), lambda i: (0, i, 0))],
        out_shape=[jax.ShapeDtypeStruct((S, D), BF16), jax.ShapeDtypeStruct((NDEV, S, PC), F32)],
        scratch_shapes=[], args=(x, vec, win_g))


def mix_in_bwd(dproj, x, dxo, vec, win_g, tag):
    S = x.shape[0]
    ts = min(MIX_IN_TS, S)

    def body(dp_ref, x_ref, dxo_ref, vec_ref, win_ref, dx_ref, acc_ref):
        @pl.when(pl.program_id(0) == 0)
        def _():
            acc_ref[...] = jnp.zeros_like(acc_ref)

        dh = jnp.zeros((ts, D), F32)
        for k in range(NDEV):
            dh = dh + _dot_nt(dp_ref[k], win_ref[k])
        dx, dshift, dscale, dgain = _norm_mod_bwd(dh, x_ref[...], vec_ref[3:4, :], vec_ref[1:2, :])
        dx_ref[...] = dx + dxo_ref[...]
        acc_ref[0:1, :] += dshift
        acc_ref[1:2, :] += dscale
        acc_ref[3:4, :] += dgain

    row = pl.BlockSpec((ts, D), lambda i: (i, 0))
    return pl.pallas_call(
        body, name=f"mix_in_bwd_{tag}",
        grid=(S // ts,),
        in_specs=[pl.BlockSpec((NDEV, ts, PC), lambda i: (0, i, 0)), row, row,
                  pl.BlockSpec((8, D), lambda i: (0, 0)),
                  pl.BlockSpec((NDEV, D, PC), lambda i: (0, 0, 0))],
        out_specs=[row, pl.BlockSpec((8, D), lambda i: (0, 0))],
        out_shape=[jax.ShapeDtypeStruct((S, D), F32), jax.ShapeDtypeStruct((8, D), F32)],
        compiler_params=_cparams("arbitrary"),
    )(dproj, x, dxo, vec, win_g)


SCAN_UNROLL = 8


def _shift_down(z, k, row):
    return jnp.where(row >= k, pltpu.roll(z, k, 0), 0.0)


def _shift_up(z, k, row, n):
    return jnp.where(row < n - k, pltpu.roll(z, n - k, 0), 0.0)


def _lru_gates(xc, lp_ref, wa_ref, wx_ref):
    xcb = xc.astype(BF16)
    ra = _sigmoid(_dot(xcb, wa_ref[...]) + lp_ref[5:6, :])
    ix = _sigmoid(_dot(xcb, wx_ref[...]) + lp_ref[6:7, :])
    lam = lp_ref[7:8, :]
    ls = jnp.minimum(lam, 0.0) - jnp.log(1.0 + jnp.exp(-jnp.abs(lam)))
    log_a = (RG_LRU_C * ls) * ra
    a = jnp.exp(log_a)
    mult = jnp.sqrt(-jnp.tanh(log_a) * (a * a + 1.0))
    return ra, ix, ls, a, mult


def _conv(x, lp_ref, row):
    return (lp_ref[4:5, :] + lp_ref[3:4, :] * x + lp_ref[2:3, :] * _shift_down(x, 1, row)
            + lp_ref[1:2, :] * _shift_down(x, 2, row) + lp_ref[0:1, :] * _shift_down(x, 3, row))


def lru_fwd(proj, lp, wa_t, wx_t, tag, ride=None):
    S = proj.shape[1]
    nblk = S // 8

    def body(x_ref, g_ref, lp_ref, wa_ref, wx_ref, y_ref, xc_ref, h_ref, a_s, b_s):
        x = x_ref[...]
        row = lax.broadcasted_iota(jnp.int32, x.shape, 0)
        xc = _conv(x, lp_ref, row)
        xc_ref[...] = xc
        ra, ix, ls, a, mult = _lru_gates(xc, lp_ref, wa_ref, wx_ref)
        a_s[...] = a
        b_s[...] = mult * (ix * xc)
        rowb = lax.broadcasted_iota(jnp.int32, (8, LC), 0)

        def step(i, carry):
            for q in range(SCAN_UNROLL):
                r0 = pl.multiple_of((i * SCAN_UNROLL + q) * 8, 8)
                A = a_s[pl.ds(r0, 8), :]
                B = b_s[pl.ds(r0, 8), :]
                for d in (1, 2, 4):
                    m = rowb >= d
                    As = jnp.where(m, pltpu.roll(A, d, 0), 1.0)
                    Bs = jnp.where(m, pltpu.roll(B, d, 0), 0.0)
                    B = A * Bs + B
                    A = A * As
                H = B + A * carry
                h_ref[pl.ds(r0, 8), :] = H
                carry = H[7:8, :]
            return carry

        lax.fori_loop(0, nblk // SCAN_UNROLL, step, jnp.zeros((1, LC), F32))
        y_ref[...] = h_ref[...] * _gelu(g_ref[...])

    col = pl.BlockSpec((S, LC), lambda c: (0, c))
    return _call(
        body, ride, name=f"lru_fwd_{tag}",
        grid=(LW // LC,),
        in_specs=[pl.BlockSpec((None, S, LC), lambda c: (c // 2, 0, c % 2)),
                  pl.BlockSpec((None, S, LC), lambda c: (2 + c // 2, 0, c % 2)),
                  pl.BlockSpec((8, LC), lambda c: (0, c)),
                  pl.BlockSpec((None, LC, LC), lambda c: (c, 0, 0)),
                  pl.BlockSpec((None, LC, LC), lambda c: (c, 0, 0))],
        out_specs=[col, col, col],
        out_shape=[jax.ShapeDtypeStruct((S, LW), F32)] * 3,
        scratch_shapes=[pltpu.VMEM((S, LC), F32), pltpu.VMEM((S, LC), F32)],
        args=(proj, proj, lp, wa_t, wx_t))


def lru_bwd(dy, proj, xc_all, hst, lp, wa_t, wx_t, tag):
    S = proj.shape[1]
    nblk = S // 8

    def body(dy_ref, x_ref, g_ref, xc_ref, h_ref, lp_ref, wa_ref, wx_ref,
             dx_ref, dg_ref, dlp_ref, dwa_ref, dwx_ref, c_s, l_s):
        xc = xc_ref[...]
        row = lax.broadcasted_iota(jnp.int32, xc.shape, 0)
        ra, ix, ls, a, mult = _lru_gates(xc, lp_ref, wa_ref, wx_ref)
        g = g_ref[...]
        dyv = dy_ref[...]
        h = h_ref[...]
        gelu_g, gelu_grad_g = _gelu_and_grad(g)
        dg_ref[...] = (dyv * h * gelu_grad_g).astype(BF16)
        c_s[...] = _shift_up(a, 1, row, S)
        l_s[...] = dyv * gelu_g
        rowb = lax.broadcasted_iota(jnp.int32, (8, LC), 0)

        def step(i, carry):
            for q in range(SCAN_UNROLL):
                r0 = pl.multiple_of((nblk - 1 - (i * SCAN_UNROLL + q)) * 8, 8)
                C = c_s[pl.ds(r0, 8), :]
                L = l_s[pl.ds(r0, 8), :]
                for d in (1, 2, 4):
                    m = rowb < 8 - d
                    Cs = jnp.where(m, pltpu.roll(C, 8 - d, 0), 1.0)
                    Ls = jnp.where(m, pltpu.roll(L, 8 - d, 0), 0.0)
                    L = C * Ls + L
                    C = C * Cs
                L = L + C * carry
                l_s[pl.ds(r0, 8), :] = L
                carry = L[0:1, :]
            return carry

        lax.fori_loop(0, nblk // SCAN_UNROLL, step, jnp.zeros((1, LC), F32))
        db = l_s[...]
        da = db * _shift_down(h, 1, row)
        ixc = ix * xc
        dmult = db * ixc
        dix = db * (mult * xc)
        dxc = db * (mult * ix)
        dlog_a = da * a - dmult * (a * a) / mult
        dra = dlog_a * (RG_LRU_C * ls)
        dls = _csum(dlog_a * ra) * RG_LRU_C
        lam = lp_ref[7:8, :]
        dlam = dls * _sigmoid(-lam)
        dpa = dra * ra * (1.0 - ra)
        dpx = dix * ix * (1.0 - ix)
        dpab = dpa.astype(BF16)
        dpxb = dpx.astype(BF16)
        xcb = xc.astype(BF16)
        dwa_ref[...] = _dot_tn(xcb, dpab)
        dwx_ref[...] = _dot_tn(xcb, dpxb)
        dxc = dxc + _dot_nt(dpab, wa_ref[...]) + _dot_nt(dpxb, wx_ref[...])
        x = x_ref[...]
        dlp_ref[0:1, :] = _csum(dxc * _shift_down(x, 3, row))
        dlp_ref[1:2, :] = _csum(dxc * _shift_down(x, 2, row))
        dlp_ref[2:3, :] = _csum(dxc * _shift_down(x, 1, row))
        dlp_ref[3:4, :] = _csum(dxc * x)
        dlp_ref[4:5, :] = _csum(dxc)
        dlp_ref[5:6, :] = _csum(dpa)
        dlp_ref[6:7, :] = _csum(dpx)
        dlp_ref[7:8, :] = dlam
        dx = (lp_ref[3:4, :] * dxc + lp_ref[2:3, :] * _shift_up(dxc, 1, row, S)
              + lp_ref[1:2, :] * _shift_up(dxc, 2, row, S) + lp_ref[0:1, :] * _shift_up(dxc, 3, row, S))
        dx_ref[...] = dx.astype(BF16)

    col = pl.BlockSpec((S, LC), lambda c: (0, c))
    pcol = pl.BlockSpec((None, S, LC), lambda c: (c // 2, 0, c % 2))
    return pl.pallas_call(
        body, name=f"lru_bwd_{tag}",
        grid=(LW // LC,),
        in_specs=[col, pcol, pl.BlockSpec((None, S, LC), lambda c: (2 + c // 2, 0, c % 2)), col, col,
                  pl.BlockSpec((8, LC), lambda c: (0, c)),
                  pl.BlockSpec((None, LC, LC), lambda c: (c, 0, 0)),
                  pl.BlockSpec((None, LC, LC), lambda c: (c, 0, 0))],
        out_specs=[pcol, pcol, pl.BlockSpec((8, LC), lambda c: (0, c)),
                   pl.BlockSpec((None, LC, LC), lambda c: (c, 0, 0)),
                   pl.BlockSpec((None, LC, LC), lambda c: (c, 0, 0))],
        out_shape=[jax.ShapeDtypeStruct((2, S, PC), BF16), jax.ShapeDtypeStruct((2, S, PC), BF16),
                   jax.ShapeDtypeStruct((8, LW), F32),
                   jax.ShapeDtypeStruct((LW // LC, LC, LC), F32), jax.ShapeDtypeStruct((LW // LC, LC, LC), F32)],
        scratch_shapes=[pltpu.VMEM((S, LC), F32), pltpu.VMEM((S, LC), F32)],
        compiler_params=_cparams("arbitrary"),
    )(dy, proj, proj, xc_all, hst, lp, wa_t, wx_t)


def _pair_stack(zp, low):
    return jnp.concatenate([jnp.where(low, zp, 0.0), jnp.where(low, 0.0, zp)], axis=0).astype(BF16)


def _spatial(w_ref, zc, low):
    return jnp.concatenate(
        [_dot(w_ref[:, 2 * p * CHUNK:2 * (p + 1) * CHUNK], _pair_stack(zc[:, p * PAIR:(p + 1) * PAIR], low))
         for p in range(GW // PAIR)], axis=1)


def _gmlp_fwd_parts(u, v, gp_ref, wcat_ref, bz_ref, pavg_ref, ts, with_grad=False):
    if with_grad:
        ug, ugrad = _gelu_and_grad(u)
        vg, vgrad = _gelu_and_grad(v)
    else:
        ug, vg, ugrad, vgrad = _gelu(u), _gelu(v), None, None
    pavg = pavg_ref[...]
    vc = vg - _seg_mean(vg, pavg)
    rs = lax.rsqrt(_seg_mean(vc * vc, pavg) + EPS)
    vhat = vc * rs
    vh = vhat * gp_ref[0:1, :]
    low = lax.broadcasted_iota(jnp.int32, (CHUNK, PAIR), 1) < HD
    zs = [_spatial(wcat_ref, vh[n * CHUNK:(n + 1) * CHUNK, :], low) + bz_ref[...] for n in range(ts // CHUNK)]
    z = jnp.concatenate(zs, axis=0) if len(zs) > 1 else zs[0]
    return ug, rs, vhat, vh, z, ugrad, vgrad


def mix_out_fwd(proj, ylru, x, vec, gp, wcat, bz, pavg, wout_g, tag, ride=None):
    S = x.shape[0]
    ts = min(MIX_TS, S)

    def body(u_ref, v_ref, yl_ref, x_ref, vec_ref, gp_ref, wcat_ref, bz_ref, pavg_ref, wout_ref,
             xo_ref, y_ref, fo_ref):
        u = jnp.concatenate([u_ref[0], u_ref[1]], axis=1)
        v = jnp.concatenate([v_ref[0], v_ref[1]], axis=1)
        ug, _, _, _, z, _, _ = _gmlp_fwd_parts(u, v, gp_ref, wcat_ref, bz_ref, pavg_ref, ts)
        n1 = _rms(yl_ref[...], gp_ref[1:2, :])
        n2 = _rms(ug * z, gp_ref[2:3, :])
        y = jnp.concatenate([n1, n2], axis=1).astype(BF16)
        y_ref[...] = y
        fo = _dot(y, wout_ref[...])
        fo_ref[...] = fo.astype(BF16)
        xo_ref[...] = x_ref[...] + vec_ref[2:3, :] * fo

    row = pl.BlockSpec((ts, D), lambda i: (i, 0))
    full = lambda shp: pl.BlockSpec(shp, lambda i: tuple(0 for _ in shp))
    return _call(
        body, ride, name=f"mix_out_fwd_{tag}",
        grid=(S // ts,),
        in_specs=[pl.BlockSpec((2, ts, PC), lambda i: (2, i, 0)), pl.BlockSpec((2, ts, PC), lambda i: (3, i, 0)),
                  pl.BlockSpec((ts, LW), lambda i: (i, 0)), row, full((8, D)), full((8, GW)),
                  full((CHUNK, HEADS * CHUNK)), full((CHUNK, GW)), full((PAIR, PAIR)), full((D, D))],
        out_specs=[row, row, row],
        out_shape=[jax.ShapeDtypeStruct((S, D), F32), jax.ShapeDtypeStruct((S, D), BF16),
                   jax.ShapeDtypeStruct((S, D), BF16)],
        scratch_shapes=[], args=(proj, proj, ylru, x, vec, gp, wcat, bz, pavg, wout_g))


def mix_out_bwd(dxo, proj, ylru, fo, vec, gp, wcat, wcat_t, bz, pavg, wout_t, tag):
    S = dxo.shape[0]
    ts = min(MIX_TS, S)

    def body(dxo_ref, u_ref, v_ref, yl_ref, fo_ref, vec_ref, gp_ref, wcat_ref, wcatt_ref, bz_ref, pavg_ref,
             wout_ref, dyo_ref, dyl_ref, duv_ref, acc_ref, dgp_ref, dwm_ref, dbz_ref):
        @pl.when(pl.program_id(0) == 0)
        def _():
            acc_ref[...] = jnp.zeros_like(acc_ref)
            dgp_ref[...] = jnp.zeros_like(dgp_ref)
            dwm_ref[...] = jnp.zeros_like(dwm_ref)
            dbz_ref[...] = jnp.zeros_like(dbz_ref)

        dxo_v = dxo_ref[...]
        acc_ref[2:3, :] += _csum(dxo_v * fo_ref[...].astype(F32))
        dyo = (vec_ref[2:3, :] * dxo_v).astype(BF16)
        dyo_ref[...] = dyo
        dn = _dot(dyo, wout_ref[...])
        dn1, dn2 = dn[:, :LW], dn[:, LW:]
        dyl, dg1 = _rms_bwd(dn1, yl_ref[...], gp_ref[1:2, :])
        dyl_ref[...] = dyl
        u = jnp.concatenate([u_ref[0], u_ref[1]], axis=1)
        v = jnp.concatenate([v_ref[0], v_ref[1]], axis=1)
        ug, rs, vhat, vh, z, ugrad, vgrad = _gmlp_fwd_parts(u, v, gp_ref, wcat_ref, bz_ref, pavg_ref, ts,
                                                            with_grad=True)
        dyg, dg2 = _rms_bwd(dn2, ug * z, gp_ref[2:3, :])
        du = (dyg * z) * ugrad
        dz = dyg * ug
        low = lax.broadcasted_iota(jnp.int32, (CHUNK, PAIR), 1) < HD
        vhb = vh.astype(BF16)
        dvhs = []
        dbz = jnp.zeros((CHUNK, GW), F32)
        dwm = [jnp.zeros((2 * CHUNK, CHUNK), F32) for _ in range(GW // PAIR)]
        for n in range(ts // CHUNK):
            dzc = dz[n * CHUNK:(n + 1) * CHUNK, :]
            dbz = dbz + dzc
            for p in range(GW // PAIR):
                stack = _pair_stack(dzc[:, p * PAIR:(p + 1) * PAIR], low)
                dwm[p] = dwm[p] + _dot_nt(stack, vhb[n * CHUNK:(n + 1) * CHUNK, p * PAIR:(p + 1) * PAIR])
            dvhs.append(_spatial(wcatt_ref, dzc, low))
        dbz_ref[...] += dbz
        for p in range(GW // PAIR):
            dwm_ref[2 * p * CHUNK:2 * (p + 1) * CHUNK, :] += dwm[p]
        dvh = jnp.concatenate(dvhs, axis=0) if len(dvhs) > 1 else dvhs[0]
        pavg = pavg_ref[...]
        dvn = _csum(dvh * vhat)
        dvhat = dvh * gp_ref[0:1, :]
        dvg = rs * (dvhat - _seg_mean(dvhat, pavg) - vhat * _seg_mean(dvhat * vhat, pavg))
        dv = dvg * vgrad
        duv_ref[0] = du[:, :PC].astype(BF16)
        duv_ref[1] = du[:, PC:].astype(BF16)
        duv_ref[2] = dv[:, :PC].astype(BF16)
        duv_ref[3] = dv[:, PC:].astype(BF16)
        dgp_ref[0:1, :] += dvn
        dgp_ref[1:2, :] += dg1
        dgp_ref[2:3, :] += dg2

    row = pl.BlockSpec((ts, D), lambda i: (i, 0))
    full = lambda shp: pl.BlockSpec(shp, lambda i: tuple(0 for _ in shp))
    return pl.pallas_call(
        body, name=f"mix_out_bwd_{tag}",
        grid=(S // ts,),
        in_specs=[row, pl.BlockSpec((2, ts, PC), lambda i: (2, i, 0)), pl.BlockSpec((2, ts, PC), lambda i: (3, i, 0)),
                  pl.BlockSpec((ts, LW), lambda i: (i, 0)), row, full((8, D)), full((8, GW)),
                  full((CHUNK, HEADS * CHUNK)), full((CHUNK, HEADS * CHUNK)), full((CHUNK, GW)), full((PAIR, PAIR)),
                  full((D, D))],
        out_specs=[row, pl.BlockSpec((ts, LW), lambda i: (i, 0)), pl.BlockSpec((4, ts, PC), lambda i: (0, i, 0)),
                   full((8, D)), full((8, GW)), full((HEADS * CHUNK, CHUNK)), full((CHUNK, GW))],
        out_shape=[jax.ShapeDtypeStruct((S, D), BF16), jax.ShapeDtypeStruct((S, LW), F32),
                   jax.ShapeDtypeStruct((4, S, PC), BF16), jax.ShapeDtypeStruct((8, D), F32),
                   jax.ShapeDtypeStruct((8, GW), F32), jax.ShapeDtypeStruct((HEADS * CHUNK, CHUNK), F32),
                   jax.ShapeDtypeStruct((CHUNK, GW), F32)],
        compiler_params=_cparams("arbitrary"),
    )(dxo, proj, proj, ylru, fo, vec, gp, wcat, wcat_t, bz, pavg, wout_t)


def final_loss(x, target, gain):
    S = x.shape[0]
    ts = min(512, S)

    def body(x_ref, t_ref, g_ref, loss_ref, dx_ref, dg_ref):
        @pl.when(pl.program_id(0) == 0)
        def _():
            loss_ref[...] = jnp.zeros_like(loss_ref)
            dg_ref[...] = jnp.zeros_like(dg_ref)

        xv = x_ref[...]
        gain_v = g_ref[0:1, :]
        rstd = lax.rsqrt(_rmean(xv * xv) + EPS)
        xhat = xv * rstd
        err = xhat * gain_v - t_ref[...]
        loss_ref[...] += 0.5 * _csum(_rmean(err * err))
        dy = err * (1.0 / D)
        dg_ref[0:1, :] += _csum(dy * xhat)
        dxhat = dy * gain_v
        dx_ref[...] = rstd * (dxhat - xhat * _rmean(dxhat * xhat))

    row = pl.BlockSpec((ts, D), lambda i: (i, 0))
    return pl.pallas_call(
        body, name="final_loss",
        grid=(S // ts,),
        in_specs=[row, row, pl.BlockSpec((8, D), lambda i: (0, 0))],
        out_specs=[pl.BlockSpec((8, 128), lambda i: (0, 0)), row, pl.BlockSpec((8, D), lambda i: (0, 0))],
        out_shape=[jax.ShapeDtypeStruct((8, 128), F32), jax.ShapeDtypeStruct((S, D), F32),
                   jax.ShapeDtypeStruct((8, D), F32)],
        compiler_params=_cparams("arbitrary"),
    )(x, target, gain)


def _vec(mod_l, j, gain):
    return jnp.concatenate([mod_l[3 * j:3 * j + 3], gain[None, :], jnp.zeros((4, D), F32)], axis=0)


def _block_diag_tiles(w):
    w4 = w.reshape(LW // LC, 2, HD, HD)
    eye2 = jnp.eye(2, dtype=w.dtype)
    return (w4[:, :, :, None, :] * eye2[None, :, None, :, None]).reshape(LW // LC, LC, LC).astype(BF16)


def _block_diag_extract(dw):
    d5 = dw.reshape(LW // LC, 2, HD, 2, HD)
    return jnp.einsum('cihkj,ik->cihj', d5, jnp.eye(2, dtype=dw.dtype)).reshape(HEADS, HD, HD)


def _layer_params(l, p, conv_w_full):
    lp = jnp.concatenate([conv_w_full[l], p['conv_b'][l][None], p['gate_a_b'][l].reshape(1, LW),
                          p['gate_x_b'][l].reshape(1, LW), p['lru_lambda'][l][None]], axis=0)
    gp = jnp.concatenate([p['v_norm'][l][None], p['lru_out_norm'][l][None], p['gmlp_out_norm'][l][None],
                          jnp.zeros((5, GW), F32)], axis=0)
    ws = p['spatial_w'][l] * jnp.tril(jnp.ones((CHUNK, CHUNK), F32))
    wcat = ws.transpose(1, 0, 2).reshape(CHUNK, HEADS * CHUNK).astype(BF16)
    wcat_t = ws.transpose(2, 0, 1).reshape(CHUNK, HEADS * CHUNK).astype(BF16)
    bz = jnp.repeat(p['spatial_b'][l].T, HD, axis=1)
    return dict(lp=lp, gp=gp, wcat=wcat, wcat_t=wcat_t, bz=bz,
                wa_t=_block_diag_tiles(p['gate_a_w'][l]), wx_t=_block_diag_tiles(p['gate_x_w'][l]))


def _pavg():
    return jnp.kron(jnp.eye(2, dtype=F32), jnp.full((HD, HD), 1.0 / HD, F32)).astype(BF16)


GATHER_RIDES = {
    ('ffn_a', 0): [('w_in', 0), ('gu', DEPTH)],
    ('mix_in', 0): [('w_out', 0)],
    ('lru', 0): [('down', DEPTH)],
    ('mix_out', 0): [('down', 1)],
    ('ffn_b', 0): [('gu', 1), ('w_in', 1)],
    ('ffn_a', 1): [('gu', DEPTH + 1), ('w_out', 1)],
    ('mix_in', 1): [('down', DEPTH + 1)],
}


def local_fwd_bwd(me_arr, x, target, mod, p, loc, gathered, conv_w_full):
    pavg = _pavg()
    g = dict(gathered)

    def ride(call, l):
        todo = GATHER_RIDES.get((call, l))
        return None if todo is None else (todo, GatherRide([(loc[kind], slot) for kind, slot in todo]))

    def run(fn, call, l, *args):
        r = ride(call, l)
        outs, got = fn(*args, ride=None if r is None else r[1])
        if r is not None:
            g.update(dict(zip(r[0], got)))
        return outs

    saved = []
    h = x
    for l in range(DEPTH):
        q = _layer_params(l, p, conv_w_full)
        v1 = _vec(mod[l], 0, p['ffn1_norm'][l])
        vm = _vec(mod[l], 1, p['mix_norm'][l])
        v2 = _vec(mod[l], 2, p['ffn2_norm'][l])
        x0 = h
        x1, h1, gu1, f1 = run(ffn_fwd, 'ffn_a', l, x0, v1, g['gu', l], g['down', l], f"a{l}")
        hm, proj = run(mix_in_fwd, 'mix_in', l, x1, vm, g['w_in', l], f"{l}")
        ylru, xc, hst = run(lru_fwd, 'lru', l, proj, q['lp'], q['wa_t'], q['wx_t'], f"{l}")
        x2, y, fo = run(mix_out_fwd, 'mix_out', l, proj, ylru, x1, vm, q['gp'], q['wcat'], q['bz'], pavg,
                        g['w_out', l].reshape(D, D), f"{l}")
        x3, h2, gu2, f2 = run(ffn_fwd, 'ffn_b', l, x2, v2, g['gu', DEPTH + l], g['down', DEPTH + l], f"b{l}")
        saved.append(dict(q=q, v1=v1, vm=vm, v2=v2, x0=x0, x1=x1, x2=x2, h1=h1, gu1=gu1, f1=f1, hm=hm, proj=proj,
                          ylru=ylru, xc=xc, hst=hst, y=y, fo=fo, h2=h2, gu2=gu2, f2=f2))
        h = x3
    fin = jnp.concatenate([p['final_norm'][None], jnp.zeros((7, D), F32)], axis=0)
    loss8, dx, dfin = final_loss(h, target, fin)
    loss = loss8[0, 0]

    big = dict(gu=None, down=None, w_in=None, w_out=None)
    small = {k: [None] * DEPTH for k in ('ffn1_norm', 'mix_norm', 'ffn2_norm', 'conv_w', 'conv_b', 'gate_a_w',
                                         'gate_a_b', 'gate_x_w', 'gate_x_b', 'lru_lambda', 'v_norm', 'spatial_w',
                                         'spatial_b', 'lru_out_norm', 'gmlp_out_norm')}
    dmod = [None] * DEPTH
    tril = jnp.tril(jnp.ones((CHUNK, CHUNK), F32))
    for l in reversed(range(DEPTH)):
        sv = saved[l]
        q = sv['q']
        dx2, dgu, a, df, acc2 = ffn_bwd(dx, sv['x2'], sv['gu2'], sv['f2'], sv['v2'],
                                        g['gu', DEPTH + l], g['down', DEPTH + l], f"b{l}")
        big['gu'] = tn_matmul_scatter(me_arr, dgu, sv['h2'][None], DEPTH + l, 2 * DEPTH, big['gu'], f"dw_gu_b{l}")
        big['down'] = tn_matmul_scatter(me_arr, a, df[None], DEPTH + l, 2 * DEPTH, big['down'], f"dw_down_b{l}", split=2)
        dyo, dylru, duv, accmo, dgp, dwm, dbz = mix_out_bwd(dx2, sv['proj'], sv['ylru'], sv['fo'], sv['vm'], q['gp'],
                                                             q['wcat'], q['wcat_t'], q['bz'], pavg,
                                                             g['w_out', l].reshape(D, D).T, f"{l}")
        big['w_out'] = tn_matmul_scatter(me_arr, sv['y'][None], dyo[None], l, DEPTH, big['w_out'], f"dw_out_{l}",
                                         split=NDEV)
        dxl, dgl, dlp, dwa, dwx = lru_bwd(dylru, sv['proj'], sv['xc'], sv['hst'], q['lp'], q['wa_t'], q['wx_t'], f"{l}")
        dproj = jnp.concatenate([dxl, dgl, duv], axis=0)
        dx1, accmi = mix_in_bwd(dproj, sv['x1'], dx2, sv['vm'], g['w_in', l], f"{l}")
        big['w_in'] = tn_matmul_scatter(me_arr, sv['hm'][None], dproj, l, DEPTH, big['w_in'], f"dw_in_{l}")
        dx0, dgu, a, df, acc1 = ffn_bwd(dx1, sv['x0'], sv['gu1'], sv['f1'], sv['v1'],
                                        g['gu', l], g['down', l], f"a{l}")
        big['gu'] = tn_matmul_scatter(me_arr, dgu, sv['h1'][None], l, 2 * DEPTH, big['gu'], f"dw_gu_a{l}")
        big['down'] = tn_matmul_scatter(me_arr, a, df[None], l, 2 * DEPTH, big['down'], f"dw_down_a{l}", split=2)
        dx = dx0
        dmod[l] = jnp.concatenate([acc1[0:3], accmi[0:2], accmo[2:3], acc2[0:3]], axis=0)
        small['ffn1_norm'][l] = acc1[3]
        small['mix_norm'][l] = accmi[3]
        small['ffn2_norm'][l] = acc2[3]
        small['conv_w'][l] = dlp[0:4]
        small['conv_b'][l] = dlp[4]
        small['gate_a_b'][l] = dlp[5].reshape(HEADS, HD)
        small['gate_x_b'][l] = dlp[6].reshape(HEADS, HD)
        small['lru_lambda'][l] = dlp[7]
        small['gate_a_w'][l] = _block_diag_extract(dwa)
        small['gate_x_w'][l] = _block_diag_extract(dwx)
        small['v_norm'][l] = dgp[0]
        small['lru_out_norm'][l] = dgp[1]
        small['gmlp_out_norm'][l] = dgp[2]
        small['spatial_w'][l] = dwm.reshape(HEADS, CHUNK, CHUNK) * tril
        small['spatial_b'][l] = dbz.reshape(CHUNK, HEADS, HD).sum(-1).T
    small = {k: jnp.stack(v) for k, v in small.items()}
    small['final_norm'] = dfin[0]
    return loss, dx, big, small, jnp.stack(dmod)


def ada_fwd(c_all, w_ada, b_loc):
    def body(c_ref, w_ref, b_ref, mod_ref, sc_ref):
        cv = c_ref[...]
        sc = cv * _sigmoid(cv)
        sc_ref[...] = sc
        mod_ref[...] = _dot3(sc, w_ref[...]) + b_ref[...]

    return pl.pallas_call(
        body, name="ada_fwd",
        grid=(DEPTH,),
        in_specs=[pl.BlockSpec((NDEV, D), lambda l: (0, 0)), pl.BlockSpec((None, D, AC), lambda l: (l, 0, 0)),
                  pl.BlockSpec((None, 1, AC), lambda l: (l, 0, 0))],
        out_specs=[pl.BlockSpec((None, NDEV, AC), lambda l: (l, 0, 0)), pl.BlockSpec((NDEV, D), lambda l: (0, 0))],
        out_shape=[jax.ShapeDtypeStruct((DEPTH, NDEV, AC), F32), jax.ShapeDtypeStruct((NDEV, D), F32)],
        compiler_params=_cparams("arbitrary"),
    )(c_all, w_ada, b_loc)


def ada_bwd(sc_t, dmod_cols):
    def body(sc_ref, dm_ref, g_ref):
        sc = sc_ref[...]
        dm = dm_ref[...]
        acc = sc[:, 0:1] * dm[0:1, :]
        for b in range(1, NDEV):
            acc = acc + sc[:, b:b + 1] * dm[b:b + 1, :]
        g_ref[...] = acc

    return pl.pallas_call(
        body, name="ada_bwd",
        grid=(DEPTH,),
        in_specs=[pl.BlockSpec((D, NDEV), lambda l: (0, 0)), pl.BlockSpec((None, NDEV, AC), lambda l: (l, 0, 0))],
        out_specs=pl.BlockSpec((None, None, D, AC), lambda l: (0, l, 0, 0)),
        out_shape=jax.ShapeDtypeStruct((1, DEPTH, D, AC), F32),
        compiler_params=_cparams("arbitrary"),
    )(sc_t, dmod_cols)


def _row_tile(rows, cols):
    if rows * cols <= 512 * 1024:
        return rows
    for tr in (512, 384, 352, 256, 128, 64, 32, 16, 8):
        if rows % tr == 0:
            return tr
    return rows


def adamw(gparts, slot0, w, m, v, name):
    P, _, R, C = gparts.shape
    L = w.shape[0]
    tr = _row_tile(R, C)

    def body(g_ref, w_ref, m_ref, v_ref, go_ref, do_ref, mo_ref, vo_ref):
        g = g_ref[0].astype(F32)
        for p in range(1, P):
            g = g + g_ref[p].astype(F32)
        go_ref[...] = g
        mn = ADAM_B1 * m_ref[...] + (1.0 - ADAM_B1) * g
        vn = ADAM_B2 * v_ref[...] + (1.0 - ADAM_B2) * (g * g)
        mo_ref[...] = mn
        vo_ref[...] = vn
        m_hat = mn / (1.0 - ADAM_B1 ** ADAM_STEP)
        v_hat = vn / (1.0 - ADAM_B2 ** ADAM_STEP)
        do_ref[...] = -ADAM_LR * (m_hat / (jnp.sqrt(v_hat) + ADAM_EPS) + ADAM_WD * w_ref[...])

    blk = pl.BlockSpec((None, tr, C), lambda l, i: (l, i, 0))
    return pl.pallas_call(
        body, name=name,
        grid=(L, R // tr),
        in_specs=[pl.BlockSpec((P, None, tr, C), lambda l, i: (0, slot0 + l, i, 0)), blk, blk, blk],
        out_specs=[blk, blk, blk, blk],
        out_shape=[jax.ShapeDtypeStruct((L, R, C), F32)] * 4,
        compiler_params=_cparams("arbitrary", "arbitrary"),
    )(gparts, w, m, v)


WEIGHTS = ['w_ada', 'b_ada', 'ffn1_norm', 'ffn1_w_gu', 'ffn1_w_down', 'mix_norm', 'w_in', 'conv_w', 'conv_b',
           'gate_a_w', 'gate_a_b', 'gate_x_w', 'gate_x_b', 'lru_lambda', 'v_norm', 'spatial_w', 'spatial_b',
           'lru_out_norm', 'gmlp_out_norm', 'w_out', 'ffn2_norm', 'ffn2_w_gu', 'ffn2_w_down', 'final_norm']
PACKED = ['b_ada', 'ffn1_norm', 'mix_norm', 'conv_b', 'gate_a_w', 'gate_a_b', 'gate_x_w', 'gate_x_b', 'lru_lambda',
          'v_norm', 'spatial_w', 'spatial_b', 'lru_out_norm', 'gmlp_out_norm', 'ffn2_norm', 'final_norm', 'conv_w']
PACK_LANES = 128
PACK_ROW_ALIGN = 8 * NDEV


PACK_TAIL = 8


def _pack_rows(shapes):
    used = 0
    for k in PACKED:
        size = 1
        for s in shapes[k]:
            size *= s
        used += size // PACK_LANES
    return used, -(-(used + PACK_TAIL) // PACK_ROW_ALIGN) * PACK_ROW_ALIGN


def _pack(d, tail=None):
    parts = [d[k].reshape(-1, PACK_LANES).astype(F32) for k in PACKED]
    used, rows = _pack_rows({k: d[k].shape for k in PACKED})
    parts.append(jnp.zeros((PACK_TAIL, PACK_LANES), F32) if tail is None else tail)
    return jnp.concatenate(parts + [jnp.zeros((rows - used - PACK_TAIL, PACK_LANES), F32)], axis=0)


def _unpack(buf, shapes):
    out, off = {}, 0
    for k in PACKED:
        size = 1
        for s in shapes[k]:
            size *= s
        nrows = size // PACK_LANES
        out[k] = buf[off:off + nrows].reshape(shapes[k])
        off += nrows
    return out


def kernel(x, c, w_ada, b_ada, ffn1_norm, ffn1_w_gu, ffn1_w_down, mix_norm, w_in, conv_w, conv_b, gate_a_w, gate_a_b, gate_x_w, gate_x_b, lru_lambda, v_norm, spatial_w, spatial_b, lru_out_norm, gmlp_out_norm, w_out, ffn2_norm, ffn2_w_gu, ffn2_w_down, final_norm, loss_target, m_w_ada, m_b_ada, m_ffn1_norm, m_ffn1_w_gu, m_ffn1_w_down, m_mix_norm, m_w_in, m_conv_w, m_conv_b, m_gate_a_w, m_gate_a_b, m_gate_x_w, m_gate_x_b, m_lru_lambda, m_v_norm, m_spatial_w, m_spatial_b, m_lru_out_norm, m_gmlp_out_norm, m_w_out, m_ffn2_norm, m_ffn2_w_gu, m_ffn2_w_down, m_final_norm, v_w_ada, v_b_ada, v_ffn1_norm, v_ffn1_w_gu, v_ffn1_w_down, v_mix_norm, v_w_in, v_conv_w, v_conv_b, v_gate_a_w, v_gate_a_b, v_gate_x_w, v_gate_x_b, v_lru_lambda, v_v_norm, v_spatial_w, v_spatial_b, v_lru_out_norm, v_gmlp_out_norm, v_w_out, v_ffn2_norm, v_ffn2_w_gu, v_ffn2_w_down, v_final_norm):
    w = dict(w_ada=w_ada, b_ada=b_ada, ffn1_norm=ffn1_norm, ffn1_w_gu=ffn1_w_gu, ffn1_w_down=ffn1_w_down, mix_norm=mix_norm, w_in=w_in, conv_w=conv_w, conv_b=conv_b, gate_a_w=gate_a_w, gate_a_b=gate_a_b, gate_x_w=gate_x_w, gate_x_b=gate_x_b, lru_lambda=lru_lambda, v_norm=v_norm, spatial_w=spatial_w, spatial_b=spatial_b, lru_out_norm=lru_out_norm, gmlp_out_norm=gmlp_out_norm, w_out=w_out, ffn2_norm=ffn2_norm, ffn2_w_gu=ffn2_w_gu, ffn2_w_down=ffn2_w_down, final_norm=final_norm)
    m = dict(w_ada=m_w_ada, b_ada=m_b_ada, ffn1_norm=m_ffn1_norm, ffn1_w_gu=m_ffn1_w_gu, ffn1_w_down=m_ffn1_w_down, mix_norm=m_mix_norm, w_in=m_w_in, conv_w=m_conv_w, conv_b=m_conv_b, gate_a_w=m_gate_a_w, gate_a_b=m_gate_a_b, gate_x_w=m_gate_x_w, gate_x_b=m_gate_x_b, lru_lambda=m_lru_lambda, v_norm=m_v_norm, spatial_w=m_spatial_w, spatial_b=m_spatial_b, lru_out_norm=m_lru_out_norm, gmlp_out_norm=m_gmlp_out_norm, w_out=m_w_out, ffn2_norm=m_ffn2_norm, ffn2_w_gu=m_ffn2_w_gu, ffn2_w_down=m_ffn2_w_down, final_norm=m_final_norm)
    v = dict(w_ada=v_w_ada, b_ada=v_b_ada, ffn1_norm=v_ffn1_norm, ffn1_w_gu=v_ffn1_w_gu, ffn1_w_down=v_ffn1_w_down, mix_norm=v_mix_norm, w_in=v_w_in, conv_w=v_conv_w, conv_b=v_conv_b, gate_a_w=v_gate_a_w, gate_a_b=v_gate_a_b, gate_x_w=v_gate_x_w, gate_x_b=v_gate_x_b, lru_lambda=v_lru_lambda, v_norm=v_v_norm, spatial_w=v_spatial_w, spatial_b=v_spatial_b, lru_out_norm=v_lru_out_norm, gmlp_out_norm=v_gmlp_out_norm, w_out=v_w_out, ffn2_norm=v_ffn2_norm, ffn2_w_gu=v_ffn2_w_gu, ffn2_w_down=v_ffn2_w_down, final_norm=v_final_norm)
    me = 4 * lax.axis_index("x") + 2 * lax.axis_index("y") + lax.axis_index("c")

    loc = dict(gu=jnp.concatenate([ffn1_w_gu, ffn2_w_gu], axis=0).astype(BF16),
               down=jnp.concatenate([ffn1_w_down, ffn2_w_down], axis=0).astype(BF16),
               w_in=w_in.astype(BF16), w_out=w_out.astype(BF16))
    c_g, conv_g, gu0, down0 = all_gather([(c, None), (conv_w, None), (loc['gu'], 0), (loc['down'], 0)], "gather_first")
    conv_w_full = conv_g.transpose(1, 2, 0, 3).reshape(DEPTH, CONV_WIDTH, LW)

    b_loc = lax.dynamic_slice(b_ada, (0, me * AC), (DEPTH, AC)).reshape(DEPTH, 1, AC)
    mod_cols, sc_all = ada_fwd(c_g.reshape(NDEV, D), w_ada, b_loc)
    (mod_rows,) = all_to_all([mod_cols.transpose(1, 0, 2)], "scatter_mod")
    mod = mod_rows.transpose(1, 0, 2).reshape(DEPTH, NMOD, D)

    small_w = {k: w[k] for k in PACKED if k != 'conv_w'}
    me_arr = jnp.reshape(me, (1,)).astype(jnp.int32)
    loss_loc, dx, big, small_g, dmod = local_fwd_bwd(me_arr, x[0], loss_target[0], mod, small_w, loc,
                                                     {('gu', 0): gu0, ('down', 0): down0}, conv_w_full)

    small_g['b_ada'] = dmod.reshape(DEPTH, NMOD * D)
    first = (lax.broadcasted_iota(jnp.int32, (PACK_TAIL, PACK_LANES), 0)
             + lax.broadcasted_iota(jnp.int32, (PACK_TAIL, PACK_LANES), 1)) == 0
    gpack = _pack(small_g, jnp.where(first, loss_loc, 0.0))
    rows = gpack.shape[0]
    dmod_out = dmod.reshape(DEPTH, NDEV, AC).transpose(1, 0, 2)
    (dmod_r,) = all_to_all([dmod_out], "scatter_dmod")
    gsum_g = all_reduce_rows(gpack.reshape(NDEV, rows // NDEV, PACK_LANES), "allreduce_small_grads")
    gsum = gsum_g.reshape(1, 1, rows, PACK_LANES)
    loss = gsum[0, 0, _pack_rows({k: small_g[k].shape for k in PACKED})[0], 0]

    res = {}
    t = lambda a: a.transpose(0, 2, 1)
    gu_t = big['gu']
    res['ffn1_w_gu'] = tuple(t(r) for r in adamw(gu_t, 0, t(w['ffn1_w_gu']), t(m['ffn1_w_gu']), t(v['ffn1_w_gu']),
                                                 "adamw_gu_a"))
    res['ffn2_w_gu'] = tuple(t(r) for r in adamw(gu_t, DEPTH, t(w['ffn2_w_gu']), t(m['ffn2_w_gu']),
                                                 t(v['ffn2_w_gu']), "adamw_gu_b"))
    res['ffn1_w_down'] = adamw(big['down'], 0, w['ffn1_w_down'], m['ffn1_w_down'], v['ffn1_w_down'], "adamw_down_a")
    res['ffn2_w_down'] = adamw(big['down'], DEPTH, w['ffn2_w_down'], m['ffn2_w_down'], v['ffn2_w_down'], "adamw_down_b")
    res['w_in'] = adamw(big['w_in'], 0, w['w_in'], m['w_in'], v['w_in'], "adamw_w_in")
    res['w_out'] = adamw(big['w_out'], 0, w['w_out'], m['w_out'], v['w_out'], "adamw_w_out")
    g_ada = ada_bwd(sc_all.T, dmod_r.transpose(1, 0, 2))
    res['w_ada'] = adamw(g_ada, 0, w['w_ada'], m['w_ada'], v['w_ada'], "adamw_w_ada")
    shapes = {k: w[k].shape for k in PACKED}
    shapes['conv_w'] = (DEPTH, CONV_WIDTH, LW)
    dummy = jnp.zeros(shapes['conv_w'], F32)
    packs = adamw(gsum, 0, _pack({**small_w, 'conv_w': dummy})[None], _pack({**{k: m[k] for k in small_w}, 'conv_w': dummy})[None],
                  _pack({**{k: v[k] for k in small_w}, 'conv_w': dummy})[None], "adamw_small")
    unpacked = [_unpack(b[0], shapes) for b in packs]
    for k in small_w:
        res[k] = tuple(u[k] for u in unpacked)
    gconv = lax.dynamic_slice(unpacked[0]['conv_w'], (0, 0, me * (LW // NDEV)), (DEPTH, CONV_WIDTH, LW // NDEV))
    cshape = (1, DEPTH * CONV_WIDTH, LW // NDEV)
    rc = adamw(gconv.reshape((1,) + cshape), 0, conv_w.reshape(cshape), m['conv_w'].reshape(cshape),
               v['conv_w'].reshape(cshape), "adamw_conv_w")
    res['conv_w'] = tuple(r.reshape(conv_w.shape) for r in rc)

    return (loss, dx[None], *[res[k][0] for k in WEIGHTS], *[res[k][1] for k in WEIGHTS],
            *[res[k][2] for k in WEIGHTS], *[res[k][3] for k in WEIGHTS])
```
